```python
import jax
import jax.numpy as jnp
from jax import lax
import numpy as np

D_MODEL = 1024
BATCH = 16
SEQ = 4096
DEPTH = 1

CHUNK = 64
N_META = 16
Q_BLOCK = 128
FOX_HEADS = 8
FOX_HEAD_DIM = 64
FOX_WIDTH = FOX_HEADS * FOX_HEAD_DIM
HG_HEADS = 4
HG_KEY_DIM = 128
HG_VAL_DIM = 128
HG_KWIDTH = HG_HEADS * HG_KEY_DIM
HG_VWIDTH = HG_HEADS * HG_VAL_DIM
N_BRANCH = 2
D_FF = 2816
CONV_WIDTH = 3
EPS = 1e-6
SPLIT_SIZES = (FOX_WIDTH, FOX_WIDTH, FOX_WIDTH, FOX_HEADS,
               HG_KWIDTH, HG_KWIDTH, HG_VWIDTH, HG_VWIDTH,
               D_MODEL, D_MODEL)
IN_COLS = sum(SPLIT_SIZES)
SPLIT_POINTS = tuple(int(v) for v in np.cumsum(SPLIT_SIZES)[:-1])

kernel_name = 'hybrid_hgrn2_fox_gated_block'


def _rms(x, gain):
    xf = x.astype(jnp.float32)
    y = xf * lax.rsqrt(jnp.mean(xf * xf, axis=-1, keepdims=True) + EPS)
    return (y * gain.astype(jnp.float32)).astype(x.dtype)


def _fox_attention(q, k, v, f_logit):
    b, nh, l, dh = q.shape
    n_blk = -(-l // Q_BLOCK)
    lp = n_blk * Q_BLOCK
    pad4 = ((0, 0), (0, 0), (0, lp - l), (0, 0))
    qp = jnp.pad(q, pad4)
    kp = jnp.pad(k, pad4)
    vp = jnp.pad(v, pad4)
    logf = jax.nn.log_sigmoid(f_logit.astype(jnp.float32))
    cum = jnp.cumsum(jnp.pad(logf, ((0, 0), (0, 0), (0, lp - l))), axis=-1)
    scale = dh ** -0.5
    outs = []
    for i in range(n_blk):
        start, stop = i * Q_BLOCK, (i + 1) * Q_BLOCK
        qb = qp[:, :, start:stop]
        s = jnp.einsum('bhqd,bhkd->bhqk', qb, kp[:, :, :stop],
                       preferred_element_type=jnp.float32) * scale
        s = s + (cum[:, :, start:stop, None] - cum[:, :, None, :stop])
        causal = jnp.arange(stop)[None, :] <= jnp.arange(start, stop)[:, None]
        s = jnp.where(causal, s, -jnp.inf)
        p = jax.nn.softmax(s, axis=-1)
        outs.append(jnp.einsum('bhqk,bhkd->bhqd', p.astype(vp.dtype), vp[:, :, :stop]))
    out = jnp.concatenate(outs, axis=2)
    return out[:, :, :l]


def _hgrn2(q, log_f, k, v):
    b, l, nh, dk = q.shape
    dv = v.shape[-1]
    lead = (-N_META) % CHUNK
    tail = (-(l + lead)) % CHUNK
    n_chunk = (l + lead + tail) // CHUNK
    pad = ((0, 0), (lead, tail), (0, 0), (0, 0))

    def to_chunks(a):
        a = jnp.pad(a.astype(jnp.float32), pad)
        a = a.reshape(b, n_chunk, CHUNK, nh, a.shape[-1])
        return jnp.transpose(a, (2, 0, 1, 3, 4))

    qc, fc, kc, vc = to_chunks(q), to_chunks(log_f), to_chunks(k), to_chunks(v)

    def step(s, inp):
        q_t, f_t, k_t, v_t = inp
        s = jnp.exp(f_t)[..., None] * s + k_t[..., None] * v_t[..., None, :]
        return s, jnp.einsum('bnhk,bnhkv->bnhv', q_t, s)

    s0 = jnp.zeros((b, n_chunk, nh, dk, dv), jnp.float32)
    s_local, o_local = lax.scan(step, s0, (qc, fc, kc, vc))
    g = jnp.cumsum(fc, axis=0)

    def carry_step(s, inp):
        s_loc, g_tot = inp
        return jnp.exp(g_tot)[..., None] * s + s_loc, s

    _, s_prev = lax.scan(carry_step, jnp.zeros((b, nh, dk, dv), jnp.float32),
                         (jnp.moveaxis(s_local, 1, 0), jnp.moveaxis(g[-1], 1, 0)))
    s_prev = jnp.moveaxis(s_prev, 0, 1)
    o = o_local + jnp.einsum('cbnhk,bnhkv->cbnhv', qc * jnp.exp(g), s_prev)
    o = jnp.transpose(o, (1, 2, 0, 3, 4)).reshape(b, n_chunk * CHUNK, nh, dv)
    return o[:, lead:lead + l]


def _causal_dwconv(u, w, bias):
    c = u.shape[-1]
    y = lax.conv_general_dilated(u, w[:, None, :].astype(u.dtype), window_strides=(1,),
                                 padding=[(CONV_WIDTH - 1, 0)],
                                 dimension_numbers=('NWC', 'WIO', 'NWC'),
                                 feature_group_count=c)
    return y + bias.astype(u.dtype)


def _fwd_setup_inputs(seed: int = 0) -> dict:
    key = jax.random.key(seed)
    ks = jax.random.split(key, 18)
    f32 = jnp.float32

    def nrm(k, shape, scale):
        return jax.random.normal(k, shape, f32) * scale

    return {
        'x': nrm(ks[0], (BATCH, SEQ, D_MODEL), 1.0),
        'meta_tokens': nrm(ks[1], (N_META, D_MODEL), 1.0),
        'norm1_gain': 1.0 + nrm(ks[2], (DEPTH, D_MODEL), 0.02),
        'w_in': nrm(ks[3], (DEPTH, D_MODEL, IN_COLS), D_MODEL ** -0.5),
        'fox_b_f': 3.0 + nrm(ks[4], (DEPTH, FOX_HEADS), 0.5),
        'q_norm_gain': 1.0 + nrm(ks[5], (DEPTH, FOX_HEAD_DIM), 0.02),
        'k_norm_gain': 1.0 + nrm(ks[6], (DEPTH, FOX_HEAD_DIM), 0.02),
        'hg_lb_logits': nrm(ks[7], (DEPTH + 1, HG_KWIDTH), 0.1),
        'hg_out_gain': 1.0 + nrm(ks[8], (DEPTH, HG_VAL_DIM), 0.02),
        'w_branch_a': nrm(ks[9], (DEPTH, HG_VWIDTH, D_MODEL), HG_VWIDTH ** -0.5),
        'w_branch_b': nrm(ks[10], (DEPTH, FOX_WIDTH, D_MODEL), FOX_WIDTH ** -0.5),
        'w_out': nrm(ks[11], (DEPTH, D_MODEL, D_MODEL), D_MODEL ** -0.5),
        'norm2_gain': 1.0 + nrm(ks[12], (DEPTH, D_MODEL), 0.02),
        'w_up': nrm(ks[13], (DEPTH, D_MODEL, 2 * D_FF), D_MODEL ** -0.5),
        'conv_w': nrm(ks[14], (DEPTH, CONV_WIDTH, 2 * D_FF), CONV_WIDTH ** -0.5),
        'conv_b': nrm(ks[15], (DEPTH, 2 * D_FF), 0.02),
        'w_down': nrm(ks[16], (DEPTH, D_FF, D_MODEL), D_FF ** -0.5),
    }


def _fwd_reference(x, meta_tokens, norm1_gain, w_in, fox_b_f, q_norm_gain, k_norm_gain,
              hg_lb_logits, hg_out_gain, w_branch_a, w_branch_b, w_out, norm2_gain,
              w_up, conv_w, conv_b, w_down):
    b = x.shape[0]
    meta = jnp.broadcast_to(meta_tokens[None].astype(x.dtype), (b, N_META, D_MODEL))
    h = jnp.concatenate([meta, x], axis=1)
    l = h.shape[1]
    lower_bounds = jnp.cumsum(jax.nn.softmax(hg_lb_logits.astype(jnp.float32), axis=0), axis=0)

    for layer in range(DEPTH):
        xn = _rms(h, norm1_gain[layer])
        proj = xn @ w_in[layer]
        (fq, fk, fv, ff, hq, hf, hi, hg, gate_a, gate_b) = jnp.split(proj, SPLIT_POINTS, axis=-1)

        fq = _rms(fq.reshape(b, l, FOX_HEADS, FOX_HEAD_DIM), q_norm_gain[layer])
        fk = _rms(fk.reshape(b, l, FOX_HEADS, FOX_HEAD_DIM), k_norm_gain[layer])
        fv = fv.reshape(b, l, FOX_HEADS, FOX_HEAD_DIM)
        f_logit = jnp.transpose(ff + fox_b_f[layer].astype(ff.dtype), (0, 2, 1))
        o_fox = _fox_attention(jnp.transpose(fq, (0, 2, 1, 3)), jnp.transpose(fk, (0, 2, 1, 3)),
                               jnp.transpose(fv, (0, 2, 1, 3)), f_logit)
        o_fox = jnp.transpose(o_fox, (0, 2, 1, 3)).reshape(b, l, FOX_WIDTH)

        lb = lower_bounds[layer].reshape(HG_HEADS, HG_KEY_DIM)
        hf32 = hf.astype(jnp.float32).reshape(b, l, HG_HEADS, HG_KEY_DIM)
        log_f = jnp.log(lb + (1.0 - lb) * jax.nn.sigmoid(hf32))
        k_in = (1.0 - lb) * jax.nn.sigmoid(-hf32)
        o_hg = _hgrn2(hq.reshape(b, l, HG_HEADS, HG_KEY_DIM), log_f, k_in,
                      hi.reshape(b, l, HG_HEADS, HG_VAL_DIM)).astype(h.dtype)
        o_hg = _rms(o_hg, hg_out_gain[layer]) * jax.nn.silu(hg.reshape(b, l, HG_HEADS, HG_VAL_DIM))
        o_hg = o_hg.reshape(b, l, HG_VWIDTH)

        y_a = o_hg @ w_branch_a[layer]
        y_b = o_fox @ w_branch_b[layer]
        merged = jax.nn.sigmoid(gate_a) * y_a + jax.nn.sigmoid(gate_b) * y_b
        h = h + merged @ w_out[layer]

        hn = _rms(h, norm2_gain[layer])
        u = _causal_dwconv(hn @ w_up[layer], conv_w[layer], conv_b[layer])
        u_gate, u_val = jnp.split(u, 2, axis=-1)
        h = h + (jax.nn.silu(u_gate) * u_val) @ w_down[layer]

    return h[:, N_META:]


import jax as _jax
import jax.numpy as _jnp

TWIN_FORMAT = 'train_step'
FWD_PARAMS = ['x', 'meta_tokens', 'norm1_gain', 'w_in', 'fox_b_f', 'q_norm_gain', 'k_norm_gain', 'hg_lb_logits', 'hg_out_gain', 'w_branch_a', 'w_branch_b', 'w_out', 'norm2_gain', 'w_up', 'conv_w', 'conv_b', 'w_down']
TWIN_WEIGHTS = ['meta_tokens', 'norm1_gain', 'w_in', 'fox_b_f', 'q_norm_gain', 'k_norm_gain', 'hg_lb_logits', 'hg_out_gain', 'w_branch_a', 'w_branch_b', 'w_out', 'norm2_gain', 'w_up', 'conv_w', 'conv_b', 'w_down']
TWIN_DIFF_INPUT = 'x'
TWIN_INPUTS = ['x', 'meta_tokens', 'norm1_gain', 'w_in', 'fox_b_f', 'q_norm_gain', 'k_norm_gain', 'hg_lb_logits', 'hg_out_gain', 'w_branch_a', 'w_branch_b', 'w_out', 'norm2_gain', 'w_up', 'conv_w', 'conv_b', 'w_down', 'loss_target', 'm_meta_tokens', 'm_norm1_gain', 'm_w_in', 'm_fox_b_f', 'm_q_norm_gain', 'm_k_norm_gain', 'm_hg_lb_logits', 'm_hg_out_gain', 'm_w_branch_a', 'm_w_branch_b', 'm_w_out', 'm_norm2_gain', 'm_w_up', 'm_conv_w', 'm_conv_b', 'm_w_down', 'v_meta_tokens', 'v_norm1_gain', 'v_w_in', 'v_fox_b_f', 'v_q_norm_gain', 'v_k_norm_gain', 'v_hg_lb_logits', 'v_hg_out_gain', 'v_w_branch_a', 'v_w_branch_b', 'v_w_out', 'v_norm2_gain', 'v_w_up', 'v_conv_w', 'v_conv_b', 'v_w_down']
TWIN_OUTPUTS = ['loss', 'grad_x', 'grad_meta_tokens', 'grad_norm1_gain', 'grad_w_in', 'grad_fox_b_f', 'grad_q_norm_gain', 'grad_k_norm_gain', 'grad_hg_lb_logits', 'grad_hg_out_gain', 'grad_w_branch_a', 'grad_w_branch_b', 'grad_w_out', 'grad_norm2_gain', 'grad_w_up', 'grad_conv_w', 'grad_conv_b', 'grad_w_down', 'delta_meta_tokens', 'delta_norm1_gain', 'delta_w_in', 'delta_fox_b_f', 'delta_q_norm_gain', 'delta_k_norm_gain', 'delta_hg_lb_logits', 'delta_hg_out_gain', 'delta_w_branch_a', 'delta_w_branch_b', 'delta_w_out', 'delta_norm2_gain', 'delta_w_up', 'delta_conv_w', 'delta_conv_b', 'delta_w_down', 'new_m_meta_tokens', 'new_m_norm1_gain', 'new_m_w_in', 'new_m_fox_b_f', 'new_m_q_norm_gain', 'new_m_k_norm_gain', 'new_m_hg_lb_logits', 'new_m_hg_out_gain', 'new_m_w_branch_a', 'new_m_w_branch_b', 'new_m_w_out', 'new_m_norm2_gain', 'new_m_w_up', 'new_m_conv_w', 'new_m_conv_b', 'new_m_w_down', 'new_v_meta_tokens', 'new_v_norm1_gain', 'new_v_w_in', 'new_v_fox_b_f', 'new_v_q_norm_gain', 'new_v_k_norm_gain', 'new_v_hg_lb_logits', 'new_v_hg_out_gain', 'new_v_w_branch_a', 'new_v_w_branch_b', 'new_v_w_out', 'new_v_norm2_gain', 'new_v_w_up', 'new_v_conv_w', 'new_v_conv_b', 'new_v_w_down']
TWIN_LEAF_KINDS = {'loss': 'loss', 'grad_x': 'grad_x', 'grad_meta_tokens': 'grad_w', 'grad_norm1_gain': 'grad_w', 'grad_w_in': 'grad_w', 'grad_fox_b_f': 'grad_w', 'grad_q_norm_gain': 'grad_w', 'grad_k_norm_gain': 'grad_w', 'grad_hg_lb_logits': 'grad_w', 'grad_hg_out_gain': 'grad_w', 'grad_w_branch_a': 'grad_w', 'grad_w_branch_b': 'grad_w', 'grad_w_out': 'grad_w', 'grad_norm2_gain': 'grad_w', 'grad_w_up': 'grad_w', 'grad_conv_w': 'grad_w', 'grad_conv_b': 'grad_w', 'grad_w_down': 'grad_w', 'delta_meta_tokens': 'delta_w', 'delta_norm1_gain': 'delta_w', 'delta_w_in': 'delta_w', 'delta_fox_b_f': 'delta_w', 'delta_q_norm_gain': 'delta_w', 'delta_k_norm_gain': 'delta_w', 'delta_hg_lb_logits': 'delta_w', 'delta_hg_out_gain': 'delta_w', 'delta_w_branch_a': 'delta_w', 'delta_w_branch_b': 'delta_w', 'delta_w_out': 'delta_w', 'delta_norm2_gain': 'delta_w', 'delta_w_up': 'delta_w', 'delta_conv_w': 'delta_w', 'delta_conv_b': 'delta_w', 'delta_w_down': 'delta_w', 'new_m_meta_tokens': 'new_m', 'new_m_norm1_gain': 'new_m', 'new_m_w_in': 'new_m', 'new_m_fox_b_f': 'new_m', 'new_m_q_norm_gain': 'new_m', 'new_m_k_norm_gain': 'new_m', 'new_m_hg_lb_logits': 'new_m', 'new_m_hg_out_gain': 'new_m', 'new_m_w_branch_a': 'new_m', 'new_m_w_branch_b': 'new_m', 'new_m_w_out': 'new_m', 'new_m_norm2_gain': 'new_m', 'new_m_w_up': 'new_m', 'new_m_conv_w': 'new_m', 'new_m_conv_b': 'new_m', 'new_m_w_down': 'new_m', 'new_v_meta_tokens': 'new_v', 'new_v_norm1_gain': 'new_v', 'new_v_w_in': 'new_v', 'new_v_fox_b_f': 'new_v', 'new_v_q_norm_gain': 'new_v', 'new_v_k_norm_gain': 'new_v', 'new_v_hg_lb_logits': 'new_v', 'new_v_hg_out_gain': 'new_v', 'new_v_w_branch_a': 'new_v', 'new_v_w_branch_b': 'new_v', 'new_v_w_out': 'new_v', 'new_v_norm2_gain': 'new_v', 'new_v_w_up': 'new_v', 'new_v_conv_w': 'new_v', 'new_v_conv_b': 'new_v', 'new_v_w_down': 'new_v'}


def _forward(args):
    return _fwd_reference(*[args[k] for k in FWD_PARAMS])


def _output_shape():
    out = _jax.eval_shape(lambda: _forward(_fwd_setup_inputs(0)))
    return out.shape, out.dtype

N_MICROBATCH = 1
ADAM_LR = 0.001
ADAM_B1 = 0.9
ADAM_B2 = 0.999
ADAM_EPS = 1e-08
ADAM_WD = 0.01
ADAM_STEP = 10
PER_EXAMPLE_BATCH_AXIS = {'x': 0, 'loss_target': 0}
SHARED_INPUTS = []
_WEIGHT_DTYPES = {'meta_tokens': _jnp.float32, 'norm1_gain': _jnp.float32, 'w_in': _jnp.float32, 'fox_b_f': _jnp.float32, 'q_norm_gain': _jnp.float32, 'k_norm_gain': _jnp.float32, 'hg_lb_logits': _jnp.float32, 'hg_out_gain': _jnp.float32, 'w_branch_a': _jnp.float32, 'w_branch_b': _jnp.float32, 'w_out': _jnp.float32, 'norm2_gain': _jnp.float32, 'w_up': _jnp.float32, 'conv_w': _jnp.float32, 'conv_b': _jnp.float32, 'w_down': _jnp.float32}
MOMENT_SCALE = {'meta_tokens': 1.295628e-02, 'norm1_gain': 9.971846e+00, 'w_in': 2.432864e-01, 'fox_b_f': 8.250345e+01, 'q_norm_gain': 1.211417e+01, 'k_norm_gain': 1.220492e+01, 'hg_lb_logits': 2.195326e-01, 'hg_out_gain': 5.360727e+01, 'w_branch_a': 2.722544e-01, 'w_branch_b': 1.298179e-01, 'w_out': 2.958312e-01, 'norm2_gain': 5.047685e+01, 'w_up': 3.262914e-01, 'conv_w': 6.899825e+00, 'conv_b': 5.966554e+00, 'w_down': 4.780480e-01}


def _to_microbatches(a, axis):
    t = _jnp.moveaxis(a, axis, 0)
    t = t.reshape((N_MICROBATCH, t.shape[0] // N_MICROBATCH) + t.shape[1:])
    return _jnp.moveaxis(t, 1, axis + 1)


def setup_inputs(seed: int = 0) -> dict:
    inp = _fwd_setup_inputs(seed)
    key = _jax.random.fold_in(_jax.random.key(seed), 7919)
    shape, _ = _output_shape()
    out = dict(inp)
    out["loss_target"] = _jax.random.normal(_jax.random.fold_in(key, 0), shape, _jnp.float32)
    for i, name in enumerate(TWIN_WEIGHTS):
        w = inp[name].astype(_jnp.float32)
        if MOMENT_SCALE is None:
            s = _jnp.sqrt(_jnp.mean(_jnp.square(w)) + 1e-30)
        else:
            s = MOMENT_SCALE[name]
        km, kv = _jax.random.split(_jax.random.fold_in(key, i + 1))
        out[name] = w
        out["m_" + name] = s * _jax.random.normal(km, w.shape, _jnp.float32)
        out["v_" + name] = (s * s) * _jax.random.uniform(kv, w.shape, _jnp.float32, 0.5, 1.5)
    if N_MICROBATCH > 1:
        for name, axis in PER_EXAMPLE_BATCH_AXIS.items():
            out[name] = _to_microbatches(out[name], axis)
    return {'x': out['x'], 'meta_tokens': out['meta_tokens'], 'norm1_gain': out['norm1_gain'], 'w_in': out['w_in'], 'fox_b_f': out['fox_b_f'], 'q_norm_gain': out['q_norm_gain'], 'k_norm_gain': out['k_norm_gain'], 'hg_lb_logits': out['hg_lb_logits'], 'hg_out_gain': out['hg_out_gain'], 'w_branch_a': out['w_branch_a'], 'w_branch_b': out['w_branch_b'], 'w_out': out['w_out'], 'norm2_gain': out['norm2_gain'], 'w_up': out['w_up'], 'conv_w': out['conv_w'], 'conv_b': out['conv_b'], 'w_down': out['w_down'], 'loss_target': out['loss_target'], 'm_meta_tokens': out['m_meta_tokens'], 'm_norm1_gain': out['m_norm1_gain'], 'm_w_in': out['m_w_in'], 'm_fox_b_f': out['m_fox_b_f'], 'm_q_norm_gain': out['m_q_norm_gain'], 'm_k_norm_gain': out['m_k_norm_gain'], 'm_hg_lb_logits': out['m_hg_lb_logits'], 'm_hg_out_gain': out['m_hg_out_gain'], 'm_w_branch_a': out['m_w_branch_a'], 'm_w_branch_b': out['m_w_branch_b'], 'm_w_out': out['m_w_out'], 'm_norm2_gain': out['m_norm2_gain'], 'm_w_up': out['m_w_up'], 'm_conv_w': out['m_conv_w'], 'm_conv_b': out['m_conv_b'], 'm_w_down': out['m_w_down'], 'v_meta_tokens': out['v_meta_tokens'], 'v_norm1_gain': out['v_norm1_gain'], 'v_w_in': out['v_w_in'], 'v_fox_b_f': out['v_fox_b_f'], 'v_q_norm_gain': out['v_q_norm_gain'], 'v_k_norm_gain': out['v_k_norm_gain'], 'v_hg_lb_logits': out['v_hg_lb_logits'], 'v_hg_out_gain': out['v_hg_out_gain'], 'v_w_branch_a': out['v_w_branch_a'], 'v_w_branch_b': out['v_w_branch_b'], 'v_w_out': out['v_w_out'], 'v_norm2_gain': out['v_norm2_gain'], 'v_w_up': out['v_w_up'], 'v_conv_w': out['v_conv_w'], 'v_conv_b': out['v_conv_b'], 'v_w_down': out['v_w_down']}


def _loss(weights, diff, rest, loss_target):
    with _jax.named_scope("forward"):
        args = {**rest, TWIN_DIFF_INPUT: diff, **{k: w.astype(_WEIGHT_DTYPES[k]) for k, w in weights.items()}}
        y = _forward(args)
    with _jax.named_scope("loss_head"):
        err = _jnp.square(y.astype(_jnp.float32) - loss_target)
        return 0.5 * _jnp.sum(_jnp.mean(err, axis=-1)) if err.ndim else 0.5 * err


def _adamw(w, g, m, v):
    m = ADAM_B1 * m + (1.0 - ADAM_B1) * g
    v = ADAM_B2 * v + (1.0 - ADAM_B2) * _jnp.square(g)
    m_hat = m / (1.0 - ADAM_B1 ** ADAM_STEP)
    v_hat = v / (1.0 - ADAM_B2 ** ADAM_STEP)
    delta = -ADAM_LR * (m_hat / (_jnp.sqrt(v_hat) + ADAM_EPS) + ADAM_WD * w)
    return delta, m, v


def reference(x, meta_tokens, norm1_gain, w_in, fox_b_f, q_norm_gain, k_norm_gain, hg_lb_logits, hg_out_gain, w_branch_a, w_branch_b, w_out, norm2_gain, w_up, conv_w, conv_b, w_down, loss_target, m_meta_tokens, m_norm1_gain, m_w_in, m_fox_b_f, m_q_norm_gain, m_k_norm_gain, m_hg_lb_logits, m_hg_out_gain, m_w_branch_a, m_w_branch_b, m_w_out, m_norm2_gain, m_w_up, m_conv_w, m_conv_b, m_w_down, v_meta_tokens, v_norm1_gain, v_w_in, v_fox_b_f, v_q_norm_gain, v_k_norm_gain, v_hg_lb_logits, v_hg_out_gain, v_w_branch_a, v_w_branch_b, v_w_out, v_norm2_gain, v_w_up, v_conv_w, v_conv_b, v_w_down):
    given = dict(x=x, meta_tokens=meta_tokens, norm1_gain=norm1_gain, w_in=w_in, fox_b_f=fox_b_f, q_norm_gain=q_norm_gain, k_norm_gain=k_norm_gain, hg_lb_logits=hg_lb_logits, hg_out_gain=hg_out_gain, w_branch_a=w_branch_a, w_branch_b=w_branch_b, w_out=w_out, norm2_gain=norm2_gain, w_up=w_up, conv_w=conv_w, conv_b=conv_b, w_down=w_down, loss_target=loss_target, m_meta_tokens=m_meta_tokens, m_norm1_gain=m_norm1_gain, m_w_in=m_w_in, m_fox_b_f=m_fox_b_f, m_q_norm_gain=m_q_norm_gain, m_k_norm_gain=m_k_norm_gain, m_hg_lb_logits=m_hg_lb_logits, m_hg_out_gain=m_hg_out_gain, m_w_branch_a=m_w_branch_a, m_w_branch_b=m_w_branch_b, m_w_out=m_w_out, m_norm2_gain=m_norm2_gain, m_w_up=m_w_up, m_conv_w=m_conv_w, m_conv_b=m_conv_b, m_w_down=m_w_down, v_meta_tokens=v_meta_tokens, v_norm1_gain=v_norm1_gain, v_w_in=v_w_in, v_fox_b_f=v_fox_b_f, v_q_norm_gain=v_q_norm_gain, v_k_norm_gain=v_k_norm_gain, v_hg_lb_logits=v_hg_lb_logits, v_hg_out_gain=v_hg_out_gain, v_w_branch_a=v_w_branch_a, v_w_branch_b=v_w_branch_b, v_w_out=v_w_out, v_norm2_gain=v_norm2_gain, v_w_up=v_w_up, v_conv_w=v_conv_w, v_conv_b=v_conv_b, v_w_down=v_w_down)
    weights = {n: given[n] for n in TWIN_WEIGHTS}
    shared = {n: given[n] for n in SHARED_INPUTS}
    per_example = {n: given[n] for n in ['x']}
    grad_fn = _jax.value_and_grad(_loss, argnums=(0, 1))

    def one_microbatch(ex, loss_target):
        ex = dict(ex)
        diff = ex.pop(TWIN_DIFF_INPUT)
        return grad_fn(weights, diff, {**shared, **ex}, loss_target)

    if N_MICROBATCH == 1:
        loss, (grad_w, grad_x) = one_microbatch(per_example, given["loss_target"])
    else:
        def body(carry, xs):
            loss_sum, grad_sum = carry
            l_k, (gw_k, gx_k) = one_microbatch(xs[0], xs[1])
            with _jax.named_scope("update"):
                return (loss_sum + l_k, _jax.tree.map(_jnp.add, grad_sum, gw_k)), gx_k

        init = (_jnp.zeros((), _jnp.float32), _jax.tree.map(_jnp.zeros_like, weights))
        (loss, grad_w), grad_x = _jax.lax.scan(body, init, (per_example, given["loss_target"]))
    with _jax.named_scope("update"):
        delta_w, new_m, new_v = {}, {}, {}
        for n in TWIN_WEIGHTS:
            delta_w[n], new_m[n], new_v[n] = _adamw(weights[n], grad_w[n], given["m_" + n], given["v_" + n])
    return (loss, grad_x, *[grad_w[n] for n in TWIN_WEIGHTS], *[delta_w[n] for n in TWIN_WEIGHTS],
            *[new_m[n] for n in TWIN_WEIGHTS], *[new_v[n] for n in TWIN_WEIGHTS])
```

```python
import functools

import jax
import jax.numpy as jnp
import numpy as np
from jax import lax
from jax.experimental import pallas as pl
from jax.experimental.pallas import tpu as pltpu

F32 = jnp.float32
BF16 = jnp.bfloat16

D_MODEL = 1024
N_META = 16
FOX_HEADS = 8
FOX_HEAD_DIM = 64
FOX_WIDTH = FOX_HEADS * FOX_HEAD_DIM
HG_HEADS = 4
HG_DIM = 128
HG_WIDTH = HG_HEADS * HG_DIM
D_FF = 2816
CONV_WIDTH = 3
EPS = 1e-6
IN_COLS = 3 * FOX_WIDTH + FOX_HEADS + 4 * HG_WIDTH + 2 * D_MODEL
N_DEV = 8

ADAM_LR = 0.001
ADAM_B1 = 0.9
ADAM_B2 = 0.999
ADAM_EPS = 1e-08
ADAM_WD = 0.01
ADAM_STEP = 10

LANES = 128
SEQ_BLOCK = 128
SUB = 16
NEG = -1e30
VMEM_LIMIT = 48 * 1024 * 1024

ALIGNED_COLS = IN_COLS - FOX_HEADS + LANES
FOX_CB = 2 * D_MODEL // FOX_WIDTH
CB_HQ = (2 * D_MODEL + 3 * FOX_WIDTH) // LANES
CB_HF = CB_HQ + HG_HEADS
CB_HI = CB_HF + HG_HEADS
CB_HG = CB_HI + HG_HEADS
CB_FF = CB_HG + HG_HEADS


def _align_cols(w):
    a, b = 3 * FOX_WIDTH, 3 * FOX_WIDTH + FOX_HEADS
    c = b + 4 * HG_WIDTH
    pad = [(0, 0)] * (w.ndim - 1) + [(0, LANES - FOX_HEADS)]
    return jnp.concatenate([w[..., c:], w[..., :a], w[..., b:c], jnp.pad(w[..., a:b], pad)], axis=-1)


def _unalign_cols(g):
    a, b = 2 * D_MODEL, 2 * D_MODEL + 3 * FOX_WIDTH
    c = b + 4 * HG_WIDTH
    return jnp.concatenate([g[..., a:b], g[..., c:c + FOX_HEADS], g[..., b:c], g[..., :a]], axis=-1)


def _div_tile(n, target, mult):
    best = None
    for t in range(mult, min(n, target) + 1, mult):
        if n % t == 0:
            best = t
    if best is None:
        best = n
    return best


def _cp(*sem):
    return pltpu.CompilerParams(dimension_semantics=sem, vmem_limit_bytes=VMEM_LIMIT)


def _sigmoid(x):
    return 1.0 / (1.0 + jnp.exp(-x))


def _dot(a, b, dims, precision=None):
    return lax.dot_general(a, b, (dims, ((), ())), preferred_element_type=F32, precision=precision)


NN = ((1,), (0,))
NT = ((1,), (1,))
TN = ((0,), (0,))
HI = lax.Precision.HIGHEST


def _matmul(a, b, mode, out_dtype, name, residual=None, tm_t=1024, tn_t=1024, tk_t=1024):
    if mode == "nn":
        (m, k), (k2, n) = a.shape, b.shape
    elif mode == "nt":
        (m, k), (n, k2) = a.shape, b.shape
    else:
        (k, m), (k2, n) = a.shape, b.shape
    assert k == k2, (a.shape, b.shape, mode)
    tm = _div_tile(m, tm_t, LANES)
    tn = _div_tile(n, tn_t, LANES)
    tk = _div_tile(k, tk_t, LANES)
    nk = k // tk
    if mode == "nn":
        a_spec = pl.BlockSpec((tm, tk), lambda i, j, kk: (i, kk))
        b_spec = pl.BlockSpec((tk, tn), lambda i, j, kk: (kk, j))
        dims = NN
    elif mode == "nt":
        a_spec = pl.BlockSpec((tm, tk), lambda i, j, kk: (i, kk))
        b_spec = pl.BlockSpec((tn, tk), lambda i, j, kk: (j, kk))
        dims = NT
    else:
        a_spec = pl.BlockSpec((tk, tm), lambda i, j, kk: (kk, i))
        b_spec = pl.BlockSpec((tk, tn), lambda i, j, kk: (kk, j))
        dims = TN
    o_spec = pl.BlockSpec((tm, tn), lambda i, j, kk: (i, j))
    has_res = residual is not None

    def body(*refs):
        if has_res:
            a_ref, b_ref, r_ref, o_ref, acc_ref = refs
        else:
            a_ref, b_ref, o_ref, acc_ref = refs
        kk = pl.program_id(2)
        part = _dot(a_ref[...].astype(BF16), b_ref[...].astype(BF16), dims)

        @pl.when(kk == 0)
        def _():
            acc_ref[...] = part

        @pl.when(kk > 0)
        def _():
            acc_ref[...] += part

        @pl.when(kk == nk - 1)
        def _():
            acc = acc_ref[...]
            if has_res:
                acc = acc + r_ref[...]
            o_ref[...] = acc.astype(o_ref.dtype)

    in_specs = [a_spec, b_spec] + ([o_spec] if has_res else [])
    args = (a, b) + ((residual,) if has_res else ())
    return pl.pallas_call(
        body, name=name, grid=(m // tm, n // tn, nk),
        in_specs=in_specs, out_specs=o_spec,
        out_shape=jax.ShapeDtypeStruct((m, n), out_dtype),
        scratch_shapes=[pltpu.VMEM((tm, tn), F32)],
        compiler_params=_cp("parallel", "parallel", "arbitrary"),
    )(*args)


def _rms_fwd(x, gain, name):
    m, d = x.shape
    tm = _div_tile(m, 512, 16)

    def body(x_ref, g_ref, o_ref):
        xv = x_ref[...]
        r = lax.rsqrt(jnp.mean(xv * xv, axis=-1, keepdims=True) + EPS)
        o_ref[...] = ((xv * r) * g_ref[...]).astype(o_ref.dtype)

    return pl.pallas_call(
        body, name=name, grid=(m // tm,),
        in_specs=[pl.BlockSpec((tm, d), lambda i: (i, 0)), pl.BlockSpec((1, d), lambda i: (0, 0))],
        out_specs=pl.BlockSpec((tm, d), lambda i: (i, 0)),
        out_shape=jax.ShapeDtypeStruct((m, d), BF16),
        compiler_params=_cp("parallel"),
    )(x, gain)


def _rms_bwd(x, gain, dy, dres, name):
    m, d = x.shape
    tm = _div_tile(m, 256, 8)

    def body(x_ref, g_ref, dy_ref, dr_ref, dx_ref, dg_ref):
        xv = x_ref[...]
        r = lax.rsqrt(jnp.mean(xv * xv, axis=-1, keepdims=True) + EPS)
        nv = xv * r
        dyv = dy_ref[...]
        gdy = dyv * g_ref[...]
        dx_ref[...] = dr_ref[...] + r * (gdy - nv * jnp.mean(gdy * nv, axis=-1, keepdims=True))
        part = jnp.sum(dyv * nv, axis=0, keepdims=True)

        @pl.when(pl.program_id(0) == 0)
        def _():
            dg_ref[...] = part

        @pl.when(pl.program_id(0) > 0)
        def _():
            dg_ref[...] += part

    row = pl.BlockSpec((tm, d), lambda i: (i, 0))
    vec = pl.BlockSpec((1, d), lambda i: (0, 0))
    return pl.pallas_call(
        body, name=name, grid=(m // tm,),
        in_specs=[row, vec, row, row], out_specs=[row, vec],
        out_shape=[jax.ShapeDtypeStruct((m, d), F32), jax.ShapeDtypeStruct((1, d), F32)],
        compiler_params=_cp("arbitrary"),
    )(x, gain, dy, dres)


def _head_stats(xv, lo):
    sq = xv * xv
    s_lo = jnp.sum(jnp.where(lo, sq, 0.0), axis=1, keepdims=True)
    s_hi = jnp.sum(jnp.where(lo, 0.0, sq), axis=1, keepdims=True)
    return jnp.where(lo, s_lo, s_hi) * (1.0 / FOX_HEAD_DIM)


def _fox_prep(proj, qg, kg, bf, nb, lp, name):
    m = proj.shape[0]
    ts = SEQ_BLOCK
    nblk = lp // ts
    scale = FOX_HEAD_DIM ** -0.5

    def body(q_ref, k_ref, v_ref, f_ref, qg_ref, kg_ref, bf_ref, qo_ref, ko_ref, vo_ref, cum_ref, carry_ref):
        lane = lax.broadcasted_iota(jnp.int32, (1, LANES), 1)
        lo = lane < FOX_HEAD_DIM
        for j in range(FOX_WIDTH // LANES):
            cs = slice(j * LANES, (j + 1) * LANES)
            xq = q_ref[:, cs]
            rq = lax.rsqrt(_head_stats(xq, lo) + EPS)
            qo_ref[:, cs] = (((xq * rq) * qg_ref[:, cs]) * scale).astype(BF16)
            xk = k_ref[:, cs]
            rk = lax.rsqrt(_head_stats(xk, lo) + EPS)
            ko_ref[:, cs] = ((xk * rk) * kg_ref[:, cs]).astype(BF16)
        vo_ref[...] = v_ref[...].astype(BF16)

        @pl.when(pl.program_id(1) == 0)
        def _():
            carry_ref[...] = jnp.zeros_like(carry_ref)

        z = f_ref[...] + bf_ref[...]
        logf = jnp.minimum(z, 0.0) - jnp.log(1.0 + jnp.exp(-jnp.abs(z)))
        logf = jnp.where(lane < FOX_HEADS, logf, 0.0)
        r = lax.broadcasted_iota(jnp.int32, (ts, ts), 0)
        c = lax.broadcasted_iota(jnp.int32, (ts, ts), 1)
        tri = jnp.where(c <= r, 1.0, 0.0).astype(F32)
        cum = _dot(tri, logf, NN, HI) + carry_ref[...]
        cum_ref[...] = cum
        carry_ref[...] = cum_ref[ts - 1:ts, :]

    w = FOX_WIDTH
    row = lambda b, i: (b * nblk + i, 0)
    return pl.pallas_call(
        body, name=name, grid=(nb, nblk),
        in_specs=[pl.BlockSpec((ts, w), lambda b, i: (b * nblk + i, FOX_CB)),
                  pl.BlockSpec((ts, w), lambda b, i: (b * nblk + i, FOX_CB + 1)),
                  pl.BlockSpec((ts, w), lambda b, i: (b * nblk + i, FOX_CB + 2)),
                  pl.BlockSpec((ts, LANES), lambda b, i: (b * nblk + i, CB_FF)),
                  pl.BlockSpec((1, w), lambda b, i: (0, 0)),
                  pl.BlockSpec((1, w), lambda b, i: (0, 0)),
                  pl.BlockSpec((1, LANES), lambda b, i: (0, 0))],
        out_specs=[pl.BlockSpec((ts, w), row), pl.BlockSpec((ts, w), row), pl.BlockSpec((ts, w), row),
                   pl.BlockSpec((ts, LANES), row)],
        out_shape=[jax.ShapeDtypeStruct((m, w), BF16)] * 3 + [jax.ShapeDtypeStruct((m, LANES), F32)],
        scratch_shapes=[pltpu.VMEM((1, LANES), F32)],
        compiler_params=_cp("arbitrary", "arbitrary"),
    )(proj, proj, proj, proj, qg, kg, bf)


def _att_tile(lp):
    return 384 if (lp % 384 == 0 and lp > 384) else 128


def _lane_pick(blk, lane, idx):
    return jnp.sum(jnp.where(lane == idx, blk, 0.0), axis=1, keepdims=True)


def _fox_fwd(qs, kn, vb, cum, cum_t, nb, lp, name):
    m = qs.shape[0]
    tq = _att_tile(lp)
    nq = lp // tq
    npair = FOX_WIDTH // LANES

    def body(q_ref, k_ref, v_ref, cum_ref, ck0_ref, ck1_ref, o_ref, lse_ref):
        p = pl.program_id(1)
        qi = pl.program_id(2)
        lane = lax.broadcasted_iota(jnp.int32, (1, LANES), 1)
        q = q_ref[...]
        cumblk = cum_ref[...]
        rows = qi * tq + lax.broadcasted_iota(jnp.int32, (tq, 1), 0)
        o_tot = jnp.zeros((tq, LANES), F32)
        lse_out = jnp.zeros((tq, LANES), F32)
        for hh in range(2):
            hmask = (lane >= hh * FOX_HEAD_DIM) & (lane < (hh + 1) * FOX_HEAD_DIM)
            cq = _lane_pick(cumblk, lane, 2 * p + hh)
            ck_ref = ck0_ref if hh == 0 else ck1_ref

            def step(j, carry, hmask=hmask, cq=cq, ck_ref=ck_ref):
                mx, l, acc = carry
                k0 = pl.multiple_of(j * tq, tq)
                kz = jnp.where(hmask, k_ref[pl.ds(k0, tq), :], jnp.zeros((), BF16))
                vz = jnp.where(hmask, v_ref[pl.ds(k0, tq), :], jnp.zeros((), BF16))
                s = _dot(q, kz, NT) + cq - ck_ref[:, pl.ds(k0, tq)]
                cols = k0 + lax.broadcasted_iota(jnp.int32, (1, tq), 1)
                s = jnp.where(rows >= cols, s, NEG)
                m_new = jnp.maximum(mx, jnp.max(s, axis=1, keepdims=True))
                alpha = jnp.exp(mx - m_new)
                pe = jnp.exp(s - m_new)
                l = alpha * l + jnp.sum(pe, axis=1, keepdims=True)
                acc = alpha * acc + _dot(pe.astype(BF16), vz, NN)
                return m_new, l, acc

            init = (jnp.full((tq, 1), NEG, F32), jnp.zeros((tq, 1), F32), jnp.zeros((tq, LANES), F32))
            mx, l, acc = lax.fori_loop(0, qi + 1, step, init)
            o_tot = o_tot + acc / l
            lse_out = jnp.where(lane == hh, mx + jnp.log(l), lse_out)
        o_ref[...] = o_tot
        lse_ref[...] = lse_out

    return pl.pallas_call(
        body, name=name, grid=(nb, npair, nq),
        in_specs=[pl.BlockSpec((tq, LANES), lambda b, p, i: (b * nq + i, p)),
                  pl.BlockSpec((lp, LANES), lambda b, p, i: (b, p)),
                  pl.BlockSpec((lp, LANES), lambda b, p, i: (b, p)),
                  pl.BlockSpec((tq, LANES), lambda b, p, i: (b * nq + i, 0)),
                  pl.BlockSpec((None, None, 1, lp), lambda b, p, i: (b, 2 * p, 0, 0)),
                  pl.BlockSpec((None, None, 1, lp), lambda b, p, i: (b, 2 * p + 1, 0, 0))],
        out_specs=[pl.BlockSpec((tq, LANES), lambda b, p, i: (b * nq + i, p)),
                   pl.BlockSpec((None, None, tq, LANES), lambda b, p, i: (b, p, i, 0))],
        out_shape=[jax.ShapeDtypeStruct((m, FOX_WIDTH), F32),
                   jax.ShapeDtypeStruct((nb, npair, lp, LANES), F32)],
        compiler_params=_cp("parallel", "parallel", "arbitrary"),
    )(qs, kn, vb, cum, cum_t, cum_t)


def _fox_bwd(qs, kn, vb, do, o, lse, cum, cum_t, nb, lp, name):
    m = qs.shape[0]
    tq = _att_tile(lp)
    nq = lp // tq
    npair = FOX_WIDTH // LANES

    def body(k_ref, v_ref, q_ref, do_ref, o_ref, lse_ref, cum_ref, ck0_ref, ck1_ref,
             dq_ref, dk_ref, dv_ref, dc0_ref, dc1_ref):
        p = pl.program_id(1)
        j = pl.program_id(2)
        lane = lax.broadcasted_iota(jnp.int32, (1, LANES), 1)

        @pl.when(j == 0)
        def _():
            dq_ref[...] = jnp.zeros_like(dq_ref)

        kb = k_ref[...]
        vv = v_ref[...]
        cols = j * tq + lax.broadcasted_iota(jnp.int32, (1, tq), 1)
        hmasks = [(lane >= hh * FOX_HEAD_DIM) & (lane < (hh + 1) * FOX_HEAD_DIM) for hh in range(2)]
        zero16 = jnp.zeros((), BF16)
        kzs = [jnp.where(hm, kb, zero16) for hm in hmasks]
        vzs = [jnp.where(hm, vv, zero16) for hm in hmasks]
        cks = [ck0_ref[...], ck1_ref[...]]

        def step(qi, carry):
            dk, dv, dc0, dc1 = carry
            q0 = pl.multiple_of(qi * tq, tq)
            q = q_ref[pl.ds(q0, tq), :]
            dob = do_ref[pl.ds(q0, tq), :]
            ob = o_ref[pl.ds(q0, tq), :]
            lseb = lse_ref[pl.ds(q0, tq), :]
            cumb = cum_ref[pl.ds(q0, tq), :]
            rows = q0 + lax.broadcasted_iota(jnp.int32, (tq, 1), 0)
            mask = rows >= cols
            dq_acc = jnp.zeros((tq, LANES), F32)
            dcs = [dc0, dc1]
            for hh in range(2):
                doz = jnp.where(hmasks[hh], dob, 0.0)
                delta = jnp.sum(doz * ob, axis=1, keepdims=True)
                lse_h = _lane_pick(lseb, lane, hh)
                cq = _lane_pick(cumb, lane, 2 * p + hh)
                s = _dot(q, kzs[hh], NT) + cq - cks[hh]
                pm = jnp.exp(jnp.where(mask, s - lse_h, NEG))
                doz16 = doz.astype(BF16)
                dp = _dot(doz16, vzs[hh], NT)
                ds = pm * (dp - delta)
                ds16 = ds.astype(BF16)
                dv = dv + _dot(pm.astype(BF16), doz16, TN)
                dk = dk + _dot(ds16, jnp.where(hmasks[hh], q, zero16), TN)
                dq_acc = dq_acc + _dot(ds16, kzs[hh], NN)
                dcs[hh] = dcs[hh] - jnp.sum(ds, axis=0, keepdims=True)
            dq_ref[pl.ds(q0, tq), :] += dq_acc
            return dk, dv, dcs[0], dcs[1]

        init = (jnp.zeros((tq, LANES), F32), jnp.zeros((tq, LANES), F32),
                jnp.zeros((1, tq), F32), jnp.zeros((1, tq), F32))
        dk, dv, dc0, dc1 = lax.fori_loop(j, nq, step, init)
        dk_ref[...] = dk
        dv_ref[...] = dv
        dc0_ref[...] = dc0
        dc1_ref[...] = dc1

    full = pl.BlockSpec((lp, LANES), lambda b, p, j: (b, p))
    blk = pl.BlockSpec((tq, LANES), lambda b, p, j: (b * nq + j, p))
    return pl.pallas_call(
        body, name=name, grid=(nb, npair, nq),
        in_specs=[blk, blk, full, full, full,
                  pl.BlockSpec((None, None, lp, LANES), lambda b, p, j: (b, p, 0, 0)),
                  pl.BlockSpec((lp, LANES), lambda b, p, j: (b, 0)),
                  pl.BlockSpec((None, None, 1, tq), lambda b, p, j: (b, 2 * p, 0, j)),
                  pl.BlockSpec((None, None, 1, tq), lambda b, p, j: (b, 2 * p + 1, 0, j))],
        out_specs=[full, blk, blk,
                   pl.BlockSpec((None, None, 1, tq), lambda b, p, j: (b, p, 0, j)),
                   pl.BlockSpec((None, None, 1, tq), lambda b, p, j: (b, p, 0, j))],
        out_shape=[jax.ShapeDtypeStruct((m, FOX_WIDTH), F32)] * 3
        + [jax.ShapeDtypeStruct((nb, npair, 1, lp), F32)] * 2,
        compiler_params=_cp("parallel", "parallel", "arbitrary"),
    )(kn, vb, qs, do, o, lse, cum, cum_t, cum_t)


def _fox_prep_bwd(proj, dqs, dkn, dv, dcum, qg, kg, bf, nb, lp, name):
    m = proj.shape[0]
    ts = SEQ_BLOCK
    nblk = lp // ts
    scale = FOX_HEAD_DIM ** -0.5
    w = FOX_WIDTH
    wo = 3 * w

    def body(q_ref, k_ref, f_ref, dq_ref, dk_ref, dv_ref, dc_ref, qg_ref, kg_ref, bf_ref,
             out_ref, dff_ref, dqg_ref, dkg_ref, dbf_ref, carry_ref):
        first = (pl.program_id(0) == 0) & (pl.program_id(1) == 0)
        lane = lax.broadcasted_iota(jnp.int32, (1, LANES), 1)
        lo = lane < FOX_HEAD_DIM

        @pl.when(first)
        def _():
            dqg_ref[...] = jnp.zeros_like(dqg_ref)
            dkg_ref[...] = jnp.zeros_like(dkg_ref)
            dbf_ref[...] = jnp.zeros_like(dbf_ref)

        def norm_bwd(x, g, dy):
            r = lax.rsqrt(_head_stats(x, lo) + EPS)
            nv = x * r
            gdy = dy * g
            prod = gdy * nv
            s_lo = jnp.sum(jnp.where(lo, prod, 0.0), axis=1, keepdims=True)
            s_hi = jnp.sum(jnp.where(lo, 0.0, prod), axis=1, keepdims=True)
            mean = jnp.where(lo, s_lo, s_hi) * (1.0 / FOX_HEAD_DIM)
            return r * (gdy - nv * mean), jnp.sum(dy * nv, axis=0, keepdims=True)

        for jj in range(w // LANES):
            cs = slice(jj * LANES, (jj + 1) * LANES)
            dx, dg = norm_bwd(q_ref[:, cs], qg_ref[:, cs], dq_ref[:, cs] * scale)
            out_ref[:, cs] = dx.astype(BF16)
            dqg_ref[:, cs] += dg
            dx, dg = norm_bwd(k_ref[:, cs], kg_ref[:, cs], dk_ref[:, cs])
            out_ref[:, w + jj * LANES:w + (jj + 1) * LANES] = dx.astype(BF16)
            dkg_ref[:, cs] += dg
        out_ref[:, 2 * w:3 * w] = dv_ref[...].astype(BF16)

        @pl.when(pl.program_id(1) == 0)
        def _():
            carry_ref[...] = jnp.zeros_like(carry_ref)

        dc = dc_ref[...]
        r = lax.broadcasted_iota(jnp.int32, (ts, ts), 0)
        c = lax.broadcasted_iota(jnp.int32, (ts, ts), 1)
        triu = jnp.where(c >= r, 1.0, 0.0).astype(F32)
        dlogf = _dot(triu, dc, NN, HI) + carry_ref[...]
        carry_ref[...] += jnp.sum(dc, axis=0, keepdims=True)
        z = f_ref[...] + bf_ref[...]
        dz = jnp.where(lane < FOX_HEADS, dlogf * _sigmoid(-z), 0.0)
        dff_ref[...] = dz.astype(BF16)
        dbf_ref[...] += jnp.sum(dz, axis=0, keepdims=True)

    rev = lambda b, i: (b * nblk + (nblk - 1 - i), 0)
    vec = lambda n: pl.BlockSpec((1, n), lambda b, i: (0, 0))
    return pl.pallas_call(
        body, name=name, grid=(nb, nblk),
        in_specs=[pl.BlockSpec((ts, w), lambda b, i: (b * nblk + (nblk - 1 - i), FOX_CB)),
                  pl.BlockSpec((ts, w), lambda b, i: (b * nblk + (nblk - 1 - i), FOX_CB + 1)),
                  pl.BlockSpec((ts, LANES), lambda b, i: (b * nblk + (nblk - 1 - i), CB_FF)),
                  pl.BlockSpec((ts, w), rev), pl.BlockSpec((ts, w), rev), pl.BlockSpec((ts, w), rev),
                  pl.BlockSpec((ts, LANES), rev), vec(w), vec(w), vec(LANES)],
        out_specs=[pl.BlockSpec((ts, wo), rev), pl.BlockSpec((ts, LANES), rev), vec(w), vec(w), vec(LANES)],
        out_shape=[jax.ShapeDtypeStruct((m, wo), BF16), jax.ShapeDtypeStruct((m, LANES), BF16),
                   jax.ShapeDtypeStruct((1, w), F32),
                   jax.ShapeDtypeStruct((1, w), F32), jax.ShapeDtypeStruct((1, LANES), F32)],
        scratch_shapes=[pltpu.VMEM((1, LANES), F32)],
        compiler_params=_cp("arbitrary", "arbitrary"),
    )(proj, proj, proj, dqs, dkn, dv, dcum, qg, kg, bf)


def _chunk_masks():
    r = lax.broadcasted_iota(jnp.int32, (SEQ_BLOCK, SEQ_BLOCK), 0)
    c = lax.broadcasted_iota(jnp.int32, (SEQ_BLOCK, SEQ_BLOCK), 1)
    same = (r // SUB) == (c // SUB)
    return r, c, same


def _hg_gates(hf, lb):
    sg = _sigmoid(hf)
    f = lb + (1.0 - lb) * sg
    return sg, f, jnp.log(f), (1.0 - lb) * _sigmoid(-hf)


def _hg_intra_e(g_ref, base, t, srow):
    diff = g_ref[pl.ds(base + t, 1), :] - g_ref[pl.ds(base, SUB), :]
    return jnp.exp(jnp.where(srow <= t, diff, NEG))


def _hgrn_fwd(proj, lb, gain, nb, lp, name):
    m = proj.shape[0]
    tb = SEQ_BLOCK
    nblk = lp // tb
    ns = tb // SUB

    def body(q_ref, f_ref, i_ref, g_ref, lb_ref, gain_ref, oraw_ref, y_ref, ssave_ref,
             st_ref, g_scr, kin_scr, o_scr):
        @pl.when(pl.program_id(2) == 0)
        def _():
            st_ref[...] = jnp.zeros_like(st_ref)

        ssave_ref[...] = st_ref[...]
        lbv = lb_ref[...]
        _, _, lf, kin = _hg_gates(f_ref[...], lbv)
        r, c, same = _chunk_masks()
        ltri = jnp.where(same & (c <= r), 1.0, 0.0).astype(F32)
        lall = jnp.where(same, 1.0, 0.0).astype(F32)
        g = _dot(ltri, lf, NN, HI)
        gt = _dot(lall, lf, NN, HI)
        g_scr[...] = g
        kin_scr[...] = kin
        qv = q_ref[...]
        qg = (qv * jnp.exp(g)).astype(BF16)
        kg = (kin * jnp.exp(gt - g)).astype(BF16)
        et = jnp.exp(gt)
        srow = lax.broadcasted_iota(jnp.int32, (SUB, 1), 0)
        for cc in range(ns):
            base = cc * SUB
            sl = slice(base, base + SUB)
            st = st_ref[...]
            o_c = _dot(qg[sl], st.astype(BF16), NT)
            kc = kin_scr[sl, :]
            vc = i_ref[sl, :]
            for t in range(SUB):
                e = _hg_intra_e(g_scr, base, t, srow)
                a = jnp.sum((q_ref[pl.ds(base + t, 1), :] * kc) * e, axis=1, keepdims=True)
                ot = jnp.sum(a * vc, axis=0, keepdims=True)
                o_c = o_c + jnp.where(srow == t, ot, 0.0)
            o_scr[sl, :] = o_c
            st_ref[...] = et[base:base + 1, :] * st + _dot(vc.astype(BF16), kg[sl], TN)
        o = o_scr[...]
        oraw_ref[...] = o
        rr = lax.rsqrt(jnp.mean(o * o, axis=-1, keepdims=True) + EPS)
        hg = g_ref[...]
        y_ref[...] = (((o * rr) * gain_ref[...]) * (hg * _sigmoid(hg))).astype(y_ref.dtype)

    col = lambda cb: pl.BlockSpec((tb, LANES), lambda b, h, i, cb=cb: (b * nblk + i, cb + h))
    out_blk = pl.BlockSpec((tb, LANES), lambda b, h, i: (b * nblk + i, h))
    return pl.pallas_call(
        body, name=name, grid=(nb, HG_HEADS, nblk),
        in_specs=[col(CB_HQ), col(CB_HF), col(CB_HI), col(CB_HG),
                  pl.BlockSpec((1, LANES), lambda b, h, i: (0, h)),
                  pl.BlockSpec((1, LANES), lambda b, h, i: (0, 0))],
        out_specs=[out_blk, out_blk,
                   pl.BlockSpec((None, None, None, HG_DIM, HG_DIM), lambda b, h, i: (b, h, i, 0, 0))],
        out_shape=[jax.ShapeDtypeStruct((m, HG_WIDTH), F32), jax.ShapeDtypeStruct((m, HG_WIDTH), BF16),
                   jax.ShapeDtypeStruct((nb, HG_HEADS, nblk, HG_DIM, HG_DIM), F32)],
        scratch_shapes=[pltpu.VMEM((HG_DIM, HG_DIM), F32), pltpu.VMEM((tb, LANES), F32),
                        pltpu.VMEM((tb, LANES), F32), pltpu.VMEM((tb, LANES), F32)],
        compiler_params=_cp("parallel", "parallel", "arbitrary"),
    )(proj, proj, proj, proj, lb, gain)


def _hgrn_bwd(proj, oraw, ssave, dy, lb, gain, nb, lp, name):
    m = proj.shape[0]
    tb = SEQ_BLOCK
    nblk = lp // tb
    ns = tb // SUB

    def body(q_ref, f_ref, i_ref, g_ref, oraw_ref, ssave_ref, dy_ref, lb_ref, gain_ref,
             dq_ref, df_ref, di_ref, dg_ref, dgain_ref, dlb_ref,
             dst_ref, sts_ref, g_scr, kin_scr, do_scr, dq_scr, dk_scr, dv_scr, dgg_scr):
        hd = pl.program_id(0)
        bb = pl.program_id(1)
        ii = pl.program_id(2)
        gainv = gain_ref[...]
        lbv = lb_ref[...]

        @pl.when((hd == 0) & (bb == 0) & (ii == 0))
        def _():
            dgain_ref[...] = jnp.zeros_like(dgain_ref)

        @pl.when((bb == 0) & (ii == 0))
        def _():
            dlb_ref[...] = jnp.zeros_like(dlb_ref)

        @pl.when(ii == 0)
        def _():
            dst_ref[...] = jnp.zeros_like(dst_ref)

        o = oraw_ref[...]
        rr = lax.rsqrt(jnp.mean(o * o, axis=-1, keepdims=True) + EPS)
        nv = o * rr
        hg = g_ref[...]
        sgg = _sigmoid(hg)
        sil = hg * sgg
        dyv = dy_ref[...]
        dg_ref[...] = (dyv * nv * gainv * (sgg * (1.0 + hg * (1.0 - sgg)))).astype(dg_ref.dtype)
        dgain_ref[...] += jnp.sum(dyv * nv * sil, axis=0, keepdims=True)
        dn = dyv * gainv * sil
        do_scr[...] = rr * (dn - nv * jnp.mean(dn * nv, axis=-1, keepdims=True))

        hf = f_ref[...]
        sg, f, lf, kin = _hg_gates(hf, lbv)
        r, c, same = _chunk_masks()
        ltri = jnp.where(same & (c <= r), 1.0, 0.0).astype(F32)
        lall = jnp.where(same, 1.0, 0.0).astype(F32)
        g = _dot(ltri, lf, NN, HI)
        gt = _dot(lall, lf, NN, HI)
        g_scr[...] = g
        kin_scr[...] = kin
        qv = q_ref[...]
        eg = jnp.exp(g)
        ekg = jnp.exp(gt - g)
        qg = qv * eg
        kg = kin * ekg
        qg16 = qg.astype(BF16)
        kg16 = kg.astype(BF16)
        et = jnp.exp(gt)
        st = ssave_ref[...]
        for cc in range(ns):
            sl = slice(cc * SUB, (cc + 1) * SUB)
            sts_ref[cc] = st
            st = et[cc * SUB:cc * SUB + 1, :] * st + _dot(i_ref[sl, :].astype(BF16), kg16[sl], TN)

        srow = lax.broadcasted_iota(jnp.int32, (SUB, 1), 0)
        for cc in reversed(range(ns)):
            base = cc * SUB
            sl = slice(base, base + SUB)
            st = sts_ref[cc]
            st16 = st.astype(BF16)
            dst = dst_ref[...]
            dst16 = dst.astype(BF16)
            doc = do_scr[sl, :]
            doc16 = doc.astype(BF16)
            vc = i_ref[sl, :]
            vc16 = vc.astype(BF16)
            kc = kin_scr[sl, :]
            etc = et[base:base + 1, :]
            dqg = _dot(doc16, st16, NN)
            dst_o = _dot(doc16, qg16[sl], TN)
            dv_c = _dot(kg16[sl], dst16, NT)
            dkg = _dot(vc16, dst16, NN)
            dgt = jnp.sum(dst * st, axis=0, keepdims=True) * etc
            dst_ref[...] = etc * dst + dst_o
            dq_c = dqg * eg[sl]
            dk_c = dkg * ekg[sl]
            dg_c = dqg * qg[sl] - dkg * kg[sl]
            dgt = dgt + jnp.sum(dkg * kg[sl], axis=0, keepdims=True)
            for t in range(SUB):
                e = _hg_intra_e(g_scr, base, t, srow)
                qt = q_ref[pl.ds(base + t, 1), :]
                dot_t = do_scr[pl.ds(base + t, 1), :]
                a = jnp.sum((qt * kc) * e, axis=1, keepdims=True)
                da = jnp.sum(dot_t * vc, axis=1, keepdims=True)
                dv_c = dv_c + a * dot_t
                w = da * e
                dq_t = jnp.sum(w * kc, axis=0, keepdims=True)
                wq = w * qt
                dk_c = dk_c + wq
                dg_c = dg_c - kc * wq + jnp.where(srow == t, qt * dq_t, 0.0)
                dq_c = dq_c + jnp.where(srow == t, dq_t, 0.0)
            dg_c = dg_c + jnp.where(srow == SUB - 1, dgt, 0.0)
            dq_scr[sl, :] = dq_c
            dk_scr[sl, :] = dk_c
            dv_scr[sl, :] = dv_c
            dgg_scr[sl, :] = dg_c

        utri = jnp.where(same & (c >= r), 1.0, 0.0).astype(F32)
        dlf = _dot(utri, dgg_scr[...], NN, HI)
        dkin = dk_scr[...]
        dsg = sg * (1.0 - sg)
        df_ref[...] = ((dlf / f - dkin) * ((1.0 - lbv) * dsg)).astype(df_ref.dtype)
        dlb_ref[...] += jnp.sum((dlf / f - dkin) * (1.0 - sg), axis=0, keepdims=True)
        dq_ref[...] = dq_scr[...].astype(dq_ref.dtype)
        di_ref[...] = dv_scr[...].astype(di_ref.dtype)

    rowi = lambda b, i: b * nblk + (nblk - 1 - i)
    col = lambda cb: pl.BlockSpec((tb, LANES), lambda h, b, i, cb=cb: (rowi(b, i), cb + h))
    hblk = pl.BlockSpec((tb, LANES), lambda h, b, i: (rowi(b, i), h))
    return pl.pallas_call(
        body, name=name, grid=(HG_HEADS, nb, nblk),
        in_specs=[col(CB_HQ), col(CB_HF), col(CB_HI), col(CB_HG), hblk,
                  pl.BlockSpec((None, None, None, HG_DIM, HG_DIM), lambda h, b, i: (b, h, nblk - 1 - i, 0, 0)),
                  hblk,
                  pl.BlockSpec((1, LANES), lambda h, b, i: (0, h)),
                  pl.BlockSpec((1, LANES), lambda h, b, i: (0, 0))],
        out_specs=[hblk, hblk, hblk, hblk,
                   pl.BlockSpec((1, LANES), lambda h, b, i: (0, 0)),
                   pl.BlockSpec((1, LANES), lambda h, b, i: (0, h))],
        out_shape=[jax.ShapeDtypeStruct((m, HG_WIDTH), BF16)] * 4
        + [jax.ShapeDtypeStruct((1, LANES), F32), jax.ShapeDtypeStruct((1, HG_WIDTH), F32)],
        scratch_shapes=[pltpu.VMEM((HG_DIM, HG_DIM), F32), pltpu.VMEM((ns, HG_DIM, HG_DIM), F32)]
        + [pltpu.VMEM((tb, LANES), F32)] * 7,
        compiler_params=_cp("arbitrary", "arbitrary", "arbitrary"),
    )(proj, proj, proj, proj, oraw, ssave, dy, lb, gain)


def _gate_fwd(proj, ya, yb, name):
    m = proj.shape[0]
    tm = _div_tile(m, 256, 16)

    def body(ga_ref, gb_ref, ya_ref, yb_ref, o_ref):
        o_ref[...] = (_sigmoid(ga_ref[...]) * ya_ref[...] + _sigmoid(gb_ref[...]) * yb_ref[...]).astype(o_ref.dtype)

    row = pl.BlockSpec((tm, D_MODEL), lambda i: (i, 0))
    return pl.pallas_call(
        body, name=name, grid=(m // tm,),
        in_specs=[row, pl.BlockSpec((tm, D_MODEL), lambda i: (i, 1)), row, row],
        out_specs=row, out_shape=jax.ShapeDtypeStruct((m, D_MODEL), BF16),
        compiler_params=_cp("parallel"),
    )(proj, proj, ya, yb)


def _gate_bwd(proj, ya, yb, dm, name):
    m = proj.shape[0]
    tm = _div_tile(m, 256, 16)

    def body(ga_ref, gb_ref, ya_ref, yb_ref, dm_ref, dya_ref, dyb_ref, dg_ref):
        dmv = dm_ref[...]
        sa = _sigmoid(ga_ref[...])
        sb = _sigmoid(gb_ref[...])
        dya_ref[...] = (dmv * sa).astype(BF16)
        dyb_ref[...] = (dmv * sb).astype(BF16)
        dg_ref[:, :D_MODEL] = (dmv * ya_ref[...] * (sa * (1.0 - sa))).astype(BF16)
        dg_ref[:, D_MODEL:] = (dmv * yb_ref[...] * (sb * (1.0 - sb))).astype(BF16)

    row = pl.BlockSpec((tm, D_MODEL), lambda i: (i, 0))
    wide = pl.BlockSpec((tm, 2 * D_MODEL), lambda i: (i, 0))
    return pl.pallas_call(
        body, name=name, grid=(m // tm,),
        in_specs=[row, pl.BlockSpec((tm, D_MODEL), lambda i: (i, 1)), row, row, row],
        out_specs=[row, row, wide],
        out_shape=[jax.ShapeDtypeStruct((m, D_MODEL), BF16)] * 2 + [jax.ShapeDtypeStruct((m, 2 * D_MODEL), BF16)],
        compiler_params=_cp("parallel"),
    )(proj, proj, ya, yb, dm)


CONV_ROWS = 128


def _conv3(x, xprev, w_ref, b_ref, rowi):
    r = x.shape[0]
    x1 = jnp.where(rowi < 1, pltpu.roll(xprev, 1, 0), pltpu.roll(x, 1, 0))
    x2 = jnp.where(rowi < 2, pltpu.roll(xprev, 2, 0), pltpu.roll(x, 2, 0))
    u = w_ref[0:1, :] * x2 + w_ref[1:2, :] * x1 + w_ref[2:3, :] * x + b_ref[...]
    return u, x1, x2


def _conv_fwd(up, cw, cb, nb, lp, name):
    m = up.shape[0]
    nct = D_FF // LANES
    r = CONV_ROWS
    nch = lp // r

    def body(g_ref, v_ref, wg_ref, wv_ref, bg_ref, bv_ref, o_ref):
        rowi = lax.broadcasted_iota(jnp.int32, (r, 1), 0)

        def step(i, carry):
            gp, vp = carry
            r0 = pl.multiple_of(i * r, r)
            xg = g_ref[pl.ds(r0, r), :]
            xv = v_ref[pl.ds(r0, r), :]
            ug, _, _ = _conv3(xg, gp, wg_ref, bg_ref, rowi)
            uv, _, _ = _conv3(xv, vp, wv_ref, bv_ref, rowi)
            o_ref[pl.ds(r0, r), :] = ((ug * _sigmoid(ug)) * uv).astype(o_ref.dtype)
            return xg, xv

        z = jnp.zeros((r, LANES), F32)
        lax.fori_loop(0, nch, step, (z, z))

    return pl.pallas_call(
        body, name=name, grid=(nb, nct),
        in_specs=[pl.BlockSpec((lp, LANES), lambda b, c: (b, c)),
                  pl.BlockSpec((lp, LANES), lambda b, c: (b, nct + c)),
                  pl.BlockSpec((CONV_WIDTH, LANES), lambda b, c: (0, c)),
                  pl.BlockSpec((CONV_WIDTH, LANES), lambda b, c: (0, nct + c)),
                  pl.BlockSpec((1, LANES), lambda b, c: (0, c)),
                  pl.BlockSpec((1, LANES), lambda b, c: (0, nct + c))],
        out_specs=pl.BlockSpec((lp, LANES), lambda b, c: (b, c)),
        out_shape=jax.ShapeDtypeStruct((m, D_FF), BF16),
        compiler_params=_cp("parallel", "parallel"),
    )(up, up, cw, cw, cb, cb)


def _conv_bwd(up, dact, cw, cb, nb, lp, name):
    m = up.shape[0]
    nct = D_FF // LANES
    r = CONV_ROWS
    nch = lp // r

    def body(g_ref, v_ref, da_ref, wg_ref, wv_ref, bg_ref, bv_ref, dup_ref, dw_ref, db_ref):
        is_gate = pl.program_id(0) < nct
        rowi = lax.broadcasted_iota(jnp.int32, (r, 1), 0)
        w_own = jnp.where(is_gate, wg_ref[...], wv_ref[...])

        def step(k, carry):
            dun, dw0, dw1, dw2, dbs = carry
            i = nch - 1 - k
            r0 = pl.multiple_of(i * r, r)
            rp = pl.multiple_of(jnp.maximum(i - 1, 0) * r, r)
            keep = (i > 0).astype(F32)
            xg = g_ref[pl.ds(r0, r), :]
            xv = v_ref[pl.ds(r0, r), :]
            gp = g_ref[pl.ds(rp, r), :] * keep
            vp = v_ref[pl.ds(rp, r), :] * keep
            ug, g1, g2 = _conv3(xg, gp, wg_ref, bg_ref, rowi)
            uv, v1, v2 = _conv3(xv, vp, wv_ref, bv_ref, rowi)
            da = da_ref[pl.ds(r0, r), :]
            sg = _sigmoid(ug)
            du = jnp.where(is_gate, da * uv * (sg * (1.0 + ug * (1.0 - sg))), da * (ug * sg))
            d1 = jnp.where(rowi >= r - 1, pltpu.roll(dun, r - 1, 0), pltpu.roll(du, r - 1, 0))
            d2 = jnp.where(rowi >= r - 2, pltpu.roll(dun, r - 2, 0), pltpu.roll(du, r - 2, 0))
            dup_ref[pl.ds(r0, r), :] = (w_own[2:3, :] * du + w_own[1:2, :] * d1 + w_own[0:1, :] * d2).astype(dup_ref.dtype)
            x0 = jnp.where(is_gate, xg, xv)
            x1 = jnp.where(is_gate, g1, v1)
            x2 = jnp.where(is_gate, g2, v2)
            dw0 = dw0 + jnp.sum(du * x2, axis=0, keepdims=True)
            dw1 = dw1 + jnp.sum(du * x1, axis=0, keepdims=True)
            dw2 = dw2 + jnp.sum(du * x0, axis=0, keepdims=True)
            dbs = dbs + jnp.sum(du, axis=0, keepdims=True)
            return du, dw0, dw1, dw2, dbs

        z1 = jnp.zeros((1, LANES), F32)
        _, dw0, dw1, dw2, dbs = lax.fori_loop(0, nch, step, (jnp.zeros((r, LANES), F32), z1, z1, z1, z1))

        @pl.when(pl.program_id(1) == 0)
        def _():
            dw_ref[...] = jnp.zeros_like(dw_ref)
            db_ref[...] = jnp.zeros_like(db_ref)

        dw_ref[0:1, :] += dw0
        dw_ref[1:2, :] += dw1
        dw_ref[2:3, :] += dw2
        db_ref[...] += dbs

    cg = lambda c: c % nct
    return pl.pallas_call(
        body, name=name, grid=(2 * nct, nb),
        in_specs=[pl.BlockSpec((lp, LANES), lambda c, b: (b, cg(c))),
                  pl.BlockSpec((lp, LANES), lambda c, b: (b, nct + cg(c))),
                  pl.BlockSpec((lp, LANES), lambda c, b: (b, cg(c))),
                  pl.BlockSpec((CONV_WIDTH, LANES), lambda c, b: (0, cg(c))),
                  pl.BlockSpec((CONV_WIDTH, LANES), lambda c, b: (0, nct + cg(c))),
                  pl.BlockSpec((1, LANES), lambda c, b: (0, cg(c))),
                  pl.BlockSpec((1, LANES), lambda c, b: (0, nct + cg(c)))],
        out_specs=[pl.BlockSpec((lp, LANES), lambda c, b: (b, c)),
                   pl.BlockSpec((CONV_WIDTH, LANES), lambda c, b: (0, c)),
                   pl.BlockSpec((1, LANES), lambda c, b: (0, c))],
        out_shape=[jax.ShapeDtypeStruct((m, 2 * D_FF), BF16),
                   jax.ShapeDtypeStruct((CONV_WIDTH, 2 * D_FF), F32),
                   jax.ShapeDtypeStruct((1, 2 * D_FF), F32)],
        compiler_params=_cp("parallel", "arbitrary"),
    )(up, up, dact, cw, cw, cb, cb)


def _loss_head(out, tgt, nb, lp, l, name):
    m, d = out.shape
    tr = SEQ_BLOCK
    nblk = lp // tr

    def body(o_ref, t_ref, dy_ref, ls_ref):
        t = pl.program_id(1) * tr + lax.broadcasted_iota(jnp.int32, (tr, 1), 0)
        valid = (t >= N_META) & (t < l)
        err = jnp.where(valid, o_ref[...] - t_ref[...], 0.0)
        dy_ref[...] = err * (1.0 / d)
        part = jnp.sum(err * err, axis=0, keepdims=True)
        first = (pl.program_id(0) == 0) & (pl.program_id(1) == 0)

        @pl.when(first)
        def _():
            ls_ref[...] = part

        @pl.when(jnp.logical_not(first))
        def _():
            ls_ref[...] += part

    row = pl.BlockSpec((tr, d), lambda b, i: (b * nblk + i, 0))
    return pl.pallas_call(
        body, name=name, grid=(nb, nblk),
        in_specs=[row, row], out_specs=[row, pl.BlockSpec((1, d), lambda b, i: (0, 0))],
        out_shape=[jax.ShapeDtypeStruct((m, d), F32), jax.ShapeDtypeStruct((1, d), F32)],
        compiler_params=_cp("arbitrary", "arbitrary"),
    )(out, tgt)


def _sum_adamw(recv, w, mom, var, name):
    rr = w.shape[0]
    tr = _div_tile(rr, 1024, 8)
    c1 = 1.0 - ADAM_B1 ** ADAM_STEP
    c2 = 1.0 - ADAM_B2 ** ADAM_STEP

    def body(r_ref, w_ref, m_ref, v_ref, g_ref, d_ref, mo_ref, vo_ref):
        g = r_ref[0]
        for s in range(1, N_DEV):
            g = g + r_ref[s]
        mn = ADAM_B1 * m_ref[...] + (1.0 - ADAM_B1) * g
        vn = ADAM_B2 * v_ref[...] + (1.0 - ADAM_B2) * (g * g)
        m_hat = mn / c1
        v_hat = vn / c2
        g_ref[...] = g
        d_ref[...] = -ADAM_LR * (m_hat / (jnp.sqrt(v_hat) + ADAM_EPS) + ADAM_WD * w_ref[...])
        mo_ref[...] = mn
        vo_ref[...] = vn

    row = pl.BlockSpec((tr, LANES), lambda i: (i, 0))
    return pl.pallas_call(
        body, name=name, grid=(rr // tr,),
        in_specs=[pl.BlockSpec((N_DEV, tr, LANES), lambda i: (0, i, 0)), row, row, row],
        out_specs=[row] * 4,
        out_shape=[jax.ShapeDtypeStruct((rr, LANES), F32)] * 4,
        compiler_params=_cp("parallel"),
    )(recv, w, mom, var)


_MESH = pl.DeviceIdType.MESH
_HBM = pl.BlockSpec(memory_space=pltpu.HBM)


def _position():
    return lax.axis_index("x"), lax.axis_index("y"), lax.axis_index("c")


def _all_gather(shard, name):
    def body(x_ref, out_ref, send_sems, recv_sems, local_sem):
        x, y, c = _position()
        me, sibling = (x, y, c), (x, y, 1 - c)
        chips = [(1 - x, y), (x, 1 - y), (1 - x, 1 - y)]

        def slot(px, py, pc):
            return out_ref.at[4 * px + 2 * py + pc]

        def copy(k, block, to, src=None):
            return pltpu.make_async_remote_copy(
                src_ref=slot(*block) if src is None else src, dst_ref=slot(*block),
                send_sem=send_sems.at[k], recv_sem=recv_sems.at[k], device_id=to, device_id_type=_MESH)

        mine = pltpu.make_async_copy(x_ref, slot(*me), local_sem)
        mine.start()
        first = [copy(0, me, sibling, src=x_ref)]
        first += [copy(1 + j, me, (*chip, c), src=x_ref) for j, chip in enumerate(chips)]
        for cp in first:
            cp.start()
        passed = [copy(4 + j, (*chip, c), sibling) for j, chip in enumerate(chips)]
        for j, chip in enumerate(chips):
            copy(1 + j, (*chip, c), me).wait_recv()
            passed[j].start()
        copy(0, sibling, me).wait_recv()
        for j, chip in enumerate(chips):
            copy(4 + j, (*chip, 1 - c), me).wait_recv()
        for cp in first + passed:
            cp.wait_send()
        mine.wait()

    return pl.pallas_call(
        body, name=name,
        out_shape=jax.ShapeDtypeStruct((N_DEV,) + shard.shape, shard.dtype),
        in_specs=[_HBM], out_specs=_HBM,
        scratch_shapes=[pltpu.SemaphoreType.DMA((7,)), pltpu.SemaphoreType.DMA((7,)), pltpu.SemaphoreType.DMA],
    )(shard)


def _exchange_blocks(blocks, name):
    flips = [(fx, fy, fc) for fx in (0, 1) for fy in (0, 1) for fc in (0, 1)][1:]

    def body(b_ref, out_ref, send_sems, recv_sems, local_sem):
        x, y, c = _position()
        me = 4 * x + 2 * y + c

        def peer(f):
            return (1 - x if f[0] else x, 1 - y if f[1] else y, 1 - c if f[2] else c)

        def idx(p):
            return 4 * p[0] + 2 * p[1] + p[2]

        mine = pltpu.make_async_copy(b_ref.at[me], out_ref.at[me], local_sem)
        mine.start()
        sends = []
        for k, f in enumerate(flips):
            p = peer(f)
            sends.append(pltpu.make_async_remote_copy(
                src_ref=b_ref.at[idx(p)], dst_ref=out_ref.at[me],
                send_sem=send_sems.at[k], recv_sem=recv_sems.at[k], device_id=p, device_id_type=_MESH))
        for cp in sends:
            cp.start()
        for k, f in enumerate(flips):
            p = peer(f)
            pltpu.make_async_remote_copy(
                src_ref=b_ref.at[me], dst_ref=out_ref.at[idx(p)],
                send_sem=send_sems.at[k], recv_sem=recv_sems.at[k], device_id=p, device_id_type=_MESH).wait_recv()
        for cp in sends:
            cp.wait_send()
        mine.wait()

    return pl.pallas_call(
        body, name=name,
        out_shape=jax.ShapeDtypeStruct(blocks.shape, blocks.dtype),
        in_specs=[_HBM], out_specs=_HBM,
        scratch_shapes=[pltpu.SemaphoreType.DMA((7,)), pltpu.SemaphoreType.DMA((7,)), pltpu.SemaphoreType.DMA],
    )(blocks)


def _pack(parts, rows):
    lead = parts[0].shape[:-1]
    flat = jnp.concatenate(parts, axis=-1)
    pad = rows * LANES - flat.shape[-1]
    flat = jnp.pad(flat, [(0, 0)] * len(lead) + [(0, pad)])
    return flat.reshape(lead + (rows, LANES))


def _unpack(packed, shapes):
    lead = packed.shape[:-2]
    flat = packed.reshape(lead + (-1,))
    out, off = [], 0
    for shp in shapes:
        n = int(np.prod(shp))
        out.append(flat[..., off:off + n].reshape(lead + tuple(shp)))
        off += n
    return out


def _rows_for(shapes):
    n = sum(int(np.prod(s)) for s in shapes)
    return -(-n // (8 * LANES)) * 8


def _lower_bound(logits):
    return jnp.cumsum(jax.nn.softmax(logits.astype(F32), axis=0), axis=0)[0:1]


def _local_step(x, target, meta, norm1_gain, w_in_al, fox_b_f, q_gain, k_gain, lb, hg_out_gain, w_a, w_b, w_out,
                norm2_gain, w_up, conv_w, conv_b, w_down):
    nb, seq, d = x.shape
    l = seq + N_META
    lp = -(-l // SEQ_BLOCK) * SEQ_BLOCK
    m = nb * lp

    h0 = jnp.concatenate([jnp.broadcast_to(meta[None], (nb, N_META, d)), x,
                          jnp.zeros((nb, lp - l, d), F32)], axis=1).reshape(m, d)
    tgt = jnp.pad(target, ((0, 0), (N_META, lp - l), (0, 0))).reshape(m, d)
    qg = jnp.tile(q_gain, (1, FOX_HEADS))
    kg = jnp.tile(k_gain, (1, FOX_HEADS))
    bf = jnp.pad(fox_b_f, ((0, 0), (0, LANES - FOX_HEADS)))

    xn = _rms_fwd(h0, norm1_gain, "rms1_fwd")
    proj = _matmul(xn, w_in_al, "nn", F32, "proj_in")
    qs, kn, vb, cum = _fox_prep(proj, qg, kg, bf, nb, lp, "fox_prep")
    cum_t = jnp.transpose(cum.reshape(nb, lp, LANES)[:, :, :FOX_HEADS], (0, 2, 1)).reshape(nb, FOX_HEADS, 1, lp)
    o_fox, lse = _fox_fwd(qs, kn, vb, cum, cum_t, nb, lp, "fox_fwd")
    o_raw, o_hg, s_save = _hgrn_fwd(proj, lb, hg_out_gain, nb, lp, "hgrn_fwd")
    ya = _matmul(o_hg, w_a, "nn", F32, "branch_a")
    yb = _matmul(o_fox, w_b, "nn", F32, "branch_b")
    merged = _gate_fwd(proj, ya, yb, "gate_fwd")
    h1 = _matmul(merged, w_out, "nn", F32, "mix_out", residual=h0)
    hn = _rms_fwd(h1, norm2_gain, "rms2_fwd")
    up = _matmul(hn, w_up, "nn", F32, "ffn_up")
    act = _conv_fwd(up, conv_w, conv_b, nb, lp, "conv_fwd")
    out = _matmul(act, w_down, "nn", F32, "ffn_down", residual=h1)
    dy, lsum = _loss_head(out, tgt, nb, lp, l, "loss_head")
    loss = (0.5 / d) * jnp.sum(lsum)

    dact = _matmul(dy, w_down, "nt", F32, "d_act")
    g_w_down = _matmul(act, dy, "tn", F32, "g_w_down")
    dup, g_conv_w, g_conv_b = _conv_bwd(up, dact, conv_w, conv_b, nb, lp, "conv_bwd")
    dhn = _matmul(dup, w_up, "nt", F32, "d_hn")
    g_w_up = _matmul(hn, dup, "tn", F32, "g_w_up")
    dh1, g_norm2 = _rms_bwd(h1, norm2_gain, dhn, dy, "rms2_bwd")

    dmerged = _matmul(dh1, w_out, "nt", F32, "d_merged")
    g_w_out = _matmul(merged, dh1, "tn", F32, "g_w_out")
    dya, dyb, dgab = _gate_bwd(proj, ya, yb, dmerged, "gate_bwd")
    do_hg = _matmul(dya, w_a, "nt", F32, "d_o_hg")
    g_w_a = _matmul(o_hg, dya, "tn", F32, "g_w_a")
    do_fox = _matmul(dyb, w_b, "nt", F32, "d_o_fox")
    g_w_b = _matmul(o_fox, dyb, "tn", F32, "g_w_b")
    dhq, dhf, dhi, dhg, g_hg_gain, g_lb = _hgrn_bwd(proj, o_raw, s_save, do_hg, lb, hg_out_gain, nb, lp, "hgrn_bwd")
    dqs, dkn, dvv, dc0, dc1 = _fox_bwd(qs, kn, vb, do_fox, o_fox, lse, cum, cum_t, nb, lp, "fox_bwd")
    dcum = jnp.stack([dc0, dc1], axis=2).reshape(nb, FOX_HEADS, lp)
    dcum = jnp.pad(jnp.transpose(dcum, (0, 2, 1)), ((0, 0), (0, 0), (0, LANES - FOX_HEADS))).reshape(m, LANES)
    dfqkv, dff, g_qg, g_kg, g_bf = _fox_prep_bwd(proj, dqs, dkn, dvv, dcum, qg, kg, bf, nb, lp, "fox_prep_bwd")
    dproj = jnp.concatenate([dgab, dfqkv, dhq, dhf, dhi, dhg, dff], axis=1)
    dxn = _matmul(dproj, w_in_al, "nt", F32, "d_xn")
    g_w_in_al = _matmul(xn, dproj, "tn", F32, "g_w_in")
    dh0, g_norm1 = _rms_bwd(h0, norm1_gain, dxn, dh1, "rms1_bwd")

    dh0 = dh0.reshape(nb, lp, d)
    grad_x = dh0[:, N_META:l]
    g_meta = jnp.sum(dh0[:, :N_META], axis=0)
    g_q_gain = jnp.sum(g_qg.reshape(FOX_HEADS, FOX_HEAD_DIM), axis=0, keepdims=True)
    g_k_gain = jnp.sum(g_kg.reshape(FOX_HEADS, FOX_HEAD_DIM), axis=0, keepdims=True)
    grads = dict(meta_tokens=g_meta, norm1_gain=g_norm1, w_in_al=g_w_in_al, fox_b_f=g_bf[:, :FOX_HEADS],
                 q_norm_gain=g_q_gain, k_norm_gain=g_k_gain, lb=g_lb, hg_out_gain=g_hg_gain,
                 w_branch_a=g_w_a, w_branch_b=g_w_b, w_out=g_w_out, norm2_gain=g_norm2, w_up=g_w_up,
                 conv_w=g_conv_w, conv_b=g_conv_b, w_down=g_w_down)
    return loss, grad_x, grads


SHARDED = ("w_in", "w_branch_a", "w_branch_b", "w_out", "w_up", "conv_w", "w_down", "meta_tokens")
SMALL = ("norm1_gain", "fox_b_f", "q_norm_gain", "k_norm_gain", "hg_lb_logits", "hg_out_gain", "norm2_gain", "conv_b")
ORDER = ("meta_tokens", "norm1_gain", "w_in", "fox_b_f", "q_norm_gain", "k_norm_gain", "hg_lb_logits", "hg_out_gain",
         "w_branch_a", "w_branch_b", "w_out", "norm2_gain", "w_up", "conv_w", "conv_b", "w_down")


def _from_slots(g, kind):
    if kind == "cols":
        return jnp.transpose(g, (1, 0, 2)).reshape(g.shape[1], -1)
    return g.reshape((-1,) + g.shape[2:])


def _to_slots(full, kind):
    if kind == "cols":
        r = full.shape[0]
        return jnp.transpose(full.reshape(r, N_DEV, -1), (1, 0, 2))
    return full.reshape((N_DEV, -1) + full.shape[1:])


KIND = dict(w_in="cols", w_branch_a="cols", w_branch_b="cols", w_out="rows", w_up="cols", conv_w="cols",
            w_down="rows", meta_tokens="cols")


def kernel(x, meta_tokens, norm1_gain, w_in, fox_b_f, q_norm_gain, k_norm_gain, hg_lb_logits, hg_out_gain, w_branch_a, w_branch_b, w_out, norm2_gain, w_up, conv_w, conv_b, w_down, loss_target, m_meta_tokens, m_norm1_gain, m_w_in, m_fox_b_f, m_q_norm_gain, m_k_norm_gain, m_hg_lb_logits, m_hg_out_gain, m_w_branch_a, m_w_branch_b, m_w_out, m_norm2_gain, m_w_up, m_conv_w, m_conv_b, m_w_down, v_meta_tokens, v_norm1_gain, v_w_in, v_fox_b_f, v_q_norm_gain, v_k_norm_gain, v_hg_lb_logits, v_hg_out_gain, v_w_branch_a, v_w_branch_b, v_w_out, v_norm2_gain, v_w_up, v_conv_w, v_conv_b, v_w_down):
    w = dict(meta_tokens=meta_tokens, norm1_gain=norm1_gain, w_in=w_in, fox_b_f=fox_b_f, q_norm_gain=q_norm_gain,
             k_norm_gain=k_norm_gain, hg_lb_logits=hg_lb_logits, hg_out_gain=hg_out_gain, w_branch_a=w_branch_a,
             w_branch_b=w_branch_b, w_out=w_out, norm2_gain=norm2_gain, w_up=w_up, conv_w=conv_w, conv_b=conv_b,
             w_down=w_down)
    mom = dict(meta_tokens=m_meta_tokens, norm1_gain=m_norm1_gain, w_in=m_w_in, fox_b_f=m_fox_b_f,
               q_norm_gain=m_q_norm_gain, k_norm_gain=m_k_norm_gain, hg_lb_logits=m_hg_lb_logits,
               hg_out_gain=m_hg_out_gain, w_branch_a=m_w_branch_a, w_branch_b=m_w_branch_b, w_out=m_w_out,
               norm2_gain=m_norm2_gain, w_up=m_w_up, conv_w=m_conv_w, conv_b=m_conv_b, w_down=m_w_down)
    var = dict(meta_tokens=v_meta_tokens, norm1_gain=v_norm1_gain, w_in=v_w_in, fox_b_f=v_fox_b_f,
               q_norm_gain=v_q_norm_gain, k_norm_gain=v_k_norm_gain, hg_lb_logits=v_hg_lb_logits,
               hg_out_gain=v_hg_out_gain, w_branch_a=v_w_branch_a, w_branch_b=v_w_branch_b, w_out=v_w_out,
               norm2_gain=v_norm2_gain, w_up=v_w_up, conv_w=v_conv_w, conv_b=v_conv_b, w_down=v_w_down)

    def shard2d(name, a):
        return a if name == "meta_tokens" else a[0]

    shard_shapes = [shard2d(n, w[n]).shape for n in SHARDED]
    small_shapes = [w[n].shape for n in SMALL]
    rows_sh = _rows_for(shard_shapes)
    rows_sm = _rows_for(small_shapes)

    def pack_local(tree):
        sh = _pack([shard2d(n, tree[n]).reshape(-1) for n in SHARDED], rows_sh)
        sm = _pack([tree[n].reshape(-1) for n in SMALL], rows_sm)
        return sh, sm

    w_sh, w_sm = pack_local(w)
    gathered = _all_gather(w_sh, "gather_weights")
    full = {n: _from_slots(g, KIND[n]) for n, g in zip(SHARDED, _unpack(gathered, shard_shapes))}

    lb, lb_vjp = jax.vjp(_lower_bound, hg_lb_logits)
    loss, grad_x, g = _local_step(
        x, loss_target, full["meta_tokens"], norm1_gain, _align_cols(full["w_in"]).astype(BF16), fox_b_f,
        q_norm_gain, k_norm_gain, lb, hg_out_gain, full["w_branch_a"].astype(BF16), full["w_branch_b"].astype(BF16),
        full["w_out"].astype(BF16), norm2_gain, full["w_up"].astype(BF16), full["conv_w"], conv_b,
        full["w_down"].astype(BF16))
    loss = lax.psum(loss, ("x", "y", "c"))

    g["w_in"] = _unalign_cols(g.pop("w_in_al"))
    g["hg_lb_logits"] = lb_vjp(g.pop("lb"))[0]
    g_sh = _pack([_to_slots(g[n], KIND[n]).reshape(N_DEV, -1) for n in SHARDED], rows_sh)
    g_sm = _pack([g[n].reshape(-1) for n in SMALL], rows_sm)
    blocks = jnp.concatenate([g_sh, jnp.broadcast_to(g_sm[None], (N_DEV,) + g_sm.shape)], axis=1)
    recv = _exchange_blocks(blocks, "exchange_grads")

    m_sh, m_sm = pack_local(mom)
    v_sh, v_sm = pack_local(var)
    cat = lambda a, b: jnp.concatenate([a, b], axis=0)
    outs = _sum_adamw(recv, cat(w_sh, w_sm), cat(m_sh, m_sm), cat(v_sh, v_sm), "sum_adamw")

    result = []
    for o in outs:
        sh = dict(zip(SHARDED, _unpack(o[:rows_sh], shard_shapes)))
        sm = dict(zip(SMALL, _unpack(o[rows_sh:], small_shapes)))
        tree = {**sh, **sm}
        result.append([tree[n].reshape(w[n].shape) for n in ORDER])
    return (loss, grad_x, *result[0], *result[1], *result[2], *result[3])
```

```python
import jax
import jax.numpy as jnp
import numpy as np
from jax import lax
from jax.experimental import pallas as pl
from jax.experimental.pallas import tpu as pltpu

F32 = jnp.float32
BF16 = jnp.bfloat16

D_MODEL = 1024
N_META = 16
FOX_HEADS = 8
FOX_HEAD_DIM = 64
FOX_WIDTH = FOX_HEADS * FOX_HEAD_DIM
HG_HEADS = 4
HG_DIM = 128
HG_WIDTH = HG_HEADS * HG_DIM
D_FF = 2816
CONV_WIDTH = 3
EPS = 1e-6
IN_COLS = 3 * FOX_WIDTH + FOX_HEADS + 4 * HG_WIDTH + 2 * D_MODEL
N_DEV = 8

ADAM_LR = 0.001
ADAM_B1 = 0.9
ADAM_B2 = 0.999
ADAM_EPS = 1e-08
ADAM_WD = 0.01
ADAM_STEP = 10

LANES = 128
SEQ_BLOCK = 128
SUB = 16
NEG = -1e30
VMEM_LIMIT = 48 * 1024 * 1024

FOX_CB = 2 * D_MODEL // FOX_WIDTH
CB_HQ = (2 * D_MODEL + 3 * FOX_WIDTH) // LANES
CB_HF = CB_HQ + HG_HEADS
CB_HI = CB_HF + HG_HEADS
CB_HG = CB_HI + HG_HEADS
CB_FF = CB_HG + HG_HEADS


def _div_tile(n, target, mult):
    best = None
    for t in range(mult, min(n, target) + 1, mult):
        if n % t == 0:
            best = t
    if best is None:
        best = n
    return best


def _cp(*sem):
    return pltpu.CompilerParams(dimension_semantics=sem, vmem_limit_bytes=VMEM_LIMIT)


def _sigmoid(x):
    return 1.0 / (1.0 + jnp.exp(-x))


def _dot(a, b, dims, precision=None):
    return lax.dot_general(a, b, (dims, ((), ())), preferred_element_type=F32, precision=precision)


NN = ((1,), (0,))
NT = ((1,), (1,))
TN = ((0,), (0,))
HI = lax.Precision.HIGHEST


def _matmul(a, b, mode, out_dtype, name, residual=None, tm_t=1024, tn_t=1024, tk_t=1024):
    if mode == "nn":
        (m, k), (k2, n) = a.shape, b.shape
    elif mode == "nt":
        (m, k), (n, k2) = a.shape, b.shape
    else:
        (k, m), (k2, n) = a.shape, b.shape
    assert k == k2, (a.shape, b.shape, mode)
    tm = _div_tile(m, tm_t, LANES)
    tn = _div_tile(n, tn_t, LANES)
    tk = _div_tile(k, tk_t, LANES)
    nk = k // tk
    if mode == "nn":
        a_spec = pl.BlockSpec((tm, tk), lambda i, j, kk: (i, kk))
        b_spec = pl.BlockSpec((tk, tn), lambda i, j, kk: (kk, j))
        dims = NN
    elif mode == "nt":
        a_spec = pl.BlockSpec((tm, tk), lambda i, j, kk: (i, kk))
        b_spec = pl.BlockSpec((tn, tk), lambda i, j, kk: (j, kk))
        dims = NT
    else:
        a_spec = pl.BlockSpec((tk, tm), lambda i, j, kk: (kk, i))
        b_spec = pl.BlockSpec((tk, tn), lambda i, j, kk: (kk, j))
        dims = TN
    o_spec = pl.BlockSpec((tm, tn), lambda i, j, kk: (i, j))
    has_res = residual is not None

    def body(*refs):
        if has_res:
            a_ref, b_ref, r_ref, o_ref, acc_ref = refs
        else:
            a_ref, b_ref, o_ref, acc_ref = refs
        kk = pl.program_id(2)
        part = _dot(a_ref[...].astype(BF16), b_ref[...].astype(BF16), dims)

        @pl.when(kk == 0)
        def _():
            acc_ref[...] = part

        @pl.when(kk > 0)
        def _():
            acc_ref[...] += part

        @pl.when(kk == nk - 1)
        def _():
            acc = acc_ref[...]
            if has_res:
                acc = acc + r_ref[...]
            o_ref[...] = acc.astype(o_ref.dtype)

    in_specs = [a_spec, b_spec] + ([o_spec] if has_res else [])
    args = (a, b) + ((residual,) if has_res else ())
    return pl.pallas_call(
        body, name=name, grid=(m // tm, n // tn, nk),
        in_specs=in_specs, out_specs=o_spec,
        out_shape=jax.ShapeDtypeStruct((m, n), out_dtype),
        scratch_shapes=[pltpu.VMEM((tm, tn), F32)],
        compiler_params=_cp("parallel", "parallel", "arbitrary"),
    )(*args)


def _rms_fwd(x, gain, name):
    m, d = x.shape
    tm = _div_tile(m, 512, 16)

    def body(x_ref, g_ref, o_ref):
        xv = x_ref[...]
        r = lax.rsqrt(jnp.mean(xv * xv, axis=-1, keepdims=True) + EPS)
        o_ref[...] = ((xv * r) * g_ref[...]).astype(o_ref.dtype)

    return pl.pallas_call(
        body, name=name, grid=(m // tm,),
        in_specs=[pl.BlockSpec((tm, d), lambda i: (i, 0)), pl.BlockSpec((1, d), lambda i: (0, 0))],
        out_specs=pl.BlockSpec((tm, d), lambda i: (i, 0)),
        out_shape=jax.ShapeDtypeStruct((m, d), BF16),
        compiler_params=_cp("parallel"),
    )(x, gain)


def _rms_bwd(x, gain, dy, dres, name):
    m, d = x.shape
    tm = _div_tile(m, 256, 8)

    def body(x_ref, g_ref, dy_ref, dr_ref, dx_ref, dg_ref):
        xv = x_ref[...]
        r = lax.rsqrt(jnp.mean(xv * xv, axis=-1, keepdims=True) + EPS)
        nv = xv * r
        dyv = dy_ref[...]
        gdy = dyv * g_ref[...]
        dx_ref[...] = dr_ref[...] + r * (gdy - nv * jnp.mean(gdy * nv, axis=-1, keepdims=True))
        part = jnp.sum(dyv * nv, axis=0, keepdims=True)

        @pl.when(pl.program_id(0) == 0)
        def _():
            dg_ref[...] = part

        @pl.when(pl.program_id(0) > 0)
        def _():
            dg_ref[...] += part

    row = pl.BlockSpec((tm, d), lambda i: (i, 0))
    vec = pl.BlockSpec((1, d), lambda i: (0, 0))
    return pl.pallas_call(
        body, name=name, grid=(m // tm,),
        in_specs=[row, vec, row, row], out_specs=[row, vec],
        out_shape=[jax.ShapeDtypeStruct((m, d), F32), jax.ShapeDtypeStruct((1, d), F32)],
        compiler_params=_cp("arbitrary"),
    )(x, gain, dy, dres)


def _head_stats(xv, lo):
    sq = xv * xv
    s_lo = jnp.sum(jnp.where(lo, sq, 0.0), axis=1, keepdims=True)
    s_hi = jnp.sum(jnp.where(lo, 0.0, sq), axis=1, keepdims=True)
    return jnp.where(lo, s_lo, s_hi) * (1.0 / FOX_HEAD_DIM)


def _fox_prep(proj, qg, kg, bf, nb, lp, name):
    m = proj.shape[0]
    ts = SEQ_BLOCK
    nblk = lp // ts
    scale = FOX_HEAD_DIM ** -0.5

    def body(q_ref, k_ref, v_ref, f_ref, qg_ref, kg_ref, bf_ref, qo_ref, ko_ref, vo_ref, cum_ref, carry_ref):
        lane = lax.broadcasted_iota(jnp.int32, (1, LANES), 1)
        lo = lane < FOX_HEAD_DIM
        for j in range(FOX_WIDTH // LANES):
            cs = slice(j * LANES, (j + 1) * LANES)
            xq = q_ref[:, cs]
            rq = lax.rsqrt(_head_stats(xq, lo) + EPS)
            qo_ref[:, cs] = (((xq * rq) * qg_ref[:, cs]) * scale).astype(BF16)
            xk = k_ref[:, cs]
            rk = lax.rsqrt(_head_stats(xk, lo) + EPS)
            ko_ref[:, cs] = ((xk * rk) * kg_ref[:, cs]).astype(BF16)
        vo_ref[...] = v_ref[...].astype(BF16)

        @pl.when(pl.program_id(1) == 0)
        def _():
            carry_ref[...] = jnp.zeros_like(carry_ref)

        z = f_ref[...] + bf_ref[...]
        logf = jnp.minimum(z, 0.0) - jnp.log(1.0 + jnp.exp(-jnp.abs(z)))
        logf = jnp.where(lane < FOX_HEADS, logf, 0.0)
        r = lax.broadcasted_iota(jnp.int32, (ts, ts), 0)
        c = lax.broadcasted_iota(jnp.int32, (ts, ts), 1)
        tri = jnp.where(c <= r, 1.0, 0.0).astype(F32)
        cum = _dot(tri, logf, NN, HI) + carry_ref[...]
        cum_ref[...] = cum
        carry_ref[...] = cum_ref[ts - 1:ts, :]

    w = FOX_WIDTH
    row = lambda b, i: (b * nblk + i, 0)
    return pl.pallas_call(
        body, name=name, grid=(nb, nblk),
        in_specs=[pl.BlockSpec((ts, w), lambda b, i: (b * nblk + i, FOX_CB)),
                  pl.BlockSpec((ts, w), lambda b, i: (b * nblk + i, FOX_CB + 1)),
                  pl.BlockSpec((ts, w), lambda b, i: (b * nblk + i, FOX_CB + 2)),
                  pl.BlockSpec((ts, LANES), lambda b, i: (b * nblk + i, CB_FF)),
                  pl.BlockSpec((1, w), lambda b, i: (0, 0)),
                  pl.BlockSpec((1, w), lambda b, i: (0, 0)),
                  pl.BlockSpec((1, LANES), lambda b, i: (0, 0))],
        out_specs=[pl.BlockSpec((ts, w), row), pl.BlockSpec((ts, w), row), pl.BlockSpec((ts, w), row),
                   pl.BlockSpec((ts, LANES), row)],
        out_shape=[jax.ShapeDtypeStruct((m, w), BF16)] * 3 + [jax.ShapeDtypeStruct((m, LANES), F32)],
        scratch_shapes=[pltpu.VMEM((1, LANES), F32)],
        compiler_params=_cp("arbitrary", "arbitrary"),
    )(proj, proj, proj, proj, qg, kg, bf)


def _att_tile(lp):
    return 384 if (lp % 384 == 0 and lp > 384) else 128


def _lane_pick(blk, lane, idx):
    return jnp.sum(jnp.where(lane == idx, blk, 0.0), axis=1, keepdims=True)


def _fox_fwd(qs, kn, vb, cum, cum_t, nb, lp, name):
    m = qs.shape[0]
    tq = _att_tile(lp)
    nq = lp // tq
    npair = FOX_WIDTH // LANES

    def body(q_ref, k_ref, v_ref, cum_ref, ck0_ref, ck1_ref, o_ref, lse_ref):
        p = pl.program_id(1)
        qi = pl.program_id(2)
        lane = lax.broadcasted_iota(jnp.int32, (1, LANES), 1)
        q = q_ref[...]
        cumblk = cum_ref[...]
        rows = qi * tq + lax.broadcasted_iota(jnp.int32, (tq, 1), 0)
        o_tot = jnp.zeros((tq, LANES), F32)
        lse_out = jnp.zeros((tq, LANES), F32)
        for hh in range(2):
            hmask = (lane >= hh * FOX_HEAD_DIM) & (lane < (hh + 1) * FOX_HEAD_DIM)
            cq = _lane_pick(cumblk, lane, 2 * p + hh)
            ck_ref = ck0_ref if hh == 0 else ck1_ref

            def step(j, carry, hmask=hmask, cq=cq, ck_ref=ck_ref):
                mx, l, acc = carry
                k0 = pl.multiple_of(j * tq, tq)
                kz = jnp.where(hmask, k_ref[pl.ds(k0, tq), :], jnp.zeros((), BF16))
                vz = jnp.where(hmask, v_ref[pl.ds(k0, tq), :], jnp.zeros((), BF16))
                s = _dot(q, kz, NT) + cq - ck_ref[:, pl.ds(k0, tq)]
                cols = k0 + lax.broadcasted_iota(jnp.int32, (1, tq), 1)
                s = jnp.where(rows >= cols, s, NEG)
                m_new = jnp.maximum(mx, jnp.max(s, axis=1, keepdims=True))
                alpha = jnp.exp(mx - m_new)
                pe = jnp.exp(s - m_new)
                l = alpha * l + jnp.sum(pe, axis=1, keepdims=True)
                acc = alpha * acc + _dot(pe.astype(BF16), vz, NN)
                return m_new, l, acc

            init = (jnp.full((tq, 1), NEG, F32), jnp.zeros((tq, 1), F32), jnp.zeros((tq, LANES), F32))
            mx, l, acc = lax.fori_loop(0, qi + 1, step, init)
            o_tot = o_tot + acc / l
            lse_out = jnp.where(lane == hh, mx + jnp.log(l), lse_out)
        o_ref[...] = o_tot
        lse_ref[...] = lse_out

    return pl.pallas_call(
        body, name=name, grid=(nb, npair, nq),
        in_specs=[pl.BlockSpec((tq, LANES), lambda b, p, i: (b * nq + i, p)),
                  pl.BlockSpec((lp, LANES), lambda b, p, i: (b, p)),
                  pl.BlockSpec((lp, LANES), lambda b, p, i: (b, p)),
                  pl.BlockSpec((tq, LANES), lambda b, p, i: (b * nq + i, 0)),
                  pl.BlockSpec((None, None, 1, lp), lambda b, p, i: (b, 2 * p, 0, 0)),
                  pl.BlockSpec((None, None, 1, lp), lambda b, p, i: (b, 2 * p + 1, 0, 0))],
        out_specs=[pl.BlockSpec((tq, LANES), lambda b, p, i: (b * nq + i, p)),
                   pl.BlockSpec((None, None, tq, LANES), lambda b, p, i: (b, p, i, 0))],
        out_shape=[jax.ShapeDtypeStruct((m, FOX_WIDTH), F32),
                   jax.ShapeDtypeStruct((nb, npair, lp, LANES), F32)],
        compiler_params=_cp("parallel", "parallel", "arbitrary"),
    )(qs, kn, vb, cum, cum_t, cum_t)


def _fox_bwd(qs, kn, vb, do, o, lse, cum, cum_t, nb, lp, name):
    m = qs.shape[0]
    tq = _att_tile(lp)
    nq = lp // tq
    npair = FOX_WIDTH // LANES

    def body(k_ref, v_ref, q_ref, do_ref, o_ref, lse_ref, cum_ref, ck0_ref, ck1_ref,
             dq_ref, dk_ref, dv_ref, dc0_ref, dc1_ref):
        p = pl.program_id(1)
        j = pl.program_id(2)
        lane = lax.broadcasted_iota(jnp.int32, (1, LANES), 1)

        @pl.when(j == 0)
        def _():
            dq_ref[...] = jnp.zeros_like(dq_ref)

        kb = k_ref[...]
        vv = v_ref[...]
        cols = j * tq + lax.broadcasted_iota(jnp.int32, (1, tq), 1)
        hmasks = [(lane >= hh * FOX_HEAD_DIM) & (lane < (hh + 1) * FOX_HEAD_DIM) for hh in range(2)]
        zero16 = jnp.zeros((), BF16)
        kzs = [jnp.where(hm, kb, zero16) for hm in hmasks]
        vzs = [jnp.where(hm, vv, zero16) for hm in hmasks]
        cks = [ck0_ref[...], ck1_ref[...]]

        def step(qi, carry):
            dk, dv, dc0, dc1 = carry
            q0 = pl.multiple_of(qi * tq, tq)
            q = q_ref[pl.ds(q0, tq), :]
            dob = do_ref[pl.ds(q0, tq), :]
            ob = o_ref[pl.ds(q0, tq), :]
            lseb = lse_ref[pl.ds(q0, tq), :]
            cumb = cum_ref[pl.ds(q0, tq), :]
            rows = q0 + lax.broadcasted_iota(jnp.int32, (tq, 1), 0)
            mask = rows >= cols
            dq_acc = jnp.zeros((tq, LANES), F32)
            dcs = [dc0, dc1]
            for hh in range(2):
                doz = jnp.where(hmasks[hh], dob, 0.0)
                delta = jnp.sum(doz * ob, axis=1, keepdims=True)
                lse_h = _lane_pick(lseb, lane, hh)
                cq = _lane_pick(cumb, lane, 2 * p + hh)
                s = _dot(q, kzs[hh], NT) + cq - cks[hh]
                pm = jnp.exp(jnp.where(mask, s - lse_h, NEG))
                doz16 = doz.astype(BF16)
                dp = _dot(doz16, vzs[hh], NT)
                ds = pm * (dp - delta)
                ds16 = ds.astype(BF16)
                dv = dv + _dot(pm.astype(BF16), doz16, TN)
                dk = dk + _dot(ds16, jnp.where(hmasks[hh], q, zero16), TN)
                dq_acc = dq_acc + _dot(ds16, kzs[hh], NN)
                dcs[hh] = dcs[hh] - jnp.sum(ds, axis=0, keepdims=True)
            dq_ref[pl.ds(q0, tq), :] += dq_acc
            return dk, dv, dcs[0], dcs[1]

        init = (jnp.zeros((tq, LANES), F32), jnp.zeros((tq, LANES), F32),
                jnp.zeros((1, tq), F32), jnp.zeros((1, tq), F32))
        dk, dv, dc0, dc1 = lax.fori_loop(j, nq, step, init)
        dk_ref[...] = dk
        dv_ref[...] = dv
        dc0_ref[...] = dc0
        dc1_ref[...] = dc1

    full = pl.BlockSpec((lp, LANES), lambda b, p, j: (b, p))
    blk = pl.BlockSpec((tq, LANES), lambda b, p, j: (b * nq + j, p))
    return pl.pallas_call(
        body, name=name, grid=(nb, npair, nq),
        in_specs=[blk, blk, full, full, full,
                  pl.BlockSpec((None, None, lp, LANES), lambda b, p, j: (b, p, 0, 0)),
                  pl.BlockSpec((lp, LANES), lambda b, p, j: (b, 0)),
                  pl.BlockSpec((None, None, 1, tq), lambda b, p, j: (b, 2 * p, 0, j)),
                  pl.BlockSpec((None, None, 1, tq), lambda b, p, j: (b, 2 * p + 1, 0, j))],
        out_specs=[full, blk, blk,
                   pl.BlockSpec((None, None, 1, tq), lambda b, p, j: (b, p, 0, j)),
                   pl.BlockSpec((None, None, 1, tq), lambda b, p, j: (b, p, 0, j))],
        out_shape=[jax.ShapeDtypeStruct((m, FOX_WIDTH), F32)] * 3
        + [jax.ShapeDtypeStruct((nb, npair, 1, lp), F32)] * 2,
        compiler_params=_cp("parallel", "parallel", "arbitrary"),
    )(kn, vb, qs, do, o, lse, cum, cum_t, cum_t)


def _fox_prep_bwd(proj, dqs, dkn, dv, dcum, qg, kg, bf, nb, lp, name):
    m = proj.shape[0]
    ts = SEQ_BLOCK
    nblk = lp // ts
    scale = FOX_HEAD_DIM ** -0.5
    w = FOX_WIDTH
    wo = 3 * w

    def body(q_ref, k_ref, f_ref, dq_ref, dk_ref, dv_ref, dc_ref, qg_ref, kg_ref, bf_ref,
             out_ref, dff_ref, dqg_ref, dkg_ref, dbf_ref, carry_ref):
        first = (pl.program_id(0) == 0) & (pl.program_id(1) == 0)
        lane = lax.broadcasted_iota(jnp.int32, (1, LANES), 1)
        lo = lane < FOX_HEAD_DIM

        @pl.when(first)
        def _():
            dqg_ref[...] = jnp.zeros_like(dqg_ref)
            dkg_ref[...] = jnp.zeros_like(dkg_ref)
            dbf_ref[...] = jnp.zeros_like(dbf_ref)

        def norm_bwd(x, g, dy):
            r = lax.rsqrt(_head_stats(x, lo) + EPS)
            nv = x * r
            gdy = dy * g
            prod = gdy * nv
            s_lo = jnp.sum(jnp.where(lo, prod, 0.0), axis=1, keepdims=True)
            s_hi = jnp.sum(jnp.where(lo, 0.0, prod), axis=1, keepdims=True)
            mean = jnp.where(lo, s_lo, s_hi) * (1.0 / FOX_HEAD_DIM)
            return r * (gdy - nv * mean), jnp.sum(dy * nv, axis=0, keepdims=True)

        for jj in range(w // LANES):
            cs = slice(jj * LANES, (jj + 1) * LANES)
            dx, dg = norm_bwd(q_ref[:, cs], qg_ref[:, cs], dq_ref[:, cs] * scale)
            out_ref[:, cs] = dx.astype(BF16)
            dqg_ref[:, cs] += dg
            dx, dg = norm_bwd(k_ref[:, cs], kg_ref[:, cs], dk_ref[:, cs])
            out_ref[:, w + jj * LANES:w + (jj + 1) * LANES] = dx.astype(BF16)
            dkg_ref[:, cs] += dg
        out_ref[:, 2 * w:3 * w] = dv_ref[...].astype(BF16)

        @pl.when(pl.program_id(1) == 0)
        def _():
            carry_ref[...] = jnp.zeros_like(carry_ref)

        dc = dc_ref[...]
        r = lax.broadcasted_iota(jnp.int32, (ts, ts), 0)
        c = lax.broadcasted_iota(jnp.int32, (ts, ts), 1)
        triu = jnp.where(c >= r, 1.0, 0.0).astype(F32)
        dlogf = _dot(triu, dc, NN, HI) + carry_ref[...]
        carry_ref[...] += jnp.sum(dc, axis=0, keepdims=True)
        z = f_ref[...] + bf_ref[...]
        dz = jnp.where(lane < FOX_HEADS, dlogf * _sigmoid(-z), 0.0)
        dff_ref[...] = dz.astype(BF16)
        dbf_ref[...] += jnp.sum(dz, axis=0, keepdims=True)

    rev = lambda b, i: (b * nblk + (nblk - 1 - i), 0)
    vec = lambda n: pl.BlockSpec((1, n), lambda b, i: (0, 0))
    return pl.pallas_call(
        body, name=name, grid=(nb, nblk),
        in_specs=[pl.BlockSpec((ts, w), lambda b, i: (b * nblk + (nblk - 1 - i), FOX_CB)),
                  pl.BlockSpec((ts, w), lambda b, i: (b * nblk + (nblk - 1 - i), FOX_CB + 1)),
                  pl.BlockSpec((ts, LANES), lambda b, i: (b * nblk + (nblk - 1 - i), CB_FF)),
                  pl.BlockSpec((ts, w), rev), pl.BlockSpec((ts, w), rev), pl.BlockSpec((ts, w), rev),
                  pl.BlockSpec((ts, LANES), rev), vec(w), vec(w), vec(LANES)],
        out_specs=[pl.BlockSpec((ts, wo), rev), pl.BlockSpec((ts, LANES), rev), vec(w), vec(w), vec(LANES)],
        out_shape=[jax.ShapeDtypeStruct((m, wo), BF16), jax.ShapeDtypeStruct((m, LANES), BF16),
                   jax.ShapeDtypeStruct((1, w), F32),
                   jax.ShapeDtypeStruct((1, w), F32), jax.ShapeDtypeStruct((1, LANES), F32)],
        scratch_shapes=[pltpu.VMEM((1, LANES), F32)],
        compiler_params=_cp("arbitrary", "arbitrary"),
    )(proj, proj, proj, dqs, dkn, dv, dcum, qg, kg, bf)


def _chunk_masks():
    r = lax.broadcasted_iota(jnp.int32, (SEQ_BLOCK, SEQ_BLOCK), 0)
    c = lax.broadcasted_iota(jnp.int32, (SEQ_BLOCK, SEQ_BLOCK), 1)
    same = (r // SUB) == (c // SUB)
    return r, c, same


def _hg_gates(hf, lb):
    sg = _sigmoid(hf)
    f = lb + (1.0 - lb) * sg
    return sg, f, jnp.log(f), (1.0 - lb) * _sigmoid(-hf)


def _hg_intra_e(g_ref, base, t, srow):
    diff = g_ref[pl.ds(base + t, 1), :] - g_ref[pl.ds(base, SUB), :]
    return jnp.exp(jnp.where(srow <= t, diff, NEG))


def _hgrn_fwd(proj, lb, gain, nb, lp, name):
    m = proj.shape[0]
    tb = SEQ_BLOCK
    nblk = lp // tb
    ns = tb // SUB

    def body(q_ref, f_ref, i_ref, g_ref, lb_ref, gain_ref, oraw_ref, y_ref, ssave_ref,
             st_ref, g_scr, kin_scr, o_scr):
        @pl.when(pl.program_id(2) == 0)
        def _():
            st_ref[...] = jnp.zeros_like(st_ref)

        ssave_ref[...] = st_ref[...]
        lbv = lb_ref[...]
        _, _, lf, kin = _hg_gates(f_ref[...], lbv)
        r, c, same = _chunk_masks()
        ltri = jnp.where(same & (c <= r), 1.0, 0.0).astype(F32)
        lall = jnp.where(same, 1.0, 0.0).astype(F32)
        g = _dot(ltri, lf, NN, HI)
        gt = _dot(lall, lf, NN, HI)
        g_scr[...] = g
        kin_scr[...] = kin
        qv = q_ref[...]
        qg = (qv * jnp.exp(g)).astype(BF16)
        kg = (kin * jnp.exp(gt - g)).astype(BF16)
        et = jnp.exp(gt)
        srow = lax.broadcasted_iota(jnp.int32, (SUB, 1), 0)
        for cc in range(ns):
            base = cc * SUB
            sl = slice(base, base + SUB)
            st = st_ref[...]
            o_c = _dot(qg[sl], st.astype(BF16), NT)
            kc = kin_scr[sl, :]
            vc = i_ref[sl, :]
            for t in range(SUB):
                e = _hg_intra_e(g_scr, base, t, srow)
                a = jnp.sum((q_ref[pl.ds(base + t, 1), :] * kc) * e, axis=1, keepdims=True)
                ot = jnp.sum(a * vc, axis=0, keepdims=True)
                o_c = o_c + jnp.where(srow == t, ot, 0.0)
            o_scr[sl, :] = o_c
            st_ref[...] = et[base:base + 1, :] * st + _dot(vc.astype(BF16), kg[sl], TN)
        o = o_scr[...]
        oraw_ref[...] = o
        rr = lax.rsqrt(jnp.mean(o * o, axis=-1, keepdims=True) + EPS)
        hg = g_ref[...]
        y_ref[...] = (((o * rr) * gain_ref[...]) * (hg * _sigmoid(hg))).astype(y_ref.dtype)

    col = lambda cb: pl.BlockSpec((tb, LANES), lambda b, h, i, cb=cb: (b * nblk + i, cb + h))
    out_blk = pl.BlockSpec((tb, LANES), lambda b, h, i: (b * nblk + i, h))
    return pl.pallas_call(
        body, name=name, grid=(nb, HG_HEADS, nblk),
        in_specs=[col(CB_HQ), col(CB_HF), col(CB_HI), col(CB_HG),
                  pl.BlockSpec((1, LANES), lambda b, h, i: (0, h)),
                  pl.BlockSpec((1, LANES), lambda b, h, i: (0, 0))],
        out_specs=[out_blk, out_blk,
                   pl.BlockSpec((None, None, None, HG_DIM, HG_DIM), lambda b, h, i: (b, h, i, 0, 0))],
        out_shape=[jax.ShapeDtypeStruct((m, HG_WIDTH), F32), jax.ShapeDtypeStruct((m, HG_WIDTH), BF16),
                   jax.ShapeDtypeStruct((nb, HG_HEADS, nblk, HG_DIM, HG_DIM), F32)],
        scratch_shapes=[pltpu.VMEM((HG_DIM, HG_DIM), F32), pltpu.VMEM((tb, LANES), F32),
                        pltpu.VMEM((tb, LANES), F32), pltpu.VMEM((tb, LANES), F32)],
        compiler_params=_cp("parallel", "parallel", "arbitrary"),
    )(proj, proj, proj, proj, lb, gain)


def _hgrn_bwd(proj, oraw, ssave, dy, lb, gain, nb, lp, name):
    m = proj.shape[0]
    tb = SEQ_BLOCK
    nblk = lp // tb
    ns = tb // SUB

    def body(q_ref, f_ref, i_ref, g_ref, oraw_ref, ssave_ref, dy_ref, lb_ref, gain_ref,
             dq_ref, df_ref, di_ref, dg_ref, dgain_ref, dlb_ref,
             dst_ref, sts_ref, g_scr, kin_scr, do_scr, dq_scr, dk_scr, dv_scr, dgg_scr):
        hd = pl.program_id(0)
        bb = pl.program_id(1)
        ii = pl.program_id(2)
        gainv = gain_ref[...]
        lbv = lb_ref[...]

        @pl.when((hd == 0) & (bb == 0) & (ii == 0))
        def _():
            dgain_ref[...] = jnp.zeros_like(dgain_ref)

        @pl.when((bb == 0) & (ii == 0))
        def _():
            dlb_ref[...] = jnp.zeros_like(dlb_ref)

        @pl.when(ii == 0)
        def _():
            dst_ref[...] = jnp.zeros_like(dst_ref)

        o = oraw_ref[...]
        rr = lax.rsqrt(jnp.mean(o * o, axis=-1, keepdims=True) + EPS)
        nv = o * rr
        hg = g_ref[...]
        sgg = _sigmoid(hg)
        sil = hg * sgg
        dyv = dy_ref[...]
        dg_ref[...] = (dyv * nv * gainv * (sgg * (1.0 + hg * (1.0 - sgg)))).astype(dg_ref.dtype)
        dgain_ref[...] += jnp.sum(dyv * nv * sil, axis=0, keepdims=True)
        dn = dyv * gainv * sil
        do_scr[...] = rr * (dn - nv * jnp.mean(dn * nv, axis=-1, keepdims=True))

        hf = f_ref[...]
        sg, f, lf, kin = _hg_gates(hf, lbv)
        r, c, same = _chunk_masks()
        ltri = jnp.where(same & (c <= r), 1.0, 0.0).astype(F32)
        lall = jnp.where(same, 1.0, 0.0).astype(F32)
        g = _dot(ltri, lf, NN, HI)
        gt = _dot(lall, lf, NN, HI)
        g_scr[...] = g
        kin_scr[...] = kin
        qv = q_ref[...]
        eg = jnp.exp(g)
        ekg = jnp.exp(gt - g)
        qg = qv * eg
        kg = kin * ekg
        qg16 = qg.astype(BF16)
        kg16 = kg.astype(BF16)
        et = jnp.exp(gt)
        st = ssave_ref[...]
        for cc in range(ns):
            sl = slice(cc * SUB, (cc + 1) * SUB)
            sts_ref[cc] = st
            st = et[cc * SUB:cc * SUB + 1, :] * st + _dot(i_ref[sl, :].astype(BF16), kg16[sl], TN)

        srow = lax.broadcasted_iota(jnp.int32, (SUB, 1), 0)
        for cc in reversed(range(ns)):
            base = cc * SUB
            sl = slice(base, base + SUB)
            st = sts_ref[cc]
            st16 = st.astype(BF16)
            dst = dst_ref[...]
            dst16 = dst.astype(BF16)
            doc = do_scr[sl, :]
            doc16 = doc.astype(BF16)
            vc = i_ref[sl, :]
            vc16 = vc.astype(BF16)
            kc = kin_scr[sl, :]
            etc = et[base:base + 1, :]
            dqg = _dot(doc16, st16, NN)
            dst_o = _dot(doc16, qg16[sl], TN)
            dv_c = _dot(kg16[sl], dst16, NT)
            dkg = _dot(vc16, dst16, NN)
            dgt = jnp.sum(dst * st, axis=0, keepdims=True) * etc
            dst_ref[...] = etc * dst + dst_o
            dq_c = dqg * eg[sl]
            dk_c = dkg * ekg[sl]
            dg_c = dqg * qg[sl] - dkg * kg[sl]
            dgt = dgt + jnp.sum(dkg * kg[sl], axis=0, keepdims=True)
            for t in range(SUB):
                e = _hg_intra_e(g_scr, base, t, srow)
                qt = q_ref[pl.ds(base + t, 1), :]
                dot_t = do_scr[pl.ds(base + t, 1), :]
                a = jnp.sum((qt * kc) * e, axis=1, keepdims=True)
                da = jnp.sum(dot_t * vc, axis=1, keepdims=True)
                dv_c = dv_c + a * dot_t
                w = da * e
                dq_t = jnp.sum(w * kc, axis=0, keepdims=True)
                wq = w * qt
                dk_c = dk_c + wq
                dg_c = dg_c - kc * wq + jnp.where(srow == t, qt * dq_t, 0.0)
                dq_c = dq_c + jnp.where(srow == t, dq_t, 0.0)
            dg_c = dg_c + jnp.where(srow == SUB - 1, dgt, 0.0)
            dq_scr[sl, :] = dq_c
            dk_scr[sl, :] = dk_c
            dv_scr[sl, :] = dv_c
            dgg_scr[sl, :] = dg_c

        utri = jnp.where(same & (c >= r), 1.0, 0.0).astype(F32)
        dlf = _dot(utri, dgg_scr[...], NN, HI)
        dkin = dk_scr[...]
        dsg = sg * (1.0 - sg)
        df_ref[...] = ((dlf / f - dkin) * ((1.0 - lbv) * dsg)).astype(df_ref.dtype)
        dlb_ref[...] += jnp.sum((dlf / f - dkin) * (1.0 - sg), axis=0, keepdims=True)
        dq_ref[...] = dq_scr[...].astype(dq_ref.dtype)
        di_ref[...] = dv_scr[...].astype(di_ref.dtype)

    rowi = lambda b, i: b * nblk + (nblk - 1 - i)
    col = lambda cb: pl.BlockSpec((tb, LANES), lambda h, b, i, cb=cb: (rowi(b, i), cb + h))
    hblk = pl.BlockSpec((tb, LANES), lambda h, b, i: (rowi(b, i), h))
    return pl.pallas_call(
        body, name=name, grid=(HG_HEADS, nb, nblk),
        in_specs=[col(CB_HQ), col(CB_HF), col(CB_HI), col(CB_HG), hblk,
                  pl.BlockSpec((None, None, None, HG_DIM, HG_DIM), lambda h, b, i: (b, h, nblk - 1 - i, 0, 0)),
                  hblk,
                  pl.BlockSpec((1, LANES), lambda h, b, i: (0, h)),
                  pl.BlockSpec((1, LANES), lambda h, b, i: (0, 0))],
        out_specs=[hblk, hblk, hblk, hblk,
                   pl.BlockSpec((1, LANES), lambda h, b, i: (0, 0)),
                   pl.BlockSpec((1, LANES), lambda h, b, i: (0, h))],
        out_shape=[jax.ShapeDtypeStruct((m, HG_WIDTH), BF16)] * 4
        + [jax.ShapeDtypeStruct((1, LANES), F32), jax.ShapeDtypeStruct((1, HG_WIDTH), F32)],
        scratch_shapes=[pltpu.VMEM((HG_DIM, HG_DIM), F32), pltpu.VMEM((ns, HG_DIM, HG_DIM), F32)]
        + [pltpu.VMEM((tb, LANES), F32)] * 7,
        compiler_params=_cp("arbitrary", "arbitrary", "arbitrary"),
    )(proj, proj, proj, proj, oraw, ssave, dy, lb, gain)


def _gate_fwd(proj, ya, yb, name):
    m = proj.shape[0]
    tm = _div_tile(m, 256, 16)

    def body(ga_ref, gb_ref, ya_ref, yb_ref, o_ref):
        o_ref[...] = (_sigmoid(ga_ref[...]) * ya_ref[...] + _sigmoid(gb_ref[...]) * yb_ref[...]).astype(o_ref.dtype)

    row = pl.BlockSpec((tm, D_MODEL), lambda i: (i, 0))
    return pl.pallas_call(
        body, name=name, grid=(m // tm,),
        in_specs=[row, pl.BlockSpec((tm, D_MODEL), lambda i: (i, 1)), row, row],
        out_specs=row, out_shape=jax.ShapeDtypeStruct((m, D_MODEL), BF16),
        compiler_params=_cp("parallel"),
    )(proj, proj, ya, yb)


def _gate_bwd(proj, ya, yb, dm, name):
    m = proj.shape[0]
    tm = _div_tile(m, 256, 16)

    def body(ga_ref, gb_ref, ya_ref, yb_ref, dm_ref, dya_ref, dyb_ref, dg_ref):
        dmv = dm_ref[...]
        sa = _sigmoid(ga_ref[...])
        sb = _sigmoid(gb_ref[...])
        dya_ref[...] = (dmv * sa).astype(BF16)
        dyb_ref[...] = (dmv * sb).astype(BF16)
        dg_ref[:, :D_MODEL] = (dmv * ya_ref[...] * (sa * (1.0 - sa))).astype(BF16)
        dg_ref[:, D_MODEL:] = (dmv * yb_ref[...] * (sb * (1.0 - sb))).astype(BF16)

    row = pl.BlockSpec((tm, D_MODEL), lambda i: (i, 0))
    wide = pl.BlockSpec((tm, 2 * D_MODEL), lambda i: (i, 0))
    return pl.pallas_call(
        body, name=name, grid=(m // tm,),
        in_specs=[row, pl.BlockSpec((tm, D_MODEL), lambda i: (i, 1)), row, row, row],
        out_specs=[row, row, wide],
        out_shape=[jax.ShapeDtypeStruct((m, D_MODEL), BF16)] * 2 + [jax.ShapeDtypeStruct((m, 2 * D_MODEL), BF16)],
        compiler_params=_cp("parallel"),
    )(proj, proj, ya, yb, dm)


CONV_ROWS = 128


def _conv3(x, xprev, w_ref, b_ref, rowi):
    r = x.shape[0]
    x1 = jnp.where(rowi < 1, pltpu.roll(xprev, 1, 0), pltpu.roll(x, 1, 0))
    x2 = jnp.where(rowi < 2, pltpu.roll(xprev, 2, 0), pltpu.roll(x, 2, 0))
    u = w_ref[0:1, :] * x2 + w_ref[1:2, :] * x1 + w_ref[2:3, :] * x + b_ref[...]
    return u, x1, x2


def _conv_fwd(up, cw, cb, nb, lp, name):
    m = up.shape[0]
    nct = D_FF // LANES
    r = CONV_ROWS
    nch = lp // r

    def body(u_ref, w_ref, b_ref, o_ref):
        rowi = lax.broadcasted_iota(jnp.int32, (r, 1), 0)

        def step(i, xp):
            r0 = pl.multiple_of(i * r, r)
            xc = u_ref[pl.ds(r0, r), :]
            u, _, _ = _conv3(xc, xp, w_ref, b_ref, rowi)
            ug, uv = u[:, :LANES], u[:, LANES:]
            o_ref[pl.ds(r0, r), :] = ((ug * _sigmoid(ug)) * uv).astype(o_ref.dtype)
            return xc

        lax.fori_loop(0, nch, step, jnp.zeros((r, 2 * LANES), F32))

    return pl.pallas_call(
        body, name=name, grid=(nb, nct),
        in_specs=[pl.BlockSpec((lp, 2 * LANES), lambda b, c: (b, c)),
                  pl.BlockSpec((CONV_WIDTH, 2 * LANES), lambda b, c: (0, c)),
                  pl.BlockSpec((1, 2 * LANES), lambda b, c: (0, c))],
        out_specs=pl.BlockSpec((lp, LANES), lambda b, c: (b, c)),
        out_shape=jax.ShapeDtypeStruct((m, D_FF), BF16),
        compiler_params=_cp("parallel", "parallel"),
    )(up, cw, cb)


def _conv_bwd(up, dact, cw, cb, nb, lp, name):
    m = up.shape[0]
    nct = D_FF // LANES
    r = CONV_ROWS
    nch = lp // r

    def body(u_ref, da_ref, w_ref, b_ref, dup_ref, dw_ref, db_ref):
        rowi = lax.broadcasted_iota(jnp.int32, (r, 1), 0)
        wv = w_ref[...]

        def step(k, carry):
            dun, dw0, dw1, dw2, dbs = carry
            i = nch - 1 - k
            r0 = pl.multiple_of(i * r, r)
            rp = pl.multiple_of(jnp.maximum(i - 1, 0) * r, r)
            xc = u_ref[pl.ds(r0, r), :]
            xp = u_ref[pl.ds(rp, r), :] * (i > 0).astype(F32)
            u, x1, x2 = _conv3(xc, xp, w_ref, b_ref, rowi)
            ug, uv = u[:, :LANES], u[:, LANES:]
            da = da_ref[pl.ds(r0, r), :]
            sg = _sigmoid(ug)
            du = jnp.concatenate([da * uv * (sg * (1.0 + ug * (1.0 - sg))), da * (ug * sg)], axis=1)
            d1 = jnp.where(rowi >= r - 1, pltpu.roll(dun, r - 1, 0), pltpu.roll(du, r - 1, 0))
            d2 = jnp.where(rowi >= r - 2, pltpu.roll(dun, r - 2, 0), pltpu.roll(du, r - 2, 0))
            dup_ref[pl.ds(r0, r), :] = (wv[2:3, :] * du + wv[1:2, :] * d1 + wv[0:1, :] * d2).astype(dup_ref.dtype)
            dw0 = dw0 + jnp.sum(du * x2, axis=0, keepdims=True)
            dw1 = dw1 + jnp.sum(du * x1, axis=0, keepdims=True)
            dw2 = dw2 + jnp.sum(du * xc, axis=0, keepdims=True)
            dbs = dbs + jnp.sum(du, axis=0, keepdims=True)
            return du, dw0, dw1, dw2, dbs

        z1 = jnp.zeros((1, 2 * LANES), F32)
        _, dw0, dw1, dw2, dbs = lax.fori_loop(0, nch, step, (jnp.zeros((r, 2 * LANES), F32), z1, z1, z1, z1))

        @pl.when(pl.program_id(1) == 0)
        def _():
            dw_ref[...] = jnp.zeros_like(dw_ref)
            db_ref[...] = jnp.zeros_like(db_ref)

        dw_ref[0:1, :] += dw0
        dw_ref[1:2, :] += dw1
        dw_ref[2:3, :] += dw2
        db_ref[...] += dbs

    return pl.pallas_call(
        body, name=name, grid=(nct, nb),
        in_specs=[pl.BlockSpec((lp, 2 * LANES), lambda c, b: (b, c)),
                  pl.BlockSpec((lp, LANES), lambda c, b: (b, c)),
                  pl.BlockSpec((CONV_WIDTH, 2 * LANES), lambda c, b: (0, c)),
                  pl.BlockSpec((1, 2 * LANES), lambda c, b: (0, c))],
        out_specs=[pl.BlockSpec((lp, 2 * LANES), lambda c, b: (b, c)),
                   pl.BlockSpec((CONV_WIDTH, 2 * LANES), lambda c, b: (0, c)),
                   pl.BlockSpec((1, 2 * LANES), lambda c, b: (0, c))],
        out_shape=[jax.ShapeDtypeStruct((m, 2 * D_FF), BF16),
                   jax.ShapeDtypeStruct((CONV_WIDTH, 2 * D_FF), F32),
                   jax.ShapeDtypeStruct((1, 2 * D_FF), F32)],
        compiler_params=_cp("parallel", "arbitrary"),
    )(up, dact, cw, cb)


def _ffn_interleave(a, axis):
    shp = a.shape
    a = a.reshape(shp[:axis] + (2, D_FF // LANES, LANES) + shp[axis + 1:])
    return jnp.swapaxes(a, axis, axis + 1).reshape(shp)


def _ffn_deinterleave(a, axis):
    shp = a.shape
    a = a.reshape(shp[:axis] + (D_FF // LANES, 2, LANES) + shp[axis + 1:])
    return jnp.swapaxes(a, axis, axis + 1).reshape(shp)


def _shifted_rows(prev_ref, cur_ref):
    keep = SEQ_BLOCK - N_META
    return jnp.concatenate([prev_ref[keep:, :], cur_ref[:keep, :]], axis=0)


def _frame_specs(nblk, nfb, d):
    prev = pl.BlockSpec((SEQ_BLOCK, d), lambda b, i: (b * nfb + jnp.clip(i - 1, 0, nfb - 1), 0))
    cur = pl.BlockSpec((SEQ_BLOCK, d), lambda b, i: (b * nfb + jnp.clip(i, 0, nfb - 1), 0))
    return prev, cur


def _embed_rms(x2, meta, gain, nb, lp, l, name):
    d = x2.shape[1]
    tr = SEQ_BLOCK
    nblk = lp // tr
    nfb = (l - N_META) // tr
    m = nb * lp

    def body(prev_ref, cur_ref, meta_ref, g_ref, h_ref, o_ref):
        i = pl.program_id(1)
        t = i * tr + lax.broadcasted_iota(jnp.int32, (tr, 1), 0)
        rows = jnp.where(t < l, _shifted_rows(prev_ref, cur_ref), 0.0)
        head = jnp.concatenate([meta_ref[...], jnp.zeros((tr - N_META, d), F32)], axis=0)
        xv = jnp.where(t < N_META, head, rows)
        h_ref[...] = xv
        r = lax.rsqrt(jnp.mean(xv * xv, axis=-1, keepdims=True) + EPS)
        o_ref[...] = ((xv * r) * g_ref[...]).astype(o_ref.dtype)

    prev, cur = _frame_specs(nblk, nfb, d)
    row = pl.BlockSpec((tr, d), lambda b, i: (b * nblk + i, 0))
    return pl.pallas_call(
        body, name=name, grid=(nb, nblk),
        in_specs=[prev, cur, pl.BlockSpec((N_META, d), lambda b, i: (0, 0)), pl.BlockSpec((1, d), lambda b, i: (0, 0))],
        out_specs=[row, row],
        out_shape=[jax.ShapeDtypeStruct((m, d), F32), jax.ShapeDtypeStruct((m, d), BF16)],
        compiler_params=_cp("parallel", "parallel"),
    )(x2, x2, meta, gain)


def _loss_head(out, tgt2, nb, lp, l, name):
    m, d = out.shape
    tr = SEQ_BLOCK
    nblk = lp // tr
    nfb = (l - N_META) // tr

    def body(o_ref, prev_ref, cur_ref, dy_ref, ls_ref):
        t = pl.program_id(1) * tr + lax.broadcasted_iota(jnp.int32, (tr, 1), 0)
        valid = (t >= N_META) & (t < l)
        err = jnp.where(valid, o_ref[...] - _shifted_rows(prev_ref, cur_ref), 0.0)
        dy_ref[...] = err * (1.0 / d)
        part = jnp.sum(err * err, axis=0, keepdims=True)
        first = (pl.program_id(0) == 0) & (pl.program_id(1) == 0)

        @pl.when(first)
        def _():
            ls_ref[...] = part

        @pl.when(jnp.logical_not(first))
        def _():
            ls_ref[...] += part

    prev, cur = _frame_specs(nblk, nfb, d)
    row = pl.BlockSpec((tr, d), lambda b, i: (b * nblk + i, 0))
    return pl.pallas_call(
        body, name=name, grid=(nb, nblk),
        in_specs=[row, prev, cur], out_specs=[row, pl.BlockSpec((1, d), lambda b, i: (0, 0))],
        out_shape=[jax.ShapeDtypeStruct((m, d), F32), jax.ShapeDtypeStruct((1, d), F32)],
        compiler_params=_cp("arbitrary", "arbitrary"),
    )(out, tgt2, tgt2)


def _adam_math(g, w, mom, var):
    c1 = 1.0 - ADAM_B1 ** ADAM_STEP
    c2 = 1.0 - ADAM_B2 ** ADAM_STEP
    mn = ADAM_B1 * mom + (1.0 - ADAM_B1) * g
    vn = ADAM_B2 * var + (1.0 - ADAM_B2) * (g * g)
    delta = -ADAM_LR * ((mn / c1) / (jnp.sqrt(vn / c2) + ADAM_EPS) + ADAM_WD * w)
    return delta, mn, vn


def _slot_sum(recv, name):
    _, r, c = recv.shape
    tc = _div_tile(c, 256, LANES)

    def body(r_ref, g_ref):
        g = r_ref[0]
        for s in range(1, N_DEV):
            g = g + r_ref[s]
        g_ref[...] = g

    return pl.pallas_call(
        body, name=name, grid=(c // tc,),
        in_specs=[pl.BlockSpec((N_DEV, r, tc), lambda j: (0, 0, j))],
        out_specs=pl.BlockSpec((r, tc), lambda j: (0, j)),
        out_shape=jax.ShapeDtypeStruct((r, c), F32),
        compiler_params=_cp("parallel"),
    )(recv)


def _adamw(g, w, mom, var, name):
    r, c = w.shape
    tr = _div_tile(r, 256, 8)

    def body(g_ref, w_ref, m_ref, v_ref, d_ref, mo_ref, vo_ref):
        d_ref[...], mo_ref[...], vo_ref[...] = _adam_math(g_ref[...], w_ref[...], m_ref[...], v_ref[...])

    row = pl.BlockSpec((tr, c), lambda i: (i, 0))
    return pl.pallas_call(
        body, name=name, grid=(r // tr,), in_specs=[row] * 4, out_specs=[row] * 3,
        out_shape=[jax.ShapeDtypeStruct((r, c), F32)] * 3,
        compiler_params=_cp("parallel"),
    )(g, w, mom, var)


def _sum_adamw(recv, w, mom, var, name):
    r, c = w.shape
    tr = _div_tile(r, 256, 8)

    def body(r_ref, w_ref, m_ref, v_ref, g_ref, d_ref, mo_ref, vo_ref):
        g = r_ref[0]
        for s in range(1, N_DEV):
            g = g + r_ref[s]
        g_ref[...] = g
        d_ref[...], mo_ref[...], vo_ref[...] = _adam_math(g, w_ref[...], m_ref[...], v_ref[...])

    row = pl.BlockSpec((tr, c), lambda i: (i, 0))
    return pl.pallas_call(
        body, name=name, grid=(r // tr,),
        in_specs=[pl.BlockSpec((N_DEV, tr, c), lambda i: (0, i, 0)), row, row, row],
        out_specs=[row] * 4,
        out_shape=[jax.ShapeDtypeStruct((r, c), F32)] * 4,
        compiler_params=_cp("parallel"),
    )(recv, w, mom, var)


_MESH = pl.DeviceIdType.MESH
_HBM = pl.BlockSpec(memory_space=pltpu.HBM)
N_PEER = N_DEV - 1


def _position():
    return lax.axis_index("x"), lax.axis_index("y"), lax.axis_index("c")


def _all_gather(shards, name):
    n = len(shards)

    def body(*refs):
        x_refs, out_refs = refs[:n], refs[n:2 * n]
        send_sems, recv_sems, local_sems = refs[2 * n:]
        x, y, c = _position()
        me, sibling = (x, y, c), (x, y, 1 - c)
        chips = [(1 - x, y), (x, 1 - y), (1 - x, 1 - y)]

        def copy(a, k, block, to, src=None):
            slot = out_refs[a].at[4 * block[0] + 2 * block[1] + block[2]]
            return pltpu.make_async_remote_copy(
                src_ref=slot if src is None else src, dst_ref=slot,
                send_sem=send_sems.at[a * N_PEER + k], recv_sem=recv_sems.at[a * N_PEER + k],
                device_id=to, device_id_type=_MESH)

        mine, sent = [], []
        for a in range(n):
            cp = pltpu.make_async_copy(x_refs[a], out_refs[a].at[4 * x + 2 * y + c], local_sems.at[a])
            cp.start()
            mine.append(cp)
            first = [copy(a, 0, me, sibling, src=x_refs[a])]
            first += [copy(a, 1 + j, me, (*chip, c), src=x_refs[a]) for j, chip in enumerate(chips)]
            for cp in first:
                cp.start()
            sent += first
        for a in range(n):
            for j, chip in enumerate(chips):
                copy(a, 1 + j, (*chip, c), me).wait_recv()
                fwd = copy(a, 4 + j, (*chip, c), sibling)
                fwd.start()
                sent.append(fwd)
        for a in range(n):
            copy(a, 0, sibling, me).wait_recv()
            for j, chip in enumerate(chips):
                copy(a, 4 + j, (*chip, 1 - c), me).wait_recv()
        for cp in sent:
            cp.wait_send()
        for cp in mine:
            cp.wait()

    return pl.pallas_call(
        body, name=name,
        out_shape=[jax.ShapeDtypeStruct((N_DEV,) + a.shape, a.dtype) for a in shards],
        in_specs=[_HBM] * n, out_specs=[_HBM] * n,
        scratch_shapes=[pltpu.SemaphoreType.DMA((n * N_PEER,)), pltpu.SemaphoreType.DMA((n * N_PEER,)),
                        pltpu.SemaphoreType.DMA((n,))],
    )(*shards)


def _exchange(blocks, shared, name):
    flips = [(fx, fy, fc) for fx in (0, 1) for fy in (0, 1) for fc in (0, 1)][1:]
    nblk, n = len(blocks), len(blocks) + len(shared)

    def body(*refs):
        in_refs, out_refs = refs[:n], refs[n:2 * n]
        send_sems, recv_sems, local_sems = refs[2 * n:]
        x, y, c = _position()
        me = 4 * x + 2 * y + c

        def peer(f):
            return (1 - x if f[0] else x, 1 - y if f[1] else y, 1 - c if f[2] else c)

        def idx(p):
            return 4 * p[0] + 2 * p[1] + p[2]

        def src(a, p):
            return in_refs[a].at[idx(p)] if a < nblk else in_refs[a]

        mine, sends = [], []
        for a in range(n):
            cp = pltpu.make_async_copy(in_refs[a].at[me] if a < nblk else in_refs[a], out_refs[a].at[me], local_sems.at[a])
            cp.start()
            mine.append(cp)
            for k, f in enumerate(flips):
                p = peer(f)
                cp = pltpu.make_async_remote_copy(
                    src_ref=src(a, p), dst_ref=out_refs[a].at[me],
                    send_sem=send_sems.at[a * N_PEER + k], recv_sem=recv_sems.at[a * N_PEER + k],
                    device_id=p, device_id_type=_MESH)
                cp.start()
                sends.append(cp)
        for a in range(n):
            for k, f in enumerate(flips):
                p = peer(f)
                pltpu.make_async_remote_copy(
                    src_ref=src(a, p), dst_ref=out_refs[a].at[idx(p)],
                    send_sem=send_sems.at[a * N_PEER + k], recv_sem=recv_sems.at[a * N_PEER + k],
                    device_id=p, device_id_type=_MESH).wait_recv()
        for cp in sends:
            cp.wait_send()
        for cp in mine:
            cp.wait()

    arrays = list(blocks) + list(shared)
    out_shape = [jax.ShapeDtypeStruct(a.shape, a.dtype) for a in blocks]
    out_shape += [jax.ShapeDtypeStruct((N_DEV,) + a.shape, a.dtype) for a in shared]
    return pl.pallas_call(
        body, name=name, out_shape=out_shape,
        in_specs=[_HBM] * n, out_specs=[_HBM] * n,
        scratch_shapes=[pltpu.SemaphoreType.DMA((n * N_PEER,)), pltpu.SemaphoreType.DMA((n * N_PEER,)),
                        pltpu.SemaphoreType.DMA((n,))],
    )(*arrays)


def _pack(parts, rows):
    flat = jnp.concatenate(parts, axis=-1)
    return jnp.pad(flat, [(0, rows * LANES - flat.shape[-1])]).reshape(rows, LANES)


def _unpack(packed, shapes):
    flat = packed.reshape(-1)
    out, off = [], 0
    for shp in shapes:
        n = int(np.prod(shp))
        out.append(flat[off:off + n].reshape(shp))
        off += n
    return out


def _rows_for(shapes, extra=0):
    n = sum(int(np.prod(s)) for s in shapes) + extra
    return -(-n // (8 * LANES)) * 8


def _lower_bound(logits):
    return jnp.cumsum(jax.nn.softmax(logits.astype(F32), axis=0), axis=0)[0:1]


def _align_axis0(w):
    a, b = 3 * FOX_WIDTH, 3 * FOX_WIDTH + FOX_HEADS
    c = b + 4 * HG_WIDTH
    pad = [(0, LANES - FOX_HEADS)] + [(0, 0)] * (w.ndim - 1)
    return jnp.concatenate([w[c:], w[:a], w[b:c], jnp.pad(w[a:b], pad)], axis=0)


def _unalign_axis0(g):
    a, b = 2 * D_MODEL, 2 * D_MODEL + 3 * FOX_WIDTH
    c = b + 4 * HG_WIDTH
    return jnp.concatenate([g[a:b], g[c:c + FOX_HEADS], g[b:c], g[:a]], axis=0)


TINY_COLS = 768


def _tiny_pack(conv_w_shard, meta_shard):
    cw = jnp.pad(conv_w_shard, ((0, 8 - CONV_WIDTH), (0, TINY_COLS - conv_w_shard.shape[1])))
    mt = jnp.pad(meta_shard, ((0, 0), (0, TINY_COLS - meta_shard.shape[1])))
    return jnp.concatenate([cw, mt], axis=0)


def _tiny_unpack(t, ncw, nmeta):
    return t[..., :CONV_WIDTH, :ncw], t[..., 8:8 + N_META, :nmeta]


def _local_step(x, target, meta, norm1_gain, w_in_t, fox_b_f, q_gain, k_gain, lb, hg_out_gain, w_a_t, w_b_t, w_out,
                norm2_gain, w_up_t, conv_w, conv_b, w_down):
    nb, seq, d = x.shape
    assert seq % SEQ_BLOCK == 0 and N_META < SEQ_BLOCK
    l = seq + N_META
    lp = -(-l // SEQ_BLOCK) * SEQ_BLOCK
    m = nb * lp
    qg = jnp.tile(q_gain, (1, FOX_HEADS))
    kg = jnp.tile(k_gain, (1, FOX_HEADS))
    bf = jnp.pad(fox_b_f, ((0, 0), (0, LANES - FOX_HEADS)))

    h0, xn = _embed_rms(x.reshape(nb * seq, d), meta, norm1_gain, nb, lp, l, "embed_rms1")
    proj = _matmul(xn, w_in_t, "nt", F32, "proj_in")
    qs, kn, vb, cum = _fox_prep(proj, qg, kg, bf, nb, lp, "fox_prep")
    cum_t = jnp.transpose(cum.reshape(nb, lp, LANES)[:, :, :FOX_HEADS], (0, 2, 1)).reshape(nb, FOX_HEADS, 1, lp)
    o_fox, lse = _fox_fwd(qs, kn, vb, cum, cum_t, nb, lp, "fox_fwd")
    o_raw, o_hg, s_save = _hgrn_fwd(proj, lb, hg_out_gain, nb, lp, "hgrn_fwd")
    ya = _matmul(o_hg, w_a_t, "nt", F32, "branch_a")
    yb = _matmul(o_fox, w_b_t, "nt", F32, "branch_b")
    merged = _gate_fwd(proj, ya, yb, "gate_fwd")
    h1 = _matmul(merged, w_out, "nn", F32, "mix_out", residual=h0)
    hn = _rms_fwd(h1, norm2_gain, "rms2_fwd")
    up = _matmul(hn, w_up_t, "nt", F32, "ffn_up")
    act = _conv_fwd(up, conv_w, conv_b, nb, lp, "conv_fwd")
    out = _matmul(act, w_down, "nn", F32, "ffn_down", residual=h1)
    dy, lsum = _loss_head(out, target.reshape(nb * seq, d), nb, lp, l, "loss_head")
    loss = (0.5 / d) * jnp.sum(lsum)

    dact = _matmul(dy, w_down, "nt", F32, "d_act")
    g_w_down = _matmul(act, dy, "tn", F32, "g_w_down")
    dup, g_conv_w, g_conv_b = _conv_bwd(up, dact, conv_w, conv_b, nb, lp, "conv_bwd")
    dhn = _matmul(dup, w_up_t, "nn", F32, "d_hn")
    g_w_up_t = _matmul(dup, hn, "tn", F32, "g_w_up")
    dh1, g_norm2 = _rms_bwd(h1, norm2_gain, dhn, dy, "rms2_bwd")

    dmerged = _matmul(dh1, w_out, "nt", F32, "d_merged")
    g_w_out = _matmul(merged, dh1, "tn", F32, "g_w_out")
    dya, dyb, dgab = _gate_bwd(proj, ya, yb, dmerged, "gate_bwd")
    do_hg = _matmul(dya, w_a_t, "nn", F32, "d_o_hg")
    g_w_a_t = _matmul(dya, o_hg, "tn", F32, "g_w_a")
    do_fox = _matmul(dyb, w_b_t, "nn", F32, "d_o_fox")
    g_w_b_t = _matmul(dyb, o_fox, "tn", F32, "g_w_b")
    dhq, dhf, dhi, dhg, g_hg_gain, g_lb = _hgrn_bwd(proj, o_raw, s_save, do_hg, lb, hg_out_gain, nb, lp, "hgrn_bwd")
    dqs, dkn, dvv, dc0, dc1 = _fox_bwd(qs, kn, vb, do_fox, o_fox, lse, cum, cum_t, nb, lp, "fox_bwd")
    dcum = jnp.stack([dc0, dc1], axis=2).reshape(nb, FOX_HEADS, lp)
    dcum = jnp.pad(jnp.transpose(dcum, (0, 2, 1)), ((0, 0), (0, 0), (0, LANES - FOX_HEADS))).reshape(m, LANES)
    dfqkv, dff, g_qg, g_kg, g_bf = _fox_prep_bwd(proj, dqs, dkn, dvv, dcum, qg, kg, bf, nb, lp, "fox_prep_bwd")
    dproj = jnp.concatenate([dgab, dfqkv, dhq, dhf, dhi, dhg, dff], axis=1)
    dxn = _matmul(dproj, w_in_t, "nn", F32, "d_xn")
    g_w_in_t = _matmul(dproj, xn, "tn", F32, "g_w_in")
    dh0, g_norm1 = _rms_bwd(h0, norm1_gain, dxn, dh1, "rms1_bwd")

    dh0 = dh0.reshape(nb, lp, d)
    grad_x = dh0[:, N_META:l]
    g_meta = jnp.sum(dh0[:, :N_META], axis=0)
    g_q_gain = jnp.sum(g_qg.reshape(FOX_HEADS, FOX_HEAD_DIM), axis=0, keepdims=True)
    g_k_gain = jnp.sum(g_kg.reshape(FOX_HEADS, FOX_HEAD_DIM), axis=0, keepdims=True)
    grads = dict(meta_tokens=g_meta, norm1_gain=g_norm1, w_in_t=g_w_in_t, fox_b_f=g_bf[:, :FOX_HEADS],
                 q_norm_gain=g_q_gain, k_norm_gain=g_k_gain, lb=g_lb, hg_out_gain=g_hg_gain,
                 w_a_t=g_w_a_t, w_b_t=g_w_b_t, w_out=g_w_out, norm2_gain=g_norm2, w_up_t=g_w_up_t,
                 conv_w=g_conv_w, conv_b=g_conv_b, w_down=g_w_down)
    return loss, grad_x, grads


SMALL = ("norm1_gain", "fox_b_f", "q_norm_gain", "k_norm_gain", "hg_lb_logits", "hg_out_gain", "norm2_gain", "conv_b")
ORDER = ("meta_tokens", "norm1_gain", "w_in", "fox_b_f", "q_norm_gain", "k_norm_gain", "hg_lb_logits", "hg_out_gain",
         "w_branch_a", "w_branch_b", "w_out", "norm2_gain", "w_up", "conv_w", "conv_b", "w_down")


def kernel(x, meta_tokens, norm1_gain, w_in, fox_b_f, q_norm_gain, k_norm_gain, hg_lb_logits, hg_out_gain, w_branch_a, w_branch_b, w_out, norm2_gain, w_up, conv_w, conv_b, w_down, loss_target, m_meta_tokens, m_norm1_gain, m_w_in, m_fox_b_f, m_q_norm_gain, m_k_norm_gain, m_hg_lb_logits, m_hg_out_gain, m_w_branch_a, m_w_branch_b, m_w_out, m_norm2_gain, m_w_up, m_conv_w, m_conv_b, m_w_down, v_meta_tokens, v_norm1_gain, v_w_in, v_fox_b_f, v_q_norm_gain, v_k_norm_gain, v_hg_lb_logits, v_hg_out_gain, v_w_branch_a, v_w_branch_b, v_w_out, v_norm2_gain, v_w_up, v_conv_w, v_conv_b, v_w_down):
    w = dict(meta_tokens=meta_tokens, norm1_gain=norm1_gain, w_in=w_in, fox_b_f=fox_b_f, q_norm_gain=q_norm_gain,
             k_norm_gain=k_norm_gain, hg_lb_logits=hg_lb_logits, hg_out_gain=hg_out_gain, w_branch_a=w_branch_a,
             w_branch_b=w_branch_b, w_out=w_out, norm2_gain=norm2_gain, w_up=w_up, conv_w=conv_w, conv_b=conv_b,
             w_down=w_down)
    mom = dict(meta_tokens=m_meta_tokens, norm1_gain=m_norm1_gain, w_in=m_w_in, fox_b_f=m_fox_b_f,
               q_norm_gain=m_q_norm_gain, k_norm_gain=m_k_norm_gain, hg_lb_logits=m_hg_lb_logits,
               hg_out_gain=m_hg_out_gain, w_branch_a=m_w_branch_a, w_branch_b=m_w_branch_b, w_out=m_w_out,
               norm2_gain=m_norm2_gain, w_up=m_w_up, conv_w=m_conv_w, conv_b=m_conv_b, w_down=m_w_down)
    var = dict(meta_tokens=v_meta_tokens, norm1_gain=v_norm1_gain, w_in=v_w_in, fox_b_f=v_fox_b_f,
               q_norm_gain=v_q_norm_gain, k_norm_gain=v_k_norm_gain, hg_lb_logits=v_hg_lb_logits,
               hg_out_gain=v_hg_out_gain, w_branch_a=v_w_branch_a, w_branch_b=v_w_branch_b, w_out=v_w_out,
               norm2_gain=v_norm2_gain, w_up=v_w_up, conv_w=v_conv_w, conv_b=v_conv_b, w_down=v_w_down)
    d = D_MODEL
    n_in, n_up = w_in.shape[2], w_up.shape[2]
    n_ab, n_meta = w_branch_a.shape[2], meta_tokens.shape[1]

    shards = [w_in[0].T,
              w_up[0].T.astype(BF16),
              jnp.stack([w_branch_a[0].T, w_branch_b[0].T]).astype(BF16),
              w_out[0].astype(BF16),
              w_down[0].astype(BF16),
              _tiny_pack(conv_w[0], meta_tokens)]
    g_in, g_up, g_ab, g_out, g_down, g_tiny = _all_gather(shards, "gather_weights")
    w_in_t = _align_axis0(g_in.reshape(N_DEV * n_in, d)).astype(BF16)
    w_up_t = _ffn_interleave(g_up.reshape(N_DEV * n_up, d), 0)
    w_a_t = g_ab[:, 0].reshape(N_DEV * n_ab, -1)
    w_b_t = g_ab[:, 1].reshape(N_DEV * n_ab, -1)
    cw_slots, meta_slots = _tiny_unpack(g_tiny, n_up, n_meta)
    conv_w_f = _ffn_interleave(jnp.transpose(cw_slots, (1, 0, 2)).reshape(CONV_WIDTH, -1), 1)
    meta_f = jnp.transpose(meta_slots, (1, 0, 2)).reshape(N_META, -1)
    conv_b_i = _ffn_interleave(conv_b, 1)

    lb, lb_vjp = jax.vjp(_lower_bound, hg_lb_logits)
    loss, grad_x, g = _local_step(
        x, loss_target, meta_f, norm1_gain, w_in_t, fox_b_f, q_norm_gain, k_norm_gain, lb, hg_out_gain,
        w_a_t, w_b_t, g_out.reshape(d, d), norm2_gain, w_up_t, conv_w_f, conv_b_i, g_down.reshape(-1, d))

    g["hg_lb_logits"] = lb_vjp(g.pop("lb"))[0]
    g["conv_b"] = _ffn_deinterleave(g["conv_b"], 1)
    gcw = _ffn_deinterleave(g["conv_w"], 1).reshape(CONV_WIDTH, N_DEV, n_up)
    gmeta = g["meta_tokens"].reshape(N_META, N_DEV, n_meta)
    tiny = jnp.concatenate([
        jnp.pad(jnp.transpose(gcw, (1, 0, 2)), ((0, 0), (0, 8 - CONV_WIDTH), (0, TINY_COLS - n_up))),
        jnp.pad(jnp.transpose(gmeta, (1, 0, 2)), ((0, 0), (0, 0), (0, TINY_COLS - n_meta)))], axis=1)
    small_shapes = [w[n].shape for n in SMALL]
    rows_sm = _rows_for(small_shapes, extra=1)
    small = _pack([g[n].reshape(-1) for n in SMALL] + [loss.reshape(1)], rows_sm)
    blocks = [_unalign_axis0(g["w_in_t"]).reshape(N_DEV, n_in, d),
              _ffn_deinterleave(g["w_up_t"], 0).reshape(N_DEV, n_up, d),
              jnp.stack([g["w_a_t"].reshape(N_DEV, n_ab, -1), g["w_b_t"].reshape(N_DEV, n_ab, -1)], axis=1),
              g["w_out"].reshape(N_DEV, -1, d),
              g["w_down"].reshape(N_DEV, -1, d),
              tiny]
    r_in, r_up, r_ab, r_out, r_down, r_tiny, r_small = _exchange(blocks, [small], "exchange_grads")

    res = {}
    g_in_s = _slot_sum(r_in, "sum_w_in").T
    res["w_in"] = (g_in_s,) + tuple(_adamw(g_in_s, w_in[0], m_w_in[0], v_w_in[0], "adamw_w_in"))
    g_up_s = _slot_sum(r_up, "sum_w_up").T
    res["w_up"] = (g_up_s,) + tuple(_adamw(g_up_s, w_up[0], m_w_up[0], v_w_up[0], "adamw_w_up"))
    g_ab_s = jnp.swapaxes(_slot_sum(r_ab.reshape(N_DEV, 2 * n_ab, -1), "sum_w_ab").reshape(2, n_ab, -1), 1, 2)
    ab = lambda t: jnp.concatenate([t["w_branch_a"][0], t["w_branch_b"][0]], axis=0)
    o_ab = (g_ab_s.reshape(-1, n_ab),) + tuple(_adamw(g_ab_s.reshape(-1, n_ab), ab(w), ab(mom), ab(var), "adamw_w_ab"))
    half = o_ab[0].shape[0] // 2
    res["w_branch_a"] = tuple(o[:half] for o in o_ab)
    res["w_branch_b"] = tuple(o[half:] for o in o_ab)
    res["w_out"] = tuple(_sum_adamw(r_out, w_out[0], m_w_out[0], v_w_out[0], "adamw_w_out"))
    res["w_down"] = tuple(_sum_adamw(r_down, w_down[0], m_w_down[0], v_w_down[0], "adamw_w_down"))
    tp = lambda t: _tiny_pack(t["conv_w"][0], t["meta_tokens"])
    o_tiny = [_tiny_unpack(o, n_up, n_meta) for o in _sum_adamw(r_tiny, tp(w), tp(mom), tp(var), "adamw_tiny")]
    res["conv_w"] = tuple(o[0] for o in o_tiny)
    res["meta_tokens"] = tuple(o[1] for o in o_tiny)
    zero1 = jnp.zeros((1,), F32)
    sp = lambda t: _pack([t[n].reshape(-1) for n in SMALL] + [zero1], rows_sm)
    o_small = [_unpack(o, small_shapes + [(1,)]) for o in _sum_adamw(r_small, sp(w), sp(mom), sp(var), "adamw_small")]
    for i, n in enumerate(SMALL):
        res[n] = tuple(o[i] for o in o_small)
    loss_all = o_small[0][len(SMALL)].reshape(())

    result = [[res[n][k].reshape(w[n].shape) for n in ORDER] for k in range(4)]
    return (loss_all, grad_x, *result[0], *result[1], *result[2], *result[3])
```

```python
import jax
import jax.numpy as jnp
import numpy as np
from jax import lax
from jax.experimental import pallas as pl
from jax.experimental.pallas import tpu as pltpu

F32 = jnp.float32
BF16 = jnp.bfloat16

D_MODEL = 1024
N_META = 16
FOX_HEADS = 8
FOX_HEAD_DIM = 64
FOX_WIDTH = FOX_HEADS * FOX_HEAD_DIM
HG_HEADS = 4
HG_DIM = 128
HG_WIDTH = HG_HEADS * HG_DIM
D_FF = 2816
CONV_WIDTH = 3
EPS = 1e-6
IN_COLS = 3 * FOX_WIDTH + FOX_HEADS + 4 * HG_WIDTH + 2 * D_MODEL
N_DEV = 8

ADAM_LR = 0.001
ADAM_B1 = 0.9
ADAM_B2 = 0.999
ADAM_EPS = 1e-08
ADAM_WD = 0.01
ADAM_STEP = 10

LANES = 128
SEQ_BLOCK = 128
SUB = 16
NEG = -1e30
VMEM_LIMIT = 48 * 1024 * 1024

FOX_CB = 2 * D_MODEL // FOX_WIDTH
CB_HQ = (2 * D_MODEL + 3 * FOX_WIDTH) // LANES
CB_HF = CB_HQ + HG_HEADS
CB_HI = CB_HF + HG_HEADS
CB_HG = CB_HI + HG_HEADS
CB_FF = CB_HG + HG_HEADS


def _div_tile(n, target, mult):
    best = None
    for t in range(mult, min(n, target) + 1, mult):
        if n % t == 0:
            best = t
    if best is None:
        best = n
    return best


def _cp(*sem):
    return pltpu.CompilerParams(dimension_semantics=sem, vmem_limit_bytes=VMEM_LIMIT)


def _sigmoid(x):
    return 1.0 / (1.0 + jnp.exp(-x))


def _dot(a, b, dims, precision=None):
    return lax.dot_general(a, b, (dims, ((), ())), preferred_element_type=F32, precision=precision)


NN = ((1,), (0,))
NT = ((1,), (1,))
TN = ((0,), (0,))
HI = lax.Precision.HIGHEST


MATMUL_VMEM_BUDGET = 30 * 1024 * 1024
MATMUL_MAX_TILE = 2048


def _tile_options(n):
    return [t for t in range(LANES, min(n, MATMUL_MAX_TILE) + 1, LANES) if n % t == 0] or [n]


def _matmul_tiles(m, n, k, a_bytes, b_bytes, o_bytes, has_res):
    tk = _div_tile(k, MATMUL_MAX_TILE, LANES)
    best = None
    for tm in _tile_options(m):
        for tn in _tile_options(n):
            vmem = 2 * (tm * tk * a_bytes + tk * tn * b_bytes) + 2 * tm * tn * o_bytes
            vmem += tm * tn * 4 if tk < k else 0
            vmem += 2 * tm * tn * 4 if has_res else 0
            if vmem > MATMUL_VMEM_BUDGET:
                continue
            key = (tm * tn, tn % 256 == 0, tn)
            if best is None or key > best[0]:
                best = (key, tm, tn)
    assert best is not None, (m, n, k)
    return best[1], best[2], tk


def _matmul(a, b, mode, out_dtype, name, residual=None):
    if mode == "nn":
        (m, k), (k2, n) = a.shape, b.shape
    elif mode == "nt":
        (m, k), (n, k2) = a.shape, b.shape
    else:
        (k, m), (k2, n) = a.shape, b.shape
    assert k == k2, (a.shape, b.shape, mode)
    has_res = residual is not None
    tm, tn, tk = _matmul_tiles(m, n, k, a.dtype.itemsize, b.dtype.itemsize, jnp.dtype(out_dtype).itemsize, has_res)
    nk = k // tk
    if mode == "nn":
        a_spec = pl.BlockSpec((tm, tk), lambda i, j, kk: (i, kk))
        b_spec = pl.BlockSpec((tk, tn), lambda i, j, kk: (kk, j))
        dims = NN
    elif mode == "nt":
        a_spec = pl.BlockSpec((tm, tk), lambda i, j, kk: (i, kk))
        b_spec = pl.BlockSpec((tn, tk), lambda i, j, kk: (j, kk))
        dims = NT
    else:
        a_spec = pl.BlockSpec((tk, tm), lambda i, j, kk: (kk, i))
        b_spec = pl.BlockSpec((tk, tn), lambda i, j, kk: (kk, j))
        dims = TN
    o_spec = pl.BlockSpec((tm, tn), lambda i, j, kk: (i, j))

    def body(*refs):
        a_ref, b_ref = refs[0], refs[1]
        r_ref = refs[2] if has_res else None
        o_ref = refs[3] if has_res else refs[2]
        part = _dot(a_ref[...].astype(BF16), b_ref[...].astype(BF16), dims)
        if nk == 1:
            o_ref[...] = (part + r_ref[...] if has_res else part).astype(o_ref.dtype)
            return
        acc_ref = refs[-1]
        kk = pl.program_id(2)

        @pl.when(kk == 0)
        def _():
            acc_ref[...] = part

        @pl.when(kk > 0)
        def _():
            acc_ref[...] += part

        @pl.when(kk == nk - 1)
        def _():
            acc = acc_ref[...]
            if has_res:
                acc = acc + r_ref[...]
            o_ref[...] = acc.astype(o_ref.dtype)

    in_specs = [a_spec, b_spec] + ([o_spec] if has_res else [])
    args = (a, b) + ((residual,) if has_res else ())
    return pl.pallas_call(
        body, name=name, grid=(m // tm, n // tn, nk),
        in_specs=in_specs, out_specs=o_spec,
        out_shape=jax.ShapeDtypeStruct((m, n), out_dtype),
        scratch_shapes=[pltpu.VMEM((tm, tn), F32)] if nk > 1 else [],
        compiler_params=_cp("parallel", "parallel", "arbitrary"),
    )(*args)


def _rms_fwd(x, gain, name):
    m, d = x.shape
    tm = _div_tile(m, 512, 16)

    def body(x_ref, g_ref, o_ref):
        xv = x_ref[...]
        r = lax.rsqrt(jnp.mean(xv * xv, axis=-1, keepdims=True) + EPS)
        o_ref[...] = ((xv * r) * g_ref[...]).astype(o_ref.dtype)

    return pl.pallas_call(
        body, name=name, grid=(m // tm,),
        in_specs=[pl.BlockSpec((tm, d), lambda i: (i, 0)), pl.BlockSpec((1, d), lambda i: (0, 0))],
        out_specs=pl.BlockSpec((tm, d), lambda i: (i, 0)),
        out_shape=jax.ShapeDtypeStruct((m, d), BF16),
        compiler_params=_cp("parallel"),
    )(x, gain)


def _rms_bwd(x, gain, dy, dres, name):
    m, d = x.shape
    tm = _div_tile(m, 256, 8)

    def body(x_ref, g_ref, dy_ref, dr_ref, dx_ref, dg_ref):
        xv = x_ref[...]
        r = lax.rsqrt(jnp.mean(xv * xv, axis=-1, keepdims=True) + EPS)
        nv = xv * r
        dyv = dy_ref[...]
        gdy = dyv * g_ref[...]
        dx_ref[...] = dr_ref[...] + r * (gdy - nv * jnp.mean(gdy * nv, axis=-1, keepdims=True))
        part = jnp.sum(dyv * nv, axis=0, keepdims=True)

        @pl.when(pl.program_id(0) == 0)
        def _():
            dg_ref[...] = part

        @pl.when(pl.program_id(0) > 0)
        def _():
            dg_ref[...] += part

    row = pl.BlockSpec((tm, d), lambda i: (i, 0))
    vec = pl.BlockSpec((1, d), lambda i: (0, 0))
    return pl.pallas_call(
        body, name=name, grid=(m // tm,),
        in_specs=[row, vec, row, row], out_specs=[row, vec],
        out_shape=[jax.ShapeDtypeStruct((m, d), F32), jax.ShapeDtypeStruct((1, d), F32)],
        compiler_params=_cp("arbitrary"),
    )(x, gain, dy, dres)


def _head_stats(xv, lo):
    sq = xv * xv
    s_lo = jnp.sum(jnp.where(lo, sq, 0.0), axis=1, keepdims=True)
    s_hi = jnp.sum(jnp.where(lo, 0.0, sq), axis=1, keepdims=True)
    return jnp.where(lo, s_lo, s_hi) * (1.0 / FOX_HEAD_DIM)


BIAS_LANE = FOX_HEAD_DIM
N_SPLIT = 3


def _split3(c):
    c1 = c.astype(BF16).astype(F32)
    r1 = c - c1
    c2 = r1.astype(BF16).astype(F32)
    c3 = (r1 - c2).astype(BF16).astype(F32)
    return c1, c2, c3


def _fox_prep(proj, qg, kg, bf, nb, lp, name):
    m = proj.shape[0]
    ts = SEQ_BLOCK
    nblk = lp // ts
    scale = FOX_HEAD_DIM ** -0.5

    def body(q_ref, k_ref, v_ref, f_ref, qg_ref, kg_ref, bf_ref, qo_ref, ko_ref, vo_ref, carry_ref):
        lane = lax.broadcasted_iota(jnp.int32, (1, LANES), 1)
        lo = lane < FOX_HEAD_DIM

        @pl.when(pl.program_id(1) == 0)
        def _():
            carry_ref[...] = jnp.zeros_like(carry_ref)

        z = f_ref[...] + bf_ref[...]
        logf = jnp.minimum(z, 0.0) - jnp.log(1.0 + jnp.exp(-jnp.abs(z)))
        logf = jnp.where(lane < FOX_HEADS, logf, 0.0)
        r = lax.broadcasted_iota(jnp.int32, (ts, ts), 0)
        c = lax.broadcasted_iota(jnp.int32, (ts, ts), 1)
        tri = jnp.where(c <= r, 1.0, 0.0).astype(F32)
        cum = _dot(tri, logf, NN, HI) + carry_ref[...]
        carry_ref[...] = cum[ts - 1:ts, :]

        ones = jnp.where((lane >= BIAS_LANE + N_SPLIT) & (lane < BIAS_LANE + 2 * N_SPLIT), 1.0, 0.0)
        ones_k = jnp.where((lane >= BIAS_LANE) & (lane < BIAS_LANE + N_SPLIT), 1.0, 0.0)
        for j in range(FOX_WIDTH // LANES):
            cs = slice(j * LANES, (j + 1) * LANES)
            xq = q_ref[:, cs]
            yq = ((xq * lax.rsqrt(_head_stats(xq, lo) + EPS)) * qg_ref[:, cs]) * scale
            xk = k_ref[:, cs]
            yk = (xk * lax.rsqrt(_head_stats(xk, lo) + EPS)) * kg_ref[:, cs]
            for hh in range(2):
                h = 2 * j + hh
                pieces = _split3(_lane_pick(cum, lane, h))
                qb, kb = ones, ones_k
                for i, piece in enumerate(pieces):
                    qb = jnp.where(lane == BIAS_LANE + i, piece, qb)
                    kb = jnp.where(lane == BIAS_LANE + N_SPLIT + i, -piece, kb)
                yq_h = yq if hh == 0 else pltpu.roll(yq, FOX_HEAD_DIM, 1)
                yk_h = yk if hh == 0 else pltpu.roll(yk, FOX_HEAD_DIM, 1)
                hs = slice(h * LANES, (h + 1) * LANES)
                qo_ref[:, hs] = jnp.where(lo, yq_h, qb).astype(BF16)
                ko_ref[:, hs] = jnp.where(lo, yk_h, kb).astype(BF16)
        vo_ref[...] = v_ref[...].astype(BF16)

    w = FOX_WIDTH
    row = lambda b, i: (b * nblk + i, 0)
    return pl.pallas_call(
        body, name=name, grid=(nb, nblk),
        in_specs=[pl.BlockSpec((ts, w), lambda b, i: (b * nblk + i, FOX_CB)),
                  pl.BlockSpec((ts, w), lambda b, i: (b * nblk + i, FOX_CB + 1)),
                  pl.BlockSpec((ts, w), lambda b, i: (b * nblk + i, FOX_CB + 2)),
                  pl.BlockSpec((ts, LANES), lambda b, i: (b * nblk + i, CB_FF)),
                  pl.BlockSpec((1, w), lambda b, i: (0, 0)),
                  pl.BlockSpec((1, w), lambda b, i: (0, 0)),
                  pl.BlockSpec((1, LANES), lambda b, i: (0, 0))],
        out_specs=[pl.BlockSpec((ts, 2 * w), row), pl.BlockSpec((ts, 2 * w), row), pl.BlockSpec((ts, w), row)],
        out_shape=[jax.ShapeDtypeStruct((m, 2 * w), BF16)] * 2 + [jax.ShapeDtypeStruct((m, w), BF16)],
        scratch_shapes=[pltpu.VMEM((1, LANES), F32)],
        compiler_params=_cp("arbitrary", "arbitrary"),
    )(proj, proj, proj, proj, qg, kg, bf)


def _att_tile(lp):
    return 384 if (lp % 384 == 0 and lp > 384) else 128


def _lane_pick(blk, lane, idx):
    return jnp.sum(jnp.where(lane == idx, blk, 0.0), axis=1, keepdims=True)


def _head_masks():
    lane = lax.broadcasted_iota(jnp.int32, (1, LANES), 1)
    return lane, [(lane >= hh * FOX_HEAD_DIM) & (lane < (hh + 1) * FOX_HEAD_DIM) for hh in range(2)]


def _fox_fwd(qa, ka, vb, nb, lp, name):
    m = qa.shape[0]
    tq = _att_tile(lp)
    nq = lp // tq
    npair = FOX_WIDTH // LANES

    def body(q_ref, k_ref, v_ref, o_ref, lse_ref):
        qi = pl.program_id(2)
        lane, hmasks = _head_masks()
        zero16 = jnp.zeros((), BF16)
        causal = lax.broadcasted_iota(jnp.int32, (tq, 1), 0) >= lax.broadcasted_iota(jnp.int32, (1, tq), 1)
        o_tot = jnp.zeros((tq, LANES), F32)
        lse_out = jnp.zeros((tq, LANES), F32)
        for hh in range(2):
            hs = slice(hh * LANES, (hh + 1) * LANES)
            q = q_ref[:, hs]

            def tile(j, carry, diagonal, hs=hs, q=q, hmask=hmasks[hh]):
                mx, l, acc = carry
                k0 = pl.multiple_of(j * tq, tq)
                vz = jnp.where(hmask, v_ref[pl.ds(k0, tq), :], zero16)
                s = _dot(q, k_ref[pl.ds(k0, tq), hs], NT)
                if diagonal:
                    s = jnp.where(causal, s, NEG)
                m_new = jnp.maximum(mx, jnp.max(s, axis=1, keepdims=True))
                alpha = jnp.exp(mx - m_new)
                pe = jnp.exp(s - m_new)
                l = alpha * l + jnp.sum(pe, axis=1, keepdims=True)
                acc = alpha * acc + _dot(pe.astype(BF16), vz, NN)
                return m_new, l, acc

            init = (jnp.full((tq, 1), NEG, F32), jnp.zeros((tq, 1), F32), jnp.zeros((tq, LANES), F32))
            carry = lax.fori_loop(0, qi, lambda j, c, tile=tile: tile(j, c, False), init)
            mx, l, acc = tile(qi, carry, True)
            o_tot = o_tot + acc / l
            lse_out = jnp.where(lane == hh, mx + jnp.log(l), lse_out)
        o_ref[...] = o_tot
        lse_ref[...] = lse_out

    return pl.pallas_call(
        body, name=name, grid=(nb, npair, nq),
        in_specs=[pl.BlockSpec((tq, 2 * LANES), lambda b, p, i: (b * nq + i, p)),
                  pl.BlockSpec((lp, 2 * LANES), lambda b, p, i: (b, p)),
                  pl.BlockSpec((lp, LANES), lambda b, p, i: (b, p))],
        out_specs=[pl.BlockSpec((tq, LANES), lambda b, p, i: (b * nq + i, p)),
                   pl.BlockSpec((None, None, tq, LANES), lambda b, p, i: (b, p, i, 0))],
        out_shape=[jax.ShapeDtypeStruct((m, FOX_WIDTH), F32),
                   jax.ShapeDtypeStruct((nb, npair, lp, LANES), F32)],
        compiler_params=_cp("parallel", "parallel", "arbitrary"),
    )(qa, ka, vb)


def _fox_bwd(qa, ka, vb, do, o, lse, nb, lp, name):
    m = qa.shape[0]
    tq = _att_tile(lp)
    nq = lp // tq
    npair = FOX_WIDTH // LANES

    def body(k_ref, v_ref, q_ref, do_ref, o_ref, lse_ref, dq_ref, dk_ref, dv_ref, dc0_ref, dc1_ref):
        j = pl.program_id(2)
        lane, hmasks = _head_masks()
        zero16 = jnp.zeros((), BF16)
        causal = lax.broadcasted_iota(jnp.int32, (tq, 1), 0) >= lax.broadcasted_iota(jnp.int32, (1, tq), 1)

        @pl.when(j == 0)
        def _():
            dq_ref[...] = jnp.zeros_like(dq_ref)

        vv = v_ref[...]
        vzs = [jnp.where(hm, vv, zero16) for hm in hmasks]

        def tile(qi, carry, diagonal):
            dk0, dk1, dv, dc0, dc1 = carry
            q0 = pl.multiple_of(qi * tq, tq)
            dob16 = do_ref[pl.ds(q0, tq), :].astype(BF16)
            ob = o_ref[pl.ds(q0, tq), :]
            lseb = lse_ref[pl.ds(q0, tq), :]
            dks, dcs = [dk0, dk1], [dc0, dc1]
            for hh in range(2):
                hs = slice(hh * LANES, (hh + 1) * LANES)
                q = q_ref[pl.ds(q0, tq), hs]
                doz16 = jnp.where(hmasks[hh], dob16, zero16)
                delta = jnp.sum(doz16.astype(F32) * ob, axis=1, keepdims=True)
                s = _dot(q, k_ref[:, hs], NT) - _lane_pick(lseb, lane, hh)
                if diagonal:
                    s = jnp.where(causal, s, NEG)
                pm = jnp.exp(s)
                ds = pm * (_dot(doz16, vzs[hh], NT) - delta)
                ds16 = ds.astype(BF16)
                dv = dv + _dot(pm.astype(BF16), doz16, TN)
                dks[hh] = dks[hh] + _dot(ds16, q, TN)
                dq_ref[pl.ds(q0, tq), hs] += _dot(ds16, k_ref[:, hs], NN)
                dcs[hh] = dcs[hh] - jnp.sum(ds, axis=0, keepdims=True)
            return dks[0], dks[1], dv, dcs[0], dcs[1]

        zt = jnp.zeros((tq, LANES), F32)
        zr = jnp.zeros((1, tq), F32)
        carry = tile(j, (zt, zt, zt, zr, zr), True)
        dk0, dk1, dv, dc0, dc1 = lax.fori_loop(j + 1, nq, lambda qi, c: tile(qi, c, False), carry)
        dk_ref[:, :LANES] = dk0
        dk_ref[:, LANES:] = dk1
        dv_ref[...] = dv
        dc0_ref[...] = dc0
        dc1_ref[...] = dc1

    full2 = pl.BlockSpec((lp, 2 * LANES), lambda b, p, j: (b, p))
    full = pl.BlockSpec((lp, LANES), lambda b, p, j: (b, p))
    blk2 = pl.BlockSpec((tq, 2 * LANES), lambda b, p, j: (b * nq + j, p))
    blk = pl.BlockSpec((tq, LANES), lambda b, p, j: (b * nq + j, p))
    dcs = pl.BlockSpec((None, None, 1, tq), lambda b, p, j: (b, p, 0, j))
    return pl.pallas_call(
        body, name=name, grid=(nb, npair, nq),
        in_specs=[blk2, blk, full2, full, full,
                  pl.BlockSpec((None, None, lp, LANES), lambda b, p, j: (b, p, 0, 0))],
        out_specs=[full2, blk2, blk, dcs, dcs],
        out_shape=[jax.ShapeDtypeStruct((m, 2 * FOX_WIDTH), F32)] * 2 + [jax.ShapeDtypeStruct((m, FOX_WIDTH), F32)]
        + [jax.ShapeDtypeStruct((nb, npair, 1, lp), F32)] * 2,
        compiler_params=_cp("parallel", "parallel", "arbitrary"),
    )(ka, vb, qa, do, o, lse)


def _fox_prep_bwd(proj, dqa, dka, dv, dcum, qg, kg, bf, nb, lp, name):
    m = proj.shape[0]
    ts = SEQ_BLOCK
    nblk = lp // ts
    scale = FOX_HEAD_DIM ** -0.5
    w = FOX_WIDTH
    wo = 3 * w

    def body(q_ref, k_ref, f_ref, dq_ref, dk_ref, dv_ref, dc_ref, qg_ref, kg_ref, bf_ref,
             out_ref, dff_ref, dqg_ref, dkg_ref, dbf_ref, carry_ref):
        first = (pl.program_id(0) == 0) & (pl.program_id(1) == 0)
        lane = lax.broadcasted_iota(jnp.int32, (1, LANES), 1)
        lo = lane < FOX_HEAD_DIM

        @pl.when(first)
        def _():
            dqg_ref[...] = jnp.zeros_like(dqg_ref)
            dkg_ref[...] = jnp.zeros_like(dkg_ref)
            dbf_ref[...] = jnp.zeros_like(dbf_ref)

        def norm_bwd(x, g, dy):
            r = lax.rsqrt(_head_stats(x, lo) + EPS)
            nv = x * r
            gdy = dy * g
            prod = gdy * nv
            s_lo = jnp.sum(jnp.where(lo, prod, 0.0), axis=1, keepdims=True)
            s_hi = jnp.sum(jnp.where(lo, 0.0, prod), axis=1, keepdims=True)
            mean = jnp.where(lo, s_lo, s_hi) * (1.0 / FOX_HEAD_DIM)
            return r * (gdy - nv * mean), jnp.sum(dy * nv, axis=0, keepdims=True)

        def pair(d_ref, jj):
            even = d_ref[:, 2 * jj * LANES:(2 * jj + 1) * LANES]
            odd = d_ref[:, (2 * jj + 1) * LANES:(2 * jj + 2) * LANES]
            return jnp.where(lo, even, pltpu.roll(odd, FOX_HEAD_DIM, 1))

        for jj in range(w // LANES):
            cs = slice(jj * LANES, (jj + 1) * LANES)
            dx, dg = norm_bwd(q_ref[:, cs], qg_ref[:, cs], pair(dq_ref, jj) * scale)
            out_ref[:, cs] = dx.astype(BF16)
            dqg_ref[:, cs] += dg
            dx, dg = norm_bwd(k_ref[:, cs], kg_ref[:, cs], pair(dk_ref, jj))
            out_ref[:, w + jj * LANES:w + (jj + 1) * LANES] = dx.astype(BF16)
            dkg_ref[:, cs] += dg
        out_ref[:, 2 * w:3 * w] = dv_ref[...].astype(BF16)

        @pl.when(pl.program_id(1) == 0)
        def _():
            carry_ref[...] = jnp.zeros_like(carry_ref)

        dc = dc_ref[...]
        r = lax.broadcasted_iota(jnp.int32, (ts, ts), 0)
        c = lax.broadcasted_iota(jnp.int32, (ts, ts), 1)
        triu = jnp.where(c >= r, 1.0, 0.0).astype(F32)
        dlogf = _dot(triu, dc, NN, HI) + carry_ref[...]
        carry_ref[...] += jnp.sum(dc, axis=0, keepdims=True)
        z = f_ref[...] + bf_ref[...]
        dz = jnp.where(lane < FOX_HEADS, dlogf * _sigmoid(-z), 0.0)
        dff_ref[...] = dz.astype(BF16)
        dbf_ref[...] += jnp.sum(dz, axis=0, keepdims=True)

    rev = lambda b, i: (b * nblk + (nblk - 1 - i), 0)
    vec = lambda n: pl.BlockSpec((1, n), lambda b, i: (0, 0))
    return pl.pallas_call(
        body, name=name, grid=(nb, nblk),
        in_specs=[pl.BlockSpec((ts, w), lambda b, i: (b * nblk + (nblk - 1 - i), FOX_CB)),
                  pl.BlockSpec((ts, w), lambda b, i: (b * nblk + (nblk - 1 - i), FOX_CB + 1)),
                  pl.BlockSpec((ts, LANES), lambda b, i: (b * nblk + (nblk - 1 - i), CB_FF)),
                  pl.BlockSpec((ts, 2 * w), rev), pl.BlockSpec((ts, 2 * w), rev), pl.BlockSpec((ts, w), rev),
                  pl.BlockSpec((ts, LANES), rev), vec(w), vec(w), vec(LANES)],
        out_specs=[pl.BlockSpec((ts, wo), rev), pl.BlockSpec((ts, LANES), rev), vec(w), vec(w), vec(LANES)],
        out_shape=[jax.ShapeDtypeStruct((m, wo), BF16), jax.ShapeDtypeStruct((m, LANES), BF16),
                   jax.ShapeDtypeStruct((1, w), F32),
                   jax.ShapeDtypeStruct((1, w), F32), jax.ShapeDtypeStruct((1, LANES), F32)],
        scratch_shapes=[pltpu.VMEM((1, LANES), F32)],
        compiler_params=_cp("arbitrary", "arbitrary"),
    )(proj, proj, proj, dqa, dka, dv, dcum, qg, kg, bf)


def _chunk_masks():
    r = lax.broadcasted_iota(jnp.int32, (SEQ_BLOCK, SEQ_BLOCK), 0)
    c = lax.broadcasted_iota(jnp.int32, (SEQ_BLOCK, SEQ_BLOCK), 1)
    same = (r // SUB) == (c // SUB)
    return r, c, same


def _hg_gates(hf, lb):
    sg = _sigmoid(hf)
    f = lb + (1.0 - lb) * sg
    return sg, f, jnp.log(f), (1.0 - lb) * _sigmoid(-hf)


def _hg_intra_e(g_ref, base, t, srow):
    diff = g_ref[pl.ds(base + t, 1), :] - g_ref[pl.ds(base, SUB), :]
    return jnp.exp(jnp.where(srow <= t, diff, NEG))


def _hgrn_fwd(proj, lb, gain, nb, lp, name):
    m = proj.shape[0]
    tb = SEQ_BLOCK
    nblk = lp // tb
    ns = tb // SUB

    def body(q_ref, f_ref, i_ref, g_ref, lb_ref, gain_ref, oraw_ref, y_ref, ssave_ref,
             st_ref, g_scr, kin_scr, o_scr):
        @pl.when(pl.program_id(2) == 0)
        def _():
            st_ref[...] = jnp.zeros_like(st_ref)

        ssave_ref[...] = st_ref[...]
        lbv = lb_ref[...]
        _, _, lf, kin = _hg_gates(f_ref[...], lbv)
        r, c, same = _chunk_masks()
        ltri = jnp.where(same & (c <= r), 1.0, 0.0).astype(F32)
        lall = jnp.where(same, 1.0, 0.0).astype(F32)
        g = _dot(ltri, lf, NN, HI)
        gt = _dot(lall, lf, NN, HI)
        g_scr[...] = g
        kin_scr[...] = kin
        qv = q_ref[...]
        qg = (qv * jnp.exp(g)).astype(BF16)
        kg = (kin * jnp.exp(gt - g)).astype(BF16)
        et = jnp.exp(gt)
        srow = lax.broadcasted_iota(jnp.int32, (SUB, 1), 0)
        for cc in range(ns):
            base = cc * SUB
            sl = slice(base, base + SUB)
            st = st_ref[...]
            o_c = _dot(qg[sl], st.astype(BF16), NT)
            kc = kin_scr[sl, :]
            vc = i_ref[sl, :]
            for t in range(SUB):
                e = _hg_intra_e(g_scr, base, t, srow)
                a = jnp.sum((q_ref[pl.ds(base + t, 1), :] * kc) * e, axis=1, keepdims=True)
                ot = jnp.sum(a * vc, axis=0, keepdims=True)
                o_c = o_c + jnp.where(srow == t, ot, 0.0)
            o_scr[sl, :] = o_c
            st_ref[...] = et[base:base + 1, :] * st + _dot(vc.astype(BF16), kg[sl], TN)
        o = o_scr[...]
        oraw_ref[...] = o
        rr = lax.rsqrt(jnp.mean(o * o, axis=-1, keepdims=True) + EPS)
        hg = g_ref[...]
        y_ref[...] = (((o * rr) * gain_ref[...]) * (hg * _sigmoid(hg))).astype(y_ref.dtype)

    col = lambda cb: pl.BlockSpec((tb, LANES), lambda b, h, i, cb=cb: (b * nblk + i, cb + h))
    out_blk = pl.BlockSpec((tb, LANES), lambda b, h, i: (b * nblk + i, h))
    return pl.pallas_call(
        body, name=name, grid=(nb, HG_HEADS, nblk),
        in_specs=[col(CB_HQ), col(CB_HF), col(CB_HI), col(CB_HG),
                  pl.BlockSpec((1, LANES), lambda b, h, i: (0, h)),
                  pl.BlockSpec((1, LANES), lambda b, h, i: (0, 0))],
        out_specs=[out_blk, out_blk,
                   pl.BlockSpec((None, None, None, HG_DIM, HG_DIM), lambda b, h, i: (b, h, i, 0, 0))],
        out_shape=[jax.ShapeDtypeStruct((m, HG_WIDTH), F32), jax.ShapeDtypeStruct((m, HG_WIDTH), BF16),
                   jax.ShapeDtypeStruct((nb, HG_HEADS, nblk, HG_DIM, HG_DIM), F32)],
        scratch_shapes=[pltpu.VMEM((HG_DIM, HG_DIM), F32), pltpu.VMEM((tb, LANES), F32),
                        pltpu.VMEM((tb, LANES), F32), pltpu.VMEM((tb, LANES), F32)],
        compiler_params=_cp("parallel", "parallel", "arbitrary"),
    )(proj, proj, proj, proj, lb, gain)


def _hgrn_bwd(proj, oraw, ssave, dy, lb, gain, nb, lp, name):
    m = proj.shape[0]
    tb = SEQ_BLOCK
    nblk = lp // tb
    ns = tb // SUB

    def body(q_ref, f_ref, i_ref, g_ref, oraw_ref, ssave_ref, dy_ref, lb_ref, gain_ref,
             dq_ref, df_ref, di_ref, dg_ref, dgain_ref, dlb_ref,
             dst_ref, sts_ref, g_scr, kin_scr, do_scr, dq_scr, dk_scr, dv_scr, dgg_scr):
        hd = pl.program_id(0)
        bb = pl.program_id(1)
        ii = pl.program_id(2)
        gainv = gain_ref[...]
        lbv = lb_ref[...]

        @pl.when((hd == 0) & (bb == 0) & (ii == 0))
        def _():
            dgain_ref[...] = jnp.zeros_like(dgain_ref)

        @pl.when((bb == 0) & (ii == 0))
        def _():
            dlb_ref[...] = jnp.zeros_like(dlb_ref)

        @pl.when(ii == 0)
        def _():
            dst_ref[...] = jnp.zeros_like(dst_ref)

        o = oraw_ref[...]
        rr = lax.rsqrt(jnp.mean(o * o, axis=-1, keepdims=True) + EPS)
        nv = o * rr
        hg = g_ref[...]
        sgg = _sigmoid(hg)
        sil = hg * sgg
        dyv = dy_ref[...]
        dg_ref[...] = (dyv * nv * gainv * (sgg * (1.0 + hg * (1.0 - sgg)))).astype(dg_ref.dtype)
        dgain_ref[...] += jnp.sum(dyv * nv * sil, axis=0, keepdims=True)
        dn = dyv * gainv * sil
        do_scr[...] = rr * (dn - nv * jnp.mean(dn * nv, axis=-1, keepdims=True))

        hf = f_ref[...]
        sg, f, lf, kin = _hg_gates(hf, lbv)
        r, c, same = _chunk_masks()
        ltri = jnp.where(same & (c <= r), 1.0, 0.0).astype(F32)
        lall = jnp.where(same, 1.0, 0.0).astype(F32)
        g = _dot(ltri, lf, NN, HI)
        gt = _dot(lall, lf, NN, HI)
        g_scr[...] = g
        kin_scr[...] = kin
        qv = q_ref[...]
        eg = jnp.exp(g)
        ekg = jnp.exp(gt - g)
        qg = qv * eg
        kg = kin * ekg
        qg16 = qg.astype(BF16)
        kg16 = kg.astype(BF16)
        et = jnp.exp(gt)
        st = ssave_ref[...]
        for cc in range(ns):
            sl = slice(cc * SUB, (cc + 1) * SUB)
            sts_ref[cc] = st
            st = et[cc * SUB:cc * SUB + 1, :] * st + _dot(i_ref[sl, :].astype(BF16), kg16[sl], TN)

        srow = lax.broadcasted_iota(jnp.int32, (SUB, 1), 0)
        for cc in reversed(range(ns)):
            base = cc * SUB
            sl = slice(base, base + SUB)
            st = sts_ref[cc]
            st16 = st.astype(BF16)
            dst = dst_ref[...]
            dst16 = dst.astype(BF16)
            doc = do_scr[sl, :]
            doc16 = doc.astype(BF16)
            vc = i_ref[sl, :]
            vc16 = vc.astype(BF16)
            kc = kin_scr[sl, :]
            etc = et[base:base + 1, :]
            dqg = _dot(doc16, st16, NN)
            dst_o = _dot(doc16, qg16[sl], TN)
            dv_c = _dot(kg16[sl], dst16, NT)
            dkg = _dot(vc16, dst16, NN)
            dgt = jnp.sum(dst * st, axis=0, keepdims=True) * etc
            dst_ref[...] = etc * dst + dst_o
            dq_c = dqg * eg[sl]
            dk_c = dkg * ekg[sl]
            dg_c = dqg * qg[sl] - dkg * kg[sl]
            dgt = dgt + jnp.sum(dkg * kg[sl], axis=0, keepdims=True)
            for t in range(SUB):
                e = _hg_intra_e(g_scr, base, t, srow)
                qt = q_ref[pl.ds(base + t, 1), :]
                dot_t = do_scr[pl.ds(base + t, 1), :]
                a = jnp.sum((qt * kc) * e, axis=1, keepdims=True)
                da = jnp.sum(dot_t * vc, axis=1, keepdims=True)
                dv_c = dv_c + a * dot_t
                w = da * e
                dq_t = jnp.sum(w * kc, axis=0, keepdims=True)
                wq = w * qt
                dk_c = dk_c + wq
                dg_c = dg_c - kc * wq + jnp.where(srow == t, qt * dq_t, 0.0)
                dq_c = dq_c + jnp.where(srow == t, dq_t, 0.0)
            dg_c = dg_c + jnp.where(srow == SUB - 1, dgt, 0.0)
            dq_scr[sl, :] = dq_c
            dk_scr[sl, :] = dk_c
            dv_scr[sl, :] = dv_c
            dgg_scr[sl, :] = dg_c

        utri = jnp.where(same & (c >= r), 1.0, 0.0).astype(F32)
        dlf = _dot(utri, dgg_scr[...], NN, HI)
        dkin = dk_scr[...]
        dsg = sg * (1.0 - sg)
        df_ref[...] = ((dlf / f - dkin) * ((1.0 - lbv) * dsg)).astype(df_ref.dtype)
        dlb_ref[...] += jnp.sum((dlf / f - dkin) * (1.0 - sg), axis=0, keepdims=True)
        dq_ref[...] = dq_scr[...].astype(dq_ref.dtype)
        di_ref[...] = dv_scr[...].astype(di_ref.dtype)

    rowi = lambda b, i: b * nblk + (nblk - 1 - i)
    col = lambda cb: pl.BlockSpec((tb, LANES), lambda h, b, i, cb=cb: (rowi(b, i), cb + h))
    hblk = pl.BlockSpec((tb, LANES), lambda h, b, i: (rowi(b, i), h))
    return pl.pallas_call(
        body, name=name, grid=(HG_HEADS, nb, nblk),
        in_specs=[col(CB_HQ), col(CB_HF), col(CB_HI), col(CB_HG), hblk,
                  pl.BlockSpec((None, None, None, HG_DIM, HG_DIM), lambda h, b, i: (b, h, nblk - 1 - i, 0, 0)),
                  hblk,
                  pl.BlockSpec((1, LANES), lambda h, b, i: (0, h)),
                  pl.BlockSpec((1, LANES), lambda h, b, i: (0, 0))],
        out_specs=[hblk, hblk, hblk, hblk,
                   pl.BlockSpec((1, LANES), lambda h, b, i: (0, 0)),
                   pl.BlockSpec((1, LANES), lambda h, b, i: (0, h))],
        out_shape=[jax.ShapeDtypeStruct((m, HG_WIDTH), BF16)] * 4
        + [jax.ShapeDtypeStruct((1, LANES), F32), jax.ShapeDtypeStruct((1, HG_WIDTH), F32)],
        scratch_shapes=[pltpu.VMEM((HG_DIM, HG_DIM), F32), pltpu.VMEM((ns, HG_DIM, HG_DIM), F32)]
        + [pltpu.VMEM((tb, LANES), F32)] * 7,
        compiler_params=_cp("arbitrary", "arbitrary", "arbitrary"),
    )(proj, proj, proj, proj, oraw, ssave, dy, lb, gain)


def _gate_fwd(proj, ya, yb, name):
    m = proj.shape[0]
    tm = _div_tile(m, 256, 16)

    def body(ga_ref, gb_ref, ya_ref, yb_ref, o_ref):
        o_ref[...] = (_sigmoid(ga_ref[...]) * ya_ref[...] + _sigmoid(gb_ref[...]) * yb_ref[...]).astype(o_ref.dtype)

    row = pl.BlockSpec((tm, D_MODEL), lambda i: (i, 0))
    return pl.pallas_call(
        body, name=name, grid=(m // tm,),
        in_specs=[row, pl.BlockSpec((tm, D_MODEL), lambda i: (i, 1)), row, row],
        out_specs=row, out_shape=jax.ShapeDtypeStruct((m, D_MODEL), BF16),
        compiler_params=_cp("parallel"),
    )(proj, proj, ya, yb)


def _gate_bwd(proj, ya, yb, dm, name):
    m = proj.shape[0]
    tm = _div_tile(m, 256, 16)

    def body(ga_ref, gb_ref, ya_ref, yb_ref, dm_ref, dya_ref, dyb_ref, dg_ref):
        dmv = dm_ref[...]
        sa = _sigmoid(ga_ref[...])
        sb = _sigmoid(gb_ref[...])
        dya_ref[...] = (dmv * sa).astype(BF16)
        dyb_ref[...] = (dmv * sb).astype(BF16)
        dg_ref[:, :D_MODEL] = (dmv * ya_ref[...] * (sa * (1.0 - sa))).astype(BF16)
        dg_ref[:, D_MODEL:] = (dmv * yb_ref[...] * (sb * (1.0 - sb))).astype(BF16)

    row = pl.BlockSpec((tm, D_MODEL), lambda i: (i, 0))
    wide = pl.BlockSpec((tm, 2 * D_MODEL), lambda i: (i, 0))
    return pl.pallas_call(
        body, name=name, grid=(m // tm,),
        in_specs=[row, pl.BlockSpec((tm, D_MODEL), lambda i: (i, 1)), row, row, row],
        out_specs=[row, row, wide],
        out_shape=[jax.ShapeDtypeStruct((m, D_MODEL), BF16)] * 2 + [jax.ShapeDtypeStruct((m, 2 * D_MODEL), BF16)],
        compiler_params=_cp("parallel"),
    )(proj, proj, ya, yb, dm)


CONV_ROWS = 128


def _conv3(x, xprev, w_ref, b_ref, rowi):
    r = x.shape[0]
    x1 = jnp.where(rowi < 1, pltpu.roll(xprev, 1, 0), pltpu.roll(x, 1, 0))
    x2 = jnp.where(rowi < 2, pltpu.roll(xprev, 2, 0), pltpu.roll(x, 2, 0))
    u = w_ref[0:1, :] * x2 + w_ref[1:2, :] * x1 + w_ref[2:3, :] * x + b_ref[...]
    return u, x1, x2


def _conv_fwd(up, cw, cb, nb, lp, name):
    m = up.shape[0]
    nct = D_FF // LANES
    r = CONV_ROWS
    nch = lp // r

    def body(u_ref, w_ref, b_ref, o_ref):
        rowi = lax.broadcasted_iota(jnp.int32, (r, 1), 0)

        def step(i, xp):
            r0 = pl.multiple_of(i * r, r)
            xc = u_ref[pl.ds(r0, r), :]
            u, _, _ = _conv3(xc, xp, w_ref, b_ref, rowi)
            ug, uv = u[:, :LANES], u[:, LANES:]
            o_ref[pl.ds(r0, r), :] = ((ug * _sigmoid(ug)) * uv).astype(o_ref.dtype)
            return xc

        lax.fori_loop(0, nch, step, jnp.zeros((r, 2 * LANES), F32))

    return pl.pallas_call(
        body, name=name, grid=(nb, nct),
        in_specs=[pl.BlockSpec((lp, 2 * LANES), lambda b, c: (b, c)),
                  pl.BlockSpec((CONV_WIDTH, 2 * LANES), lambda b, c: (0, c)),
                  pl.BlockSpec((1, 2 * LANES), lambda b, c: (0, c))],
        out_specs=pl.BlockSpec((lp, LANES), lambda b, c: (b, c)),
        out_shape=jax.ShapeDtypeStruct((m, D_FF), BF16),
        compiler_params=_cp("parallel", "parallel"),
    )(up, cw, cb)


def _conv_bwd(up, dact, cw, cb, nb, lp, name):
    m = up.shape[0]
    nct = D_FF // LANES
    r = CONV_ROWS
    nch = lp // r

    def body(u_ref, da_ref, w_ref, b_ref, dup_ref, dw_ref, db_ref):
        rowi = lax.broadcasted_iota(jnp.int32, (r, 1), 0)
        wv = w_ref[...]

        def step(k, carry):
            dun, dw0, dw1, dw2, dbs = carry
            i = nch - 1 - k
            r0 = pl.multiple_of(i * r, r)
            rp = pl.multiple_of(jnp.maximum(i - 1, 0) * r, r)
            xc = u_ref[pl.ds(r0, r), :]
            xp = u_ref[pl.ds(rp, r), :] * (i > 0).astype(F32)
            u, x1, x2 = _conv3(xc, xp, w_ref, b_ref, rowi)
            ug, uv = u[:, :LANES], u[:, LANES:]
            da = da_ref[pl.ds(r0, r), :]
            sg = _sigmoid(ug)
            du = jnp.concatenate([da * uv * (sg * (1.0 + ug * (1.0 - sg))), da * (ug * sg)], axis=1)
            d1 = jnp.where(rowi >= r - 1, pltpu.roll(dun, r - 1, 0), pltpu.roll(du, r - 1, 0))
            d2 = jnp.where(rowi >= r - 2, pltpu.roll(dun, r - 2, 0), pltpu.roll(du, r - 2, 0))
            dup_ref[pl.ds(r0, r), :] = (wv[2:3, :] * du + wv[1:2, :] * d1 + wv[0:1, :] * d2).astype(dup_ref.dtype)
            dw0 = dw0 + jnp.sum(du * x2, axis=0, keepdims=True)
            dw1 = dw1 + jnp.sum(du * x1, axis=0, keepdims=True)
            dw2 = dw2 + jnp.sum(du * xc, axis=0, keepdims=True)
            dbs = dbs + jnp.sum(du, axis=0, keepdims=True)
            return du, dw0, dw1, dw2, dbs

        z1 = jnp.zeros((1, 2 * LANES), F32)
        _, dw0, dw1, dw2, dbs = lax.fori_loop(0, nch, step, (jnp.zeros((r, 2 * LANES), F32), z1, z1, z1, z1))

        @pl.when(pl.program_id(1) == 0)
        def _():
            dw_ref[...] = jnp.zeros_like(dw_ref)
            db_ref[...] = jnp.zeros_like(db_ref)

        dw_ref[0:1, :] += dw0
        dw_ref[1:2, :] += dw1
        dw_ref[2:3, :] += dw2
        db_ref[...] += dbs

    return pl.pallas_call(
        body, name=name, grid=(nct, nb),
        in_specs=[pl.BlockSpec((lp, 2 * LANES), lambda c, b: (b, c)),
                  pl.BlockSpec((lp, LANES), lambda c, b: (b, c)),
                  pl.BlockSpec((CONV_WIDTH, 2 * LANES), lambda c, b: (0, c)),
                  pl.BlockSpec((1, 2 * LANES), lambda c, b: (0, c))],
        out_specs=[pl.BlockSpec((lp, 2 * LANES), lambda c, b: (b, c)),
                   pl.BlockSpec((CONV_WIDTH, 2 * LANES), lambda c, b: (0, c)),
                   pl.BlockSpec((1, 2 * LANES), lambda c, b: (0, c))],
        out_shape=[jax.ShapeDtypeStruct((m, 2 * D_FF), BF16),
                   jax.ShapeDtypeStruct((CONV_WIDTH, 2 * D_FF), F32),
                   jax.ShapeDtypeStruct((1, 2 * D_FF), F32)],
        compiler_params=_cp("parallel", "arbitrary"),
    )(up, dact, cw, cb)


def _ffn_interleave(a, axis):
    shp = a.shape
    a = a.reshape(shp[:axis] + (2, D_FF // LANES, LANES) + shp[axis + 1:])
    return jnp.swapaxes(a, axis, axis + 1).reshape(shp)


def _ffn_deinterleave(a, axis):
    shp = a.shape
    a = a.reshape(shp[:axis] + (D_FF // LANES, 2, LANES) + shp[axis + 1:])
    return jnp.swapaxes(a, axis, axis + 1).reshape(shp)


def _shifted_rows(prev_ref, cur_ref):
    keep = SEQ_BLOCK - N_META
    return jnp.concatenate([prev_ref[keep:, :], cur_ref[:keep, :]], axis=0)


def _frame_specs(nblk, nfb, d):
    prev = pl.BlockSpec((SEQ_BLOCK, d), lambda b, i: (b * nfb + jnp.clip(i - 1, 0, nfb - 1), 0))
    cur = pl.BlockSpec((SEQ_BLOCK, d), lambda b, i: (b * nfb + jnp.clip(i, 0, nfb - 1), 0))
    return prev, cur


def _embed_rms(x2, meta, gain, nb, lp, l, name):
    d = x2.shape[1]
    tr = SEQ_BLOCK
    nblk = lp // tr
    nfb = (l - N_META) // tr
    m = nb * lp

    def body(prev_ref, cur_ref, meta_ref, g_ref, h_ref, o_ref):
        i = pl.program_id(1)
        t = i * tr + lax.broadcasted_iota(jnp.int32, (tr, 1), 0)
        rows = jnp.where(t < l, _shifted_rows(prev_ref, cur_ref), 0.0)
        head = jnp.concatenate([meta_ref[...], jnp.zeros((tr - N_META, d), F32)], axis=0)
        xv = jnp.where(t < N_META, head, rows)
        h_ref[...] = xv
        r = lax.rsqrt(jnp.mean(xv * xv, axis=-1, keepdims=True) + EPS)
        o_ref[...] = ((xv * r) * g_ref[...]).astype(o_ref.dtype)

    prev, cur = _frame_specs(nblk, nfb, d)
    row = pl.BlockSpec((tr, d), lambda b, i: (b * nblk + i, 0))
    return pl.pallas_call(
        body, name=name, grid=(nb, nblk),
        in_specs=[prev, cur, pl.BlockSpec((N_META, d), lambda b, i: (0, 0)), pl.BlockSpec((1, d), lambda b, i: (0, 0))],
        out_specs=[row, row],
        out_shape=[jax.ShapeDtypeStruct((m, d), F32), jax.ShapeDtypeStruct((m, d), BF16)],
        compiler_params=_cp("parallel", "parallel"),
    )(x2, x2, meta, gain)


def _loss_head(out, tgt2, nb, lp, l, name):
    m, d = out.shape
    tr = SEQ_BLOCK
    nblk = lp // tr
    nfb = (l - N_META) // tr

    def body(o_ref, prev_ref, cur_ref, dy_ref, ls_ref):
        t = pl.program_id(1) * tr + lax.broadcasted_iota(jnp.int32, (tr, 1), 0)
        valid = (t >= N_META) & (t < l)
        err = jnp.where(valid, o_ref[...] - _shifted_rows(prev_ref, cur_ref), 0.0)
        dy_ref[...] = err * (1.0 / d)
        part = jnp.sum(err * err, axis=0, keepdims=True)
        first = (pl.program_id(0) == 0) & (pl.program_id(1) == 0)

        @pl.when(first)
        def _():
            ls_ref[...] = part

        @pl.when(jnp.logical_not(first))
        def _():
            ls_ref[...] += part

    prev, cur = _frame_specs(nblk, nfb, d)
    row = pl.BlockSpec((tr, d), lambda b, i: (b * nblk + i, 0))
    return pl.pallas_call(
        body, name=name, grid=(nb, nblk),
        in_specs=[row, prev, cur], out_specs=[row, pl.BlockSpec((1, d), lambda b, i: (0, 0))],
        out_shape=[jax.ShapeDtypeStruct((m, d), F32), jax.ShapeDtypeStruct((1, d), F32)],
        compiler_params=_cp("arbitrary", "arbitrary"),
    )(out, tgt2, tgt2)


def _adam_math(g, w, mom, var):
    c1 = 1.0 - ADAM_B1 ** ADAM_STEP
    c2 = 1.0 - ADAM_B2 ** ADAM_STEP
    mn = ADAM_B1 * mom + (1.0 - ADAM_B1) * g
    vn = ADAM_B2 * var + (1.0 - ADAM_B2) * (g * g)
    delta = -ADAM_LR * ((mn / c1) / (jnp.sqrt(vn / c2) + ADAM_EPS) + ADAM_WD * w)
    return delta, mn, vn


def _slot_sum(recv, name):
    _, r, c = recv.shape
    tc = _div_tile(c, 256, LANES)

    def body(r_ref, g_ref):
        g = r_ref[0].astype(F32)
        for s in range(1, N_DEV):
            g = g + r_ref[s].astype(F32)
        g_ref[...] = g

    return pl.pallas_call(
        body, name=name, grid=(c // tc,),
        in_specs=[pl.BlockSpec((N_DEV, r, tc), lambda j: (0, 0, j))],
        out_specs=pl.BlockSpec((r, tc), lambda j: (0, j)),
        out_shape=jax.ShapeDtypeStruct((r, c), F32),
        compiler_params=_cp("parallel"),
    )(recv)


def _adamw(g, w, mom, var, name):
    r, c = w.shape
    tr = _div_tile(r, 256, 8)

    def body(g_ref, w_ref, m_ref, v_ref, d_ref, mo_ref, vo_ref):
        d_ref[...], mo_ref[...], vo_ref[...] = _adam_math(g_ref[...], w_ref[...], m_ref[...], v_ref[...])

    row = pl.BlockSpec((tr, c), lambda i: (i, 0))
    return pl.pallas_call(
        body, name=name, grid=(r // tr,), in_specs=[row] * 4, out_specs=[row] * 3,
        out_shape=[jax.ShapeDtypeStruct((r, c), F32)] * 3,
        compiler_params=_cp("parallel"),
    )(g, w, mom, var)


def _sum_adamw(recv, w, mom, var, name):
    r, c = w.shape
    tr = _div_tile(r, 256, 8)

    def body(r_ref, w_ref, m_ref, v_ref, g_ref, d_ref, mo_ref, vo_ref):
        g = r_ref[0].astype(F32)
        for s in range(1, N_DEV):
            g = g + r_ref[s].astype(F32)
        g_ref[...] = g
        d_ref[...], mo_ref[...], vo_ref[...] = _adam_math(g, w_ref[...], m_ref[...], v_ref[...])

    row = pl.BlockSpec((tr, c), lambda i: (i, 0))
    return pl.pallas_call(
        body, name=name, grid=(r // tr,),
        in_specs=[pl.BlockSpec((N_DEV, tr, c), lambda i: (0, i, 0)), row, row, row],
        out_specs=[row] * 4,
        out_shape=[jax.ShapeDtypeStruct((r, c), F32)] * 4,
        compiler_params=_cp("parallel"),
    )(recv, w, mom, var)


_MESH = pl.DeviceIdType.MESH
_HBM = pl.BlockSpec(memory_space=pltpu.HBM)
N_PEER = N_DEV - 1


def _position():
    return lax.axis_index("x"), lax.axis_index("y"), lax.axis_index("c")


def _all_gather(shards, name):
    n = len(shards)

    def body(*refs):
        x_refs, out_refs = refs[:n], refs[n:2 * n]
        send_sems, recv_sems, local_sems = refs[2 * n:]
        x, y, c = _position()
        me, sibling = (x, y, c), (x, y, 1 - c)
        chips = [(1 - x, y), (x, 1 - y), (1 - x, 1 - y)]

        def copy(a, k, block, to, src=None):
            slot = out_refs[a].at[4 * block[0] + 2 * block[1] + block[2]]
            return pltpu.make_async_remote_copy(
                src_ref=slot if src is None else src, dst_ref=slot,
                send_sem=send_sems.at[a * N_PEER + k], recv_sem=recv_sems.at[a * N_PEER + k],
                device_id=to, device_id_type=_MESH)

        mine, sent = [], []
        for a in range(n):
            cp = pltpu.make_async_copy(x_refs[a], out_refs[a].at[4 * x + 2 * y + c], local_sems.at[a])
            cp.start()
            mine.append(cp)
            first = [copy(a, 0, me, sibling, src=x_refs[a])]
            first += [copy(a, 1 + j, me, (*chip, c), src=x_refs[a]) for j, chip in enumerate(chips)]
            for cp in first:
                cp.start()
            sent += first
        for a in range(n):
            for j, chip in enumerate(chips):
                copy(a, 1 + j, (*chip, c), me).wait_recv()
                fwd = copy(a, 4 + j, (*chip, c), sibling)
                fwd.start()
                sent.append(fwd)
        for a in range(n):
            copy(a, 0, sibling, me).wait_recv()
            for j, chip in enumerate(chips):
                copy(a, 4 + j, (*chip, 1 - c), me).wait_recv()
        for cp in sent:
            cp.wait_send()
        for cp in mine:
            cp.wait()

    return pl.pallas_call(
        body, name=name,
        out_shape=[jax.ShapeDtypeStruct((N_DEV,) + a.shape, a.dtype) for a in shards],
        in_specs=[_HBM] * n, out_specs=[_HBM] * n,
        scratch_shapes=[pltpu.SemaphoreType.DMA((n * N_PEER,)), pltpu.SemaphoreType.DMA((n * N_PEER,)),
                        pltpu.SemaphoreType.DMA((n,))],
    )(*shards)


def _exchange(blocks, shared, name):
    flips = [(fx, fy, fc) for fx in (0, 1) for fy in (0, 1) for fc in (0, 1)][1:]
    nblk, n = len(blocks), len(blocks) + len(shared)

    def body(*refs):
        in_refs, out_refs = refs[:n], refs[n:2 * n]
        send_sems, recv_sems, local_sems = refs[2 * n:]
        x, y, c = _position()
        me = 4 * x + 2 * y + c

        def peer(f):
            return (1 - x if f[0] else x, 1 - y if f[1] else y, 1 - c if f[2] else c)

        def idx(p):
            return 4 * p[0] + 2 * p[1] + p[2]

        def src(a, p):
            return in_refs[a].at[idx(p)] if a < nblk else in_refs[a]

        mine, sends = [], []
        for a in range(n):
            cp = pltpu.make_async_copy(in_refs[a].at[me] if a < nblk else in_refs[a], out_refs[a].at[me], local_sems.at[a])
            cp.start()
            mine.append(cp)
            for k, f in enumerate(flips):
                p = peer(f)
                cp = pltpu.make_async_remote_copy(
                    src_ref=src(a, p), dst_ref=out_refs[a].at[me],
                    send_sem=send_sems.at[a * N_PEER + k], recv_sem=recv_sems.at[a * N_PEER + k],
                    device_id=p, device_id_type=_MESH)
                cp.start()
                sends.append(cp)
        for a in range(n):
            for k, f in enumerate(flips):
                p = peer(f)
                pltpu.make_async_remote_copy(
                    src_ref=src(a, p), dst_ref=out_refs[a].at[idx(p)],
                    send_sem=send_sems.at[a * N_PEER + k], recv_sem=recv_sems.at[a * N_PEER + k],
                    device_id=p, device_id_type=_MESH).wait_recv()
        for cp in sends:
            cp.wait_send()
        for cp in mine:
            cp.wait()

    arrays = list(blocks) + list(shared)
    out_shape = [jax.ShapeDtypeStruct(a.shape, a.dtype) for a in blocks]
    out_shape += [jax.ShapeDtypeStruct((N_DEV,) + a.shape, a.dtype) for a in shared]
    return pl.pallas_call(
        body, name=name, out_shape=out_shape,
        in_specs=[_HBM] * n, out_specs=[_HBM] * n,
        scratch_shapes=[pltpu.SemaphoreType.DMA((n * N_PEER,)), pltpu.SemaphoreType.DMA((n * N_PEER,)),
                        pltpu.SemaphoreType.DMA((n,))],
    )(*arrays)


def _pack(parts, rows):
    flat = jnp.concatenate(parts, axis=-1)
    return jnp.pad(flat, [(0, rows * LANES - flat.shape[-1])]).reshape(rows, LANES)


def _unpack(packed, shapes):
    flat = packed.reshape(-1)
    out, off = [], 0
    for shp in shapes:
        n = int(np.prod(shp))
        out.append(flat[off:off + n].reshape(shp))
        off += n
    return out


def _rows_for(shapes, extra=0):
    n = sum(int(np.prod(s)) for s in shapes) + extra
    return -(-n // (8 * LANES)) * 8


def _lower_bound(logits):
    return jnp.cumsum(jax.nn.softmax(logits.astype(F32), axis=0), axis=0)[0:1]


def _align_axis0(w):
    a, b = 3 * FOX_WIDTH, 3 * FOX_WIDTH + FOX_HEADS
    c = b + 4 * HG_WIDTH
    pad = [(0, LANES - FOX_HEADS)] + [(0, 0)] * (w.ndim - 1)
    return jnp.concatenate([w[c:], w[:a], w[b:c], jnp.pad(w[a:b], pad)], axis=0)


def _unalign_axis0(g):
    a, b = 2 * D_MODEL, 2 * D_MODEL + 3 * FOX_WIDTH
    c = b + 4 * HG_WIDTH
    return jnp.concatenate([g[a:b], g[c:c + FOX_HEADS], g[b:c], g[:a]], axis=0)


TINY_COLS = 768


def _tiny_pack(conv_w_shard, meta_shard):
    cw = jnp.pad(conv_w_shard, ((0, 8 - CONV_WIDTH), (0, TINY_COLS - conv_w_shard.shape[1])))
    mt = jnp.pad(meta_shard, ((0, 0), (0, TINY_COLS - meta_shard.shape[1])))
    return jnp.concatenate([cw, mt], axis=0)


def _tiny_unpack(t, ncw, nmeta):
    return t[..., :CONV_WIDTH, :ncw], t[..., 8:8 + N_META, :nmeta]


def _local_step(x, target, meta, norm1_gain, w_in_t, fox_b_f, q_gain, k_gain, lb, hg_out_gain, w_a_t, w_b_t, w_out,
                norm2_gain, w_up_t, conv_w, conv_b, w_down):
    nb, seq, d = x.shape
    assert seq % SEQ_BLOCK == 0 and N_META < SEQ_BLOCK
    l = seq + N_META
    lp = -(-l // SEQ_BLOCK) * SEQ_BLOCK
    m = nb * lp
    qg = jnp.tile(q_gain, (1, FOX_HEADS))
    kg = jnp.tile(k_gain, (1, FOX_HEADS))
    bf = jnp.pad(fox_b_f, ((0, 0), (0, LANES - FOX_HEADS)))

    h0, xn = _embed_rms(x.reshape(nb * seq, d), meta, norm1_gain, nb, lp, l, "embed_rms1")
    proj = _matmul(xn, w_in_t, "nt", F32, "proj_in")
    qa, ka, vb = _fox_prep(proj, qg, kg, bf, nb, lp, "fox_prep")
    o_fox, lse = _fox_fwd(qa, ka, vb, nb, lp, "fox_fwd")
    o_raw, o_hg, s_save = _hgrn_fwd(proj, lb, hg_out_gain, nb, lp, "hgrn_fwd")
    ya = _matmul(o_hg, w_a_t, "nt", F32, "branch_a")
    yb = _matmul(o_fox, w_b_t, "nt", F32, "branch_b")
    merged = _gate_fwd(proj, ya, yb, "gate_fwd")
    h1 = _matmul(merged, w_out, "nn", F32, "mix_out", residual=h0)
    hn = _rms_fwd(h1, norm2_gain, "rms2_fwd")
    up = _matmul(hn, w_up_t, "nt", F32, "ffn_up")
    act = _conv_fwd(up, conv_w, conv_b, nb, lp, "conv_fwd")
    out = _matmul(act, w_down, "nn", F32, "ffn_down", residual=h1)
    dy, lsum = _loss_head(out, target.reshape(nb * seq, d), nb, lp, l, "loss_head")
    loss = (0.5 / d) * jnp.sum(lsum)

    dact = _matmul(dy, w_down, "nt", F32, "d_act")
    g_w_down = _matmul(act, dy, "tn", F32, "g_w_down")
    dup, g_conv_w, g_conv_b = _conv_bwd(up, dact, conv_w, conv_b, nb, lp, "conv_bwd")
    dhn = _matmul(dup, w_up_t, "nn", F32, "d_hn")
    g_w_up_t = _matmul(dup, hn, "tn", F32, "g_w_up")
    dh1, g_norm2 = _rms_bwd(h1, norm2_gain, dhn, dy, "rms2_bwd")

    dmerged = _matmul(dh1, w_out, "nt", F32, "d_merged")
    g_w_out = _matmul(merged, dh1, "tn", F32, "g_w_out")
    dya, dyb, dgab = _gate_bwd(proj, ya, yb, dmerged, "gate_bwd")
    do_hg = _matmul(dya, w_a_t, "nn", F32, "d_o_hg")
    g_w_a_t = _matmul(dya, o_hg, "tn", F32, "g_w_a")
    do_fox = _matmul(dyb, w_b_t, "nn", F32, "d_o_fox")
    g_w_b_t = _matmul(dyb, o_fox, "tn", F32, "g_w_b")
    dhq, dhf, dhi, dhg, g_hg_gain, g_lb = _hgrn_bwd(proj, o_raw, s_save, do_hg, lb, hg_out_gain, nb, lp, "hgrn_bwd")
    dqs, dkn, dvv, dc0, dc1 = _fox_bwd(qa, ka, vb, do_fox, o_fox, lse, nb, lp, "fox_bwd")
    dcum = jnp.stack([dc0, dc1], axis=2).reshape(nb, FOX_HEADS, lp)
    dcum = jnp.pad(jnp.transpose(dcum, (0, 2, 1)), ((0, 0), (0, 0), (0, LANES - FOX_HEADS))).reshape(m, LANES)
    dfqkv, dff, g_qg, g_kg, g_bf = _fox_prep_bwd(proj, dqs, dkn, dvv, dcum, qg, kg, bf, nb, lp, "fox_prep_bwd")
    dproj = jnp.concatenate([dgab, dfqkv, dhq, dhf, dhi, dhg, dff], axis=1)
    dxn = _matmul(dproj, w_in_t, "nn", F32, "d_xn")
    g_w_in_t = _matmul(dproj, xn, "tn", F32, "g_w_in")
    dh0, g_norm1 = _rms_bwd(h0, norm1_gain, dxn, dh1, "rms1_bwd")

    dh0 = dh0.reshape(nb, lp, d)
    grad_x = dh0[:, N_META:l]
    g_meta = jnp.sum(dh0[:, :N_META], axis=0)
    g_q_gain = jnp.sum(g_qg.reshape(FOX_HEADS, FOX_HEAD_DIM), axis=0, keepdims=True)
    g_k_gain = jnp.sum(g_kg.reshape(FOX_HEADS, FOX_HEAD_DIM), axis=0, keepdims=True)
    grads = dict(meta_tokens=g_meta, norm1_gain=g_norm1, w_in_t=g_w_in_t, fox_b_f=g_bf[:, :FOX_HEADS],
                 q_norm_gain=g_q_gain, k_norm_gain=g_k_gain, lb=g_lb, hg_out_gain=g_hg_gain,
                 w_a_t=g_w_a_t, w_b_t=g_w_b_t, w_out=g_w_out, norm2_gain=g_norm2, w_up_t=g_w_up_t,
                 conv_w=g_conv_w, conv_b=g_conv_b, w_down=g_w_down)
    return loss, grad_x, grads


SMALL = ("norm1_gain", "fox_b_f", "q_norm_gain", "k_norm_gain", "hg_lb_logits", "hg_out_gain", "norm2_gain", "conv_b")
ORDER = ("meta_tokens", "norm1_gain", "w_in", "fox_b_f", "q_norm_gain", "k_norm_gain", "hg_lb_logits", "hg_out_gain",
         "w_branch_a", "w_branch_b", "w_out", "norm2_gain", "w_up", "conv_w", "conv_b", "w_down")


def kernel(x, meta_tokens, norm1_gain, w_in, fox_b_f, q_norm_gain, k_norm_gain, hg_lb_logits, hg_out_gain, w_branch_a, w_branch_b, w_out, norm2_gain, w_up, conv_w, conv_b, w_down, loss_target, m_meta_tokens, m_norm1_gain, m_w_in, m_fox_b_f, m_q_norm_gain, m_k_norm_gain, m_hg_lb_logits, m_hg_out_gain, m_w_branch_a, m_w_branch_b, m_w_out, m_norm2_gain, m_w_up, m_conv_w, m_conv_b, m_w_down, v_meta_tokens, v_norm1_gain, v_w_in, v_fox_b_f, v_q_norm_gain, v_k_norm_gain, v_hg_lb_logits, v_hg_out_gain, v_w_branch_a, v_w_branch_b, v_w_out, v_norm2_gain, v_w_up, v_conv_w, v_conv_b, v_w_down):
    w = dict(meta_tokens=meta_tokens, norm1_gain=norm1_gain, w_in=w_in, fox_b_f=fox_b_f, q_norm_gain=q_norm_gain,
             k_norm_gain=k_norm_gain, hg_lb_logits=hg_lb_logits, hg_out_gain=hg_out_gain, w_branch_a=w_branch_a,
             w_branch_b=w_branch_b, w_out=w_out, norm2_gain=norm2_gain, w_up=w_up, conv_w=conv_w, conv_b=conv_b,
             w_down=w_down)
    mom = dict(meta_tokens=m_meta_tokens, norm1_gain=m_norm1_gain, w_in=m_w_in, fox_b_f=m_fox_b_f,
               q_norm_gain=m_q_norm_gain, k_norm_gain=m_k_norm_gain, hg_lb_logits=m_hg_lb_logits,
               hg_out_gain=m_hg_out_gain, w_branch_a=m_w_branch_a, w_branch_b=m_w_branch_b, w_out=m_w_out,
               norm2_gain=m_norm2_gain, w_up=m_w_up, conv_w=m_conv_w, conv_b=m_conv_b, w_down=m_w_down)
    var = dict(meta_tokens=v_meta_tokens, norm1_gain=v_norm1_gain, w_in=v_w_in, fox_b_f=v_fox_b_f,
               q_norm_gain=v_q_norm_gain, k_norm_gain=v_k_norm_gain, hg_lb_logits=v_hg_lb_logits,
               hg_out_gain=v_hg_out_gain, w_branch_a=v_w_branch_a, w_branch_b=v_w_branch_b, w_out=v_w_out,
               norm2_gain=v_norm2_gain, w_up=v_w_up, conv_w=v_conv_w, conv_b=v_conv_b, w_down=v_w_down)
    d = D_MODEL
    n_in, n_up = w_in.shape[2], w_up.shape[2]
    n_ab, n_meta = w_branch_a.shape[2], meta_tokens.shape[1]

    shards = [w_in[0].T.astype(BF16),
              w_up[0].T.astype(BF16),
              jnp.stack([w_branch_a[0].T, w_branch_b[0].T]).astype(BF16),
              w_out[0].astype(BF16),
              w_down[0].astype(BF16),
              _tiny_pack(conv_w[0], meta_tokens)]
    g_in, g_up, g_ab, g_out, g_down, g_tiny = _all_gather(shards, "gather_weights")
    w_in_t = _align_axis0(g_in.reshape(N_DEV * n_in, d))
    w_up_t = _ffn_interleave(g_up.reshape(N_DEV * n_up, d), 0)
    w_a_t = g_ab[:, 0].reshape(N_DEV * n_ab, -1)
    w_b_t = g_ab[:, 1].reshape(N_DEV * n_ab, -1)
    cw_slots, meta_slots = _tiny_unpack(g_tiny, n_up, n_meta)
    conv_w_f = _ffn_interleave(jnp.transpose(cw_slots, (1, 0, 2)).reshape(CONV_WIDTH, -1), 1)
    meta_f = jnp.transpose(meta_slots, (1, 0, 2)).reshape(N_META, -1)
    conv_b_i = _ffn_interleave(conv_b, 1)

    lb, lb_vjp = jax.vjp(_lower_bound, hg_lb_logits)
    loss, grad_x, g = _local_step(
        x, loss_target, meta_f, norm1_gain, w_in_t, fox_b_f, q_norm_gain, k_norm_gain, lb, hg_out_gain,
        w_a_t, w_b_t, g_out.reshape(d, d), norm2_gain, w_up_t, conv_w_f, conv_b_i, g_down.reshape(-1, d))

    g["hg_lb_logits"] = lb_vjp(g.pop("lb"))[0]
    g["conv_b"] = _ffn_deinterleave(g["conv_b"], 1)
    gcw = _ffn_deinterleave(g["conv_w"], 1).reshape(CONV_WIDTH, N_DEV, n_up)
    gmeta = g["meta_tokens"].reshape(N_META, N_DEV, n_meta)
    tiny = jnp.concatenate([
        jnp.pad(jnp.transpose(gcw, (1, 0, 2)), ((0, 0), (0, 8 - CONV_WIDTH), (0, TINY_COLS - n_up))),
        jnp.pad(jnp.transpose(gmeta, (1, 0, 2)), ((0, 0), (0, 0), (0, TINY_COLS - n_meta)))], axis=1)
    small_shapes = [w[n].shape for n in SMALL]
    rows_sm = _rows_for(small_shapes, extra=1)
    small = _pack([g[n].reshape(-1) for n in SMALL] + [loss.reshape(1)], rows_sm)
    blocks = [_unalign_axis0(g["w_in_t"]).reshape(N_DEV, n_in, d).astype(BF16),
              _ffn_deinterleave(g["w_up_t"], 0).reshape(N_DEV, n_up, d).astype(BF16),
              jnp.stack([g["w_a_t"].reshape(N_DEV, n_ab, -1), g["w_b_t"].reshape(N_DEV, n_ab, -1)], axis=1).astype(BF16),
              g["w_out"].reshape(N_DEV, -1, d).astype(BF16),
              g["w_down"].reshape(N_DEV, -1, d).astype(BF16),
              tiny]
    r_in, r_up, r_ab, r_out, r_down, r_tiny, r_small = _exchange(blocks, [small], "exchange_grads")

    res = {}
    g_in_s = _slot_sum(r_in, "sum_w_in").T
    res["w_in"] = (g_in_s,) + tuple(_adamw(g_in_s, w_in[0], m_w_in[0], v_w_in[0], "adamw_w_in"))
    g_up_s = _slot_sum(r_up, "sum_w_up").T
    res["w_up"] = (g_up_s,) + tuple(_adamw(g_up_s, w_up[0], m_w_up[0], v_w_up[0], "adamw_w_up"))
    g_ab_s = jnp.swapaxes(_slot_sum(r_ab.reshape(N_DEV, 2 * n_ab, -1), "sum_w_ab").reshape(2, n_ab, -1), 1, 2)
    ab = lambda t: jnp.concatenate([t["w_branch_a"][0], t["w_branch_b"][0]], axis=0)
    o_ab = (g_ab_s.reshape(-1, n_ab),) + tuple(_adamw(g_ab_s.reshape(-1, n_ab), ab(w), ab(mom), ab(var), "adamw_w_ab"))
    half = o_ab[0].shape[0] // 2
    res["w_branch_a"] = tuple(o[:half] for o in o_ab)
    res["w_branch_b"] = tuple(o[half:] for o in o_ab)
    res["w_out"] = tuple(_sum_adamw(r_out, w_out[0], m_w_out[0], v_w_out[0], "adamw_w_out"))
    res["w_down"] = tuple(_sum_adamw(r_down, w_down[0], m_w_down[0], v_w_down[0], "adamw_w_down"))
    tp = lambda t: _tiny_pack(t["conv_w"][0], t["meta_tokens"])
    o_tiny = [_tiny_unpack(o, n_up, n_meta) for o in _sum_adamw(r_tiny, tp(w), tp(mom), tp(var), "adamw_tiny")]
    res["conv_w"] = tuple(o[0] for o in o_tiny)
    res["meta_tokens"] = tuple(o[1] for o in o_tiny)
    zero1 = jnp.zeros((1,), F32)
    sp = lambda t: _pack([t[n].reshape(-1) for n in SMALL] + [zero1], rows_sm)
    o_small = [_unpack(o, small_shapes + [(1,)]) for o in _sum_adamw(r_small, sp(w), sp(mom), sp(var), "adamw_small")]
    for i, n in enumerate(SMALL):
        res[n] = tuple(o[i] for o in o_small)
    loss_all = o_small[0][len(SMALL)].reshape(())

    result = [[res[n][k].reshape(w[n].shape) for n in ORDER] for k in range(4)]
    return (loss_all, grad_x, *result[0], *result[1], *result[2], *result[3])
```

```python
import jax
import jax.numpy as jnp
import numpy as np
from jax import lax
from jax.experimental import pallas as pl
from jax.experimental.pallas import tpu as pltpu

F32 = jnp.float32
BF16 = jnp.bfloat16

D_MODEL = 1024
N_META = 16
FOX_HEADS = 8
FOX_HEAD_DIM = 64
FOX_WIDTH = FOX_HEADS * FOX_HEAD_DIM
HG_HEADS = 4
HG_DIM = 128
HG_WIDTH = HG_HEADS * HG_DIM
D_FF = 2816
CONV_WIDTH = 3
EPS = 1e-6
IN_COLS = 3 * FOX_WIDTH + FOX_HEADS + 4 * HG_WIDTH + 2 * D_MODEL
N_DEV = 8

ADAM_LR = 0.001
ADAM_B1 = 0.9
ADAM_B2 = 0.999
ADAM_EPS = 1e-08
ADAM_WD = 0.01
ADAM_STEP = 10

LANES = 128
SEQ_BLOCK = 128
SUB = 16
NEG = -1e30
VMEM_LIMIT = 48 * 1024 * 1024

FOX_CB = 2 * D_MODEL // FOX_WIDTH
CB_HQ = (2 * D_MODEL + 3 * FOX_WIDTH) // LANES
CB_HF = CB_HQ + HG_HEADS
CB_HI = CB_HF + HG_HEADS
CB_HG = CB_HI + HG_HEADS
CB_FF = CB_HG + HG_HEADS


def _div_tile(n, target, mult):
    best = None
    for t in range(mult, min(n, target) + 1, mult):
        if n % t == 0:
            best = t
    if best is None:
        best = n
    return best


def _cp(*sem):
    return pltpu.CompilerParams(dimension_semantics=sem, vmem_limit_bytes=VMEM_LIMIT)


def _sigmoid(x):
    return 0.5 * jnp.tanh(0.5 * x) + 0.5


def _dot(a, b, dims, precision=None):
    return lax.dot_general(a, b, (dims, ((), ())), preferred_element_type=F32, precision=precision)


NN = ((1,), (0,))
NT = ((1,), (1,))
TN = ((0,), (0,))
HI = lax.Precision.HIGHEST


MATMUL_VMEM_BUDGET = 30 * 1024 * 1024
MATMUL_MAX_TILE = 2048


def _tile_options(n):
    return [t for t in range(LANES, min(n, MATMUL_MAX_TILE) + 1, LANES) if n % t == 0] or [n]


def _matmul_tiles(m, n, k, a_bytes, b_bytes, o_bytes, has_res):
    tk = _div_tile(k, MATMUL_MAX_TILE, LANES)
    best = None
    for tm in _tile_options(m):
        for tn in _tile_options(n):
            vmem = 2 * (tm * tk * a_bytes + tk * tn * b_bytes) + 2 * tm * tn * o_bytes
            vmem += tm * tn * 4 if tk < k else 0
            vmem += 2 * tm * tn * 4 if has_res else 0
            if vmem > MATMUL_VMEM_BUDGET:
                continue
            key = (tm * tn, tn % 256 == 0, tn)
            if best is None or key > best[0]:
                best = (key, tm, tn)
    assert best is not None, (m, n, k)
    return best[1], best[2], tk


def _matmul(a, b, mode, out_dtype, name, residual=None):
    if mode == "nn":
        (m, k), (k2, n) = a.shape, b.shape
    elif mode == "nt":
        (m, k), (n, k2) = a.shape, b.shape
    else:
        (k, m), (k2, n) = a.shape, b.shape
    assert k == k2, (a.shape, b.shape, mode)
    has_res = residual is not None
    tm, tn, tk = _matmul_tiles(m, n, k, a.dtype.itemsize, b.dtype.itemsize, jnp.dtype(out_dtype).itemsize, has_res)
    nk = k // tk
    if mode == "nn":
        a_spec = pl.BlockSpec((tm, tk), lambda i, j, kk: (i, kk))
        b_spec = pl.BlockSpec((tk, tn), lambda i, j, kk: (kk, j))
        dims = NN
    elif mode == "nt":
        a_spec = pl.BlockSpec((tm, tk), lambda i, j, kk: (i, kk))
        b_spec = pl.BlockSpec((tn, tk), lambda i, j, kk: (j, kk))
        dims = NT
    else:
        a_spec = pl.BlockSpec((tk, tm), lambda i, j, kk: (kk, i))
        b_spec = pl.BlockSpec((tk, tn), lambda i, j, kk: (kk, j))
        dims = TN
    o_spec = pl.BlockSpec((tm, tn), lambda i, j, kk: (i, j))

    def body(*refs):
        a_ref, b_ref = refs[0], refs[1]
        r_ref = refs[2] if has_res else None
        o_ref = refs[3] if has_res else refs[2]
        part = _dot(a_ref[...].astype(BF16), b_ref[...].astype(BF16), dims)
        if nk == 1:
            o_ref[...] = (part + r_ref[...] if has_res else part).astype(o_ref.dtype)
            return
        acc_ref = refs[-1]
        kk = pl.program_id(2)

        @pl.when(kk == 0)
        def _():
            acc_ref[...] = part

        @pl.when(kk > 0)
        def _():
            acc_ref[...] += part

        @pl.when(kk == nk - 1)
        def _():
            acc = acc_ref[...]
            if has_res:
                acc = acc + r_ref[...]
            o_ref[...] = acc.astype(o_ref.dtype)

    in_specs = [a_spec, b_spec] + ([o_spec] if has_res else [])
    args = (a, b) + ((residual,) if has_res else ())
    return pl.pallas_call(
        body, name=name, grid=(m // tm, n // tn, nk),
        in_specs=in_specs, out_specs=o_spec,
        out_shape=jax.ShapeDtypeStruct((m, n), out_dtype),
        scratch_shapes=[pltpu.VMEM((tm, tn), F32)] if nk > 1 else [],
        compiler_params=_cp("parallel", "parallel", "arbitrary"),
    )(*args)


def _rms_fwd(x, gain, name):
    m, d = x.shape
    tm = _div_tile(m, 512, 16)

    def body(x_ref, g_ref, o_ref):
        xv = x_ref[...]
        r = lax.rsqrt(jnp.mean(xv * xv, axis=-1, keepdims=True) + EPS)
        o_ref[...] = ((xv * r) * g_ref[...]).astype(o_ref.dtype)

    return pl.pallas_call(
        body, name=name, grid=(m // tm,),
        in_specs=[pl.BlockSpec((tm, d), lambda i: (i, 0)), pl.BlockSpec((1, d), lambda i: (0, 0))],
        out_specs=pl.BlockSpec((tm, d), lambda i: (i, 0)),
        out_shape=jax.ShapeDtypeStruct((m, d), BF16),
        compiler_params=_cp("parallel"),
    )(x, gain)


def _rms_bwd(x, gain, dy, dres, name):
    m, d = x.shape
    tm = _div_tile(m, 256, 8)

    def body(x_ref, g_ref, dy_ref, dr_ref, dx_ref, dg_ref):
        xv = x_ref[...]
        r = lax.rsqrt(jnp.mean(xv * xv, axis=-1, keepdims=True) + EPS)
        nv = xv * r
        dyv = dy_ref[...]
        gdy = dyv * g_ref[...]
        dx_ref[...] = dr_ref[...] + r * (gdy - nv * jnp.mean(gdy * nv, axis=-1, keepdims=True))
        part = jnp.sum(dyv * nv, axis=0, keepdims=True)

        @pl.when(pl.program_id(0) == 0)
        def _():
            dg_ref[...] = part

        @pl.when(pl.program_id(0) > 0)
        def _():
            dg_ref[...] += part

    row = pl.BlockSpec((tm, d), lambda i: (i, 0))
    vec = pl.BlockSpec((1, d), lambda i: (0, 0))
    return pl.pallas_call(
        body, name=name, grid=(m // tm,),
        in_specs=[row, vec, row, row], out_specs=[row, vec],
        out_shape=[jax.ShapeDtypeStruct((m, d), F32), jax.ShapeDtypeStruct((1, d), F32)],
        compiler_params=_cp("arbitrary"),
    )(x, gain, dy, dres)


def _head_stats(xv, lo):
    sq = xv * xv
    s_lo = jnp.sum(jnp.where(lo, sq, 0.0), axis=1, keepdims=True)
    s_hi = jnp.sum(jnp.where(lo, 0.0, sq), axis=1, keepdims=True)
    return jnp.where(lo, s_lo, s_hi) * (1.0 / FOX_HEAD_DIM)


BIAS_LANE = FOX_HEAD_DIM
N_SPLIT = 3


def _split3(c):
    c1 = c.astype(BF16).astype(F32)
    r1 = c - c1
    c2 = r1.astype(BF16).astype(F32)
    c3 = (r1 - c2).astype(BF16).astype(F32)
    return c1, c2, c3


def _fox_prep(proj, qg, kg, bf, nb, lp, name):
    m = proj.shape[0]
    ts = SEQ_BLOCK
    nblk = lp // ts
    scale = FOX_HEAD_DIM ** -0.5

    def body(q_ref, k_ref, v_ref, f_ref, qg_ref, kg_ref, bf_ref, qo_ref, ko_ref, vo_ref, carry_ref):
        lane = lax.broadcasted_iota(jnp.int32, (1, LANES), 1)
        lo = lane < FOX_HEAD_DIM

        @pl.when(pl.program_id(1) == 0)
        def _():
            carry_ref[...] = jnp.zeros_like(carry_ref)

        z = f_ref[...] + bf_ref[...]
        logf = jnp.minimum(z, 0.0) - jnp.log(1.0 + jnp.exp(-jnp.abs(z)))
        logf = jnp.where(lane < FOX_HEADS, logf, 0.0)
        r = lax.broadcasted_iota(jnp.int32, (ts, ts), 0)
        c = lax.broadcasted_iota(jnp.int32, (ts, ts), 1)
        tri = jnp.where(c <= r, 1.0, 0.0).astype(F32)
        cum = _dot(tri, logf, NN, HI) + carry_ref[...]
        carry_ref[...] = cum[ts - 1:ts, :]

        ones = jnp.where((lane >= BIAS_LANE + N_SPLIT) & (lane < BIAS_LANE + 2 * N_SPLIT), 1.0, 0.0)
        ones_k = jnp.where((lane >= BIAS_LANE) & (lane < BIAS_LANE + N_SPLIT), 1.0, 0.0)
        for j in range(FOX_WIDTH // LANES):
            cs = slice(j * LANES, (j + 1) * LANES)
            xq = q_ref[:, cs]
            yq = ((xq * lax.rsqrt(_head_stats(xq, lo) + EPS)) * qg_ref[:, cs]) * scale
            xk = k_ref[:, cs]
            yk = (xk * lax.rsqrt(_head_stats(xk, lo) + EPS)) * kg_ref[:, cs]
            for hh in range(2):
                h = 2 * j + hh
                pieces = _split3(_lane_pick(cum, lane, h))
                qb, kb = ones, ones_k
                for i, piece in enumerate(pieces):
                    qb = jnp.where(lane == BIAS_LANE + i, piece, qb)
                    kb = jnp.where(lane == BIAS_LANE + N_SPLIT + i, -piece, kb)
                yq_h = yq if hh == 0 else pltpu.roll(yq, FOX_HEAD_DIM, 1)
                yk_h = yk if hh == 0 else pltpu.roll(yk, FOX_HEAD_DIM, 1)
                hs = slice(h * LANES, (h + 1) * LANES)
                qo_ref[:, hs] = jnp.where(lo, yq_h, qb).astype(BF16)
                ko_ref[:, hs] = jnp.where(lo, yk_h, kb).astype(BF16)
        vo_ref[...] = v_ref[...].astype(BF16)

    w = FOX_WIDTH
    row = lambda b, i: (b * nblk + i, 0)
    return pl.pallas_call(
        body, name=name, grid=(nb, nblk),
        in_specs=[pl.BlockSpec((ts, w), lambda b, i: (b * nblk + i, FOX_CB)),
                  pl.BlockSpec((ts, w), lambda b, i: (b * nblk + i, FOX_CB + 1)),
                  pl.BlockSpec((ts, w), lambda b, i: (b * nblk + i, FOX_CB + 2)),
                  pl.BlockSpec((ts, LANES), lambda b, i: (b * nblk + i, CB_FF)),
                  pl.BlockSpec((1, w), lambda b, i: (0, 0)),
                  pl.BlockSpec((1, w), lambda b, i: (0, 0)),
                  pl.BlockSpec((1, LANES), lambda b, i: (0, 0))],
        out_specs=[pl.BlockSpec((ts, 2 * w), row), pl.BlockSpec((ts, 2 * w), row), pl.BlockSpec((ts, w), row)],
        out_shape=[jax.ShapeDtypeStruct((m, 2 * w), BF16)] * 2 + [jax.ShapeDtypeStruct((m, w), BF16)],
        scratch_shapes=[pltpu.VMEM((1, LANES), F32)],
        compiler_params=_cp("arbitrary", "arbitrary"),
    )(proj, proj, proj, proj, qg, kg, bf)


def _att_tile(lp):
    return 384 if (lp % 384 == 0 and lp > 384) else 128


def _lane_pick(blk, lane, idx):
    return jnp.sum(jnp.where(lane == idx, blk, 0.0), axis=1, keepdims=True)


def _head_masks():
    lane = lax.broadcasted_iota(jnp.int32, (1, LANES), 1)
    return lane, [(lane >= hh * FOX_HEAD_DIM) & (lane < (hh + 1) * FOX_HEAD_DIM) for hh in range(2)]


def _fox_fwd(qa, ka, vb, nb, lp, name):
    m = qa.shape[0]
    tq = _att_tile(lp)
    nq = lp // tq
    npair = FOX_WIDTH // LANES

    def body(q_ref, k_ref, v_ref, o_ref, lse_ref):
        qi = pl.program_id(2)
        lane, hmasks = _head_masks()
        zero16 = jnp.zeros((), BF16)
        causal = lax.broadcasted_iota(jnp.int32, (tq, 1), 0) >= lax.broadcasted_iota(jnp.int32, (1, tq), 1)
        o_tot = jnp.zeros((tq, LANES), F32)
        lse_out = jnp.zeros((tq, LANES), F32)
        for hh in range(2):
            hs = slice(hh * LANES, (hh + 1) * LANES)
            q = q_ref[:, hs]

            def tile(j, carry, diagonal, hs=hs, q=q, hmask=hmasks[hh]):
                mx, l, acc = carry
                k0 = pl.multiple_of(j * tq, tq)
                vz = jnp.where(hmask, v_ref[pl.ds(k0, tq), :], zero16)
                s = _dot(q, k_ref[pl.ds(k0, tq), hs], NT)
                if diagonal:
                    s = jnp.where(causal, s, NEG)
                m_new = jnp.maximum(mx, jnp.max(s, axis=1, keepdims=True))
                alpha = jnp.exp(mx - m_new)
                pe = jnp.exp(s - m_new)
                l = alpha * l + jnp.sum(pe, axis=1, keepdims=True)
                acc = alpha * acc + _dot(pe.astype(BF16), vz, NN)
                return m_new, l, acc

            init = (jnp.full((tq, 1), NEG, F32), jnp.zeros((tq, 1), F32), jnp.zeros((tq, LANES), F32))
            carry = lax.fori_loop(0, qi, lambda j, c, tile=tile: tile(j, c, False), init)
            mx, l, acc = tile(qi, carry, True)
            o_tot = o_tot + acc / l
            lse_out = jnp.where(lane == hh, mx + jnp.log(l), lse_out)
        o_ref[...] = o_tot
        lse_ref[...] = lse_out

    return pl.pallas_call(
        body, name=name, grid=(nb, npair, nq),
        in_specs=[pl.BlockSpec((tq, 2 * LANES), lambda b, p, i: (b * nq + i, p)),
                  pl.BlockSpec((lp, 2 * LANES), lambda b, p, i: (b, p)),
                  pl.BlockSpec((lp, LANES), lambda b, p, i: (b, p))],
        out_specs=[pl.BlockSpec((tq, LANES), lambda b, p, i: (b * nq + i, p)),
                   pl.BlockSpec((None, None, tq, LANES), lambda b, p, i: (b, p, i, 0))],
        out_shape=[jax.ShapeDtypeStruct((m, FOX_WIDTH), F32),
                   jax.ShapeDtypeStruct((nb, npair, lp, LANES), F32)],
        compiler_params=_cp("parallel", "parallel", "arbitrary"),
    )(qa, ka, vb)


def _fox_bwd(qa, ka, vb, do, o, lse, nb, lp, name):
    m = qa.shape[0]
    tq = _att_tile(lp)
    nq = lp // tq
    npair = FOX_WIDTH // LANES

    def body(k_ref, v_ref, q_ref, do_ref, o_ref, lse_ref, dq_ref, dk_ref, dv_ref, dc0_ref, dc1_ref):
        j = pl.program_id(2)
        lane, hmasks = _head_masks()
        zero16 = jnp.zeros((), BF16)
        causal = lax.broadcasted_iota(jnp.int32, (tq, 1), 0) >= lax.broadcasted_iota(jnp.int32, (1, tq), 1)

        @pl.when(j == 0)
        def _():
            dq_ref[...] = jnp.zeros_like(dq_ref)

        vv = v_ref[...]
        vzs = [jnp.where(hm, vv, zero16) for hm in hmasks]

        def tile(qi, carry, diagonal):
            dk0, dk1, dv, dc0, dc1 = carry
            q0 = pl.multiple_of(qi * tq, tq)
            dob16 = do_ref[pl.ds(q0, tq), :].astype(BF16)
            ob = o_ref[pl.ds(q0, tq), :]
            lseb = lse_ref[pl.ds(q0, tq), :]
            dks, dcs = [dk0, dk1], [dc0, dc1]
            for hh in range(2):
                hs = slice(hh * LANES, (hh + 1) * LANES)
                q = q_ref[pl.ds(q0, tq), hs]
                doz16 = jnp.where(hmasks[hh], dob16, zero16)
                delta = jnp.sum(doz16.astype(F32) * ob, axis=1, keepdims=True)
                s = _dot(q, k_ref[:, hs], NT) - _lane_pick(lseb, lane, hh)
                if diagonal:
                    s = jnp.where(causal, s, NEG)
                pm = jnp.exp(s)
                ds = pm * (_dot(doz16, vzs[hh], NT) - delta)
                ds16 = ds.astype(BF16)
                dv = dv + _dot(pm.astype(BF16), doz16, TN)
                dks[hh] = dks[hh] + _dot(ds16, q, TN)
                dq_ref[pl.ds(q0, tq), hs] += _dot(ds16, k_ref[:, hs], NN)
                dcs[hh] = dcs[hh] - jnp.sum(ds, axis=0, keepdims=True)
            return dks[0], dks[1], dv, dcs[0], dcs[1]

        zt = jnp.zeros((tq, LANES), F32)
        zr = jnp.zeros((1, tq), F32)
        carry = tile(j, (zt, zt, zt, zr, zr), True)
        dk0, dk1, dv, dc0, dc1 = lax.fori_loop(j + 1, nq, lambda qi, c: tile(qi, c, False), carry)
        dk_ref[:, :LANES] = dk0
        dk_ref[:, LANES:] = dk1
        dv_ref[...] = dv
        dc0_ref[...] = dc0
        dc1_ref[...] = dc1

    full2 = pl.BlockSpec((lp, 2 * LANES), lambda b, p, j: (b, p))
    full = pl.BlockSpec((lp, LANES), lambda b, p, j: (b, p))
    blk2 = pl.BlockSpec((tq, 2 * LANES), lambda b, p, j: (b * nq + j, p))
    blk = pl.BlockSpec((tq, LANES), lambda b, p, j: (b * nq + j, p))
    dcs = pl.BlockSpec((None, None, 1, tq), lambda b, p, j: (b, p, 0, j))
    return pl.pallas_call(
        body, name=name, grid=(nb, npair, nq),
        in_specs=[blk2, blk, full2, full, full,
                  pl.BlockSpec((None, None, lp, LANES), lambda b, p, j: (b, p, 0, 0))],
        out_specs=[full2, blk2, blk, dcs, dcs],
        out_shape=[jax.ShapeDtypeStruct((m, 2 * FOX_WIDTH), F32)] * 2 + [jax.ShapeDtypeStruct((m, FOX_WIDTH), F32)]
        + [jax.ShapeDtypeStruct((nb, npair, 1, lp), F32)] * 2,
        compiler_params=_cp("parallel", "parallel", "arbitrary"),
    )(ka, vb, qa, do, o, lse)


def _fox_prep_bwd(proj, dqa, dka, dv, dcum, qg, kg, bf, nb, lp, name):
    m = proj.shape[0]
    ts = SEQ_BLOCK
    nblk = lp // ts
    scale = FOX_HEAD_DIM ** -0.5
    w = FOX_WIDTH
    wo = 3 * w

    def body(q_ref, k_ref, f_ref, dq_ref, dk_ref, dv_ref, dc_ref, qg_ref, kg_ref, bf_ref,
             out_ref, dff_ref, dqg_ref, dkg_ref, dbf_ref, carry_ref):
        first = (pl.program_id(0) == 0) & (pl.program_id(1) == 0)
        lane = lax.broadcasted_iota(jnp.int32, (1, LANES), 1)
        lo = lane < FOX_HEAD_DIM

        @pl.when(first)
        def _():
            dqg_ref[...] = jnp.zeros_like(dqg_ref)
            dkg_ref[...] = jnp.zeros_like(dkg_ref)
            dbf_ref[...] = jnp.zeros_like(dbf_ref)

        def norm_bwd(x, g, dy):
            r = lax.rsqrt(_head_stats(x, lo) + EPS)
            nv = x * r
            gdy = dy * g
            prod = gdy * nv
            s_lo = jnp.sum(jnp.where(lo, prod, 0.0), axis=1, keepdims=True)
            s_hi = jnp.sum(jnp.where(lo, 0.0, prod), axis=1, keepdims=True)
            mean = jnp.where(lo, s_lo, s_hi) * (1.0 / FOX_HEAD_DIM)
            return r * (gdy - nv * mean), jnp.sum(dy * nv, axis=0, keepdims=True)

        def pair(d_ref, jj):
            even = d_ref[:, 2 * jj * LANES:(2 * jj + 1) * LANES]
            odd = d_ref[:, (2 * jj + 1) * LANES:(2 * jj + 2) * LANES]
            return jnp.where(lo, even, pltpu.roll(odd, FOX_HEAD_DIM, 1))

        for jj in range(w // LANES):
            cs = slice(jj * LANES, (jj + 1) * LANES)
            dx, dg = norm_bwd(q_ref[:, cs], qg_ref[:, cs], pair(dq_ref, jj) * scale)
            out_ref[:, cs] = dx.astype(BF16)
            dqg_ref[:, cs] += dg
            dx, dg = norm_bwd(k_ref[:, cs], kg_ref[:, cs], pair(dk_ref, jj))
            out_ref[:, w + jj * LANES:w + (jj + 1) * LANES] = dx.astype(BF16)
            dkg_ref[:, cs] += dg
        out_ref[:, 2 * w:3 * w] = dv_ref[...].astype(BF16)

        @pl.when(pl.program_id(1) == 0)
        def _():
            carry_ref[...] = jnp.zeros_like(carry_ref)

        dc = dc_ref[...]
        r = lax.broadcasted_iota(jnp.int32, (ts, ts), 0)
        c = lax.broadcasted_iota(jnp.int32, (ts, ts), 1)
        triu = jnp.where(c >= r, 1.0, 0.0).astype(F32)
        dlogf = _dot(triu, dc, NN, HI) + carry_ref[...]
        carry_ref[...] += jnp.sum(dc, axis=0, keepdims=True)
        z = f_ref[...] + bf_ref[...]
        dz = jnp.where(lane < FOX_HEADS, dlogf * _sigmoid(-z), 0.0)
        dff_ref[...] = dz.astype(BF16)
        dbf_ref[...] += jnp.sum(dz, axis=0, keepdims=True)

    rev = lambda b, i: (b * nblk + (nblk - 1 - i), 0)
    vec = lambda n: pl.BlockSpec((1, n), lambda b, i: (0, 0))
    return pl.pallas_call(
        body, name=name, grid=(nb, nblk),
        in_specs=[pl.BlockSpec((ts, w), lambda b, i: (b * nblk + (nblk - 1 - i), FOX_CB)),
                  pl.BlockSpec((ts, w), lambda b, i: (b * nblk + (nblk - 1 - i), FOX_CB + 1)),
                  pl.BlockSpec((ts, LANES), lambda b, i: (b * nblk + (nblk - 1 - i), CB_FF)),
                  pl.BlockSpec((ts, 2 * w), rev), pl.BlockSpec((ts, 2 * w), rev), pl.BlockSpec((ts, w), rev),
                  pl.BlockSpec((ts, LANES), rev), vec(w), vec(w), vec(LANES)],
        out_specs=[pl.BlockSpec((ts, wo), rev), pl.BlockSpec((ts, LANES), rev), vec(w), vec(w), vec(LANES)],
        out_shape=[jax.ShapeDtypeStruct((m, wo), BF16), jax.ShapeDtypeStruct((m, LANES), BF16),
                   jax.ShapeDtypeStruct((1, w), F32),
                   jax.ShapeDtypeStruct((1, w), F32), jax.ShapeDtypeStruct((1, LANES), F32)],
        scratch_shapes=[pltpu.VMEM((1, LANES), F32)],
        compiler_params=_cp("arbitrary", "arbitrary"),
    )(proj, proj, proj, dqa, dka, dv, dcum, qg, kg, bf)


def _chunk_masks():
    r = lax.broadcasted_iota(jnp.int32, (SEQ_BLOCK, SEQ_BLOCK), 0)
    c = lax.broadcasted_iota(jnp.int32, (SEQ_BLOCK, SEQ_BLOCK), 1)
    same = (r // SUB) == (c // SUB)
    return r, c, same


def _hg_gates(hf, lb):
    sg = _sigmoid(hf)
    f = lb + (1.0 - lb) * sg
    return sg, f, jnp.log(f), (1.0 - lb) * _sigmoid(-hf)


def _hg_intra_e(g_ref, base, t, srow):
    diff = g_ref[pl.ds(base + t, 1), :] - g_ref[pl.ds(base, SUB), :]
    return jnp.exp(jnp.where(srow <= t, diff, NEG))


def _hgrn_fwd(proj, lb, gain, nb, lp, name):
    m = proj.shape[0]
    tb = SEQ_BLOCK
    nblk = lp // tb
    ns = tb // SUB

    def body(q_ref, f_ref, i_ref, g_ref, lb_ref, gain_ref, oraw_ref, y_ref, ssave_ref,
             st_ref, g_scr, kin_scr, o_scr):
        @pl.when(pl.program_id(2) == 0)
        def _():
            st_ref[...] = jnp.zeros_like(st_ref)

        ssave_ref[...] = st_ref[...]
        lbv = lb_ref[...]
        _, _, lf, kin = _hg_gates(f_ref[...], lbv)
        r, c, same = _chunk_masks()
        ltri = jnp.where(same & (c <= r), 1.0, 0.0).astype(F32)
        lall = jnp.where(same, 1.0, 0.0).astype(F32)
        g = _dot(ltri, lf, NN, HI)
        gt = _dot(lall, lf, NN, HI)
        g_scr[...] = g
        kin_scr[...] = kin
        qv = q_ref[...]
        qg = (qv * jnp.exp(g)).astype(BF16)
        kg = (kin * jnp.exp(gt - g)).astype(BF16)
        et = jnp.exp(gt)
        srow = lax.broadcasted_iota(jnp.int32, (SUB, 1), 0)
        subs = [slice(cc * SUB, (cc + 1) * SUB) for cc in range(ns)]
        ups = [_dot(i_ref[sl, :].astype(BF16), kg[sl], TN) for sl in subs]
        st = st_ref[...]
        starts = []
        for cc in range(ns):
            starts.append(st)
            st = et[cc * SUB:cc * SUB + 1, :] * st + ups[cc]
        st_ref[...] = st
        for cc, sl in enumerate(subs):
            base = cc * SUB
            o_c = _dot(qg[sl], starts[cc].astype(BF16), NT)
            kc = kin_scr[sl, :]
            vc = i_ref[sl, :]
            for t in range(SUB):
                e = _hg_intra_e(g_scr, base, t, srow)
                a = jnp.sum((q_ref[pl.ds(base + t, 1), :] * kc) * e, axis=1, keepdims=True)
                ot = jnp.sum(a * vc, axis=0, keepdims=True)
                o_c = o_c + jnp.where(srow == t, ot, 0.0)
            o_scr[sl, :] = o_c
        o = o_scr[...]
        oraw_ref[...] = o
        rr = lax.rsqrt(jnp.mean(o * o, axis=-1, keepdims=True) + EPS)
        hg = g_ref[...]
        y_ref[...] = (((o * rr) * gain_ref[...]) * (hg * _sigmoid(hg))).astype(y_ref.dtype)

    col = lambda cb: pl.BlockSpec((tb, LANES), lambda b, h, i, cb=cb: (b * nblk + i, cb + h))
    out_blk = pl.BlockSpec((tb, LANES), lambda b, h, i: (b * nblk + i, h))
    return pl.pallas_call(
        body, name=name, grid=(nb, HG_HEADS, nblk),
        in_specs=[col(CB_HQ), col(CB_HF), col(CB_HI), col(CB_HG),
                  pl.BlockSpec((1, LANES), lambda b, h, i: (0, h)),
                  pl.BlockSpec((1, LANES), lambda b, h, i: (0, 0))],
        out_specs=[out_blk, out_blk,
                   pl.BlockSpec((None, None, None, HG_DIM, HG_DIM), lambda b, h, i: (b, h, i, 0, 0))],
        out_shape=[jax.ShapeDtypeStruct((m, HG_WIDTH), F32), jax.ShapeDtypeStruct((m, HG_WIDTH), BF16),
                   jax.ShapeDtypeStruct((nb, HG_HEADS, nblk, HG_DIM, HG_DIM), F32)],
        scratch_shapes=[pltpu.VMEM((HG_DIM, HG_DIM), F32), pltpu.VMEM((tb, LANES), F32),
                        pltpu.VMEM((tb, LANES), F32), pltpu.VMEM((tb, LANES), F32)],
        compiler_params=_cp("parallel", "parallel", "arbitrary"),
    )(proj, proj, proj, proj, lb, gain)


def _hgrn_bwd(proj, oraw, ssave, dy, lb, gain, nb, lp, name):
    m = proj.shape[0]
    tb = SEQ_BLOCK
    nblk = lp // tb
    ns = tb // SUB

    def body(q_ref, f_ref, i_ref, g_ref, oraw_ref, ssave_ref, dy_ref, lb_ref, gain_ref,
             dq_ref, df_ref, di_ref, dg_ref, dgain_ref, dlb_ref,
             dst_ref, g_scr, kin_scr, do_scr, dq_scr, dk_scr, dv_scr, dgg_scr):
        hd = pl.program_id(0)
        bb = pl.program_id(1)
        ii = pl.program_id(2)
        gainv = gain_ref[...]
        lbv = lb_ref[...]

        @pl.when((hd == 0) & (bb == 0) & (ii == 0))
        def _():
            dgain_ref[...] = jnp.zeros_like(dgain_ref)

        @pl.when((bb == 0) & (ii == 0))
        def _():
            dlb_ref[...] = jnp.zeros_like(dlb_ref)

        @pl.when(ii == 0)
        def _():
            dst_ref[...] = jnp.zeros_like(dst_ref)

        o = oraw_ref[...]
        rr = lax.rsqrt(jnp.mean(o * o, axis=-1, keepdims=True) + EPS)
        nv = o * rr
        hg = g_ref[...]
        sgg = _sigmoid(hg)
        sil = hg * sgg
        dyv = dy_ref[...]
        dg_ref[...] = (dyv * nv * gainv * (sgg * (1.0 + hg * (1.0 - sgg)))).astype(dg_ref.dtype)
        dgain_ref[...] += jnp.sum(dyv * nv * sil, axis=0, keepdims=True)
        dn = dyv * gainv * sil
        do_scr[...] = rr * (dn - nv * jnp.mean(dn * nv, axis=-1, keepdims=True))

        hf = f_ref[...]
        sg, f, lf, kin = _hg_gates(hf, lbv)
        r, c, same = _chunk_masks()
        ltri = jnp.where(same & (c <= r), 1.0, 0.0).astype(F32)
        lall = jnp.where(same, 1.0, 0.0).astype(F32)
        g = _dot(ltri, lf, NN, HI)
        gt = _dot(lall, lf, NN, HI)
        g_scr[...] = g
        kin_scr[...] = kin
        qv = q_ref[...]
        eg = jnp.exp(g)
        ekg = jnp.exp(gt - g)
        qg = qv * eg
        kg = kin * ekg
        qg16 = qg.astype(BF16)
        kg16 = kg.astype(BF16)
        et = jnp.exp(gt)
        subs = [slice(cc * SUB, (cc + 1) * SUB) for cc in range(ns)]
        ups = [_dot(i_ref[sl, :].astype(BF16), kg16[sl], TN) for sl in subs]
        st = ssave_ref[...]
        starts = []
        for cc in range(ns):
            starts.append(st)
            st = et[cc * SUB:cc * SUB + 1, :] * st + ups[cc]
        do16 = do_scr[...].astype(BF16)
        downs = [_dot(do16[sl], qg16[sl], TN) for sl in subs]
        dst = dst_ref[...]
        afters = [None] * ns
        for cc in reversed(range(ns)):
            afters[cc] = dst
            dst = et[cc * SUB:cc * SUB + 1, :] * dst + downs[cc]
        dst_ref[...] = dst

        srow = lax.broadcasted_iota(jnp.int32, (SUB, 1), 0)
        for cc, sl in enumerate(subs):
            base = cc * SUB
            st = starts[cc]
            st16 = st.astype(BF16)
            dst = afters[cc]
            dst16 = dst.astype(BF16)
            doc16 = do16[sl]
            vc = i_ref[sl, :]
            vc16 = vc.astype(BF16)
            kc = kin_scr[sl, :]
            etc = et[base:base + 1, :]
            dqg = _dot(doc16, st16, NN)
            dv_c = _dot(kg16[sl], dst16, NT)
            dkg = _dot(vc16, dst16, NN)
            dgt = jnp.sum(dst * st, axis=0, keepdims=True) * etc
            dq_c = dqg * eg[sl]
            dk_c = dkg * ekg[sl]
            dg_c = dqg * qg[sl] - dkg * kg[sl]
            dgt = dgt + jnp.sum(dkg * kg[sl], axis=0, keepdims=True)
            for t in range(SUB):
                e = _hg_intra_e(g_scr, base, t, srow)
                qt = q_ref[pl.ds(base + t, 1), :]
                dot_t = do_scr[pl.ds(base + t, 1), :]
                a = jnp.sum((qt * kc) * e, axis=1, keepdims=True)
                da = jnp.sum(dot_t * vc, axis=1, keepdims=True)
                dv_c = dv_c + a * dot_t
                w = da * e
                dq_t = jnp.sum(w * kc, axis=0, keepdims=True)
                wq = w * qt
                dk_c = dk_c + wq
                dg_c = dg_c - kc * wq + jnp.where(srow == t, qt * dq_t, 0.0)
                dq_c = dq_c + jnp.where(srow == t, dq_t, 0.0)
            dg_c = dg_c + jnp.where(srow == SUB - 1, dgt, 0.0)
            dq_scr[sl, :] = dq_c
            dk_scr[sl, :] = dk_c
            dv_scr[sl, :] = dv_c
            dgg_scr[sl, :] = dg_c

        utri = jnp.where(same & (c >= r), 1.0, 0.0).astype(F32)
        dlf = _dot(utri, dgg_scr[...], NN, HI)
        dkin = dk_scr[...]
        dsg = sg * (1.0 - sg)
        df_ref[...] = ((dlf / f - dkin) * ((1.0 - lbv) * dsg)).astype(df_ref.dtype)
        dlb_ref[...] += jnp.sum((dlf / f - dkin) * (1.0 - sg), axis=0, keepdims=True)
        dq_ref[...] = dq_scr[...].astype(dq_ref.dtype)
        di_ref[...] = dv_scr[...].astype(di_ref.dtype)

    rowi = lambda b, i: b * nblk + (nblk - 1 - i)
    col = lambda cb: pl.BlockSpec((tb, LANES), lambda h, b, i, cb=cb: (rowi(b, i), cb + h))
    hblk = pl.BlockSpec((tb, LANES), lambda h, b, i: (rowi(b, i), h))
    return pl.pallas_call(
        body, name=name, grid=(HG_HEADS, nb, nblk),
        in_specs=[col(CB_HQ), col(CB_HF), col(CB_HI), col(CB_HG), hblk,
                  pl.BlockSpec((None, None, None, HG_DIM, HG_DIM), lambda h, b, i: (b, h, nblk - 1 - i, 0, 0)),
                  hblk,
                  pl.BlockSpec((1, LANES), lambda h, b, i: (0, h)),
                  pl.BlockSpec((1, LANES), lambda h, b, i: (0, 0))],
        out_specs=[hblk, hblk, hblk, hblk,
                   pl.BlockSpec((1, LANES), lambda h, b, i: (0, 0)),
                   pl.BlockSpec((1, LANES), lambda h, b, i: (0, h))],
        out_shape=[jax.ShapeDtypeStruct((m, HG_WIDTH), BF16)] * 4
        + [jax.ShapeDtypeStruct((1, LANES), F32), jax.ShapeDtypeStruct((1, HG_WIDTH), F32)],
        scratch_shapes=[pltpu.VMEM((HG_DIM, HG_DIM), F32)] + [pltpu.VMEM((tb, LANES), F32)] * 7,
        compiler_params=_cp("arbitrary", "arbitrary", "arbitrary"),
    )(proj, proj, proj, proj, oraw, ssave, dy, lb, gain)


def _gate_fwd(proj, ya, yb, name):
    m = proj.shape[0]
    tm = _div_tile(m, 256, 16)

    def body(ga_ref, gb_ref, ya_ref, yb_ref, o_ref):
        o_ref[...] = (_sigmoid(ga_ref[...]) * ya_ref[...] + _sigmoid(gb_ref[...]) * yb_ref[...]).astype(o_ref.dtype)

    row = pl.BlockSpec((tm, D_MODEL), lambda i: (i, 0))
    return pl.pallas_call(
        body, name=name, grid=(m // tm,),
        in_specs=[row, pl.BlockSpec((tm, D_MODEL), lambda i: (i, 1)), row, row],
        out_specs=row, out_shape=jax.ShapeDtypeStruct((m, D_MODEL), BF16),
        compiler_params=_cp("parallel"),
    )(proj, proj, ya, yb)


def _gate_bwd(proj, ya, yb, dm, name):
    m = proj.shape[0]
    tm = _div_tile(m, 256, 16)

    def body(ga_ref, gb_ref, ya_ref, yb_ref, dm_ref, dya_ref, dyb_ref, dg_ref):
        dmv = dm_ref[...]
        sa = _sigmoid(ga_ref[...])
        sb = _sigmoid(gb_ref[...])
        dya_ref[...] = (dmv * sa).astype(BF16)
        dyb_ref[...] = (dmv * sb).astype(BF16)
        dg_ref[:, :D_MODEL] = (dmv * ya_ref[...] * (sa * (1.0 - sa))).astype(BF16)
        dg_ref[:, D_MODEL:] = (dmv * yb_ref[...] * (sb * (1.0 - sb))).astype(BF16)

    row = pl.BlockSpec((tm, D_MODEL), lambda i: (i, 0))
    wide = pl.BlockSpec((tm, 2 * D_MODEL), lambda i: (i, 0))
    return pl.pallas_call(
        body, name=name, grid=(m // tm,),
        in_specs=[row, pl.BlockSpec((tm, D_MODEL), lambda i: (i, 1)), row, row, row],
        out_specs=[row, row, wide],
        out_shape=[jax.ShapeDtypeStruct((m, D_MODEL), BF16)] * 2 + [jax.ShapeDtypeStruct((m, 2 * D_MODEL), BF16)],
        compiler_params=_cp("parallel"),
    )(proj, proj, ya, yb, dm)


CONV_ROWS = 128


def _conv3(x, xprev, w_ref, b_ref, rowi):
    r = x.shape[0]
    x1 = jnp.where(rowi < 1, pltpu.roll(xprev, 1, 0), pltpu.roll(x, 1, 0))
    x2 = jnp.where(rowi < 2, pltpu.roll(xprev, 2, 0), pltpu.roll(x, 2, 0))
    u = w_ref[0:1, :] * x2 + w_ref[1:2, :] * x1 + w_ref[2:3, :] * x + b_ref[...]
    return u, x1, x2


def _conv_fwd(up, cw, cb, nb, lp, name):
    m = up.shape[0]
    nct = D_FF // LANES
    r = CONV_ROWS
    nch = lp // r

    def body(u_ref, w_ref, b_ref, o_ref):
        rowi = lax.broadcasted_iota(jnp.int32, (r, 1), 0)

        def step(i, xp):
            r0 = pl.multiple_of(i * r, r)
            xc = u_ref[pl.ds(r0, r), :]
            u, _, _ = _conv3(xc, xp, w_ref, b_ref, rowi)
            ug, uv = u[:, :LANES], u[:, LANES:]
            o_ref[pl.ds(r0, r), :] = ((ug * _sigmoid(ug)) * uv).astype(o_ref.dtype)
            return xc

        lax.fori_loop(0, nch, step, jnp.zeros((r, 2 * LANES), F32))

    return pl.pallas_call(
        body, name=name, grid=(nb, nct),
        in_specs=[pl.BlockSpec((lp, 2 * LANES), lambda b, c: (b, c)),
                  pl.BlockSpec((CONV_WIDTH, 2 * LANES), lambda b, c: (0, c)),
                  pl.BlockSpec((1, 2 * LANES), lambda b, c: (0, c))],
        out_specs=pl.BlockSpec((lp, LANES), lambda b, c: (b, c)),
        out_shape=jax.ShapeDtypeStruct((m, D_FF), BF16),
        compiler_params=_cp("parallel", "parallel"),
    )(up, cw, cb)


def _conv_bwd(up, dact, cw, cb, nb, lp, name):
    m = up.shape[0]
    nct = D_FF // LANES
    r = CONV_ROWS
    nch = lp // r

    def body(u_ref, da_ref, w_ref, b_ref, dup_ref, dw_ref, db_ref):
        rowi = lax.broadcasted_iota(jnp.int32, (r, 1), 0)
        wv = w_ref[...]

        def step(k, carry):
            dun, dw0, dw1, dw2, dbs = carry
            i = nch - 1 - k
            r0 = pl.multiple_of(i * r, r)
            rp = pl.multiple_of(jnp.maximum(i - 1, 0) * r, r)
            xc = u_ref[pl.ds(r0, r), :]
            xp = u_ref[pl.ds(rp, r), :] * (i > 0).astype(F32)
            u, x1, x2 = _conv3(xc, xp, w_ref, b_ref, rowi)
            ug, uv = u[:, :LANES], u[:, LANES:]
            da = da_ref[pl.ds(r0, r), :]
            sg = _sigmoid(ug)
            du = jnp.concatenate([da * uv * (sg * (1.0 + ug * (1.0 - sg))), da * (ug * sg)], axis=1)
            d1 = jnp.where(rowi >= r - 1, pltpu.roll(dun, r - 1, 0), pltpu.roll(du, r - 1, 0))
            d2 = jnp.where(rowi >= r - 2, pltpu.roll(dun, r - 2, 0), pltpu.roll(du, r - 2, 0))
            dup_ref[pl.ds(r0, r), :] = (wv[2:3, :] * du + wv[1:2, :] * d1 + wv[0:1, :] * d2).astype(dup_ref.dtype)
            dw0 = dw0 + jnp.sum(du * x2, axis=0, keepdims=True)
            dw1 = dw1 + jnp.sum(du * x1, axis=0, keepdims=True)
            dw2 = dw2 + jnp.sum(du * xc, axis=0, keepdims=True)
            dbs = dbs + jnp.sum(du, axis=0, keepdims=True)
            return du, dw0, dw1, dw2, dbs

        z1 = jnp.zeros((1, 2 * LANES), F32)
        _, dw0, dw1, dw2, dbs = lax.fori_loop(0, nch, step, (jnp.zeros((r, 2 * LANES), F32), z1, z1, z1, z1))

        @pl.when(pl.program_id(1) == 0)
        def _():
            dw_ref[...] = jnp.zeros_like(dw_ref)
            db_ref[...] = jnp.zeros_like(db_ref)

        dw_ref[0:1, :] += dw0
        dw_ref[1:2, :] += dw1
        dw_ref[2:3, :] += dw2
        db_ref[...] += dbs

    return pl.pallas_call(
        body, name=name, grid=(nct, nb),
        in_specs=[pl.BlockSpec((lp, 2 * LANES), lambda c, b: (b, c)),
                  pl.BlockSpec((lp, LANES), lambda c, b: (b, c)),
                  pl.BlockSpec((CONV_WIDTH, 2 * LANES), lambda c, b: (0, c)),
                  pl.BlockSpec((1, 2 * LANES), lambda c, b: (0, c))],
        out_specs=[pl.BlockSpec((lp, 2 * LANES), lambda c, b: (b, c)),
                   pl.BlockSpec((CONV_WIDTH, 2 * LANES), lambda c, b: (0, c)),
                   pl.BlockSpec((1, 2 * LANES), lambda c, b: (0, c))],
        out_shape=[jax.ShapeDtypeStruct((m, 2 * D_FF), BF16),
                   jax.ShapeDtypeStruct((CONV_WIDTH, 2 * D_FF), F32),
                   jax.ShapeDtypeStruct((1, 2 * D_FF), F32)],
        compiler_params=_cp("parallel", "arbitrary"),
    )(up, dact, cw, cb)


def _ffn_interleave(a, axis):
    shp = a.shape
    a = a.reshape(shp[:axis] + (2, D_FF // LANES, LANES) + shp[axis + 1:])
    return jnp.swapaxes(a, axis, axis + 1).reshape(shp)


def _ffn_deinterleave(a, axis):
    shp = a.shape
    a = a.reshape(shp[:axis] + (D_FF // LANES, 2, LANES) + shp[axis + 1:])
    return jnp.swapaxes(a, axis, axis + 1).reshape(shp)


def _shifted_rows(prev_ref, cur_ref):
    keep = SEQ_BLOCK - N_META
    return jnp.concatenate([prev_ref[keep:, :], cur_ref[:keep, :]], axis=0)


def _frame_specs(nblk, nfb, d):
    prev = pl.BlockSpec((SEQ_BLOCK, d), lambda b, i: (b * nfb + jnp.clip(i - 1, 0, nfb - 1), 0))
    cur = pl.BlockSpec((SEQ_BLOCK, d), lambda b, i: (b * nfb + jnp.clip(i, 0, nfb - 1), 0))
    return prev, cur


def _embed_rms(x2, meta, gain, nb, lp, l, name):
    d = x2.shape[1]
    tr = SEQ_BLOCK
    nblk = lp // tr
    nfb = (l - N_META) // tr
    m = nb * lp

    def body(prev_ref, cur_ref, meta_ref, g_ref, h_ref, o_ref):
        i = pl.program_id(1)
        t = i * tr + lax.broadcasted_iota(jnp.int32, (tr, 1), 0)
        rows = jnp.where(t < l, _shifted_rows(prev_ref, cur_ref), 0.0)
        head = jnp.concatenate([meta_ref[...], jnp.zeros((tr - N_META, d), F32)], axis=0)
        xv = jnp.where(t < N_META, head, rows)
        h_ref[...] = xv
        r = lax.rsqrt(jnp.mean(xv * xv, axis=-1, keepdims=True) + EPS)
        o_ref[...] = ((xv * r) * g_ref[...]).astype(o_ref.dtype)

    prev, cur = _frame_specs(nblk, nfb, d)
    row = pl.BlockSpec((tr, d), lambda b, i: (b * nblk + i, 0))
    return pl.pallas_call(
        body, name=name, grid=(nb, nblk),
        in_specs=[prev, cur, pl.BlockSpec((N_META, d), lambda b, i: (0, 0)), pl.BlockSpec((1, d), lambda b, i: (0, 0))],
        out_specs=[row, row],
        out_shape=[jax.ShapeDtypeStruct((m, d), F32), jax.ShapeDtypeStruct((m, d), BF16)],
        compiler_params=_cp("parallel", "parallel"),
    )(x2, x2, meta, gain)


def _loss_head(out, tgt2, nb, lp, l, name):
    m, d = out.shape
    tr = SEQ_BLOCK
    nblk = lp // tr
    nfb = (l - N_META) // tr

    def body(o_ref, prev_ref, cur_ref, dy_ref, ls_ref):
        t = pl.program_id(1) * tr + lax.broadcasted_iota(jnp.int32, (tr, 1), 0)
        valid = (t >= N_META) & (t < l)
        err = jnp.where(valid, o_ref[...] - _shifted_rows(prev_ref, cur_ref), 0.0)
        dy_ref[...] = err * (1.0 / d)
        part = jnp.sum(err * err, axis=0, keepdims=True)
        first = (pl.program_id(0) == 0) & (pl.program_id(1) == 0)

        @pl.when(first)
        def _():
            ls_ref[...] = part

        @pl.when(jnp.logical_not(first))
        def _():
            ls_ref[...] += part

    prev, cur = _frame_specs(nblk, nfb, d)
    row = pl.BlockSpec((tr, d), lambda b, i: (b * nblk + i, 0))
    return pl.pallas_call(
        body, name=name, grid=(nb, nblk),
        in_specs=[row, prev, cur], out_specs=[row, pl.BlockSpec((1, d), lambda b, i: (0, 0))],
        out_shape=[jax.ShapeDtypeStruct((m, d), F32), jax.ShapeDtypeStruct((1, d), F32)],
        compiler_params=_cp("arbitrary", "arbitrary"),
    )(out, tgt2, tgt2)


def _adam_math(g, w, mom, var):
    c1 = 1.0 - ADAM_B1 ** ADAM_STEP
    c2 = 1.0 - ADAM_B2 ** ADAM_STEP
    mn = ADAM_B1 * mom + (1.0 - ADAM_B1) * g
    vn = ADAM_B2 * var + (1.0 - ADAM_B2) * (g * g)
    delta = -ADAM_LR * ((mn / c1) / (jnp.sqrt(vn / c2) + ADAM_EPS) + ADAM_WD * w)
    return delta, mn, vn


def _slot_sum(recv, name):
    _, r, c = recv.shape
    tc = _div_tile(c, 256, LANES)

    def body(r_ref, g_ref):
        g = r_ref[0].astype(F32)
        for s in range(1, N_DEV):
            g = g + r_ref[s].astype(F32)
        g_ref[...] = g

    return pl.pallas_call(
        body, name=name, grid=(c // tc,),
        in_specs=[pl.BlockSpec((N_DEV, r, tc), lambda j: (0, 0, j))],
        out_specs=pl.BlockSpec((r, tc), lambda j: (0, j)),
        out_shape=jax.ShapeDtypeStruct((r, c), F32),
        compiler_params=_cp("parallel"),
    )(recv)


def _adamw(g, w, mom, var, name):
    r, c = w.shape
    tr = _div_tile(r, 256, 8)

    def body(g_ref, w_ref, m_ref, v_ref, d_ref, mo_ref, vo_ref):
        d_ref[...], mo_ref[...], vo_ref[...] = _adam_math(g_ref[...], w_ref[...], m_ref[...], v_ref[...])

    row = pl.BlockSpec((tr, c), lambda i: (i, 0))
    return pl.pallas_call(
        body, name=name, grid=(r // tr,), in_specs=[row] * 4, out_specs=[row] * 3,
        out_shape=[jax.ShapeDtypeStruct((r, c), F32)] * 3,
        compiler_params=_cp("parallel"),
    )(g, w, mom, var)


def _sum_adamw(recv, w, mom, var, name):
    r, c = w.shape
    tr = _div_tile(r, 256, 8)

    def body(r_ref, w_ref, m_ref, v_ref, g_ref, d_ref, mo_ref, vo_ref):
        g = r_ref[0].astype(F32)
        for s in range(1, N_DEV):
            g = g + r_ref[s].astype(F32)
        g_ref[...] = g
        d_ref[...], mo_ref[...], vo_ref[...] = _adam_math(g, w_ref[...], m_ref[...], v_ref[...])

    row = pl.BlockSpec((tr, c), lambda i: (i, 0))
    return pl.pallas_call(
        body, name=name, grid=(r // tr,),
        in_specs=[pl.BlockSpec((N_DEV, tr, c), lambda i: (0, i, 0)), row, row, row],
        out_specs=[row] * 4,
        out_shape=[jax.ShapeDtypeStruct((r, c), F32)] * 4,
        compiler_params=_cp("parallel"),
    )(recv, w, mom, var)


_MESH = pl.DeviceIdType.MESH
_HBM = pl.BlockSpec(memory_space=pltpu.HBM)
N_PEER = N_DEV - 1


def _position():
    return lax.axis_index("x"), lax.axis_index("y"), lax.axis_index("c")


def _all_gather(shards, name):
    n = len(shards)

    def body(*refs):
        x_refs, out_refs = refs[:n], refs[n:2 * n]
        send_sems, recv_sems, local_sems = refs[2 * n:]
        x, y, c = _position()
        me, sibling = (x, y, c), (x, y, 1 - c)
        chips = [(1 - x, y), (x, 1 - y), (1 - x, 1 - y)]

        def copy(a, k, block, to, src=None):
            slot = out_refs[a].at[4 * block[0] + 2 * block[1] + block[2]]
            return pltpu.make_async_remote_copy(
                src_ref=slot if src is None else src, dst_ref=slot,
                send_sem=send_sems.at[a * N_PEER + k], recv_sem=recv_sems.at[a * N_PEER + k],
                device_id=to, device_id_type=_MESH)

        mine, sent = [], []
        for a in range(n):
            cp = pltpu.make_async_copy(x_refs[a], out_refs[a].at[4 * x + 2 * y + c], local_sems.at[a])
            cp.start()
            mine.append(cp)
            first = [copy(a, 0, me, sibling, src=x_refs[a])]
            first += [copy(a, 1 + j, me, (*chip, c), src=x_refs[a]) for j, chip in enumerate(chips)]
            for cp in first:
                cp.start()
            sent += first
        for a in range(n):
            for j, chip in enumerate(chips):
                copy(a, 1 + j, (*chip, c), me).wait_recv()
                fwd = copy(a, 4 + j, (*chip, c), sibling)
                fwd.start()
                sent.append(fwd)
        for a in range(n):
            copy(a, 0, sibling, me).wait_recv()
            for j, chip in enumerate(chips):
                copy(a, 4 + j, (*chip, 1 - c), me).wait_recv()
        for cp in sent:
            cp.wait_send()
        for cp in mine:
            cp.wait()

    return pl.pallas_call(
        body, name=name,
        out_shape=[jax.ShapeDtypeStruct((N_DEV,) + a.shape, a.dtype) for a in shards],
        in_specs=[_HBM] * n, out_specs=[_HBM] * n,
        scratch_shapes=[pltpu.SemaphoreType.DMA((n * N_PEER,)), pltpu.SemaphoreType.DMA((n * N_PEER,)),
                        pltpu.SemaphoreType.DMA((n,))],
    )(*shards)


def _exchange(blocks, shared, name):
    flips = [(fx, fy, fc) for fx in (0, 1) for fy in (0, 1) for fc in (0, 1)][1:]
    nblk, n = len(blocks), len(blocks) + len(shared)

    def body(*refs):
        in_refs, out_refs = refs[:n], refs[n:2 * n]
        send_sems, recv_sems, local_sems = refs[2 * n:]
        x, y, c = _position()
        me = 4 * x + 2 * y + c

        def peer(f):
            return (1 - x if f[0] else x, 1 - y if f[1] else y, 1 - c if f[2] else c)

        def idx(p):
            return 4 * p[0] + 2 * p[1] + p[2]

        def src(a, p):
            return in_refs[a].at[idx(p)] if a < nblk else in_refs[a]

        mine, sends = [], []
        for a in range(n):
            cp = pltpu.make_async_copy(in_refs[a].at[me] if a < nblk else in_refs[a], out_refs[a].at[me], local_sems.at[a])
            cp.start()
            mine.append(cp)
            for k, f in enumerate(flips):
                p = peer(f)
                cp = pltpu.make_async_remote_copy(
                    src_ref=src(a, p), dst_ref=out_refs[a].at[me],
                    send_sem=send_sems.at[a * N_PEER + k], recv_sem=recv_sems.at[a * N_PEER + k],
                    device_id=p, device_id_type=_MESH)
                cp.start()
                sends.append(cp)
        for a in range(n):
            for k, f in enumerate(flips):
                p = peer(f)
                pltpu.make_async_remote_copy(
                    src_ref=src(a, p), dst_ref=out_refs[a].at[idx(p)],
                    send_sem=send_sems.at[a * N_PEER + k], recv_sem=recv_sems.at[a * N_PEER + k],
                    device_id=p, device_id_type=_MESH).wait_recv()
        for cp in sends:
            cp.wait_send()
        for cp in mine:
            cp.wait()

    arrays = list(blocks) + list(shared)
    out_shape = [jax.ShapeDtypeStruct(a.shape, a.dtype) for a in blocks]
    out_shape += [jax.ShapeDtypeStruct((N_DEV,) + a.shape, a.dtype) for a in shared]
    return pl.pallas_call(
        body, name=name, out_shape=out_shape,
        in_specs=[_HBM] * n, out_specs=[_HBM] * n,
        scratch_shapes=[pltpu.SemaphoreType.DMA((n * N_PEER,)), pltpu.SemaphoreType.DMA((n * N_PEER,)),
                        pltpu.SemaphoreType.DMA((n,))],
    )(*arrays)


def _pack(parts, rows):
    flat = jnp.concatenate(parts, axis=-1)
    return jnp.pad(flat, [(0, rows * LANES - flat.shape[-1])]).reshape(rows, LANES)


def _unpack(packed, shapes):
    flat = packed.reshape(-1)
    out, off = [], 0
    for shp in shapes:
        n = int(np.prod(shp))
        out.append(flat[off:off + n].reshape(shp))
        off += n
    return out


def _rows_for(shapes, extra=0):
    n = sum(int(np.prod(s)) for s in shapes) + extra
    return -(-n // (8 * LANES)) * 8


def _lower_bound(logits):
    return jnp.cumsum(jax.nn.softmax(logits.astype(F32), axis=0), axis=0)[0:1]


def _align_axis0(w):
    a, b = 3 * FOX_WIDTH, 3 * FOX_WIDTH + FOX_HEADS
    c = b + 4 * HG_WIDTH
    pad = [(0, LANES - FOX_HEADS)] + [(0, 0)] * (w.ndim - 1)
    return jnp.concatenate([w[c:], w[:a], w[b:c], jnp.pad(w[a:b], pad)], axis=0)


def _unalign_axis0(g):
    a, b = 2 * D_MODEL, 2 * D_MODEL + 3 * FOX_WIDTH
    c = b + 4 * HG_WIDTH
    return jnp.concatenate([g[a:b], g[c:c + FOX_HEADS], g[b:c], g[:a]], axis=0)


TINY_COLS = 768


def _tiny_pack(conv_w_shard, meta_shard):
    cw = jnp.pad(conv_w_shard, ((0, 8 - CONV_WIDTH), (0, TINY_COLS - conv_w_shard.shape[1])))
    mt = jnp.pad(meta_shard, ((0, 0), (0, TINY_COLS - meta_shard.shape[1])))
    return jnp.concatenate([cw, mt], axis=0)


def _tiny_unpack(t, ncw, nmeta):
    return t[..., :CONV_WIDTH, :ncw], t[..., 8:8 + N_META, :nmeta]


def _local_step(x, target, meta, norm1_gain, w_in_t, fox_b_f, q_gain, k_gain, lb, hg_out_gain, w_a_t, w_b_t, w_out,
                norm2_gain, w_up_t, conv_w, conv_b, w_down):
    nb, seq, d = x.shape
    assert seq % SEQ_BLOCK == 0 and N_META < SEQ_BLOCK
    l = seq + N_META
    lp = -(-l // SEQ_BLOCK) * SEQ_BLOCK
    m = nb * lp
    qg = jnp.tile(q_gain, (1, FOX_HEADS))
    kg = jnp.tile(k_gain, (1, FOX_HEADS))
    bf = jnp.pad(fox_b_f, ((0, 0), (0, LANES - FOX_HEADS)))

    h0, xn = _embed_rms(x.reshape(nb * seq, d), meta, norm1_gain, nb, lp, l, "embed_rms1")
    proj = _matmul(xn, w_in_t, "nt", F32, "proj_in")
    qa, ka, vb = _fox_prep(proj, qg, kg, bf, nb, lp, "fox_prep")
    o_fox, lse = _fox_fwd(qa, ka, vb, nb, lp, "fox_fwd")
    o_raw, o_hg, s_save = _hgrn_fwd(proj, lb, hg_out_gain, nb, lp, "hgrn_fwd")
    ya = _matmul(o_hg, w_a_t, "nt", F32, "branch_a")
    yb = _matmul(o_fox, w_b_t, "nt", F32, "branch_b")
    merged = _gate_fwd(proj, ya, yb, "gate_fwd")
    h1 = _matmul(merged, w_out, "nn", F32, "mix_out", residual=h0)
    hn = _rms_fwd(h1, norm2_gain, "rms2_fwd")
    up = _matmul(hn, w_up_t, "nt", F32, "ffn_up")
    act = _conv_fwd(up, conv_w, conv_b, nb, lp, "conv_fwd")
    out = _matmul(act, w_down, "nn", F32, "ffn_down", residual=h1)
    dy, lsum = _loss_head(out, target.reshape(nb * seq, d), nb, lp, l, "loss_head")
    loss = (0.5 / d) * jnp.sum(lsum)

    dact = _matmul(dy, w_down, "nt", F32, "d_act")
    g_w_down = _matmul(act, dy, "tn", F32, "g_w_down")
    dup, g_conv_w, g_conv_b = _conv_bwd(up, dact, conv_w, conv_b, nb, lp, "conv_bwd")
    dhn = _matmul(dup, w_up_t, "nn", F32, "d_hn")
    g_w_up_t = _matmul(dup, hn, "tn", F32, "g_w_up")
    dh1, g_norm2 = _rms_bwd(h1, norm2_gain, dhn, dy, "rms2_bwd")

    dmerged = _matmul(dh1, w_out, "nt", F32, "d_merged")
    g_w_out = _matmul(merged, dh1, "tn", F32, "g_w_out")
    dya, dyb, dgab = _gate_bwd(proj, ya, yb, dmerged, "gate_bwd")
    do_hg = _matmul(dya, w_a_t, "nn", F32, "d_o_hg")
    g_w_a_t = _matmul(dya, o_hg, "tn", F32, "g_w_a")
    do_fox = _matmul(dyb, w_b_t, "nn", F32, "d_o_fox")
    g_w_b_t = _matmul(dyb, o_fox, "tn", F32, "g_w_b")
    dhq, dhf, dhi, dhg, g_hg_gain, g_lb = _hgrn_bwd(proj, o_raw, s_save, do_hg, lb, hg_out_gain, nb, lp, "hgrn_bwd")
    dqs, dkn, dvv, dc0, dc1 = _fox_bwd(qa, ka, vb, do_fox, o_fox, lse, nb, lp, "fox_bwd")
    dcum = jnp.stack([dc0, dc1], axis=2).reshape(nb, FOX_HEADS, lp)
    dcum = jnp.pad(jnp.transpose(dcum, (0, 2, 1)), ((0, 0), (0, 0), (0, LANES - FOX_HEADS))).reshape(m, LANES)
    dfqkv, dff, g_qg, g_kg, g_bf = _fox_prep_bwd(proj, dqs, dkn, dvv, dcum, qg, kg, bf, nb, lp, "fox_prep_bwd")
    dproj = jnp.concatenate([dgab, dfqkv, dhq, dhf, dhi, dhg, dff], axis=1)
    dxn = _matmul(dproj, w_in_t, "nn", F32, "d_xn")
    g_w_in_t = _matmul(dproj, xn, "tn", F32, "g_w_in")
    dh0, g_norm1 = _rms_bwd(h0, norm1_gain, dxn, dh1, "rms1_bwd")

    dh0 = dh0.reshape(nb, lp, d)
    grad_x = dh0[:, N_META:l]
    g_meta = jnp.sum(dh0[:, :N_META], axis=0)
    g_q_gain = jnp.sum(g_qg.reshape(FOX_HEADS, FOX_HEAD_DIM), axis=0, keepdims=True)
    g_k_gain = jnp.sum(g_kg.reshape(FOX_HEADS, FOX_HEAD_DIM), axis=0, keepdims=True)
    grads = dict(meta_tokens=g_meta, norm1_gain=g_norm1, w_in_t=g_w_in_t, fox_b_f=g_bf[:, :FOX_HEADS],
                 q_norm_gain=g_q_gain, k_norm_gain=g_k_gain, lb=g_lb, hg_out_gain=g_hg_gain,
                 w_a_t=g_w_a_t, w_b_t=g_w_b_t, w_out=g_w_out, norm2_gain=g_norm2, w_up_t=g_w_up_t,
                 conv_w=g_conv_w, conv_b=g_conv_b, w_down=g_w_down)
    return loss, grad_x, grads


SMALL = ("norm1_gain", "fox_b_f", "q_norm_gain", "k_norm_gain", "hg_lb_logits", "hg_out_gain", "norm2_gain", "conv_b")
ORDER = ("meta_tokens", "norm1_gain", "w_in", "fox_b_f", "q_norm_gain", "k_norm_gain", "hg_lb_logits", "hg_out_gain",
         "w_branch_a", "w_branch_b", "w_out", "norm2_gain", "w_up", "conv_w", "conv_b", "w_down")


def kernel(x, meta_tokens, norm1_gain, w_in, fox_b_f, q_norm_gain, k_norm_gain, hg_lb_logits, hg_out_gain, w_branch_a, w_branch_b, w_out, norm2_gain, w_up, conv_w, conv_b, w_down, loss_target, m_meta_tokens, m_norm1_gain, m_w_in, m_fox_b_f, m_q_norm_gain, m_k_norm_gain, m_hg_lb_logits, m_hg_out_gain, m_w_branch_a, m_w_branch_b, m_w_out, m_norm2_gain, m_w_up, m_conv_w, m_conv_b, m_w_down, v_meta_tokens, v_norm1_gain, v_w_in, v_fox_b_f, v_q_norm_gain, v_k_norm_gain, v_hg_lb_logits, v_hg_out_gain, v_w_branch_a, v_w_branch_b, v_w_out, v_norm2_gain, v_w_up, v_conv_w, v_conv_b, v_w_down):
    w = dict(meta_tokens=meta_tokens, norm1_gain=norm1_gain, w_in=w_in, fox_b_f=fox_b_f, q_norm_gain=q_norm_gain,
             k_norm_gain=k_norm_gain, hg_lb_logits=hg_lb_logits, hg_out_gain=hg_out_gain, w_branch_a=w_branch_a,
             w_branch_b=w_branch_b, w_out=w_out, norm2_gain=norm2_gain, w_up=w_up, conv_w=conv_w, conv_b=conv_b,
             w_down=w_down)
    mom = dict(meta_tokens=m_meta_tokens, norm1_gain=m_norm1_gain, w_in=m_w_in, fox_b_f=m_fox_b_f,
               q_norm_gain=m_q_norm_gain, k_norm_gain=m_k_norm_gain, hg_lb_logits=m_hg_lb_logits,
               hg_out_gain=m_hg_out_gain, w_branch_a=m_w_branch_a, w_branch_b=m_w_branch_b, w_out=m_w_out,
               norm2_gain=m_norm2_gain, w_up=m_w_up, conv_w=m_conv_w, conv_b=m_conv_b, w_down=m_w_down)
    var = dict(meta_tokens=v_meta_tokens, norm1_gain=v_norm1_gain, w_in=v_w_in, fox_b_f=v_fox_b_f,
               q_norm_gain=v_q_norm_gain, k_norm_gain=v_k_norm_gain, hg_lb_logits=v_hg_lb_logits,
               hg_out_gain=v_hg_out_gain, w_branch_a=v_w_branch_a, w_branch_b=v_w_branch_b, w_out=v_w_out,
               norm2_gain=v_norm2_gain, w_up=v_w_up, conv_w=v_conv_w, conv_b=v_conv_b, w_down=v_w_down)
    d = D_MODEL
    n_in, n_up = w_in.shape[2], w_up.shape[2]
    n_ab, n_meta = w_branch_a.shape[2], meta_tokens.shape[1]

    shards = [w_in[0].T.astype(BF16),
              w_up[0].T.astype(BF16),
              jnp.stack([w_branch_a[0].T, w_branch_b[0].T]).astype(BF16),
              w_out[0].astype(BF16),
              w_down[0].astype(BF16),
              _tiny_pack(conv_w[0], meta_tokens)]
    g_in, g_up, g_ab, g_out, g_down, g_tiny = _all_gather(shards, "gather_weights")
    w_in_t = _align_axis0(g_in.reshape(N_DEV * n_in, d))
    w_up_t = _ffn_interleave(g_up.reshape(N_DEV * n_up, d), 0)
    w_a_t = g_ab[:, 0].reshape(N_DEV * n_ab, -1)
    w_b_t = g_ab[:, 1].reshape(N_DEV * n_ab, -1)
    cw_slots, meta_slots = _tiny_unpack(g_tiny, n_up, n_meta)
    conv_w_f = _ffn_interleave(jnp.transpose(cw_slots, (1, 0, 2)).reshape(CONV_WIDTH, -1), 1)
    meta_f = jnp.transpose(meta_slots, (1, 0, 2)).reshape(N_META, -1)
    conv_b_i = _ffn_interleave(conv_b, 1)

    lb, lb_vjp = jax.vjp(_lower_bound, hg_lb_logits)
    loss, grad_x, g = _local_step(
        x, loss_target, meta_f, norm1_gain, w_in_t, fox_b_f, q_norm_gain, k_norm_gain, lb, hg_out_gain,
        w_a_t, w_b_t, g_out.reshape(d, d), norm2_gain, w_up_t, conv_w_f, conv_b_i, g_down.reshape(-1, d))

    g["hg_lb_logits"] = lb_vjp(g.pop("lb"))[0]
    g["conv_b"] = _ffn_deinterleave(g["conv_b"], 1)
    gcw = _ffn_deinterleave(g["conv_w"], 1).reshape(CONV_WIDTH, N_DEV, n_up)
    gmeta = g["meta_tokens"].reshape(N_META, N_DEV, n_meta)
    tiny = jnp.concatenate([
        jnp.pad(jnp.transpose(gcw, (1, 0, 2)), ((0, 0), (0, 8 - CONV_WIDTH), (0, TINY_COLS - n_up))),
        jnp.pad(jnp.transpose(gmeta, (1, 0, 2)), ((0, 0), (0, 0), (0, TINY_COLS - n_meta)))], axis=1)
    small_shapes = [w[n].shape for n in SMALL]
    rows_sm = _rows_for(small_shapes, extra=1)
    small = _pack([g[n].reshape(-1) for n in SMALL] + [loss.reshape(1)], rows_sm)
    blocks = [_unalign_axis0(g["w_in_t"]).reshape(N_DEV, n_in, d).astype(BF16),
              _ffn_deinterleave(g["w_up_t"], 0).reshape(N_DEV, n_up, d).astype(BF16),
              jnp.stack([g["w_a_t"].reshape(N_DEV, n_ab, -1), g["w_b_t"].reshape(N_DEV, n_ab, -1)], axis=1).astype(BF16),
              g["w_out"].reshape(N_DEV, -1, d).astype(BF16),
              g["w_down"].reshape(N_DEV, -1, d).astype(BF16),
              tiny]
    r_in, r_up, r_ab, r_out, r_down, r_tiny, r_small = _exchange(blocks, [small], "exchange_grads")

    res = {}
    g_in_s = _slot_sum(r_in, "sum_w_in").T
    res["w_in"] = (g_in_s,) + tuple(_adamw(g_in_s, w_in[0], m_w_in[0], v_w_in[0], "adamw_w_in"))
    g_up_s = _slot_sum(r_up, "sum_w_up").T
    res["w_up"] = (g_up_s,) + tuple(_adamw(g_up_s, w_up[0], m_w_up[0], v_w_up[0], "adamw_w_up"))
    g_ab_s = jnp.swapaxes(_slot_sum(r_ab.reshape(N_DEV, 2 * n_ab, -1), "sum_w_ab").reshape(2, n_ab, -1), 1, 2)
    ab = lambda t: jnp.concatenate([t["w_branch_a"][0], t["w_branch_b"][0]], axis=0)
    o_ab = (g_ab_s.reshape(-1, n_ab),) + tuple(_adamw(g_ab_s.reshape(-1, n_ab), ab(w), ab(mom), ab(var), "adamw_w_ab"))
    half = o_ab[0].shape[0] // 2
    res["w_branch_a"] = tuple(o[:half] for o in o_ab)
    res["w_branch_b"] = tuple(o[half:] for o in o_ab)
    res["w_out"] = tuple(_sum_adamw(r_out, w_out[0], m_w_out[0], v_w_out[0], "adamw_w_out"))
    res["w_down"] = tuple(_sum_adamw(r_down, w_down[0], m_w_down[0], v_w_down[0], "adamw_w_down"))
    tp = lambda t: _tiny_pack(t["conv_w"][0], t["meta_tokens"])
    o_tiny = [_tiny_unpack(o, n_up, n_meta) for o in _sum_adamw(r_tiny, tp(w), tp(mom), tp(var), "adamw_tiny")]
    res["conv_w"] = tuple(o[0] for o in o_tiny)
    res["meta_tokens"] = tuple(o[1] for o in o_tiny)
    zero1 = jnp.zeros((1,), F32)
    sp = lambda t: _pack([t[n].reshape(-1) for n in SMALL] + [zero1], rows_sm)
    o_small = [_unpack(o, small_shapes + [(1,)]) for o in _sum_adamw(r_small, sp(w), sp(mom), sp(var), "adamw_small")]
    for i, n in enumerate(SMALL):
        res[n] = tuple(o[i] for o in o_small)
    loss_all = o_small[0][len(SMALL)].reshape(())

    result = [[res[n][k].reshape(w[n].shape) for n in ORDER] for k in range(4)]
    return (loss_all, grad_x, *result[0], *result[1], *result[2], *result[3])
```

```python
import functools

import jax
import jax.numpy as jnp
import numpy as np
from jax import lax
from jax.experimental import pallas as pl
from jax.experimental.pallas import tpu as pltpu

F32 = jnp.float32
BF16 = jnp.bfloat16

D_MODEL = 1024
N_META = 16
FOX_HEADS = 8
FOX_HEAD_DIM = 64
FOX_WIDTH = FOX_HEADS * FOX_HEAD_DIM
HG_HEADS = 4
HG_DIM = 128
HG_WIDTH = HG_HEADS * HG_DIM
D_FF = 2816
CONV_WIDTH = 3
EPS = 1e-6
IN_COLS = 3 * FOX_WIDTH + FOX_HEADS + 4 * HG_WIDTH + 2 * D_MODEL
N_DEV = 8

ADAM_LR = 0.001
ADAM_B1 = 0.9
ADAM_B2 = 0.999
ADAM_EPS = 1e-08
ADAM_WD = 0.01
ADAM_STEP = 10

LANES = 128
SEQ_BLOCK = 128
SUB = 16
NEG = -1e30
VMEM_LIMIT = 48 * 1024 * 1024

FOX_CB = 2 * D_MODEL // FOX_WIDTH
CB_HQ = (2 * D_MODEL + 3 * FOX_WIDTH) // LANES
CB_HF = CB_HQ + HG_HEADS
CB_HI = CB_HF + HG_HEADS
CB_HG = CB_HI + HG_HEADS
CB_FF = CB_HG + HG_HEADS


def _div_tile(n, target, mult):
    best = None
    for t in range(mult, min(n, target) + 1, mult):
        if n % t == 0:
            best = t
    if best is None:
        best = n
    return best


def _cp(*sem):
    return pltpu.CompilerParams(dimension_semantics=sem, vmem_limit_bytes=VMEM_LIMIT)


def _sigmoid(x):
    return 0.5 * jnp.tanh(0.5 * x) + 0.5


def _dot(a, b, dims, precision=None):
    return lax.dot_general(a, b, (dims, ((), ())), preferred_element_type=F32, precision=precision)


NN = ((1,), (0,))
NT = ((1,), (1,))
TN = ((0,), (0,))
HI = lax.Precision.HIGHEST


MATMUL_VMEM_BUDGET = 30 * 1024 * 1024
MATMUL_MAX_TILE = 2048


def _tile_options(n):
    return [t for t in range(LANES, min(n, MATMUL_MAX_TILE) + 1, LANES) if n % t == 0] or [n]


def _matmul_tiles(m, n, k, a_bytes, b_bytes, o_bytes, has_res):
    tk = _div_tile(k, MATMUL_MAX_TILE, LANES)
    best = None
    for tm in _tile_options(m):
        for tn in _tile_options(n):
            vmem = 2 * (tm * tk * a_bytes + tk * tn * b_bytes) + 2 * tm * tn * o_bytes
            vmem += tm * tn * 4 if tk < k else 0
            vmem += 2 * tm * tn * 4 if has_res else 0
            if vmem > MATMUL_VMEM_BUDGET:
                continue
            key = (tm * tn, tn % 256 == 0, tn)
            if best is None or key > best[0]:
                best = (key, tm, tn)
    assert best is not None, (m, n, k)
    return best[1], best[2], tk


def _matmul(a, b, mode, out_dtype, name, residual=None):
    if mode == "nn":
        (m, k), (k2, n) = a.shape, b.shape
    elif mode == "nt":
        (m, k), (n, k2) = a.shape, b.shape
    else:
        (k, m), (k2, n) = a.shape, b.shape
    assert k == k2, (a.shape, b.shape, mode)
    has_res = residual is not None
    tm, tn, tk = _matmul_tiles(m, n, k, a.dtype.itemsize, b.dtype.itemsize, jnp.dtype(out_dtype).itemsize, has_res)
    nk = k // tk
    if mode == "nn":
        a_spec = pl.BlockSpec((tm, tk), lambda i, j, kk: (i, kk))
        b_spec = pl.BlockSpec((tk, tn), lambda i, j, kk: (kk, j))
        dims = NN
    elif mode == "nt":
        a_spec = pl.BlockSpec((tm, tk), lambda i, j, kk: (i, kk))
        b_spec = pl.BlockSpec((tn, tk), lambda i, j, kk: (j, kk))
        dims = NT
    else:
        a_spec = pl.BlockSpec((tk, tm), lambda i, j, kk: (kk, i))
        b_spec = pl.BlockSpec((tk, tn), lambda i, j, kk: (kk, j))
        dims = TN
    o_spec = pl.BlockSpec((tm, tn), lambda i, j, kk: (i, j))

    def body(*refs):
        a_ref, b_ref = refs[0], refs[1]
        r_ref = refs[2] if has_res else None
        o_ref = refs[3] if has_res else refs[2]
        part = _dot(a_ref[...].astype(BF16), b_ref[...].astype(BF16), dims)
        if nk == 1:
            o_ref[...] = (part + r_ref[...] if has_res else part).astype(o_ref.dtype)
            return
        acc_ref = refs[-1]
        kk = pl.program_id(2)

        @pl.when(kk == 0)
        def _():
            acc_ref[...] = part

        @pl.when(kk > 0)
        def _():
            acc_ref[...] += part

        @pl.when(kk == nk - 1)
        def _():
            acc = acc_ref[...]
            if has_res:
                acc = acc + r_ref[...]
            o_ref[...] = acc.astype(o_ref.dtype)

    in_specs = [a_spec, b_spec] + ([o_spec] if has_res else [])
    args = (a, b) + ((residual,) if has_res else ())
    return pl.pallas_call(
        body, name=name, grid=(m // tm, n // tn, nk),
        in_specs=in_specs, out_specs=o_spec,
        out_shape=jax.ShapeDtypeStruct((m, n), out_dtype),
        scratch_shapes=[pltpu.VMEM((tm, tn), F32)] if nk > 1 else [],
        compiler_params=_cp("parallel", "parallel", "arbitrary"),
    )(*args)


def _rms_fwd(x, gain, name):
    m, d = x.shape
    tm = _div_tile(m, 512, 16)

    def body(x_ref, g_ref, o_ref):
        xv = x_ref[...]
        r = lax.rsqrt(jnp.mean(xv * xv, axis=-1, keepdims=True) + EPS)
        o_ref[...] = ((xv * r) * g_ref[...]).astype(o_ref.dtype)

    return pl.pallas_call(
        body, name=name, grid=(m // tm,),
        in_specs=[pl.BlockSpec((tm, d), lambda i: (i, 0)), pl.BlockSpec((1, d), lambda i: (0, 0))],
        out_specs=pl.BlockSpec((tm, d), lambda i: (i, 0)),
        out_shape=jax.ShapeDtypeStruct((m, d), BF16),
        compiler_params=_cp("parallel"),
    )(x, gain)


def _rms_bwd(x, gain, dy, dres, name):
    m, d = x.shape
    tm = _div_tile(m, 256, 8)

    def body(x_ref, g_ref, dy_ref, dr_ref, dx_ref, dg_ref):
        xv = x_ref[...]
        r = lax.rsqrt(jnp.mean(xv * xv, axis=-1, keepdims=True) + EPS)
        nv = xv * r
        dyv = dy_ref[...]
        gdy = dyv * g_ref[...]
        dx_ref[...] = dr_ref[...] + r * (gdy - nv * jnp.mean(gdy * nv, axis=-1, keepdims=True))
        part = jnp.sum(dyv * nv, axis=0, keepdims=True)

        @pl.when(pl.program_id(0) == 0)
        def _():
            dg_ref[...] = part

        @pl.when(pl.program_id(0) > 0)
        def _():
            dg_ref[...] += part

    row = pl.BlockSpec((tm, d), lambda i: (i, 0))
    vec = pl.BlockSpec((1, d), lambda i: (0, 0))
    return pl.pallas_call(
        body, name=name, grid=(m // tm,),
        in_specs=[row, vec, row, row], out_specs=[row, vec],
        out_shape=[jax.ShapeDtypeStruct((m, d), F32), jax.ShapeDtypeStruct((1, d), F32)],
        compiler_params=_cp("arbitrary"),
    )(x, gain, dy, dres)


def _head_stats(xv, lo):
    sq = xv * xv
    s_lo = jnp.sum(jnp.where(lo, sq, 0.0), axis=1, keepdims=True)
    s_hi = jnp.sum(jnp.where(lo, 0.0, sq), axis=1, keepdims=True)
    return jnp.where(lo, s_lo, s_hi) * (1.0 / FOX_HEAD_DIM)


BIAS_LANE = FOX_HEAD_DIM
N_SPLIT = 3


def _split3(c):
    c1 = c.astype(BF16).astype(F32)
    r1 = c - c1
    c2 = r1.astype(BF16).astype(F32)
    c3 = (r1 - c2).astype(BF16).astype(F32)
    return c1, c2, c3


def _fox_prep(proj, qg, kg, bf, nb, lp, name):
    m = proj.shape[0]
    ts = SEQ_BLOCK
    nblk = lp // ts
    scale = FOX_HEAD_DIM ** -0.5

    def body(q_ref, k_ref, v_ref, f_ref, qg_ref, kg_ref, bf_ref, qo_ref, ko_ref, vo_ref, carry_ref):
        lane = lax.broadcasted_iota(jnp.int32, (1, LANES), 1)
        lo = lane < FOX_HEAD_DIM

        @pl.when(pl.program_id(1) == 0)
        def _():
            carry_ref[...] = jnp.zeros_like(carry_ref)

        z = f_ref[...] + bf_ref[...]
        logf = jnp.minimum(z, 0.0) - jnp.log(1.0 + jnp.exp(-jnp.abs(z)))
        logf = jnp.where(lane < FOX_HEADS, logf, 0.0)
        r = lax.broadcasted_iota(jnp.int32, (ts, ts), 0)
        c = lax.broadcasted_iota(jnp.int32, (ts, ts), 1)
        tri = jnp.where(c <= r, 1.0, 0.0).astype(F32)
        cum = _dot(tri, logf, NN, HI) + carry_ref[...]
        carry_ref[...] = cum[ts - 1:ts, :]

        ones = jnp.where((lane >= BIAS_LANE + N_SPLIT) & (lane < BIAS_LANE + 2 * N_SPLIT), 1.0, 0.0)
        ones_k = jnp.where((lane >= BIAS_LANE) & (lane < BIAS_LANE + N_SPLIT), 1.0, 0.0)
        for j in range(FOX_WIDTH // LANES):
            cs = slice(j * LANES, (j + 1) * LANES)
            xq = q_ref[:, cs]
            yq = ((xq * lax.rsqrt(_head_stats(xq, lo) + EPS)) * qg_ref[:, cs]) * scale
            xk = k_ref[:, cs]
            yk = (xk * lax.rsqrt(_head_stats(xk, lo) + EPS)) * kg_ref[:, cs]
            for hh in range(2):
                h = 2 * j + hh
                pieces = _split3(_lane_pick(cum, lane, h))
                qb, kb = ones, ones_k
                for i, piece in enumerate(pieces):
                    qb = jnp.where(lane == BIAS_LANE + i, piece, qb)
                    kb = jnp.where(lane == BIAS_LANE + N_SPLIT + i, -piece, kb)
                yq_h = yq if hh == 0 else pltpu.roll(yq, FOX_HEAD_DIM, 1)
                yk_h = yk if hh == 0 else pltpu.roll(yk, FOX_HEAD_DIM, 1)
                hs = slice(h * LANES, (h + 1) * LANES)
                qo_ref[:, hs] = jnp.where(lo, yq_h, qb).astype(BF16)
                ko_ref[:, hs] = jnp.where(lo, yk_h, kb).astype(BF16)
        vo_ref[...] = v_ref[...].astype(BF16)

    w = FOX_WIDTH
    row = lambda b, i: (b * nblk + i, 0)
    return pl.pallas_call(
        body, name=name, grid=(nb, nblk),
        in_specs=[pl.BlockSpec((ts, w), lambda b, i: (b * nblk + i, FOX_CB)),
                  pl.BlockSpec((ts, w), lambda b, i: (b * nblk + i, FOX_CB + 1)),
                  pl.BlockSpec((ts, w), lambda b, i: (b * nblk + i, FOX_CB + 2)),
                  pl.BlockSpec((ts, LANES), lambda b, i: (b * nblk + i, CB_FF)),
                  pl.BlockSpec((1, w), lambda b, i: (0, 0)),
                  pl.BlockSpec((1, w), lambda b, i: (0, 0)),
                  pl.BlockSpec((1, LANES), lambda b, i: (0, 0))],
        out_specs=[pl.BlockSpec((ts, 2 * w), row), pl.BlockSpec((ts, 2 * w), row), pl.BlockSpec((ts, w), row)],
        out_shape=[jax.ShapeDtypeStruct((m, 2 * w), BF16)] * 2 + [jax.ShapeDtypeStruct((m, w), BF16)],
        scratch_shapes=[pltpu.VMEM((1, LANES), F32)],
        compiler_params=_cp("arbitrary", "arbitrary"),
    )(proj, proj, proj, proj, qg, kg, bf)


def _att_tile(lp):
    return 384 if (lp % 384 == 0 and lp > 384) else 128


def _lane_pick(blk, lane, idx):
    return jnp.sum(jnp.where(lane == idx, blk, 0.0), axis=1, keepdims=True)


def _head_masks():
    lane = lax.broadcasted_iota(jnp.int32, (1, LANES), 1)
    return lane, [(lane >= hh * FOX_HEAD_DIM) & (lane < (hh + 1) * FOX_HEAD_DIM) for hh in range(2)]


def _fox_fwd(qa, ka, vb, nb, lp, name, ride=()):
    m = qa.shape[0]
    tq = _att_tile(lp)
    nq = lp // tq
    npair = FOX_WIDTH // LANES
    grid = (nb, npair, nq)
    r_arrays, r_shapes, r_sems = _exchange_io((), ride)
    nr = len(r_arrays)

    def body(q_ref, k_ref, v_ref, *rest):
        r_in, (o_ref, lse_ref), r_out, sems = rest[:nr], rest[nr:nr + 2], rest[nr + 2:2 * nr + 2], rest[2 * nr + 2:]
        if nr:
            first, last = _grid_ends(grid)
            start, wait = _exchange_copies(r_in, r_out, 0, *sems)
            pl.when(first)(start)
        qi = pl.program_id(2)
        lane, hmasks = _head_masks()
        zero16 = jnp.zeros((), BF16)
        causal = lax.broadcasted_iota(jnp.int32, (tq, 1), 0) >= lax.broadcasted_iota(jnp.int32, (1, tq), 1)
        o_tot = jnp.zeros((tq, LANES), F32)
        lse_out = jnp.zeros((tq, LANES), F32)
        for hh in range(2):
            hs = slice(hh * LANES, (hh + 1) * LANES)
            q = q_ref[:, hs]

            def tile(j, carry, diagonal, hs=hs, q=q, hmask=hmasks[hh]):
                mx, l, acc = carry
                k0 = pl.multiple_of(j * tq, tq)
                vz = jnp.where(hmask, v_ref[pl.ds(k0, tq), :], zero16)
                s = _dot(q, k_ref[pl.ds(k0, tq), hs], NT)
                if diagonal:
                    s = jnp.where(causal, s, NEG)
                m_new = jnp.maximum(mx, jnp.max(s, axis=1, keepdims=True))
                alpha = jnp.exp(mx - m_new)
                pe = jnp.exp(s - m_new)
                l = alpha * l + jnp.sum(pe, axis=1, keepdims=True)
                acc = alpha * acc + _dot(pe.astype(BF16), vz, NN)
                return m_new, l, acc

            init = (jnp.full((tq, 1), NEG, F32), jnp.zeros((tq, 1), F32), jnp.zeros((tq, LANES), F32))
            carry = lax.fori_loop(0, qi, lambda j, c, tile=tile: tile(j, c, False), init)
            mx, l, acc = tile(qi, carry, True)
            o_tot = o_tot + acc / l
            lse_out = jnp.where(lane == hh, mx + jnp.log(l), lse_out)
        o_ref[...] = o_tot
        lse_ref[...] = lse_out
        if nr:
            pl.when(last)(wait)

    return pl.pallas_call(
        body, name=name, grid=grid,
        in_specs=[pl.BlockSpec((tq, 2 * LANES), lambda b, p, i: (b * nq + i, p)),
                  pl.BlockSpec((lp, 2 * LANES), lambda b, p, i: (b, p)),
                  pl.BlockSpec((lp, LANES), lambda b, p, i: (b, p))] + [_HBM] * nr,
        out_specs=[pl.BlockSpec((tq, LANES), lambda b, p, i: (b * nq + i, p)),
                   pl.BlockSpec((None, None, tq, LANES), lambda b, p, i: (b, p, i, 0))] + [_HBM] * nr,
        out_shape=[jax.ShapeDtypeStruct((m, FOX_WIDTH), F32),
                   jax.ShapeDtypeStruct((nb, npair, lp, LANES), F32)] + r_shapes,
        scratch_shapes=r_sems if nr else [],
        compiler_params=_cp(*(["arbitrary"] * 3 if nr else ["parallel", "parallel", "arbitrary"])),
    )(qa, ka, vb, *r_arrays)


def _fox_bwd(qa, ka, vb, do, o, lse, nb, lp, name, ride=()):
    m = qa.shape[0]
    tq = _att_tile(lp)
    nq = lp // tq
    npair = FOX_WIDTH // LANES
    grid = (nb, npair, nq)
    r_arrays, r_shapes, r_sems = _exchange_io(ride, ())
    nr = len(r_arrays)

    def body(k_ref, v_ref, q_ref, do_ref, o_ref, lse_ref, *rest):
        r_in, r_out, sems = rest[:nr], rest[nr + 5:2 * nr + 5], rest[2 * nr + 5:]
        dq_ref, dk_ref, dv_ref, dc0_ref, dc1_ref = rest[nr:nr + 5]
        if nr:
            first, last = _grid_ends(grid)
            start, wait = _exchange_copies(r_in, r_out, nr, *sems)
            pl.when(first)(start)
        j = pl.program_id(2)
        lane, hmasks = _head_masks()
        zero16 = jnp.zeros((), BF16)
        causal = lax.broadcasted_iota(jnp.int32, (tq, 1), 0) >= lax.broadcasted_iota(jnp.int32, (1, tq), 1)

        @pl.when(j == 0)
        def _():
            dq_ref[...] = jnp.zeros_like(dq_ref)

        vv = v_ref[...]
        vzs = [jnp.where(hm, vv, zero16) for hm in hmasks]

        def tile(qi, carry, diagonal):
            dk0, dk1, dv, dc0, dc1 = carry
            q0 = pl.multiple_of(qi * tq, tq)
            dob16 = do_ref[pl.ds(q0, tq), :].astype(BF16)
            ob = o_ref[pl.ds(q0, tq), :]
            lseb = lse_ref[pl.ds(q0, tq), :]
            dks, dcs = [dk0, dk1], [dc0, dc1]
            for hh in range(2):
                hs = slice(hh * LANES, (hh + 1) * LANES)
                q = q_ref[pl.ds(q0, tq), hs]
                doz16 = jnp.where(hmasks[hh], dob16, zero16)
                delta = jnp.sum(doz16.astype(F32) * ob, axis=1, keepdims=True)
                s = _dot(q, k_ref[:, hs], NT) - _lane_pick(lseb, lane, hh)
                if diagonal:
                    s = jnp.where(causal, s, NEG)
                pm = jnp.exp(s)
                ds = pm * (_dot(doz16, vzs[hh], NT) - delta)
                ds16 = ds.astype(BF16)
                dv = dv + _dot(pm.astype(BF16), doz16, TN)
                dks[hh] = dks[hh] + _dot(ds16, q, TN)
                dq_ref[pl.ds(q0, tq), hs] += _dot(ds16, k_ref[:, hs], NN)
                dcs[hh] = dcs[hh] - jnp.sum(ds, axis=0, keepdims=True)
            return dks[0], dks[1], dv, dcs[0], dcs[1]

        zt = jnp.zeros((tq, LANES), F32)
        zr = jnp.zeros((1, tq), F32)
        carry = tile(j, (zt, zt, zt, zr, zr), True)
        dk0, dk1, dv, dc0, dc1 = lax.fori_loop(j + 1, nq, lambda qi, c: tile(qi, c, False), carry)
        dk_ref[:, :LANES] = dk0
        dk_ref[:, LANES:] = dk1
        dv_ref[...] = dv
        dc0_ref[...] = dc0
        dc1_ref[...] = dc1
        if nr:
            pl.when(last)(wait)

    full2 = pl.BlockSpec((lp, 2 * LANES), lambda b, p, j: (b, p))
    full = pl.BlockSpec((lp, LANES), lambda b, p, j: (b, p))
    blk2 = pl.BlockSpec((tq, 2 * LANES), lambda b, p, j: (b * nq + j, p))
    blk = pl.BlockSpec((tq, LANES), lambda b, p, j: (b * nq + j, p))
    dcs = pl.BlockSpec((None, None, 1, tq), lambda b, p, j: (b, p, 0, j))
    return pl.pallas_call(
        body, name=name, grid=grid,
        in_specs=[blk2, blk, full2, full, full,
                  pl.BlockSpec((None, None, lp, LANES), lambda b, p, j: (b, p, 0, 0))] + [_HBM] * nr,
        out_specs=[full2, blk2, blk, dcs, dcs] + [_HBM] * nr,
        out_shape=[jax.ShapeDtypeStruct((m, 2 * FOX_WIDTH), F32)] * 2 + [jax.ShapeDtypeStruct((m, FOX_WIDTH), F32)]
        + [jax.ShapeDtypeStruct((nb, npair, 1, lp), F32)] * 2 + r_shapes,
        scratch_shapes=r_sems if nr else [],
        compiler_params=_cp(*(["arbitrary"] * 3 if nr else ["parallel", "parallel", "arbitrary"])),
    )(ka, vb, qa, do, o, lse, *r_arrays)


def _fox_prep_bwd(proj, dqa, dka, dv, dcum, qg, kg, bf, nb, lp, name):
    m = proj.shape[0]
    ts = SEQ_BLOCK
    nblk = lp // ts
    scale = FOX_HEAD_DIM ** -0.5
    w = FOX_WIDTH
    wo = 3 * w

    def body(q_ref, k_ref, f_ref, dq_ref, dk_ref, dv_ref, dc_ref, qg_ref, kg_ref, bf_ref,
             out_ref, dff_ref, dqg_ref, dkg_ref, dbf_ref, carry_ref):
        first = (pl.program_id(0) == 0) & (pl.program_id(1) == 0)
        lane = lax.broadcasted_iota(jnp.int32, (1, LANES), 1)
        lo = lane < FOX_HEAD_DIM

        @pl.when(first)
        def _():
            dqg_ref[...] = jnp.zeros_like(dqg_ref)
            dkg_ref[...] = jnp.zeros_like(dkg_ref)
            dbf_ref[...] = jnp.zeros_like(dbf_ref)

        def norm_bwd(x, g, dy):
            r = lax.rsqrt(_head_stats(x, lo) + EPS)
            nv = x * r
            gdy = dy * g
            prod = gdy * nv
            s_lo = jnp.sum(jnp.where(lo, prod, 0.0), axis=1, keepdims=True)
            s_hi = jnp.sum(jnp.where(lo, 0.0, prod), axis=1, keepdims=True)
            mean = jnp.where(lo, s_lo, s_hi) * (1.0 / FOX_HEAD_DIM)
            return r * (gdy - nv * mean), jnp.sum(dy * nv, axis=0, keepdims=True)

        def pair(d_ref, jj):
            even = d_ref[:, 2 * jj * LANES:(2 * jj + 1) * LANES]
            odd = d_ref[:, (2 * jj + 1) * LANES:(2 * jj + 2) * LANES]
            return jnp.where(lo, even, pltpu.roll(odd, FOX_HEAD_DIM, 1))

        for jj in range(w // LANES):
            cs = slice(jj * LANES, (jj + 1) * LANES)
            dx, dg = norm_bwd(q_ref[:, cs], qg_ref[:, cs], pair(dq_ref, jj) * scale)
            out_ref[:, cs] = dx.astype(BF16)
            dqg_ref[:, cs] += dg
            dx, dg = norm_bwd(k_ref[:, cs], kg_ref[:, cs], pair(dk_ref, jj))
            out_ref[:, w + jj * LANES:w + (jj + 1) * LANES] = dx.astype(BF16)
            dkg_ref[:, cs] += dg
        out_ref[:, 2 * w:3 * w] = dv_ref[...].astype(BF16)

        @pl.when(pl.program_id(1) == 0)
        def _():
            carry_ref[...] = jnp.zeros_like(carry_ref)

        dc = dc_ref[...]
        r = lax.broadcasted_iota(jnp.int32, (ts, ts), 0)
        c = lax.broadcasted_iota(jnp.int32, (ts, ts), 1)
        triu = jnp.where(c >= r, 1.0, 0.0).astype(F32)
        dlogf = _dot(triu, dc, NN, HI) + carry_ref[...]
        carry_ref[...] += jnp.sum(dc, axis=0, keepdims=True)
        z = f_ref[...] + bf_ref[...]
        dz = jnp.where(lane < FOX_HEADS, dlogf * _sigmoid(-z), 0.0)
        dff_ref[...] = dz.astype(BF16)
        dbf_ref[...] += jnp.sum(dz, axis=0, keepdims=True)

    rev = lambda b, i: (b * nblk + (nblk - 1 - i), 0)
    vec = lambda n: pl.BlockSpec((1, n), lambda b, i: (0, 0))
    return pl.pallas_call(
        body, name=name, grid=(nb, nblk),
        in_specs=[pl.BlockSpec((ts, w), lambda b, i: (b * nblk + (nblk - 1 - i), FOX_CB)),
                  pl.BlockSpec((ts, w), lambda b, i: (b * nblk + (nblk - 1 - i), FOX_CB + 1)),
                  pl.BlockSpec((ts, LANES), lambda b, i: (b * nblk + (nblk - 1 - i), CB_FF)),
                  pl.BlockSpec((ts, 2 * w), rev), pl.BlockSpec((ts, 2 * w), rev), pl.BlockSpec((ts, w), rev),
                  pl.BlockSpec((ts, LANES), rev), vec(w), vec(w), vec(LANES)],
        out_specs=[pl.BlockSpec((ts, wo), rev), pl.BlockSpec((ts, LANES), rev), vec(w), vec(w), vec(LANES)],
        out_shape=[jax.ShapeDtypeStruct((m, wo), BF16), jax.ShapeDtypeStruct((m, LANES), BF16),
                   jax.ShapeDtypeStruct((1, w), F32),
                   jax.ShapeDtypeStruct((1, w), F32), jax.ShapeDtypeStruct((1, LANES), F32)],
        scratch_shapes=[pltpu.VMEM((1, LANES), F32)],
        compiler_params=_cp("arbitrary", "arbitrary"),
    )(proj, proj, proj, dqa, dka, dv, dcum, qg, kg, bf)


def _chunk_masks():
    r = lax.broadcasted_iota(jnp.int32, (SEQ_BLOCK, SEQ_BLOCK), 0)
    c = lax.broadcasted_iota(jnp.int32, (SEQ_BLOCK, SEQ_BLOCK), 1)
    same = (r // SUB) == (c // SUB)
    return r, c, same


def _hg_gates(hf, lb):
    sg = _sigmoid(hf)
    f = lb + (1.0 - lb) * sg
    return sg, f, jnp.log(f), (1.0 - lb) * _sigmoid(-hf)


def _hg_intra_e(g_ref, base, t, srow):
    diff = g_ref[pl.ds(base + t, 1), :] - g_ref[pl.ds(base, SUB), :]
    return jnp.exp(jnp.where(srow <= t, diff, NEG))


def _hgrn_fwd(proj, lb, gain, nb, lp, name):
    m = proj.shape[0]
    tb = SEQ_BLOCK
    nblk = lp // tb
    ns = tb // SUB

    def body(q_ref, f_ref, i_ref, g_ref, lb_ref, gain_ref, oraw_ref, y_ref, ssave_ref,
             st_ref, g_scr, kin_scr, o_scr):
        @pl.when(pl.program_id(2) == 0)
        def _():
            st_ref[...] = jnp.zeros_like(st_ref)

        ssave_ref[...] = st_ref[...]
        lbv = lb_ref[...]
        _, _, lf, kin = _hg_gates(f_ref[...], lbv)
        r, c, same = _chunk_masks()
        ltri = jnp.where(same & (c <= r), 1.0, 0.0).astype(F32)
        lall = jnp.where(same, 1.0, 0.0).astype(F32)
        g = _dot(ltri, lf, NN, HI)
        gt = _dot(lall, lf, NN, HI)
        g_scr[...] = g
        kin_scr[...] = kin
        qv = q_ref[...]
        qg = (qv * jnp.exp(g)).astype(BF16)
        kg = (kin * jnp.exp(gt - g)).astype(BF16)
        et = jnp.exp(gt)
        srow = lax.broadcasted_iota(jnp.int32, (SUB, 1), 0)
        subs = [slice(cc * SUB, (cc + 1) * SUB) for cc in range(ns)]
        ups = [_dot(i_ref[sl, :].astype(BF16), kg[sl], TN) for sl in subs]
        st = st_ref[...]
        starts = []
        for cc in range(ns):
            starts.append(st)
            st = et[cc * SUB:cc * SUB + 1, :] * st + ups[cc]
        st_ref[...] = st
        for cc, sl in enumerate(subs):
            base = cc * SUB
            o_c = _dot(qg[sl], starts[cc].astype(BF16), NT)
            kc = kin_scr[sl, :]
            vc = i_ref[sl, :]
            for t in range(SUB):
                e = _hg_intra_e(g_scr, base, t, srow)
                a = jnp.sum((q_ref[pl.ds(base + t, 1), :] * kc) * e, axis=1, keepdims=True)
                ot = jnp.sum(a * vc, axis=0, keepdims=True)
                o_c = o_c + jnp.where(srow == t, ot, 0.0)
            o_scr[sl, :] = o_c
        o = o_scr[...]
        oraw_ref[...] = o
        rr = lax.rsqrt(jnp.mean(o * o, axis=-1, keepdims=True) + EPS)
        hg = g_ref[...]
        y_ref[...] = (((o * rr) * gain_ref[...]) * (hg * _sigmoid(hg))).astype(y_ref.dtype)

    col = lambda cb: pl.BlockSpec((tb, LANES), lambda b, h, i, cb=cb: (b * nblk + i, cb + h))
    out_blk = pl.BlockSpec((tb, LANES), lambda b, h, i: (b * nblk + i, h))
    return pl.pallas_call(
        body, name=name, grid=(nb, HG_HEADS, nblk),
        in_specs=[col(CB_HQ), col(CB_HF), col(CB_HI), col(CB_HG),
                  pl.BlockSpec((1, LANES), lambda b, h, i: (0, h)),
                  pl.BlockSpec((1, LANES), lambda b, h, i: (0, 0))],
        out_specs=[out_blk, out_blk,
                   pl.BlockSpec((None, None, None, HG_DIM, HG_DIM), lambda b, h, i: (b, h, i, 0, 0))],
        out_shape=[jax.ShapeDtypeStruct((m, HG_WIDTH), F32), jax.ShapeDtypeStruct((m, HG_WIDTH), BF16),
                   jax.ShapeDtypeStruct((nb, HG_HEADS, nblk, HG_DIM, HG_DIM), F32)],
        scratch_shapes=[pltpu.VMEM((HG_DIM, HG_DIM), F32), pltpu.VMEM((tb, LANES), F32),
                        pltpu.VMEM((tb, LANES), F32), pltpu.VMEM((tb, LANES), F32)],
        compiler_params=_cp("parallel", "parallel", "arbitrary"),
    )(proj, proj, proj, proj, lb, gain)


def _hgrn_bwd(proj, oraw, ssave, dy, lb, gain, nb, lp, name):
    m = proj.shape[0]
    tb = SEQ_BLOCK
    nblk = lp // tb
    ns = tb // SUB

    def body(q_ref, f_ref, i_ref, g_ref, oraw_ref, ssave_ref, dy_ref, lb_ref, gain_ref,
             dq_ref, df_ref, di_ref, dg_ref, dgain_ref, dlb_ref,
             dst_ref, g_scr, kin_scr, do_scr, dq_scr, dk_scr, dv_scr, dgg_scr):
        hd = pl.program_id(0)
        bb = pl.program_id(1)
        ii = pl.program_id(2)
        gainv = gain_ref[...]
        lbv = lb_ref[...]

        @pl.when((hd == 0) & (bb == 0) & (ii == 0))
        def _():
            dgain_ref[...] = jnp.zeros_like(dgain_ref)

        @pl.when((bb == 0) & (ii == 0))
        def _():
            dlb_ref[...] = jnp.zeros_like(dlb_ref)

        @pl.when(ii == 0)
        def _():
            dst_ref[...] = jnp.zeros_like(dst_ref)

        o = oraw_ref[...]
        rr = lax.rsqrt(jnp.mean(o * o, axis=-1, keepdims=True) + EPS)
        nv = o * rr
        hg = g_ref[...]
        sgg = _sigmoid(hg)
        sil = hg * sgg
        dyv = dy_ref[...]
        dg_ref[...] = (dyv * nv * gainv * (sgg * (1.0 + hg * (1.0 - sgg)))).astype(dg_ref.dtype)
        dgain_ref[...] += jnp.sum(dyv * nv * sil, axis=0, keepdims=True)
        dn = dyv * gainv * sil
        do_scr[...] = rr * (dn - nv * jnp.mean(dn * nv, axis=-1, keepdims=True))

        hf = f_ref[...]
        sg, f, lf, kin = _hg_gates(hf, lbv)
        r, c, same = _chunk_masks()
        ltri = jnp.where(same & (c <= r), 1.0, 0.0).astype(F32)
        lall = jnp.where(same, 1.0, 0.0).astype(F32)
        g = _dot(ltri, lf, NN, HI)
        gt = _dot(lall, lf, NN, HI)
        g_scr[...] = g
        kin_scr[...] = kin
        qv = q_ref[...]
        eg = jnp.exp(g)
        ekg = jnp.exp(gt - g)
        qg = qv * eg
        kg = kin * ekg
        qg16 = qg.astype(BF16)
        kg16 = kg.astype(BF16)
        et = jnp.exp(gt)
        subs = [slice(cc * SUB, (cc + 1) * SUB) for cc in range(ns)]
        ups = [_dot(i_ref[sl, :].astype(BF16), kg16[sl], TN) for sl in subs]
        st = ssave_ref[...]
        starts = []
        for cc in range(ns):
            starts.append(st)
            st = et[cc * SUB:cc * SUB + 1, :] * st + ups[cc]
        do16 = do_scr[...].astype(BF16)
        downs = [_dot(do16[sl], qg16[sl], TN) for sl in subs]
        dst = dst_ref[...]
        afters = [None] * ns
        for cc in reversed(range(ns)):
            afters[cc] = dst
            dst = et[cc * SUB:cc * SUB + 1, :] * dst + downs[cc]
        dst_ref[...] = dst

        srow = lax.broadcasted_iota(jnp.int32, (SUB, 1), 0)
        for cc, sl in enumerate(subs):
            base = cc * SUB
            st = starts[cc]
            st16 = st.astype(BF16)
            dst = afters[cc]
            dst16 = dst.astype(BF16)
            doc16 = do16[sl]
            vc = i_ref[sl, :]
            vc16 = vc.astype(BF16)
            kc = kin_scr[sl, :]
            etc = et[base:base + 1, :]
            dqg = _dot(doc16, st16, NN)
            dv_c = _dot(kg16[sl], dst16, NT)
            dkg = _dot(vc16, dst16, NN)
            dgt = jnp.sum(dst * st, axis=0, keepdims=True) * etc
            dq_c = dqg * eg[sl]
            dk_c = dkg * ekg[sl]
            dg_c = dqg * qg[sl] - dkg * kg[sl]
            dgt = dgt + jnp.sum(dkg * kg[sl], axis=0, keepdims=True)
            for t in range(SUB):
                e = _hg_intra_e(g_scr, base, t, srow)
                qt = q_ref[pl.ds(base + t, 1), :]
                dot_t = do_scr[pl.ds(base + t, 1), :]
                a = jnp.sum((qt * kc) * e, axis=1, keepdims=True)
                da = jnp.sum(dot_t * vc, axis=1, keepdims=True)
                dv_c = dv_c + a * dot_t
                w = da * e
                dq_t = jnp.sum(w * kc, axis=0, keepdims=True)
                wq = w * qt
                dk_c = dk_c + wq
                dg_c = dg_c - kc * wq + jnp.where(srow == t, qt * dq_t, 0.0)
                dq_c = dq_c + jnp.where(srow == t, dq_t, 0.0)
            dg_c = dg_c + jnp.where(srow == SUB - 1, dgt, 0.0)
            dq_scr[sl, :] = dq_c
            dk_scr[sl, :] = dk_c
            dv_scr[sl, :] = dv_c
            dgg_scr[sl, :] = dg_c

        utri = jnp.where(same & (c >= r), 1.0, 0.0).astype(F32)
        dlf = _dot(utri, dgg_scr[...], NN, HI)
        dkin = dk_scr[...]
        dsg = sg * (1.0 - sg)
        df_ref[...] = ((dlf / f - dkin) * ((1.0 - lbv) * dsg)).astype(df_ref.dtype)
        dlb_ref[...] += jnp.sum((dlf / f - dkin) * (1.0 - sg), axis=0, keepdims=True)
        dq_ref[...] = dq_scr[...].astype(dq_ref.dtype)
        di_ref[...] = dv_scr[...].astype(di_ref.dtype)

    rowi = lambda b, i: b * nblk + (nblk - 1 - i)
    col = lambda cb: pl.BlockSpec((tb, LANES), lambda h, b, i, cb=cb: (rowi(b, i), cb + h))
    hblk = pl.BlockSpec((tb, LANES), lambda h, b, i: (rowi(b, i), h))
    return pl.pallas_call(
        body, name=name, grid=(HG_HEADS, nb, nblk),
        in_specs=[col(CB_HQ), col(CB_HF), col(CB_HI), col(CB_HG), hblk,
                  pl.BlockSpec((None, None, None, HG_DIM, HG_DIM), lambda h, b, i: (b, h, nblk - 1 - i, 0, 0)),
                  hblk,
                  pl.BlockSpec((1, LANES), lambda h, b, i: (0, h)),
                  pl.BlockSpec((1, LANES), lambda h, b, i: (0, 0))],
        out_specs=[hblk, hblk, hblk, hblk,
                   pl.BlockSpec((1, LANES), lambda h, b, i: (0, 0)),
                   pl.BlockSpec((1, LANES), lambda h, b, i: (0, h))],
        out_shape=[jax.ShapeDtypeStruct((m, HG_WIDTH), BF16)] * 4
        + [jax.ShapeDtypeStruct((1, LANES), F32), jax.ShapeDtypeStruct((1, HG_WIDTH), F32)],
        scratch_shapes=[pltpu.VMEM((HG_DIM, HG_DIM), F32)] + [pltpu.VMEM((tb, LANES), F32)] * 7,
        compiler_params=_cp("arbitrary", "arbitrary", "arbitrary"),
    )(proj, proj, proj, proj, oraw, ssave, dy, lb, gain)


def _gate_fwd(proj, ya, yb, name):
    m = proj.shape[0]
    tm = _div_tile(m, 256, 16)

    def body(ga_ref, gb_ref, ya_ref, yb_ref, o_ref):
        o_ref[...] = (_sigmoid(ga_ref[...]) * ya_ref[...] + _sigmoid(gb_ref[...]) * yb_ref[...]).astype(o_ref.dtype)

    row = pl.BlockSpec((tm, D_MODEL), lambda i: (i, 0))
    return pl.pallas_call(
        body, name=name, grid=(m // tm,),
        in_specs=[row, pl.BlockSpec((tm, D_MODEL), lambda i: (i, 1)), row, row],
        out_specs=row, out_shape=jax.ShapeDtypeStruct((m, D_MODEL), BF16),
        compiler_params=_cp("parallel"),
    )(proj, proj, ya, yb)


def _gate_bwd(proj, ya, yb, dm, name):
    m = proj.shape[0]
    tm = _div_tile(m, 256, 16)

    def body(ga_ref, gb_ref, ya_ref, yb_ref, dm_ref, dya_ref, dyb_ref, dg_ref):
        dmv = dm_ref[...]
        sa = _sigmoid(ga_ref[...])
        sb = _sigmoid(gb_ref[...])
        dya_ref[...] = (dmv * sa).astype(BF16)
        dyb_ref[...] = (dmv * sb).astype(BF16)
        dg_ref[:, :D_MODEL] = (dmv * ya_ref[...] * (sa * (1.0 - sa))).astype(BF16)
        dg_ref[:, D_MODEL:] = (dmv * yb_ref[...] * (sb * (1.0 - sb))).astype(BF16)

    row = pl.BlockSpec((tm, D_MODEL), lambda i: (i, 0))
    wide = pl.BlockSpec((tm, 2 * D_MODEL), lambda i: (i, 0))
    return pl.pallas_call(
        body, name=name, grid=(m // tm,),
        in_specs=[row, pl.BlockSpec((tm, D_MODEL), lambda i: (i, 1)), row, row, row],
        out_specs=[row, row, wide],
        out_shape=[jax.ShapeDtypeStruct((m, D_MODEL), BF16)] * 2 + [jax.ShapeDtypeStruct((m, 2 * D_MODEL), BF16)],
        compiler_params=_cp("parallel"),
    )(proj, proj, ya, yb, dm)


CONV_ROWS = 128


def _conv3(x, xprev, w_ref, b_ref, rowi):
    r = x.shape[0]
    x1 = jnp.where(rowi < 1, pltpu.roll(xprev, 1, 0), pltpu.roll(x, 1, 0))
    x2 = jnp.where(rowi < 2, pltpu.roll(xprev, 2, 0), pltpu.roll(x, 2, 0))
    u = w_ref[0:1, :] * x2 + w_ref[1:2, :] * x1 + w_ref[2:3, :] * x + b_ref[...]
    return u, x1, x2


def _conv_fwd(up, cw, cb, nb, lp, name):
    m = up.shape[0]
    nct = D_FF // LANES
    r = CONV_ROWS
    nch = lp // r

    def body(u_ref, w_ref, b_ref, o_ref):
        rowi = lax.broadcasted_iota(jnp.int32, (r, 1), 0)

        def step(i, xp):
            r0 = pl.multiple_of(i * r, r)
            xc = u_ref[pl.ds(r0, r), :]
            u, _, _ = _conv3(xc, xp, w_ref, b_ref, rowi)
            ug, uv = u[:, :LANES], u[:, LANES:]
            o_ref[pl.ds(r0, r), :] = ((ug * _sigmoid(ug)) * uv).astype(o_ref.dtype)
            return xc

        lax.fori_loop(0, nch, step, jnp.zeros((r, 2 * LANES), F32))

    return pl.pallas_call(
        body, name=name, grid=(nb, nct),
        in_specs=[pl.BlockSpec((lp, 2 * LANES), lambda b, c: (b, c)),
                  pl.BlockSpec((CONV_WIDTH, 2 * LANES), lambda b, c: (0, c)),
                  pl.BlockSpec((1, 2 * LANES), lambda b, c: (0, c))],
        out_specs=pl.BlockSpec((lp, LANES), lambda b, c: (b, c)),
        out_shape=jax.ShapeDtypeStruct((m, D_FF), BF16),
        compiler_params=_cp("parallel", "parallel"),
    )(up, cw, cb)


def _conv_bwd(up, dact, cw, cb, nb, lp, name):
    m = up.shape[0]
    nct = D_FF // LANES
    r = CONV_ROWS
    nch = lp // r

    def body(u_ref, da_ref, w_ref, b_ref, dup_ref, dw_ref, db_ref):
        rowi = lax.broadcasted_iota(jnp.int32, (r, 1), 0)
        wv = w_ref[...]

        def step(k, carry):
            dun, dw0, dw1, dw2, dbs = carry
            i = nch - 1 - k
            r0 = pl.multiple_of(i * r, r)
            rp = pl.multiple_of(jnp.maximum(i - 1, 0) * r, r)
            xc = u_ref[pl.ds(r0, r), :]
            xp = u_ref[pl.ds(rp, r), :] * (i > 0).astype(F32)
            u, x1, x2 = _conv3(xc, xp, w_ref, b_ref, rowi)
            ug, uv = u[:, :LANES], u[:, LANES:]
            da = da_ref[pl.ds(r0, r), :]
            sg = _sigmoid(ug)
            du = jnp.concatenate([da * uv * (sg * (1.0 + ug * (1.0 - sg))), da * (ug * sg)], axis=1)
            d1 = jnp.where(rowi >= r - 1, pltpu.roll(dun, r - 1, 0), pltpu.roll(du, r - 1, 0))
            d2 = jnp.where(rowi >= r - 2, pltpu.roll(dun, r - 2, 0), pltpu.roll(du, r - 2, 0))
            dup_ref[pl.ds(r0, r), :] = (wv[2:3, :] * du + wv[1:2, :] * d1 + wv[0:1, :] * d2).astype(dup_ref.dtype)
            dw0 = dw0 + jnp.sum(du * x2, axis=0, keepdims=True)
            dw1 = dw1 + jnp.sum(du * x1, axis=0, keepdims=True)
            dw2 = dw2 + jnp.sum(du * xc, axis=0, keepdims=True)
            dbs = dbs + jnp.sum(du, axis=0, keepdims=True)
            return du, dw0, dw1, dw2, dbs

        z1 = jnp.zeros((1, 2 * LANES), F32)
        _, dw0, dw1, dw2, dbs = lax.fori_loop(0, nch, step, (jnp.zeros((r, 2 * LANES), F32), z1, z1, z1, z1))

        @pl.when(pl.program_id(1) == 0)
        def _():
            dw_ref[...] = jnp.zeros_like(dw_ref)
            db_ref[...] = jnp.zeros_like(db_ref)

        dw_ref[0:1, :] += dw0
        dw_ref[1:2, :] += dw1
        dw_ref[2:3, :] += dw2
        db_ref[...] += dbs

    return pl.pallas_call(
        body, name=name, grid=(nct, nb),
        in_specs=[pl.BlockSpec((lp, 2 * LANES), lambda c, b: (b, c)),
                  pl.BlockSpec((lp, LANES), lambda c, b: (b, c)),
                  pl.BlockSpec((CONV_WIDTH, 2 * LANES), lambda c, b: (0, c)),
                  pl.BlockSpec((1, 2 * LANES), lambda c, b: (0, c))],
        out_specs=[pl.BlockSpec((lp, 2 * LANES), lambda c, b: (b, c)),
                   pl.BlockSpec((CONV_WIDTH, 2 * LANES), lambda c, b: (0, c)),
                   pl.BlockSpec((1, 2 * LANES), lambda c, b: (0, c))],
        out_shape=[jax.ShapeDtypeStruct((m, 2 * D_FF), BF16),
                   jax.ShapeDtypeStruct((CONV_WIDTH, 2 * D_FF), F32),
                   jax.ShapeDtypeStruct((1, 2 * D_FF), F32)],
        compiler_params=_cp("parallel", "arbitrary"),
    )(up, dact, cw, cb)


def _ffn_interleave(a, axis):
    shp = a.shape
    a = a.reshape(shp[:axis] + (2, D_FF // LANES, LANES) + shp[axis + 1:])
    return jnp.swapaxes(a, axis, axis + 1).reshape(shp)


def _ffn_deinterleave(a, axis):
    shp = a.shape
    a = a.reshape(shp[:axis] + (D_FF // LANES, 2, LANES) + shp[axis + 1:])
    return jnp.swapaxes(a, axis, axis + 1).reshape(shp)


def _shifted_rows(prev_ref, cur_ref):
    keep = SEQ_BLOCK - N_META
    return jnp.concatenate([prev_ref[keep:, :], cur_ref[:keep, :]], axis=0)


def _frame_specs(nblk, nfb, d):
    prev = pl.BlockSpec((SEQ_BLOCK, d), lambda b, i: (b * nfb + jnp.clip(i - 1, 0, nfb - 1), 0))
    cur = pl.BlockSpec((SEQ_BLOCK, d), lambda b, i: (b * nfb + jnp.clip(i, 0, nfb - 1), 0))
    return prev, cur


def _embed_rms(x2, meta, gain, nb, lp, l, name):
    d = x2.shape[1]
    tr = SEQ_BLOCK
    nblk = lp // tr
    nfb = (l - N_META) // tr
    m = nb * lp

    def body(prev_ref, cur_ref, meta_ref, g_ref, h_ref, o_ref):
        i = pl.program_id(1)
        t = i * tr + lax.broadcasted_iota(jnp.int32, (tr, 1), 0)
        rows = jnp.where(t < l, _shifted_rows(prev_ref, cur_ref), 0.0)
        head = jnp.concatenate([meta_ref[...], jnp.zeros((tr - N_META, d), F32)], axis=0)
        xv = jnp.where(t < N_META, head, rows)
        h_ref[...] = xv
        r = lax.rsqrt(jnp.mean(xv * xv, axis=-1, keepdims=True) + EPS)
        o_ref[...] = ((xv * r) * g_ref[...]).astype(o_ref.dtype)

    prev, cur = _frame_specs(nblk, nfb, d)
    row = pl.BlockSpec((tr, d), lambda b, i: (b * nblk + i, 0))
    return pl.pallas_call(
        body, name=name, grid=(nb, nblk),
        in_specs=[prev, cur, pl.BlockSpec((N_META, d), lambda b, i: (0, 0)), pl.BlockSpec((1, d), lambda b, i: (0, 0))],
        out_specs=[row, row],
        out_shape=[jax.ShapeDtypeStruct((m, d), F32), jax.ShapeDtypeStruct((m, d), BF16)],
        compiler_params=_cp("parallel", "parallel"),
    )(x2, x2, meta, gain)


def _loss_head(out, tgt2, nb, lp, l, name):
    m, d = out.shape
    tr = SEQ_BLOCK
    nblk = lp // tr
    nfb = (l - N_META) // tr

    def body(o_ref, prev_ref, cur_ref, dy_ref, ls_ref):
        t = pl.program_id(1) * tr + lax.broadcasted_iota(jnp.int32, (tr, 1), 0)
        valid = (t >= N_META) & (t < l)
        err = jnp.where(valid, o_ref[...] - _shifted_rows(prev_ref, cur_ref), 0.0)
        dy_ref[...] = err * (1.0 / d)
        part = jnp.sum(err * err, axis=0, keepdims=True)
        first = (pl.program_id(0) == 0) & (pl.program_id(1) == 0)

        @pl.when(first)
        def _():
            ls_ref[...] = part

        @pl.when(jnp.logical_not(first))
        def _():
            ls_ref[...] += part

    prev, cur = _frame_specs(nblk, nfb, d)
    row = pl.BlockSpec((tr, d), lambda b, i: (b * nblk + i, 0))
    return pl.pallas_call(
        body, name=name, grid=(nb, nblk),
        in_specs=[row, prev, cur], out_specs=[row, pl.BlockSpec((1, d), lambda b, i: (0, 0))],
        out_shape=[jax.ShapeDtypeStruct((m, d), F32), jax.ShapeDtypeStruct((1, d), F32)],
        compiler_params=_cp("arbitrary", "arbitrary"),
    )(out, tgt2, tgt2)


def _adam_math(g, w, mom, var):
    c1 = 1.0 - ADAM_B1 ** ADAM_STEP
    c2 = 1.0 - ADAM_B2 ** ADAM_STEP
    mn = ADAM_B1 * mom + (1.0 - ADAM_B1) * g
    vn = ADAM_B2 * var + (1.0 - ADAM_B2) * (g * g)
    delta = -ADAM_LR * ((mn / c1) / (jnp.sqrt(vn / c2) + ADAM_EPS) + ADAM_WD * w)
    return delta, mn, vn


def _slot_sum(recv, name):
    _, r, c = recv.shape
    tc = _div_tile(c, 256, LANES)

    def body(r_ref, g_ref):
        g = r_ref[0].astype(F32)
        for s in range(1, N_DEV):
            g = g + r_ref[s].astype(F32)
        g_ref[...] = g

    return pl.pallas_call(
        body, name=name, grid=(c // tc,),
        in_specs=[pl.BlockSpec((N_DEV, r, tc), lambda j: (0, 0, j))],
        out_specs=pl.BlockSpec((r, tc), lambda j: (0, j)),
        out_shape=jax.ShapeDtypeStruct((r, c), F32),
        compiler_params=_cp("parallel"),
    )(recv)


def _adamw(g, w, mom, var, name):
    r, c = w.shape
    tr = _div_tile(r, 256, 8)

    def body(g_ref, w_ref, m_ref, v_ref, d_ref, mo_ref, vo_ref):
        d_ref[...], mo_ref[...], vo_ref[...] = _adam_math(g_ref[...], w_ref[...], m_ref[...], v_ref[...])

    row = pl.BlockSpec((tr, c), lambda i: (i, 0))
    return pl.pallas_call(
        body, name=name, grid=(r // tr,), in_specs=[row] * 4, out_specs=[row] * 3,
        out_shape=[jax.ShapeDtypeStruct((r, c), F32)] * 3,
        compiler_params=_cp("parallel"),
    )(g, w, mom, var)


def _sum_adamw(recv, w, mom, var, name):
    r, c = w.shape
    tr = _div_tile(r, 256, 8)

    def body(r_ref, w_ref, m_ref, v_ref, g_ref, d_ref, mo_ref, vo_ref):
        g = r_ref[0].astype(F32)
        for s in range(1, N_DEV):
            g = g + r_ref[s].astype(F32)
        g_ref[...] = g
        d_ref[...], mo_ref[...], vo_ref[...] = _adam_math(g, w_ref[...], m_ref[...], v_ref[...])

    row = pl.BlockSpec((tr, c), lambda i: (i, 0))
    return pl.pallas_call(
        body, name=name, grid=(r // tr,),
        in_specs=[pl.BlockSpec((N_DEV, tr, c), lambda i: (0, i, 0)), row, row, row],
        out_specs=[row] * 4,
        out_shape=[jax.ShapeDtypeStruct((r, c), F32)] * 4,
        compiler_params=_cp("parallel"),
    )(recv, w, mom, var)


_MESH = pl.DeviceIdType.MESH
_HBM = pl.BlockSpec(memory_space=pltpu.HBM)
N_PEER = N_DEV - 1


def _position():
    return lax.axis_index("x"), lax.axis_index("y"), lax.axis_index("c")


def _all_gather(shards, name):
    n = len(shards)

    def body(*refs):
        x_refs, out_refs = refs[:n], refs[n:2 * n]
        send_sems, recv_sems, local_sems = refs[2 * n:]
        x, y, c = _position()
        me, sibling = (x, y, c), (x, y, 1 - c)
        chips = [(1 - x, y), (x, 1 - y), (1 - x, 1 - y)]

        def copy(a, k, block, to, src=None):
            slot = out_refs[a].at[4 * block[0] + 2 * block[1] + block[2]]
            return pltpu.make_async_remote_copy(
                src_ref=slot if src is None else src, dst_ref=slot,
                send_sem=send_sems.at[a * N_PEER + k], recv_sem=recv_sems.at[a * N_PEER + k],
                device_id=to, device_id_type=_MESH)

        mine, sent = [], []
        for a in range(n):
            cp = pltpu.make_async_copy(x_refs[a], out_refs[a].at[4 * x + 2 * y + c], local_sems.at[a])
            cp.start()
            mine.append(cp)
            first = [copy(a, 0, me, sibling, src=x_refs[a])]
            first += [copy(a, 1 + j, me, (*chip, c), src=x_refs[a]) for j, chip in enumerate(chips)]
            for cp in first:
                cp.start()
            sent += first
        for a in range(n):
            for j, chip in enumerate(chips):
                copy(a, 1 + j, (*chip, c), me).wait_recv()
                fwd = copy(a, 4 + j, (*chip, c), sibling)
                fwd.start()
                sent.append(fwd)
        for a in range(n):
            copy(a, 0, sibling, me).wait_recv()
            for j, chip in enumerate(chips):
                copy(a, 4 + j, (*chip, 1 - c), me).wait_recv()
        for cp in sent:
            cp.wait_send()
        for cp in mine:
            cp.wait()

    return pl.pallas_call(
        body, name=name,
        out_shape=[jax.ShapeDtypeStruct((N_DEV,) + a.shape, a.dtype) for a in shards],
        in_specs=[_HBM] * n, out_specs=[_HBM] * n,
        scratch_shapes=[pltpu.SemaphoreType.DMA((n * N_PEER,)), pltpu.SemaphoreType.DMA((n * N_PEER,)),
                        pltpu.SemaphoreType.DMA((n,))],
    )(*shards)


_FLIPS = [(fx, fy, fc) for fx in (0, 1) for fy in (0, 1) for fc in (0, 1)][1:]


def _exchange_copies(in_refs, out_refs, nblk, send_sems, recv_sems, local_sems):
    n = len(in_refs)
    x, y, c = _position()
    me = 4 * x + 2 * y + c

    def peer(f):
        return (1 - x if f[0] else x, 1 - y if f[1] else y, 1 - c if f[2] else c)

    def idx(p):
        return 4 * p[0] + 2 * p[1] + p[2]

    def local(a):
        return pltpu.make_async_copy(in_refs[a].at[me] if a < nblk else in_refs[a], out_refs[a].at[me], local_sems.at[a])

    def remote(a, k, sending):
        p = peer(_FLIPS[k])
        src = in_refs[a].at[idx(p)] if a < nblk else in_refs[a]
        dst = out_refs[a].at[me] if sending else out_refs[a].at[idx(p)]
        return pltpu.make_async_remote_copy(
            src_ref=src, dst_ref=dst, send_sem=send_sems.at[a * N_PEER + k], recv_sem=recv_sems.at[a * N_PEER + k],
            device_id=p, device_id_type=_MESH)

    def start():
        for a in range(n):
            local(a).start()
            for k in range(N_PEER):
                remote(a, k, True).start()

    def wait():
        for a in range(n):
            for k in range(N_PEER):
                remote(a, k, False).wait_recv()
        for a in range(n):
            for k in range(N_PEER):
                remote(a, k, True).wait_send()
            local(a).wait()

    return start, wait


def _exchange_io(blocks, shared):
    arrays = list(blocks) + list(shared)
    n = len(arrays)
    out_shape = [jax.ShapeDtypeStruct(a.shape, a.dtype) for a in blocks]
    out_shape += [jax.ShapeDtypeStruct((N_DEV,) + a.shape, a.dtype) for a in shared]
    sems = [pltpu.SemaphoreType.DMA((n * N_PEER,)), pltpu.SemaphoreType.DMA((n * N_PEER,)), pltpu.SemaphoreType.DMA((n,))]
    return arrays, out_shape, sems


def _exchange(blocks, shared, name):
    arrays, out_shape, sems = _exchange_io(blocks, shared)
    n = len(arrays)

    def body(*refs):
        start, wait = _exchange_copies(refs[:n], refs[n:2 * n], len(blocks), *refs[2 * n:])
        start()
        wait()

    return pl.pallas_call(
        body, name=name, out_shape=out_shape, in_specs=[_HBM] * n, out_specs=[_HBM] * n, scratch_shapes=sems,
    )(*arrays)


def _grid_ends(grid):
    ids = [pl.program_id(i) for i in range(len(grid))]
    first = functools.reduce(jnp.logical_and, [i == 0 for i in ids])
    last = functools.reduce(jnp.logical_and, [i == g - 1 for i, g in zip(ids, grid)])
    return first, last


def _pack(parts, rows):
    flat = jnp.concatenate(parts, axis=-1)
    return jnp.pad(flat, [(0, rows * LANES - flat.shape[-1])]).reshape(rows, LANES)


def _unpack(packed, shapes):
    flat = packed.reshape(-1)
    out, off = [], 0
    for shp in shapes:
        n = int(np.prod(shp))
        out.append(flat[off:off + n].reshape(shp))
        off += n
    return out


def _rows_for(shapes, extra=0):
    n = sum(int(np.prod(s)) for s in shapes) + extra
    return -(-n // (8 * LANES)) * 8


def _lower_bound(logits):
    return jnp.cumsum(jax.nn.softmax(logits.astype(F32), axis=0), axis=0)[0:1]


def _align_axis0(w):
    a, b = 3 * FOX_WIDTH, 3 * FOX_WIDTH + FOX_HEADS
    c = b + 4 * HG_WIDTH
    pad = [(0, LANES - FOX_HEADS)] + [(0, 0)] * (w.ndim - 1)
    return jnp.concatenate([w[c:], w[:a], w[b:c], jnp.pad(w[a:b], pad)], axis=0)


def _unalign_axis0(g):
    a, b = 2 * D_MODEL, 2 * D_MODEL + 3 * FOX_WIDTH
    c = b + 4 * HG_WIDTH
    return jnp.concatenate([g[a:b], g[c:c + FOX_HEADS], g[b:c], g[:a]], axis=0)


TINY_COLS = 768


def _tiny_pack(conv_w_shard, meta_shard):
    cw = jnp.pad(conv_w_shard, ((0, 8 - CONV_WIDTH), (0, TINY_COLS - conv_w_shard.shape[1])))
    mt = jnp.pad(meta_shard, ((0, 0), (0, TINY_COLS - meta_shard.shape[1])))
    return jnp.concatenate([cw, mt], axis=0)


def _tiny_unpack(t, ncw, nmeta):
    return t[..., :CONV_WIDTH, :ncw], t[..., 8:8 + N_META, :nmeta]


def _ffn_weights(g_up, g_down):
    d = g_up.shape[-1]
    return _ffn_interleave(g_up.reshape(-1, d), 0), g_down.reshape(-1, d)


def _early_blocks(g_w_up_t, g_w_down, g_w_out, g_w_a_t, g_w_b_t):
    d = g_w_out.shape[-1]
    ab = jnp.stack([g_w_a_t.reshape(N_DEV, -1, g_w_a_t.shape[-1]), g_w_b_t.reshape(N_DEV, -1, g_w_b_t.shape[-1])], axis=1)
    return [_ffn_deinterleave(g_w_up_t, 0).reshape(N_DEV, -1, d).astype(BF16), g_w_down.reshape(N_DEV, -1, d).astype(BF16),
            g_w_out.reshape(N_DEV, -1, d).astype(BF16), ab.astype(BF16)]


def _local_step(x, target, meta, norm1_gain, w_in_t, fox_b_f, q_gain, k_gain, lb, hg_out_gain, w_a_t, w_b_t, w_out,
                norm2_gain, w_up_t, conv_w, conv_b, w_down, ffn_shards=None):
    nb, seq, d = x.shape
    assert seq % SEQ_BLOCK == 0 and N_META < SEQ_BLOCK
    l = seq + N_META
    lp = -(-l // SEQ_BLOCK) * SEQ_BLOCK
    m = nb * lp
    qg = jnp.tile(q_gain, (1, FOX_HEADS))
    kg = jnp.tile(k_gain, (1, FOX_HEADS))
    bf = jnp.pad(fox_b_f, ((0, 0), (0, LANES - FOX_HEADS)))

    h0, xn = _embed_rms(x.reshape(nb * seq, d), meta, norm1_gain, nb, lp, l, "embed_rms1")
    proj = _matmul(xn, w_in_t, "nt", F32, "proj_in")
    qa, ka, vb = _fox_prep(proj, qg, kg, bf, nb, lp, "fox_prep")
    if ffn_shards is None:
        o_fox, lse = _fox_fwd(qa, ka, vb, nb, lp, "fox_fwd")
    else:
        o_fox, lse, g_up, g_down = _fox_fwd(qa, ka, vb, nb, lp, "fox_fwd", ride=ffn_shards)
        w_up_t, w_down = _ffn_weights(g_up, g_down)
    o_raw, o_hg, s_save = _hgrn_fwd(proj, lb, hg_out_gain, nb, lp, "hgrn_fwd")
    ya = _matmul(o_hg, w_a_t, "nt", F32, "branch_a")
    yb = _matmul(o_fox, w_b_t, "nt", F32, "branch_b")
    merged = _gate_fwd(proj, ya, yb, "gate_fwd")
    h1 = _matmul(merged, w_out, "nn", F32, "mix_out", residual=h0)
    hn = _rms_fwd(h1, norm2_gain, "rms2_fwd")
    up = _matmul(hn, w_up_t, "nt", F32, "ffn_up")
    act = _conv_fwd(up, conv_w, conv_b, nb, lp, "conv_fwd")
    out = _matmul(act, w_down, "nn", F32, "ffn_down", residual=h1)
    dy, lsum = _loss_head(out, target.reshape(nb * seq, d), nb, lp, l, "loss_head")
    loss = (0.5 / d) * jnp.sum(lsum)

    dact = _matmul(dy, w_down, "nt", F32, "d_act")
    g_w_down = _matmul(act, dy, "tn", F32, "g_w_down")
    dup, g_conv_w, g_conv_b = _conv_bwd(up, dact, conv_w, conv_b, nb, lp, "conv_bwd")
    dhn = _matmul(dup, w_up_t, "nn", F32, "d_hn")
    g_w_up_t = _matmul(dup, hn, "tn", F32, "g_w_up")
    dh1, g_norm2 = _rms_bwd(h1, norm2_gain, dhn, dy, "rms2_bwd")

    dmerged = _matmul(dh1, w_out, "nt", F32, "d_merged")
    g_w_out = _matmul(merged, dh1, "tn", F32, "g_w_out")
    dya, dyb, dgab = _gate_bwd(proj, ya, yb, dmerged, "gate_bwd")
    do_hg = _matmul(dya, w_a_t, "nn", F32, "d_o_hg")
    g_w_a_t = _matmul(dya, o_hg, "tn", F32, "g_w_a")
    do_fox = _matmul(dyb, w_b_t, "nn", F32, "d_o_fox")
    g_w_b_t = _matmul(dyb, o_fox, "tn", F32, "g_w_b")
    dhq, dhf, dhi, dhg, g_hg_gain, g_lb = _hgrn_bwd(proj, o_raw, s_save, do_hg, lb, hg_out_gain, nb, lp, "hgrn_bwd")
    if ffn_shards is None:
        dqs, dkn, dvv, dc0, dc1 = _fox_bwd(qa, ka, vb, do_fox, o_fox, lse, nb, lp, "fox_bwd")
        early = None
    else:
        dqs, dkn, dvv, dc0, dc1, *early = _fox_bwd(qa, ka, vb, do_fox, o_fox, lse, nb, lp, "fox_bwd",
                                                   ride=_early_blocks(g_w_up_t, g_w_down, g_w_out, g_w_a_t, g_w_b_t))
    dcum = jnp.stack([dc0, dc1], axis=2).reshape(nb, FOX_HEADS, lp)
    dcum = jnp.pad(jnp.transpose(dcum, (0, 2, 1)), ((0, 0), (0, 0), (0, LANES - FOX_HEADS))).reshape(m, LANES)
    dfqkv, dff, g_qg, g_kg, g_bf = _fox_prep_bwd(proj, dqs, dkn, dvv, dcum, qg, kg, bf, nb, lp, "fox_prep_bwd")
    dproj = jnp.concatenate([dgab, dfqkv, dhq, dhf, dhi, dhg, dff], axis=1)
    dxn = _matmul(dproj, w_in_t, "nn", F32, "d_xn")
    g_w_in_t = _matmul(dproj, xn, "tn", F32, "g_w_in")
    dh0, g_norm1 = _rms_bwd(h0, norm1_gain, dxn, dh1, "rms1_bwd")

    dh0 = dh0.reshape(nb, lp, d)
    grad_x = dh0[:, N_META:l]
    g_meta = jnp.sum(dh0[:, :N_META], axis=0)
    g_q_gain = jnp.sum(g_qg.reshape(FOX_HEADS, FOX_HEAD_DIM), axis=0, keepdims=True)
    g_k_gain = jnp.sum(g_kg.reshape(FOX_HEADS, FOX_HEAD_DIM), axis=0, keepdims=True)
    grads = dict(meta_tokens=g_meta, norm1_gain=g_norm1, w_in_t=g_w_in_t, fox_b_f=g_bf[:, :FOX_HEADS],
                 q_norm_gain=g_q_gain, k_norm_gain=g_k_gain, lb=g_lb, hg_out_gain=g_hg_gain,
                 w_a_t=g_w_a_t, w_b_t=g_w_b_t, w_out=g_w_out, norm2_gain=g_norm2, w_up_t=g_w_up_t,
                 conv_w=g_conv_w, conv_b=g_conv_b, w_down=g_w_down, early=early)
    return loss, grad_x, grads


SMALL = ("norm1_gain", "fox_b_f", "q_norm_gain", "k_norm_gain", "hg_lb_logits", "hg_out_gain", "norm2_gain", "conv_b")
ORDER = ("meta_tokens", "norm1_gain", "w_in", "fox_b_f", "q_norm_gain", "k_norm_gain", "hg_lb_logits", "hg_out_gain",
         "w_branch_a", "w_branch_b", "w_out", "norm2_gain", "w_up", "conv_w", "conv_b", "w_down")


def kernel(x, meta_tokens, norm1_gain, w_in, fox_b_f, q_norm_gain, k_norm_gain, hg_lb_logits, hg_out_gain, w_branch_a, w_branch_b, w_out, norm2_gain, w_up, conv_w, conv_b, w_down, loss_target, m_meta_tokens, m_norm1_gain, m_w_in, m_fox_b_f, m_q_norm_gain, m_k_norm_gain, m_hg_lb_logits, m_hg_out_gain, m_w_branch_a, m_w_branch_b, m_w_out, m_norm2_gain, m_w_up, m_conv_w, m_conv_b, m_w_down, v_meta_tokens, v_norm1_gain, v_w_in, v_fox_b_f, v_q_norm_gain, v_k_norm_gain, v_hg_lb_logits, v_hg_out_gain, v_w_branch_a, v_w_branch_b, v_w_out, v_norm2_gain, v_w_up, v_conv_w, v_conv_b, v_w_down):
    w = dict(meta_tokens=meta_tokens, norm1_gain=norm1_gain, w_in=w_in, fox_b_f=fox_b_f, q_norm_gain=q_norm_gain,
             k_norm_gain=k_norm_gain, hg_lb_logits=hg_lb_logits, hg_out_gain=hg_out_gain, w_branch_a=w_branch_a,
             w_branch_b=w_branch_b, w_out=w_out, norm2_gain=norm2_gain, w_up=w_up, conv_w=conv_w, conv_b=conv_b,
             w_down=w_down)
    mom = dict(meta_tokens=m_meta_tokens, norm1_gain=m_norm1_gain, w_in=m_w_in, fox_b_f=m_fox_b_f,
               q_norm_gain=m_q_norm_gain, k_norm_gain=m_k_norm_gain, hg_lb_logits=m_hg_lb_logits,
               hg_out_gain=m_hg_out_gain, w_branch_a=m_w_branch_a, w_branch_b=m_w_branch_b, w_out=m_w_out,
               norm2_gain=m_norm2_gain, w_up=m_w_up, conv_w=m_conv_w, conv_b=m_conv_b, w_down=m_w_down)
    var = dict(meta_tokens=v_meta_tokens, norm1_gain=v_norm1_gain, w_in=v_w_in, fox_b_f=v_fox_b_f,
               q_norm_gain=v_q_norm_gain, k_norm_gain=v_k_norm_gain, hg_lb_logits=v_hg_lb_logits,
               hg_out_gain=v_hg_out_gain, w_branch_a=v_w_branch_a, w_branch_b=v_w_branch_b, w_out=v_w_out,
               norm2_gain=v_norm2_gain, w_up=v_w_up, conv_w=v_conv_w, conv_b=v_conv_b, w_down=v_w_down)
    d = D_MODEL
    n_in, n_up = w_in.shape[2], w_up.shape[2]
    n_ab, n_meta = w_branch_a.shape[2], meta_tokens.shape[1]

    shards = [w_in[0].T.astype(BF16),
              jnp.stack([w_branch_a[0].T, w_branch_b[0].T]).astype(BF16),
              w_out[0].astype(BF16),
              _tiny_pack(conv_w[0], meta_tokens)]
    g_in, g_ab, g_out, g_tiny = _all_gather(shards, "gather_weights")
    w_in_t = _align_axis0(g_in.reshape(N_DEV * n_in, d))
    w_a_t = g_ab[:, 0].reshape(N_DEV * n_ab, -1)
    w_b_t = g_ab[:, 1].reshape(N_DEV * n_ab, -1)
    cw_slots, meta_slots = _tiny_unpack(g_tiny, n_up, n_meta)
    conv_w_f = _ffn_interleave(jnp.transpose(cw_slots, (1, 0, 2)).reshape(CONV_WIDTH, -1), 1)
    meta_f = jnp.transpose(meta_slots, (1, 0, 2)).reshape(N_META, -1)
    conv_b_i = _ffn_interleave(conv_b, 1)

    lb, lb_vjp = jax.vjp(_lower_bound, hg_lb_logits)
    loss, grad_x, g = _local_step(
        x, loss_target, meta_f, norm1_gain, w_in_t, fox_b_f, q_norm_gain, k_norm_gain, lb, hg_out_gain,
        w_a_t, w_b_t, g_out.reshape(d, d), norm2_gain, None, conv_w_f, conv_b_i, None,
        ffn_shards=(w_up[0].T.astype(BF16), w_down[0].astype(BF16)))

    g["hg_lb_logits"] = lb_vjp(g.pop("lb"))[0]
    g["conv_b"] = _ffn_deinterleave(g["conv_b"], 1)
    gcw = _ffn_deinterleave(g["conv_w"], 1).reshape(CONV_WIDTH, N_DEV, n_up)
    gmeta = g["meta_tokens"].reshape(N_META, N_DEV, n_meta)
    tiny = jnp.concatenate([
        jnp.pad(jnp.transpose(gcw, (1, 0, 2)), ((0, 0), (0, 8 - CONV_WIDTH), (0, TINY_COLS - n_up))),
        jnp.pad(jnp.transpose(gmeta, (1, 0, 2)), ((0, 0), (0, 0), (0, TINY_COLS - n_meta)))], axis=1)
    small_shapes = [w[n].shape for n in SMALL]
    rows_sm = _rows_for(small_shapes, extra=1)
    small = _pack([g[n].reshape(-1) for n in SMALL] + [loss.reshape(1)], rows_sm)
    blocks = [_unalign_axis0(g["w_in_t"]).reshape(N_DEV, n_in, d).astype(BF16), tiny]
    r_in, r_tiny, r_small = _exchange(blocks, [small], "exchange_grads")
    r_up, r_down, r_out, r_ab = g["early"]

    res = {}
    g_in_s = _slot_sum(r_in, "sum_w_in").T
    res["w_in"] = (g_in_s,) + tuple(_adamw(g_in_s, w_in[0], m_w_in[0], v_w_in[0], "adamw_w_in"))
    g_up_s = _slot_sum(r_up, "sum_w_up").T
    res["w_up"] = (g_up_s,) + tuple(_adamw(g_up_s, w_up[0], m_w_up[0], v_w_up[0], "adamw_w_up"))
    g_ab_s = jnp.swapaxes(_slot_sum(r_ab.reshape(N_DEV, 2 * n_ab, -1), "sum_w_ab").reshape(2, n_ab, -1), 1, 2)
    ab = lambda t: jnp.concatenate([t["w_branch_a"][0], t["w_branch_b"][0]], axis=0)
    o_ab = (g_ab_s.reshape(-1, n_ab),) + tuple(_adamw(g_ab_s.reshape(-1, n_ab), ab(w), ab(mom), ab(var), "adamw_w_ab"))
    half = o_ab[0].shape[0] // 2
    res["w_branch_a"] = tuple(o[:half] for o in o_ab)
    res["w_branch_b"] = tuple(o[half:] for o in o_ab)
    res["w_out"] = tuple(_sum_adamw(r_out, w_out[0], m_w_out[0], v_w_out[0], "adamw_w_out"))
    res["w_down"] = tuple(_sum_adamw(r_down, w_down[0], m_w_down[0], v_w_down[0], "adamw_w_down"))
    tp = lambda t: _tiny_pack(t["conv_w"][0], t["meta_tokens"])
    o_tiny = [_tiny_unpack(o, n_up, n_meta) for o in _sum_adamw(r_tiny, tp(w), tp(mom), tp(var), "adamw_tiny")]
    res["conv_w"] = tuple(o[0] for o in o_tiny)
    res["meta_tokens"] = tuple(o[1] for o in o_tiny)
    zero1 = jnp.zeros((1,), F32)
    sp = lambda t: _pack([t[n].reshape(-1) for n in SMALL] + [zero1], rows_sm)
    o_small = [_unpack(o, small_shapes + [(1,)]) for o in _sum_adamw(r_small, sp(w), sp(mom), sp(var), "adamw_small")]
    for i, n in enumerate(SMALL):
        res[n] = tuple(o[i] for o in o_small)
    loss_all = o_small[0][len(SMALL)].reshape(())

    result = [[res[n][k].reshape(w[n].shape) for n in ORDER] for k in range(4)]
    return (loss_all, grad_x, *result[0], *result[1], *result[2], *result[3])
```

```python
import functools

import jax
import jax.numpy as jnp
import numpy as np
from jax import lax
from jax.experimental import pallas as pl
from jax.experimental.pallas import tpu as pltpu

F32 = jnp.float32
BF16 = jnp.bfloat16

D_MODEL = 1024
N_META = 16
FOX_HEADS = 8
FOX_HEAD_DIM = 64
FOX_WIDTH = FOX_HEADS * FOX_HEAD_DIM
HG_HEADS = 4
HG_DIM = 128
HG_WIDTH = HG_HEADS * HG_DIM
D_FF = 2816
CONV_WIDTH = 3
EPS = 1e-6
IN_COLS = 3 * FOX_WIDTH + FOX_HEADS + 4 * HG_WIDTH + 2 * D_MODEL
N_DEV = 8

ADAM_LR = 0.001
ADAM_B1 = 0.9
ADAM_B2 = 0.999
ADAM_EPS = 1e-08
ADAM_WD = 0.01
ADAM_STEP = 10

LANES = 128
SEQ_BLOCK = 128
SUB = 16
NEG = -1e30
VMEM_LIMIT = 48 * 1024 * 1024

FOX_CB = 2 * D_MODEL // FOX_WIDTH
CB_HQ = (2 * D_MODEL + 3 * FOX_WIDTH) // LANES
CB_HF = CB_HQ + HG_HEADS
CB_HI = CB_HF + HG_HEADS
CB_HG = CB_HI + HG_HEADS
CB_FF = CB_HG + HG_HEADS


def _div_tile(n, target, mult):
    best = None
    for t in range(mult, min(n, target) + 1, mult):
        if n % t == 0:
            best = t
    if best is None:
        best = n
    return best


def _cp(*sem):
    return pltpu.CompilerParams(dimension_semantics=sem, vmem_limit_bytes=VMEM_LIMIT)


def _sigmoid(x):
    return 0.5 * jnp.tanh(0.5 * x) + 0.5


def _dot(a, b, dims, precision=None):
    return lax.dot_general(a, b, (dims, ((), ())), preferred_element_type=F32, precision=precision)


NN = ((1,), (0,))
NT = ((1,), (1,))
TN = ((0,), (0,))
HI = lax.Precision.HIGHEST


MATMUL_VMEM_BUDGET = 30 * 1024 * 1024
MATMUL_MAX_TILE = 2048


def _tile_options(n):
    return [t for t in range(LANES, min(n, MATMUL_MAX_TILE) + 1, LANES) if n % t == 0] or [n]


def _matmul_tiles(m, n, k, a_bytes, b_bytes, o_bytes, has_res):
    tk = _div_tile(k, MATMUL_MAX_TILE, LANES)
    best = None
    for tm in _tile_options(m):
        for tn in _tile_options(n):
            vmem = 2 * (tm * tk * a_bytes + tk * tn * b_bytes) + 2 * tm * tn * o_bytes
            vmem += tm * tn * 4 if tk < k else 0
            vmem += 2 * tm * tn * 4 if has_res else 0
            if vmem > MATMUL_VMEM_BUDGET:
                continue
            key = (tm * tn, tn % 256 == 0, tn)
            if best is None or key > best[0]:
                best = (key, tm, tn)
    assert best is not None, (m, n, k)
    return best[1], best[2], tk


def _matmul(a, b, mode, out_dtype, name, residual=None, ride=()):
    if mode == "nn":
        (m, k), (k2, n) = a.shape, b.shape
    elif mode == "nt":
        (m, k), (n, k2) = a.shape, b.shape
    else:
        (k, m), (k2, n) = a.shape, b.shape
    assert k == k2, (a.shape, b.shape, mode)
    has_res = residual is not None
    tm, tn, tk = _matmul_tiles(m, n, k, a.dtype.itemsize, b.dtype.itemsize, jnp.dtype(out_dtype).itemsize, has_res)
    nk = k // tk
    if mode == "nn":
        a_spec = pl.BlockSpec((tm, tk), lambda i, j, kk: (i, kk))
        b_spec = pl.BlockSpec((tk, tn), lambda i, j, kk: (kk, j))
        dims = NN
    elif mode == "nt":
        a_spec = pl.BlockSpec((tm, tk), lambda i, j, kk: (i, kk))
        b_spec = pl.BlockSpec((tn, tk), lambda i, j, kk: (j, kk))
        dims = NT
    else:
        a_spec = pl.BlockSpec((tk, tm), lambda i, j, kk: (kk, i))
        b_spec = pl.BlockSpec((tk, tn), lambda i, j, kk: (kk, j))
        dims = TN
    o_spec = pl.BlockSpec((tm, tn), lambda i, j, kk: (i, j))
    grid = (m // tm, n // tn, nk)
    x_arrays, x_shapes, x_sems = _exchange_io(ride, ())
    nx = len(x_arrays)
    n_in = 3 if has_res else 2

    def body(*refs):
        if nx:
            first, last = _grid_ends(grid)
            x_in, x_out = refs[n_in:n_in + nx], refs[n_in + nx + 1:n_in + 2 * nx + 1]
            start, wait = _exchange_copies(x_in, x_out, nx, *refs[n_in + 2 * nx + 1:n_in + 2 * nx + 4])
            pl.when(first)(start)
        compute(*refs)
        if nx:
            pl.when(last)(wait)

    def compute(*refs):
        a_ref, b_ref = refs[0], refs[1]
        r_ref = refs[2] if has_res else None
        o_ref = refs[n_in + nx]
        part = _dot(a_ref[...].astype(BF16), b_ref[...].astype(BF16), dims)
        if nk == 1:
            o_ref[...] = (part + r_ref[...] if has_res else part).astype(o_ref.dtype)
            return
        acc_ref = refs[-1]
        kk = pl.program_id(2)

        @pl.when(kk == 0)
        def _():
            acc_ref[...] = part

        @pl.when(kk > 0)
        def _():
            acc_ref[...] += part

        @pl.when(kk == nk - 1)
        def _():
            acc = acc_ref[...]
            if has_res:
                acc = acc + r_ref[...]
            o_ref[...] = acc.astype(o_ref.dtype)

    in_specs = [a_spec, b_spec] + ([o_spec] if has_res else [])
    args = (a, b) + ((residual,) if has_res else ())
    out_shape = jax.ShapeDtypeStruct((m, n), out_dtype)
    acc = [pltpu.VMEM((tm, tn), F32)] if nk > 1 else []
    if not nx:
        return pl.pallas_call(
            body, name=name, grid=grid, in_specs=in_specs, out_specs=o_spec, out_shape=out_shape, scratch_shapes=acc,
            compiler_params=_cp("parallel", "parallel", "arbitrary"),
        )(*args)
    return pl.pallas_call(
        body, name=name, grid=grid, in_specs=in_specs + [_HBM] * nx, out_specs=[o_spec] + [_HBM] * nx,
        out_shape=[out_shape] + x_shapes, scratch_shapes=x_sems + acc,
        compiler_params=_cp("arbitrary", "arbitrary", "arbitrary"),
    )(*args, *x_arrays)


def _rms_fwd(x, gain, name):
    m, d = x.shape
    tm = _div_tile(m, 512, 16)

    def body(x_ref, g_ref, o_ref):
        xv = x_ref[...]
        r = lax.rsqrt(jnp.mean(xv * xv, axis=-1, keepdims=True) + EPS)
        o_ref[...] = ((xv * r) * g_ref[...]).astype(o_ref.dtype)

    return pl.pallas_call(
        body, name=name, grid=(m // tm,),
        in_specs=[pl.BlockSpec((tm, d), lambda i: (i, 0)), pl.BlockSpec((1, d), lambda i: (0, 0))],
        out_specs=pl.BlockSpec((tm, d), lambda i: (i, 0)),
        out_shape=jax.ShapeDtypeStruct((m, d), BF16),
        compiler_params=_cp("parallel"),
    )(x, gain)


def _rms_bwd(x, gain, dy, dres, name):
    m, d = x.shape
    tm = _div_tile(m, 256, 8)

    def body(x_ref, g_ref, dy_ref, dr_ref, dx_ref, dg_ref):
        xv = x_ref[...]
        r = lax.rsqrt(jnp.mean(xv * xv, axis=-1, keepdims=True) + EPS)
        nv = xv * r
        dyv = dy_ref[...]
        gdy = dyv * g_ref[...]
        dx_ref[...] = dr_ref[...] + r * (gdy - nv * jnp.mean(gdy * nv, axis=-1, keepdims=True))
        part = jnp.sum(dyv * nv, axis=0, keepdims=True)

        @pl.when(pl.program_id(0) == 0)
        def _():
            dg_ref[...] = part

        @pl.when(pl.program_id(0) > 0)
        def _():
            dg_ref[...] += part

    row = pl.BlockSpec((tm, d), lambda i: (i, 0))
    vec = pl.BlockSpec((1, d), lambda i: (0, 0))
    return pl.pallas_call(
        body, name=name, grid=(m // tm,),
        in_specs=[row, vec, row, row], out_specs=[row, vec],
        out_shape=[jax.ShapeDtypeStruct((m, d), F32), jax.ShapeDtypeStruct((1, d), F32)],
        compiler_params=_cp("arbitrary"),
    )(x, gain, dy, dres)


def _head_stats(xv, lo):
    sq = xv * xv
    s_lo = jnp.sum(jnp.where(lo, sq, 0.0), axis=1, keepdims=True)
    s_hi = jnp.sum(jnp.where(lo, 0.0, sq), axis=1, keepdims=True)
    return jnp.where(lo, s_lo, s_hi) * (1.0 / FOX_HEAD_DIM)


BIAS_LANE = FOX_HEAD_DIM
N_SPLIT = 3


def _split3(c):
    c1 = c.astype(BF16).astype(F32)
    r1 = c - c1
    c2 = r1.astype(BF16).astype(F32)
    c3 = (r1 - c2).astype(BF16).astype(F32)
    return c1, c2, c3


def _fox_prep(proj, qg, kg, bf, nb, lp, name):
    m = proj.shape[0]
    ts = SEQ_BLOCK
    nblk = lp // ts
    scale = FOX_HEAD_DIM ** -0.5

    def body(q_ref, k_ref, v_ref, f_ref, qg_ref, kg_ref, bf_ref, qo_ref, ko_ref, vo_ref, carry_ref):
        lane = lax.broadcasted_iota(jnp.int32, (1, LANES), 1)
        lo = lane < FOX_HEAD_DIM

        @pl.when(pl.program_id(1) == 0)
        def _():
            carry_ref[...] = jnp.zeros_like(carry_ref)

        z = f_ref[...] + bf_ref[...]
        logf = jnp.minimum(z, 0.0) - jnp.log(1.0 + jnp.exp(-jnp.abs(z)))
        logf = jnp.where(lane < FOX_HEADS, logf, 0.0)
        r = lax.broadcasted_iota(jnp.int32, (ts, ts), 0)
        c = lax.broadcasted_iota(jnp.int32, (ts, ts), 1)
        tri = jnp.where(c <= r, 1.0, 0.0).astype(F32)
        cum = _dot(tri, logf, NN, HI) + carry_ref[...]
        carry_ref[...] = cum[ts - 1:ts, :]

        ones = jnp.where((lane >= BIAS_LANE + N_SPLIT) & (lane < BIAS_LANE + 2 * N_SPLIT), 1.0, 0.0)
        ones_k = jnp.where((lane >= BIAS_LANE) & (lane < BIAS_LANE + N_SPLIT), 1.0, 0.0)
        for j in range(FOX_WIDTH // LANES):
            cs = slice(j * LANES, (j + 1) * LANES)
            xq = q_ref[:, cs]
            yq = ((xq * lax.rsqrt(_head_stats(xq, lo) + EPS)) * qg_ref[:, cs]) * scale
            xk = k_ref[:, cs]
            yk = (xk * lax.rsqrt(_head_stats(xk, lo) + EPS)) * kg_ref[:, cs]
            for hh in range(2):
                h = 2 * j + hh
                pieces = _split3(_lane_pick(cum, lane, h))
                qb, kb = ones, ones_k
                for i, piece in enumerate(pieces):
                    qb = jnp.where(lane == BIAS_LANE + i, piece, qb)
                    kb = jnp.where(lane == BIAS_LANE + N_SPLIT + i, -piece, kb)
                yq_h = yq if hh == 0 else pltpu.roll(yq, FOX_HEAD_DIM, 1)
                yk_h = yk if hh == 0 else pltpu.roll(yk, FOX_HEAD_DIM, 1)
                hs = slice(h * LANES, (h + 1) * LANES)
                qo_ref[:, hs] = jnp.where(lo, yq_h, qb).astype(BF16)
                ko_ref[:, hs] = jnp.where(lo, yk_h, kb).astype(BF16)
        vo_ref[...] = v_ref[...].astype(BF16)

    w = FOX_WIDTH
    row = lambda b, i: (b * nblk + i, 0)
    return pl.pallas_call(
        body, name=name, grid=(nb, nblk),
        in_specs=[pl.BlockSpec((ts, w), lambda b, i: (b * nblk + i, FOX_CB)),
                  pl.BlockSpec((ts, w), lambda b, i: (b * nblk + i, FOX_CB + 1)),
                  pl.BlockSpec((ts, w), lambda b, i: (b * nblk + i, FOX_CB + 2)),
                  pl.BlockSpec((ts, LANES), lambda b, i: (b * nblk + i, CB_FF)),
                  pl.BlockSpec((1, w), lambda b, i: (0, 0)),
                  pl.BlockSpec((1, w), lambda b, i: (0, 0)),
                  pl.BlockSpec((1, LANES), lambda b, i: (0, 0))],
        out_specs=[pl.BlockSpec((ts, 2 * w), row), pl.BlockSpec((ts, 2 * w), row), pl.BlockSpec((ts, w), row)],
        out_shape=[jax.ShapeDtypeStruct((m, 2 * w), BF16)] * 2 + [jax.ShapeDtypeStruct((m, w), BF16)],
        scratch_shapes=[pltpu.VMEM((1, LANES), F32)],
        compiler_params=_cp("arbitrary", "arbitrary"),
    )(proj, proj, proj, proj, qg, kg, bf)


def _att_tile(lp):
    return 384 if (lp % 384 == 0 and lp > 384) else 128


def _lane_pick(blk, lane, idx):
    return jnp.sum(jnp.where(lane == idx, blk, 0.0), axis=1, keepdims=True)


def _head_masks():
    lane = lax.broadcasted_iota(jnp.int32, (1, LANES), 1)
    return lane, [(lane >= hh * FOX_HEAD_DIM) & (lane < (hh + 1) * FOX_HEAD_DIM) for hh in range(2)]


def _fox_fwd(qa, ka, vb, nb, lp, name, ride=()):
    m = qa.shape[0]
    tq = _att_tile(lp)
    nq = lp // tq
    npair = FOX_WIDTH // LANES
    grid = (nb, npair, nq)
    r_arrays, r_shapes, r_sems = _exchange_io((), ride)
    nr = len(r_arrays)

    def body(q_ref, k_ref, v_ref, *rest):
        r_in, (o_ref, lse_ref), r_out, sems = rest[:nr], rest[nr:nr + 2], rest[nr + 2:2 * nr + 2], rest[2 * nr + 2:]
        if nr:
            first, last = _grid_ends(grid)
            start, wait = _exchange_copies(r_in, r_out, 0, *sems)
            pl.when(first)(start)
        qi = pl.program_id(2)
        lane, hmasks = _head_masks()
        zero16 = jnp.zeros((), BF16)
        causal = lax.broadcasted_iota(jnp.int32, (tq, 1), 0) >= lax.broadcasted_iota(jnp.int32, (1, tq), 1)
        o_tot = jnp.zeros((tq, LANES), F32)
        lse_out = jnp.zeros((tq, LANES), F32)
        for hh in range(2):
            hs = slice(hh * LANES, (hh + 1) * LANES)
            q = q_ref[:, hs]

            def tile(j, carry, diagonal, hs=hs, q=q, hmask=hmasks[hh]):
                mx, l, acc = carry
                k0 = pl.multiple_of(j * tq, tq)
                vz = jnp.where(hmask, v_ref[pl.ds(k0, tq), :], zero16)
                s = _dot(q, k_ref[pl.ds(k0, tq), hs], NT)
                if diagonal:
                    s = jnp.where(causal, s, NEG)
                m_new = jnp.maximum(mx, jnp.max(s, axis=1, keepdims=True))
                alpha = jnp.exp(mx - m_new)
                pe = jnp.exp(s - m_new)
                l = alpha * l + jnp.sum(pe, axis=1, keepdims=True)
                acc = alpha * acc + _dot(pe.astype(BF16), vz, NN)
                return m_new, l, acc

            init = (jnp.full((tq, 1), NEG, F32), jnp.zeros((tq, 1), F32), jnp.zeros((tq, LANES), F32))
            carry = lax.fori_loop(0, qi, lambda j, c, tile=tile: tile(j, c, False), init)
            mx, l, acc = tile(qi, carry, True)
            o_tot = o_tot + acc / l
            lse_out = jnp.where(lane == hh, mx + jnp.log(l), lse_out)
        o_ref[...] = o_tot
        lse_ref[...] = lse_out
        if nr:
            pl.when(last)(wait)

    return pl.pallas_call(
        body, name=name, grid=grid,
        in_specs=[pl.BlockSpec((tq, 2 * LANES), lambda b, p, i: (b * nq + i, p)),
                  pl.BlockSpec((lp, 2 * LANES), lambda b, p, i: (b, p)),
                  pl.BlockSpec((lp, LANES), lambda b, p, i: (b, p))] + [_HBM] * nr,
        out_specs=[pl.BlockSpec((tq, LANES), lambda b, p, i: (b * nq + i, p)),
                   pl.BlockSpec((None, None, tq, LANES), lambda b, p, i: (b, p, i, 0))] + [_HBM] * nr,
        out_shape=[jax.ShapeDtypeStruct((m, FOX_WIDTH), F32),
                   jax.ShapeDtypeStruct((nb, npair, lp, LANES), F32)] + r_shapes,
        scratch_shapes=r_sems if nr else [],
        compiler_params=_cp(*(["arbitrary"] * 3 if nr else ["parallel", "parallel", "arbitrary"])),
    )(qa, ka, vb, *r_arrays)


def _fox_bwd(qa, ka, vb, do, o, lse, nb, lp, name, ride=()):
    m = qa.shape[0]
    tq = _att_tile(lp)
    nq = lp // tq
    npair = FOX_WIDTH // LANES
    grid = (nb, npair, nq)
    r_arrays, r_shapes, r_sems = _exchange_io(ride, ())
    nr = len(r_arrays)

    def body(k_ref, v_ref, q_ref, do_ref, o_ref, lse_ref, *rest):
        r_in, r_out, sems = rest[:nr], rest[nr + 5:2 * nr + 5], rest[2 * nr + 5:]
        dq_ref, dk_ref, dv_ref, dc0_ref, dc1_ref = rest[nr:nr + 5]
        if nr:
            first, last = _grid_ends(grid)
            start, wait = _exchange_copies(r_in, r_out, nr, *sems)
            pl.when(first)(start)
        j = pl.program_id(2)
        lane, hmasks = _head_masks()
        zero16 = jnp.zeros((), BF16)
        causal = lax.broadcasted_iota(jnp.int32, (tq, 1), 0) >= lax.broadcasted_iota(jnp.int32, (1, tq), 1)

        @pl.when(j == 0)
        def _():
            dq_ref[...] = jnp.zeros_like(dq_ref)

        vv = v_ref[...]
        vzs = [jnp.where(hm, vv, zero16) for hm in hmasks]

        def tile(qi, carry, diagonal):
            dk0, dk1, dv, dc0, dc1 = carry
            q0 = pl.multiple_of(qi * tq, tq)
            dob16 = do_ref[pl.ds(q0, tq), :].astype(BF16)
            ob = o_ref[pl.ds(q0, tq), :]
            lseb = lse_ref[pl.ds(q0, tq), :]
            dks, dcs = [dk0, dk1], [dc0, dc1]
            for hh in range(2):
                hs = slice(hh * LANES, (hh + 1) * LANES)
                q = q_ref[pl.ds(q0, tq), hs]
                doz16 = jnp.where(hmasks[hh], dob16, zero16)
                delta = jnp.sum(doz16.astype(F32) * ob, axis=1, keepdims=True)
                s = _dot(q, k_ref[:, hs], NT) - _lane_pick(lseb, lane, hh)
                if diagonal:
                    s = jnp.where(causal, s, NEG)
                pm = jnp.exp(s)
                ds = pm * (_dot(doz16, vzs[hh], NT) - delta)
                ds16 = ds.astype(BF16)
                dv = dv + _dot(pm.astype(BF16), doz16, TN)
                dks[hh] = dks[hh] + _dot(ds16, q, TN)
                dq_ref[pl.ds(q0, tq), hs] += _dot(ds16, k_ref[:, hs], NN)
                dcs[hh] = dcs[hh] - jnp.sum(ds, axis=0, keepdims=True)
            return dks[0], dks[1], dv, dcs[0], dcs[1]

        zt = jnp.zeros((tq, LANES), F32)
        zr = jnp.zeros((1, tq), F32)
        carry = tile(j, (zt, zt, zt, zr, zr), True)
        dk0, dk1, dv, dc0, dc1 = lax.fori_loop(j + 1, nq, lambda qi, c: tile(qi, c, False), carry)
        dk_ref[:, :LANES] = dk0
        dk_ref[:, LANES:] = dk1
        dv_ref[...] = dv
        dc0_ref[...] = dc0
        dc1_ref[...] = dc1
        if nr:
            pl.when(last)(wait)

    full2 = pl.BlockSpec((lp, 2 * LANES), lambda b, p, j: (b, p))
    full = pl.BlockSpec((lp, LANES), lambda b, p, j: (b, p))
    blk2 = pl.BlockSpec((tq, 2 * LANES), lambda b, p, j: (b * nq + j, p))
    blk = pl.BlockSpec((tq, LANES), lambda b, p, j: (b * nq + j, p))
    dcs = pl.BlockSpec((None, None, 1, tq), lambda b, p, j: (b, p, 0, j))
    return pl.pallas_call(
        body, name=name, grid=grid,
        in_specs=[blk2, blk, full2, full, full,
                  pl.BlockSpec((None, None, lp, LANES), lambda b, p, j: (b, p, 0, 0))] + [_HBM] * nr,
        out_specs=[full2, blk2, blk, dcs, dcs] + [_HBM] * nr,
        out_shape=[jax.ShapeDtypeStruct((m, 2 * FOX_WIDTH), F32)] * 2 + [jax.ShapeDtypeStruct((m, FOX_WIDTH), F32)]
        + [jax.ShapeDtypeStruct((nb, npair, 1, lp), F32)] * 2 + r_shapes,
        scratch_shapes=r_sems if nr else [],
        compiler_params=_cp(*(["arbitrary"] * 3 if nr else ["parallel", "parallel", "arbitrary"])),
    )(ka, vb, qa, do, o, lse, *r_arrays)


def _fox_prep_bwd(proj, dqa, dka, dv, dcum, qg, kg, bf, nb, lp, name):
    m = proj.shape[0]
    ts = SEQ_BLOCK
    nblk = lp // ts
    scale = FOX_HEAD_DIM ** -0.5
    w = FOX_WIDTH
    wo = 3 * w

    def body(q_ref, k_ref, f_ref, dq_ref, dk_ref, dv_ref, dc_ref, qg_ref, kg_ref, bf_ref,
             out_ref, dff_ref, dqg_ref, dkg_ref, dbf_ref, carry_ref):
        first = (pl.program_id(0) == 0) & (pl.program_id(1) == 0)
        lane = lax.broadcasted_iota(jnp.int32, (1, LANES), 1)
        lo = lane < FOX_HEAD_DIM

        @pl.when(first)
        def _():
            dqg_ref[...] = jnp.zeros_like(dqg_ref)
            dkg_ref[...] = jnp.zeros_like(dkg_ref)
            dbf_ref[...] = jnp.zeros_like(dbf_ref)

        def norm_bwd(x, g, dy):
            r = lax.rsqrt(_head_stats(x, lo) + EPS)
            nv = x * r
            gdy = dy * g
            prod = gdy * nv
            s_lo = jnp.sum(jnp.where(lo, prod, 0.0), axis=1, keepdims=True)
            s_hi = jnp.sum(jnp.where(lo, 0.0, prod), axis=1, keepdims=True)
            mean = jnp.where(lo, s_lo, s_hi) * (1.0 / FOX_HEAD_DIM)
            return r * (gdy - nv * mean), jnp.sum(dy * nv, axis=0, keepdims=True)

        def pair(d_ref, jj):
            even = d_ref[:, 2 * jj * LANES:(2 * jj + 1) * LANES]
            odd = d_ref[:, (2 * jj + 1) * LANES:(2 * jj + 2) * LANES]
            return jnp.where(lo, even, pltpu.roll(odd, FOX_HEAD_DIM, 1))

        for jj in range(w // LANES):
            cs = slice(jj * LANES, (jj + 1) * LANES)
            dx, dg = norm_bwd(q_ref[:, cs], qg_ref[:, cs], pair(dq_ref, jj) * scale)
            out_ref[:, cs] = dx.astype(BF16)
            dqg_ref[:, cs] += dg
            dx, dg = norm_bwd(k_ref[:, cs], kg_ref[:, cs], pair(dk_ref, jj))
            out_ref[:, w + jj * LANES:w + (jj + 1) * LANES] = dx.astype(BF16)
            dkg_ref[:, cs] += dg
        out_ref[:, 2 * w:3 * w] = dv_ref[...].astype(BF16)

        @pl.when(pl.program_id(1) == 0)
        def _():
            carry_ref[...] = jnp.zeros_like(carry_ref)

        dc = dc_ref[...]
        r = lax.broadcasted_iota(jnp.int32, (ts, ts), 0)
        c = lax.broadcasted_iota(jnp.int32, (ts, ts), 1)
        triu = jnp.where(c >= r, 1.0, 0.0).astype(F32)
        dlogf = _dot(triu, dc, NN, HI) + carry_ref[...]
        carry_ref[...] += jnp.sum(dc, axis=0, keepdims=True)
        z = f_ref[...] + bf_ref[...]
        dz = jnp.where(lane < FOX_HEADS, dlogf * _sigmoid(-z), 0.0)
        dff_ref[...] = dz.astype(BF16)
        dbf_ref[...] += jnp.sum(dz, axis=0, keepdims=True)

    rev = lambda b, i: (b * nblk + (nblk - 1 - i), 0)
    vec = lambda n: pl.BlockSpec((1, n), lambda b, i: (0, 0))
    return pl.pallas_call(
        body, name=name, grid=(nb, nblk),
        in_specs=[pl.BlockSpec((ts, w), lambda b, i: (b * nblk + (nblk - 1 - i), FOX_CB)),
                  pl.BlockSpec((ts, w), lambda b, i: (b * nblk + (nblk - 1 - i), FOX_CB + 1)),
                  pl.BlockSpec((ts, LANES), lambda b, i: (b * nblk + (nblk - 1 - i), CB_FF)),
                  pl.BlockSpec((ts, 2 * w), rev), pl.BlockSpec((ts, 2 * w), rev), pl.BlockSpec((ts, w), rev),
                  pl.BlockSpec((ts, LANES), rev), vec(w), vec(w), vec(LANES)],
        out_specs=[pl.BlockSpec((ts, wo), rev), pl.BlockSpec((ts, LANES), rev), vec(w), vec(w), vec(LANES)],
        out_shape=[jax.ShapeDtypeStruct((m, wo), BF16), jax.ShapeDtypeStruct((m, LANES), BF16),
                   jax.ShapeDtypeStruct((1, w), F32),
                   jax.ShapeDtypeStruct((1, w), F32), jax.ShapeDtypeStruct((1, LANES), F32)],
        scratch_shapes=[pltpu.VMEM((1, LANES), F32)],
        compiler_params=_cp("arbitrary", "arbitrary"),
    )(proj, proj, proj, dqa, dka, dv, dcum, qg, kg, bf)


def _chunk_masks():
    r = lax.broadcasted_iota(jnp.int32, (SEQ_BLOCK, SEQ_BLOCK), 0)
    c = lax.broadcasted_iota(jnp.int32, (SEQ_BLOCK, SEQ_BLOCK), 1)
    same = (r // SUB) == (c // SUB)
    return r, c, same


def _hg_gates(hf, lb):
    sg = _sigmoid(hf)
    f = lb + (1.0 - lb) * sg
    return sg, f, jnp.log(f), (1.0 - lb) * _sigmoid(-hf)


def _hg_intra_e(g_ref, base, t, srow):
    diff = g_ref[pl.ds(base + t, 1), :] - g_ref[pl.ds(base, SUB), :]
    return jnp.exp(jnp.where(srow <= t, diff, NEG))


def _hgrn_fwd(proj, lb, gain, nb, lp, name):
    m = proj.shape[0]
    tb = SEQ_BLOCK
    nblk = lp // tb
    ns = tb // SUB

    def body(q_ref, f_ref, i_ref, g_ref, lb_ref, gain_ref, oraw_ref, y_ref, ssave_ref,
             st_ref, g_scr, kin_scr, o_scr):
        @pl.when(pl.program_id(2) == 0)
        def _():
            st_ref[...] = jnp.zeros_like(st_ref)

        ssave_ref[...] = st_ref[...]
        lbv = lb_ref[...]
        _, _, lf, kin = _hg_gates(f_ref[...], lbv)
        r, c, same = _chunk_masks()
        ltri = jnp.where(same & (c <= r), 1.0, 0.0).astype(F32)
        lall = jnp.where(same, 1.0, 0.0).astype(F32)
        g = _dot(ltri, lf, NN, HI)
        gt = _dot(lall, lf, NN, HI)
        g_scr[...] = g
        kin_scr[...] = kin
        qv = q_ref[...]
        qg = (qv * jnp.exp(g)).astype(BF16)
        kg = (kin * jnp.exp(gt - g)).astype(BF16)
        et = jnp.exp(gt)
        srow = lax.broadcasted_iota(jnp.int32, (SUB, 1), 0)
        subs = [slice(cc * SUB, (cc + 1) * SUB) for cc in range(ns)]
        ups = [_dot(i_ref[sl, :].astype(BF16), kg[sl], TN) for sl in subs]
        st = st_ref[...]
        starts = []
        for cc in range(ns):
            starts.append(st)
            st = et[cc * SUB:cc * SUB + 1, :] * st + ups[cc]
        st_ref[...] = st
        for cc, sl in enumerate(subs):
            base = cc * SUB
            o_c = _dot(qg[sl], starts[cc].astype(BF16), NT)
            kc = kin_scr[sl, :]
            vc = i_ref[sl, :]
            for t in range(SUB):
                e = _hg_intra_e(g_scr, base, t, srow)
                a = jnp.sum((q_ref[pl.ds(base + t, 1), :] * kc) * e, axis=1, keepdims=True)
                ot = jnp.sum(a * vc, axis=0, keepdims=True)
                o_c = o_c + jnp.where(srow == t, ot, 0.0)
            o_scr[sl, :] = o_c
        o = o_scr[...]
        oraw_ref[...] = o
        rr = lax.rsqrt(jnp.mean(o * o, axis=-1, keepdims=True) + EPS)
        hg = g_ref[...]
        y_ref[...] = (((o * rr) * gain_ref[...]) * (hg * _sigmoid(hg))).astype(y_ref.dtype)

    col = lambda cb: pl.BlockSpec((tb, LANES), lambda b, h, i, cb=cb: (b * nblk + i, cb + h))
    out_blk = pl.BlockSpec((tb, LANES), lambda b, h, i: (b * nblk + i, h))
    return pl.pallas_call(
        body, name=name, grid=(nb, HG_HEADS, nblk),
        in_specs=[col(CB_HQ), col(CB_HF), col(CB_HI), col(CB_HG),
                  pl.BlockSpec((1, LANES), lambda b, h, i: (0, h)),
                  pl.BlockSpec((1, LANES), lambda b, h, i: (0, 0))],
        out_specs=[out_blk, out_blk,
                   pl.BlockSpec((None, None, None, HG_DIM, HG_DIM), lambda b, h, i: (b, h, i, 0, 0))],
        out_shape=[jax.ShapeDtypeStruct((m, HG_WIDTH), F32), jax.ShapeDtypeStruct((m, HG_WIDTH), BF16),
                   jax.ShapeDtypeStruct((nb, HG_HEADS, nblk, HG_DIM, HG_DIM), F32)],
        scratch_shapes=[pltpu.VMEM((HG_DIM, HG_DIM), F32), pltpu.VMEM((tb, LANES), F32),
                        pltpu.VMEM((tb, LANES), F32), pltpu.VMEM((tb, LANES), F32)],
        compiler_params=_cp("parallel", "parallel", "arbitrary"),
    )(proj, proj, proj, proj, lb, gain)


def _hgrn_bwd(proj, oraw, ssave, dy, lb, gain, nb, lp, name):
    m = proj.shape[0]
    tb = SEQ_BLOCK
    nblk = lp // tb
    ns = tb // SUB

    def body(q_ref, f_ref, i_ref, g_ref, oraw_ref, ssave_ref, dy_ref, lb_ref, gain_ref,
             dq_ref, df_ref, di_ref, dg_ref, dgain_ref, dlb_ref,
             dst_ref, g_scr, kin_scr, do_scr, dq_scr, dk_scr, dv_scr, dgg_scr):
        hd = pl.program_id(0)
        bb = pl.program_id(1)
        ii = pl.program_id(2)
        gainv = gain_ref[...]
        lbv = lb_ref[...]

        @pl.when((hd == 0) & (bb == 0) & (ii == 0))
        def _():
            dgain_ref[...] = jnp.zeros_like(dgain_ref)

        @pl.when((bb == 0) & (ii == 0))
        def _():
            dlb_ref[...] = jnp.zeros_like(dlb_ref)

        @pl.when(ii == 0)
        def _():
            dst_ref[...] = jnp.zeros_like(dst_ref)

        o = oraw_ref[...]
        rr = lax.rsqrt(jnp.mean(o * o, axis=-1, keepdims=True) + EPS)
        nv = o * rr
        hg = g_ref[...]
        sgg = _sigmoid(hg)
        sil = hg * sgg
        dyv = dy_ref[...]
        dg_ref[...] = (dyv * nv * gainv * (sgg * (1.0 + hg * (1.0 - sgg)))).astype(dg_ref.dtype)
        dgain_ref[...] += jnp.sum(dyv * nv * sil, axis=0, keepdims=True)
        dn = dyv * gainv * sil
        do_scr[...] = rr * (dn - nv * jnp.mean(dn * nv, axis=-1, keepdims=True))

        hf = f_ref[...]
        sg, f, lf, kin = _hg_gates(hf, lbv)
        r, c, same = _chunk_masks()
        ltri = jnp.where(same & (c <= r), 1.0, 0.0).astype(F32)
        lall = jnp.where(same, 1.0, 0.0).astype(F32)
        g = _dot(ltri, lf, NN, HI)
        gt = _dot(lall, lf, NN, HI)
        g_scr[...] = g
        kin_scr[...] = kin
        qv = q_ref[...]
        eg = jnp.exp(g)
        ekg = jnp.exp(gt - g)
        qg = qv * eg
        kg = kin * ekg
        qg16 = qg.astype(BF16)
        kg16 = kg.astype(BF16)
        et = jnp.exp(gt)
        subs = [slice(cc * SUB, (cc + 1) * SUB) for cc in range(ns)]
        ups = [_dot(i_ref[sl, :].astype(BF16), kg16[sl], TN) for sl in subs]
        st = ssave_ref[...]
        starts = []
        for cc in range(ns):
            starts.append(st)
            st = et[cc * SUB:cc * SUB + 1, :] * st + ups[cc]
        do16 = do_scr[...].astype(BF16)
        downs = [_dot(do16[sl], qg16[sl], TN) for sl in subs]
        dst = dst_ref[...]
        afters = [None] * ns
        for cc in reversed(range(ns)):
            afters[cc] = dst
            dst = et[cc * SUB:cc * SUB + 1, :] * dst + downs[cc]
        dst_ref[...] = dst

        srow = lax.broadcasted_iota(jnp.int32, (SUB, 1), 0)
        for cc, sl in enumerate(subs):
            base = cc * SUB
            st = starts[cc]
            st16 = st.astype(BF16)
            dst = afters[cc]
            dst16 = dst.astype(BF16)
            doc16 = do16[sl]
            vc = i_ref[sl, :]
            vc16 = vc.astype(BF16)
            kc = kin_scr[sl, :]
            etc = et[base:base + 1, :]
            dqg = _dot(doc16, st16, NN)
            dv_c = _dot(kg16[sl], dst16, NT)
            dkg = _dot(vc16, dst16, NN)
            dgt = jnp.sum(dst * st, axis=0, keepdims=True) * etc
            dq_c = dqg * eg[sl]
            dk_c = dkg * ekg[sl]
            dg_c = dqg * qg[sl] - dkg * kg[sl]
            dgt = dgt + jnp.sum(dkg * kg[sl], axis=0, keepdims=True)
            for t in range(SUB):
                e = _hg_intra_e(g_scr, base, t, srow)
                qt = q_ref[pl.ds(base + t, 1), :]
                dot_t = do_scr[pl.ds(base + t, 1), :]
                a = jnp.sum((qt * kc) * e, axis=1, keepdims=True)
                da = jnp.sum(dot_t * vc, axis=1, keepdims=True)
                dv_c = dv_c + a * dot_t
                w = da * e
                dq_t = jnp.sum(w * kc, axis=0, keepdims=True)
                wq = w * qt
                dk_c = dk_c + wq
                dg_c = dg_c - kc * wq + jnp.where(srow == t, qt * dq_t, 0.0)
                dq_c = dq_c + jnp.where(srow == t, dq_t, 0.0)
            dg_c = dg_c + jnp.where(srow == SUB - 1, dgt, 0.0)
            dq_scr[sl, :] = dq_c
            dk_scr[sl, :] = dk_c
            dv_scr[sl, :] = dv_c
            dgg_scr[sl, :] = dg_c

        utri = jnp.where(same & (c >= r), 1.0, 0.0).astype(F32)
        dlf = _dot(utri, dgg_scr[...], NN, HI)
        dkin = dk_scr[...]
        dsg = sg * (1.0 - sg)
        df_ref[...] = ((dlf / f - dkin) * ((1.0 - lbv) * dsg)).astype(df_ref.dtype)
        dlb_ref[...] += jnp.sum((dlf / f - dkin) * (1.0 - sg), axis=0, keepdims=True)
        dq_ref[...] = dq_scr[...].astype(dq_ref.dtype)
        di_ref[...] = dv_scr[...].astype(di_ref.dtype)

    rowi = lambda b, i: b * nblk + (nblk - 1 - i)
    col = lambda cb: pl.BlockSpec((tb, LANES), lambda h, b, i, cb=cb: (rowi(b, i), cb + h))
    hblk = pl.BlockSpec((tb, LANES), lambda h, b, i: (rowi(b, i), h))
    return pl.pallas_call(
        body, name=name, grid=(HG_HEADS, nb, nblk),
        in_specs=[col(CB_HQ), col(CB_HF), col(CB_HI), col(CB_HG), hblk,
                  pl.BlockSpec((None, None, None, HG_DIM, HG_DIM), lambda h, b, i: (b, h, nblk - 1 - i, 0, 0)),
                  hblk,
                  pl.BlockSpec((1, LANES), lambda h, b, i: (0, h)),
                  pl.BlockSpec((1, LANES), lambda h, b, i: (0, 0))],
        out_specs=[hblk, hblk, hblk, hblk,
                   pl.BlockSpec((1, LANES), lambda h, b, i: (0, 0)),
                   pl.BlockSpec((1, LANES), lambda h, b, i: (0, h))],
        out_shape=[jax.ShapeDtypeStruct((m, HG_WIDTH), BF16)] * 4
        + [jax.ShapeDtypeStruct((1, LANES), F32), jax.ShapeDtypeStruct((1, HG_WIDTH), F32)],
        scratch_shapes=[pltpu.VMEM((HG_DIM, HG_DIM), F32)] + [pltpu.VMEM((tb, LANES), F32)] * 7,
        compiler_params=_cp("arbitrary", "arbitrary", "arbitrary"),
    )(proj, proj, proj, proj, oraw, ssave, dy, lb, gain)


def _gate_fwd(proj, ya, yb, name):
    m = proj.shape[0]
    tm = _div_tile(m, 256, 16)

    def body(ga_ref, gb_ref, ya_ref, yb_ref, o_ref):
        ya, yb = ya_ref[...].astype(F32), yb_ref[...].astype(F32)
        o_ref[...] = (_sigmoid(ga_ref[...]) * ya + _sigmoid(gb_ref[...]) * yb).astype(o_ref.dtype)

    row = pl.BlockSpec((tm, D_MODEL), lambda i: (i, 0))
    return pl.pallas_call(
        body, name=name, grid=(m // tm,),
        in_specs=[row, pl.BlockSpec((tm, D_MODEL), lambda i: (i, 1)), row, row],
        out_specs=row, out_shape=jax.ShapeDtypeStruct((m, D_MODEL), BF16),
        compiler_params=_cp("parallel"),
    )(proj, proj, ya, yb)


def _gate_bwd(proj, ya, yb, dm, name):
    m = proj.shape[0]
    tm = _div_tile(m, 256, 16)

    def body(ga_ref, gb_ref, ya_ref, yb_ref, dm_ref, dya_ref, dyb_ref, dg_ref):
        dmv = dm_ref[...].astype(F32)
        sa = _sigmoid(ga_ref[...])
        sb = _sigmoid(gb_ref[...])
        dya_ref[...] = (dmv * sa).astype(BF16)
        dyb_ref[...] = (dmv * sb).astype(BF16)
        dg_ref[:, :D_MODEL] = (dmv * ya_ref[...].astype(F32) * (sa * (1.0 - sa))).astype(BF16)
        dg_ref[:, D_MODEL:] = (dmv * yb_ref[...].astype(F32) * (sb * (1.0 - sb))).astype(BF16)

    row = pl.BlockSpec((tm, D_MODEL), lambda i: (i, 0))
    wide = pl.BlockSpec((tm, 2 * D_MODEL), lambda i: (i, 0))
    return pl.pallas_call(
        body, name=name, grid=(m // tm,),
        in_specs=[row, pl.BlockSpec((tm, D_MODEL), lambda i: (i, 1)), row, row, row],
        out_specs=[row, row, wide],
        out_shape=[jax.ShapeDtypeStruct((m, D_MODEL), BF16)] * 2 + [jax.ShapeDtypeStruct((m, 2 * D_MODEL), BF16)],
        compiler_params=_cp("parallel"),
    )(proj, proj, ya, yb, dm)


CONV_ROWS = 128


def _conv3(x, xprev, w_ref, b_ref, rowi):
    r = x.shape[0]
    x1 = jnp.where(rowi < 1, pltpu.roll(xprev, 1, 0), pltpu.roll(x, 1, 0))
    x2 = jnp.where(rowi < 2, pltpu.roll(xprev, 2, 0), pltpu.roll(x, 2, 0))
    u = w_ref[0:1, :] * x2 + w_ref[1:2, :] * x1 + w_ref[2:3, :] * x + b_ref[...]
    return u, x1, x2


def _conv_fwd(up, cw, cb, nb, lp, name):
    m = up.shape[0]
    nct = D_FF // LANES
    r = CONV_ROWS
    nch = lp // r

    def body(u_ref, w_ref, b_ref, o_ref):
        rowi = lax.broadcasted_iota(jnp.int32, (r, 1), 0)

        def step(i, xp):
            r0 = pl.multiple_of(i * r, r)
            xc = u_ref[pl.ds(r0, r), :].astype(F32)
            u, _, _ = _conv3(xc, xp, w_ref, b_ref, rowi)
            ug, uv = u[:, :LANES], u[:, LANES:]
            o_ref[pl.ds(r0, r), :] = ((ug * _sigmoid(ug)) * uv).astype(o_ref.dtype)
            return xc

        lax.fori_loop(0, nch, step, jnp.zeros((r, 2 * LANES), F32))

    return pl.pallas_call(
        body, name=name, grid=(nb, nct),
        in_specs=[pl.BlockSpec((lp, 2 * LANES), lambda b, c: (b, c)),
                  pl.BlockSpec((CONV_WIDTH, 2 * LANES), lambda b, c: (0, c)),
                  pl.BlockSpec((1, 2 * LANES), lambda b, c: (0, c))],
        out_specs=pl.BlockSpec((lp, LANES), lambda b, c: (b, c)),
        out_shape=jax.ShapeDtypeStruct((m, D_FF), BF16),
        compiler_params=_cp("parallel", "parallel"),
    )(up, cw, cb)


def _conv_bwd(up, dact, cw, cb, nb, lp, name):
    m = up.shape[0]
    nct = D_FF // LANES
    r = CONV_ROWS
    nch = lp // r

    def body(u_ref, da_ref, w_ref, b_ref, dup_ref, dw_ref, db_ref):
        rowi = lax.broadcasted_iota(jnp.int32, (r, 1), 0)
        wv = w_ref[...]

        def step(k, carry):
            dun, dw0, dw1, dw2, dbs = carry
            i = nch - 1 - k
            r0 = pl.multiple_of(i * r, r)
            rp = pl.multiple_of(jnp.maximum(i - 1, 0) * r, r)
            xc = u_ref[pl.ds(r0, r), :].astype(F32)
            xp = u_ref[pl.ds(rp, r), :].astype(F32) * (i > 0).astype(F32)
            u, x1, x2 = _conv3(xc, xp, w_ref, b_ref, rowi)
            ug, uv = u[:, :LANES], u[:, LANES:]
            da = da_ref[pl.ds(r0, r), :].astype(F32)
            sg = _sigmoid(ug)
            du = jnp.concatenate([da * uv * (sg * (1.0 + ug * (1.0 - sg))), da * (ug * sg)], axis=1)
            d1 = jnp.where(rowi >= r - 1, pltpu.roll(dun, r - 1, 0), pltpu.roll(du, r - 1, 0))
            d2 = jnp.where(rowi >= r - 2, pltpu.roll(dun, r - 2, 0), pltpu.roll(du, r - 2, 0))
            dup_ref[pl.ds(r0, r), :] = (wv[2:3, :] * du + wv[1:2, :] * d1 + wv[0:1, :] * d2).astype(dup_ref.dtype)
            dw0 = dw0 + jnp.sum(du * x2, axis=0, keepdims=True)
            dw1 = dw1 + jnp.sum(du * x1, axis=0, keepdims=True)
            dw2 = dw2 + jnp.sum(du * xc, axis=0, keepdims=True)
            dbs = dbs + jnp.sum(du, axis=0, keepdims=True)
            return du, dw0, dw1, dw2, dbs

        z1 = jnp.zeros((1, 2 * LANES), F32)
        _, dw0, dw1, dw2, dbs = lax.fori_loop(0, nch, step, (jnp.zeros((r, 2 * LANES), F32), z1, z1, z1, z1))

        @pl.when(pl.program_id(1) == 0)
        def _():
            dw_ref[...] = jnp.zeros_like(dw_ref)
            db_ref[...] = jnp.zeros_like(db_ref)

        dw_ref[0:1, :] += dw0
        dw_ref[1:2, :] += dw1
        dw_ref[2:3, :] += dw2
        db_ref[...] += dbs

    return pl.pallas_call(
        body, name=name, grid=(nct, nb),
        in_specs=[pl.BlockSpec((lp, 2 * LANES), lambda c, b: (b, c)),
                  pl.BlockSpec((lp, LANES), lambda c, b: (b, c)),
                  pl.BlockSpec((CONV_WIDTH, 2 * LANES), lambda c, b: (0, c)),
                  pl.BlockSpec((1, 2 * LANES), lambda c, b: (0, c))],
        out_specs=[pl.BlockSpec((lp, 2 * LANES), lambda c, b: (b, c)),
                   pl.BlockSpec((CONV_WIDTH, 2 * LANES), lambda c, b: (0, c)),
                   pl.BlockSpec((1, 2 * LANES), lambda c, b: (0, c))],
        out_shape=[jax.ShapeDtypeStruct((m, 2 * D_FF), BF16),
                   jax.ShapeDtypeStruct((CONV_WIDTH, 2 * D_FF), F32),
                   jax.ShapeDtypeStruct((1, 2 * D_FF), F32)],
        compiler_params=_cp("parallel", "arbitrary"),
    )(up, dact, cw, cb)


def _ffn_interleave(a, axis):
    shp = a.shape
    a = a.reshape(shp[:axis] + (2, D_FF // LANES, LANES) + shp[axis + 1:])
    return jnp.swapaxes(a, axis, axis + 1).reshape(shp)


def _ffn_deinterleave(a, axis):
    shp = a.shape
    a = a.reshape(shp[:axis] + (D_FF // LANES, 2, LANES) + shp[axis + 1:])
    return jnp.swapaxes(a, axis, axis + 1).reshape(shp)


def _shifted_rows(prev_ref, cur_ref):
    keep = SEQ_BLOCK - N_META
    return jnp.concatenate([prev_ref[keep:, :], cur_ref[:keep, :]], axis=0)


def _frame_specs(nblk, nfb, d):
    prev = pl.BlockSpec((SEQ_BLOCK, d), lambda b, i: (b * nfb + jnp.clip(i - 1, 0, nfb - 1), 0))
    cur = pl.BlockSpec((SEQ_BLOCK, d), lambda b, i: (b * nfb + jnp.clip(i, 0, nfb - 1), 0))
    return prev, cur


def _embed_rms(x2, meta, gain, nb, lp, l, name):
    d = x2.shape[1]
    tr = SEQ_BLOCK
    nblk = lp // tr
    nfb = (l - N_META) // tr
    m = nb * lp

    def body(prev_ref, cur_ref, meta_ref, g_ref, h_ref, o_ref):
        i = pl.program_id(1)
        t = i * tr + lax.broadcasted_iota(jnp.int32, (tr, 1), 0)
        rows = jnp.where(t < l, _shifted_rows(prev_ref, cur_ref), 0.0)
        head = jnp.concatenate([meta_ref[...], jnp.zeros((tr - N_META, d), F32)], axis=0)
        xv = jnp.where(t < N_META, head, rows)
        h_ref[...] = xv
        r = lax.rsqrt(jnp.mean(xv * xv, axis=-1, keepdims=True) + EPS)
        o_ref[...] = ((xv * r) * g_ref[...]).astype(o_ref.dtype)

    prev, cur = _frame_specs(nblk, nfb, d)
    row = pl.BlockSpec((tr, d), lambda b, i: (b * nblk + i, 0))
    return pl.pallas_call(
        body, name=name, grid=(nb, nblk),
        in_specs=[prev, cur, pl.BlockSpec((N_META, d), lambda b, i: (0, 0)), pl.BlockSpec((1, d), lambda b, i: (0, 0))],
        out_specs=[row, row],
        out_shape=[jax.ShapeDtypeStruct((m, d), F32), jax.ShapeDtypeStruct((m, d), BF16)],
        compiler_params=_cp("parallel", "parallel"),
    )(x2, x2, meta, gain)


def _loss_head(out, tgt2, nb, lp, l, name):
    m, d = out.shape
    tr = SEQ_BLOCK
    nblk = lp // tr
    nfb = (l - N_META) // tr

    def body(o_ref, prev_ref, cur_ref, dy_ref, ls_ref):
        t = pl.program_id(1) * tr + lax.broadcasted_iota(jnp.int32, (tr, 1), 0)
        valid = (t >= N_META) & (t < l)
        err = jnp.where(valid, o_ref[...] - _shifted_rows(prev_ref, cur_ref), 0.0)
        dy_ref[...] = err * (1.0 / d)
        part = jnp.sum(err * err, axis=0, keepdims=True)
        first = (pl.program_id(0) == 0) & (pl.program_id(1) == 0)

        @pl.when(first)
        def _():
            ls_ref[...] = part

        @pl.when(jnp.logical_not(first))
        def _():
            ls_ref[...] += part

    prev, cur = _frame_specs(nblk, nfb, d)
    row = pl.BlockSpec((tr, d), lambda b, i: (b * nblk + i, 0))
    return pl.pallas_call(
        body, name=name, grid=(nb, nblk),
        in_specs=[row, prev, cur], out_specs=[row, pl.BlockSpec((1, d), lambda b, i: (0, 0))],
        out_shape=[jax.ShapeDtypeStruct((m, d), F32), jax.ShapeDtypeStruct((1, d), F32)],
        compiler_params=_cp("arbitrary", "arbitrary"),
    )(out, tgt2, tgt2)


def _adam_math(g, w, mom, var):
    c1 = 1.0 - ADAM_B1 ** ADAM_STEP
    c2 = 1.0 - ADAM_B2 ** ADAM_STEP
    mn = ADAM_B1 * mom + (1.0 - ADAM_B1) * g
    vn = ADAM_B2 * var + (1.0 - ADAM_B2) * (g * g)
    delta = -ADAM_LR * ((mn / c1) / (jnp.sqrt(vn / c2) + ADAM_EPS) + ADAM_WD * w)
    return delta, mn, vn


def _slot_sum(recv, name):
    _, r, c = recv.shape
    tc = _div_tile(c, 256, LANES)

    def body(r_ref, g_ref):
        g = r_ref[0].astype(F32)
        for s in range(1, N_DEV):
            g = g + r_ref[s].astype(F32)
        g_ref[...] = g

    return pl.pallas_call(
        body, name=name, grid=(c // tc,),
        in_specs=[pl.BlockSpec((N_DEV, r, tc), lambda j: (0, 0, j))],
        out_specs=pl.BlockSpec((r, tc), lambda j: (0, j)),
        out_shape=jax.ShapeDtypeStruct((r, c), F32),
        compiler_params=_cp("parallel"),
    )(recv)


def _adamw(g, w, mom, var, name):
    r, c = w.shape
    tr = _div_tile(r, 256, 8)

    def body(g_ref, w_ref, m_ref, v_ref, d_ref, mo_ref, vo_ref):
        d_ref[...], mo_ref[...], vo_ref[...] = _adam_math(g_ref[...], w_ref[...], m_ref[...], v_ref[...])

    row = pl.BlockSpec((tr, c), lambda i: (i, 0))
    return pl.pallas_call(
        body, name=name, grid=(r // tr,), in_specs=[row] * 4, out_specs=[row] * 3,
        out_shape=[jax.ShapeDtypeStruct((r, c), F32)] * 3,
        compiler_params=_cp("parallel"),
    )(g, w, mom, var)


def _sum_adamw(recv, w, mom, var, name):
    r, c = w.shape
    tr = _div_tile(r, 256, 8)

    def body(r_ref, w_ref, m_ref, v_ref, g_ref, d_ref, mo_ref, vo_ref):
        g = r_ref[0].astype(F32)
        for s in range(1, N_DEV):
            g = g + r_ref[s].astype(F32)
        g_ref[...] = g
        d_ref[...], mo_ref[...], vo_ref[...] = _adam_math(g, w_ref[...], m_ref[...], v_ref[...])

    row = pl.BlockSpec((tr, c), lambda i: (i, 0))
    return pl.pallas_call(
        body, name=name, grid=(r // tr,),
        in_specs=[pl.BlockSpec((N_DEV, tr, c), lambda i: (0, i, 0)), row, row, row],
        out_specs=[row] * 4,
        out_shape=[jax.ShapeDtypeStruct((r, c), F32)] * 4,
        compiler_params=_cp("parallel"),
    )(recv, w, mom, var)


_MESH = pl.DeviceIdType.MESH
_HBM = pl.BlockSpec(memory_space=pltpu.HBM)
N_PEER = N_DEV - 1


def _position():
    return lax.axis_index("x"), lax.axis_index("y"), lax.axis_index("c")


def _all_gather(shards, name):
    n = len(shards)

    def body(*refs):
        x_refs, out_refs = refs[:n], refs[n:2 * n]
        send_sems, recv_sems, local_sems = refs[2 * n:]
        x, y, c = _position()
        me, sibling = (x, y, c), (x, y, 1 - c)
        chips = [(1 - x, y), (x, 1 - y), (1 - x, 1 - y)]

        def copy(a, k, block, to, src=None):
            slot = out_refs[a].at[4 * block[0] + 2 * block[1] + block[2]]
            return pltpu.make_async_remote_copy(
                src_ref=slot if src is None else src, dst_ref=slot,
                send_sem=send_sems.at[a * N_PEER + k], recv_sem=recv_sems.at[a * N_PEER + k],
                device_id=to, device_id_type=_MESH)

        mine, sent = [], []
        for a in range(n):
            cp = pltpu.make_async_copy(x_refs[a], out_refs[a].at[4 * x + 2 * y + c], local_sems.at[a])
            cp.start()
            mine.append(cp)
            first = [copy(a, 0, me, sibling, src=x_refs[a])]
            first += [copy(a, 1 + j, me, (*chip, c), src=x_refs[a]) for j, chip in enumerate(chips)]
            for cp in first:
                cp.start()
            sent += first
        for a in range(n):
            for j, chip in enumerate(chips):
                copy(a, 1 + j, (*chip, c), me).wait_recv()
                fwd = copy(a, 4 + j, (*chip, c), sibling)
                fwd.start()
                sent.append(fwd)
        for a in range(n):
            copy(a, 0, sibling, me).wait_recv()
            for j, chip in enumerate(chips):
                copy(a, 4 + j, (*chip, 1 - c), me).wait_recv()
        for cp in sent:
            cp.wait_send()
        for cp in mine:
            cp.wait()

    return pl.pallas_call(
        body, name=name,
        out_shape=[jax.ShapeDtypeStruct((N_DEV,) + a.shape, a.dtype) for a in shards],
        in_specs=[_HBM] * n, out_specs=[_HBM] * n,
        scratch_shapes=[pltpu.SemaphoreType.DMA((n * N_PEER,)), pltpu.SemaphoreType.DMA((n * N_PEER,)),
                        pltpu.SemaphoreType.DMA((n,))],
    )(*shards)


_FLIPS = [(fx, fy, fc) for fx in (0, 1) for fy in (0, 1) for fc in (0, 1)][1:]


def _exchange_copies(in_refs, out_refs, nblk, send_sems, recv_sems, local_sems):
    n = len(in_refs)
    x, y, c = _position()
    me = 4 * x + 2 * y + c

    def peer(f):
        return (1 - x if f[0] else x, 1 - y if f[1] else y, 1 - c if f[2] else c)

    def idx(p):
        return 4 * p[0] + 2 * p[1] + p[2]

    def local(a):
        return pltpu.make_async_copy(in_refs[a].at[me] if a < nblk else in_refs[a], out_refs[a].at[me], local_sems.at[a])

    def remote(a, k, sending):
        p = peer(_FLIPS[k])
        src = in_refs[a].at[idx(p)] if a < nblk else in_refs[a]
        dst = out_refs[a].at[me] if sending else out_refs[a].at[idx(p)]
        return pltpu.make_async_remote_copy(
            src_ref=src, dst_ref=dst, send_sem=send_sems.at[a * N_PEER + k], recv_sem=recv_sems.at[a * N_PEER + k],
            device_id=p, device_id_type=_MESH)

    def start():
        for a in range(n):
            local(a).start()
            for k in range(N_PEER):
                remote(a, k, True).start()

    def wait():
        for a in range(n):
            for k in range(N_PEER):
                remote(a, k, False).wait_recv()
        for a in range(n):
            for k in range(N_PEER):
                remote(a, k, True).wait_send()
            local(a).wait()

    return start, wait


def _exchange_io(blocks, shared):
    arrays = list(blocks) + list(shared)
    n = len(arrays)
    out_shape = [jax.ShapeDtypeStruct(a.shape, a.dtype) for a in blocks]
    out_shape += [jax.ShapeDtypeStruct((N_DEV,) + a.shape, a.dtype) for a in shared]
    sems = [pltpu.SemaphoreType.DMA((n * N_PEER,)), pltpu.SemaphoreType.DMA((n * N_PEER,)), pltpu.SemaphoreType.DMA((n,))]
    return arrays, out_shape, sems


def _exchange(blocks, shared, name):
    arrays, out_shape, sems = _exchange_io(blocks, shared)
    n = len(arrays)

    def body(*refs):
        start, wait = _exchange_copies(refs[:n], refs[n:2 * n], len(blocks), *refs[2 * n:])
        start()
        wait()

    return pl.pallas_call(
        body, name=name, out_shape=out_shape, in_specs=[_HBM] * n, out_specs=[_HBM] * n, scratch_shapes=sems,
    )(*arrays)


def _grid_ends(grid):
    ids = [pl.program_id(i) for i in range(len(grid))]
    first = functools.reduce(jnp.logical_and, [i == 0 for i in ids])
    last = functools.reduce(jnp.logical_and, [i == g - 1 for i, g in zip(ids, grid)])
    return first, last


def _pack(parts, rows):
    flat = jnp.concatenate(parts, axis=-1)
    return jnp.pad(flat, [(0, rows * LANES - flat.shape[-1])]).reshape(rows, LANES)


def _unpack(packed, shapes):
    flat = packed.reshape(-1)
    out, off = [], 0
    for shp in shapes:
        n = int(np.prod(shp))
        out.append(flat[off:off + n].reshape(shp))
        off += n
    return out


def _rows_for(shapes, extra=0):
    n = sum(int(np.prod(s)) for s in shapes) + extra
    return -(-n // (8 * LANES)) * 8


def _lower_bound(logits):
    return jnp.cumsum(jax.nn.softmax(logits.astype(F32), axis=0), axis=0)[0:1]


def _align_axis0(w):
    a, b = 3 * FOX_WIDTH, 3 * FOX_WIDTH + FOX_HEADS
    c = b + 4 * HG_WIDTH
    pad = [(0, LANES - FOX_HEADS)] + [(0, 0)] * (w.ndim - 1)
    return jnp.concatenate([w[c:], w[:a], w[b:c], jnp.pad(w[a:b], pad)], axis=0)


def _unalign_axis0(g):
    a, b = 2 * D_MODEL, 2 * D_MODEL + 3 * FOX_WIDTH
    c = b + 4 * HG_WIDTH
    return jnp.concatenate([g[a:b], g[c:c + FOX_HEADS], g[b:c], g[:a]], axis=0)


TINY_COLS = 768


def _tiny_pack(conv_w_shard, meta_shard):
    cw = jnp.pad(conv_w_shard, ((0, 8 - CONV_WIDTH), (0, TINY_COLS - conv_w_shard.shape[1])))
    mt = jnp.pad(meta_shard, ((0, 0), (0, TINY_COLS - meta_shard.shape[1])))
    return jnp.concatenate([cw, mt], axis=0)


def _tiny_unpack(t, ncw, nmeta):
    return t[..., :CONV_WIDTH, :ncw], t[..., 8:8 + N_META, :nmeta]


def _ffn_weights(g_up, g_down):
    d = g_up.shape[-1]
    return _ffn_interleave(g_up.reshape(-1, d), 0), g_down.reshape(-1, d)


def _early_blocks(g_w_up_t, g_w_down, g_w_out, g_w_a_t, g_w_b_t):
    d = g_w_out.shape[-1]
    ab = jnp.stack([g_w_a_t.reshape(N_DEV, -1, g_w_a_t.shape[-1]), g_w_b_t.reshape(N_DEV, -1, g_w_b_t.shape[-1])], axis=1)
    return [_ffn_deinterleave(g_w_up_t, 0).reshape(N_DEV, -1, d).astype(BF16), g_w_down.reshape(N_DEV, -1, d).astype(BF16),
            g_w_out.reshape(N_DEV, -1, d).astype(BF16), ab.astype(BF16)]


def _local_step(x, target, meta, norm1_gain, w_in_t, fox_b_f, q_gain, k_gain, lb, hg_out_gain, w_a_t, w_b_t, w_out,
                norm2_gain, w_up_t, conv_w, conv_b, w_down, ffn_shards=None):
    nb, seq, d = x.shape
    assert seq % SEQ_BLOCK == 0 and N_META < SEQ_BLOCK
    l = seq + N_META
    lp = -(-l // SEQ_BLOCK) * SEQ_BLOCK
    m = nb * lp
    qg = jnp.tile(q_gain, (1, FOX_HEADS))
    kg = jnp.tile(k_gain, (1, FOX_HEADS))
    bf = jnp.pad(fox_b_f, ((0, 0), (0, LANES - FOX_HEADS)))

    h0, xn = _embed_rms(x.reshape(nb * seq, d), meta, norm1_gain, nb, lp, l, "embed_rms1")
    proj = _matmul(xn, w_in_t, "nt", F32, "proj_in")
    qa, ka, vb = _fox_prep(proj, qg, kg, bf, nb, lp, "fox_prep")
    if ffn_shards is None:
        o_fox, lse = _fox_fwd(qa, ka, vb, nb, lp, "fox_fwd")
    else:
        o_fox, lse, g_up, g_down = _fox_fwd(qa, ka, vb, nb, lp, "fox_fwd", ride=ffn_shards)
        w_up_t, w_down = _ffn_weights(g_up, g_down)
    o_raw, o_hg, s_save = _hgrn_fwd(proj, lb, hg_out_gain, nb, lp, "hgrn_fwd")
    ya = _matmul(o_hg, w_a_t, "nt", BF16, "branch_a")
    yb = _matmul(o_fox, w_b_t, "nt", BF16, "branch_b")
    merged = _gate_fwd(proj, ya, yb, "gate_fwd")
    h1 = _matmul(merged, w_out, "nn", F32, "mix_out", residual=h0)
    hn = _rms_fwd(h1, norm2_gain, "rms2_fwd")
    up = _matmul(hn, w_up_t, "nt", BF16, "ffn_up")
    act = _conv_fwd(up, conv_w, conv_b, nb, lp, "conv_fwd")
    out = _matmul(act, w_down, "nn", F32, "ffn_down", residual=h1)
    dy, lsum = _loss_head(out, target.reshape(nb * seq, d), nb, lp, l, "loss_head")
    loss = (0.5 / d) * jnp.sum(lsum)

    dact = _matmul(dy, w_down, "nt", BF16, "d_act")
    g_w_down = _matmul(act, dy, "tn", F32, "g_w_down")
    dup, g_conv_w, g_conv_b = _conv_bwd(up, dact, conv_w, conv_b, nb, lp, "conv_bwd")
    dhn = _matmul(dup, w_up_t, "nn", F32, "d_hn")
    g_w_up_t = _matmul(dup, hn, "tn", F32, "g_w_up")
    dh1, g_norm2 = _rms_bwd(h1, norm2_gain, dhn, dy, "rms2_bwd")

    dmerged = _matmul(dh1, w_out, "nt", BF16, "d_merged")
    g_w_out = _matmul(merged, dh1, "tn", F32, "g_w_out")
    dya, dyb, dgab = _gate_bwd(proj, ya, yb, dmerged, "gate_bwd")
    do_hg = _matmul(dya, w_a_t, "nn", F32, "d_o_hg")
    g_w_a_t = _matmul(dya, o_hg, "tn", F32, "g_w_a")
    do_fox = _matmul(dyb, w_b_t, "nn", BF16, "d_o_fox")
    g_w_b_t = _matmul(dyb, o_fox, "tn", F32, "g_w_b")
    dhq, dhf, dhi, dhg, g_hg_gain, g_lb = _hgrn_bwd(proj, o_raw, s_save, do_hg, lb, hg_out_gain, nb, lp, "hgrn_bwd")
    if ffn_shards is None:
        dqs, dkn, dvv, dc0, dc1 = _fox_bwd(qa, ka, vb, do_fox, o_fox, lse, nb, lp, "fox_bwd")
        early = None
    else:
        dqs, dkn, dvv, dc0, dc1, *early = _fox_bwd(qa, ka, vb, do_fox, o_fox, lse, nb, lp, "fox_bwd",
                                                   ride=_early_blocks(g_w_up_t, g_w_down, g_w_out, g_w_a_t, g_w_b_t))
    dcum = jnp.stack([dc0, dc1], axis=2).reshape(nb, FOX_HEADS, lp)
    dcum = jnp.pad(jnp.transpose(dcum, (0, 2, 1)), ((0, 0), (0, 0), (0, LANES - FOX_HEADS))).reshape(m, LANES)
    dfqkv, dff, g_qg, g_kg, g_bf = _fox_prep_bwd(proj, dqs, dkn, dvv, dcum, qg, kg, bf, nb, lp, "fox_prep_bwd")
    dproj = jnp.concatenate([dgab, dfqkv, dhq, dhf, dhi, dhg, dff], axis=1)
    g_w_in_t = _matmul(dproj, xn, "tn", F32, "g_w_in")
    if ffn_shards is None:
        dxn = _matmul(dproj, w_in_t, "nn", F32, "d_xn")
    else:
        blocks_in = _unalign_axis0(g_w_in_t).reshape(N_DEV, -1, d).astype(BF16)
        dxn, r_in = _matmul(dproj, w_in_t, "nn", F32, "d_xn", ride=[blocks_in])
        early = early + [r_in]
    dh0, g_norm1 = _rms_bwd(h0, norm1_gain, dxn, dh1, "rms1_bwd")

    dh0 = dh0.reshape(nb, lp, d)
    grad_x = dh0[:, N_META:l]
    g_meta = jnp.sum(dh0[:, :N_META], axis=0)
    g_q_gain = jnp.sum(g_qg.reshape(FOX_HEADS, FOX_HEAD_DIM), axis=0, keepdims=True)
    g_k_gain = jnp.sum(g_kg.reshape(FOX_HEADS, FOX_HEAD_DIM), axis=0, keepdims=True)
    grads = dict(meta_tokens=g_meta, norm1_gain=g_norm1, w_in_t=g_w_in_t, fox_b_f=g_bf[:, :FOX_HEADS],
                 q_norm_gain=g_q_gain, k_norm_gain=g_k_gain, lb=g_lb, hg_out_gain=g_hg_gain,
                 w_a_t=g_w_a_t, w_b_t=g_w_b_t, w_out=g_w_out, norm2_gain=g_norm2, w_up_t=g_w_up_t,
                 conv_w=g_conv_w, conv_b=g_conv_b, w_down=g_w_down, early=early)
    return loss, grad_x, grads


SMALL = ("norm1_gain", "fox_b_f", "q_norm_gain", "k_norm_gain", "hg_lb_logits", "hg_out_gain", "norm2_gain", "conv_b")
ORDER = ("meta_tokens", "norm1_gain", "w_in", "fox_b_f", "q_norm_gain", "k_norm_gain", "hg_lb_logits", "hg_out_gain",
         "w_branch_a", "w_branch_b", "w_out", "norm2_gain", "w_up", "conv_w", "conv_b", "w_down")


def kernel(x, meta_tokens, norm1_gain, w_in, fox_b_f, q_norm_gain, k_norm_gain, hg_lb_logits, hg_out_gain, w_branch_a, w_branch_b, w_out, norm2_gain, w_up, conv_w, conv_b, w_down, loss_target, m_meta_tokens, m_norm1_gain, m_w_in, m_fox_b_f, m_q_norm_gain, m_k_norm_gain, m_hg_lb_logits, m_hg_out_gain, m_w_branch_a, m_w_branch_b, m_w_out, m_norm2_gain, m_w_up, m_conv_w, m_conv_b, m_w_down, v_meta_tokens, v_norm1_gain, v_w_in, v_fox_b_f, v_q_norm_gain, v_k_norm_gain, v_hg_lb_logits, v_hg_out_gain, v_w_branch_a, v_w_branch_b, v_w_out, v_norm2_gain, v_w_up, v_conv_w, v_conv_b, v_w_down):
    w = dict(meta_tokens=meta_tokens, norm1_gain=norm1_gain, w_in=w_in, fox_b_f=fox_b_f, q_norm_gain=q_norm_gain,
             k_norm_gain=k_norm_gain, hg_lb_logits=hg_lb_logits, hg_out_gain=hg_out_gain, w_branch_a=w_branch_a,
             w_branch_b=w_branch_b, w_out=w_out, norm2_gain=norm2_gain, w_up=w_up, conv_w=conv_w, conv_b=conv_b,
             w_down=w_down)
    mom = dict(meta_tokens=m_meta_tokens, norm1_gain=m_norm1_gain, w_in=m_w_in, fox_b_f=m_fox_b_f,
               q_norm_gain=m_q_norm_gain, k_norm_gain=m_k_norm_gain, hg_lb_logits=m_hg_lb_logits,
               hg_out_gain=m_hg_out_gain, w_branch_a=m_w_branch_a, w_branch_b=m_w_branch_b, w_out=m_w_out,
               norm2_gain=m_norm2_gain, w_up=m_w_up, conv_w=m_conv_w, conv_b=m_conv_b, w_down=m_w_down)
    var = dict(meta_tokens=v_meta_tokens, norm1_gain=v_norm1_gain, w_in=v_w_in, fox_b_f=v_fox_b_f,
               q_norm_gain=v_q_norm_gain, k_norm_gain=v_k_norm_gain, hg_lb_logits=v_hg_lb_logits,
               hg_out_gain=v_hg_out_gain, w_branch_a=v_w_branch_a, w_branch_b=v_w_branch_b, w_out=v_w_out,
               norm2_gain=v_norm2_gain, w_up=v_w_up, conv_w=v_conv_w, conv_b=v_conv_b, w_down=v_w_down)
    d = D_MODEL
    n_in, n_up = w_in.shape[2], w_up.shape[2]
    n_ab, n_meta = w_branch_a.shape[2], meta_tokens.shape[1]

    shards = [w_in[0].T.astype(BF16),
              jnp.stack([w_branch_a[0].T, w_branch_b[0].T]).astype(BF16),
              w_out[0].astype(BF16),
              _tiny_pack(conv_w[0], meta_tokens)]
    g_in, g_ab, g_out, g_tiny = _all_gather(shards, "gather_weights")
    w_in_t = _align_axis0(g_in.reshape(N_DEV * n_in, d))
    w_a_t = g_ab[:, 0].reshape(N_DEV * n_ab, -1)
    w_b_t = g_ab[:, 1].reshape(N_DEV * n_ab, -1)
    cw_slots, meta_slots = _tiny_unpack(g_tiny, n_up, n_meta)
    conv_w_f = _ffn_interleave(jnp.transpose(cw_slots, (1, 0, 2)).reshape(CONV_WIDTH, -1), 1)
    meta_f = jnp.transpose(meta_slots, (1, 0, 2)).reshape(N_META, -1)
    conv_b_i = _ffn_interleave(conv_b, 1)

    lb, lb_vjp = jax.vjp(_lower_bound, hg_lb_logits)
    loss, grad_x, g = _local_step(
        x, loss_target, meta_f, norm1_gain, w_in_t, fox_b_f, q_norm_gain, k_norm_gain, lb, hg_out_gain,
        w_a_t, w_b_t, g_out.reshape(d, d), norm2_gain, None, conv_w_f, conv_b_i, None,
        ffn_shards=(w_up[0].T.astype(BF16), w_down[0].astype(BF16)))

    g["hg_lb_logits"] = lb_vjp(g.pop("lb"))[0]
    g["conv_b"] = _ffn_deinterleave(g["conv_b"], 1)
    gcw = _ffn_deinterleave(g["conv_w"], 1).reshape(CONV_WIDTH, N_DEV, n_up)
    gmeta = g["meta_tokens"].reshape(N_META, N_DEV, n_meta)
    tiny = jnp.concatenate([
        jnp.pad(jnp.transpose(gcw, (1, 0, 2)), ((0, 0), (0, 8 - CONV_WIDTH), (0, TINY_COLS - n_up))),
        jnp.pad(jnp.transpose(gmeta, (1, 0, 2)), ((0, 0), (0, 0), (0, TINY_COLS - n_meta)))], axis=1)
    small_shapes = [w[n].shape for n in SMALL]
    rows_sm = _rows_for(small_shapes, extra=1)
    small = _pack([g[n].reshape(-1) for n in SMALL] + [loss.reshape(1)], rows_sm)
    r_tiny, r_small = _exchange([tiny], [small], "exchange_grads")
    r_up, r_down, r_out, r_ab, r_in = g["early"]

    res = {}
    g_in_s = _slot_sum(r_in, "sum_w_in").T
    res["w_in"] = (g_in_s,) + tuple(_adamw(g_in_s, w_in[0], m_w_in[0], v_w_in[0], "adamw_w_in"))
    g_up_s = _slot_sum(r_up, "sum_w_up").T
    res["w_up"] = (g_up_s,) + tuple(_adamw(g_up_s, w_up[0], m_w_up[0], v_w_up[0], "adamw_w_up"))
    g_ab_s = jnp.swapaxes(_slot_sum(r_ab.reshape(N_DEV, 2 * n_ab, -1), "sum_w_ab").reshape(2, n_ab, -1), 1, 2)
    ab = lambda t: jnp.concatenate([t["w_branch_a"][0], t["w_branch_b"][0]], axis=0)
    o_ab = (g_ab_s.reshape(-1, n_ab),) + tuple(_adamw(g_ab_s.reshape(-1, n_ab), ab(w), ab(mom), ab(var), "adamw_w_ab"))
    half = o_ab[0].shape[0] // 2
    res["w_branch_a"] = tuple(o[:half] for o in o_ab)
    res["w_branch_b"] = tuple(o[half:] for o in o_ab)
    res["w_out"] = tuple(_sum_adamw(r_out, w_out[0], m_w_out[0], v_w_out[0], "adamw_w_out"))
    res["w_down"] = tuple(_sum_adamw(r_down, w_down[0], m_w_down[0], v_w_down[0], "adamw_w_down"))
    tp = lambda t: _tiny_pack(t["conv_w"][0], t["meta_tokens"])
    o_tiny = [_tiny_unpack(o, n_up, n_meta) for o in _sum_adamw(r_tiny, tp(w), tp(mom), tp(var), "adamw_tiny")]
    res["conv_w"] = tuple(o[0] for o in o_tiny)
    res["meta_tokens"] = tuple(o[1] for o in o_tiny)
    zero1 = jnp.zeros((1,), F32)
    sp = lambda t: _pack([t[n].reshape(-1) for n in SMALL] + [zero1], rows_sm)
    o_small = [_unpack(o, small_shapes + [(1,)]) for o in _sum_adamw(r_small, sp(w), sp(mom), sp(var), "adamw_small")]
    for i, n in enumerate(SMALL):
        res[n] = tuple(o[i] for o in o_small)
    loss_all = o_small[0][len(SMALL)].reshape(())

    result = [[res[n][k].reshape(w[n].shape) for n in ORDER] for k in range(4)]
    return (loss_all, grad_x, *result[0], *result[1], *result[2], *result[3])
```

```python
import functools

import jax
import jax.numpy as jnp
import numpy as np
from jax import lax
from jax.experimental import pallas as pl
from jax.experimental.pallas import tpu as pltpu

F32 = jnp.float32
BF16 = jnp.bfloat16

D_MODEL = 1024
N_META = 16
FOX_HEADS = 8
FOX_HEAD_DIM = 64
FOX_WIDTH = FOX_HEADS * FOX_HEAD_DIM
HG_HEADS = 4
HG_DIM = 128
HG_WIDTH = HG_HEADS * HG_DIM
D_FF = 2816
CONV_WIDTH = 3
EPS = 1e-6
IN_COLS = 3 * FOX_WIDTH + FOX_HEADS + 4 * HG_WIDTH + 2 * D_MODEL
N_DEV = 8

ADAM_LR = 0.001
ADAM_B1 = 0.9
ADAM_B2 = 0.999
ADAM_EPS = 1e-08
ADAM_WD = 0.01
ADAM_STEP = 10

LANES = 128
SEQ_BLOCK = 128
SUB = 16
NEG = -1e30
VMEM_LIMIT = 48 * 1024 * 1024

FOX_CB = 2 * D_MODEL // FOX_WIDTH
CB_HQ = (2 * D_MODEL + 3 * FOX_WIDTH) // LANES
CB_HF = CB_HQ + HG_HEADS
CB_HI = CB_HF + HG_HEADS
CB_HG = CB_HI + HG_HEADS
CB_FF = CB_HG + HG_HEADS


def _div_tile(n, target, mult):
    best = None
    for t in range(mult, min(n, target) + 1, mult):
        if n % t == 0:
            best = t
    if best is None:
        best = n
    return best


def _cp(*sem):
    return pltpu.CompilerParams(dimension_semantics=sem, vmem_limit_bytes=VMEM_LIMIT)


def _sigmoid(x):
    return 0.5 * jnp.tanh(0.5 * x) + 0.5


def _dot(a, b, dims, precision=None):
    return lax.dot_general(a, b, (dims, ((), ())), preferred_element_type=F32, precision=precision)


NN = ((1,), (0,))
NT = ((1,), (1,))
TN = ((0,), (0,))
HI = lax.Precision.HIGHEST


MATMUL_VMEM_BUDGET = 30 * 1024 * 1024
MATMUL_MAX_TILE = 2048


def _tile_options(n):
    return [t for t in range(LANES, min(n, MATMUL_MAX_TILE) + 1, LANES) if n % t == 0] or [n]


def _matmul_tiles(m, n, k, a_bytes, b_bytes, o_bytes, has_res):
    tk = _div_tile(k, MATMUL_MAX_TILE, LANES)
    best = None
    for tm in _tile_options(m):
        for tn in _tile_options(n):
            vmem = 2 * (tm * tk * a_bytes + tk * tn * b_bytes) + 2 * tm * tn * o_bytes
            vmem += tm * tn * 4 if (tk < k and o_bytes != 4) else 0
            vmem += 2 * tm * tn * 4 if has_res else 0
            if vmem > MATMUL_VMEM_BUDGET:
                continue
            key = (tm * tn, tn % 256 == 0, tn)
            if best is None or key > best[0]:
                best = (key, tm, tn)
    assert best is not None, (m, n, k)
    return best[1], best[2], tk


def _matmul(a, b, mode, out_dtype, name, residual=None, ride=()):
    if mode == "nn":
        (m, k), (k2, n) = a.shape, b.shape
    elif mode == "nt":
        (m, k), (n, k2) = a.shape, b.shape
    else:
        (k, m), (k2, n) = a.shape, b.shape
    assert k == k2, (a.shape, b.shape, mode)
    has_res = residual is not None
    tm, tn, tk = _matmul_tiles(m, n, k, a.dtype.itemsize, b.dtype.itemsize, jnp.dtype(out_dtype).itemsize, has_res)
    nk = k // tk
    in_place = jnp.dtype(out_dtype) == jnp.dtype(F32)
    if mode == "nn":
        a_spec = pl.BlockSpec((tm, tk), lambda i, j, kk: (i, kk))
        b_spec = pl.BlockSpec((tk, tn), lambda i, j, kk: (kk, j))
        dims = NN
    elif mode == "nt":
        a_spec = pl.BlockSpec((tm, tk), lambda i, j, kk: (i, kk))
        b_spec = pl.BlockSpec((tn, tk), lambda i, j, kk: (j, kk))
        dims = NT
    else:
        a_spec = pl.BlockSpec((tk, tm), lambda i, j, kk: (kk, i))
        b_spec = pl.BlockSpec((tk, tn), lambda i, j, kk: (kk, j))
        dims = TN
    o_spec = pl.BlockSpec((tm, tn), lambda i, j, kk: (i, j))
    grid = (m // tm, n // tn, nk)
    x_arrays, x_shapes, x_sems = _exchange_io(ride, ())
    nx = len(x_arrays)
    n_in = 3 if has_res else 2

    def body(*refs):
        if nx:
            first, last = _grid_ends(grid)
            x_in, x_out = refs[n_in:n_in + nx], refs[n_in + nx + 1:n_in + 2 * nx + 1]
            start, wait = _exchange_copies(x_in, x_out, nx, *refs[n_in + 2 * nx + 1:n_in + 2 * nx + 4])
            pl.when(first)(start)
        compute(*refs)
        if nx:
            pl.when(last)(wait)

    def compute(*refs):
        a_ref, b_ref = refs[0], refs[1]
        r_ref = refs[2] if has_res else None
        o_ref = refs[n_in + nx]
        if nk == 1:
            part = _dot(a_ref[...].astype(BF16), b_ref[...].astype(BF16), dims)
            o_ref[...] = (part + r_ref[...] if has_res else part).astype(o_ref.dtype)
            return
        acc_ref = o_ref if in_place else refs[-1]
        kk = pl.program_id(2)

        @pl.when(kk == 0)
        def _():
            acc_ref[...] = r_ref[...] if (has_res and in_place) else jnp.zeros_like(acc_ref)

        acc_ref[...] += _dot(a_ref[...].astype(BF16), b_ref[...].astype(BF16), dims)

        if not in_place:
            @pl.when(kk == nk - 1)
            def _():
                acc = acc_ref[...]
                if has_res:
                    acc = acc + r_ref[...]
                o_ref[...] = acc.astype(o_ref.dtype)

    in_specs = [a_spec, b_spec] + ([o_spec] if has_res else [])
    args = (a, b) + ((residual,) if has_res else ())
    out_shape = jax.ShapeDtypeStruct((m, n), out_dtype)
    acc = [pltpu.VMEM((tm, tn), F32)] if (nk > 1 and not in_place) else []
    if not nx:
        return pl.pallas_call(
            body, name=name, grid=grid, in_specs=in_specs, out_specs=o_spec, out_shape=out_shape, scratch_shapes=acc,
            compiler_params=_cp("parallel", "parallel", "arbitrary"),
        )(*args)
    return pl.pallas_call(
        body, name=name, grid=grid, in_specs=in_specs + [_HBM] * nx, out_specs=[o_spec] + [_HBM] * nx,
        out_shape=[out_shape] + x_shapes, scratch_shapes=x_sems + acc,
        compiler_params=_cp("arbitrary", "arbitrary", "arbitrary"),
    )(*args, *x_arrays)


def _rms_fwd(x, gain, name):
    m, d = x.shape
    tm = _div_tile(m, 512, 16)

    def body(x_ref, g_ref, o_ref):
        xv = x_ref[...]
        r = lax.rsqrt(jnp.mean(xv * xv, axis=-1, keepdims=True) + EPS)
        o_ref[...] = ((xv * r) * g_ref[...]).astype(o_ref.dtype)

    return pl.pallas_call(
        body, name=name, grid=(m // tm,),
        in_specs=[pl.BlockSpec((tm, d), lambda i: (i, 0)), pl.BlockSpec((1, d), lambda i: (0, 0))],
        out_specs=pl.BlockSpec((tm, d), lambda i: (i, 0)),
        out_shape=jax.ShapeDtypeStruct((m, d), BF16),
        compiler_params=_cp("parallel"),
    )(x, gain)


def _rms_bwd(x, gain, dy, dres, name):
    m, d = x.shape
    tm = _div_tile(m, 256, 8)

    def body(x_ref, g_ref, dy_ref, dr_ref, dx_ref, dg_ref):
        xv = x_ref[...]
        r = lax.rsqrt(jnp.mean(xv * xv, axis=-1, keepdims=True) + EPS)
        nv = xv * r
        dyv = dy_ref[...]
        gdy = dyv * g_ref[...]
        dx_ref[...] = dr_ref[...] + r * (gdy - nv * jnp.mean(gdy * nv, axis=-1, keepdims=True))
        part = jnp.sum(dyv * nv, axis=0, keepdims=True)

        @pl.when(pl.program_id(0) == 0)
        def _():
            dg_ref[...] = part

        @pl.when(pl.program_id(0) > 0)
        def _():
            dg_ref[...] += part

    row = pl.BlockSpec((tm, d), lambda i: (i, 0))
    vec = pl.BlockSpec((1, d), lambda i: (0, 0))
    return pl.pallas_call(
        body, name=name, grid=(m // tm,),
        in_specs=[row, vec, row, row], out_specs=[row, vec],
        out_shape=[jax.ShapeDtypeStruct((m, d), F32), jax.ShapeDtypeStruct((1, d), F32)],
        compiler_params=_cp("arbitrary"),
    )(x, gain, dy, dres)


def _head_stats(xv, lo):
    sq = xv * xv
    s_lo = jnp.sum(jnp.where(lo, sq, 0.0), axis=1, keepdims=True)
    s_hi = jnp.sum(jnp.where(lo, 0.0, sq), axis=1, keepdims=True)
    return jnp.where(lo, s_lo, s_hi) * (1.0 / FOX_HEAD_DIM)


BIAS_LANE = FOX_HEAD_DIM
N_SPLIT = 3


def _split3(c):
    c1 = c.astype(BF16).astype(F32)
    r1 = c - c1
    c2 = r1.astype(BF16).astype(F32)
    c3 = (r1 - c2).astype(BF16).astype(F32)
    return c1, c2, c3


def _fox_prep(proj, qg, kg, bf, nb, lp, name):
    m = proj.shape[0]
    ts = SEQ_BLOCK
    nblk = lp // ts
    scale = FOX_HEAD_DIM ** -0.5

    def body(q_ref, k_ref, v_ref, f_ref, qg_ref, kg_ref, bf_ref, qo_ref, ko_ref, vo_ref, carry_ref):
        lane = lax.broadcasted_iota(jnp.int32, (1, LANES), 1)
        lo = lane < FOX_HEAD_DIM

        @pl.when(pl.program_id(1) == 0)
        def _():
            carry_ref[...] = jnp.zeros_like(carry_ref)

        z = f_ref[...] + bf_ref[...]
        logf = jnp.minimum(z, 0.0) - jnp.log(1.0 + jnp.exp(-jnp.abs(z)))
        logf = jnp.where(lane < FOX_HEADS, logf, 0.0)
        r = lax.broadcasted_iota(jnp.int32, (ts, ts), 0)
        c = lax.broadcasted_iota(jnp.int32, (ts, ts), 1)
        tri = jnp.where(c <= r, 1.0, 0.0).astype(F32)
        cum = _dot(tri, logf, NN, HI) + carry_ref[...]
        carry_ref[...] = cum[ts - 1:ts, :]

        ones = jnp.where((lane >= BIAS_LANE + N_SPLIT) & (lane < BIAS_LANE + 2 * N_SPLIT), 1.0, 0.0)
        ones_k = jnp.where((lane >= BIAS_LANE) & (lane < BIAS_LANE + N_SPLIT), 1.0, 0.0)
        for j in range(FOX_WIDTH // LANES):
            cs = slice(j * LANES, (j + 1) * LANES)
            xq = q_ref[:, cs]
            yq = ((xq * lax.rsqrt(_head_stats(xq, lo) + EPS)) * qg_ref[:, cs]) * scale
            xk = k_ref[:, cs]
            yk = (xk * lax.rsqrt(_head_stats(xk, lo) + EPS)) * kg_ref[:, cs]
            for hh in range(2):
                h = 2 * j + hh
                pieces = _split3(_lane_pick(cum, lane, h))
                qb, kb = ones, ones_k
                for i, piece in enumerate(pieces):
                    qb = jnp.where(lane == BIAS_LANE + i, piece, qb)
                    kb = jnp.where(lane == BIAS_LANE + N_SPLIT + i, -piece, kb)
                yq_h = yq if hh == 0 else pltpu.roll(yq, FOX_HEAD_DIM, 1)
                yk_h = yk if hh == 0 else pltpu.roll(yk, FOX_HEAD_DIM, 1)
                hs = slice(h * LANES, (h + 1) * LANES)
                qo_ref[:, hs] = jnp.where(lo, yq_h, qb).astype(BF16)
                ko_ref[:, hs] = jnp.where(lo, yk_h, kb).astype(BF16)
        vo_ref[...] = v_ref[...].astype(BF16)

    w = FOX_WIDTH
    row = lambda b, i: (b * nblk + i, 0)
    return pl.pallas_call(
        body, name=name, grid=(nb, nblk),
        in_specs=[pl.BlockSpec((ts, w), lambda b, i: (b * nblk + i, FOX_CB)),
                  pl.BlockSpec((ts, w), lambda b, i: (b * nblk + i, FOX_CB + 1)),
                  pl.BlockSpec((ts, w), lambda b, i: (b * nblk + i, FOX_CB + 2)),
                  pl.BlockSpec((ts, LANES), lambda b, i: (b * nblk + i, CB_FF)),
                  pl.BlockSpec((1, w), lambda b, i: (0, 0)),
                  pl.BlockSpec((1, w), lambda b, i: (0, 0)),
                  pl.BlockSpec((1, LANES), lambda b, i: (0, 0))],
        out_specs=[pl.BlockSpec((ts, 2 * w), row), pl.BlockSpec((ts, 2 * w), row), pl.BlockSpec((ts, w), row)],
        out_shape=[jax.ShapeDtypeStruct((m, 2 * w), BF16)] * 2 + [jax.ShapeDtypeStruct((m, w), BF16)],
        scratch_shapes=[pltpu.VMEM((1, LANES), F32)],
        compiler_params=_cp("arbitrary", "arbitrary"),
    )(proj, proj, proj, proj, qg, kg, bf)


def _att_tile(lp):
    return 384 if (lp % 384 == 0 and lp > 384) else 128


def _lane_pick(blk, lane, idx):
    return jnp.sum(jnp.where(lane == idx, blk, 0.0), axis=1, keepdims=True)


def _head_masks():
    lane = lax.broadcasted_iota(jnp.int32, (1, LANES), 1)
    return lane, [(lane >= hh * FOX_HEAD_DIM) & (lane < (hh + 1) * FOX_HEAD_DIM) for hh in range(2)]


def _fox_fwd(qa, ka, vb, nb, lp, name, ride=()):
    m = qa.shape[0]
    tq = _att_tile(lp)
    nq = lp // tq
    npair = FOX_WIDTH // LANES
    grid = (nb, npair, nq)
    r_arrays, r_shapes, r_sems = _exchange_io((), ride)
    nr = len(r_arrays)

    def body(q_ref, k_ref, v_ref, *rest):
        r_in, (o_ref, lse_ref), r_out, sems = rest[:nr], rest[nr:nr + 2], rest[nr + 2:2 * nr + 2], rest[2 * nr + 2:]
        if nr:
            first, last = _grid_ends(grid)
            start, wait = _exchange_copies(r_in, r_out, 0, *sems)
            pl.when(first)(start)
        qi = pl.program_id(2)
        lane, hmasks = _head_masks()
        zero16 = jnp.zeros((), BF16)
        causal = lax.broadcasted_iota(jnp.int32, (tq, 1), 0) >= lax.broadcasted_iota(jnp.int32, (1, tq), 1)
        o_tot = jnp.zeros((tq, LANES), F32)
        lse_out = jnp.zeros((tq, LANES), F32)
        for hh in range(2):
            hs = slice(hh * LANES, (hh + 1) * LANES)
            q = q_ref[:, hs]

            def tile(j, carry, diagonal, hs=hs, q=q, hmask=hmasks[hh]):
                mx, l, acc = carry
                k0 = pl.multiple_of(j * tq, tq)
                vz = jnp.where(hmask, v_ref[pl.ds(k0, tq), :], zero16)
                s = _dot(q, k_ref[pl.ds(k0, tq), hs], NT)
                if diagonal:
                    s = jnp.where(causal, s, NEG)
                m_new = jnp.maximum(mx, jnp.max(s, axis=1, keepdims=True))
                alpha = jnp.exp(mx - m_new)
                pe = jnp.exp(s - m_new)
                l = alpha * l + jnp.sum(pe, axis=1, keepdims=True)
                acc = alpha * acc + _dot(pe.astype(BF16), vz, NN)
                return m_new, l, acc

            init = (jnp.full((tq, 1), NEG, F32), jnp.zeros((tq, 1), F32), jnp.zeros((tq, LANES), F32))
            carry = lax.fori_loop(0, qi, lambda j, c, tile=tile: tile(j, c, False), init)
            mx, l, acc = tile(qi, carry, True)
            o_tot = o_tot + acc / l
            lse_out = jnp.where(lane == hh, mx + jnp.log(l), lse_out)
        o_ref[...] = o_tot
        lse_ref[...] = lse_out
        if nr:
            pl.when(last)(wait)

    return pl.pallas_call(
        body, name=name, grid=grid,
        in_specs=[pl.BlockSpec((tq, 2 * LANES), lambda b, p, i: (b * nq + i, p)),
                  pl.BlockSpec((lp, 2 * LANES), lambda b, p, i: (b, p)),
                  pl.BlockSpec((lp, LANES), lambda b, p, i: (b, p))] + [_HBM] * nr,
        out_specs=[pl.BlockSpec((tq, LANES), lambda b, p, i: (b * nq + i, p)),
                   pl.BlockSpec((None, None, tq, LANES), lambda b, p, i: (b, p, i, 0))] + [_HBM] * nr,
        out_shape=[jax.ShapeDtypeStruct((m, FOX_WIDTH), F32),
                   jax.ShapeDtypeStruct((nb, npair, lp, LANES), F32)] + r_shapes,
        scratch_shapes=r_sems if nr else [],
        compiler_params=_cp(*(["arbitrary"] * 3 if nr else ["parallel", "parallel", "arbitrary"])),
    )(qa, ka, vb, *r_arrays)


def _fox_bwd(qa, ka, vb, do, o, lse, nb, lp, name, ride=()):
    m = qa.shape[0]
    tq = _att_tile(lp)
    nq = lp // tq
    npair = FOX_WIDTH // LANES
    grid = (nb, npair, nq)
    r_arrays, r_shapes, r_sems = _exchange_io(ride, ())
    nr = len(r_arrays)

    def body(k_ref, v_ref, q_ref, do_ref, o_ref, lse_ref, *rest):
        r_in, r_out, sems = rest[:nr], rest[nr + 5:2 * nr + 5], rest[2 * nr + 5:]
        dq_ref, dk_ref, dv_ref, dc0_ref, dc1_ref = rest[nr:nr + 5]
        if nr:
            first, last = _grid_ends(grid)
            start, wait = _exchange_copies(r_in, r_out, nr, *sems)
            pl.when(first)(start)
        j = pl.program_id(2)
        lane, hmasks = _head_masks()
        zero16 = jnp.zeros((), BF16)
        causal = lax.broadcasted_iota(jnp.int32, (tq, 1), 0) >= lax.broadcasted_iota(jnp.int32, (1, tq), 1)

        @pl.when(j == 0)
        def _():
            dq_ref[...] = jnp.zeros_like(dq_ref)

        vv = v_ref[...]
        vzs = [jnp.where(hm, vv, zero16) for hm in hmasks]

        def tile(qi, carry, diagonal):
            dk0, dk1, dv, dc0, dc1 = carry
            q0 = pl.multiple_of(qi * tq, tq)
            dob16 = do_ref[pl.ds(q0, tq), :].astype(BF16)
            ob = o_ref[pl.ds(q0, tq), :]
            lseb = lse_ref[pl.ds(q0, tq), :]
            dks, dcs = [dk0, dk1], [dc0, dc1]
            for hh in range(2):
                hs = slice(hh * LANES, (hh + 1) * LANES)
                q = q_ref[pl.ds(q0, tq), hs]
                doz16 = jnp.where(hmasks[hh], dob16, zero16)
                delta = jnp.sum(doz16.astype(F32) * ob, axis=1, keepdims=True)
                s = _dot(q, k_ref[:, hs], NT) - _lane_pick(lseb, lane, hh)
                if diagonal:
                    s = jnp.where(causal, s, NEG)
                pm = jnp.exp(s)
                ds = pm * (_dot(doz16, vzs[hh], NT) - delta)
                ds16 = ds.astype(BF16)
                dv = dv + _dot(pm.astype(BF16), doz16, TN)
                dks[hh] = dks[hh] + _dot(ds16, q, TN)
                dq_ref[pl.ds(q0, tq), hs] += _dot(ds16, k_ref[:, hs], NN)
                dcs[hh] = dcs[hh] - jnp.sum(ds, axis=0, keepdims=True)
            return dks[0], dks[1], dv, dcs[0], dcs[1]

        zt = jnp.zeros((tq, LANES), F32)
        zr = jnp.zeros((1, tq), F32)
        carry = tile(j, (zt, zt, zt, zr, zr), True)
        dk0, dk1, dv, dc0, dc1 = lax.fori_loop(j + 1, nq, lambda qi, c: tile(qi, c, False), carry)
        dk_ref[:, :LANES] = dk0
        dk_ref[:, LANES:] = dk1
        dv_ref[...] = dv
        dc0_ref[...] = dc0
        dc1_ref[...] = dc1
        if nr:
            pl.when(last)(wait)

    full2 = pl.BlockSpec((lp, 2 * LANES), lambda b, p, j: (b, p))
    full = pl.BlockSpec((lp, LANES), lambda b, p, j: (b, p))
    blk2 = pl.BlockSpec((tq, 2 * LANES), lambda b, p, j: (b * nq + j, p))
    blk = pl.BlockSpec((tq, LANES), lambda b, p, j: (b * nq + j, p))
    dcs = pl.BlockSpec((None, None, 1, tq), lambda b, p, j: (b, p, 0, j))
    return pl.pallas_call(
        body, name=name, grid=grid,
        in_specs=[blk2, blk, full2, full, full,
                  pl.BlockSpec((None, None, lp, LANES), lambda b, p, j: (b, p, 0, 0))] + [_HBM] * nr,
        out_specs=[full2, blk2, blk, dcs, dcs] + [_HBM] * nr,
        out_shape=[jax.ShapeDtypeStruct((m, 2 * FOX_WIDTH), F32)] * 2 + [jax.ShapeDtypeStruct((m, FOX_WIDTH), F32)]
        + [jax.ShapeDtypeStruct((nb, npair, 1, lp), F32)] * 2 + r_shapes,
        scratch_shapes=r_sems if nr else [],
        compiler_params=_cp(*(["arbitrary"] * 3 if nr else ["parallel", "parallel", "arbitrary"])),
    )(ka, vb, qa, do, o, lse, *r_arrays)


def _fox_prep_bwd(proj, dqa, dka, dv, dcum, qg, kg, bf, nb, lp, name):
    m = proj.shape[0]
    ts = SEQ_BLOCK
    nblk = lp // ts
    scale = FOX_HEAD_DIM ** -0.5
    w = FOX_WIDTH
    wo = 3 * w

    def body(q_ref, k_ref, f_ref, dq_ref, dk_ref, dv_ref, dc_ref, qg_ref, kg_ref, bf_ref,
             out_ref, dff_ref, dqg_ref, dkg_ref, dbf_ref, carry_ref):
        first = (pl.program_id(0) == 0) & (pl.program_id(1) == 0)
        lane = lax.broadcasted_iota(jnp.int32, (1, LANES), 1)
        lo = lane < FOX_HEAD_DIM

        @pl.when(first)
        def _():
            dqg_ref[...] = jnp.zeros_like(dqg_ref)
            dkg_ref[...] = jnp.zeros_like(dkg_ref)
            dbf_ref[...] = jnp.zeros_like(dbf_ref)

        def norm_bwd(x, g, dy):
            r = lax.rsqrt(_head_stats(x, lo) + EPS)
            nv = x * r
            gdy = dy * g
            prod = gdy * nv
            s_lo = jnp.sum(jnp.where(lo, prod, 0.0), axis=1, keepdims=True)
            s_hi = jnp.sum(jnp.where(lo, 0.0, prod), axis=1, keepdims=True)
            mean = jnp.where(lo, s_lo, s_hi) * (1.0 / FOX_HEAD_DIM)
            return r * (gdy - nv * mean), jnp.sum(dy * nv, axis=0, keepdims=True)

        def pair(d_ref, jj):
            even = d_ref[:, 2 * jj * LANES:(2 * jj + 1) * LANES]
            odd = d_ref[:, (2 * jj + 1) * LANES:(2 * jj + 2) * LANES]
            return jnp.where(lo, even, pltpu.roll(odd, FOX_HEAD_DIM, 1))

        for jj in range(w // LANES):
            cs = slice(jj * LANES, (jj + 1) * LANES)
            dx, dg = norm_bwd(q_ref[:, cs], qg_ref[:, cs], pair(dq_ref, jj) * scale)
            out_ref[:, cs] = dx.astype(BF16)
            dqg_ref[:, cs] += dg
            dx, dg = norm_bwd(k_ref[:, cs], kg_ref[:, cs], pair(dk_ref, jj))
            out_ref[:, w + jj * LANES:w + (jj + 1) * LANES] = dx.astype(BF16)
            dkg_ref[:, cs] += dg
        out_ref[:, 2 * w:3 * w] = dv_ref[...].astype(BF16)

        @pl.when(pl.program_id(1) == 0)
        def _():
            carry_ref[...] = jnp.zeros_like(carry_ref)

        dc = dc_ref[...]
        r = lax.broadcasted_iota(jnp.int32, (ts, ts), 0)
        c = lax.broadcasted_iota(jnp.int32, (ts, ts), 1)
        triu = jnp.where(c >= r, 1.0, 0.0).astype(F32)
        dlogf = _dot(triu, dc, NN, HI) + carry_ref[...]
        carry_ref[...] += jnp.sum(dc, axis=0, keepdims=True)
        z = f_ref[...] + bf_ref[...]
        dz = jnp.where(lane < FOX_HEADS, dlogf * _sigmoid(-z), 0.0)
        dff_ref[...] = dz.astype(BF16)
        dbf_ref[...] += jnp.sum(dz, axis=0, keepdims=True)

    rev = lambda b, i: (b * nblk + (nblk - 1 - i), 0)
    vec = lambda n: pl.BlockSpec((1, n), lambda b, i: (0, 0))
    return pl.pallas_call(
        body, name=name, grid=(nb, nblk),
        in_specs=[pl.BlockSpec((ts, w), lambda b, i: (b * nblk + (nblk - 1 - i), FOX_CB)),
                  pl.BlockSpec((ts, w), lambda b, i: (b * nblk + (nblk - 1 - i), FOX_CB + 1)),
                  pl.BlockSpec((ts, LANES), lambda b, i: (b * nblk + (nblk - 1 - i), CB_FF)),
                  pl.BlockSpec((ts, 2 * w), rev), pl.BlockSpec((ts, 2 * w), rev), pl.BlockSpec((ts, w), rev),
                  pl.BlockSpec((ts, LANES), rev), vec(w), vec(w), vec(LANES)],
        out_specs=[pl.BlockSpec((ts, wo), rev), pl.BlockSpec((ts, LANES), rev), vec(w), vec(w), vec(LANES)],
        out_shape=[jax.ShapeDtypeStruct((m, wo), BF16), jax.ShapeDtypeStruct((m, LANES), BF16),
                   jax.ShapeDtypeStruct((1, w), F32),
                   jax.ShapeDtypeStruct((1, w), F32), jax.ShapeDtypeStruct((1, LANES), F32)],
        scratch_shapes=[pltpu.VMEM((1, LANES), F32)],
        compiler_params=_cp("arbitrary", "arbitrary"),
    )(proj, proj, proj, dqa, dka, dv, dcum, qg, kg, bf)


def _chunk_masks():
    r = lax.broadcasted_iota(jnp.int32, (SEQ_BLOCK, SEQ_BLOCK), 0)
    c = lax.broadcasted_iota(jnp.int32, (SEQ_BLOCK, SEQ_BLOCK), 1)
    same = (r // SUB) == (c // SUB)
    return r, c, same


def _hg_gates(hf, lb):
    sg = _sigmoid(hf)
    f = lb + (1.0 - lb) * sg
    return sg, f, jnp.log(f), (1.0 - lb) * _sigmoid(-hf)


def _hg_intra_e(g_ref, base, t, srow):
    diff = g_ref[pl.ds(base + t, 1), :] - g_ref[pl.ds(base, SUB), :]
    return jnp.exp(jnp.where(srow <= t, diff, NEG))


def _hgrn_fwd(proj, lb, gain, nb, lp, name):
    m = proj.shape[0]
    tb = SEQ_BLOCK
    nblk = lp // tb
    ns = tb // SUB

    def body(q_ref, f_ref, i_ref, g_ref, lb_ref, gain_ref, oraw_ref, y_ref, ssave_ref,
             st_ref, g_scr, kin_scr, o_scr):
        @pl.when(pl.program_id(2) == 0)
        def _():
            st_ref[...] = jnp.zeros_like(st_ref)

        ssave_ref[...] = st_ref[...]
        lbv = lb_ref[...]
        _, _, lf, kin = _hg_gates(f_ref[...], lbv)
        r, c, same = _chunk_masks()
        ltri = jnp.where(same & (c <= r), 1.0, 0.0).astype(F32)
        lall = jnp.where(same, 1.0, 0.0).astype(F32)
        g = _dot(ltri, lf, NN, HI)
        gt = _dot(lall, lf, NN, HI)
        g_scr[...] = g
        kin_scr[...] = kin
        qv = q_ref[...]
        qg = (qv * jnp.exp(g)).astype(BF16)
        kg = (kin * jnp.exp(gt - g)).astype(BF16)
        et = jnp.exp(gt)
        srow = lax.broadcasted_iota(jnp.int32, (SUB, 1), 0)
        subs = [slice(cc * SUB, (cc + 1) * SUB) for cc in range(ns)]
        ups = [_dot(i_ref[sl, :].astype(BF16), kg[sl], TN) for sl in subs]
        st = st_ref[...]
        starts = []
        for cc in range(ns):
            starts.append(st)
            st = et[cc * SUB:cc * SUB + 1, :] * st + ups[cc]
        st_ref[...] = st
        for cc, sl in enumerate(subs):
            base = cc * SUB
            o_c = _dot(qg[sl], starts[cc].astype(BF16), NT)
            kc = kin_scr[sl, :]
            vc = i_ref[sl, :]
            for t in range(SUB):
                e = _hg_intra_e(g_scr, base, t, srow)
                a = jnp.sum((q_ref[pl.ds(base + t, 1), :] * kc) * e, axis=1, keepdims=True)
                ot = jnp.sum(a * vc, axis=0, keepdims=True)
                o_c = o_c + jnp.where(srow == t, ot, 0.0)
            o_scr[sl, :] = o_c
        o = o_scr[...]
        oraw_ref[...] = o
        rr = lax.rsqrt(jnp.mean(o * o, axis=-1, keepdims=True) + EPS)
        hg = g_ref[...]
        y_ref[...] = (((o * rr) * gain_ref[...]) * (hg * _sigmoid(hg))).astype(y_ref.dtype)

    col = lambda cb: pl.BlockSpec((tb, LANES), lambda b, h, i, cb=cb: (b * nblk + i, cb + h))
    out_blk = pl.BlockSpec((tb, LANES), lambda b, h, i: (b * nblk + i, h))
    return pl.pallas_call(
        body, name=name, grid=(nb, HG_HEADS, nblk),
        in_specs=[col(CB_HQ), col(CB_HF), col(CB_HI), col(CB_HG),
                  pl.BlockSpec((1, LANES), lambda b, h, i: (0, h)),
                  pl.BlockSpec((1, LANES), lambda b, h, i: (0, 0))],
        out_specs=[out_blk, out_blk,
                   pl.BlockSpec((None, None, None, HG_DIM, HG_DIM), lambda b, h, i: (b, h, i, 0, 0))],
        out_shape=[jax.ShapeDtypeStruct((m, HG_WIDTH), F32), jax.ShapeDtypeStruct((m, HG_WIDTH), BF16),
                   jax.ShapeDtypeStruct((nb, HG_HEADS, nblk, HG_DIM, HG_DIM), F32)],
        scratch_shapes=[pltpu.VMEM((HG_DIM, HG_DIM), F32), pltpu.VMEM((tb, LANES), F32),
                        pltpu.VMEM((tb, LANES), F32), pltpu.VMEM((tb, LANES), F32)],
        compiler_params=_cp("parallel", "parallel", "arbitrary"),
    )(proj, proj, proj, proj, lb, gain)


def _hgrn_bwd(proj, oraw, ssave, dy, lb, gain, nb, lp, name):
    m = proj.shape[0]
    tb = SEQ_BLOCK
    nblk = lp // tb
    ns = tb // SUB

    def body(q_ref, f_ref, i_ref, g_ref, oraw_ref, ssave_ref, dy_ref, lb_ref, gain_ref,
             dq_ref, df_ref, di_ref, dg_ref, dgain_ref, dlb_ref,
             dst_ref, g_scr, kin_scr, do_scr, dq_scr, dk_scr, dv_scr, dgg_scr):
        hd = pl.program_id(0)
        bb = pl.program_id(1)
        ii = pl.program_id(2)
        gainv = gain_ref[...]
        lbv = lb_ref[...]

        @pl.when((hd == 0) & (bb == 0) & (ii == 0))
        def _():
            dgain_ref[...] = jnp.zeros_like(dgain_ref)

        @pl.when((bb == 0) & (ii == 0))
        def _():
            dlb_ref[...] = jnp.zeros_like(dlb_ref)

        @pl.when(ii == 0)
        def _():
            dst_ref[...] = jnp.zeros_like(dst_ref)

        o = oraw_ref[...]
        rr = lax.rsqrt(jnp.mean(o * o, axis=-1, keepdims=True) + EPS)
        nv = o * rr
        hg = g_ref[...]
        sgg = _sigmoid(hg)
        sil = hg * sgg
        dyv = dy_ref[...]
        dg_ref[...] = (dyv * nv * gainv * (sgg * (1.0 + hg * (1.0 - sgg)))).astype(dg_ref.dtype)
        dgain_ref[...] += jnp.sum(dyv * nv * sil, axis=0, keepdims=True)
        dn = dyv * gainv * sil
        do_scr[...] = rr * (dn - nv * jnp.mean(dn * nv, axis=-1, keepdims=True))

        hf = f_ref[...]
        sg, f, lf, kin = _hg_gates(hf, lbv)
        r, c, same = _chunk_masks()
        ltri = jnp.where(same & (c <= r), 1.0, 0.0).astype(F32)
        lall = jnp.where(same, 1.0, 0.0).astype(F32)
        g = _dot(ltri, lf, NN, HI)
        gt = _dot(lall, lf, NN, HI)
        g_scr[...] = g
        kin_scr[...] = kin
        qv = q_ref[...]
        eg = jnp.exp(g)
        ekg = jnp.exp(gt - g)
        qg = qv * eg
        kg = kin * ekg
        qg16 = qg.astype(BF16)
        kg16 = kg.astype(BF16)
        et = jnp.exp(gt)
        subs = [slice(cc * SUB, (cc + 1) * SUB) for cc in range(ns)]
        ups = [_dot(i_ref[sl, :].astype(BF16), kg16[sl], TN) for sl in subs]
        st = ssave_ref[...]
        starts = []
        for cc in range(ns):
            starts.append(st)
            st = et[cc * SUB:cc * SUB + 1, :] * st + ups[cc]
        do16 = do_scr[...].astype(BF16)
        downs = [_dot(do16[sl], qg16[sl], TN) for sl in subs]
        dst = dst_ref[...]
        afters = [None] * ns
        for cc in reversed(range(ns)):
            afters[cc] = dst
            dst = et[cc * SUB:cc * SUB + 1, :] * dst + downs[cc]
        dst_ref[...] = dst

        srow = lax.broadcasted_iota(jnp.int32, (SUB, 1), 0)
        for cc, sl in enumerate(subs):
            base = cc * SUB
            st = starts[cc]
            st16 = st.astype(BF16)
            dst = afters[cc]
            dst16 = dst.astype(BF16)
            doc16 = do16[sl]
            vc = i_ref[sl, :]
            vc16 = vc.astype(BF16)
            kc = kin_scr[sl, :]
            etc = et[base:base + 1, :]
            dqg = _dot(doc16, st16, NN)
            dv_c = _dot(kg16[sl], dst16, NT)
            dkg = _dot(vc16, dst16, NN)
            dgt = jnp.sum(dst * st, axis=0, keepdims=True) * etc
            dq_c = dqg * eg[sl]
            dk_c = dkg * ekg[sl]
            dg_c = dqg * qg[sl] - dkg * kg[sl]
            dgt = dgt + jnp.sum(dkg * kg[sl], axis=0, keepdims=True)
            for t in range(SUB):
                e = _hg_intra_e(g_scr, base, t, srow)
                qt = q_ref[pl.ds(base + t, 1), :]
                dot_t = do_scr[pl.ds(base + t, 1), :]
                a = jnp.sum((qt * kc) * e, axis=1, keepdims=True)
                da = jnp.sum(dot_t * vc, axis=1, keepdims=True)
                dv_c = dv_c + a * dot_t
                w = da * e
                dq_t = jnp.sum(w * kc, axis=0, keepdims=True)
                wq = w * qt
                dk_c = dk_c + wq
                dg_c = dg_c - kc * wq + jnp.where(srow == t, qt * dq_t, 0.0)
                dq_c = dq_c + jnp.where(srow == t, dq_t, 0.0)
            dg_c = dg_c + jnp.where(srow == SUB - 1, dgt, 0.0)
            dq_scr[sl, :] = dq_c
            dk_scr[sl, :] = dk_c
            dv_scr[sl, :] = dv_c
            dgg_scr[sl, :] = dg_c

        utri = jnp.where(same & (c >= r), 1.0, 0.0).astype(F32)
        dlf = _dot(utri, dgg_scr[...], NN, HI)
        dkin = dk_scr[...]
        dsg = sg * (1.0 - sg)
        df_ref[...] = ((dlf / f - dkin) * ((1.0 - lbv) * dsg)).astype(df_ref.dtype)
        dlb_ref[...] += jnp.sum((dlf / f - dkin) * (1.0 - sg), axis=0, keepdims=True)
        dq_ref[...] = dq_scr[...].astype(dq_ref.dtype)
        di_ref[...] = dv_scr[...].astype(di_ref.dtype)

    rowi = lambda b, i: b * nblk + (nblk - 1 - i)
    col = lambda cb: pl.BlockSpec((tb, LANES), lambda h, b, i, cb=cb: (rowi(b, i), cb + h))
    hblk = pl.BlockSpec((tb, LANES), lambda h, b, i: (rowi(b, i), h))
    return pl.pallas_call(
        body, name=name, grid=(HG_HEADS, nb, nblk),
        in_specs=[col(CB_HQ), col(CB_HF), col(CB_HI), col(CB_HG), hblk,
                  pl.BlockSpec((None, None, None, HG_DIM, HG_DIM), lambda h, b, i: (b, h, nblk - 1 - i, 0, 0)),
                  hblk,
                  pl.BlockSpec((1, LANES), lambda h, b, i: (0, h)),
                  pl.BlockSpec((1, LANES), lambda h, b, i: (0, 0))],
        out_specs=[hblk, hblk, hblk, hblk,
                   pl.BlockSpec((1, LANES), lambda h, b, i: (0, 0)),
                   pl.BlockSpec((1, LANES), lambda h, b, i: (0, h))],
        out_shape=[jax.ShapeDtypeStruct((m, HG_WIDTH), BF16)] * 4
        + [jax.ShapeDtypeStruct((1, LANES), F32), jax.ShapeDtypeStruct((1, HG_WIDTH), F32)],
        scratch_shapes=[pltpu.VMEM((HG_DIM, HG_DIM), F32)] + [pltpu.VMEM((tb, LANES), F32)] * 7,
        compiler_params=_cp("arbitrary", "arbitrary", "arbitrary"),
    )(proj, proj, proj, proj, oraw, ssave, dy, lb, gain)


def _gate_fwd(proj, ya, yb, name):
    m = proj.shape[0]
    tm = _div_tile(m, 256, 16)

    def body(ga_ref, gb_ref, ya_ref, yb_ref, o_ref):
        ya, yb = ya_ref[...].astype(F32), yb_ref[...].astype(F32)
        o_ref[...] = (_sigmoid(ga_ref[...]) * ya + _sigmoid(gb_ref[...]) * yb).astype(o_ref.dtype)

    row = pl.BlockSpec((tm, D_MODEL), lambda i: (i, 0))
    return pl.pallas_call(
        body, name=name, grid=(m // tm,),
        in_specs=[row, pl.BlockSpec((tm, D_MODEL), lambda i: (i, 1)), row, row],
        out_specs=row, out_shape=jax.ShapeDtypeStruct((m, D_MODEL), BF16),
        compiler_params=_cp("parallel"),
    )(proj, proj, ya, yb)


def _gate_bwd(proj, ya, yb, dm, name):
    m = proj.shape[0]
    tm = _div_tile(m, 256, 16)

    def body(ga_ref, gb_ref, ya_ref, yb_ref, dm_ref, dya_ref, dyb_ref, dg_ref):
        dmv = dm_ref[...].astype(F32)
        sa = _sigmoid(ga_ref[...])
        sb = _sigmoid(gb_ref[...])
        dya_ref[...] = (dmv * sa).astype(BF16)
        dyb_ref[...] = (dmv * sb).astype(BF16)
        dg_ref[:, :D_MODEL] = (dmv * ya_ref[...].astype(F32) * (sa * (1.0 - sa))).astype(BF16)
        dg_ref[:, D_MODEL:] = (dmv * yb_ref[...].astype(F32) * (sb * (1.0 - sb))).astype(BF16)

    row = pl.BlockSpec((tm, D_MODEL), lambda i: (i, 0))
    wide = pl.BlockSpec((tm, 2 * D_MODEL), lambda i: (i, 0))
    return pl.pallas_call(
        body, name=name, grid=(m // tm,),
        in_specs=[row, pl.BlockSpec((tm, D_MODEL), lambda i: (i, 1)), row, row, row],
        out_specs=[row, row, wide],
        out_shape=[jax.ShapeDtypeStruct((m, D_MODEL), BF16)] * 2 + [jax.ShapeDtypeStruct((m, 2 * D_MODEL), BF16)],
        compiler_params=_cp("parallel"),
    )(proj, proj, ya, yb, dm)


CONV_ROWS = 128


def _conv3(x, xprev, w_ref, b_ref, rowi):
    r = x.shape[0]
    x1 = jnp.where(rowi < 1, pltpu.roll(xprev, 1, 0), pltpu.roll(x, 1, 0))
    x2 = jnp.where(rowi < 2, pltpu.roll(xprev, 2, 0), pltpu.roll(x, 2, 0))
    u = w_ref[0:1, :] * x2 + w_ref[1:2, :] * x1 + w_ref[2:3, :] * x + b_ref[...]
    return u, x1, x2


def _conv_fwd(up, cw, cb, nb, lp, name):
    m = up.shape[0]
    nct = D_FF // LANES
    r = CONV_ROWS
    nch = lp // r

    def body(u_ref, w_ref, b_ref, o_ref):
        rowi = lax.broadcasted_iota(jnp.int32, (r, 1), 0)

        def step(i, xp):
            r0 = pl.multiple_of(i * r, r)
            xc = u_ref[pl.ds(r0, r), :].astype(F32)
            u, _, _ = _conv3(xc, xp, w_ref, b_ref, rowi)
            ug, uv = u[:, :LANES], u[:, LANES:]
            o_ref[pl.ds(r0, r), :] = ((ug * _sigmoid(ug)) * uv).astype(o_ref.dtype)
            return xc

        lax.fori_loop(0, nch, step, jnp.zeros((r, 2 * LANES), F32))

    return pl.pallas_call(
        body, name=name, grid=(nb, nct),
        in_specs=[pl.BlockSpec((lp, 2 * LANES), lambda b, c: (b, c)),
                  pl.BlockSpec((CONV_WIDTH, 2 * LANES), lambda b, c: (0, c)),
                  pl.BlockSpec((1, 2 * LANES), lambda b, c: (0, c))],
        out_specs=pl.BlockSpec((lp, LANES), lambda b, c: (b, c)),
        out_shape=jax.ShapeDtypeStruct((m, D_FF), BF16),
        compiler_params=_cp("parallel", "parallel"),
    )(up, cw, cb)


def _conv_bwd(up, dact, cw, cb, nb, lp, name):
    m = up.shape[0]
    nct = D_FF // LANES
    r = CONV_ROWS
    nch = lp // r

    def body(u_ref, da_ref, w_ref, b_ref, dup_ref, dw_ref, db_ref):
        rowi = lax.broadcasted_iota(jnp.int32, (r, 1), 0)
        wv = w_ref[...]

        def step(k, carry):
            dun, dw0, dw1, dw2, dbs = carry
            i = nch - 1 - k
            r0 = pl.multiple_of(i * r, r)
            rp = pl.multiple_of(jnp.maximum(i - 1, 0) * r, r)
            xc = u_ref[pl.ds(r0, r), :].astype(F32)
            xp = u_ref[pl.ds(rp, r), :].astype(F32) * (i > 0).astype(F32)
            u, x1, x2 = _conv3(xc, xp, w_ref, b_ref, rowi)
            ug, uv = u[:, :LANES], u[:, LANES:]
            da = da_ref[pl.ds(r0, r), :].astype(F32)
            sg = _sigmoid(ug)
            du = jnp.concatenate([da * uv * (sg * (1.0 + ug * (1.0 - sg))), da * (ug * sg)], axis=1)
            d1 = jnp.where(rowi >= r - 1, pltpu.roll(dun, r - 1, 0), pltpu.roll(du, r - 1, 0))
            d2 = jnp.where(rowi >= r - 2, pltpu.roll(dun, r - 2, 0), pltpu.roll(du, r - 2, 0))
            dup_ref[pl.ds(r0, r), :] = (wv[2:3, :] * du + wv[1:2, :] * d1 + wv[0:1, :] * d2).astype(dup_ref.dtype)
            dw0 = dw0 + jnp.sum(du * x2, axis=0, keepdims=True)
            dw1 = dw1 + jnp.sum(du * x1, axis=0, keepdims=True)
            dw2 = dw2 + jnp.sum(du * xc, axis=0, keepdims=True)
            dbs = dbs + jnp.sum(du, axis=0, keepdims=True)
            return du, dw0, dw1, dw2, dbs

        z1 = jnp.zeros((1, 2 * LANES), F32)
        _, dw0, dw1, dw2, dbs = lax.fori_loop(0, nch, step, (jnp.zeros((r, 2 * LANES), F32), z1, z1, z1, z1))

        @pl.when(pl.program_id(1) == 0)
        def _():
            dw_ref[...] = jnp.zeros_like(dw_ref)
            db_ref[...] = jnp.zeros_like(db_ref)

        dw_ref[0:1, :] += dw0
        dw_ref[1:2, :] += dw1
        dw_ref[2:3, :] += dw2
        db_ref[...] += dbs

    return pl.pallas_call(
        body, name=name, grid=(nct, nb),
        in_specs=[pl.BlockSpec((lp, 2 * LANES), lambda c, b: (b, c)),
                  pl.BlockSpec((lp, LANES), lambda c, b: (b, c)),
                  pl.BlockSpec((CONV_WIDTH, 2 * LANES), lambda c, b: (0, c)),
                  pl.BlockSpec((1, 2 * LANES), lambda c, b: (0, c))],
        out_specs=[pl.BlockSpec((lp, 2 * LANES), lambda c, b: (b, c)),
                   pl.BlockSpec((CONV_WIDTH, 2 * LANES), lambda c, b: (0, c)),
                   pl.BlockSpec((1, 2 * LANES), lambda c, b: (0, c))],
        out_shape=[jax.ShapeDtypeStruct((m, 2 * D_FF), BF16),
                   jax.ShapeDtypeStruct((CONV_WIDTH, 2 * D_FF), F32),
                   jax.ShapeDtypeStruct((1, 2 * D_FF), F32)],
        compiler_params=_cp("parallel", "arbitrary"),
    )(up, dact, cw, cb)


def _ffn_interleave(a, axis):
    shp = a.shape
    a = a.reshape(shp[:axis] + (2, D_FF // LANES, LANES) + shp[axis + 1:])
    return jnp.swapaxes(a, axis, axis + 1).reshape(shp)


def _ffn_deinterleave(a, axis):
    shp = a.shape
    a = a.reshape(shp[:axis] + (D_FF // LANES, 2, LANES) + shp[axis + 1:])
    return jnp.swapaxes(a, axis, axis + 1).reshape(shp)


def _shifted_rows(prev_ref, cur_ref):
    keep = SEQ_BLOCK - N_META
    return jnp.concatenate([prev_ref[keep:, :], cur_ref[:keep, :]], axis=0)


def _frame_specs(nblk, nfb, d):
    prev = pl.BlockSpec((SEQ_BLOCK, d), lambda b, i: (b * nfb + jnp.clip(i - 1, 0, nfb - 1), 0))
    cur = pl.BlockSpec((SEQ_BLOCK, d), lambda b, i: (b * nfb + jnp.clip(i, 0, nfb - 1), 0))
    return prev, cur


def _embed_rms(x2, meta, gain, nb, lp, l, name):
    d = x2.shape[1]
    tr = SEQ_BLOCK
    nblk = lp // tr
    nfb = (l - N_META) // tr
    m = nb * lp

    def body(prev_ref, cur_ref, meta_ref, g_ref, h_ref, o_ref):
        i = pl.program_id(1)
        t = i * tr + lax.broadcasted_iota(jnp.int32, (tr, 1), 0)
        rows = jnp.where(t < l, _shifted_rows(prev_ref, cur_ref), 0.0)
        head = jnp.concatenate([meta_ref[...], jnp.zeros((tr - N_META, d), F32)], axis=0)
        xv = jnp.where(t < N_META, head, rows)
        h_ref[...] = xv
        r = lax.rsqrt(jnp.mean(xv * xv, axis=-1, keepdims=True) + EPS)
        o_ref[...] = ((xv * r) * g_ref[...]).astype(o_ref.dtype)

    prev, cur = _frame_specs(nblk, nfb, d)
    row = pl.BlockSpec((tr, d), lambda b, i: (b * nblk + i, 0))
    return pl.pallas_call(
        body, name=name, grid=(nb, nblk),
        in_specs=[prev, cur, pl.BlockSpec((N_META, d), lambda b, i: (0, 0)), pl.BlockSpec((1, d), lambda b, i: (0, 0))],
        out_specs=[row, row],
        out_shape=[jax.ShapeDtypeStruct((m, d), F32), jax.ShapeDtypeStruct((m, d), BF16)],
        compiler_params=_cp("parallel", "parallel"),
    )(x2, x2, meta, gain)


def _loss_head(out, tgt2, nb, lp, l, name):
    m, d = out.shape
    tr = SEQ_BLOCK
    nblk = lp // tr
    nfb = (l - N_META) // tr

    def body(o_ref, prev_ref, cur_ref, dy_ref, ls_ref):
        t = pl.program_id(1) * tr + lax.broadcasted_iota(jnp.int32, (tr, 1), 0)
        valid = (t >= N_META) & (t < l)
        err = jnp.where(valid, o_ref[...] - _shifted_rows(prev_ref, cur_ref), 0.0)
        dy_ref[...] = err * (1.0 / d)
        part = jnp.sum(err * err, axis=0, keepdims=True)
        first = (pl.program_id(0) == 0) & (pl.program_id(1) == 0)

        @pl.when(first)
        def _():
            ls_ref[...] = part

        @pl.when(jnp.logical_not(first))
        def _():
            ls_ref[...] += part

    prev, cur = _frame_specs(nblk, nfb, d)
    row = pl.BlockSpec((tr, d), lambda b, i: (b * nblk + i, 0))
    return pl.pallas_call(
        body, name=name, grid=(nb, nblk),
        in_specs=[row, prev, cur], out_specs=[row, pl.BlockSpec((1, d), lambda b, i: (0, 0))],
        out_shape=[jax.ShapeDtypeStruct((m, d), F32), jax.ShapeDtypeStruct((1, d), F32)],
        compiler_params=_cp("arbitrary", "arbitrary"),
    )(out, tgt2, tgt2)


def _adam_math(g, w, mom, var):
    c1 = 1.0 - ADAM_B1 ** ADAM_STEP
    c2 = 1.0 - ADAM_B2 ** ADAM_STEP
    mn = ADAM_B1 * mom + (1.0 - ADAM_B1) * g
    vn = ADAM_B2 * var + (1.0 - ADAM_B2) * (g * g)
    delta = -ADAM_LR * ((mn / c1) / (jnp.sqrt(vn / c2) + ADAM_EPS) + ADAM_WD * w)
    return delta, mn, vn


def _slot_sum(recv, name):
    _, r, c = recv.shape
    tc = _div_tile(c, 256, LANES)

    def body(r_ref, g_ref):
        g = r_ref[0].astype(F32)
        for s in range(1, N_DEV):
            g = g + r_ref[s].astype(F32)
        g_ref[...] = g

    return pl.pallas_call(
        body, name=name, grid=(c // tc,),
        in_specs=[pl.BlockSpec((N_DEV, r, tc), lambda j: (0, 0, j))],
        out_specs=pl.BlockSpec((r, tc), lambda j: (0, j)),
        out_shape=jax.ShapeDtypeStruct((r, c), F32),
        compiler_params=_cp("parallel"),
    )(recv)


def _adamw(g, w, mom, var, name):
    r, c = w.shape
    tr = _div_tile(r, 256, 8)

    def body(g_ref, w_ref, m_ref, v_ref, d_ref, mo_ref, vo_ref):
        d_ref[...], mo_ref[...], vo_ref[...] = _adam_math(g_ref[...], w_ref[...], m_ref[...], v_ref[...])

    row = pl.BlockSpec((tr, c), lambda i: (i, 0))
    return pl.pallas_call(
        body, name=name, grid=(r // tr,), in_specs=[row] * 4, out_specs=[row] * 3,
        out_shape=[jax.ShapeDtypeStruct((r, c), F32)] * 3,
        compiler_params=_cp("parallel"),
    )(g, w, mom, var)


def _sum_adamw(recv, w, mom, var, name):
    r, c = w.shape
    tr = _div_tile(r, 256, 8)

    def body(r_ref, w_ref, m_ref, v_ref, g_ref, d_ref, mo_ref, vo_ref):
        g = r_ref[0].astype(F32)
        for s in range(1, N_DEV):
            g = g + r_ref[s].astype(F32)
        g_ref[...] = g
        d_ref[...], mo_ref[...], vo_ref[...] = _adam_math(g, w_ref[...], m_ref[...], v_ref[...])

    row = pl.BlockSpec((tr, c), lambda i: (i, 0))
    return pl.pallas_call(
        body, name=name, grid=(r // tr,),
        in_specs=[pl.BlockSpec((N_DEV, tr, c), lambda i: (0, i, 0)), row, row, row],
        out_specs=[row] * 4,
        out_shape=[jax.ShapeDtypeStruct((r, c), F32)] * 4,
        compiler_params=_cp("parallel"),
    )(recv, w, mom, var)


_MESH = pl.DeviceIdType.MESH
_HBM = pl.BlockSpec(memory_space=pltpu.HBM)
N_PEER = N_DEV - 1


def _position():
    return lax.axis_index("x"), lax.axis_index("y"), lax.axis_index("c")


def _all_gather(shards, name):
    n = len(shards)

    def body(*refs):
        x_refs, out_refs = refs[:n], refs[n:2 * n]
        send_sems, recv_sems, local_sems = refs[2 * n:]
        x, y, c = _position()
        me, sibling = (x, y, c), (x, y, 1 - c)
        chips = [(1 - x, y), (x, 1 - y), (1 - x, 1 - y)]

        def copy(a, k, block, to, src=None):
            slot = out_refs[a].at[4 * block[0] + 2 * block[1] + block[2]]
            return pltpu.make_async_remote_copy(
                src_ref=slot if src is None else src, dst_ref=slot,
                send_sem=send_sems.at[a * N_PEER + k], recv_sem=recv_sems.at[a * N_PEER + k],
                device_id=to, device_id_type=_MESH)

        mine, sent = [], []
        for a in range(n):
            cp = pltpu.make_async_copy(x_refs[a], out_refs[a].at[4 * x + 2 * y + c], local_sems.at[a])
            cp.start()
            mine.append(cp)
            first = [copy(a, 0, me, sibling, src=x_refs[a])]
            first += [copy(a, 1 + j, me, (*chip, c), src=x_refs[a]) for j, chip in enumerate(chips)]
            for cp in first:
                cp.start()
            sent += first
        for a in range(n):
            for j, chip in enumerate(chips):
                copy(a, 1 + j, (*chip, c), me).wait_recv()
                fwd = copy(a, 4 + j, (*chip, c), sibling)
                fwd.start()
                sent.append(fwd)
        for a in range(n):
            copy(a, 0, sibling, me).wait_recv()
            for j, chip in enumerate(chips):
                copy(a, 4 + j, (*chip, 1 - c), me).wait_recv()
        for cp in sent:
            cp.wait_send()
        for cp in mine:
            cp.wait()

    return pl.pallas_call(
        body, name=name,
        out_shape=[jax.ShapeDtypeStruct((N_DEV,) + a.shape, a.dtype) for a in shards],
        in_specs=[_HBM] * n, out_specs=[_HBM] * n,
        scratch_shapes=[pltpu.SemaphoreType.DMA((n * N_PEER,)), pltpu.SemaphoreType.DMA((n * N_PEER,)),
                        pltpu.SemaphoreType.DMA((n,))],
    )(*shards)


_FLIPS = [(fx, fy, fc) for fx in (0, 1) for fy in (0, 1) for fc in (0, 1)][1:]


def _exchange_copies(in_refs, out_refs, nblk, send_sems, recv_sems, local_sems):
    n = len(in_refs)
    x, y, c = _position()
    me = 4 * x + 2 * y + c

    def peer(f):
        return (1 - x if f[0] else x, 1 - y if f[1] else y, 1 - c if f[2] else c)

    def idx(p):
        return 4 * p[0] + 2 * p[1] + p[2]

    def local(a):
        return pltpu.make_async_copy(in_refs[a].at[me] if a < nblk else in_refs[a], out_refs[a].at[me], local_sems.at[a])

    def remote(a, k, sending):
        p = peer(_FLIPS[k])
        src = in_refs[a].at[idx(p)] if a < nblk else in_refs[a]
        dst = out_refs[a].at[me] if sending else out_refs[a].at[idx(p)]
        return pltpu.make_async_remote_copy(
            src_ref=src, dst_ref=dst, send_sem=send_sems.at[a * N_PEER + k], recv_sem=recv_sems.at[a * N_PEER + k],
            device_id=p, device_id_type=_MESH)

    def start():
        for a in range(n):
            local(a).start()
            for k in range(N_PEER):
                remote(a, k, True).start()

    def wait():
        for a in range(n):
            for k in range(N_PEER):
                remote(a, k, False).wait_recv()
        for a in range(n):
            for k in range(N_PEER):
                remote(a, k, True).wait_send()
            local(a).wait()

    return start, wait


def _exchange_io(blocks, shared):
    arrays = list(blocks) + list(shared)
    n = len(arrays)
    out_shape = [jax.ShapeDtypeStruct(a.shape, a.dtype) for a in blocks]
    out_shape += [jax.ShapeDtypeStruct((N_DEV,) + a.shape, a.dtype) for a in shared]
    sems = [pltpu.SemaphoreType.DMA((n * N_PEER,)), pltpu.SemaphoreType.DMA((n * N_PEER,)), pltpu.SemaphoreType.DMA((n,))]
    return arrays, out_shape, sems


def _exchange(blocks, shared, name):
    arrays, out_shape, sems = _exchange_io(blocks, shared)
    n = len(arrays)

    def body(*refs):
        start, wait = _exchange_copies(refs[:n], refs[n:2 * n], len(blocks), *refs[2 * n:])
        start()
        wait()

    return pl.pallas_call(
        body, name=name, out_shape=out_shape, in_specs=[_HBM] * n, out_specs=[_HBM] * n, scratch_shapes=sems,
    )(*arrays)


def _grid_ends(grid):
    ids = [pl.program_id(i) for i in range(len(grid))]
    first = functools.reduce(jnp.logical_and, [i == 0 for i in ids])
    last = functools.reduce(jnp.logical_and, [i == g - 1 for i, g in zip(ids, grid)])
    return first, last


def _pack(parts, rows):
    flat = jnp.concatenate(parts, axis=-1)
    return jnp.pad(flat, [(0, rows * LANES - flat.shape[-1])]).reshape(rows, LANES)


def _unpack(packed, shapes):
    flat = packed.reshape(-1)
    out, off = [], 0
    for shp in shapes:
        n = int(np.prod(shp))
        out.append(flat[off:off + n].reshape(shp))
        off += n
    return out


def _rows_for(shapes, extra=0):
    n = sum(int(np.prod(s)) for s in shapes) + extra
    return -(-n // (8 * LANES)) * 8


def _lower_bound(logits):
    return jnp.cumsum(jax.nn.softmax(logits.astype(F32), axis=0), axis=0)[0:1]


def _align_axis0(w):
    a, b = 3 * FOX_WIDTH, 3 * FOX_WIDTH + FOX_HEADS
    c = b + 4 * HG_WIDTH
    pad = [(0, LANES - FOX_HEADS)] + [(0, 0)] * (w.ndim - 1)
    return jnp.concatenate([w[c:], w[:a], w[b:c], jnp.pad(w[a:b], pad)], axis=0)


def _unalign_axis0(g):
    a, b = 2 * D_MODEL, 2 * D_MODEL + 3 * FOX_WIDTH
    c = b + 4 * HG_WIDTH
    return jnp.concatenate([g[a:b], g[c:c + FOX_HEADS], g[b:c], g[:a]], axis=0)


TINY_COLS = 768


def _tiny_pack(conv_w_shard, meta_shard):
    cw = jnp.pad(conv_w_shard, ((0, 8 - CONV_WIDTH), (0, TINY_COLS - conv_w_shard.shape[1])))
    mt = jnp.pad(meta_shard, ((0, 0), (0, TINY_COLS - meta_shard.shape[1])))
    return jnp.concatenate([cw, mt], axis=0)


def _tiny_unpack(t, ncw, nmeta):
    return t[..., :CONV_WIDTH, :ncw], t[..., 8:8 + N_META, :nmeta]


def _ffn_weights(g_up, g_down):
    d = g_up.shape[-1]
    return _ffn_interleave(g_up.reshape(-1, d), 0), g_down.reshape(-1, d)


def _early_blocks(g_w_up_t, g_w_down, g_w_out, g_w_a_t, g_w_b_t):
    d = g_w_out.shape[-1]
    ab = jnp.stack([g_w_a_t.reshape(N_DEV, -1, g_w_a_t.shape[-1]), g_w_b_t.reshape(N_DEV, -1, g_w_b_t.shape[-1])], axis=1)
    return [_ffn_deinterleave(g_w_up_t, 0).reshape(N_DEV, -1, d).astype(BF16), g_w_down.reshape(N_DEV, -1, d).astype(BF16),
            g_w_out.reshape(N_DEV, -1, d).astype(BF16), ab.astype(BF16)]


def _local_step(x, target, meta, norm1_gain, w_in_t, fox_b_f, q_gain, k_gain, lb, hg_out_gain, w_a_t, w_b_t, w_out,
                norm2_gain, w_up_t, conv_w, conv_b, w_down, ffn_shards=None):
    nb, seq, d = x.shape
    assert seq % SEQ_BLOCK == 0 and N_META < SEQ_BLOCK
    l = seq + N_META
    lp = -(-l // SEQ_BLOCK) * SEQ_BLOCK
    m = nb * lp
    qg = jnp.tile(q_gain, (1, FOX_HEADS))
    kg = jnp.tile(k_gain, (1, FOX_HEADS))
    bf = jnp.pad(fox_b_f, ((0, 0), (0, LANES - FOX_HEADS)))

    h0, xn = _embed_rms(x.reshape(nb * seq, d), meta, norm1_gain, nb, lp, l, "embed_rms1")
    proj = _matmul(xn, w_in_t, "nt", F32, "proj_in")
    qa, ka, vb = _fox_prep(proj, qg, kg, bf, nb, lp, "fox_prep")
    if ffn_shards is None:
        o_fox, lse = _fox_fwd(qa, ka, vb, nb, lp, "fox_fwd")
    else:
        o_fox, lse, g_up, g_down = _fox_fwd(qa, ka, vb, nb, lp, "fox_fwd", ride=ffn_shards)
        w_up_t, w_down = _ffn_weights(g_up, g_down)
    o_raw, o_hg, s_save = _hgrn_fwd(proj, lb, hg_out_gain, nb, lp, "hgrn_fwd")
    ya = _matmul(o_hg, w_a_t, "nt", BF16, "branch_a")
    yb = _matmul(o_fox, w_b_t, "nt", BF16, "branch_b")
    merged = _gate_fwd(proj, ya, yb, "gate_fwd")
    h1 = _matmul(merged, w_out, "nn", F32, "mix_out", residual=h0)
    hn = _rms_fwd(h1, norm2_gain, "rms2_fwd")
    up = _matmul(hn, w_up_t, "nt", BF16, "ffn_up")
    act = _conv_fwd(up, conv_w, conv_b, nb, lp, "conv_fwd")
    out = _matmul(act, w_down, "nn", F32, "ffn_down", residual=h1)
    dy, lsum = _loss_head(out, target.reshape(nb * seq, d), nb, lp, l, "loss_head")
    loss = (0.5 / d) * jnp.sum(lsum)

    dact = _matmul(dy, w_down, "nt", BF16, "d_act")
    g_w_down = _matmul(act, dy, "tn", F32, "g_w_down")
    dup, g_conv_w, g_conv_b = _conv_bwd(up, dact, conv_w, conv_b, nb, lp, "conv_bwd")
    dhn = _matmul(dup, w_up_t, "nn", F32, "d_hn")
    g_w_up_t = _matmul(dup, hn, "tn", F32, "g_w_up")
    dh1, g_norm2 = _rms_bwd(h1, norm2_gain, dhn, dy, "rms2_bwd")

    dmerged = _matmul(dh1, w_out, "nt", BF16, "d_merged")
    g_w_out = _matmul(merged, dh1, "tn", F32, "g_w_out")
    dya, dyb, dgab = _gate_bwd(proj, ya, yb, dmerged, "gate_bwd")
    do_hg = _matmul(dya, w_a_t, "nn", F32, "d_o_hg")
    g_w_a_t = _matmul(dya, o_hg, "tn", F32, "g_w_a")
    do_fox = _matmul(dyb, w_b_t, "nn", BF16, "d_o_fox")
    g_w_b_t = _matmul(dyb, o_fox, "tn", F32, "g_w_b")
    dhq, dhf, dhi, dhg, g_hg_gain, g_lb = _hgrn_bwd(proj, o_raw, s_save, do_hg, lb, hg_out_gain, nb, lp, "hgrn_bwd")
    if ffn_shards is None:
        dqs, dkn, dvv, dc0, dc1 = _fox_bwd(qa, ka, vb, do_fox, o_fox, lse, nb, lp, "fox_bwd")
        early = None
    else:
        dqs, dkn, dvv, dc0, dc1, *early = _fox_bwd(qa, ka, vb, do_fox, o_fox, lse, nb, lp, "fox_bwd",
                                                   ride=_early_blocks(g_w_up_t, g_w_down, g_w_out, g_w_a_t, g_w_b_t))
    dcum = jnp.stack([dc0, dc1], axis=2).reshape(nb, FOX_HEADS, lp)
    dcum = jnp.pad(jnp.transpose(dcum, (0, 2, 1)), ((0, 0), (0, 0), (0, LANES - FOX_HEADS))).reshape(m, LANES)
    dfqkv, dff, g_qg, g_kg, g_bf = _fox_prep_bwd(proj, dqs, dkn, dvv, dcum, qg, kg, bf, nb, lp, "fox_prep_bwd")
    dproj = jnp.concatenate([dgab, dfqkv, dhq, dhf, dhi, dhg, dff], axis=1)
    g_w_in_t = _matmul(dproj, xn, "tn", F32, "g_w_in")
    if ffn_shards is None:
        dxn = _matmul(dproj, w_in_t, "nn", F32, "d_xn")
    else:
        blocks_in = _unalign_axis0(g_w_in_t).reshape(N_DEV, -1, d).astype(BF16)
        dxn, r_in = _matmul(dproj, w_in_t, "nn", F32, "d_xn", ride=[blocks_in])
        early = early + [r_in]
    dh0, g_norm1 = _rms_bwd(h0, norm1_gain, dxn, dh1, "rms1_bwd")

    dh0 = dh0.reshape(nb, lp, d)
    grad_x = dh0[:, N_META:l]
    g_meta = jnp.sum(dh0[:, :N_META], axis=0)
    g_q_gain = jnp.sum(g_qg.reshape(FOX_HEADS, FOX_HEAD_DIM), axis=0, keepdims=True)
    g_k_gain = jnp.sum(g_kg.reshape(FOX_HEADS, FOX_HEAD_DIM), axis=0, keepdims=True)
    grads = dict(meta_tokens=g_meta, norm1_gain=g_norm1, w_in_t=g_w_in_t, fox_b_f=g_bf[:, :FOX_HEADS],
                 q_norm_gain=g_q_gain, k_norm_gain=g_k_gain, lb=g_lb, hg_out_gain=g_hg_gain,
                 w_a_t=g_w_a_t, w_b_t=g_w_b_t, w_out=g_w_out, norm2_gain=g_norm2, w_up_t=g_w_up_t,
                 conv_w=g_conv_w, conv_b=g_conv_b, w_down=g_w_down, early=early)
    return loss, grad_x, grads


SMALL = ("norm1_gain", "fox_b_f", "q_norm_gain", "k_norm_gain", "hg_lb_logits", "hg_out_gain", "norm2_gain", "conv_b")
ORDER = ("meta_tokens", "norm1_gain", "w_in", "fox_b_f", "q_norm_gain", "k_norm_gain", "hg_lb_logits", "hg_out_gain",
         "w_branch_a", "w_branch_b", "w_out", "norm2_gain", "w_up", "conv_w", "conv_b", "w_down")


def kernel(x, meta_tokens, norm1_gain, w_in, fox_b_f, q_norm_gain, k_norm_gain, hg_lb_logits, hg_out_gain, w_branch_a, w_branch_b, w_out, norm2_gain, w_up, conv_w, conv_b, w_down, loss_target, m_meta_tokens, m_norm1_gain, m_w_in, m_fox_b_f, m_q_norm_gain, m_k_norm_gain, m_hg_lb_logits, m_hg_out_gain, m_w_branch_a, m_w_branch_b, m_w_out, m_norm2_gain, m_w_up, m_conv_w, m_conv_b, m_w_down, v_meta_tokens, v_norm1_gain, v_w_in, v_fox_b_f, v_q_norm_gain, v_k_norm_gain, v_hg_lb_logits, v_hg_out_gain, v_w_branch_a, v_w_branch_b, v_w_out, v_norm2_gain, v_w_up, v_conv_w, v_conv_b, v_w_down):
    w = dict(meta_tokens=meta_tokens, norm1_gain=norm1_gain, w_in=w_in, fox_b_f=fox_b_f, q_norm_gain=q_norm_gain,
             k_norm_gain=k_norm_gain, hg_lb_logits=hg_lb_logits, hg_out_gain=hg_out_gain, w_branch_a=w_branch_a,
             w_branch_b=w_branch_b, w_out=w_out, norm2_gain=norm2_gain, w_up=w_up, conv_w=conv_w, conv_b=conv_b,
             w_down=w_down)
    mom = dict(meta_tokens=m_meta_tokens, norm1_gain=m_norm1_gain, w_in=m_w_in, fox_b_f=m_fox_b_f,
               q_norm_gain=m_q_norm_gain, k_norm_gain=m_k_norm_gain, hg_lb_logits=m_hg_lb_logits,
               hg_out_gain=m_hg_out_gain, w_branch_a=m_w_branch_a, w_branch_b=m_w_branch_b, w_out=m_w_out,
               norm2_gain=m_norm2_gain, w_up=m_w_up, conv_w=m_conv_w, conv_b=m_conv_b, w_down=m_w_down)
    var = dict(meta_tokens=v_meta_tokens, norm1_gain=v_norm1_gain, w_in=v_w_in, fox_b_f=v_fox_b_f,
               q_norm_gain=v_q_norm_gain, k_norm_gain=v_k_norm_gain, hg_lb_logits=v_hg_lb_logits,
               hg_out_gain=v_hg_out_gain, w_branch_a=v_w_branch_a, w_branch_b=v_w_branch_b, w_out=v_w_out,
               norm2_gain=v_norm2_gain, w_up=v_w_up, conv_w=v_conv_w, conv_b=v_conv_b, w_down=v_w_down)
    d = D_MODEL
    n_in, n_up = w_in.shape[2], w_up.shape[2]
    n_ab, n_meta = w_branch_a.shape[2], meta_tokens.shape[1]

    shards = [w_in[0].T.astype(BF16),
              jnp.stack([w_branch_a[0].T, w_branch_b[0].T]).astype(BF16),
              w_out[0].astype(BF16),
              _tiny_pack(conv_w[0], meta_tokens)]
    g_in, g_ab, g_out, g_tiny = _all_gather(shards, "gather_weights")
    w_in_t = _align_axis0(g_in.reshape(N_DEV * n_in, d))
    w_a_t = g_ab[:, 0].reshape(N_DEV * n_ab, -1)
    w_b_t = g_ab[:, 1].reshape(N_DEV * n_ab, -1)
    cw_slots, meta_slots = _tiny_unpack(g_tiny, n_up, n_meta)
    conv_w_f = _ffn_interleave(jnp.transpose(cw_slots, (1, 0, 2)).reshape(CONV_WIDTH, -1), 1)
    meta_f = jnp.transpose(meta_slots, (1, 0, 2)).reshape(N_META, -1)
    conv_b_i = _ffn_interleave(conv_b, 1)

    lb, lb_vjp = jax.vjp(_lower_bound, hg_lb_logits)
    loss, grad_x, g = _local_step(
        x, loss_target, meta_f, norm1_gain, w_in_t, fox_b_f, q_norm_gain, k_norm_gain, lb, hg_out_gain,
        w_a_t, w_b_t, g_out.reshape(d, d), norm2_gain, None, conv_w_f, conv_b_i, None,
        ffn_shards=(w_up[0].T.astype(BF16), w_down[0].astype(BF16)))

    g["hg_lb_logits"] = lb_vjp(g.pop("lb"))[0]
    g["conv_b"] = _ffn_deinterleave(g["conv_b"], 1)
    gcw = _ffn_deinterleave(g["conv_w"], 1).reshape(CONV_WIDTH, N_DEV, n_up)
    gmeta = g["meta_tokens"].reshape(N_META, N_DEV, n_meta)
    tiny = jnp.concatenate([
        jnp.pad(jnp.transpose(gcw, (1, 0, 2)), ((0, 0), (0, 8 - CONV_WIDTH), (0, TINY_COLS - n_up))),
        jnp.pad(jnp.transpose(gmeta, (1, 0, 2)), ((0, 0), (0, 0), (0, TINY_COLS - n_meta)))], axis=1)
    small_shapes = [w[n].shape for n in SMALL]
    rows_sm = _rows_for(small_shapes, extra=1)
    small = _pack([g[n].reshape(-1) for n in SMALL] + [loss.reshape(1)], rows_sm)
    r_tiny, r_small = _exchange([tiny], [small], "exchange_grads")
    r_up, r_down, r_out, r_ab, r_in = g["early"]

    res = {}
    g_in_s = _slot_sum(r_in, "sum_w_in").T
    res["w_in"] = (g_in_s,) + tuple(_adamw(g_in_s, w_in[0], m_w_in[0], v_w_in[0], "adamw_w_in"))
    g_up_s = _slot_sum(r_up, "sum_w_up").T
    res["w_up"] = (g_up_s,) + tuple(_adamw(g_up_s, w_up[0], m_w_up[0], v_w_up[0], "adamw_w_up"))
    g_ab_s = jnp.swapaxes(_slot_sum(r_ab.reshape(N_DEV, 2 * n_ab, -1), "sum_w_ab").reshape(2, n_ab, -1), 1, 2)
    ab = lambda t: jnp.concatenate([t["w_branch_a"][0], t["w_branch_b"][0]], axis=0)
    o_ab = (g_ab_s.reshape(-1, n_ab),) + tuple(_adamw(g_ab_s.reshape(-1, n_ab), ab(w), ab(mom), ab(var), "adamw_w_ab"))
    half = o_ab[0].shape[0] // 2
    res["w_branch_a"] = tuple(o[:half] for o in o_ab)
    res["w_branch_b"] = tuple(o[half:] for o in o_ab)
    res["w_out"] = tuple(_sum_adamw(r_out, w_out[0], m_w_out[0], v_w_out[0], "adamw_w_out"))
    res["w_down"] = tuple(_sum_adamw(r_down, w_down[0], m_w_down[0], v_w_down[0], "adamw_w_down"))
    tp = lambda t: _tiny_pack(t["conv_w"][0], t["meta_tokens"])
    o_tiny = [_tiny_unpack(o, n_up, n_meta) for o in _sum_adamw(r_tiny, tp(w), tp(mom), tp(var), "adamw_tiny")]
    res["conv_w"] = tuple(o[0] for o in o_tiny)
    res["meta_tokens"] = tuple(o[1] for o in o_tiny)
    zero1 = jnp.zeros((1,), F32)
    sp = lambda t: _pack([t[n].reshape(-1) for n in SMALL] + [zero1], rows_sm)
    o_small = [_unpack(o, small_shapes + [(1,)]) for o in _sum_adamw(r_small, sp(w), sp(mom), sp(var), "adamw_small")]
    for i, n in enumerate(SMALL):
        res[n] = tuple(o[i] for o in o_small)
    loss_all = o_small[0][len(SMALL)].reshape(())

    result = [[res[n][k].reshape(w[n].shape) for n in ORDER] for k in range(4)]
    return (loss_all, grad_x, *result[0], *result[1], *result[2], *result[3])
```

```python
import functools

import jax
import jax.numpy as jnp
import numpy as np
from jax import lax
from jax.experimental import pallas as pl
from jax.experimental.pallas import tpu as pltpu

F32 = jnp.float32
BF16 = jnp.bfloat16

D_MODEL = 1024
N_META = 16
FOX_HEADS = 8
FOX_HEAD_DIM = 64
FOX_WIDTH = FOX_HEADS * FOX_HEAD_DIM
HG_HEADS = 4
HG_DIM = 128
HG_WIDTH = HG_HEADS * HG_DIM
D_FF = 2816
CONV_WIDTH = 3
EPS = 1e-6
IN_COLS = 3 * FOX_WIDTH + FOX_HEADS + 4 * HG_WIDTH + 2 * D_MODEL
N_DEV = 8

ADAM_LR = 0.001
ADAM_B1 = 0.9
ADAM_B2 = 0.999
ADAM_EPS = 1e-08
ADAM_WD = 0.01
ADAM_STEP = 10

LANES = 128
SEQ_BLOCK = 128
SUB = 16
NEG = -1e30
VMEM_LIMIT = 48 * 1024 * 1024

FOX_CB = 2 * D_MODEL // FOX_WIDTH
CB_HQ = (2 * D_MODEL + 3 * FOX_WIDTH) // LANES
CB_HF = CB_HQ + HG_HEADS
CB_HI = CB_HF + HG_HEADS
CB_HG = CB_HI + HG_HEADS
CB_FF = CB_HG + HG_HEADS


def _div_tile(n, target, mult):
    best = None
    for t in range(mult, min(n, target) + 1, mult):
        if n % t == 0:
            best = t
    if best is None:
        best = n
    return best


def _cp(*sem):
    return pltpu.CompilerParams(dimension_semantics=sem, vmem_limit_bytes=VMEM_LIMIT)


def _sigmoid(x):
    return 0.5 * jnp.tanh(0.5 * x) + 0.5


def _dot(a, b, dims, precision=None):
    return lax.dot_general(a, b, (dims, ((), ())), preferred_element_type=F32, precision=precision)


NN = ((1,), (0,))
NT = ((1,), (1,))
TN = ((0,), (0,))
HI = lax.Precision.HIGHEST


MATMUL_VMEM_BUDGET = 30 * 1024 * 1024
MATMUL_MAX_TILE = 2048


def _tile_options(n):
    return [t for t in range(LANES, min(n, MATMUL_MAX_TILE) + 1, LANES) if n % t == 0] or [n]


def _matmul_tiles(m, n, k, a_bytes, b_bytes, o_bytes, has_res):
    tk = _div_tile(k, MATMUL_MAX_TILE, LANES)
    best = None
    for tm in _tile_options(m):
        for tn in _tile_options(n):
            vmem = 2 * (tm * tk * a_bytes + tk * tn * b_bytes) + 2 * tm * tn * o_bytes
            vmem += tm * tn * 4 if (tk < k and o_bytes != 4) else 0
            vmem += 2 * tm * tn * 4 if has_res else 0
            if vmem > MATMUL_VMEM_BUDGET:
                continue
            key = (tm * tn, tn % 256 == 0, tn)
            if best is None or key > best[0]:
                best = (key, tm, tn)
    assert best is not None, (m, n, k)
    return best[1], best[2], tk


def _matmul(a, b, mode, out_dtype, name, residual=None, ride=()):
    if mode == "nn":
        (m, k), (k2, n) = a.shape, b.shape
    elif mode == "nt":
        (m, k), (n, k2) = a.shape, b.shape
    else:
        (k, m), (k2, n) = a.shape, b.shape
    assert k == k2, (a.shape, b.shape, mode)
    has_res = residual is not None
    tm, tn, tk = _matmul_tiles(m, n, k, a.dtype.itemsize, b.dtype.itemsize, jnp.dtype(out_dtype).itemsize, has_res)
    nk = k // tk
    in_place = jnp.dtype(out_dtype) == jnp.dtype(F32)
    if mode == "nn":
        a_spec = pl.BlockSpec((tm, tk), lambda i, j, kk: (i, kk))
        b_spec = pl.BlockSpec((tk, tn), lambda i, j, kk: (kk, j))
        dims = NN
    elif mode == "nt":
        a_spec = pl.BlockSpec((tm, tk), lambda i, j, kk: (i, kk))
        b_spec = pl.BlockSpec((tn, tk), lambda i, j, kk: (j, kk))
        dims = NT
    else:
        a_spec = pl.BlockSpec((tk, tm), lambda i, j, kk: (kk, i))
        b_spec = pl.BlockSpec((tk, tn), lambda i, j, kk: (kk, j))
        dims = TN
    o_spec = pl.BlockSpec((tm, tn), lambda i, j, kk: (i, j))
    grid = (m // tm, n // tn, nk)
    x_arrays, x_shapes, x_sems = _exchange_io(ride, ())
    nx = len(x_arrays)
    n_in = 3 if has_res else 2

    def body(*refs):
        if nx:
            first, last = _grid_ends(grid)
            x_in, x_out = refs[n_in:n_in + nx], refs[n_in + nx + 1:n_in + 2 * nx + 1]
            start, wait = _exchange_copies(x_in, x_out, nx, *refs[n_in + 2 * nx + 1:n_in + 2 * nx + 4])
            pl.when(first)(start)
        compute(*refs)
        if nx:
            pl.when(last)(wait)

    def compute(*refs):
        a_ref, b_ref = refs[0], refs[1]
        r_ref = refs[2] if has_res else None
        o_ref = refs[n_in + nx]
        if nk == 1:
            part = _dot(a_ref[...].astype(BF16), b_ref[...].astype(BF16), dims)
            o_ref[...] = (part + r_ref[...] if has_res else part).astype(o_ref.dtype)
            return
        acc_ref = o_ref if in_place else refs[-1]
        kk = pl.program_id(2)

        @pl.when(kk == 0)
        def _():
            acc_ref[...] = r_ref[...] if (has_res and in_place) else jnp.zeros_like(acc_ref)

        acc_ref[...] += _dot(a_ref[...].astype(BF16), b_ref[...].astype(BF16), dims)

        if not in_place:
            @pl.when(kk == nk - 1)
            def _():
                acc = acc_ref[...]
                if has_res:
                    acc = acc + r_ref[...]
                o_ref[...] = acc.astype(o_ref.dtype)

    in_specs = [a_spec, b_spec] + ([o_spec] if has_res else [])
    args = (a, b) + ((residual,) if has_res else ())
    out_shape = jax.ShapeDtypeStruct((m, n), out_dtype)
    acc = [pltpu.VMEM((tm, tn), F32)] if (nk > 1 and not in_place) else []
    if not nx:
        return pl.pallas_call(
            body, name=name, grid=grid, in_specs=in_specs, out_specs=o_spec, out_shape=out_shape, scratch_shapes=acc,
            compiler_params=_cp("parallel", "parallel", "arbitrary"),
        )(*args)
    return pl.pallas_call(
        body, name=name, grid=grid, in_specs=in_specs + [_HBM] * nx, out_specs=[o_spec] + [_HBM] * nx,
        out_shape=[out_shape] + x_shapes, scratch_shapes=x_sems + acc,
        compiler_params=_cp("arbitrary", "arbitrary", "arbitrary"),
    )(*args, *x_arrays)


def _rms_fwd(x, gain, name):
    m, d = x.shape
    tm = _div_tile(m, 512, 16)

    def body(x_ref, g_ref, o_ref):
        xv = x_ref[...]
        r = lax.rsqrt(jnp.mean(xv * xv, axis=-1, keepdims=True) + EPS)
        o_ref[...] = ((xv * r) * g_ref[...]).astype(o_ref.dtype)

    return pl.pallas_call(
        body, name=name, grid=(m // tm,),
        in_specs=[pl.BlockSpec((tm, d), lambda i: (i, 0)), pl.BlockSpec((1, d), lambda i: (0, 0))],
        out_specs=pl.BlockSpec((tm, d), lambda i: (i, 0)),
        out_shape=jax.ShapeDtypeStruct((m, d), BF16),
        compiler_params=_cp("parallel"),
    )(x, gain)


def _rms_bwd(x, gain, dy, dres, name):
    m, d = x.shape
    tm = _div_tile(m, 256, 8)

    def body(x_ref, g_ref, dy_ref, dr_ref, dx_ref, dg_ref):
        xv = x_ref[...]
        r = lax.rsqrt(jnp.mean(xv * xv, axis=-1, keepdims=True) + EPS)
        nv = xv * r
        dyv = dy_ref[...]
        gdy = dyv * g_ref[...]
        dx_ref[...] = dr_ref[...] + r * (gdy - nv * jnp.mean(gdy * nv, axis=-1, keepdims=True))
        part = jnp.sum(dyv * nv, axis=0, keepdims=True)

        @pl.when(pl.program_id(0) == 0)
        def _():
            dg_ref[...] = part

        @pl.when(pl.program_id(0) > 0)
        def _():
            dg_ref[...] += part

    row = pl.BlockSpec((tm, d), lambda i: (i, 0))
    vec = pl.BlockSpec((1, d), lambda i: (0, 0))
    return pl.pallas_call(
        body, name=name, grid=(m // tm,),
        in_specs=[row, vec, row, row], out_specs=[row, vec],
        out_shape=[jax.ShapeDtypeStruct((m, d), F32), jax.ShapeDtypeStruct((1, d), F32)],
        compiler_params=_cp("arbitrary"),
    )(x, gain, dy, dres)


def _head_stats(xv, lo):
    sq = xv * xv
    s_lo = jnp.sum(jnp.where(lo, sq, 0.0), axis=1, keepdims=True)
    s_hi = jnp.sum(jnp.where(lo, 0.0, sq), axis=1, keepdims=True)
    return jnp.where(lo, s_lo, s_hi) * (1.0 / FOX_HEAD_DIM)


BIAS_LANE = FOX_HEAD_DIM
N_SPLIT = 3


def _split3(c):
    c1 = c.astype(BF16).astype(F32)
    r1 = c - c1
    c2 = r1.astype(BF16).astype(F32)
    c3 = (r1 - c2).astype(BF16).astype(F32)
    return c1, c2, c3


def _fox_prep(proj, qg, kg, bf, nb, lp, name):
    m = proj.shape[0]
    ts = SEQ_BLOCK
    nblk = lp // ts
    scale = FOX_HEAD_DIM ** -0.5

    def body(q_ref, k_ref, v_ref, f_ref, qg_ref, kg_ref, bf_ref, qo_ref, ko_ref, vo_ref, carry_ref):
        lane = lax.broadcasted_iota(jnp.int32, (1, LANES), 1)
        lo = lane < FOX_HEAD_DIM

        @pl.when(pl.program_id(1) == 0)
        def _():
            carry_ref[...] = jnp.zeros_like(carry_ref)

        z = f_ref[...] + bf_ref[...]
        logf = jnp.minimum(z, 0.0) - jnp.log(1.0 + jnp.exp(-jnp.abs(z)))
        logf = jnp.where(lane < FOX_HEADS, logf, 0.0)
        r = lax.broadcasted_iota(jnp.int32, (ts, ts), 0)
        c = lax.broadcasted_iota(jnp.int32, (ts, ts), 1)
        tri = jnp.where(c <= r, 1.0, 0.0).astype(F32)
        cum = _dot(tri, logf, NN, HI) + carry_ref[...]
        carry_ref[...] = cum[ts - 1:ts, :]

        ones = jnp.where((lane >= BIAS_LANE + N_SPLIT) & (lane < BIAS_LANE + 2 * N_SPLIT), 1.0, 0.0)
        ones_k = jnp.where((lane >= BIAS_LANE) & (lane < BIAS_LANE + N_SPLIT), 1.0, 0.0)
        for j in range(FOX_WIDTH // LANES):
            cs = slice(j * LANES, (j + 1) * LANES)
            xq = q_ref[:, cs]
            yq = ((xq * lax.rsqrt(_head_stats(xq, lo) + EPS)) * qg_ref[:, cs]) * scale
            xk = k_ref[:, cs]
            yk = (xk * lax.rsqrt(_head_stats(xk, lo) + EPS)) * kg_ref[:, cs]
            for hh in range(2):
                h = 2 * j + hh
                pieces = _split3(_lane_pick(cum, lane, h))
                qb, kb = ones, ones_k
                for i, piece in enumerate(pieces):
                    qb = jnp.where(lane == BIAS_LANE + i, piece, qb)
                    kb = jnp.where(lane == BIAS_LANE + N_SPLIT + i, -piece, kb)
                yq_h = yq if hh == 0 else pltpu.roll(yq, FOX_HEAD_DIM, 1)
                yk_h = yk if hh == 0 else pltpu.roll(yk, FOX_HEAD_DIM, 1)
                hs = slice(h * LANES, (h + 1) * LANES)
                qo_ref[:, hs] = jnp.where(lo, yq_h, qb).astype(BF16)
                ko_ref[:, hs] = jnp.where(lo, yk_h, kb).astype(BF16)
        vo_ref[...] = v_ref[...].astype(BF16)

    w = FOX_WIDTH
    row = lambda b, i: (b * nblk + i, 0)
    return pl.pallas_call(
        body, name=name, grid=(nb, nblk),
        in_specs=[pl.BlockSpec((ts, w), lambda b, i: (b * nblk + i, FOX_CB)),
                  pl.BlockSpec((ts, w), lambda b, i: (b * nblk + i, FOX_CB + 1)),
                  pl.BlockSpec((ts, w), lambda b, i: (b * nblk + i, FOX_CB + 2)),
                  pl.BlockSpec((ts, LANES), lambda b, i: (b * nblk + i, CB_FF)),
                  pl.BlockSpec((1, w), lambda b, i: (0, 0)),
                  pl.BlockSpec((1, w), lambda b, i: (0, 0)),
                  pl.BlockSpec((1, LANES), lambda b, i: (0, 0))],
        out_specs=[pl.BlockSpec((ts, 2 * w), row), pl.BlockSpec((ts, 2 * w), row), pl.BlockSpec((ts, w), row)],
        out_shape=[jax.ShapeDtypeStruct((m, 2 * w), BF16)] * 2 + [jax.ShapeDtypeStruct((m, w), BF16)],
        scratch_shapes=[pltpu.VMEM((1, LANES), F32)],
        compiler_params=_cp("arbitrary", "arbitrary"),
    )(proj, proj, proj, proj, qg, kg, bf)


def _att_tile(lp):
    return 384 if (lp % 384 == 0 and lp > 384) else 128


def _lane_pick(blk, lane, idx):
    return jnp.sum(jnp.where(lane == idx, blk, 0.0), axis=1, keepdims=True)


def _head_masks():
    lane = lax.broadcasted_iota(jnp.int32, (1, LANES), 1)
    return lane, [(lane >= hh * FOX_HEAD_DIM) & (lane < (hh + 1) * FOX_HEAD_DIM) for hh in range(2)]


def _fox_fwd(qa, ka, vb, nb, lp, name, ride=()):
    m = qa.shape[0]
    tq = _att_tile(lp)
    nq = lp // tq
    npair = FOX_WIDTH // LANES
    grid = (nb, npair, nq)
    r_arrays, r_shapes, r_sems = _exchange_io((), ride)
    nr = len(r_arrays)

    def body(q_ref, k_ref, v_ref, *rest):
        r_in, (o_ref, lse_ref), r_out, sems = rest[:nr], rest[nr:nr + 2], rest[nr + 2:2 * nr + 2], rest[2 * nr + 2:]
        if nr:
            first, last = _grid_ends(grid)
            start, wait = _exchange_copies(r_in, r_out, 0, *sems)
            pl.when(first)(start)
        qi = pl.program_id(2)
        lane, hmasks = _head_masks()
        zero16 = jnp.zeros((), BF16)
        causal = lax.broadcasted_iota(jnp.int32, (tq, 1), 0) >= lax.broadcasted_iota(jnp.int32, (1, tq), 1)
        o_tot = jnp.zeros((tq, LANES), F32)
        lse_out = jnp.zeros((tq, LANES), F32)
        for hh in range(2):
            hs = slice(hh * LANES, (hh + 1) * LANES)
            q = q_ref[:, hs]

            def tile(j, carry, diagonal, hs=hs, q=q, hmask=hmasks[hh]):
                mx, l, acc = carry
                k0 = pl.multiple_of(j * tq, tq)
                vz = jnp.where(hmask, v_ref[pl.ds(k0, tq), :], zero16)
                s = _dot(q, k_ref[pl.ds(k0, tq), hs], NT)
                if diagonal:
                    s = jnp.where(causal, s, NEG)
                m_new = jnp.maximum(mx, jnp.max(s, axis=1, keepdims=True))
                alpha = jnp.exp(mx - m_new)
                pe = jnp.exp(s - m_new)
                l = alpha * l + jnp.sum(pe, axis=1, keepdims=True)
                acc = alpha * acc + _dot(pe.astype(BF16), vz, NN)
                return m_new, l, acc

            init = (jnp.full((tq, 1), NEG, F32), jnp.zeros((tq, 1), F32), jnp.zeros((tq, LANES), F32))
            carry = lax.fori_loop(0, qi, lambda j, c, tile=tile: tile(j, c, False), init)
            mx, l, acc = tile(qi, carry, True)
            o_tot = o_tot + acc / l
            lse_out = jnp.where(lane == hh, mx + jnp.log(l), lse_out)
        o_ref[...] = o_tot
        lse_ref[...] = lse_out
        if nr:
            pl.when(last)(wait)

    return pl.pallas_call(
        body, name=name, grid=grid,
        in_specs=[pl.BlockSpec((tq, 2 * LANES), lambda b, p, i: (b * nq + i, p)),
                  pl.BlockSpec((lp, 2 * LANES), lambda b, p, i: (b, p)),
                  pl.BlockSpec((lp, LANES), lambda b, p, i: (b, p))] + [_HBM] * nr,
        out_specs=[pl.BlockSpec((tq, LANES), lambda b, p, i: (b * nq + i, p)),
                   pl.BlockSpec((None, None, tq, LANES), lambda b, p, i: (b, p, i, 0))] + [_HBM] * nr,
        out_shape=[jax.ShapeDtypeStruct((m, FOX_WIDTH), F32),
                   jax.ShapeDtypeStruct((nb, npair, lp, LANES), F32)] + r_shapes,
        scratch_shapes=r_sems if nr else [],
        compiler_params=_cp(*(["arbitrary"] * 3 if nr else ["parallel", "parallel", "arbitrary"])),
    )(qa, ka, vb, *r_arrays)


def _fox_bwd(qa, ka, vb, do, o, lse, nb, lp, name, ride=()):
    m = qa.shape[0]
    tq = _att_tile(lp)
    nq = lp // tq
    npair = FOX_WIDTH // LANES
    grid = (nb, npair, nq)
    r_arrays, r_shapes, r_sems = _exchange_io(ride, ())
    nr = len(r_arrays)

    def body(k_ref, v_ref, q_ref, do_ref, o_ref, lse_ref, *rest):
        r_in, r_out, sems = rest[:nr], rest[nr + 5:2 * nr + 5], rest[2 * nr + 5:]
        dq_ref, dk_ref, dv_ref, dc0_ref, dc1_ref = rest[nr:nr + 5]
        if nr:
            first, last = _grid_ends(grid)
            start, wait = _exchange_copies(r_in, r_out, nr, *sems)
            pl.when(first)(start)
        j = pl.program_id(2)
        lane, hmasks = _head_masks()
        zero16 = jnp.zeros((), BF16)
        causal = lax.broadcasted_iota(jnp.int32, (tq, 1), 0) >= lax.broadcasted_iota(jnp.int32, (1, tq), 1)

        @pl.when(j == 0)
        def _():
            dq_ref[...] = jnp.zeros_like(dq_ref)

        vv = v_ref[...]
        vzs = [jnp.where(hm, vv, zero16) for hm in hmasks]

        def tile(qi, carry, diagonal):
            dk0, dk1, dv, dc0, dc1 = carry
            q0 = pl.multiple_of(qi * tq, tq)
            dob16 = do_ref[pl.ds(q0, tq), :].astype(BF16)
            ob = o_ref[pl.ds(q0, tq), :]
            lseb = lse_ref[pl.ds(q0, tq), :]
            dks, dcs = [dk0, dk1], [dc0, dc1]
            for hh in range(2):
                hs = slice(hh * LANES, (hh + 1) * LANES)
                q = q_ref[pl.ds(q0, tq), hs]
                doz16 = jnp.where(hmasks[hh], dob16, zero16)
                delta = jnp.sum(doz16.astype(F32) * ob, axis=1, keepdims=True)
                s = _dot(q, k_ref[:, hs], NT) - _lane_pick(lseb, lane, hh)
                if diagonal:
                    s = jnp.where(causal, s, NEG)
                pm = jnp.exp(s)
                ds = pm * (_dot(doz16, vzs[hh], NT) - delta)
                ds16 = ds.astype(BF16)
                dv = dv + _dot(pm.astype(BF16), doz16, TN)
                dks[hh] = dks[hh] + _dot(ds16, q, TN)
                dq_ref[pl.ds(q0, tq), hs] += _dot(ds16, k_ref[:, hs], NN)
                dcs[hh] = dcs[hh] - jnp.sum(ds, axis=0, keepdims=True)
            return dks[0], dks[1], dv, dcs[0], dcs[1]

        zt = jnp.zeros((tq, LANES), F32)
        zr = jnp.zeros((1, tq), F32)
        carry = tile(j, (zt, zt, zt, zr, zr), True)
        dk0, dk1, dv, dc0, dc1 = lax.fori_loop(j + 1, nq, lambda qi, c: tile(qi, c, False), carry)
        dk_ref[:, :LANES] = dk0
        dk_ref[:, LANES:] = dk1
        dv_ref[...] = dv
        dc0_ref[...] = dc0
        dc1_ref[...] = dc1
        if nr:
            pl.when(last)(wait)

    full2 = pl.BlockSpec((lp, 2 * LANES), lambda b, p, j: (b, p))
    full = pl.BlockSpec((lp, LANES), lambda b, p, j: (b, p))
    blk2 = pl.BlockSpec((tq, 2 * LANES), lambda b, p, j: (b * nq + j, p))
    blk = pl.BlockSpec((tq, LANES), lambda b, p, j: (b * nq + j, p))
    dcs = pl.BlockSpec((None, None, 1, tq), lambda b, p, j: (b, p, 0, j))
    return pl.pallas_call(
        body, name=name, grid=grid,
        in_specs=[blk2, blk, full2, full, full,
                  pl.BlockSpec((None, None, lp, LANES), lambda b, p, j: (b, p, 0, 0))] + [_HBM] * nr,
        out_specs=[full2, blk2, blk, dcs, dcs] + [_HBM] * nr,
        out_shape=[jax.ShapeDtypeStruct((m, 2 * FOX_WIDTH), F32)] * 2 + [jax.ShapeDtypeStruct((m, FOX_WIDTH), F32)]
        + [jax.ShapeDtypeStruct((nb, npair, 1, lp), F32)] * 2 + r_shapes,
        scratch_shapes=r_sems if nr else [],
        compiler_params=_cp(*(["arbitrary"] * 3 if nr else ["parallel", "parallel", "arbitrary"])),
    )(ka, vb, qa, do, o, lse, *r_arrays)


def _fox_prep_bwd(proj, dqa, dka, dv, dcum, qg, kg, bf, nb, lp, name):
    m = proj.shape[0]
    ts = SEQ_BLOCK
    nblk = lp // ts
    scale = FOX_HEAD_DIM ** -0.5
    w = FOX_WIDTH
    wo = 3 * w

    def body(q_ref, k_ref, f_ref, dq_ref, dk_ref, dv_ref, dc_ref, qg_ref, kg_ref, bf_ref,
             out_ref, dff_ref, dqg_ref, dkg_ref, dbf_ref, carry_ref):
        first = (pl.program_id(0) == 0) & (pl.program_id(1) == 0)
        lane = lax.broadcasted_iota(jnp.int32, (1, LANES), 1)
        lo = lane < FOX_HEAD_DIM

        @pl.when(first)
        def _():
            dqg_ref[...] = jnp.zeros_like(dqg_ref)
            dkg_ref[...] = jnp.zeros_like(dkg_ref)
            dbf_ref[...] = jnp.zeros_like(dbf_ref)

        def norm_bwd(x, g, dy):
            r = lax.rsqrt(_head_stats(x, lo) + EPS)
            nv = x * r
            gdy = dy * g
            prod = gdy * nv
            s_lo = jnp.sum(jnp.where(lo, prod, 0.0), axis=1, keepdims=True)
            s_hi = jnp.sum(jnp.where(lo, 0.0, prod), axis=1, keepdims=True)
            mean = jnp.where(lo, s_lo, s_hi) * (1.0 / FOX_HEAD_DIM)
            return r * (gdy - nv * mean), jnp.sum(dy * nv, axis=0, keepdims=True)

        def pair(d_ref, jj):
            even = d_ref[:, 2 * jj * LANES:(2 * jj + 1) * LANES]
            odd = d_ref[:, (2 * jj + 1) * LANES:(2 * jj + 2) * LANES]
            return jnp.where(lo, even, pltpu.roll(odd, FOX_HEAD_DIM, 1))

        for jj in range(w // LANES):
            cs = slice(jj * LANES, (jj + 1) * LANES)
            dx, dg = norm_bwd(q_ref[:, cs], qg_ref[:, cs], pair(dq_ref, jj) * scale)
            out_ref[:, cs] = dx.astype(BF16)
            dqg_ref[:, cs] += dg
            dx, dg = norm_bwd(k_ref[:, cs], kg_ref[:, cs], pair(dk_ref, jj))
            out_ref[:, w + jj * LANES:w + (jj + 1) * LANES] = dx.astype(BF16)
            dkg_ref[:, cs] += dg
        out_ref[:, 2 * w:3 * w] = dv_ref[...].astype(BF16)

        @pl.when(pl.program_id(1) == 0)
        def _():
            carry_ref[...] = jnp.zeros_like(carry_ref)

        dc = dc_ref[...]
        r = lax.broadcasted_iota(jnp.int32, (ts, ts), 0)
        c = lax.broadcasted_iota(jnp.int32, (ts, ts), 1)
        triu = jnp.where(c >= r, 1.0, 0.0).astype(F32)
        dlogf = _dot(triu, dc, NN, HI) + carry_ref[...]
        carry_ref[...] += jnp.sum(dc, axis=0, keepdims=True)
        z = f_ref[...] + bf_ref[...]
        dz = jnp.where(lane < FOX_HEADS, dlogf * _sigmoid(-z), 0.0)
        dff_ref[...] = dz.astype(BF16)
        dbf_ref[...] += jnp.sum(dz, axis=0, keepdims=True)

    rev = lambda b, i: (b * nblk + (nblk - 1 - i), 0)
    vec = lambda n: pl.BlockSpec((1, n), lambda b, i: (0, 0))
    return pl.pallas_call(
        body, name=name, grid=(nb, nblk),
        in_specs=[pl.BlockSpec((ts, w), lambda b, i: (b * nblk + (nblk - 1 - i), FOX_CB)),
                  pl.BlockSpec((ts, w), lambda b, i: (b * nblk + (nblk - 1 - i), FOX_CB + 1)),
                  pl.BlockSpec((ts, LANES), lambda b, i: (b * nblk + (nblk - 1 - i), CB_FF)),
                  pl.BlockSpec((ts, 2 * w), rev), pl.BlockSpec((ts, 2 * w), rev), pl.BlockSpec((ts, w), rev),
                  pl.BlockSpec((ts, LANES), rev), vec(w), vec(w), vec(LANES)],
        out_specs=[pl.BlockSpec((ts, wo), rev), pl.BlockSpec((ts, LANES), rev), vec(w), vec(w), vec(LANES)],
        out_shape=[jax.ShapeDtypeStruct((m, wo), BF16), jax.ShapeDtypeStruct((m, LANES), BF16),
                   jax.ShapeDtypeStruct((1, w), F32),
                   jax.ShapeDtypeStruct((1, w), F32), jax.ShapeDtypeStruct((1, LANES), F32)],
        scratch_shapes=[pltpu.VMEM((1, LANES), F32)],
        compiler_params=_cp("arbitrary", "arbitrary"),
    )(proj, proj, proj, dqa, dka, dv, dcum, qg, kg, bf)


def _chunk_masks():
    r = lax.broadcasted_iota(jnp.int32, (SEQ_BLOCK, SEQ_BLOCK), 0)
    c = lax.broadcasted_iota(jnp.int32, (SEQ_BLOCK, SEQ_BLOCK), 1)
    same = (r // SUB) == (c // SUB)
    return r, c, same


def _hg_gates(hf, lb):
    sg = _sigmoid(hf)
    f = lb + (1.0 - lb) * sg
    return sg, f, jnp.log(f), (1.0 - lb) * _sigmoid(-hf)


def _hg_intra_e(g_ref, base, t, srow):
    diff = g_ref[pl.ds(base + t, 1), :] - g_ref[pl.ds(base, SUB), :]
    return jnp.exp(jnp.where(srow <= t, diff, NEG))


def _hgrn_fwd(proj, lb, gain, nb, lp, name):
    m = proj.shape[0]
    tb = SEQ_BLOCK
    nblk = lp // tb
    ns = tb // SUB

    def body(q_ref, f_ref, i_ref, g_ref, lb_ref, gain_ref, oraw_ref, y_ref, ssave_ref,
             st_ref, g_scr, kin_scr, o_scr):
        @pl.when(pl.program_id(2) == 0)
        def _():
            st_ref[...] = jnp.zeros_like(st_ref)

        ssave_ref[...] = st_ref[...]
        lbv = lb_ref[...]
        _, _, lf, kin = _hg_gates(f_ref[...], lbv)
        r, c, same = _chunk_masks()
        ltri = jnp.where(same & (c <= r), 1.0, 0.0).astype(F32)
        lall = jnp.where(same, 1.0, 0.0).astype(F32)
        g = _dot(ltri, lf, NN, HI)
        gt = _dot(lall, lf, NN, HI)
        g_scr[...] = g
        kin_scr[...] = kin
        qv = q_ref[...]
        qg = (qv * jnp.exp(g)).astype(BF16)
        kg = (kin * jnp.exp(gt - g)).astype(BF16)
        et = jnp.exp(gt)
        srow = lax.broadcasted_iota(jnp.int32, (SUB, 1), 0)
        subs = [slice(cc * SUB, (cc + 1) * SUB) for cc in range(ns)]
        ups = [_dot(i_ref[sl, :].astype(BF16), kg[sl], TN) for sl in subs]
        st = st_ref[...]
        starts = []
        for cc in range(ns):
            starts.append(st)
            st = et[cc * SUB:cc * SUB + 1, :] * st + ups[cc]
        st_ref[...] = st
        for cc, sl in enumerate(subs):
            base = cc * SUB
            kc = kin_scr[sl, :]
            vc = i_ref[sl, :]
            for t in range(SUB):
                e = _hg_intra_e(g_scr, base, t, srow)
                a = jnp.sum((q_ref[pl.ds(base + t, 1), :] * kc) * e, axis=1, keepdims=True)
                o_scr[pl.ds(base + t, 1), :] = jnp.sum(a * vc, axis=0, keepdims=True)
            o_scr[sl, :] += _dot(qg[sl], starts[cc].astype(BF16), NT)
        o = o_scr[...]
        oraw_ref[...] = o
        rr = lax.rsqrt(jnp.mean(o * o, axis=-1, keepdims=True) + EPS)
        hg = g_ref[...]
        y_ref[...] = (((o * rr) * gain_ref[...]) * (hg * _sigmoid(hg))).astype(y_ref.dtype)

    col = lambda cb: pl.BlockSpec((tb, LANES), lambda b, h, i, cb=cb: (b * nblk + i, cb + h))
    out_blk = pl.BlockSpec((tb, LANES), lambda b, h, i: (b * nblk + i, h))
    return pl.pallas_call(
        body, name=name, grid=(nb, HG_HEADS, nblk),
        in_specs=[col(CB_HQ), col(CB_HF), col(CB_HI), col(CB_HG),
                  pl.BlockSpec((1, LANES), lambda b, h, i: (0, h)),
                  pl.BlockSpec((1, LANES), lambda b, h, i: (0, 0))],
        out_specs=[out_blk, out_blk,
                   pl.BlockSpec((None, None, None, HG_DIM, HG_DIM), lambda b, h, i: (b, h, i, 0, 0))],
        out_shape=[jax.ShapeDtypeStruct((m, HG_WIDTH), F32), jax.ShapeDtypeStruct((m, HG_WIDTH), BF16),
                   jax.ShapeDtypeStruct((nb, HG_HEADS, nblk, HG_DIM, HG_DIM), F32)],
        scratch_shapes=[pltpu.VMEM((HG_DIM, HG_DIM), F32), pltpu.VMEM((tb, LANES), F32),
                        pltpu.VMEM((tb, LANES), F32), pltpu.VMEM((tb, LANES), F32)],
        compiler_params=_cp("parallel", "parallel", "arbitrary"),
    )(proj, proj, proj, proj, lb, gain)


def _hgrn_bwd(proj, oraw, ssave, dy, lb, gain, nb, lp, name):
    m = proj.shape[0]
    tb = SEQ_BLOCK
    nblk = lp // tb
    ns = tb // SUB

    def body(q_ref, f_ref, i_ref, g_ref, oraw_ref, ssave_ref, dy_ref, lb_ref, gain_ref,
             dq_ref, df_ref, di_ref, dg_ref, dgain_ref, dlb_ref,
             dst_ref, g_scr, kin_scr, do_scr, dq_scr, dk_scr, dv_scr, dgg_scr):
        hd = pl.program_id(0)
        bb = pl.program_id(1)
        ii = pl.program_id(2)
        gainv = gain_ref[...]
        lbv = lb_ref[...]

        @pl.when((hd == 0) & (bb == 0) & (ii == 0))
        def _():
            dgain_ref[...] = jnp.zeros_like(dgain_ref)

        @pl.when((bb == 0) & (ii == 0))
        def _():
            dlb_ref[...] = jnp.zeros_like(dlb_ref)

        @pl.when(ii == 0)
        def _():
            dst_ref[...] = jnp.zeros_like(dst_ref)

        o = oraw_ref[...]
        rr = lax.rsqrt(jnp.mean(o * o, axis=-1, keepdims=True) + EPS)
        nv = o * rr
        hg = g_ref[...]
        sgg = _sigmoid(hg)
        sil = hg * sgg
        dyv = dy_ref[...]
        dg_ref[...] = (dyv * nv * gainv * (sgg * (1.0 + hg * (1.0 - sgg)))).astype(dg_ref.dtype)
        dgain_ref[...] += jnp.sum(dyv * nv * sil, axis=0, keepdims=True)
        dn = dyv * gainv * sil
        do_scr[...] = rr * (dn - nv * jnp.mean(dn * nv, axis=-1, keepdims=True))

        hf = f_ref[...]
        sg, f, lf, kin = _hg_gates(hf, lbv)
        r, c, same = _chunk_masks()
        ltri = jnp.where(same & (c <= r), 1.0, 0.0).astype(F32)
        lall = jnp.where(same, 1.0, 0.0).astype(F32)
        g = _dot(ltri, lf, NN, HI)
        gt = _dot(lall, lf, NN, HI)
        g_scr[...] = g
        kin_scr[...] = kin
        qv = q_ref[...]
        eg = jnp.exp(g)
        ekg = jnp.exp(gt - g)
        qg = qv * eg
        kg = kin * ekg
        qg16 = qg.astype(BF16)
        kg16 = kg.astype(BF16)
        et = jnp.exp(gt)
        subs = [slice(cc * SUB, (cc + 1) * SUB) for cc in range(ns)]
        ups = [_dot(i_ref[sl, :].astype(BF16), kg16[sl], TN) for sl in subs]
        st = ssave_ref[...]
        starts = []
        for cc in range(ns):
            starts.append(st)
            st = et[cc * SUB:cc * SUB + 1, :] * st + ups[cc]
        do16 = do_scr[...].astype(BF16)
        downs = [_dot(do16[sl], qg16[sl], TN) for sl in subs]
        dst = dst_ref[...]
        afters = [None] * ns
        for cc in reversed(range(ns)):
            afters[cc] = dst
            dst = et[cc * SUB:cc * SUB + 1, :] * dst + downs[cc]
        dst_ref[...] = dst

        srow = lax.broadcasted_iota(jnp.int32, (SUB, 1), 0)
        for cc, sl in enumerate(subs):
            base = cc * SUB
            st = starts[cc]
            st16 = st.astype(BF16)
            dst = afters[cc]
            dst16 = dst.astype(BF16)
            doc16 = do16[sl]
            vc = i_ref[sl, :]
            vc16 = vc.astype(BF16)
            kc = kin_scr[sl, :]
            etc = et[base:base + 1, :]
            dqg = _dot(doc16, st16, NN)
            dv_c = _dot(kg16[sl], dst16, NT)
            dkg = _dot(vc16, dst16, NN)
            dgt = jnp.sum(dst * st, axis=0, keepdims=True) * etc
            dq_c = dqg * eg[sl]
            dk_c = dkg * ekg[sl]
            dg_c = dqg * qg[sl] - dkg * kg[sl]
            dgt = dgt + jnp.sum(dkg * kg[sl], axis=0, keepdims=True)
            for t in range(SUB):
                e = _hg_intra_e(g_scr, base, t, srow)
                qt = q_ref[pl.ds(base + t, 1), :]
                dot_t = do_scr[pl.ds(base + t, 1), :]
                a = jnp.sum((qt * kc) * e, axis=1, keepdims=True)
                da = jnp.sum(dot_t * vc, axis=1, keepdims=True)
                dv_c = dv_c + a * dot_t
                w = da * e
                dq_scr[pl.ds(base + t, 1), :] = jnp.sum(w * kc, axis=0, keepdims=True)
                wq = w * qt
                dk_c = dk_c + wq
                dg_c = dg_c - kc * wq
            dq_i = dq_scr[sl, :]
            dg_c = dg_c + qv[sl] * dq_i + jnp.where(srow == SUB - 1, dgt, 0.0)
            dq_scr[sl, :] = dq_c + dq_i
            dk_scr[sl, :] = dk_c
            dv_scr[sl, :] = dv_c
            dgg_scr[sl, :] = dg_c

        utri = jnp.where(same & (c >= r), 1.0, 0.0).astype(F32)
        dlf = _dot(utri, dgg_scr[...], NN, HI)
        dkin = dk_scr[...]
        dsg = sg * (1.0 - sg)
        df_ref[...] = ((dlf / f - dkin) * ((1.0 - lbv) * dsg)).astype(df_ref.dtype)
        dlb_ref[...] += jnp.sum((dlf / f - dkin) * (1.0 - sg), axis=0, keepdims=True)
        dq_ref[...] = dq_scr[...].astype(dq_ref.dtype)
        di_ref[...] = dv_scr[...].astype(di_ref.dtype)

    rowi = lambda b, i: b * nblk + (nblk - 1 - i)
    col = lambda cb: pl.BlockSpec((tb, LANES), lambda h, b, i, cb=cb: (rowi(b, i), cb + h))
    hblk = pl.BlockSpec((tb, LANES), lambda h, b, i: (rowi(b, i), h))
    return pl.pallas_call(
        body, name=name, grid=(HG_HEADS, nb, nblk),
        in_specs=[col(CB_HQ), col(CB_HF), col(CB_HI), col(CB_HG), hblk,
                  pl.BlockSpec((None, None, None, HG_DIM, HG_DIM), lambda h, b, i: (b, h, nblk - 1 - i, 0, 0)),
                  hblk,
                  pl.BlockSpec((1, LANES), lambda h, b, i: (0, h)),
                  pl.BlockSpec((1, LANES), lambda h, b, i: (0, 0))],
        out_specs=[hblk, hblk, hblk, hblk,
                   pl.BlockSpec((1, LANES), lambda h, b, i: (0, 0)),
                   pl.BlockSpec((1, LANES), lambda h, b, i: (0, h))],
        out_shape=[jax.ShapeDtypeStruct((m, HG_WIDTH), BF16)] * 4
        + [jax.ShapeDtypeStruct((1, LANES), F32), jax.ShapeDtypeStruct((1, HG_WIDTH), F32)],
        scratch_shapes=[pltpu.VMEM((HG_DIM, HG_DIM), F32)] + [pltpu.VMEM((tb, LANES), F32)] * 7,
        compiler_params=_cp("arbitrary", "arbitrary", "arbitrary"),
    )(proj, proj, proj, proj, oraw, ssave, dy, lb, gain)


def _gate_fwd(proj, ya, yb, name):
    m = proj.shape[0]
    tm = _div_tile(m, 256, 16)

    def body(ga_ref, gb_ref, ya_ref, yb_ref, o_ref):
        ya, yb = ya_ref[...].astype(F32), yb_ref[...].astype(F32)
        o_ref[...] = (_sigmoid(ga_ref[...]) * ya + _sigmoid(gb_ref[...]) * yb).astype(o_ref.dtype)

    row = pl.BlockSpec((tm, D_MODEL), lambda i: (i, 0))
    return pl.pallas_call(
        body, name=name, grid=(m // tm,),
        in_specs=[row, pl.BlockSpec((tm, D_MODEL), lambda i: (i, 1)), row, row],
        out_specs=row, out_shape=jax.ShapeDtypeStruct((m, D_MODEL), BF16),
        compiler_params=_cp("parallel"),
    )(proj, proj, ya, yb)


def _gate_bwd(proj, ya, yb, dm, name):
    m = proj.shape[0]
    tm = _div_tile(m, 256, 16)

    def body(ga_ref, gb_ref, ya_ref, yb_ref, dm_ref, dya_ref, dyb_ref, dg_ref):
        dmv = dm_ref[...].astype(F32)
        sa = _sigmoid(ga_ref[...])
        sb = _sigmoid(gb_ref[...])
        dya_ref[...] = (dmv * sa).astype(BF16)
        dyb_ref[...] = (dmv * sb).astype(BF16)
        dg_ref[:, :D_MODEL] = (dmv * ya_ref[...].astype(F32) * (sa * (1.0 - sa))).astype(BF16)
        dg_ref[:, D_MODEL:] = (dmv * yb_ref[...].astype(F32) * (sb * (1.0 - sb))).astype(BF16)

    row = pl.BlockSpec((tm, D_MODEL), lambda i: (i, 0))
    wide = pl.BlockSpec((tm, 2 * D_MODEL), lambda i: (i, 0))
    return pl.pallas_call(
        body, name=name, grid=(m // tm,),
        in_specs=[row, pl.BlockSpec((tm, D_MODEL), lambda i: (i, 1)), row, row, row],
        out_specs=[row, row, wide],
        out_shape=[jax.ShapeDtypeStruct((m, D_MODEL), BF16)] * 2 + [jax.ShapeDtypeStruct((m, 2 * D_MODEL), BF16)],
        compiler_params=_cp("parallel"),
    )(proj, proj, ya, yb, dm)


CONV_ROWS = 128


def _conv3(x, xprev, w_ref, b_ref, rowi):
    r = x.shape[0]
    x1 = jnp.where(rowi < 1, pltpu.roll(xprev, 1, 0), pltpu.roll(x, 1, 0))
    x2 = jnp.where(rowi < 2, pltpu.roll(xprev, 2, 0), pltpu.roll(x, 2, 0))
    u = w_ref[0:1, :] * x2 + w_ref[1:2, :] * x1 + w_ref[2:3, :] * x + b_ref[...]
    return u, x1, x2


def _conv_fwd(up, cw, cb, nb, lp, name):
    m = up.shape[0]
    nct = D_FF // LANES
    r = CONV_ROWS
    nch = lp // r

    def body(u_ref, w_ref, b_ref, o_ref):
        rowi = lax.broadcasted_iota(jnp.int32, (r, 1), 0)

        def step(i, xp):
            r0 = pl.multiple_of(i * r, r)
            xc = u_ref[pl.ds(r0, r), :].astype(F32)
            u, _, _ = _conv3(xc, xp, w_ref, b_ref, rowi)
            ug, uv = u[:, :LANES], u[:, LANES:]
            o_ref[pl.ds(r0, r), :] = ((ug * _sigmoid(ug)) * uv).astype(o_ref.dtype)
            return xc

        lax.fori_loop(0, nch, step, jnp.zeros((r, 2 * LANES), F32))

    return pl.pallas_call(
        body, name=name, grid=(nb, nct),
        in_specs=[pl.BlockSpec((lp, 2 * LANES), lambda b, c: (b, c)),
                  pl.BlockSpec((CONV_WIDTH, 2 * LANES), lambda b, c: (0, c)),
                  pl.BlockSpec((1, 2 * LANES), lambda b, c: (0, c))],
        out_specs=pl.BlockSpec((lp, LANES), lambda b, c: (b, c)),
        out_shape=jax.ShapeDtypeStruct((m, D_FF), BF16),
        compiler_params=_cp("parallel", "parallel"),
    )(up, cw, cb)


def _conv_bwd(up, dact, cw, cb, nb, lp, name):
    m = up.shape[0]
    nct = D_FF // LANES
    r = CONV_ROWS
    nch = lp // r

    def body(u_ref, da_ref, w_ref, b_ref, dup_ref, dw_ref, db_ref):
        rowi = lax.broadcasted_iota(jnp.int32, (r, 1), 0)
        wv = w_ref[...]

        def step(k, carry):
            dun, dw0, dw1, dw2, dbs = carry
            i = nch - 1 - k
            r0 = pl.multiple_of(i * r, r)
            rp = pl.multiple_of(jnp.maximum(i - 1, 0) * r, r)
            xc = u_ref[pl.ds(r0, r), :].astype(F32)
            xp = u_ref[pl.ds(rp, r), :].astype(F32) * (i > 0).astype(F32)
            u, x1, x2 = _conv3(xc, xp, w_ref, b_ref, rowi)
            ug, uv = u[:, :LANES], u[:, LANES:]
            da = da_ref[pl.ds(r0, r), :].astype(F32)
            sg = _sigmoid(ug)
            du = jnp.concatenate([da * uv * (sg * (1.0 + ug * (1.0 - sg))), da * (ug * sg)], axis=1)
            d1 = jnp.where(rowi >= r - 1, pltpu.roll(dun, r - 1, 0), pltpu.roll(du, r - 1, 0))
            d2 = jnp.where(rowi >= r - 2, pltpu.roll(dun, r - 2, 0), pltpu.roll(du, r - 2, 0))
            dup_ref[pl.ds(r0, r), :] = (wv[2:3, :] * du + wv[1:2, :] * d1 + wv[0:1, :] * d2).astype(dup_ref.dtype)
            dw0 = dw0 + jnp.sum(du * x2, axis=0, keepdims=True)
            dw1 = dw1 + jnp.sum(du * x1, axis=0, keepdims=True)
            dw2 = dw2 + jnp.sum(du * xc, axis=0, keepdims=True)
            dbs = dbs + jnp.sum(du, axis=0, keepdims=True)
            return du, dw0, dw1, dw2, dbs

        z1 = jnp.zeros((1, 2 * LANES), F32)
        _, dw0, dw1, dw2, dbs = lax.fori_loop(0, nch, step, (jnp.zeros((r, 2 * LANES), F32), z1, z1, z1, z1))

        @pl.when(pl.program_id(1) == 0)
        def _():
            dw_ref[...] = jnp.zeros_like(dw_ref)
            db_ref[...] = jnp.zeros_like(db_ref)

        dw_ref[0:1, :] += dw0
        dw_ref[1:2, :] += dw1
        dw_ref[2:3, :] += dw2
        db_ref[...] += dbs

    return pl.pallas_call(
        body, name=name, grid=(nct, nb),
        in_specs=[pl.BlockSpec((lp, 2 * LANES), lambda c, b: (b, c)),
                  pl.BlockSpec((lp, LANES), lambda c, b: (b, c)),
                  pl.BlockSpec((CONV_WIDTH, 2 * LANES), lambda c, b: (0, c)),
                  pl.BlockSpec((1, 2 * LANES), lambda c, b: (0, c))],
        out_specs=[pl.BlockSpec((lp, 2 * LANES), lambda c, b: (b, c)),
                   pl.BlockSpec((CONV_WIDTH, 2 * LANES), lambda c, b: (0, c)),
                   pl.BlockSpec((1, 2 * LANES), lambda c, b: (0, c))],
        out_shape=[jax.ShapeDtypeStruct((m, 2 * D_FF), BF16),
                   jax.ShapeDtypeStruct((CONV_WIDTH, 2 * D_FF), F32),
                   jax.ShapeDtypeStruct((1, 2 * D_FF), F32)],
        compiler_params=_cp("parallel", "arbitrary"),
    )(up, dact, cw, cb)


def _ffn_interleave(a, axis):
    shp = a.shape
    a = a.reshape(shp[:axis] + (2, D_FF // LANES, LANES) + shp[axis + 1:])
    return jnp.swapaxes(a, axis, axis + 1).reshape(shp)


def _ffn_deinterleave(a, axis):
    shp = a.shape
    a = a.reshape(shp[:axis] + (D_FF // LANES, 2, LANES) + shp[axis + 1:])
    return jnp.swapaxes(a, axis, axis + 1).reshape(shp)


def _shifted_rows(prev_ref, cur_ref):
    keep = SEQ_BLOCK - N_META
    return jnp.concatenate([prev_ref[keep:, :], cur_ref[:keep, :]], axis=0)


def _frame_specs(nblk, nfb, d):
    prev = pl.BlockSpec((SEQ_BLOCK, d), lambda b, i: (b * nfb + jnp.clip(i - 1, 0, nfb - 1), 0))
    cur = pl.BlockSpec((SEQ_BLOCK, d), lambda b, i: (b * nfb + jnp.clip(i, 0, nfb - 1), 0))
    return prev, cur


def _embed_rms(x2, meta, gain, nb, lp, l, name):
    d = x2.shape[1]
    tr = SEQ_BLOCK
    nblk = lp // tr
    nfb = (l - N_META) // tr
    m = nb * lp

    def body(prev_ref, cur_ref, meta_ref, g_ref, h_ref, o_ref):
        i = pl.program_id(1)
        t = i * tr + lax.broadcasted_iota(jnp.int32, (tr, 1), 0)
        rows = jnp.where(t < l, _shifted_rows(prev_ref, cur_ref), 0.0)
        head = jnp.concatenate([meta_ref[...], jnp.zeros((tr - N_META, d), F32)], axis=0)
        xv = jnp.where(t < N_META, head, rows)
        h_ref[...] = xv
        r = lax.rsqrt(jnp.mean(xv * xv, axis=-1, keepdims=True) + EPS)
        o_ref[...] = ((xv * r) * g_ref[...]).astype(o_ref.dtype)

    prev, cur = _frame_specs(nblk, nfb, d)
    row = pl.BlockSpec((tr, d), lambda b, i: (b * nblk + i, 0))
    return pl.pallas_call(
        body, name=name, grid=(nb, nblk),
        in_specs=[prev, cur, pl.BlockSpec((N_META, d), lambda b, i: (0, 0)), pl.BlockSpec((1, d), lambda b, i: (0, 0))],
        out_specs=[row, row],
        out_shape=[jax.ShapeDtypeStruct((m, d), F32), jax.ShapeDtypeStruct((m, d), BF16)],
        compiler_params=_cp("parallel", "parallel"),
    )(x2, x2, meta, gain)


def _loss_head(out, tgt2, nb, lp, l, name):
    m, d = out.shape
    tr = SEQ_BLOCK
    nblk = lp // tr
    nfb = (l - N_META) // tr

    def body(o_ref, prev_ref, cur_ref, dy_ref, ls_ref):
        t = pl.program_id(1) * tr + lax.broadcasted_iota(jnp.int32, (tr, 1), 0)
        valid = (t >= N_META) & (t < l)
        err = jnp.where(valid, o_ref[...] - _shifted_rows(prev_ref, cur_ref), 0.0)
        dy_ref[...] = err * (1.0 / d)
        part = jnp.sum(err * err, axis=0, keepdims=True)
        first = (pl.program_id(0) == 0) & (pl.program_id(1) == 0)

        @pl.when(first)
        def _():
            ls_ref[...] = part

        @pl.when(jnp.logical_not(first))
        def _():
            ls_ref[...] += part

    prev, cur = _frame_specs(nblk, nfb, d)
    row = pl.BlockSpec((tr, d), lambda b, i: (b * nblk + i, 0))
    return pl.pallas_call(
        body, name=name, grid=(nb, nblk),
        in_specs=[row, prev, cur], out_specs=[row, pl.BlockSpec((1, d), lambda b, i: (0, 0))],
        out_shape=[jax.ShapeDtypeStruct((m, d), F32), jax.ShapeDtypeStruct((1, d), F32)],
        compiler_params=_cp("arbitrary", "arbitrary"),
    )(out, tgt2, tgt2)


def _adam_math(g, w, mom, var):
    c1 = 1.0 - ADAM_B1 ** ADAM_STEP
    c2 = 1.0 - ADAM_B2 ** ADAM_STEP
    mn = ADAM_B1 * mom + (1.0 - ADAM_B1) * g
    vn = ADAM_B2 * var + (1.0 - ADAM_B2) * (g * g)
    delta = -ADAM_LR * ((mn / c1) / (jnp.sqrt(vn / c2) + ADAM_EPS) + ADAM_WD * w)
    return delta, mn, vn


def _slot_sum(recv, name):
    _, r, c = recv.shape
    tc = _div_tile(c, 256, LANES)

    def body(r_ref, g_ref):
        g = r_ref[0].astype(F32)
        for s in range(1, N_DEV):
            g = g + r_ref[s].astype(F32)
        g_ref[...] = g

    return pl.pallas_call(
        body, name=name, grid=(c // tc,),
        in_specs=[pl.BlockSpec((N_DEV, r, tc), lambda j: (0, 0, j))],
        out_specs=pl.BlockSpec((r, tc), lambda j: (0, j)),
        out_shape=jax.ShapeDtypeStruct((r, c), F32),
        compiler_params=_cp("parallel"),
    )(recv)


def _adamw(g, w, mom, var, name):
    r, c = w.shape
    tr = _div_tile(r, 256, 8)

    def body(g_ref, w_ref, m_ref, v_ref, d_ref, mo_ref, vo_ref):
        d_ref[...], mo_ref[...], vo_ref[...] = _adam_math(g_ref[...], w_ref[...], m_ref[...], v_ref[...])

    row = pl.BlockSpec((tr, c), lambda i: (i, 0))
    return pl.pallas_call(
        body, name=name, grid=(r // tr,), in_specs=[row] * 4, out_specs=[row] * 3,
        out_shape=[jax.ShapeDtypeStruct((r, c), F32)] * 3,
        compiler_params=_cp("parallel"),
    )(g, w, mom, var)


def _sum_adamw(recv, w, mom, var, name):
    r, c = w.shape
    tr = _div_tile(r, 256, 8)

    def body(r_ref, w_ref, m_ref, v_ref, g_ref, d_ref, mo_ref, vo_ref):
        g = r_ref[0].astype(F32)
        for s in range(1, N_DEV):
            g = g + r_ref[s].astype(F32)
        g_ref[...] = g
        d_ref[...], mo_ref[...], vo_ref[...] = _adam_math(g, w_ref[...], m_ref[...], v_ref[...])

    row = pl.BlockSpec((tr, c), lambda i: (i, 0))
    return pl.pallas_call(
        body, name=name, grid=(r // tr,),
        in_specs=[pl.BlockSpec((N_DEV, tr, c), lambda i: (0, i, 0)), row, row, row],
        out_specs=[row] * 4,
        out_shape=[jax.ShapeDtypeStruct((r, c), F32)] * 4,
        compiler_params=_cp("parallel"),
    )(recv, w, mom, var)


_MESH = pl.DeviceIdType.MESH
_HBM = pl.BlockSpec(memory_space=pltpu.HBM)
N_PEER = N_DEV - 1


def _position():
    return lax.axis_index("x"), lax.axis_index("y"), lax.axis_index("c")


def _all_gather(shards, name):
    n = len(shards)

    def body(*refs):
        x_refs, out_refs = refs[:n], refs[n:2 * n]
        send_sems, recv_sems, local_sems = refs[2 * n:]
        x, y, c = _position()
        me, sibling = (x, y, c), (x, y, 1 - c)
        chips = [(1 - x, y), (x, 1 - y), (1 - x, 1 - y)]

        def copy(a, k, block, to, src=None):
            slot = out_refs[a].at[4 * block[0] + 2 * block[1] + block[2]]
            return pltpu.make_async_remote_copy(
                src_ref=slot if src is None else src, dst_ref=slot,
                send_sem=send_sems.at[a * N_PEER + k], recv_sem=recv_sems.at[a * N_PEER + k],
                device_id=to, device_id_type=_MESH)

        mine, sent = [], []
        for a in range(n):
            cp = pltpu.make_async_copy(x_refs[a], out_refs[a].at[4 * x + 2 * y + c], local_sems.at[a])
            cp.start()
            mine.append(cp)
            first = [copy(a, 0, me, sibling, src=x_refs[a])]
            first += [copy(a, 1 + j, me, (*chip, c), src=x_refs[a]) for j, chip in enumerate(chips)]
            for cp in first:
                cp.start()
            sent += first
        for a in range(n):
            for j, chip in enumerate(chips):
                copy(a, 1 + j, (*chip, c), me).wait_recv()
                fwd = copy(a, 4 + j, (*chip, c), sibling)
                fwd.start()
                sent.append(fwd)
        for a in range(n):
            copy(a, 0, sibling, me).wait_recv()
            for j, chip in enumerate(chips):
                copy(a, 4 + j, (*chip, 1 - c), me).wait_recv()
        for cp in sent:
            cp.wait_send()
        for cp in mine:
            cp.wait()

    return pl.pallas_call(
        body, name=name,
        out_shape=[jax.ShapeDtypeStruct((N_DEV,) + a.shape, a.dtype) for a in shards],
        in_specs=[_HBM] * n, out_specs=[_HBM] * n,
        scratch_shapes=[pltpu.SemaphoreType.DMA((n * N_PEER,)), pltpu.SemaphoreType.DMA((n * N_PEER,)),
                        pltpu.SemaphoreType.DMA((n,))],
    )(*shards)


_FLIPS = [(fx, fy, fc) for fx in (0, 1) for fy in (0, 1) for fc in (0, 1)][1:]


def _exchange_copies(in_refs, out_refs, nblk, send_sems, recv_sems, local_sems):
    n = len(in_refs)
    x, y, c = _position()
    me = 4 * x + 2 * y + c

    def peer(f):
        return (1 - x if f[0] else x, 1 - y if f[1] else y, 1 - c if f[2] else c)

    def idx(p):
        return 4 * p[0] + 2 * p[1] + p[2]

    def local(a):
        return pltpu.make_async_copy(in_refs[a].at[me] if a < nblk else in_refs[a], out_refs[a].at[me], local_sems.at[a])

    def remote(a, k, sending):
        p = peer(_FLIPS[k])
        src = in_refs[a].at[idx(p)] if a < nblk else in_refs[a]
        dst = out_refs[a].at[me] if sending else out_refs[a].at[idx(p)]
        return pltpu.make_async_remote_copy(
            src_ref=src, dst_ref=dst, send_sem=send_sems.at[a * N_PEER + k], recv_sem=recv_sems.at[a * N_PEER + k],
            device_id=p, device_id_type=_MESH)

    def start():
        for a in range(n):
            local(a).start()
            for k in range(N_PEER):
                remote(a, k, True).start()

    def wait():
        for a in range(n):
            for k in range(N_PEER):
                remote(a, k, False).wait_recv()
        for a in range(n):
            for k in range(N_PEER):
                remote(a, k, True).wait_send()
            local(a).wait()

    return start, wait


def _exchange_io(blocks, shared):
    arrays = list(blocks) + list(shared)
    n = len(arrays)
    out_shape = [jax.ShapeDtypeStruct(a.shape, a.dtype) for a in blocks]
    out_shape += [jax.ShapeDtypeStruct((N_DEV,) + a.shape, a.dtype) for a in shared]
    sems = [pltpu.SemaphoreType.DMA((n * N_PEER,)), pltpu.SemaphoreType.DMA((n * N_PEER,)), pltpu.SemaphoreType.DMA((n,))]
    return arrays, out_shape, sems


def _exchange(blocks, shared, name):
    arrays, out_shape, sems = _exchange_io(blocks, shared)
    n = len(arrays)

    def body(*refs):
        start, wait = _exchange_copies(refs[:n], refs[n:2 * n], len(blocks), *refs[2 * n:])
        start()
        wait()

    return pl.pallas_call(
        body, name=name, out_shape=out_shape, in_specs=[_HBM] * n, out_specs=[_HBM] * n, scratch_shapes=sems,
    )(*arrays)


def _grid_ends(grid):
    ids = [pl.program_id(i) for i in range(len(grid))]
    first = functools.reduce(jnp.logical_and, [i == 0 for i in ids])
    last = functools.reduce(jnp.logical_and, [i == g - 1 for i, g in zip(ids, grid)])
    return first, last


def _pack(parts, rows):
    flat = jnp.concatenate(parts, axis=-1)
    return jnp.pad(flat, [(0, rows * LANES - flat.shape[-1])]).reshape(rows, LANES)


def _unpack(packed, shapes):
    flat = packed.reshape(-1)
    out, off = [], 0
    for shp in shapes:
        n = int(np.prod(shp))
        out.append(flat[off:off + n].reshape(shp))
        off += n
    return out


def _rows_for(shapes, extra=0):
    n = sum(int(np.prod(s)) for s in shapes) + extra
    return -(-n // (8 * LANES)) * 8


def _lower_bound(logits):
    return jnp.cumsum(jax.nn.softmax(logits.astype(F32), axis=0), axis=0)[0:1]


def _align_axis0(w):
    a, b = 3 * FOX_WIDTH, 3 * FOX_WIDTH + FOX_HEADS
    c = b + 4 * HG_WIDTH
    pad = [(0, LANES - FOX_HEADS)] + [(0, 0)] * (w.ndim - 1)
    return jnp.concatenate([w[c:], w[:a], w[b:c], jnp.pad(w[a:b], pad)], axis=0)


def _unalign_axis0(g):
    a, b = 2 * D_MODEL, 2 * D_MODEL + 3 * FOX_WIDTH
    c = b + 4 * HG_WIDTH
    return jnp.concatenate([g[a:b], g[c:c + FOX_HEADS], g[b:c], g[:a]], axis=0)


TINY_COLS = 768


def _tiny_pack(conv_w_shard, meta_shard):
    cw = jnp.pad(conv_w_shard, ((0, 8 - CONV_WIDTH), (0, TINY_COLS - conv_w_shard.shape[1])))
    mt = jnp.pad(meta_shard, ((0, 0), (0, TINY_COLS - meta_shard.shape[1])))
    return jnp.concatenate([cw, mt], axis=0)


def _tiny_unpack(t, ncw, nmeta):
    return t[..., :CONV_WIDTH, :ncw], t[..., 8:8 + N_META, :nmeta]


def _ffn_weights(g_up, g_down):
    d = g_up.shape[-1]
    return _ffn_interleave(g_up.reshape(-1, d), 0), g_down.reshape(-1, d)


def _early_blocks(g_w_up_t, g_w_down, g_w_out, g_w_a_t, g_w_b_t):
    d = g_w_out.shape[-1]
    ab = jnp.stack([g_w_a_t.reshape(N_DEV, -1, g_w_a_t.shape[-1]), g_w_b_t.reshape(N_DEV, -1, g_w_b_t.shape[-1])], axis=1)
    return [_ffn_deinterleave(g_w_up_t, 0).reshape(N_DEV, -1, d).astype(BF16), g_w_down.reshape(N_DEV, -1, d).astype(BF16),
            g_w_out.reshape(N_DEV, -1, d).astype(BF16), ab.astype(BF16)]


def _local_step(x, target, meta, norm1_gain, w_in_t, fox_b_f, q_gain, k_gain, lb, hg_out_gain, w_a_t, w_b_t, w_out,
                norm2_gain, w_up_t, conv_w, conv_b, w_down, ffn_shards=None):
    nb, seq, d = x.shape
    assert seq % SEQ_BLOCK == 0 and N_META < SEQ_BLOCK
    l = seq + N_META
    lp = -(-l // SEQ_BLOCK) * SEQ_BLOCK
    m = nb * lp
    qg = jnp.tile(q_gain, (1, FOX_HEADS))
    kg = jnp.tile(k_gain, (1, FOX_HEADS))
    bf = jnp.pad(fox_b_f, ((0, 0), (0, LANES - FOX_HEADS)))

    h0, xn = _embed_rms(x.reshape(nb * seq, d), meta, norm1_gain, nb, lp, l, "embed_rms1")
    proj = _matmul(xn, w_in_t, "nt", F32, "proj_in")
    qa, ka, vb = _fox_prep(proj, qg, kg, bf, nb, lp, "fox_prep")
    if ffn_shards is None:
        o_fox, lse = _fox_fwd(qa, ka, vb, nb, lp, "fox_fwd")
    else:
        o_fox, lse, g_up, g_down = _fox_fwd(qa, ka, vb, nb, lp, "fox_fwd", ride=ffn_shards)
        w_up_t, w_down = _ffn_weights(g_up, g_down)
    o_raw, o_hg, s_save = _hgrn_fwd(proj, lb, hg_out_gain, nb, lp, "hgrn_fwd")
    ya = _matmul(o_hg, w_a_t, "nt", BF16, "branch_a")
    yb = _matmul(o_fox, w_b_t, "nt", BF16, "branch_b")
    merged = _gate_fwd(proj, ya, yb, "gate_fwd")
    h1 = _matmul(merged, w_out, "nn", F32, "mix_out", residual=h0)
    hn = _rms_fwd(h1, norm2_gain, "rms2_fwd")
    up = _matmul(hn, w_up_t, "nt", BF16, "ffn_up")
    act = _conv_fwd(up, conv_w, conv_b, nb, lp, "conv_fwd")
    out = _matmul(act, w_down, "nn", F32, "ffn_down", residual=h1)
    dy, lsum = _loss_head(out, target.reshape(nb * seq, d), nb, lp, l, "loss_head")
    loss = (0.5 / d) * jnp.sum(lsum)

    dact = _matmul(dy, w_down, "nt", BF16, "d_act")
    g_w_down = _matmul(act, dy, "tn", F32, "g_w_down")
    dup, g_conv_w, g_conv_b = _conv_bwd(up, dact, conv_w, conv_b, nb, lp, "conv_bwd")
    dhn = _matmul(dup, w_up_t, "nn", F32, "d_hn")
    g_w_up_t = _matmul(dup, hn, "tn", F32, "g_w_up")
    dh1, g_norm2 = _rms_bwd(h1, norm2_gain, dhn, dy, "rms2_bwd")

    dmerged = _matmul(dh1, w_out, "nt", BF16, "d_merged")
    g_w_out = _matmul(merged, dh1, "tn", F32, "g_w_out")
    dya, dyb, dgab = _gate_bwd(proj, ya, yb, dmerged, "gate_bwd")
    do_hg = _matmul(dya, w_a_t, "nn", F32, "d_o_hg")
    g_w_a_t = _matmul(dya, o_hg, "tn", F32, "g_w_a")
    do_fox = _matmul(dyb, w_b_t, "nn", BF16, "d_o_fox")
    g_w_b_t = _matmul(dyb, o_fox, "tn", F32, "g_w_b")
    dhq, dhf, dhi, dhg, g_hg_gain, g_lb = _hgrn_bwd(proj, o_raw, s_save, do_hg, lb, hg_out_gain, nb, lp, "hgrn_bwd")
    if ffn_shards is None:
        dqs, dkn, dvv, dc0, dc1 = _fox_bwd(qa, ka, vb, do_fox, o_fox, lse, nb, lp, "fox_bwd")
        early = None
    else:
        dqs, dkn, dvv, dc0, dc1, *early = _fox_bwd(qa, ka, vb, do_fox, o_fox, lse, nb, lp, "fox_bwd",
                                                   ride=_early_blocks(g_w_up_t, g_w_down, g_w_out, g_w_a_t, g_w_b_t))
    dcum = jnp.stack([dc0, dc1], axis=2).reshape(nb, FOX_HEADS, lp)
    dcum = jnp.pad(jnp.transpose(dcum, (0, 2, 1)), ((0, 0), (0, 0), (0, LANES - FOX_HEADS))).reshape(m, LANES)
    dfqkv, dff, g_qg, g_kg, g_bf = _fox_prep_bwd(proj, dqs, dkn, dvv, dcum, qg, kg, bf, nb, lp, "fox_prep_bwd")
    dproj = jnp.concatenate([dgab, dfqkv, dhq, dhf, dhi, dhg, dff], axis=1)
    g_w_in_t = _matmul(dproj, xn, "tn", F32, "g_w_in")
    if ffn_shards is None:
        dxn = _matmul(dproj, w_in_t, "nn", F32, "d_xn")
    else:
        blocks_in = _unalign_axis0(g_w_in_t).reshape(N_DEV, -1, d).astype(BF16)
        dxn, r_in = _matmul(dproj, w_in_t, "nn", F32, "d_xn", ride=[blocks_in])
        early = early + [r_in]
    dh0, g_norm1 = _rms_bwd(h0, norm1_gain, dxn, dh1, "rms1_bwd")

    dh0 = dh0.reshape(nb, lp, d)
    grad_x = dh0[:, N_META:l]
    g_meta = jnp.sum(dh0[:, :N_META], axis=0)
    g_q_gain = jnp.sum(g_qg.reshape(FOX_HEADS, FOX_HEAD_DIM), axis=0, keepdims=True)
    g_k_gain = jnp.sum(g_kg.reshape(FOX_HEADS, FOX_HEAD_DIM), axis=0, keepdims=True)
    grads = dict(meta_tokens=g_meta, norm1_gain=g_norm1, w_in_t=g_w_in_t, fox_b_f=g_bf[:, :FOX_HEADS],
                 q_norm_gain=g_q_gain, k_norm_gain=g_k_gain, lb=g_lb, hg_out_gain=g_hg_gain,
                 w_a_t=g_w_a_t, w_b_t=g_w_b_t, w_out=g_w_out, norm2_gain=g_norm2, w_up_t=g_w_up_t,
                 conv_w=g_conv_w, conv_b=g_conv_b, w_down=g_w_down, early=early)
    return loss, grad_x, grads


SMALL = ("norm1_gain", "fox_b_f", "q_norm_gain", "k_norm_gain", "hg_lb_logits", "hg_out_gain", "norm2_gain", "conv_b")
ORDER = ("meta_tokens", "norm1_gain", "w_in", "fox_b_f", "q_norm_gain", "k_norm_gain", "hg_lb_logits", "hg_out_gain",
         "w_branch_a", "w_branch_b", "w_out", "norm2_gain", "w_up", "conv_w", "conv_b", "w_down")


def kernel(x, meta_tokens, norm1_gain, w_in, fox_b_f, q_norm_gain, k_norm_gain, hg_lb_logits, hg_out_gain, w_branch_a, w_branch_b, w_out, norm2_gain, w_up, conv_w, conv_b, w_down, loss_target, m_meta_tokens, m_norm1_gain, m_w_in, m_fox_b_f, m_q_norm_gain, m_k_norm_gain, m_hg_lb_logits, m_hg_out_gain, m_w_branch_a, m_w_branch_b, m_w_out, m_norm2_gain, m_w_up, m_conv_w, m_conv_b, m_w_down, v_meta_tokens, v_norm1_gain, v_w_in, v_fox_b_f, v_q_norm_gain, v_k_norm_gain, v_hg_lb_logits, v_hg_out_gain, v_w_branch_a, v_w_branch_b, v_w_out, v_norm2_gain, v_w_up, v_conv_w, v_conv_b, v_w_down):
    w = dict(meta_tokens=meta_tokens, norm1_gain=norm1_gain, w_in=w_in, fox_b_f=fox_b_f, q_norm_gain=q_norm_gain,
             k_norm_gain=k_norm_gain, hg_lb_logits=hg_lb_logits, hg_out_gain=hg_out_gain, w_branch_a=w_branch_a,
             w_branch_b=w_branch_b, w_out=w_out, norm2_gain=norm2_gain, w_up=w_up, conv_w=conv_w, conv_b=conv_b,
             w_down=w_down)
    mom = dict(meta_tokens=m_meta_tokens, norm1_gain=m_norm1_gain, w_in=m_w_in, fox_b_f=m_fox_b_f,
               q_norm_gain=m_q_norm_gain, k_norm_gain=m_k_norm_gain, hg_lb_logits=m_hg_lb_logits,
               hg_out_gain=m_hg_out_gain, w_branch_a=m_w_branch_a, w_branch_b=m_w_branch_b, w_out=m_w_out,
               norm2_gain=m_norm2_gain, w_up=m_w_up, conv_w=m_conv_w, conv_b=m_conv_b, w_down=m_w_down)
    var = dict(meta_tokens=v_meta_tokens, norm1_gain=v_norm1_gain, w_in=v_w_in, fox_b_f=v_fox_b_f,
               q_norm_gain=v_q_norm_gain, k_norm_gain=v_k_norm_gain, hg_lb_logits=v_hg_lb_logits,
               hg_out_gain=v_hg_out_gain, w_branch_a=v_w_branch_a, w_branch_b=v_w_branch_b, w_out=v_w_out,
               norm2_gain=v_norm2_gain, w_up=v_w_up, conv_w=v_conv_w, conv_b=v_conv_b, w_down=v_w_down)
    d = D_MODEL
    n_in, n_up = w_in.shape[2], w_up.shape[2]
    n_ab, n_meta = w_branch_a.shape[2], meta_tokens.shape[1]

    shards = [w_in[0].T.astype(BF16),
              jnp.stack([w_branch_a[0].T, w_branch_b[0].T]).astype(BF16),
              w_out[0].astype(BF16),
              _tiny_pack(conv_w[0], meta_tokens)]
    g_in, g_ab, g_out, g_tiny = _all_gather(shards, "gather_weights")
    w_in_t = _align_axis0(g_in.reshape(N_DEV * n_in, d))
    w_a_t = g_ab[:, 0].reshape(N_DEV * n_ab, -1)
    w_b_t = g_ab[:, 1].reshape(N_DEV * n_ab, -1)
    cw_slots, meta_slots = _tiny_unpack(g_tiny, n_up, n_meta)
    conv_w_f = _ffn_interleave(jnp.transpose(cw_slots, (1, 0, 2)).reshape(CONV_WIDTH, -1), 1)
    meta_f = jnp.transpose(meta_slots, (1, 0, 2)).reshape(N_META, -1)
    conv_b_i = _ffn_interleave(conv_b, 1)

    lb, lb_vjp = jax.vjp(_lower_bound, hg_lb_logits)
    loss, grad_x, g = _local_step(
        x, loss_target, meta_f, norm1_gain, w_in_t, fox_b_f, q_norm_gain, k_norm_gain, lb, hg_out_gain,
        w_a_t, w_b_t, g_out.reshape(d, d), norm2_gain, None, conv_w_f, conv_b_i, None,
        ffn_shards=(w_up[0].T.astype(BF16), w_down[0].astype(BF16)))

    g["hg_lb_logits"] = lb_vjp(g.pop("lb"))[0]
    g["conv_b"] = _ffn_deinterleave(g["conv_b"], 1)
    gcw = _ffn_deinterleave(g["conv_w"], 1).reshape(CONV_WIDTH, N_DEV, n_up)
    gmeta = g["meta_tokens"].reshape(N_META, N_DEV, n_meta)
    tiny = jnp.concatenate([
        jnp.pad(jnp.transpose(gcw, (1, 0, 2)), ((0, 0), (0, 8 - CONV_WIDTH), (0, TINY_COLS - n_up))),
        jnp.pad(jnp.transpose(gmeta, (1, 0, 2)), ((0, 0), (0, 0), (0, TINY_COLS - n_meta)))], axis=1)
    small_shapes = [w[n].shape for n in SMALL]
    rows_sm = _rows_for(small_shapes, extra=1)
    small = _pack([g[n].reshape(-1) for n in SMALL] + [loss.reshape(1)], rows_sm)
    r_tiny, r_small = _exchange([tiny], [small], "exchange_grads")
    r_up, r_down, r_out, r_ab, r_in = g["early"]

    res = {}
    g_in_s = _slot_sum(r_in, "sum_w_in").T
    res["w_in"] = (g_in_s,) + tuple(_adamw(g_in_s, w_in[0], m_w_in[0], v_w_in[0], "adamw_w_in"))
    g_up_s = _slot_sum(r_up, "sum_w_up").T
    res["w_up"] = (g_up_s,) + tuple(_adamw(g_up_s, w_up[0], m_w_up[0], v_w_up[0], "adamw_w_up"))
    g_ab_s = jnp.swapaxes(_slot_sum(r_ab.reshape(N_DEV, 2 * n_ab, -1), "sum_w_ab").reshape(2, n_ab, -1), 1, 2)
    ab = lambda t: jnp.concatenate([t["w_branch_a"][0], t["w_branch_b"][0]], axis=0)
    o_ab = (g_ab_s.reshape(-1, n_ab),) + tuple(_adamw(g_ab_s.reshape(-1, n_ab), ab(w), ab(mom), ab(var), "adamw_w_ab"))
    half = o_ab[0].shape[0] // 2
    res["w_branch_a"] = tuple(o[:half] for o in o_ab)
    res["w_branch_b"] = tuple(o[half:] for o in o_ab)
    res["w_out"] = tuple(_sum_adamw(r_out, w_out[0], m_w_out[0], v_w_out[0], "adamw_w_out"))
    res["w_down"] = tuple(_sum_adamw(r_down, w_down[0], m_w_down[0], v_w_down[0], "adamw_w_down"))
    tp = lambda t: _tiny_pack(t["conv_w"][0], t["meta_tokens"])
    o_tiny = [_tiny_unpack(o, n_up, n_meta) for o in _sum_adamw(r_tiny, tp(w), tp(mom), tp(var), "adamw_tiny")]
    res["conv_w"] = tuple(o[0] for o in o_tiny)
    res["meta_tokens"] = tuple(o[1] for o in o_tiny)
    zero1 = jnp.zeros((1,), F32)
    sp = lambda t: _pack([t[n].reshape(-1) for n in SMALL] + [zero1], rows_sm)
    o_small = [_unpack(o, small_shapes + [(1,)]) for o in _sum_adamw(r_small, sp(w), sp(mom), sp(var), "adamw_small")]
    for i, n in enumerate(SMALL):
        res[n] = tuple(o[i] for o in o_small)
    loss_all = o_small[0][len(SMALL)].reshape(())

    result = [[res[n][k].reshape(w[n].shape) for n in ORDER] for k in range(4)]
    return (loss_all, grad_x, *result[0], *result[1], *result[2], *result[3])
```

```python
import functools

import jax
import jax.numpy as jnp
import numpy as np
from jax import lax
from jax.experimental import pallas as pl
from jax.experimental.pallas import tpu as pltpu

F32 = jnp.float32
BF16 = jnp.bfloat16

D_MODEL = 1024
N_META = 16
FOX_HEADS = 8
FOX_HEAD_DIM = 64
FOX_WIDTH = FOX_HEADS * FOX_HEAD_DIM
HG_HEADS = 4
HG_DIM = 128
HG_WIDTH = HG_HEADS * HG_DIM
D_FF = 2816
CONV_WIDTH = 3
EPS = 1e-6
IN_COLS = 3 * FOX_WIDTH + FOX_HEADS + 4 * HG_WIDTH + 2 * D_MODEL
N_DEV = 8

ADAM_LR = 0.001
ADAM_B1 = 0.9
ADAM_B2 = 0.999
ADAM_EPS = 1e-08
ADAM_WD = 0.01
ADAM_STEP = 10

LANES = 128
SEQ_BLOCK = 128
SUB = 16
NEG = -1e30
VMEM_LIMIT = 48 * 1024 * 1024

FOX_CB = 2 * D_MODEL // FOX_WIDTH
CB_HQ = (2 * D_MODEL + 3 * FOX_WIDTH) // LANES
CB_HF = CB_HQ + HG_HEADS
CB_HI = CB_HF + HG_HEADS
CB_HG = CB_HI + HG_HEADS
CB_FF = CB_HG + HG_HEADS


def _div_tile(n, target, mult):
    best = None
    for t in range(mult, min(n, target) + 1, mult):
        if n % t == 0:
            best = t
    if best is None:
        best = n
    return best


def _cp(*sem):
    return pltpu.CompilerParams(dimension_semantics=sem, vmem_limit_bytes=VMEM_LIMIT)


def _sigmoid(x):
    return 0.5 * jnp.tanh(0.5 * x) + 0.5


def _dot(a, b, dims, precision=None):
    return lax.dot_general(a, b, (dims, ((), ())), preferred_element_type=F32, precision=precision)


NN = ((1,), (0,))
NT = ((1,), (1,))
TN = ((0,), (0,))
HI = lax.Precision.HIGHEST


MATMUL_VMEM_BUDGET = 30 * 1024 * 1024
MATMUL_MAX_TILE = 2048


def _tile_options(n):
    return [t for t in range(LANES, min(n, MATMUL_MAX_TILE) + 1, LANES) if n % t == 0] or [n]


def _matmul_tiles(m, n, k, a_bytes, b_bytes, o_bytes, has_res):
    tk = _div_tile(k, MATMUL_MAX_TILE, LANES)
    best = None
    for tm in _tile_options(m):
        for tn in _tile_options(n):
            vmem = 2 * (tm * tk * a_bytes + tk * tn * b_bytes) + 2 * tm * tn * o_bytes
            vmem += tm * tn * 4 if (tk < k and o_bytes != 4) else 0
            vmem += 2 * tm * tn * 4 if has_res else 0
            if vmem > MATMUL_VMEM_BUDGET:
                continue
            key = (tm * tn, tn % 256 == 0, tn)
            if best is None or key > best[0]:
                best = (key, tm, tn)
    assert best is not None, (m, n, k)
    return best[1], best[2], tk


def _matmul(a, b, mode, out_dtype, name, residual=None, ride=()):
    if mode == "nn":
        (m, k), (k2, n) = a.shape, b.shape
    elif mode == "nt":
        (m, k), (n, k2) = a.shape, b.shape
    else:
        (k, m), (k2, n) = a.shape, b.shape
    assert k == k2, (a.shape, b.shape, mode)
    has_res = residual is not None
    tm, tn, tk = _matmul_tiles(m, n, k, a.dtype.itemsize, b.dtype.itemsize, jnp.dtype(out_dtype).itemsize, has_res)
    nk = k // tk
    in_place = jnp.dtype(out_dtype) == jnp.dtype(F32)
    if mode == "nn":
        a_spec = pl.BlockSpec((tm, tk), lambda i, j, kk: (i, kk))
        b_spec = pl.BlockSpec((tk, tn), lambda i, j, kk: (kk, j))
        dims = NN
    elif mode == "nt":
        a_spec = pl.BlockSpec((tm, tk), lambda i, j, kk: (i, kk))
        b_spec = pl.BlockSpec((tn, tk), lambda i, j, kk: (j, kk))
        dims = NT
    else:
        a_spec = pl.BlockSpec((tk, tm), lambda i, j, kk: (kk, i))
        b_spec = pl.BlockSpec((tk, tn), lambda i, j, kk: (kk, j))
        dims = TN
    o_spec = pl.BlockSpec((tm, tn), lambda i, j, kk: (i, j))
    grid = (m // tm, n // tn, nk)
    x_arrays, x_shapes, x_sems = _exchange_io(ride, ())
    nx = len(x_arrays)
    n_in = 3 if has_res else 2

    def body(*refs):
        if nx:
            first, last = _grid_ends(grid)
            x_in, x_out = refs[n_in:n_in + nx], refs[n_in + nx + 1:n_in + 2 * nx + 1]
            start, wait = _exchange_copies(x_in, x_out, nx, *refs[n_in + 2 * nx + 1:n_in + 2 * nx + 4])
            pl.when(first)(start)
        compute(*refs)
        if nx:
            pl.when(last)(wait)

    def compute(*refs):
        a_ref, b_ref = refs[0], refs[1]
        r_ref = refs[2] if has_res else None
        o_ref = refs[n_in + nx]
        if nk == 1:
            part = _dot(a_ref[...].astype(BF16), b_ref[...].astype(BF16), dims)
            o_ref[...] = (part + r_ref[...] if has_res else part).astype(o_ref.dtype)
            return
        acc_ref = o_ref if in_place else refs[-1]
        kk = pl.program_id(2)

        @pl.when(kk == 0)
        def _():
            acc_ref[...] = r_ref[...] if (has_res and in_place) else jnp.zeros_like(acc_ref)

        acc_ref[...] += _dot(a_ref[...].astype(BF16), b_ref[...].astype(BF16), dims)

        if not in_place:
            @pl.when(kk == nk - 1)
            def _():
                acc = acc_ref[...]
                if has_res:
                    acc = acc + r_ref[...]
                o_ref[...] = acc.astype(o_ref.dtype)

    in_specs = [a_spec, b_spec] + ([o_spec] if has_res else [])
    args = (a, b) + ((residual,) if has_res else ())
    out_shape = jax.ShapeDtypeStruct((m, n), out_dtype)
    acc = [pltpu.VMEM((tm, tn), F32)] if (nk > 1 and not in_place) else []
    if not nx:
        return pl.pallas_call(
            body, name=name, grid=grid, in_specs=in_specs, out_specs=o_spec, out_shape=out_shape, scratch_shapes=acc,
            compiler_params=_cp("parallel", "parallel", "arbitrary"),
        )(*args)
    return pl.pallas_call(
        body, name=name, grid=grid, in_specs=in_specs + [_HBM] * nx, out_specs=[o_spec] + [_HBM] * nx,
        out_shape=[out_shape] + x_shapes, scratch_shapes=x_sems + acc,
        compiler_params=_cp("arbitrary", "arbitrary", "arbitrary"),
    )(*args, *x_arrays)


def _rms_fwd(x, gain, name):
    m, d = x.shape
    tm = _div_tile(m, 512, 16)

    def body(x_ref, g_ref, o_ref):
        xv = x_ref[...]
        r = lax.rsqrt(jnp.mean(xv * xv, axis=-1, keepdims=True) + EPS)
        o_ref[...] = ((xv * r) * g_ref[...]).astype(o_ref.dtype)

    return pl.pallas_call(
        body, name=name, grid=(m // tm,),
        in_specs=[pl.BlockSpec((tm, d), lambda i: (i, 0)), pl.BlockSpec((1, d), lambda i: (0, 0))],
        out_specs=pl.BlockSpec((tm, d), lambda i: (i, 0)),
        out_shape=jax.ShapeDtypeStruct((m, d), BF16),
        compiler_params=_cp("parallel"),
    )(x, gain)


def _rms_bwd(x, gain, dy, dres, name):
    m, d = x.shape
    tm = _div_tile(m, 256, 8)

    def body(x_ref, g_ref, dy_ref, dr_ref, dx_ref, dg_ref):
        xv = x_ref[...]
        r = lax.rsqrt(jnp.mean(xv * xv, axis=-1, keepdims=True) + EPS)
        nv = xv * r
        dyv = dy_ref[...]
        gdy = dyv * g_ref[...]
        dx_ref[...] = dr_ref[...] + r * (gdy - nv * jnp.mean(gdy * nv, axis=-1, keepdims=True))
        part = jnp.sum(dyv * nv, axis=0, keepdims=True)

        @pl.when(pl.program_id(0) == 0)
        def _():
            dg_ref[...] = part

        @pl.when(pl.program_id(0) > 0)
        def _():
            dg_ref[...] += part

    row = pl.BlockSpec((tm, d), lambda i: (i, 0))
    vec = pl.BlockSpec((1, d), lambda i: (0, 0))
    return pl.pallas_call(
        body, name=name, grid=(m // tm,),
        in_specs=[row, vec, row, row], out_specs=[row, vec],
        out_shape=[jax.ShapeDtypeStruct((m, d), F32), jax.ShapeDtypeStruct((1, d), F32)],
        compiler_params=_cp("arbitrary"),
    )(x, gain, dy, dres)


def _head_stats(xv, lo):
    sq = xv * xv
    s_lo = jnp.sum(jnp.where(lo, sq, 0.0), axis=1, keepdims=True)
    s_hi = jnp.sum(jnp.where(lo, 0.0, sq), axis=1, keepdims=True)
    return jnp.where(lo, s_lo, s_hi) * (1.0 / FOX_HEAD_DIM)


BIAS_LANE = FOX_HEAD_DIM
N_SPLIT = 3


def _split3(c):
    c1 = c.astype(BF16).astype(F32)
    r1 = c - c1
    c2 = r1.astype(BF16).astype(F32)
    c3 = (r1 - c2).astype(BF16).astype(F32)
    return c1, c2, c3


def _fox_prep(proj, qg, kg, bf, nb, lp, name):
    m = proj.shape[0]
    ts = SEQ_BLOCK
    nblk = lp // ts
    scale = FOX_HEAD_DIM ** -0.5

    def body(q_ref, k_ref, v_ref, f_ref, qg_ref, kg_ref, bf_ref, qo_ref, ko_ref, vo_ref, carry_ref):
        lane = lax.broadcasted_iota(jnp.int32, (1, LANES), 1)
        lo = lane < FOX_HEAD_DIM

        @pl.when(pl.program_id(1) == 0)
        def _():
            carry_ref[...] = jnp.zeros_like(carry_ref)

        z = f_ref[...] + bf_ref[...]
        logf = jnp.minimum(z, 0.0) - jnp.log(1.0 + jnp.exp(-jnp.abs(z)))
        logf = jnp.where(lane < FOX_HEADS, logf, 0.0)
        r = lax.broadcasted_iota(jnp.int32, (ts, ts), 0)
        c = lax.broadcasted_iota(jnp.int32, (ts, ts), 1)
        tri = jnp.where(c <= r, 1.0, 0.0).astype(F32)
        cum = _dot(tri, logf, NN, HI) + carry_ref[...]
        carry_ref[...] = cum[ts - 1:ts, :]

        ones = jnp.where((lane >= BIAS_LANE + N_SPLIT) & (lane < BIAS_LANE + 2 * N_SPLIT), 1.0, 0.0)
        ones_k = jnp.where((lane >= BIAS_LANE) & (lane < BIAS_LANE + N_SPLIT), 1.0, 0.0)
        for j in range(FOX_WIDTH // LANES):
            cs = slice(j * LANES, (j + 1) * LANES)
            xq = q_ref[:, cs]
            yq = ((xq * lax.rsqrt(_head_stats(xq, lo) + EPS)) * qg_ref[:, cs]) * scale
            xk = k_ref[:, cs]
            yk = (xk * lax.rsqrt(_head_stats(xk, lo) + EPS)) * kg_ref[:, cs]
            for hh in range(2):
                h = 2 * j + hh
                pieces = _split3(_lane_pick(cum, lane, h))
                qb, kb = ones, ones_k
                for i, piece in enumerate(pieces):
                    qb = jnp.where(lane == BIAS_LANE + i, piece, qb)
                    kb = jnp.where(lane == BIAS_LANE + N_SPLIT + i, -piece, kb)
                yq_h = yq if hh == 0 else pltpu.roll(yq, FOX_HEAD_DIM, 1)
                yk_h = yk if hh == 0 else pltpu.roll(yk, FOX_HEAD_DIM, 1)
                hs = slice(h * LANES, (h + 1) * LANES)
                qo_ref[:, hs] = jnp.where(lo, yq_h, qb).astype(BF16)
                ko_ref[:, hs] = jnp.where(lo, yk_h, kb).astype(BF16)
        vo_ref[...] = v_ref[...].astype(BF16)

    w = FOX_WIDTH
    row = lambda b, i: (b * nblk + i, 0)
    return pl.pallas_call(
        body, name=name, grid=(nb, nblk),
        in_specs=[pl.BlockSpec((ts, w), lambda b, i: (b * nblk + i, FOX_CB)),
                  pl.BlockSpec((ts, w), lambda b, i: (b * nblk + i, FOX_CB + 1)),
                  pl.BlockSpec((ts, w), lambda b, i: (b * nblk + i, FOX_CB + 2)),
                  pl.BlockSpec((ts, LANES), lambda b, i: (b * nblk + i, CB_FF)),
                  pl.BlockSpec((1, w), lambda b, i: (0, 0)),
                  pl.BlockSpec((1, w), lambda b, i: (0, 0)),
                  pl.BlockSpec((1, LANES), lambda b, i: (0, 0))],
        out_specs=[pl.BlockSpec((ts, 2 * w), row), pl.BlockSpec((ts, 2 * w), row), pl.BlockSpec((ts, w), row)],
        out_shape=[jax.ShapeDtypeStruct((m, 2 * w), BF16)] * 2 + [jax.ShapeDtypeStruct((m, w), BF16)],
        scratch_shapes=[pltpu.VMEM((1, LANES), F32)],
        compiler_params=_cp("arbitrary", "arbitrary"),
    )(proj, proj, proj, proj, qg, kg, bf)


def _att_tile(lp):
    return 384 if (lp % 384 == 0 and lp > 384) else 128


def _lane_pick(blk, lane, idx):
    return jnp.sum(jnp.where(lane == idx, blk, 0.0), axis=1, keepdims=True)


def _head_masks():
    lane = lax.broadcasted_iota(jnp.int32, (1, LANES), 1)
    return lane, [(lane >= hh * FOX_HEAD_DIM) & (lane < (hh + 1) * FOX_HEAD_DIM) for hh in range(2)]


def _fox_fwd(qa, ka, vb, nb, lp, name, ride=()):
    m = qa.shape[0]
    tq = _att_tile(lp)
    nq = lp // tq
    npair = FOX_WIDTH // LANES
    grid = (nb, npair, nq)
    r_arrays, r_shapes, r_sems = _exchange_io((), ride)
    nr = len(r_arrays)

    def body(q_ref, k_ref, v_ref, *rest):
        r_in, (o_ref, lse_ref), r_out, sems = rest[:nr], rest[nr:nr + 2], rest[nr + 2:2 * nr + 2], rest[2 * nr + 2:]
        if nr:
            first, last = _grid_ends(grid)
            start, wait = _exchange_copies(r_in, r_out, 0, *sems)
            pl.when(first)(start)
        qi = pl.program_id(2)
        lane, hmasks = _head_masks()
        zero16 = jnp.zeros((), BF16)
        causal = lax.broadcasted_iota(jnp.int32, (tq, 1), 0) >= lax.broadcasted_iota(jnp.int32, (1, tq), 1)
        qs = [q_ref[:, hh * LANES:(hh + 1) * LANES] for hh in range(2)]

        def tile(j, carry, diagonal):
            k0 = pl.multiple_of(j * tq, tq)
            vb = v_ref[pl.ds(k0, tq), :]
            out = []
            for hh in range(2):
                mx, l, acc = carry[3 * hh:3 * hh + 3]
                vz = jnp.where(hmasks[hh], vb, zero16)
                s = _dot(qs[hh], k_ref[pl.ds(k0, tq), hh * LANES:(hh + 1) * LANES], NT)
                if diagonal:
                    s = jnp.where(causal, s, NEG)
                m_new = jnp.maximum(mx, jnp.max(s, axis=1, keepdims=True))
                alpha = jnp.exp(mx - m_new)
                pe = jnp.exp(s - m_new)
                l = alpha * l + jnp.sum(pe, axis=1, keepdims=True)
                acc = alpha * acc + _dot(pe.astype(BF16), vz, NN)
                out += [m_new, l, acc]
            return tuple(out)

        init = (jnp.full((tq, 1), NEG, F32), jnp.zeros((tq, 1), F32), jnp.zeros((tq, LANES), F32)) * 2
        carry = lax.fori_loop(0, qi, lambda j, c: tile(j, c, False), init)
        m0, l0, acc0, m1, l1, acc1 = tile(qi, carry, True)
        o_ref[...] = acc0 / l0 + acc1 / l1
        lse_ref[...] = jnp.where(lane == 0, m0 + jnp.log(l0), jnp.where(lane == 1, m1 + jnp.log(l1), 0.0))
        if nr:
            pl.when(last)(wait)

    return pl.pallas_call(
        body, name=name, grid=grid,
        in_specs=[pl.BlockSpec((tq, 2 * LANES), lambda b, p, i: (b * nq + i, p)),
                  pl.BlockSpec((lp, 2 * LANES), lambda b, p, i: (b, p)),
                  pl.BlockSpec((lp, LANES), lambda b, p, i: (b, p))] + [_HBM] * nr,
        out_specs=[pl.BlockSpec((tq, LANES), lambda b, p, i: (b * nq + i, p)),
                   pl.BlockSpec((None, None, tq, LANES), lambda b, p, i: (b, p, i, 0))] + [_HBM] * nr,
        out_shape=[jax.ShapeDtypeStruct((m, FOX_WIDTH), F32),
                   jax.ShapeDtypeStruct((nb, npair, lp, LANES), F32)] + r_shapes,
        scratch_shapes=r_sems if nr else [],
        compiler_params=_cp(*(["arbitrary"] * 3 if nr else ["parallel", "parallel", "arbitrary"])),
    )(qa, ka, vb, *r_arrays)


def _fox_bwd(qa, ka, vb, do, o, lse, nb, lp, name, ride=()):
    m = qa.shape[0]
    tq = _att_tile(lp)
    nq = lp // tq
    npair = FOX_WIDTH // LANES
    grid = (nb, npair, nq)
    r_arrays, r_shapes, r_sems = _exchange_io(ride, ())
    nr = len(r_arrays)

    def body(k_ref, v_ref, q_ref, do_ref, o_ref, lse_ref, *rest):
        r_in, r_out, sems = rest[:nr], rest[nr + 5:2 * nr + 5], rest[2 * nr + 5:]
        dq_ref, dk_ref, dv_ref, dc0_ref, dc1_ref = rest[nr:nr + 5]
        if nr:
            first, last = _grid_ends(grid)
            start, wait = _exchange_copies(r_in, r_out, nr, *sems)
            pl.when(first)(start)
        j = pl.program_id(2)
        lane, hmasks = _head_masks()
        zero16 = jnp.zeros((), BF16)
        causal = lax.broadcasted_iota(jnp.int32, (tq, 1), 0) >= lax.broadcasted_iota(jnp.int32, (1, tq), 1)

        @pl.when(j == 0)
        def _():
            dq_ref[...] = jnp.zeros_like(dq_ref)

        vv = v_ref[...]
        vzs = [jnp.where(hm, vv, zero16) for hm in hmasks]

        def tile(qi, carry, diagonal):
            dk0, dk1, dv, dc0, dc1 = carry
            q0 = pl.multiple_of(qi * tq, tq)
            dob16 = do_ref[pl.ds(q0, tq), :].astype(BF16)
            ob = o_ref[pl.ds(q0, tq), :]
            lseb = lse_ref[pl.ds(q0, tq), :]
            dks, dcs = [dk0, dk1], [dc0, dc1]
            for hh in range(2):
                hs = slice(hh * LANES, (hh + 1) * LANES)
                q = q_ref[pl.ds(q0, tq), hs]
                doz16 = jnp.where(hmasks[hh], dob16, zero16)
                delta = jnp.sum(doz16.astype(F32) * ob, axis=1, keepdims=True)
                s = _dot(q, k_ref[:, hs], NT) - _lane_pick(lseb, lane, hh)
                if diagonal:
                    s = jnp.where(causal, s, NEG)
                pm = jnp.exp(s)
                ds = pm * (_dot(doz16, vzs[hh], NT) - delta)
                ds16 = ds.astype(BF16)
                dv = dv + _dot(pm.astype(BF16), doz16, TN)
                dks[hh] = dks[hh] + _dot(ds16, q, TN)
                dq_ref[pl.ds(q0, tq), hs] += _dot(ds16, k_ref[:, hs], NN)
                dcs[hh] = dcs[hh] - jnp.sum(ds, axis=0, keepdims=True)
            return dks[0], dks[1], dv, dcs[0], dcs[1]

        zt = jnp.zeros((tq, LANES), F32)
        zr = jnp.zeros((1, tq), F32)
        carry = tile(j, (zt, zt, zt, zr, zr), True)
        dk0, dk1, dv, dc0, dc1 = lax.fori_loop(j + 1, nq, lambda qi, c: tile(qi, c, False), carry)
        dk_ref[:, :LANES] = dk0
        dk_ref[:, LANES:] = dk1
        dv_ref[...] = dv
        dc0_ref[...] = dc0
        dc1_ref[...] = dc1
        if nr:
            pl.when(last)(wait)

    full2 = pl.BlockSpec((lp, 2 * LANES), lambda b, p, j: (b, p))
    full = pl.BlockSpec((lp, LANES), lambda b, p, j: (b, p))
    blk2 = pl.BlockSpec((tq, 2 * LANES), lambda b, p, j: (b * nq + j, p))
    blk = pl.BlockSpec((tq, LANES), lambda b, p, j: (b * nq + j, p))
    dcs = pl.BlockSpec((None, None, 1, tq), lambda b, p, j: (b, p, 0, j))
    return pl.pallas_call(
        body, name=name, grid=grid,
        in_specs=[blk2, blk, full2, full, full,
                  pl.BlockSpec((None, None, lp, LANES), lambda b, p, j: (b, p, 0, 0))] + [_HBM] * nr,
        out_specs=[full2, blk2, blk, dcs, dcs] + [_HBM] * nr,
        out_shape=[jax.ShapeDtypeStruct((m, 2 * FOX_WIDTH), F32)] * 2 + [jax.ShapeDtypeStruct((m, FOX_WIDTH), F32)]
        + [jax.ShapeDtypeStruct((nb, npair, 1, lp), F32)] * 2 + r_shapes,
        scratch_shapes=r_sems if nr else [],
        compiler_params=_cp(*(["arbitrary"] * 3 if nr else ["parallel", "parallel", "arbitrary"])),
    )(ka, vb, qa, do, o, lse, *r_arrays)


def _fox_prep_bwd(proj, dqa, dka, dv, dcum, qg, kg, bf, nb, lp, name):
    m = proj.shape[0]
    ts = SEQ_BLOCK
    nblk = lp // ts
    scale = FOX_HEAD_DIM ** -0.5
    w = FOX_WIDTH
    wo = 3 * w

    def body(q_ref, k_ref, f_ref, dq_ref, dk_ref, dv_ref, dc_ref, qg_ref, kg_ref, bf_ref,
             out_ref, dff_ref, dqg_ref, dkg_ref, dbf_ref, carry_ref):
        first = (pl.program_id(0) == 0) & (pl.program_id(1) == 0)
        lane = lax.broadcasted_iota(jnp.int32, (1, LANES), 1)
        lo = lane < FOX_HEAD_DIM

        @pl.when(first)
        def _():
            dqg_ref[...] = jnp.zeros_like(dqg_ref)
            dkg_ref[...] = jnp.zeros_like(dkg_ref)
            dbf_ref[...] = jnp.zeros_like(dbf_ref)

        def norm_bwd(x, g, dy):
            r = lax.rsqrt(_head_stats(x, lo) + EPS)
            nv = x * r
            gdy = dy * g
            prod = gdy * nv
            s_lo = jnp.sum(jnp.where(lo, prod, 0.0), axis=1, keepdims=True)
            s_hi = jnp.sum(jnp.where(lo, 0.0, prod), axis=1, keepdims=True)
            mean = jnp.where(lo, s_lo, s_hi) * (1.0 / FOX_HEAD_DIM)
            return r * (gdy - nv * mean), jnp.sum(dy * nv, axis=0, keepdims=True)

        def pair(d_ref, jj):
            even = d_ref[:, 2 * jj * LANES:(2 * jj + 1) * LANES]
            odd = d_ref[:, (2 * jj + 1) * LANES:(2 * jj + 2) * LANES]
            return jnp.where(lo, even, pltpu.roll(odd, FOX_HEAD_DIM, 1))

        for jj in range(w // LANES):
            cs = slice(jj * LANES, (jj + 1) * LANES)
            dx, dg = norm_bwd(q_ref[:, cs], qg_ref[:, cs], pair(dq_ref, jj) * scale)
            out_ref[:, cs] = dx.astype(BF16)
            dqg_ref[:, cs] += dg
            dx, dg = norm_bwd(k_ref[:, cs], kg_ref[:, cs], pair(dk_ref, jj))
            out_ref[:, w + jj * LANES:w + (jj + 1) * LANES] = dx.astype(BF16)
            dkg_ref[:, cs] += dg
        out_ref[:, 2 * w:3 * w] = dv_ref[...].astype(BF16)

        @pl.when(pl.program_id(1) == 0)
        def _():
            carry_ref[...] = jnp.zeros_like(carry_ref)

        dc = dc_ref[...]
        r = lax.broadcasted_iota(jnp.int32, (ts, ts), 0)
        c = lax.broadcasted_iota(jnp.int32, (ts, ts), 1)
        triu = jnp.where(c >= r, 1.0, 0.0).astype(F32)
        dlogf = _dot(triu, dc, NN, HI) + carry_ref[...]
        carry_ref[...] += jnp.sum(dc, axis=0, keepdims=True)
        z = f_ref[...] + bf_ref[...]
        dz = jnp.where(lane < FOX_HEADS, dlogf * _sigmoid(-z), 0.0)
        dff_ref[...] = dz.astype(BF16)
        dbf_ref[...] += jnp.sum(dz, axis=0, keepdims=True)

    rev = lambda b, i: (b * nblk + (nblk - 1 - i), 0)
    vec = lambda n: pl.BlockSpec((1, n), lambda b, i: (0, 0))
    return pl.pallas_call(
        body, name=name, grid=(nb, nblk),
        in_specs=[pl.BlockSpec((ts, w), lambda b, i: (b * nblk + (nblk - 1 - i), FOX_CB)),
                  pl.BlockSpec((ts, w), lambda b, i: (b * nblk + (nblk - 1 - i), FOX_CB + 1)),
                  pl.BlockSpec((ts, LANES), lambda b, i: (b * nblk + (nblk - 1 - i), CB_FF)),
                  pl.BlockSpec((ts, 2 * w), rev), pl.BlockSpec((ts, 2 * w), rev), pl.BlockSpec((ts, w), rev),
                  pl.BlockSpec((ts, LANES), rev), vec(w), vec(w), vec(LANES)],
        out_specs=[pl.BlockSpec((ts, wo), rev), pl.BlockSpec((ts, LANES), rev), vec(w), vec(w), vec(LANES)],
        out_shape=[jax.ShapeDtypeStruct((m, wo), BF16), jax.ShapeDtypeStruct((m, LANES), BF16),
                   jax.ShapeDtypeStruct((1, w), F32),
                   jax.ShapeDtypeStruct((1, w), F32), jax.ShapeDtypeStruct((1, LANES), F32)],
        scratch_shapes=[pltpu.VMEM((1, LANES), F32)],
        compiler_params=_cp("arbitrary", "arbitrary"),
    )(proj, proj, proj, dqa, dka, dv, dcum, qg, kg, bf)


def _chunk_masks():
    r = lax.broadcasted_iota(jnp.int32, (SEQ_BLOCK, SEQ_BLOCK), 0)
    c = lax.broadcasted_iota(jnp.int32, (SEQ_BLOCK, SEQ_BLOCK), 1)
    same = (r // SUB) == (c // SUB)
    return r, c, same


def _hg_gates(hf, lb):
    sg = _sigmoid(hf)
    f = lb + (1.0 - lb) * sg
    return sg, f, jnp.log(f), (1.0 - lb) * _sigmoid(-hf)


def _hg_intra_e(g_ref, base, t, srow):
    diff = g_ref[pl.ds(base + t, 1), :] - g_ref[pl.ds(base, SUB), :]
    return jnp.exp(jnp.where(srow <= t, diff, NEG))


def _hgrn_fwd(proj, lb, gain, nb, lp, name):
    m = proj.shape[0]
    tb = SEQ_BLOCK
    nblk = lp // tb
    ns = tb // SUB

    def body(q_ref, f_ref, i_ref, g_ref, lb_ref, gain_ref, oraw_ref, y_ref, ssave_ref,
             st_ref, g_scr, kin_scr, o_scr):
        @pl.when(pl.program_id(2) == 0)
        def _():
            st_ref[...] = jnp.zeros_like(st_ref)

        ssave_ref[...] = st_ref[...]
        lbv = lb_ref[...]
        _, _, lf, kin = _hg_gates(f_ref[...], lbv)
        r, c, same = _chunk_masks()
        ltri = jnp.where(same & (c <= r), 1.0, 0.0).astype(F32)
        lall = jnp.where(same, 1.0, 0.0).astype(F32)
        g = _dot(ltri, lf, NN, HI)
        gt = _dot(lall, lf, NN, HI)
        g_scr[...] = g
        kin_scr[...] = kin
        qv = q_ref[...]
        qg = (qv * jnp.exp(g)).astype(BF16)
        kg = (kin * jnp.exp(gt - g)).astype(BF16)
        et = jnp.exp(gt)
        srow = lax.broadcasted_iota(jnp.int32, (SUB, 1), 0)
        subs = [slice(cc * SUB, (cc + 1) * SUB) for cc in range(ns)]
        ups = [_dot(i_ref[sl, :].astype(BF16), kg[sl], TN) for sl in subs]
        st = st_ref[...]
        starts = []
        for cc in range(ns):
            starts.append(st)
            st = et[cc * SUB:cc * SUB + 1, :] * st + ups[cc]
        st_ref[...] = st
        for cc, sl in enumerate(subs):
            base = cc * SUB
            kc = kin_scr[sl, :]
            vc = i_ref[sl, :]
            for t in range(SUB):
                e = _hg_intra_e(g_scr, base, t, srow)
                a = jnp.sum((q_ref[pl.ds(base + t, 1), :] * kc) * e, axis=1, keepdims=True)
                o_scr[pl.ds(base + t, 1), :] = jnp.sum(a * vc, axis=0, keepdims=True)
            o_scr[sl, :] += _dot(qg[sl], starts[cc].astype(BF16), NT)
        o = o_scr[...]
        oraw_ref[...] = o
        rr = lax.rsqrt(jnp.mean(o * o, axis=-1, keepdims=True) + EPS)
        hg = g_ref[...]
        y_ref[...] = (((o * rr) * gain_ref[...]) * (hg * _sigmoid(hg))).astype(y_ref.dtype)

    col = lambda cb: pl.BlockSpec((tb, LANES), lambda b, h, i, cb=cb: (b * nblk + i, cb + h))
    out_blk = pl.BlockSpec((tb, LANES), lambda b, h, i: (b * nblk + i, h))
    return pl.pallas_call(
        body, name=name, grid=(nb, HG_HEADS, nblk),
        in_specs=[col(CB_HQ), col(CB_HF), col(CB_HI), col(CB_HG),
                  pl.BlockSpec((1, LANES), lambda b, h, i: (0, h)),
                  pl.BlockSpec((1, LANES), lambda b, h, i: (0, 0))],
        out_specs=[out_blk, out_blk,
                   pl.BlockSpec((None, None, None, HG_DIM, HG_DIM), lambda b, h, i: (b, h, i, 0, 0))],
        out_shape=[jax.ShapeDtypeStruct((m, HG_WIDTH), F32), jax.ShapeDtypeStruct((m, HG_WIDTH), BF16),
                   jax.ShapeDtypeStruct((nb, HG_HEADS, nblk, HG_DIM, HG_DIM), F32)],
        scratch_shapes=[pltpu.VMEM((HG_DIM, HG_DIM), F32), pltpu.VMEM((tb, LANES), F32),
                        pltpu.VMEM((tb, LANES), F32), pltpu.VMEM((tb, LANES), F32)],
        compiler_params=_cp("parallel", "parallel", "arbitrary"),
    )(proj, proj, proj, proj, lb, gain)


def _hgrn_bwd(proj, oraw, ssave, dy, lb, gain, nb, lp, name):
    m = proj.shape[0]
    tb = SEQ_BLOCK
    nblk = lp // tb
    ns = tb // SUB

    def body(q_ref, f_ref, i_ref, g_ref, oraw_ref, ssave_ref, dy_ref, lb_ref, gain_ref,
             dq_ref, df_ref, di_ref, dg_ref, dgain_ref, dlb_ref,
             dst_ref, g_scr, kin_scr, do_scr, dq_scr, dk_scr, dv_scr, dgg_scr):
        hd = pl.program_id(0)
        bb = pl.program_id(1)
        ii = pl.program_id(2)
        gainv = gain_ref[...]
        lbv = lb_ref[...]

        @pl.when((hd == 0) & (bb == 0) & (ii == 0))
        def _():
            dgain_ref[...] = jnp.zeros_like(dgain_ref)

        @pl.when((bb == 0) & (ii == 0))
        def _():
            dlb_ref[...] = jnp.zeros_like(dlb_ref)

        @pl.when(ii == 0)
        def _():
            dst_ref[...] = jnp.zeros_like(dst_ref)

        o = oraw_ref[...]
        rr = lax.rsqrt(jnp.mean(o * o, axis=-1, keepdims=True) + EPS)
        nv = o * rr
        hg = g_ref[...]
        sgg = _sigmoid(hg)
        sil = hg * sgg
        dyv = dy_ref[...]
        dg_ref[...] = (dyv * nv * gainv * (sgg * (1.0 + hg * (1.0 - sgg)))).astype(dg_ref.dtype)
        dgain_ref[...] += jnp.sum(dyv * nv * sil, axis=0, keepdims=True)
        dn = dyv * gainv * sil
        do_scr[...] = rr * (dn - nv * jnp.mean(dn * nv, axis=-1, keepdims=True))

        hf = f_ref[...]
        sg, f, lf, kin = _hg_gates(hf, lbv)
        r, c, same = _chunk_masks()
        ltri = jnp.where(same & (c <= r), 1.0, 0.0).astype(F32)
        lall = jnp.where(same, 1.0, 0.0).astype(F32)
        g = _dot(ltri, lf, NN, HI)
        gt = _dot(lall, lf, NN, HI)
        g_scr[...] = g
        kin_scr[...] = kin
        qv = q_ref[...]
        eg = jnp.exp(g)
        ekg = jnp.exp(gt - g)
        qg = qv * eg
        kg = kin * ekg
        qg16 = qg.astype(BF16)
        kg16 = kg.astype(BF16)
        et = jnp.exp(gt)
        subs = [slice(cc * SUB, (cc + 1) * SUB) for cc in range(ns)]
        ups = [_dot(i_ref[sl, :].astype(BF16), kg16[sl], TN) for sl in subs]
        st = ssave_ref[...]
        starts = []
        for cc in range(ns):
            starts.append(st)
            st = et[cc * SUB:cc * SUB + 1, :] * st + ups[cc]
        do16 = do_scr[...].astype(BF16)
        downs = [_dot(do16[sl], qg16[sl], TN) for sl in subs]
        dst = dst_ref[...]
        afters = [None] * ns
        for cc in reversed(range(ns)):
            afters[cc] = dst
            dst = et[cc * SUB:cc * SUB + 1, :] * dst + downs[cc]
        dst_ref[...] = dst

        srow = lax.broadcasted_iota(jnp.int32, (SUB, 1), 0)
        for cc, sl in enumerate(subs):
            base = cc * SUB
            st = starts[cc]
            st16 = st.astype(BF16)
            dst = afters[cc]
            dst16 = dst.astype(BF16)
            doc16 = do16[sl]
            vc = i_ref[sl, :]
            vc16 = vc.astype(BF16)
            kc = kin_scr[sl, :]
            etc = et[base:base + 1, :]
            dqg = _dot(doc16, st16, NN)
            dv_c = _dot(kg16[sl], dst16, NT)
            dkg = _dot(vc16, dst16, NN)
            dgt = jnp.sum(dst * st, axis=0, keepdims=True) * etc
            dq_c = dqg * eg[sl]
            dk_c = dkg * ekg[sl]
            dg_c = dqg * qg[sl] - dkg * kg[sl]
            dgt = dgt + jnp.sum(dkg * kg[sl], axis=0, keepdims=True)
            for t in range(SUB):
                e = _hg_intra_e(g_scr, base, t, srow)
                qt = q_ref[pl.ds(base + t, 1), :]
                dot_t = do_scr[pl.ds(base + t, 1), :]
                a = jnp.sum((qt * kc) * e, axis=1, keepdims=True)
                da = jnp.sum(dot_t * vc, axis=1, keepdims=True)
                dv_c = dv_c + a * dot_t
                w = da * e
                dq_scr[pl.ds(base + t, 1), :] = jnp.sum(w * kc, axis=0, keepdims=True)
                wq = w * qt
                dk_c = dk_c + wq
                dg_c = dg_c - kc * wq
            dq_i = dq_scr[sl, :]
            dg_c = dg_c + qv[sl] * dq_i + jnp.where(srow == SUB - 1, dgt, 0.0)
            dq_scr[sl, :] = dq_c + dq_i
            dk_scr[sl, :] = dk_c
            dv_scr[sl, :] = dv_c
            dgg_scr[sl, :] = dg_c

        utri = jnp.where(same & (c >= r), 1.0, 0.0).astype(F32)
        dlf = _dot(utri, dgg_scr[...], NN, HI)
        dkin = dk_scr[...]
        dsg = sg * (1.0 - sg)
        df_ref[...] = ((dlf / f - dkin) * ((1.0 - lbv) * dsg)).astype(df_ref.dtype)
        dlb_ref[...] += jnp.sum((dlf / f - dkin) * (1.0 - sg), axis=0, keepdims=True)
        dq_ref[...] = dq_scr[...].astype(dq_ref.dtype)
        di_ref[...] = dv_scr[...].astype(di_ref.dtype)

    rowi = lambda b, i: b * nblk + (nblk - 1 - i)
    col = lambda cb: pl.BlockSpec((tb, LANES), lambda h, b, i, cb=cb: (rowi(b, i), cb + h))
    hblk = pl.BlockSpec((tb, LANES), lambda h, b, i: (rowi(b, i), h))
    return pl.pallas_call(
        body, name=name, grid=(HG_HEADS, nb, nblk),
        in_specs=[col(CB_HQ), col(CB_HF), col(CB_HI), col(CB_HG), hblk,
                  pl.BlockSpec((None, None, None, HG_DIM, HG_DIM), lambda h, b, i: (b, h, nblk - 1 - i, 0, 0)),
                  hblk,
                  pl.BlockSpec((1, LANES), lambda h, b, i: (0, h)),
                  pl.BlockSpec((1, LANES), lambda h, b, i: (0, 0))],
        out_specs=[hblk, hblk, hblk, hblk,
                   pl.BlockSpec((1, LANES), lambda h, b, i: (0, 0)),
                   pl.BlockSpec((1, LANES), lambda h, b, i: (0, h))],
        out_shape=[jax.ShapeDtypeStruct((m, HG_WIDTH), BF16)] * 4
        + [jax.ShapeDtypeStruct((1, LANES), F32), jax.ShapeDtypeStruct((1, HG_WIDTH), F32)],
        scratch_shapes=[pltpu.VMEM((HG_DIM, HG_DIM), F32)] + [pltpu.VMEM((tb, LANES), F32)] * 7,
        compiler_params=_cp("arbitrary", "arbitrary", "arbitrary"),
    )(proj, proj, proj, proj, oraw, ssave, dy, lb, gain)


def _gate_fwd(proj, ya, yb, name):
    m = proj.shape[0]
    tm = _div_tile(m, 256, 16)

    def body(ga_ref, gb_ref, ya_ref, yb_ref, o_ref):
        ya, yb = ya_ref[...].astype(F32), yb_ref[...].astype(F32)
        o_ref[...] = (_sigmoid(ga_ref[...]) * ya + _sigmoid(gb_ref[...]) * yb).astype(o_ref.dtype)

    row = pl.BlockSpec((tm, D_MODEL), lambda i: (i, 0))
    return pl.pallas_call(
        body, name=name, grid=(m // tm,),
        in_specs=[row, pl.BlockSpec((tm, D_MODEL), lambda i: (i, 1)), row, row],
        out_specs=row, out_shape=jax.ShapeDtypeStruct((m, D_MODEL), BF16),
        compiler_params=_cp("parallel"),
    )(proj, proj, ya, yb)


def _gate_bwd(proj, ya, yb, dm, name):
    m = proj.shape[0]
    tm = _div_tile(m, 256, 16)

    def body(ga_ref, gb_ref, ya_ref, yb_ref, dm_ref, dya_ref, dyb_ref, dg_ref):
        dmv = dm_ref[...].astype(F32)
        sa = _sigmoid(ga_ref[...])
        sb = _sigmoid(gb_ref[...])
        dya_ref[...] = (dmv * sa).astype(BF16)
        dyb_ref[...] = (dmv * sb).astype(BF16)
        dg_ref[:, :D_MODEL] = (dmv * ya_ref[...].astype(F32) * (sa * (1.0 - sa))).astype(BF16)
        dg_ref[:, D_MODEL:] = (dmv * yb_ref[...].astype(F32) * (sb * (1.0 - sb))).astype(BF16)

    row = pl.BlockSpec((tm, D_MODEL), lambda i: (i, 0))
    wide = pl.BlockSpec((tm, 2 * D_MODEL), lambda i: (i, 0))
    return pl.pallas_call(
        body, name=name, grid=(m // tm,),
        in_specs=[row, pl.BlockSpec((tm, D_MODEL), lambda i: (i, 1)), row, row, row],
        out_specs=[row, row, wide],
        out_shape=[jax.ShapeDtypeStruct((m, D_MODEL), BF16)] * 2 + [jax.ShapeDtypeStruct((m, 2 * D_MODEL), BF16)],
        compiler_params=_cp("parallel"),
    )(proj, proj, ya, yb, dm)


CONV_ROWS = 128


def _conv3(x, xprev, w_ref, b_ref, rowi):
    r = x.shape[0]
    x1 = jnp.where(rowi < 1, pltpu.roll(xprev, 1, 0), pltpu.roll(x, 1, 0))
    x2 = jnp.where(rowi < 2, pltpu.roll(xprev, 2, 0), pltpu.roll(x, 2, 0))
    u = w_ref[0:1, :] * x2 + w_ref[1:2, :] * x1 + w_ref[2:3, :] * x + b_ref[...]
    return u, x1, x2


def _conv_fwd(up, cw, cb, nb, lp, name):
    m = up.shape[0]
    nct = D_FF // LANES
    r = CONV_ROWS
    nch = lp // r

    def body(u_ref, w_ref, b_ref, o_ref):
        rowi = lax.broadcasted_iota(jnp.int32, (r, 1), 0)

        def step(i, xp):
            r0 = pl.multiple_of(i * r, r)
            xc = u_ref[pl.ds(r0, r), :].astype(F32)
            u, _, _ = _conv3(xc, xp, w_ref, b_ref, rowi)
            ug, uv = u[:, :LANES], u[:, LANES:]
            o_ref[pl.ds(r0, r), :] = ((ug * _sigmoid(ug)) * uv).astype(o_ref.dtype)
            return xc

        lax.fori_loop(0, nch, step, jnp.zeros((r, 2 * LANES), F32))

    return pl.pallas_call(
        body, name=name, grid=(nb, nct),
        in_specs=[pl.BlockSpec((lp, 2 * LANES), lambda b, c: (b, c)),
                  pl.BlockSpec((CONV_WIDTH, 2 * LANES), lambda b, c: (0, c)),
                  pl.BlockSpec((1, 2 * LANES), lambda b, c: (0, c))],
        out_specs=pl.BlockSpec((lp, LANES), lambda b, c: (b, c)),
        out_shape=jax.ShapeDtypeStruct((m, D_FF), BF16),
        compiler_params=_cp("parallel", "parallel"),
    )(up, cw, cb)


def _conv_bwd(up, dact, cw, cb, nb, lp, name):
    m = up.shape[0]
    nct = D_FF // LANES
    r = CONV_ROWS
    nch = lp // r

    def body(u_ref, da_ref, w_ref, b_ref, dup_ref, dw_ref, db_ref):
        rowi = lax.broadcasted_iota(jnp.int32, (r, 1), 0)
        wv = w_ref[...]

        def step(k, carry):
            dun, dw0, dw1, dw2, dbs = carry
            i = nch - 1 - k
            r0 = pl.multiple_of(i * r, r)
            rp = pl.multiple_of(jnp.maximum(i - 1, 0) * r, r)
            xc = u_ref[pl.ds(r0, r), :].astype(F32)
            xp = u_ref[pl.ds(rp, r), :].astype(F32) * (i > 0).astype(F32)
            u, x1, x2 = _conv3(xc, xp, w_ref, b_ref, rowi)
            ug, uv = u[:, :LANES], u[:, LANES:]
            da = da_ref[pl.ds(r0, r), :].astype(F32)
            sg = _sigmoid(ug)
            du = jnp.concatenate([da * uv * (sg * (1.0 + ug * (1.0 - sg))), da * (ug * sg)], axis=1)
            d1 = jnp.where(rowi >= r - 1, pltpu.roll(dun, r - 1, 0), pltpu.roll(du, r - 1, 0))
            d2 = jnp.where(rowi >= r - 2, pltpu.roll(dun, r - 2, 0), pltpu.roll(du, r - 2, 0))
            dup_ref[pl.ds(r0, r), :] = (wv[2:3, :] * du + wv[1:2, :] * d1 + wv[0:1, :] * d2).astype(dup_ref.dtype)
            dw0 = dw0 + jnp.sum(du * x2, axis=0, keepdims=True)
            dw1 = dw1 + jnp.sum(du * x1, axis=0, keepdims=True)
            dw2 = dw2 + jnp.sum(du * xc, axis=0, keepdims=True)
            dbs = dbs + jnp.sum(du, axis=0, keepdims=True)
            return du, dw0, dw1, dw2, dbs

        z1 = jnp.zeros((1, 2 * LANES), F32)
        _, dw0, dw1, dw2, dbs = lax.fori_loop(0, nch, step, (jnp.zeros((r, 2 * LANES), F32), z1, z1, z1, z1))

        @pl.when(pl.program_id(1) == 0)
        def _():
            dw_ref[...] = jnp.zeros_like(dw_ref)
            db_ref[...] = jnp.zeros_like(db_ref)

        dw_ref[0:1, :] += dw0
        dw_ref[1:2, :] += dw1
        dw_ref[2:3, :] += dw2
        db_ref[...] += dbs

    return pl.pallas_call(
        body, name=name, grid=(nct, nb),
        in_specs=[pl.BlockSpec((lp, 2 * LANES), lambda c, b: (b, c)),
                  pl.BlockSpec((lp, LANES), lambda c, b: (b, c)),
                  pl.BlockSpec((CONV_WIDTH, 2 * LANES), lambda c, b: (0, c)),
                  pl.BlockSpec((1, 2 * LANES), lambda c, b: (0, c))],
        out_specs=[pl.BlockSpec((lp, 2 * LANES), lambda c, b: (b, c)),
                   pl.BlockSpec((CONV_WIDTH, 2 * LANES), lambda c, b: (0, c)),
                   pl.BlockSpec((1, 2 * LANES), lambda c, b: (0, c))],
        out_shape=[jax.ShapeDtypeStruct((m, 2 * D_FF), BF16),
                   jax.ShapeDtypeStruct((CONV_WIDTH, 2 * D_FF), F32),
                   jax.ShapeDtypeStruct((1, 2 * D_FF), F32)],
        compiler_params=_cp("parallel", "arbitrary"),
    )(up, dact, cw, cb)


def _ffn_interleave(a, axis):
    shp = a.shape
    a = a.reshape(shp[:axis] + (2, D_FF // LANES, LANES) + shp[axis + 1:])
    return jnp.swapaxes(a, axis, axis + 1).reshape(shp)


def _ffn_deinterleave(a, axis):
    shp = a.shape
    a = a.reshape(shp[:axis] + (D_FF // LANES, 2, LANES) + shp[axis + 1:])
    return jnp.swapaxes(a, axis, axis + 1).reshape(shp)


def _shifted_rows(prev_ref, cur_ref):
    keep = SEQ_BLOCK - N_META
    return jnp.concatenate([prev_ref[keep:, :], cur_ref[:keep, :]], axis=0)


def _frame_specs(nblk, nfb, d):
    prev = pl.BlockSpec((SEQ_BLOCK, d), lambda b, i: (b * nfb + jnp.clip(i - 1, 0, nfb - 1), 0))
    cur = pl.BlockSpec((SEQ_BLOCK, d), lambda b, i: (b * nfb + jnp.clip(i, 0, nfb - 1), 0))
    return prev, cur


def _embed_rms(x2, meta, gain, nb, lp, l, name):
    d = x2.shape[1]
    tr = SEQ_BLOCK
    nblk = lp // tr
    nfb = (l - N_META) // tr
    m = nb * lp

    def body(prev_ref, cur_ref, meta_ref, g_ref, h_ref, o_ref):
        i = pl.program_id(1)
        t = i * tr + lax.broadcasted_iota(jnp.int32, (tr, 1), 0)
        rows = jnp.where(t < l, _shifted_rows(prev_ref, cur_ref), 0.0)
        head = jnp.concatenate([meta_ref[...], jnp.zeros((tr - N_META, d), F32)], axis=0)
        xv = jnp.where(t < N_META, head, rows)
        h_ref[...] = xv
        r = lax.rsqrt(jnp.mean(xv * xv, axis=-1, keepdims=True) + EPS)
        o_ref[...] = ((xv * r) * g_ref[...]).astype(o_ref.dtype)

    prev, cur = _frame_specs(nblk, nfb, d)
    row = pl.BlockSpec((tr, d), lambda b, i: (b * nblk + i, 0))
    return pl.pallas_call(
        body, name=name, grid=(nb, nblk),
        in_specs=[prev, cur, pl.BlockSpec((N_META, d), lambda b, i: (0, 0)), pl.BlockSpec((1, d), lambda b, i: (0, 0))],
        out_specs=[row, row],
        out_shape=[jax.ShapeDtypeStruct((m, d), F32), jax.ShapeDtypeStruct((m, d), BF16)],
        compiler_params=_cp("parallel", "parallel"),
    )(x2, x2, meta, gain)


def _loss_head(out, tgt2, nb, lp, l, name):
    m, d = out.shape
    tr = SEQ_BLOCK
    nblk = lp // tr
    nfb = (l - N_META) // tr

    def body(o_ref, prev_ref, cur_ref, dy_ref, ls_ref):
        t = pl.program_id(1) * tr + lax.broadcasted_iota(jnp.int32, (tr, 1), 0)
        valid = (t >= N_META) & (t < l)
        err = jnp.where(valid, o_ref[...] - _shifted_rows(prev_ref, cur_ref), 0.0)
        dy_ref[...] = err * (1.0 / d)
        part = jnp.sum(err * err, axis=0, keepdims=True)
        first = (pl.program_id(0) == 0) & (pl.program_id(1) == 0)

        @pl.when(first)
        def _():
            ls_ref[...] = part

        @pl.when(jnp.logical_not(first))
        def _():
            ls_ref[...] += part

    prev, cur = _frame_specs(nblk, nfb, d)
    row = pl.BlockSpec((tr, d), lambda b, i: (b * nblk + i, 0))
    return pl.pallas_call(
        body, name=name, grid=(nb, nblk),
        in_specs=[row, prev, cur], out_specs=[row, pl.BlockSpec((1, d), lambda b, i: (0, 0))],
        out_shape=[jax.ShapeDtypeStruct((m, d), F32), jax.ShapeDtypeStruct((1, d), F32)],
        compiler_params=_cp("arbitrary", "arbitrary"),
    )(out, tgt2, tgt2)


def _adam_math(g, w, mom, var):
    c1 = 1.0 - ADAM_B1 ** ADAM_STEP
    c2 = 1.0 - ADAM_B2 ** ADAM_STEP
    mn = ADAM_B1 * mom + (1.0 - ADAM_B1) * g
    vn = ADAM_B2 * var + (1.0 - ADAM_B2) * (g * g)
    delta = -ADAM_LR * ((mn / c1) / (jnp.sqrt(vn / c2) + ADAM_EPS) + ADAM_WD * w)
    return delta, mn, vn


def _slot_sum(recv, name):
    _, r, c = recv.shape
    tc = _div_tile(c, 256, LANES)

    def body(r_ref, g_ref):
        g = r_ref[0].astype(F32)
        for s in range(1, N_DEV):
            g = g + r_ref[s].astype(F32)
        g_ref[...] = g

    return pl.pallas_call(
        body, name=name, grid=(c // tc,),
        in_specs=[pl.BlockSpec((N_DEV, r, tc), lambda j: (0, 0, j))],
        out_specs=pl.BlockSpec((r, tc), lambda j: (0, j)),
        out_shape=jax.ShapeDtypeStruct((r, c), F32),
        compiler_params=_cp("parallel"),
    )(recv)


def _adamw(g, w, mom, var, name):
    r, c = w.shape
    tr = _div_tile(r, 256, 8)

    def body(g_ref, w_ref, m_ref, v_ref, d_ref, mo_ref, vo_ref):
        d_ref[...], mo_ref[...], vo_ref[...] = _adam_math(g_ref[...], w_ref[...], m_ref[...], v_ref[...])

    row = pl.BlockSpec((tr, c), lambda i: (i, 0))
    return pl.pallas_call(
        body, name=name, grid=(r // tr,), in_specs=[row] * 4, out_specs=[row] * 3,
        out_shape=[jax.ShapeDtypeStruct((r, c), F32)] * 3,
        compiler_params=_cp("parallel"),
    )(g, w, mom, var)


def _sum_adamw(recv, w, mom, var, name):
    r, c = w.shape
    tr = _div_tile(r, 256, 8)

    def body(r_ref, w_ref, m_ref, v_ref, g_ref, d_ref, mo_ref, vo_ref):
        g = r_ref[0].astype(F32)
        for s in range(1, N_DEV):
            g = g + r_ref[s].astype(F32)
        g_ref[...] = g
        d_ref[...], mo_ref[...], vo_ref[...] = _adam_math(g, w_ref[...], m_ref[...], v_ref[...])

    row = pl.BlockSpec((tr, c), lambda i: (i, 0))
    return pl.pallas_call(
        body, name=name, grid=(r // tr,),
        in_specs=[pl.BlockSpec((N_DEV, tr, c), lambda i: (0, i, 0)), row, row, row],
        out_specs=[row] * 4,
        out_shape=[jax.ShapeDtypeStruct((r, c), F32)] * 4,
        compiler_params=_cp("parallel"),
    )(recv, w, mom, var)


_MESH = pl.DeviceIdType.MESH
_HBM = pl.BlockSpec(memory_space=pltpu.HBM)
N_PEER = N_DEV - 1


def _position():
    return lax.axis_index("x"), lax.axis_index("y"), lax.axis_index("c")


def _all_gather(shards, name):
    n = len(shards)

    def body(*refs):
        x_refs, out_refs = refs[:n], refs[n:2 * n]
        send_sems, recv_sems, local_sems = refs[2 * n:]
        x, y, c = _position()
        me, sibling = (x, y, c), (x, y, 1 - c)
        chips = [(1 - x, y), (x, 1 - y), (1 - x, 1 - y)]

        def copy(a, k, block, to, src=None):
            slot = out_refs[a].at[4 * block[0] + 2 * block[1] + block[2]]
            return pltpu.make_async_remote_copy(
                src_ref=slot if src is None else src, dst_ref=slot,
                send_sem=send_sems.at[a * N_PEER + k], recv_sem=recv_sems.at[a * N_PEER + k],
                device_id=to, device_id_type=_MESH)

        mine, sent = [], []
        for a in range(n):
            cp = pltpu.make_async_copy(x_refs[a], out_refs[a].at[4 * x + 2 * y + c], local_sems.at[a])
            cp.start()
            mine.append(cp)
            first = [copy(a, 0, me, sibling, src=x_refs[a])]
            first += [copy(a, 1 + j, me, (*chip, c), src=x_refs[a]) for j, chip in enumerate(chips)]
            for cp in first:
                cp.start()
            sent += first
        for a in range(n):
            for j, chip in enumerate(chips):
                copy(a, 1 + j, (*chip, c), me).wait_recv()
                fwd = copy(a, 4 + j, (*chip, c), sibling)
                fwd.start()
                sent.append(fwd)
        for a in range(n):
            copy(a, 0, sibling, me).wait_recv()
            for j, chip in enumerate(chips):
                copy(a, 4 + j, (*chip, 1 - c), me).wait_recv()
        for cp in sent:
            cp.wait_send()
        for cp in mine:
            cp.wait()

    return pl.pallas_call(
        body, name=name,
        out_shape=[jax.ShapeDtypeStruct((N_DEV,) + a.shape, a.dtype) for a in shards],
        in_specs=[_HBM] * n, out_specs=[_HBM] * n,
        scratch_shapes=[pltpu.SemaphoreType.DMA((n * N_PEER,)), pltpu.SemaphoreType.DMA((n * N_PEER,)),
                        pltpu.SemaphoreType.DMA((n,))],
    )(*shards)


_FLIPS = [(fx, fy, fc) for fx in (0, 1) for fy in (0, 1) for fc in (0, 1)][1:]


def _exchange_copies(in_refs, out_refs, nblk, send_sems, recv_sems, local_sems):
    n = len(in_refs)
    x, y, c = _position()
    me = 4 * x + 2 * y + c

    def peer(f):
        return (1 - x if f[0] else x, 1 - y if f[1] else y, 1 - c if f[2] else c)

    def idx(p):
        return 4 * p[0] + 2 * p[1] + p[2]

    def local(a):
        return pltpu.make_async_copy(in_refs[a].at[me] if a < nblk else in_refs[a], out_refs[a].at[me], local_sems.at[a])

    def remote(a, k, sending):
        p = peer(_FLIPS[k])
        src = in_refs[a].at[idx(p)] if a < nblk else in_refs[a]
        dst = out_refs[a].at[me] if sending else out_refs[a].at[idx(p)]
        return pltpu.make_async_remote_copy(
            src_ref=src, dst_ref=dst, send_sem=send_sems.at[a * N_PEER + k], recv_sem=recv_sems.at[a * N_PEER + k],
            device_id=p, device_id_type=_MESH)

    def start():
        for a in range(n):
            local(a).start()
            for k in range(N_PEER):
                remote(a, k, True).start()

    def wait():
        for a in range(n):
            for k in range(N_PEER):
                remote(a, k, False).wait_recv()
        for a in range(n):
            for k in range(N_PEER):
                remote(a, k, True).wait_send()
            local(a).wait()

    return start, wait


def _exchange_io(blocks, shared):
    arrays = list(blocks) + list(shared)
    n = len(arrays)
    out_shape = [jax.ShapeDtypeStruct(a.shape, a.dtype) for a in blocks]
    out_shape += [jax.ShapeDtypeStruct((N_DEV,) + a.shape, a.dtype) for a in shared]
    sems = [pltpu.SemaphoreType.DMA((n * N_PEER,)), pltpu.SemaphoreType.DMA((n * N_PEER,)), pltpu.SemaphoreType.DMA((n,))]
    return arrays, out_shape, sems


def _exchange(blocks, shared, name):
    arrays, out_shape, sems = _exchange_io(blocks, shared)
    n = len(arrays)

    def body(*refs):
        start, wait = _exchange_copies(refs[:n], refs[n:2 * n], len(blocks), *refs[2 * n:])
        start()
        wait()

    return pl.pallas_call(
        body, name=name, out_shape=out_shape, in_specs=[_HBM] * n, out_specs=[_HBM] * n, scratch_shapes=sems,
    )(*arrays)


def _grid_ends(grid):
    ids = [pl.program_id(i) for i in range(len(grid))]
    first = functools.reduce(jnp.logical_and, [i == 0 for i in ids])
    last = functools.reduce(jnp.logical_and, [i == g - 1 for i, g in zip(ids, grid)])
    return first, last


def _pack(parts, rows):
    flat = jnp.concatenate(parts, axis=-1)
    return jnp.pad(flat, [(0, rows * LANES - flat.shape[-1])]).reshape(rows, LANES)


def _unpack(packed, shapes):
    flat = packed.reshape(-1)
    out, off = [], 0
    for shp in shapes:
        n = int(np.prod(shp))
        out.append(flat[off:off + n].reshape(shp))
        off += n
    return out


def _rows_for(shapes, extra=0):
    n = sum(int(np.prod(s)) for s in shapes) + extra
    return -(-n // (8 * LANES)) * 8


def _lower_bound(logits):
    return jnp.cumsum(jax.nn.softmax(logits.astype(F32), axis=0), axis=0)[0:1]


def _align_axis0(w):
    a, b = 3 * FOX_WIDTH, 3 * FOX_WIDTH + FOX_HEADS
    c = b + 4 * HG_WIDTH
    pad = [(0, LANES - FOX_HEADS)] + [(0, 0)] * (w.ndim - 1)
    return jnp.concatenate([w[c:], w[:a], w[b:c], jnp.pad(w[a:b], pad)], axis=0)


def _unalign_axis0(g):
    a, b = 2 * D_MODEL, 2 * D_MODEL + 3 * FOX_WIDTH
    c = b + 4 * HG_WIDTH
    return jnp.concatenate([g[a:b], g[c:c + FOX_HEADS], g[b:c], g[:a]], axis=0)


TINY_COLS = 768


def _tiny_pack(conv_w_shard, meta_shard):
    cw = jnp.pad(conv_w_shard, ((0, 8 - CONV_WIDTH), (0, TINY_COLS - conv_w_shard.shape[1])))
    mt = jnp.pad(meta_shard, ((0, 0), (0, TINY_COLS - meta_shard.shape[1])))
    return jnp.concatenate([cw, mt], axis=0)


def _tiny_unpack(t, ncw, nmeta):
    return t[..., :CONV_WIDTH, :ncw], t[..., 8:8 + N_META, :nmeta]


def _late_weights(g_up, g_down, g_ab, g_out):
    d = g_up.shape[-1]
    w_a_t = g_ab[:, 0].reshape(-1, g_ab.shape[-1])
    w_b_t = g_ab[:, 1].reshape(-1, g_ab.shape[-1])
    return _ffn_interleave(g_up.reshape(-1, d), 0), g_down.reshape(-1, d), w_a_t, w_b_t, g_out.reshape(-1, d)


def _early_blocks(g_w_up_t, g_w_down, g_w_out, g_w_a_t, g_w_b_t):
    d = g_w_out.shape[-1]
    ab = jnp.stack([g_w_a_t.reshape(N_DEV, -1, g_w_a_t.shape[-1]), g_w_b_t.reshape(N_DEV, -1, g_w_b_t.shape[-1])], axis=1)
    return [_ffn_deinterleave(g_w_up_t, 0).reshape(N_DEV, -1, d).astype(BF16), g_w_down.reshape(N_DEV, -1, d).astype(BF16),
            g_w_out.reshape(N_DEV, -1, d).astype(BF16), ab.astype(BF16)]


def _local_step(x, target, meta, norm1_gain, w_in_t, fox_b_f, q_gain, k_gain, lb, hg_out_gain, w_a_t, w_b_t, w_out,
                norm2_gain, w_up_t, conv_w, conv_b, w_down, ffn_shards=None):
    nb, seq, d = x.shape
    assert seq % SEQ_BLOCK == 0 and N_META < SEQ_BLOCK
    l = seq + N_META
    lp = -(-l // SEQ_BLOCK) * SEQ_BLOCK
    m = nb * lp
    qg = jnp.tile(q_gain, (1, FOX_HEADS))
    kg = jnp.tile(k_gain, (1, FOX_HEADS))
    bf = jnp.pad(fox_b_f, ((0, 0), (0, LANES - FOX_HEADS)))

    h0, xn = _embed_rms(x.reshape(nb * seq, d), meta, norm1_gain, nb, lp, l, "embed_rms1")
    proj = _matmul(xn, w_in_t, "nt", F32, "proj_in")
    qa, ka, vb = _fox_prep(proj, qg, kg, bf, nb, lp, "fox_prep")
    if ffn_shards is None:
        o_fox, lse = _fox_fwd(qa, ka, vb, nb, lp, "fox_fwd")
    else:
        o_fox, lse, *late = _fox_fwd(qa, ka, vb, nb, lp, "fox_fwd", ride=ffn_shards)
        w_up_t, w_down, w_a_t, w_b_t, w_out = _late_weights(*late)
    o_raw, o_hg, s_save = _hgrn_fwd(proj, lb, hg_out_gain, nb, lp, "hgrn_fwd")
    ya = _matmul(o_hg, w_a_t, "nt", BF16, "branch_a")
    yb = _matmul(o_fox, w_b_t, "nt", BF16, "branch_b")
    merged = _gate_fwd(proj, ya, yb, "gate_fwd")
    h1 = _matmul(merged, w_out, "nn", F32, "mix_out", residual=h0)
    hn = _rms_fwd(h1, norm2_gain, "rms2_fwd")
    up = _matmul(hn, w_up_t, "nt", BF16, "ffn_up")
    act = _conv_fwd(up, conv_w, conv_b, nb, lp, "conv_fwd")
    out = _matmul(act, w_down, "nn", F32, "ffn_down", residual=h1)
    dy, lsum = _loss_head(out, target.reshape(nb * seq, d), nb, lp, l, "loss_head")
    loss = (0.5 / d) * jnp.sum(lsum)

    dact = _matmul(dy, w_down, "nt", BF16, "d_act")
    g_w_down = _matmul(act, dy, "tn", F32, "g_w_down")
    dup, g_conv_w, g_conv_b = _conv_bwd(up, dact, conv_w, conv_b, nb, lp, "conv_bwd")
    dhn = _matmul(dup, w_up_t, "nn", F32, "d_hn")
    g_w_up_t = _matmul(dup, hn, "tn", F32, "g_w_up")
    dh1, g_norm2 = _rms_bwd(h1, norm2_gain, dhn, dy, "rms2_bwd")

    dmerged = _matmul(dh1, w_out, "nt", BF16, "d_merged")
    g_w_out = _matmul(merged, dh1, "tn", F32, "g_w_out")
    dya, dyb, dgab = _gate_bwd(proj, ya, yb, dmerged, "gate_bwd")
    do_hg = _matmul(dya, w_a_t, "nn", F32, "d_o_hg")
    g_w_a_t = _matmul(dya, o_hg, "tn", F32, "g_w_a")
    do_fox = _matmul(dyb, w_b_t, "nn", BF16, "d_o_fox")
    g_w_b_t = _matmul(dyb, o_fox, "tn", F32, "g_w_b")
    dhq, dhf, dhi, dhg, g_hg_gain, g_lb = _hgrn_bwd(proj, o_raw, s_save, do_hg, lb, hg_out_gain, nb, lp, "hgrn_bwd")
    if ffn_shards is None:
        dqs, dkn, dvv, dc0, dc1 = _fox_bwd(qa, ka, vb, do_fox, o_fox, lse, nb, lp, "fox_bwd")
        early = None
    else:
        dqs, dkn, dvv, dc0, dc1, *early = _fox_bwd(qa, ka, vb, do_fox, o_fox, lse, nb, lp, "fox_bwd",
                                                   ride=_early_blocks(g_w_up_t, g_w_down, g_w_out, g_w_a_t, g_w_b_t))
    dcum = jnp.stack([dc0, dc1], axis=2).reshape(nb, FOX_HEADS, lp)
    dcum = jnp.pad(jnp.transpose(dcum, (0, 2, 1)), ((0, 0), (0, 0), (0, LANES - FOX_HEADS))).reshape(m, LANES)
    dfqkv, dff, g_qg, g_kg, g_bf = _fox_prep_bwd(proj, dqs, dkn, dvv, dcum, qg, kg, bf, nb, lp, "fox_prep_bwd")
    dproj = jnp.concatenate([dgab, dfqkv, dhq, dhf, dhi, dhg, dff], axis=1)
    g_w_in_t = _matmul(dproj, xn, "tn", F32, "g_w_in")
    if ffn_shards is None:
        dxn = _matmul(dproj, w_in_t, "nn", F32, "d_xn")
    else:
        blocks_in = _unalign_axis0(g_w_in_t).reshape(N_DEV, -1, d).astype(BF16)
        dxn, r_in = _matmul(dproj, w_in_t, "nn", F32, "d_xn", ride=[blocks_in])
        early = early + [r_in]
    dh0, g_norm1 = _rms_bwd(h0, norm1_gain, dxn, dh1, "rms1_bwd")

    dh0 = dh0.reshape(nb, lp, d)
    grad_x = dh0[:, N_META:l]
    g_meta = jnp.sum(dh0[:, :N_META], axis=0)
    g_q_gain = jnp.sum(g_qg.reshape(FOX_HEADS, FOX_HEAD_DIM), axis=0, keepdims=True)
    g_k_gain = jnp.sum(g_kg.reshape(FOX_HEADS, FOX_HEAD_DIM), axis=0, keepdims=True)
    grads = dict(meta_tokens=g_meta, norm1_gain=g_norm1, w_in_t=g_w_in_t, fox_b_f=g_bf[:, :FOX_HEADS],
                 q_norm_gain=g_q_gain, k_norm_gain=g_k_gain, lb=g_lb, hg_out_gain=g_hg_gain,
                 w_a_t=g_w_a_t, w_b_t=g_w_b_t, w_out=g_w_out, norm2_gain=g_norm2, w_up_t=g_w_up_t,
                 conv_w=g_conv_w, conv_b=g_conv_b, w_down=g_w_down, early=early)
    return loss, grad_x, grads


SMALL = ("norm1_gain", "fox_b_f", "q_norm_gain", "k_norm_gain", "hg_lb_logits", "hg_out_gain", "norm2_gain", "conv_b")
ORDER = ("meta_tokens", "norm1_gain", "w_in", "fox_b_f", "q_norm_gain", "k_norm_gain", "hg_lb_logits", "hg_out_gain",
         "w_branch_a", "w_branch_b", "w_out", "norm2_gain", "w_up", "conv_w", "conv_b", "w_down")


def kernel(x, meta_tokens, norm1_gain, w_in, fox_b_f, q_norm_gain, k_norm_gain, hg_lb_logits, hg_out_gain, w_branch_a, w_branch_b, w_out, norm2_gain, w_up, conv_w, conv_b, w_down, loss_target, m_meta_tokens, m_norm1_gain, m_w_in, m_fox_b_f, m_q_norm_gain, m_k_norm_gain, m_hg_lb_logits, m_hg_out_gain, m_w_branch_a, m_w_branch_b, m_w_out, m_norm2_gain, m_w_up, m_conv_w, m_conv_b, m_w_down, v_meta_tokens, v_norm1_gain, v_w_in, v_fox_b_f, v_q_norm_gain, v_k_norm_gain, v_hg_lb_logits, v_hg_out_gain, v_w_branch_a, v_w_branch_b, v_w_out, v_norm2_gain, v_w_up, v_conv_w, v_conv_b, v_w_down):
    w = dict(meta_tokens=meta_tokens, norm1_gain=norm1_gain, w_in=w_in, fox_b_f=fox_b_f, q_norm_gain=q_norm_gain,
             k_norm_gain=k_norm_gain, hg_lb_logits=hg_lb_logits, hg_out_gain=hg_out_gain, w_branch_a=w_branch_a,
             w_branch_b=w_branch_b, w_out=w_out, norm2_gain=norm2_gain, w_up=w_up, conv_w=conv_w, conv_b=conv_b,
             w_down=w_down)
    mom = dict(meta_tokens=m_meta_tokens, norm1_gain=m_norm1_gain, w_in=m_w_in, fox_b_f=m_fox_b_f,
               q_norm_gain=m_q_norm_gain, k_norm_gain=m_k_norm_gain, hg_lb_logits=m_hg_lb_logits,
               hg_out_gain=m_hg_out_gain, w_branch_a=m_w_branch_a, w_branch_b=m_w_branch_b, w_out=m_w_out,
               norm2_gain=m_norm2_gain, w_up=m_w_up, conv_w=m_conv_w, conv_b=m_conv_b, w_down=m_w_down)
    var = dict(meta_tokens=v_meta_tokens, norm1_gain=v_norm1_gain, w_in=v_w_in, fox_b_f=v_fox_b_f,
               q_norm_gain=v_q_norm_gain, k_norm_gain=v_k_norm_gain, hg_lb_logits=v_hg_lb_logits,
               hg_out_gain=v_hg_out_gain, w_branch_a=v_w_branch_a, w_branch_b=v_w_branch_b, w_out=v_w_out,
               norm2_gain=v_norm2_gain, w_up=v_w_up, conv_w=v_conv_w, conv_b=v_conv_b, w_down=v_w_down)
    d = D_MODEL
    n_in, n_up = w_in.shape[2], w_up.shape[2]
    n_ab, n_meta = w_branch_a.shape[2], meta_tokens.shape[1]

    g_in, g_tiny = _all_gather([w_in[0].T.astype(BF16), _tiny_pack(conv_w[0], meta_tokens)], "gather_weights")
    w_in_t = _align_axis0(g_in.reshape(N_DEV * n_in, d))
    cw_slots, meta_slots = _tiny_unpack(g_tiny, n_up, n_meta)
    conv_w_f = _ffn_interleave(jnp.transpose(cw_slots, (1, 0, 2)).reshape(CONV_WIDTH, -1), 1)
    meta_f = jnp.transpose(meta_slots, (1, 0, 2)).reshape(N_META, -1)
    conv_b_i = _ffn_interleave(conv_b, 1)

    lb, lb_vjp = jax.vjp(_lower_bound, hg_lb_logits)
    loss, grad_x, g = _local_step(
        x, loss_target, meta_f, norm1_gain, w_in_t, fox_b_f, q_norm_gain, k_norm_gain, lb, hg_out_gain,
        None, None, None, norm2_gain, None, conv_w_f, conv_b_i, None,
        ffn_shards=(w_up[0].T.astype(BF16), w_down[0].astype(BF16),
                    jnp.stack([w_branch_a[0].T, w_branch_b[0].T]).astype(BF16), w_out[0].astype(BF16)))

    g["hg_lb_logits"] = lb_vjp(g.pop("lb"))[0]
    g["conv_b"] = _ffn_deinterleave(g["conv_b"], 1)
    gcw = _ffn_deinterleave(g["conv_w"], 1).reshape(CONV_WIDTH, N_DEV, n_up)
    gmeta = g["meta_tokens"].reshape(N_META, N_DEV, n_meta)
    tiny = jnp.concatenate([
        jnp.pad(jnp.transpose(gcw, (1, 0, 2)), ((0, 0), (0, 8 - CONV_WIDTH), (0, TINY_COLS - n_up))),
        jnp.pad(jnp.transpose(gmeta, (1, 0, 2)), ((0, 0), (0, 0), (0, TINY_COLS - n_meta)))], axis=1)
    small_shapes = [w[n].shape for n in SMALL]
    rows_sm = _rows_for(small_shapes, extra=1)
    small = _pack([g[n].reshape(-1) for n in SMALL] + [loss.reshape(1)], rows_sm)
    r_tiny, r_small = _exchange([tiny], [small], "exchange_grads")
    r_up, r_down, r_out, r_ab, r_in = g["early"]

    res = {}
    g_in_s = _slot_sum(r_in, "sum_w_in").T
    res["w_in"] = (g_in_s,) + tuple(_adamw(g_in_s, w_in[0], m_w_in[0], v_w_in[0], "adamw_w_in"))
    g_up_s = _slot_sum(r_up, "sum_w_up").T
    res["w_up"] = (g_up_s,) + tuple(_adamw(g_up_s, w_up[0], m_w_up[0], v_w_up[0], "adamw_w_up"))
    g_ab_s = jnp.swapaxes(_slot_sum(r_ab.reshape(N_DEV, 2 * n_ab, -1), "sum_w_ab").reshape(2, n_ab, -1), 1, 2)
    ab = lambda t: jnp.concatenate([t["w_branch_a"][0], t["w_branch_b"][0]], axis=0)
    o_ab = (g_ab_s.reshape(-1, n_ab),) + tuple(_adamw(g_ab_s.reshape(-1, n_ab), ab(w), ab(mom), ab(var), "adamw_w_ab"))
    half = o_ab[0].shape[0] // 2
    res["w_branch_a"] = tuple(o[:half] for o in o_ab)
    res["w_branch_b"] = tuple(o[half:] for o in o_ab)
    res["w_out"] = tuple(_sum_adamw(r_out, w_out[0], m_w_out[0], v_w_out[0], "adamw_w_out"))
    res["w_down"] = tuple(_sum_adamw(r_down, w_down[0], m_w_down[0], v_w_down[0], "adamw_w_down"))
    tp = lambda t: _tiny_pack(t["conv_w"][0], t["meta_tokens"])
    o_tiny = [_tiny_unpack(o, n_up, n_meta) for o in _sum_adamw(r_tiny, tp(w), tp(mom), tp(var), "adamw_tiny")]
    res["conv_w"] = tuple(o[0] for o in o_tiny)
    res["meta_tokens"] = tuple(o[1] for o in o_tiny)
    zero1 = jnp.zeros((1,), F32)
    sp = lambda t: _pack([t[n].reshape(-1) for n in SMALL] + [zero1], rows_sm)
    o_small = [_unpack(o, small_shapes + [(1,)]) for o in _sum_adamw(r_small, sp(w), sp(mom), sp(var), "adamw_small")]
    for i, n in enumerate(SMALL):
        res[n] = tuple(o[i] for o in o_small)
    loss_all = o_small[0][len(SMALL)].reshape(())

    result = [[res[n][k].reshape(w[n].shape) for n in ORDER] for k in range(4)]
    return (loss_all, grad_x, *result[0], *result[1], *result[2], *result[3])
```

```python
import functools

import jax
import jax.numpy as jnp
import numpy as np
from jax import lax
from jax.experimental import pallas as pl
from jax.experimental.pallas import tpu as pltpu

F32 = jnp.float32
BF16 = jnp.bfloat16

D_MODEL = 1024
N_META = 16
FOX_HEADS = 8
FOX_HEAD_DIM = 64
FOX_WIDTH = FOX_HEADS * FOX_HEAD_DIM
HG_HEADS = 4
HG_DIM = 128
HG_WIDTH = HG_HEADS * HG_DIM
D_FF = 2816
CONV_WIDTH = 3
EPS = 1e-6
IN_COLS = 3 * FOX_WIDTH + FOX_HEADS + 4 * HG_WIDTH + 2 * D_MODEL
N_DEV = 8

ADAM_LR = 0.001
ADAM_B1 = 0.9
ADAM_B2 = 0.999
ADAM_EPS = 1e-08
ADAM_WD = 0.01
ADAM_STEP = 10

LANES = 128
SEQ_BLOCK = 128
SUB = 16
NEG = -1e30
VMEM_LIMIT = 48 * 1024 * 1024

FOX_CB = 2 * D_MODEL // FOX_WIDTH
CB_HQ = (2 * D_MODEL + 3 * FOX_WIDTH) // LANES
CB_HF = CB_HQ + HG_HEADS
CB_HI = CB_HF + HG_HEADS
CB_HG = CB_HI + HG_HEADS
CB_FF = CB_HG + HG_HEADS


def _div_tile(n, target, mult):
    best = None
    for t in range(mult, min(n, target) + 1, mult):
        if n % t == 0:
            best = t
    if best is None:
        best = n
    return best


def _cp(*sem):
    return pltpu.CompilerParams(dimension_semantics=sem, vmem_limit_bytes=VMEM_LIMIT)


def _sigmoid(x):
    return 0.5 * jnp.tanh(0.5 * x) + 0.5


def _dot(a, b, dims, precision=None):
    return lax.dot_general(a, b, (dims, ((), ())), preferred_element_type=F32, precision=precision)


NN = ((1,), (0,))
NT = ((1,), (1,))
TN = ((0,), (0,))
HI = lax.Precision.HIGHEST


MATMUL_VMEM_BUDGET = 30 * 1024 * 1024
MATMUL_MAX_TILE = 2048


def _tile_options(n):
    return [t for t in range(LANES, min(n, MATMUL_MAX_TILE) + 1, LANES) if n % t == 0] or [n]


def _matmul_tiles(m, n, k, a_bytes, b_bytes, o_bytes, has_res):
    tk = _div_tile(k, MATMUL_MAX_TILE, LANES)
    best = None
    for tm in _tile_options(m):
        for tn in _tile_options(n):
            vmem = 2 * (tm * tk * a_bytes + tk * tn * b_bytes) + 2 * tm * tn * o_bytes
            vmem += tm * tn * 4 if (tk < k and o_bytes != 4) else 0
            vmem += 2 * tm * tn * 4 if has_res else 0
            if vmem > MATMUL_VMEM_BUDGET:
                continue
            key = (tm * tn, tn % 256 == 0, tn)
            if best is None or key > best[0]:
                best = (key, tm, tn)
    assert best is not None, (m, n, k)
    return best[1], best[2], tk


def _matmul(a, b, mode, out_dtype, name, residual=None, ride=()):
    if mode == "nn":
        (m, k), (k2, n) = a.shape, b.shape
    elif mode == "nt":
        (m, k), (n, k2) = a.shape, b.shape
    else:
        (k, m), (k2, n) = a.shape, b.shape
    assert k == k2, (a.shape, b.shape, mode)
    has_res = residual is not None
    tm, tn, tk = _matmul_tiles(m, n, k, a.dtype.itemsize, b.dtype.itemsize, jnp.dtype(out_dtype).itemsize, has_res)
    nk = k // tk
    in_place = jnp.dtype(out_dtype) == jnp.dtype(F32)
    if mode == "nn":
        a_spec = pl.BlockSpec((tm, tk), lambda i, j, kk: (i, kk))
        b_spec = pl.BlockSpec((tk, tn), lambda i, j, kk: (kk, j))
        dims = NN
    elif mode == "nt":
        a_spec = pl.BlockSpec((tm, tk), lambda i, j, kk: (i, kk))
        b_spec = pl.BlockSpec((tn, tk), lambda i, j, kk: (j, kk))
        dims = NT
    else:
        a_spec = pl.BlockSpec((tk, tm), lambda i, j, kk: (kk, i))
        b_spec = pl.BlockSpec((tk, tn), lambda i, j, kk: (kk, j))
        dims = TN
    o_spec = pl.BlockSpec((tm, tn), lambda i, j, kk: (i, j))
    grid = (m // tm, n // tn, nk)
    x_arrays, x_shapes, x_sems = _exchange_io(ride, ())
    nx = len(x_arrays)
    n_in = 3 if has_res else 2

    def body(*refs):
        if nx:
            first, last = _grid_ends(grid)
            x_in, x_out = refs[n_in:n_in + nx], refs[n_in + nx + 1:n_in + 2 * nx + 1]
            start, wait = _exchange_copies(x_in, x_out, nx, *refs[n_in + 2 * nx + 1:n_in + 2 * nx + 4])
            pl.when(first)(start)
        compute(*refs)
        if nx:
            pl.when(last)(wait)

    def compute(*refs):
        a_ref, b_ref = refs[0], refs[1]
        r_ref = refs[2] if has_res else None
        o_ref = refs[n_in + nx]
        if nk == 1:
            part = _dot(a_ref[...].astype(BF16), b_ref[...].astype(BF16), dims)
            o_ref[...] = (part + r_ref[...] if has_res else part).astype(o_ref.dtype)
            return
        acc_ref = o_ref if in_place else refs[-1]
        kk = pl.program_id(2)

        @pl.when(kk == 0)
        def _():
            acc_ref[...] = r_ref[...] if (has_res and in_place) else jnp.zeros_like(acc_ref)

        acc_ref[...] += _dot(a_ref[...].astype(BF16), b_ref[...].astype(BF16), dims)

        if not in_place:
            @pl.when(kk == nk - 1)
            def _():
                acc = acc_ref[...]
                if has_res:
                    acc = acc + r_ref[...]
                o_ref[...] = acc.astype(o_ref.dtype)

    in_specs = [a_spec, b_spec] + ([o_spec] if has_res else [])
    args = (a, b) + ((residual,) if has_res else ())
    out_shape = jax.ShapeDtypeStruct((m, n), out_dtype)
    acc = [pltpu.VMEM((tm, tn), F32)] if (nk > 1 and not in_place) else []
    if not nx:
        return pl.pallas_call(
            body, name=name, grid=grid, in_specs=in_specs, out_specs=o_spec, out_shape=out_shape, scratch_shapes=acc,
            compiler_params=_cp("parallel", "parallel", "arbitrary"),
        )(*args)
    return pl.pallas_call(
        body, name=name, grid=grid, in_specs=in_specs + [_HBM] * nx, out_specs=[o_spec] + [_HBM] * nx,
        out_shape=[out_shape] + x_shapes, scratch_shapes=x_sems + acc,
        compiler_params=_cp("arbitrary", "arbitrary", "arbitrary"),
    )(*args, *x_arrays)


def _rms_fwd(x, gain, name):
    m, d = x.shape
    tm = _div_tile(m, 512, 16)

    def body(x_ref, g_ref, o_ref):
        xv = x_ref[...]
        r = lax.rsqrt(jnp.mean(xv * xv, axis=-1, keepdims=True) + EPS)
        o_ref[...] = ((xv * r) * g_ref[...]).astype(o_ref.dtype)

    return pl.pallas_call(
        body, name=name, grid=(m // tm,),
        in_specs=[pl.BlockSpec((tm, d), lambda i: (i, 0)), pl.BlockSpec((1, d), lambda i: (0, 0))],
        out_specs=pl.BlockSpec((tm, d), lambda i: (i, 0)),
        out_shape=jax.ShapeDtypeStruct((m, d), BF16),
        compiler_params=_cp("parallel"),
    )(x, gain)


def _rms_bwd(x, gain, dy, dres, name):
    m, d = x.shape
    tm = _div_tile(m, 256, 8)

    def body(x_ref, g_ref, dy_ref, dr_ref, dx_ref, dg_ref):
        xv = x_ref[...]
        r = lax.rsqrt(jnp.mean(xv * xv, axis=-1, keepdims=True) + EPS)
        nv = xv * r
        dyv = dy_ref[...]
        gdy = dyv * g_ref[...]
        dx_ref[...] = dr_ref[...] + r * (gdy - nv * jnp.mean(gdy * nv, axis=-1, keepdims=True))
        part = jnp.sum(dyv * nv, axis=0, keepdims=True)

        @pl.when(pl.program_id(0) == 0)
        def _():
            dg_ref[...] = part

        @pl.when(pl.program_id(0) > 0)
        def _():
            dg_ref[...] += part

    row = pl.BlockSpec((tm, d), lambda i: (i, 0))
    vec = pl.BlockSpec((1, d), lambda i: (0, 0))
    return pl.pallas_call(
        body, name=name, grid=(m // tm,),
        in_specs=[row, vec, row, row], out_specs=[row, vec],
        out_shape=[jax.ShapeDtypeStruct((m, d), F32), jax.ShapeDtypeStruct((1, d), F32)],
        compiler_params=_cp("arbitrary"),
    )(x, gain, dy, dres)


def _head_stats(xv, lo):
    sq = xv * xv
    s_lo = jnp.sum(jnp.where(lo, sq, 0.0), axis=1, keepdims=True)
    s_hi = jnp.sum(jnp.where(lo, 0.0, sq), axis=1, keepdims=True)
    return jnp.where(lo, s_lo, s_hi) * (1.0 / FOX_HEAD_DIM)


BIAS_LANE = FOX_HEAD_DIM
N_SPLIT = 3


def _split3(c):
    c1 = c.astype(BF16).astype(F32)
    r1 = c - c1
    c2 = r1.astype(BF16).astype(F32)
    c3 = (r1 - c2).astype(BF16).astype(F32)
    return c1, c2, c3


def _fox_prep(proj, qg, kg, bf, nb, lp, name):
    m = proj.shape[0]
    ts = SEQ_BLOCK
    nblk = lp // ts
    scale = FOX_HEAD_DIM ** -0.5

    def body(q_ref, k_ref, v_ref, f_ref, qg_ref, kg_ref, bf_ref, qo_ref, ko_ref, vo_ref, carry_ref):
        lane = lax.broadcasted_iota(jnp.int32, (1, LANES), 1)
        lo = lane < FOX_HEAD_DIM

        @pl.when(pl.program_id(1) == 0)
        def _():
            carry_ref[...] = jnp.zeros_like(carry_ref)

        z = f_ref[...] + bf_ref[...]
        logf = jnp.minimum(z, 0.0) - jnp.log(1.0 + jnp.exp(-jnp.abs(z)))
        logf = jnp.where(lane < FOX_HEADS, logf, 0.0)
        r = lax.broadcasted_iota(jnp.int32, (ts, ts), 0)
        c = lax.broadcasted_iota(jnp.int32, (ts, ts), 1)
        tri = jnp.where(c <= r, 1.0, 0.0).astype(F32)
        cum = _dot(tri, logf, NN, HI) + carry_ref[...]
        carry_ref[...] = cum[ts - 1:ts, :]

        ones = jnp.where((lane >= BIAS_LANE + N_SPLIT) & (lane < BIAS_LANE + 2 * N_SPLIT), 1.0, 0.0)
        ones_k = jnp.where((lane >= BIAS_LANE) & (lane < BIAS_LANE + N_SPLIT), 1.0, 0.0)
        for j in range(FOX_WIDTH // LANES):
            cs = slice(j * LANES, (j + 1) * LANES)
            xq = q_ref[:, cs]
            yq = ((xq * lax.rsqrt(_head_stats(xq, lo) + EPS)) * qg_ref[:, cs]) * scale
            xk = k_ref[:, cs]
            yk = (xk * lax.rsqrt(_head_stats(xk, lo) + EPS)) * kg_ref[:, cs]
            for hh in range(2):
                h = 2 * j + hh
                pieces = _split3(_lane_pick(cum, lane, h))
                qb, kb = ones, ones_k
                for i, piece in enumerate(pieces):
                    qb = jnp.where(lane == BIAS_LANE + i, piece, qb)
                    kb = jnp.where(lane == BIAS_LANE + N_SPLIT + i, -piece, kb)
                yq_h = yq if hh == 0 else pltpu.roll(yq, FOX_HEAD_DIM, 1)
                yk_h = yk if hh == 0 else pltpu.roll(yk, FOX_HEAD_DIM, 1)
                hs = slice(h * LANES, (h + 1) * LANES)
                qo_ref[:, hs] = jnp.where(lo, yq_h, qb).astype(BF16)
                ko_ref[:, hs] = jnp.where(lo, yk_h, kb).astype(BF16)
        vo_ref[...] = v_ref[...].astype(BF16)

    w = FOX_WIDTH
    row = lambda b, i: (b * nblk + i, 0)
    return pl.pallas_call(
        body, name=name, grid=(nb, nblk),
        in_specs=[pl.BlockSpec((ts, w), lambda b, i: (b * nblk + i, FOX_CB)),
                  pl.BlockSpec((ts, w), lambda b, i: (b * nblk + i, FOX_CB + 1)),
                  pl.BlockSpec((ts, w), lambda b, i: (b * nblk + i, FOX_CB + 2)),
                  pl.BlockSpec((ts, LANES), lambda b, i: (b * nblk + i, CB_FF)),
                  pl.BlockSpec((1, w), lambda b, i: (0, 0)),
                  pl.BlockSpec((1, w), lambda b, i: (0, 0)),
                  pl.BlockSpec((1, LANES), lambda b, i: (0, 0))],
        out_specs=[pl.BlockSpec((ts, 2 * w), row), pl.BlockSpec((ts, 2 * w), row), pl.BlockSpec((ts, w), row)],
        out_shape=[jax.ShapeDtypeStruct((m, 2 * w), BF16)] * 2 + [jax.ShapeDtypeStruct((m, w), BF16)],
        scratch_shapes=[pltpu.VMEM((1, LANES), F32)],
        compiler_params=_cp("arbitrary", "arbitrary"),
    )(proj, proj, proj, proj, qg, kg, bf)


def _att_tile(lp):
    return 384 if (lp % 384 == 0 and lp > 384) else 128


def _lane_pick(blk, lane, idx):
    return jnp.sum(jnp.where(lane == idx, blk, 0.0), axis=1, keepdims=True)


def _head_masks():
    lane = lax.broadcasted_iota(jnp.int32, (1, LANES), 1)
    return lane, [(lane >= hh * FOX_HEAD_DIM) & (lane < (hh + 1) * FOX_HEAD_DIM) for hh in range(2)]


def _fox_fwd(qa, ka, vb, nb, lp, name, ride=()):
    m = qa.shape[0]
    tq = _att_tile(lp)
    nq = lp // tq
    npair = FOX_WIDTH // LANES
    grid = (nb, npair, nq)
    r_arrays, r_shapes, r_sems = _exchange_io((), ride)
    nr = len(r_arrays)

    def body(q_ref, k_ref, v_ref, *rest):
        r_in, (o_ref, lse_ref), r_out, sems = rest[:nr], rest[nr:nr + 2], rest[nr + 2:2 * nr + 2], rest[2 * nr + 2:]
        if nr:
            first, last = _grid_ends(grid)
            start, wait = _exchange_copies(r_in, r_out, 0, *sems)
            pl.when(first)(start)
        qi = pl.program_id(2)
        lane, hmasks = _head_masks()
        zero16 = jnp.zeros((), BF16)
        causal = lax.broadcasted_iota(jnp.int32, (tq, 1), 0) >= lax.broadcasted_iota(jnp.int32, (1, tq), 1)
        qs = [q_ref[:, hh * LANES:(hh + 1) * LANES] for hh in range(2)]

        def tile(j, carry, diagonal):
            k0 = pl.multiple_of(j * tq, tq)
            vb = v_ref[pl.ds(k0, tq), :]
            out = []
            for hh in range(2):
                mx, l, acc = carry[3 * hh:3 * hh + 3]
                vz = jnp.where(hmasks[hh], vb, zero16)
                s = _dot(qs[hh], k_ref[pl.ds(k0, tq), hh * LANES:(hh + 1) * LANES], NT)
                if diagonal:
                    s = jnp.where(causal, s, NEG)
                m_new = jnp.maximum(mx, jnp.max(s, axis=1, keepdims=True))
                alpha = jnp.exp(mx - m_new)
                pe = jnp.exp(s - m_new)
                l = alpha * l + jnp.sum(pe, axis=1, keepdims=True)
                acc = alpha * acc + _dot(pe.astype(BF16), vz, NN)
                out += [m_new, l, acc]
            return tuple(out)

        init = (jnp.full((tq, 1), NEG, F32), jnp.zeros((tq, 1), F32), jnp.zeros((tq, LANES), F32)) * 2
        carry = lax.fori_loop(0, qi, lambda j, c: tile(j, c, False), init)
        m0, l0, acc0, m1, l1, acc1 = tile(qi, carry, True)
        o_ref[...] = acc0 / l0 + acc1 / l1
        lse_ref[...] = jnp.where(lane == 0, m0 + jnp.log(l0), jnp.where(lane == 1, m1 + jnp.log(l1), 0.0))
        if nr:
            pl.when(last)(wait)

    return pl.pallas_call(
        body, name=name, grid=grid,
        in_specs=[pl.BlockSpec((tq, 2 * LANES), lambda b, p, i: (b * nq + i, p)),
                  pl.BlockSpec((lp, 2 * LANES), lambda b, p, i: (b, p)),
                  pl.BlockSpec((lp, LANES), lambda b, p, i: (b, p))] + [_HBM] * nr,
        out_specs=[pl.BlockSpec((tq, LANES), lambda b, p, i: (b * nq + i, p)),
                   pl.BlockSpec((None, None, tq, LANES), lambda b, p, i: (b, p, i, 0))] + [_HBM] * nr,
        out_shape=[jax.ShapeDtypeStruct((m, FOX_WIDTH), F32),
                   jax.ShapeDtypeStruct((nb, npair, lp, LANES), F32)] + r_shapes,
        scratch_shapes=r_sems if nr else [],
        compiler_params=_cp(*(["arbitrary"] * 3 if nr else ["parallel", "parallel", "arbitrary"])),
    )(qa, ka, vb, *r_arrays)


def _fox_bwd(qa, ka, vb, do, o, lse, nb, lp, name, ride=()):
    m = qa.shape[0]
    tq = _att_tile(lp)
    nq = lp // tq
    npair = FOX_WIDTH // LANES
    grid = (nb, npair, nq)
    r_arrays, r_shapes, r_sems = _exchange_io(ride, ())
    nr = len(r_arrays)

    def body(k_ref, v_ref, q_ref, do_ref, o_ref, lse_ref, *rest):
        r_in, r_out, sems = rest[:nr], rest[nr + 5:2 * nr + 5], rest[2 * nr + 5:]
        dq_ref, dk_ref, dv_ref, dc0_ref, dc1_ref = rest[nr:nr + 5]
        if nr:
            first, last = _grid_ends(grid)
            start, wait = _exchange_copies(r_in, r_out, nr, *sems)
            pl.when(first)(start)
        j = pl.program_id(2)
        lane, hmasks = _head_masks()
        zero16 = jnp.zeros((), BF16)
        causal = lax.broadcasted_iota(jnp.int32, (tq, 1), 0) >= lax.broadcasted_iota(jnp.int32, (1, tq), 1)

        @pl.when(j == 0)
        def _():
            dq_ref[...] = jnp.zeros_like(dq_ref)

        vv = v_ref[...]
        vzs = [jnp.where(hm, vv, zero16) for hm in hmasks]

        def tile(qi, carry, diagonal):
            dk0, dk1, dv, dc0, dc1 = carry
            q0 = pl.multiple_of(qi * tq, tq)
            dob16 = do_ref[pl.ds(q0, tq), :].astype(BF16)
            ob = o_ref[pl.ds(q0, tq), :]
            lseb = lse_ref[pl.ds(q0, tq), :]
            dks, dcs = [dk0, dk1], [dc0, dc1]
            for hh in range(2):
                hs = slice(hh * LANES, (hh + 1) * LANES)
                q = q_ref[pl.ds(q0, tq), hs]
                doz16 = jnp.where(hmasks[hh], dob16, zero16)
                delta = jnp.sum(doz16.astype(F32) * ob, axis=1, keepdims=True)
                s = _dot(q, k_ref[:, hs], NT) - _lane_pick(lseb, lane, hh)
                if diagonal:
                    s = jnp.where(causal, s, NEG)
                pm = jnp.exp(s)
                ds = pm * (_dot(doz16, vzs[hh], NT) - delta)
                ds16 = ds.astype(BF16)
                dv = dv + _dot(pm.astype(BF16), doz16, TN)
                dks[hh] = dks[hh] + _dot(ds16, q, TN)
                dq_ref[pl.ds(q0, tq), hs] += _dot(ds16, k_ref[:, hs], NN)
                dcs[hh] = dcs[hh] - jnp.sum(ds, axis=0, keepdims=True)
            return dks[0], dks[1], dv, dcs[0], dcs[1]

        zt = jnp.zeros((tq, LANES), F32)
        zr = jnp.zeros((1, tq), F32)
        carry = tile(j, (zt, zt, zt, zr, zr), True)
        dk0, dk1, dv, dc0, dc1 = lax.fori_loop(j + 1, nq, lambda qi, c: tile(qi, c, False), carry)
        dk_ref[:, :LANES] = dk0
        dk_ref[:, LANES:] = dk1
        dv_ref[...] = dv
        dc0_ref[...] = dc0
        dc1_ref[...] = dc1
        if nr:
            pl.when(last)(wait)

    full2 = pl.BlockSpec((lp, 2 * LANES), lambda b, p, j: (b, p))
    full = pl.BlockSpec((lp, LANES), lambda b, p, j: (b, p))
    blk2 = pl.BlockSpec((tq, 2 * LANES), lambda b, p, j: (b * nq + j, p))
    blk = pl.BlockSpec((tq, LANES), lambda b, p, j: (b * nq + j, p))
    dcs = pl.BlockSpec((None, None, 1, tq), lambda b, p, j: (b, p, 0, j))
    return pl.pallas_call(
        body, name=name, grid=grid,
        in_specs=[blk2, blk, full2, full, full,
                  pl.BlockSpec((None, None, lp, LANES), lambda b, p, j: (b, p, 0, 0))] + [_HBM] * nr,
        out_specs=[full2, blk2, blk, dcs, dcs] + [_HBM] * nr,
        out_shape=[jax.ShapeDtypeStruct((m, 2 * FOX_WIDTH), F32)] * 2 + [jax.ShapeDtypeStruct((m, FOX_WIDTH), F32)]
        + [jax.ShapeDtypeStruct((nb, npair, 1, lp), F32)] * 2 + r_shapes,
        scratch_shapes=r_sems if nr else [],
        compiler_params=_cp(*(["arbitrary"] * 3 if nr else ["parallel", "parallel", "arbitrary"])),
    )(ka, vb, qa, do, o, lse, *r_arrays)


def _fox_prep_bwd(proj, dqa, dka, dv, dcum, qg, kg, bf, nb, lp, name):
    m = proj.shape[0]
    ts = SEQ_BLOCK
    nblk = lp // ts
    scale = FOX_HEAD_DIM ** -0.5
    w = FOX_WIDTH
    wo = 3 * w

    def body(q_ref, k_ref, f_ref, dq_ref, dk_ref, dv_ref, dc_ref, qg_ref, kg_ref, bf_ref,
             out_ref, dff_ref, dqg_ref, dkg_ref, dbf_ref, carry_ref):
        first = (pl.program_id(0) == 0) & (pl.program_id(1) == 0)
        lane = lax.broadcasted_iota(jnp.int32, (1, LANES), 1)
        lo = lane < FOX_HEAD_DIM

        @pl.when(first)
        def _():
            dqg_ref[...] = jnp.zeros_like(dqg_ref)
            dkg_ref[...] = jnp.zeros_like(dkg_ref)
            dbf_ref[...] = jnp.zeros_like(dbf_ref)

        def norm_bwd(x, g, dy):
            r = lax.rsqrt(_head_stats(x, lo) + EPS)
            nv = x * r
            gdy = dy * g
            prod = gdy * nv
            s_lo = jnp.sum(jnp.where(lo, prod, 0.0), axis=1, keepdims=True)
            s_hi = jnp.sum(jnp.where(lo, 0.0, prod), axis=1, keepdims=True)
            mean = jnp.where(lo, s_lo, s_hi) * (1.0 / FOX_HEAD_DIM)
            return r * (gdy - nv * mean), jnp.sum(dy * nv, axis=0, keepdims=True)

        def pair(d_ref, jj):
            even = d_ref[:, 2 * jj * LANES:(2 * jj + 1) * LANES]
            odd = d_ref[:, (2 * jj + 1) * LANES:(2 * jj + 2) * LANES]
            return jnp.where(lo, even, pltpu.roll(odd, FOX_HEAD_DIM, 1))

        for jj in range(w // LANES):
            cs = slice(jj * LANES, (jj + 1) * LANES)
            dx, dg = norm_bwd(q_ref[:, cs], qg_ref[:, cs], pair(dq_ref, jj) * scale)
            out_ref[:, cs] = dx.astype(BF16)
            dqg_ref[:, cs] += dg
            dx, dg = norm_bwd(k_ref[:, cs], kg_ref[:, cs], pair(dk_ref, jj))
            out_ref[:, w + jj * LANES:w + (jj + 1) * LANES] = dx.astype(BF16)
            dkg_ref[:, cs] += dg
        out_ref[:, 2 * w:3 * w] = dv_ref[...].astype(BF16)

        @pl.when(pl.program_id(1) == 0)
        def _():
            carry_ref[...] = jnp.zeros_like(carry_ref)

        dc = dc_ref[...]
        r = lax.broadcasted_iota(jnp.int32, (ts, ts), 0)
        c = lax.broadcasted_iota(jnp.int32, (ts, ts), 1)
        triu = jnp.where(c >= r, 1.0, 0.0).astype(F32)
        dlogf = _dot(triu, dc, NN, HI) + carry_ref[...]
        carry_ref[...] += jnp.sum(dc, axis=0, keepdims=True)
        z = f_ref[...] + bf_ref[...]
        dz = jnp.where(lane < FOX_HEADS, dlogf * _sigmoid(-z), 0.0)
        dff_ref[...] = dz.astype(BF16)
        dbf_ref[...] += jnp.sum(dz, axis=0, keepdims=True)

    rev = lambda b, i: (b * nblk + (nblk - 1 - i), 0)
    vec = lambda n: pl.BlockSpec((1, n), lambda b, i: (0, 0))
    return pl.pallas_call(
        body, name=name, grid=(nb, nblk),
        in_specs=[pl.BlockSpec((ts, w), lambda b, i: (b * nblk + (nblk - 1 - i), FOX_CB)),
                  pl.BlockSpec((ts, w), lambda b, i: (b * nblk + (nblk - 1 - i), FOX_CB + 1)),
                  pl.BlockSpec((ts, LANES), lambda b, i: (b * nblk + (nblk - 1 - i), CB_FF)),
                  pl.BlockSpec((ts, 2 * w), rev), pl.BlockSpec((ts, 2 * w), rev), pl.BlockSpec((ts, w), rev),
                  pl.BlockSpec((ts, LANES), rev), vec(w), vec(w), vec(LANES)],
        out_specs=[pl.BlockSpec((ts, wo), rev), pl.BlockSpec((ts, LANES), rev), vec(w), vec(w), vec(LANES)],
        out_shape=[jax.ShapeDtypeStruct((m, wo), BF16), jax.ShapeDtypeStruct((m, LANES), BF16),
                   jax.ShapeDtypeStruct((1, w), F32),
                   jax.ShapeDtypeStruct((1, w), F32), jax.ShapeDtypeStruct((1, LANES), F32)],
        scratch_shapes=[pltpu.VMEM((1, LANES), F32)],
        compiler_params=_cp("arbitrary", "arbitrary"),
    )(proj, proj, proj, dqa, dka, dv, dcum, qg, kg, bf)


def _hg_rows(lp):
    return 3 * SEQ_BLOCK if lp % (3 * SEQ_BLOCK) == 0 else SEQ_BLOCK


def _chunk_masks():
    r = lax.broadcasted_iota(jnp.int32, (SEQ_BLOCK, SEQ_BLOCK), 0)
    c = lax.broadcasted_iota(jnp.int32, (SEQ_BLOCK, SEQ_BLOCK), 1)
    same = (r // SUB) == (c // SUB)
    return r, c, same


def _hg_gates(hf, lb):
    sg = _sigmoid(hf)
    f = lb + (1.0 - lb) * sg
    return sg, f, jnp.log(f), (1.0 - lb) * _sigmoid(-hf)


def _hg_intra_e(g_ref, base, t, srow):
    diff = g_ref[pl.ds(base + t, 1), :] - g_ref[pl.ds(base, SUB), :]
    return jnp.exp(jnp.where(srow <= t, diff, NEG))


def _hgrn_fwd(proj, lb, gain, nb, lp, name):
    m = proj.shape[0]
    tb = _hg_rows(lp)
    nblk = lp // tb
    nsb = tb // SEQ_BLOCK
    ns = SEQ_BLOCK // SUB

    def body(q_ref, f_ref, i_ref, g_ref, lb_ref, gain_ref, oraw_ref, y_ref, ssave_ref,
             st_ref, g_scr, kin_scr, o_scr):
        @pl.when(pl.program_id(2) == 0)
        def _():
            st_ref[...] = jnp.zeros_like(st_ref)

        for hb in range(nsb):
            rows = pl.ds(hb * SEQ_BLOCK, SEQ_BLOCK)
            block(q_ref.at[rows], f_ref.at[rows], i_ref.at[rows], g_ref.at[rows], lb_ref, gain_ref, oraw_ref.at[rows],
                  y_ref.at[rows], ssave_ref.at[hb], st_ref, g_scr.at[rows], kin_scr.at[rows], o_scr.at[rows])

    def block(q_ref, f_ref, i_ref, g_ref, lb_ref, gain_ref, oraw_ref, y_ref, ssave_ref,
              st_ref, g_scr, kin_scr, o_scr):
        ssave_ref[...] = st_ref[...]
        lbv = lb_ref[...]
        _, _, lf, kin = _hg_gates(f_ref[...], lbv)
        r, c, same = _chunk_masks()
        ltri = jnp.where(same & (c <= r), 1.0, 0.0).astype(F32)
        lall = jnp.where(same, 1.0, 0.0).astype(F32)
        g = _dot(ltri, lf, NN, HI)
        gt = _dot(lall, lf, NN, HI)
        g_scr[...] = g
        kin_scr[...] = kin
        qv = q_ref[...]
        qg = (qv * jnp.exp(g)).astype(BF16)
        kg = (kin * jnp.exp(gt - g)).astype(BF16)
        et = jnp.exp(gt)
        srow = lax.broadcasted_iota(jnp.int32, (SUB, 1), 0)
        subs = [slice(cc * SUB, (cc + 1) * SUB) for cc in range(ns)]
        ups = [_dot(i_ref[sl, :].astype(BF16), kg[sl], TN) for sl in subs]
        st = st_ref[...]
        starts = []
        for cc in range(ns):
            starts.append(st)
            st = et[cc * SUB:cc * SUB + 1, :] * st + ups[cc]
        st_ref[...] = st
        for cc, sl in enumerate(subs):
            base = cc * SUB
            kc = kin_scr[sl, :]
            vc = i_ref[sl, :]
            for t in range(SUB):
                e = _hg_intra_e(g_scr, base, t, srow)
                a = jnp.sum((q_ref[pl.ds(base + t, 1), :] * kc) * e, axis=1, keepdims=True)
                o_scr[pl.ds(base + t, 1), :] = jnp.sum(a * vc, axis=0, keepdims=True)
            o_scr[sl, :] += _dot(qg[sl], starts[cc].astype(BF16), NT)
        o = o_scr[...]
        oraw_ref[...] = o
        rr = lax.rsqrt(jnp.mean(o * o, axis=-1, keepdims=True) + EPS)
        hg = g_ref[...]
        y_ref[...] = (((o * rr) * gain_ref[...]) * (hg * _sigmoid(hg))).astype(y_ref.dtype)

    col = lambda cb: pl.BlockSpec((tb, LANES), lambda b, h, i, cb=cb: (b * nblk + i, cb + h))
    out_blk = pl.BlockSpec((tb, LANES), lambda b, h, i: (b * nblk + i, h))
    return pl.pallas_call(
        body, name=name, grid=(nb, HG_HEADS, nblk),
        in_specs=[col(CB_HQ), col(CB_HF), col(CB_HI), col(CB_HG),
                  pl.BlockSpec((1, LANES), lambda b, h, i: (0, h)),
                  pl.BlockSpec((1, LANES), lambda b, h, i: (0, 0))],
        out_specs=[out_blk, out_blk,
                   pl.BlockSpec((None, None, nsb, HG_DIM, HG_DIM), lambda b, h, i: (b, h, i, 0, 0))],
        out_shape=[jax.ShapeDtypeStruct((m, HG_WIDTH), F32), jax.ShapeDtypeStruct((m, HG_WIDTH), BF16),
                   jax.ShapeDtypeStruct((nb, HG_HEADS, nblk * nsb, HG_DIM, HG_DIM), F32)],
        scratch_shapes=[pltpu.VMEM((HG_DIM, HG_DIM), F32), pltpu.VMEM((tb, LANES), F32),
                        pltpu.VMEM((tb, LANES), F32), pltpu.VMEM((tb, LANES), F32)],
        compiler_params=_cp("parallel", "parallel", "arbitrary"),
    )(proj, proj, proj, proj, lb, gain)


def _hgrn_bwd(proj, oraw, ssave, dy, lb, gain, nb, lp, name):
    m = proj.shape[0]
    tb = _hg_rows(lp)
    nblk = lp // tb
    nsb = tb // SEQ_BLOCK
    ns = SEQ_BLOCK // SUB

    def body(q_ref, f_ref, i_ref, g_ref, oraw_ref, ssave_ref, dy_ref, lb_ref, gain_ref,
             dq_ref, df_ref, di_ref, dg_ref, dgain_ref, dlb_ref,
             dst_ref, *scratch):
        hd = pl.program_id(0)
        bb = pl.program_id(1)
        ii = pl.program_id(2)

        @pl.when((hd == 0) & (bb == 0) & (ii == 0))
        def _():
            dgain_ref[...] = jnp.zeros_like(dgain_ref)

        @pl.when((bb == 0) & (ii == 0))
        def _():
            dlb_ref[...] = jnp.zeros_like(dlb_ref)

        @pl.when(ii == 0)
        def _():
            dst_ref[...] = jnp.zeros_like(dst_ref)

        for hb in reversed(range(nsb)):
            rows = pl.ds(hb * SEQ_BLOCK, SEQ_BLOCK)
            block(q_ref.at[rows], f_ref.at[rows], i_ref.at[rows], g_ref.at[rows], oraw_ref.at[rows], ssave_ref.at[hb],
                  dy_ref.at[rows], lb_ref, gain_ref, dq_ref.at[rows], df_ref.at[rows], di_ref.at[rows], dg_ref.at[rows],
                  dgain_ref, dlb_ref, dst_ref, *[sc.at[rows] for sc in scratch])

    def block(q_ref, f_ref, i_ref, g_ref, oraw_ref, ssave_ref, dy_ref, lb_ref, gain_ref,
              dq_ref, df_ref, di_ref, dg_ref, dgain_ref, dlb_ref,
              dst_ref, g_scr, kin_scr, do_scr, dq_scr, dk_scr, dv_scr, dgg_scr):
        gainv = gain_ref[...]
        lbv = lb_ref[...]

        o = oraw_ref[...]
        rr = lax.rsqrt(jnp.mean(o * o, axis=-1, keepdims=True) + EPS)
        nv = o * rr
        hg = g_ref[...]
        sgg = _sigmoid(hg)
        sil = hg * sgg
        dyv = dy_ref[...]
        dg_ref[...] = (dyv * nv * gainv * (sgg * (1.0 + hg * (1.0 - sgg)))).astype(dg_ref.dtype)
        dgain_ref[...] += jnp.sum(dyv * nv * sil, axis=0, keepdims=True)
        dn = dyv * gainv * sil
        do_scr[...] = rr * (dn - nv * jnp.mean(dn * nv, axis=-1, keepdims=True))

        hf = f_ref[...]
        sg, f, lf, kin = _hg_gates(hf, lbv)
        r, c, same = _chunk_masks()
        ltri = jnp.where(same & (c <= r), 1.0, 0.0).astype(F32)
        lall = jnp.where(same, 1.0, 0.0).astype(F32)
        g = _dot(ltri, lf, NN, HI)
        gt = _dot(lall, lf, NN, HI)
        g_scr[...] = g
        kin_scr[...] = kin
        qv = q_ref[...]
        eg = jnp.exp(g)
        ekg = jnp.exp(gt - g)
        qg = qv * eg
        kg = kin * ekg
        qg16 = qg.astype(BF16)
        kg16 = kg.astype(BF16)
        et = jnp.exp(gt)
        subs = [slice(cc * SUB, (cc + 1) * SUB) for cc in range(ns)]
        ups = [_dot(i_ref[sl, :].astype(BF16), kg16[sl], TN) for sl in subs]
        st = ssave_ref[...]
        starts = []
        for cc in range(ns):
            starts.append(st)
            st = et[cc * SUB:cc * SUB + 1, :] * st + ups[cc]
        do16 = do_scr[...].astype(BF16)
        downs = [_dot(do16[sl], qg16[sl], TN) for sl in subs]
        dst = dst_ref[...]
        afters = [None] * ns
        for cc in reversed(range(ns)):
            afters[cc] = dst
            dst = et[cc * SUB:cc * SUB + 1, :] * dst + downs[cc]
        dst_ref[...] = dst

        srow = lax.broadcasted_iota(jnp.int32, (SUB, 1), 0)
        for cc, sl in enumerate(subs):
            base = cc * SUB
            st = starts[cc]
            st16 = st.astype(BF16)
            dst = afters[cc]
            dst16 = dst.astype(BF16)
            doc16 = do16[sl]
            vc = i_ref[sl, :]
            vc16 = vc.astype(BF16)
            kc = kin_scr[sl, :]
            etc = et[base:base + 1, :]
            dqg = _dot(doc16, st16, NN)
            dv_c = _dot(kg16[sl], dst16, NT)
            dkg = _dot(vc16, dst16, NN)
            dgt = jnp.sum(dst * st, axis=0, keepdims=True) * etc
            dq_c = dqg * eg[sl]
            dk_c = dkg * ekg[sl]
            dg_c = dqg * qg[sl] - dkg * kg[sl]
            dgt = dgt + jnp.sum(dkg * kg[sl], axis=0, keepdims=True)
            for t in range(SUB):
                e = _hg_intra_e(g_scr, base, t, srow)
                qt = q_ref[pl.ds(base + t, 1), :]
                dot_t = do_scr[pl.ds(base + t, 1), :]
                a = jnp.sum((qt * kc) * e, axis=1, keepdims=True)
                da = jnp.sum(dot_t * vc, axis=1, keepdims=True)
                dv_c = dv_c + a * dot_t
                w = da * e
                dq_scr[pl.ds(base + t, 1), :] = jnp.sum(w * kc, axis=0, keepdims=True)
                wq = w * qt
                dk_c = dk_c + wq
                dg_c = dg_c - kc * wq
            dq_i = dq_scr[sl, :]
            dg_c = dg_c + qv[sl] * dq_i + jnp.where(srow == SUB - 1, dgt, 0.0)
            dq_scr[sl, :] = dq_c + dq_i
            dk_scr[sl, :] = dk_c
            dv_scr[sl, :] = dv_c
            dgg_scr[sl, :] = dg_c

        utri = jnp.where(same & (c >= r), 1.0, 0.0).astype(F32)
        dlf = _dot(utri, dgg_scr[...], NN, HI)
        dkin = dk_scr[...]
        dsg = sg * (1.0 - sg)
        df_ref[...] = ((dlf / f - dkin) * ((1.0 - lbv) * dsg)).astype(df_ref.dtype)
        dlb_ref[...] += jnp.sum((dlf / f - dkin) * (1.0 - sg), axis=0, keepdims=True)
        dq_ref[...] = dq_scr[...].astype(dq_ref.dtype)
        di_ref[...] = dv_scr[...].astype(di_ref.dtype)

    rowi = lambda b, i: b * nblk + (nblk - 1 - i)
    col = lambda cb: pl.BlockSpec((tb, LANES), lambda h, b, i, cb=cb: (rowi(b, i), cb + h))
    hblk = pl.BlockSpec((tb, LANES), lambda h, b, i: (rowi(b, i), h))
    return pl.pallas_call(
        body, name=name, grid=(HG_HEADS, nb, nblk),
        in_specs=[col(CB_HQ), col(CB_HF), col(CB_HI), col(CB_HG), hblk,
                  pl.BlockSpec((None, None, nsb, HG_DIM, HG_DIM), lambda h, b, i: (b, h, nblk - 1 - i, 0, 0)),
                  hblk,
                  pl.BlockSpec((1, LANES), lambda h, b, i: (0, h)),
                  pl.BlockSpec((1, LANES), lambda h, b, i: (0, 0))],
        out_specs=[hblk, hblk, hblk, hblk,
                   pl.BlockSpec((1, LANES), lambda h, b, i: (0, 0)),
                   pl.BlockSpec((1, LANES), lambda h, b, i: (0, h))],
        out_shape=[jax.ShapeDtypeStruct((m, HG_WIDTH), BF16)] * 4
        + [jax.ShapeDtypeStruct((1, LANES), F32), jax.ShapeDtypeStruct((1, HG_WIDTH), F32)],
        scratch_shapes=[pltpu.VMEM((HG_DIM, HG_DIM), F32)] + [pltpu.VMEM((tb, LANES), F32)] * 7,
        compiler_params=_cp("arbitrary", "arbitrary", "arbitrary"),
    )(proj, proj, proj, proj, oraw, ssave, dy, lb, gain)


def _gate_fwd(proj, ya, yb, name):
    m = proj.shape[0]
    tm = _div_tile(m, 256, 16)

    def body(ga_ref, gb_ref, ya_ref, yb_ref, o_ref):
        ya, yb = ya_ref[...].astype(F32), yb_ref[...].astype(F32)
        o_ref[...] = (_sigmoid(ga_ref[...]) * ya + _sigmoid(gb_ref[...]) * yb).astype(o_ref.dtype)

    row = pl.BlockSpec((tm, D_MODEL), lambda i: (i, 0))
    return pl.pallas_call(
        body, name=name, grid=(m // tm,),
        in_specs=[row, pl.BlockSpec((tm, D_MODEL), lambda i: (i, 1)), row, row],
        out_specs=row, out_shape=jax.ShapeDtypeStruct((m, D_MODEL), BF16),
        compiler_params=_cp("parallel"),
    )(proj, proj, ya, yb)


def _gate_bwd(proj, ya, yb, dm, name):
    m = proj.shape[0]
    tm = _div_tile(m, 256, 16)

    def body(ga_ref, gb_ref, ya_ref, yb_ref, dm_ref, dya_ref, dyb_ref, dg_ref):
        dmv = dm_ref[...].astype(F32)
        sa = _sigmoid(ga_ref[...])
        sb = _sigmoid(gb_ref[...])
        dya_ref[...] = (dmv * sa).astype(BF16)
        dyb_ref[...] = (dmv * sb).astype(BF16)
        dg_ref[:, :D_MODEL] = (dmv * ya_ref[...].astype(F32) * (sa * (1.0 - sa))).astype(BF16)
        dg_ref[:, D_MODEL:] = (dmv * yb_ref[...].astype(F32) * (sb * (1.0 - sb))).astype(BF16)

    row = pl.BlockSpec((tm, D_MODEL), lambda i: (i, 0))
    wide = pl.BlockSpec((tm, 2 * D_MODEL), lambda i: (i, 0))
    return pl.pallas_call(
        body, name=name, grid=(m // tm,),
        in_specs=[row, pl.BlockSpec((tm, D_MODEL), lambda i: (i, 1)), row, row, row],
        out_specs=[row, row, wide],
        out_shape=[jax.ShapeDtypeStruct((m, D_MODEL), BF16)] * 2 + [jax.ShapeDtypeStruct((m, 2 * D_MODEL), BF16)],
        compiler_params=_cp("parallel"),
    )(proj, proj, ya, yb, dm)


CONV_ROWS = 128


def _conv3(x, xprev, w_ref, b_ref, rowi):
    r = x.shape[0]
    x1 = jnp.where(rowi < 1, pltpu.roll(xprev, 1, 0), pltpu.roll(x, 1, 0))
    x2 = jnp.where(rowi < 2, pltpu.roll(xprev, 2, 0), pltpu.roll(x, 2, 0))
    u = w_ref[0:1, :] * x2 + w_ref[1:2, :] * x1 + w_ref[2:3, :] * x + b_ref[...]
    return u, x1, x2


def _conv_fwd(up, cw, cb, nb, lp, name):
    m = up.shape[0]
    nct = D_FF // LANES
    r = CONV_ROWS
    nch = lp // r

    def body(u_ref, w_ref, b_ref, o_ref):
        rowi = lax.broadcasted_iota(jnp.int32, (r, 1), 0)

        def step(i, xp):
            r0 = pl.multiple_of(i * r, r)
            xc = u_ref[pl.ds(r0, r), :].astype(F32)
            u, _, _ = _conv3(xc, xp, w_ref, b_ref, rowi)
            ug, uv = u[:, :LANES], u[:, LANES:]
            o_ref[pl.ds(r0, r), :] = ((ug * _sigmoid(ug)) * uv).astype(o_ref.dtype)
            return xc

        lax.fori_loop(0, nch, step, jnp.zeros((r, 2 * LANES), F32))

    return pl.pallas_call(
        body, name=name, grid=(nb, nct),
        in_specs=[pl.BlockSpec((lp, 2 * LANES), lambda b, c: (b, c)),
                  pl.BlockSpec((CONV_WIDTH, 2 * LANES), lambda b, c: (0, c)),
                  pl.BlockSpec((1, 2 * LANES), lambda b, c: (0, c))],
        out_specs=pl.BlockSpec((lp, LANES), lambda b, c: (b, c)),
        out_shape=jax.ShapeDtypeStruct((m, D_FF), BF16),
        compiler_params=_cp("parallel", "parallel"),
    )(up, cw, cb)


def _conv_bwd(up, dact, cw, cb, nb, lp, name):
    m = up.shape[0]
    nct = D_FF // LANES
    r = CONV_ROWS
    nch = lp // r

    def body(u_ref, da_ref, w_ref, b_ref, dup_ref, dw_ref, db_ref):
        rowi = lax.broadcasted_iota(jnp.int32, (r, 1), 0)
        wv = w_ref[...]

        def step(k, carry):
            dun, dw0, dw1, dw2, dbs = carry
            i = nch - 1 - k
            r0 = pl.multiple_of(i * r, r)
            rp = pl.multiple_of(jnp.maximum(i - 1, 0) * r, r)
            xc = u_ref[pl.ds(r0, r), :].astype(F32)
            xp = u_ref[pl.ds(rp, r), :].astype(F32) * (i > 0).astype(F32)
            u, x1, x2 = _conv3(xc, xp, w_ref, b_ref, rowi)
            ug, uv = u[:, :LANES], u[:, LANES:]
            da = da_ref[pl.ds(r0, r), :].astype(F32)
            sg = _sigmoid(ug)
            du = jnp.concatenate([da * uv * (sg * (1.0 + ug * (1.0 - sg))), da * (ug * sg)], axis=1)
            d1 = jnp.where(rowi >= r - 1, pltpu.roll(dun, r - 1, 0), pltpu.roll(du, r - 1, 0))
            d2 = jnp.where(rowi >= r - 2, pltpu.roll(dun, r - 2, 0), pltpu.roll(du, r - 2, 0))
            dup_ref[pl.ds(r0, r), :] = (wv[2:3, :] * du + wv[1:2, :] * d1 + wv[0:1, :] * d2).astype(dup_ref.dtype)
            dw0 = dw0 + jnp.sum(du * x2, axis=0, keepdims=True)
            dw1 = dw1 + jnp.sum(du * x1, axis=0, keepdims=True)
            dw2 = dw2 + jnp.sum(du * xc, axis=0, keepdims=True)
            dbs = dbs + jnp.sum(du, axis=0, keepdims=True)
            return du, dw0, dw1, dw2, dbs

        z1 = jnp.zeros((1, 2 * LANES), F32)
        _, dw0, dw1, dw2, dbs = lax.fori_loop(0, nch, step, (jnp.zeros((r, 2 * LANES), F32), z1, z1, z1, z1))

        @pl.when(pl.program_id(1) == 0)
        def _():
            dw_ref[...] = jnp.zeros_like(dw_ref)
            db_ref[...] = jnp.zeros_like(db_ref)

        dw_ref[0:1, :] += dw0
        dw_ref[1:2, :] += dw1
        dw_ref[2:3, :] += dw2
        db_ref[...] += dbs

    return pl.pallas_call(
        body, name=name, grid=(nct, nb),
        in_specs=[pl.BlockSpec((lp, 2 * LANES), lambda c, b: (b, c)),
                  pl.BlockSpec((lp, LANES), lambda c, b: (b, c)),
                  pl.BlockSpec((CONV_WIDTH, 2 * LANES), lambda c, b: (0, c)),
                  pl.BlockSpec((1, 2 * LANES), lambda c, b: (0, c))],
        out_specs=[pl.BlockSpec((lp, 2 * LANES), lambda c, b: (b, c)),
                   pl.BlockSpec((CONV_WIDTH, 2 * LANES), lambda c, b: (0, c)),
                   pl.BlockSpec((1, 2 * LANES), lambda c, b: (0, c))],
        out_shape=[jax.ShapeDtypeStruct((m, 2 * D_FF), BF16),
                   jax.ShapeDtypeStruct((CONV_WIDTH, 2 * D_FF), F32),
                   jax.ShapeDtypeStruct((1, 2 * D_FF), F32)],
        compiler_params=_cp("parallel", "arbitrary"),
    )(up, dact, cw, cb)


def _ffn_interleave(a, axis):
    shp = a.shape
    a = a.reshape(shp[:axis] + (2, D_FF // LANES, LANES) + shp[axis + 1:])
    return jnp.swapaxes(a, axis, axis + 1).reshape(shp)


def _ffn_deinterleave(a, axis):
    shp = a.shape
    a = a.reshape(shp[:axis] + (D_FF // LANES, 2, LANES) + shp[axis + 1:])
    return jnp.swapaxes(a, axis, axis + 1).reshape(shp)


def _shifted_rows(prev_ref, cur_ref):
    keep = SEQ_BLOCK - N_META
    return jnp.concatenate([prev_ref[keep:, :], cur_ref[:keep, :]], axis=0)


def _frame_specs(nblk, nfb, d):
    prev = pl.BlockSpec((SEQ_BLOCK, d), lambda b, i: (b * nfb + jnp.clip(i - 1, 0, nfb - 1), 0))
    cur = pl.BlockSpec((SEQ_BLOCK, d), lambda b, i: (b * nfb + jnp.clip(i, 0, nfb - 1), 0))
    return prev, cur


def _embed_rms(x2, meta, gain, nb, lp, l, name):
    d = x2.shape[1]
    tr = SEQ_BLOCK
    nblk = lp // tr
    nfb = (l - N_META) // tr
    m = nb * lp

    def body(prev_ref, cur_ref, meta_ref, g_ref, h_ref, o_ref):
        i = pl.program_id(1)
        t = i * tr + lax.broadcasted_iota(jnp.int32, (tr, 1), 0)
        rows = jnp.where(t < l, _shifted_rows(prev_ref, cur_ref), 0.0)
        head = jnp.concatenate([meta_ref[...], jnp.zeros((tr - N_META, d), F32)], axis=0)
        xv = jnp.where(t < N_META, head, rows)
        h_ref[...] = xv
        r = lax.rsqrt(jnp.mean(xv * xv, axis=-1, keepdims=True) + EPS)
        o_ref[...] = ((xv * r) * g_ref[...]).astype(o_ref.dtype)

    prev, cur = _frame_specs(nblk, nfb, d)
    row = pl.BlockSpec((tr, d), lambda b, i: (b * nblk + i, 0))
    return pl.pallas_call(
        body, name=name, grid=(nb, nblk),
        in_specs=[prev, cur, pl.BlockSpec((N_META, d), lambda b, i: (0, 0)), pl.BlockSpec((1, d), lambda b, i: (0, 0))],
        out_specs=[row, row],
        out_shape=[jax.ShapeDtypeStruct((m, d), F32), jax.ShapeDtypeStruct((m, d), BF16)],
        compiler_params=_cp("parallel", "parallel"),
    )(x2, x2, meta, gain)


def _loss_head(out, tgt2, nb, lp, l, name):
    m, d = out.shape
    tr = SEQ_BLOCK
    nblk = lp // tr
    nfb = (l - N_META) // tr

    def body(o_ref, prev_ref, cur_ref, dy_ref, ls_ref):
        t = pl.program_id(1) * tr + lax.broadcasted_iota(jnp.int32, (tr, 1), 0)
        valid = (t >= N_META) & (t < l)
        err = jnp.where(valid, o_ref[...] - _shifted_rows(prev_ref, cur_ref), 0.0)
        dy_ref[...] = err * (1.0 / d)
        part = jnp.sum(err * err, axis=0, keepdims=True)
        first = (pl.program_id(0) == 0) & (pl.program_id(1) == 0)

        @pl.when(first)
        def _():
            ls_ref[...] = part

        @pl.when(jnp.logical_not(first))
        def _():
            ls_ref[...] += part

    prev, cur = _frame_specs(nblk, nfb, d)
    row = pl.BlockSpec((tr, d), lambda b, i: (b * nblk + i, 0))
    return pl.pallas_call(
        body, name=name, grid=(nb, nblk),
        in_specs=[row, prev, cur], out_specs=[row, pl.BlockSpec((1, d), lambda b, i: (0, 0))],
        out_shape=[jax.ShapeDtypeStruct((m, d), F32), jax.ShapeDtypeStruct((1, d), F32)],
        compiler_params=_cp("arbitrary", "arbitrary"),
    )(out, tgt2, tgt2)


def _adam_math(g, w, mom, var):
    c1 = 1.0 - ADAM_B1 ** ADAM_STEP
    c2 = 1.0 - ADAM_B2 ** ADAM_STEP
    mn = ADAM_B1 * mom + (1.0 - ADAM_B1) * g
    vn = ADAM_B2 * var + (1.0 - ADAM_B2) * (g * g)
    delta = -ADAM_LR * ((mn / c1) / (jnp.sqrt(vn / c2) + ADAM_EPS) + ADAM_WD * w)
    return delta, mn, vn


def _slot_sum(recv, name):
    _, r, c = recv.shape
    tc = _div_tile(c, 256, LANES)

    def body(r_ref, g_ref):
        g = r_ref[0].astype(F32)
        for s in range(1, N_DEV):
            g = g + r_ref[s].astype(F32)
        g_ref[...] = g

    return pl.pallas_call(
        body, name=name, grid=(c // tc,),
        in_specs=[pl.BlockSpec((N_DEV, r, tc), lambda j: (0, 0, j))],
        out_specs=pl.BlockSpec((r, tc), lambda j: (0, j)),
        out_shape=jax.ShapeDtypeStruct((r, c), F32),
        compiler_params=_cp("parallel"),
    )(recv)


def _adamw(g, w, mom, var, name):
    r, c = w.shape
    tr = _div_tile(r, 256, 8)

    def body(g_ref, w_ref, m_ref, v_ref, d_ref, mo_ref, vo_ref):
        d_ref[...], mo_ref[...], vo_ref[...] = _adam_math(g_ref[...], w_ref[...], m_ref[...], v_ref[...])

    row = pl.BlockSpec((tr, c), lambda i: (i, 0))
    return pl.pallas_call(
        body, name=name, grid=(r // tr,), in_specs=[row] * 4, out_specs=[row] * 3,
        out_shape=[jax.ShapeDtypeStruct((r, c), F32)] * 3,
        compiler_params=_cp("parallel"),
    )(g, w, mom, var)


def _sum_adamw(recv, w, mom, var, name):
    r, c = w.shape
    tr = _div_tile(r, 256, 8)

    def body(r_ref, w_ref, m_ref, v_ref, g_ref, d_ref, mo_ref, vo_ref):
        g = r_ref[0].astype(F32)
        for s in range(1, N_DEV):
            g = g + r_ref[s].astype(F32)
        g_ref[...] = g
        d_ref[...], mo_ref[...], vo_ref[...] = _adam_math(g, w_ref[...], m_ref[...], v_ref[...])

    row = pl.BlockSpec((tr, c), lambda i: (i, 0))
    return pl.pallas_call(
        body, name=name, grid=(r // tr,),
        in_specs=[pl.BlockSpec((N_DEV, tr, c), lambda i: (0, i, 0)), row, row, row],
        out_specs=[row] * 4,
        out_shape=[jax.ShapeDtypeStruct((r, c), F32)] * 4,
        compiler_params=_cp("parallel"),
    )(recv, w, mom, var)


_MESH = pl.DeviceIdType.MESH
_HBM = pl.BlockSpec(memory_space=pltpu.HBM)
N_PEER = N_DEV - 1


def _position():
    return lax.axis_index("x"), lax.axis_index("y"), lax.axis_index("c")


def _all_gather(shards, name):
    n = len(shards)

    def body(*refs):
        x_refs, out_refs = refs[:n], refs[n:2 * n]
        send_sems, recv_sems, local_sems = refs[2 * n:]
        x, y, c = _position()
        me, sibling = (x, y, c), (x, y, 1 - c)
        chips = [(1 - x, y), (x, 1 - y), (1 - x, 1 - y)]

        def copy(a, k, block, to, src=None):
            slot = out_refs[a].at[4 * block[0] + 2 * block[1] + block[2]]
            return pltpu.make_async_remote_copy(
                src_ref=slot if src is None else src, dst_ref=slot,
                send_sem=send_sems.at[a * N_PEER + k], recv_sem=recv_sems.at[a * N_PEER + k],
                device_id=to, device_id_type=_MESH)

        mine, sent = [], []
        for a in range(n):
            cp = pltpu.make_async_copy(x_refs[a], out_refs[a].at[4 * x + 2 * y + c], local_sems.at[a])
            cp.start()
            mine.append(cp)
            first = [copy(a, 0, me, sibling, src=x_refs[a])]
            first += [copy(a, 1 + j, me, (*chip, c), src=x_refs[a]) for j, chip in enumerate(chips)]
            for cp in first:
                cp.start()
            sent += first
        for a in range(n):
            for j, chip in enumerate(chips):
                copy(a, 1 + j, (*chip, c), me).wait_recv()
                fwd = copy(a, 4 + j, (*chip, c), sibling)
                fwd.start()
                sent.append(fwd)
        for a in range(n):
            copy(a, 0, sibling, me).wait_recv()
            for j, chip in enumerate(chips):
                copy(a, 4 + j, (*chip, 1 - c), me).wait_recv()
        for cp in sent:
            cp.wait_send()
        for cp in mine:
            cp.wait()

    return pl.pallas_call(
        body, name=name,
        out_shape=[jax.ShapeDtypeStruct((N_DEV,) + a.shape, a.dtype) for a in shards],
        in_specs=[_HBM] * n, out_specs=[_HBM] * n,
        scratch_shapes=[pltpu.SemaphoreType.DMA((n * N_PEER,)), pltpu.SemaphoreType.DMA((n * N_PEER,)),
                        pltpu.SemaphoreType.DMA((n,))],
    )(*shards)


_FLIPS = [(fx, fy, fc) for fx in (0, 1) for fy in (0, 1) for fc in (0, 1)][1:]


def _exchange_copies(in_refs, out_refs, nblk, send_sems, recv_sems, local_sems):
    n = len(in_refs)
    x, y, c = _position()
    me = 4 * x + 2 * y + c

    def peer(f):
        return (1 - x if f[0] else x, 1 - y if f[1] else y, 1 - c if f[2] else c)

    def idx(p):
        return 4 * p[0] + 2 * p[1] + p[2]

    def local(a):
        return pltpu.make_async_copy(in_refs[a].at[me] if a < nblk else in_refs[a], out_refs[a].at[me], local_sems.at[a])

    def remote(a, k, sending):
        p = peer(_FLIPS[k])
        src = in_refs[a].at[idx(p)] if a < nblk else in_refs[a]
        dst = out_refs[a].at[me] if sending else out_refs[a].at[idx(p)]
        return pltpu.make_async_remote_copy(
            src_ref=src, dst_ref=dst, send_sem=send_sems.at[a * N_PEER + k], recv_sem=recv_sems.at[a * N_PEER + k],
            device_id=p, device_id_type=_MESH)

    def start():
        for a in range(n):
            local(a).start()
            for k in range(N_PEER):
                remote(a, k, True).start()

    def wait():
        for a in range(n):
            for k in range(N_PEER):
                remote(a, k, False).wait_recv()
        for a in range(n):
            for k in range(N_PEER):
                remote(a, k, True).wait_send()
            local(a).wait()

    return start, wait


def _exchange_io(blocks, shared):
    arrays = list(blocks) + list(shared)
    n = len(arrays)
    out_shape = [jax.ShapeDtypeStruct(a.shape, a.dtype) for a in blocks]
    out_shape += [jax.ShapeDtypeStruct((N_DEV,) + a.shape, a.dtype) for a in shared]
    sems = [pltpu.SemaphoreType.DMA((n * N_PEER,)), pltpu.SemaphoreType.DMA((n * N_PEER,)), pltpu.SemaphoreType.DMA((n,))]
    return arrays, out_shape, sems


def _exchange(blocks, shared, name):
    arrays, out_shape, sems = _exchange_io(blocks, shared)
    n = len(arrays)

    def body(*refs):
        start, wait = _exchange_copies(refs[:n], refs[n:2 * n], len(blocks), *refs[2 * n:])
        start()
        wait()

    return pl.pallas_call(
        body, name=name, out_shape=out_shape, in_specs=[_HBM] * n, out_specs=[_HBM] * n, scratch_shapes=sems,
    )(*arrays)


def _grid_ends(grid):
    ids = [pl.program_id(i) for i in range(len(grid))]
    first = functools.reduce(jnp.logical_and, [i == 0 for i in ids])
    last = functools.reduce(jnp.logical_and, [i == g - 1 for i, g in zip(ids, grid)])
    return first, last


def _pack(parts, rows):
    flat = jnp.concatenate(parts, axis=-1)
    return jnp.pad(flat, [(0, rows * LANES - flat.shape[-1])]).reshape(rows, LANES)


def _unpack(packed, shapes):
    flat = packed.reshape(-1)
    out, off = [], 0
    for shp in shapes:
        n = int(np.prod(shp))
        out.append(flat[off:off + n].reshape(shp))
        off += n
    return out


def _rows_for(shapes, extra=0):
    n = sum(int(np.prod(s)) for s in shapes) + extra
    return -(-n // (8 * LANES)) * 8


def _lower_bound(logits):
    return jnp.cumsum(jax.nn.softmax(logits.astype(F32), axis=0), axis=0)[0:1]


def _align_axis0(w):
    a, b = 3 * FOX_WIDTH, 3 * FOX_WIDTH + FOX_HEADS
    c = b + 4 * HG_WIDTH
    pad = [(0, LANES - FOX_HEADS)] + [(0, 0)] * (w.ndim - 1)
    return jnp.concatenate([w[c:], w[:a], w[b:c], jnp.pad(w[a:b], pad)], axis=0)


def _unalign_axis0(g):
    a, b = 2 * D_MODEL, 2 * D_MODEL + 3 * FOX_WIDTH
    c = b + 4 * HG_WIDTH
    return jnp.concatenate([g[a:b], g[c:c + FOX_HEADS], g[b:c], g[:a]], axis=0)


TINY_COLS = 768


def _tiny_pack(conv_w_shard, meta_shard):
    cw = jnp.pad(conv_w_shard, ((0, 8 - CONV_WIDTH), (0, TINY_COLS - conv_w_shard.shape[1])))
    mt = jnp.pad(meta_shard, ((0, 0), (0, TINY_COLS - meta_shard.shape[1])))
    return jnp.concatenate([cw, mt], axis=0)


def _tiny_unpack(t, ncw, nmeta):
    return t[..., :CONV_WIDTH, :ncw], t[..., 8:8 + N_META, :nmeta]


def _late_weights(g_up, g_down, g_ab, g_out):
    d = g_up.shape[-1]
    w_a_t = g_ab[:, 0].reshape(-1, g_ab.shape[-1])
    w_b_t = g_ab[:, 1].reshape(-1, g_ab.shape[-1])
    return _ffn_interleave(g_up.reshape(-1, d), 0), g_down.reshape(-1, d), w_a_t, w_b_t, g_out.reshape(-1, d)


def _early_blocks(g_w_up_t, g_w_down, g_w_out, g_w_a_t, g_w_b_t):
    d = g_w_out.shape[-1]
    ab = jnp.stack([g_w_a_t.reshape(N_DEV, -1, g_w_a_t.shape[-1]), g_w_b_t.reshape(N_DEV, -1, g_w_b_t.shape[-1])], axis=1)
    return [_ffn_deinterleave(g_w_up_t, 0).reshape(N_DEV, -1, d).astype(BF16), g_w_down.reshape(N_DEV, -1, d).astype(BF16),
            g_w_out.reshape(N_DEV, -1, d).astype(BF16), ab.astype(BF16)]


def _local_step(x, target, meta, norm1_gain, w_in_t, fox_b_f, q_gain, k_gain, lb, hg_out_gain, w_a_t, w_b_t, w_out,
                norm2_gain, w_up_t, conv_w, conv_b, w_down, ffn_shards=None):
    nb, seq, d = x.shape
    assert seq % SEQ_BLOCK == 0 and N_META < SEQ_BLOCK
    l = seq + N_META
    lp = -(-l // SEQ_BLOCK) * SEQ_BLOCK
    m = nb * lp
    qg = jnp.tile(q_gain, (1, FOX_HEADS))
    kg = jnp.tile(k_gain, (1, FOX_HEADS))
    bf = jnp.pad(fox_b_f, ((0, 0), (0, LANES - FOX_HEADS)))

    h0, xn = _embed_rms(x.reshape(nb * seq, d), meta, norm1_gain, nb, lp, l, "embed_rms1")
    proj = _matmul(xn, w_in_t, "nt", F32, "proj_in")
    qa, ka, vb = _fox_prep(proj, qg, kg, bf, nb, lp, "fox_prep")
    if ffn_shards is None:
        o_fox, lse = _fox_fwd(qa, ka, vb, nb, lp, "fox_fwd")
    else:
        o_fox, lse, *late = _fox_fwd(qa, ka, vb, nb, lp, "fox_fwd", ride=ffn_shards)
        w_up_t, w_down, w_a_t, w_b_t, w_out = _late_weights(*late)
    o_raw, o_hg, s_save = _hgrn_fwd(proj, lb, hg_out_gain, nb, lp, "hgrn_fwd")
    ya = _matmul(o_hg, w_a_t, "nt", BF16, "branch_a")
    yb = _matmul(o_fox, w_b_t, "nt", BF16, "branch_b")
    merged = _gate_fwd(proj, ya, yb, "gate_fwd")
    h1 = _matmul(merged, w_out, "nn", F32, "mix_out", residual=h0)
    hn = _rms_fwd(h1, norm2_gain, "rms2_fwd")
    up = _matmul(hn, w_up_t, "nt", BF16, "ffn_up")
    act = _conv_fwd(up, conv_w, conv_b, nb, lp, "conv_fwd")
    out = _matmul(act, w_down, "nn", F32, "ffn_down", residual=h1)
    dy, lsum = _loss_head(out, target.reshape(nb * seq, d), nb, lp, l, "loss_head")
    loss = (0.5 / d) * jnp.sum(lsum)

    dact = _matmul(dy, w_down, "nt", BF16, "d_act")
    g_w_down = _matmul(act, dy, "tn", F32, "g_w_down")
    dup, g_conv_w, g_conv_b = _conv_bwd(up, dact, conv_w, conv_b, nb, lp, "conv_bwd")
    dhn = _matmul(dup, w_up_t, "nn", F32, "d_hn")
    g_w_up_t = _matmul(dup, hn, "tn", F32, "g_w_up")
    dh1, g_norm2 = _rms_bwd(h1, norm2_gain, dhn, dy, "rms2_bwd")

    dmerged = _matmul(dh1, w_out, "nt", BF16, "d_merged")
    g_w_out = _matmul(merged, dh1, "tn", F32, "g_w_out")
    dya, dyb, dgab = _gate_bwd(proj, ya, yb, dmerged, "gate_bwd")
    do_hg = _matmul(dya, w_a_t, "nn", F32, "d_o_hg")
    g_w_a_t = _matmul(dya, o_hg, "tn", F32, "g_w_a")
    do_fox = _matmul(dyb, w_b_t, "nn", BF16, "d_o_fox")
    g_w_b_t = _matmul(dyb, o_fox, "tn", F32, "g_w_b")
    dhq, dhf, dhi, dhg, g_hg_gain, g_lb = _hgrn_bwd(proj, o_raw, s_save, do_hg, lb, hg_out_gain, nb, lp, "hgrn_bwd")
    if ffn_shards is None:
        dqs, dkn, dvv, dc0, dc1 = _fox_bwd(qa, ka, vb, do_fox, o_fox, lse, nb, lp, "fox_bwd")
        early = None
    else:
        dqs, dkn, dvv, dc0, dc1, *early = _fox_bwd(qa, ka, vb, do_fox, o_fox, lse, nb, lp, "fox_bwd",
                                                   ride=_early_blocks(g_w_up_t, g_w_down, g_w_out, g_w_a_t, g_w_b_t))
    dcum = jnp.stack([dc0, dc1], axis=2).reshape(nb, FOX_HEADS, lp)
    dcum = jnp.pad(jnp.transpose(dcum, (0, 2, 1)), ((0, 0), (0, 0), (0, LANES - FOX_HEADS))).reshape(m, LANES)
    dfqkv, dff, g_qg, g_kg, g_bf = _fox_prep_bwd(proj, dqs, dkn, dvv, dcum, qg, kg, bf, nb, lp, "fox_prep_bwd")
    dproj = jnp.concatenate([dgab, dfqkv, dhq, dhf, dhi, dhg, dff], axis=1)
    g_w_in_t = _matmul(dproj, xn, "tn", F32, "g_w_in")
    if ffn_shards is None:
        dxn = _matmul(dproj, w_in_t, "nn", F32, "d_xn")
    else:
        blocks_in = _unalign_axis0(g_w_in_t).reshape(N_DEV, -1, d).astype(BF16)
        dxn, r_in = _matmul(dproj, w_in_t, "nn", F32, "d_xn", ride=[blocks_in])
        early = early + [r_in]
    dh0, g_norm1 = _rms_bwd(h0, norm1_gain, dxn, dh1, "rms1_bwd")

    dh0 = dh0.reshape(nb, lp, d)
    grad_x = dh0[:, N_META:l]
    g_meta = jnp.sum(dh0[:, :N_META], axis=0)
    g_q_gain = jnp.sum(g_qg.reshape(FOX_HEADS, FOX_HEAD_DIM), axis=0, keepdims=True)
    g_k_gain = jnp.sum(g_kg.reshape(FOX_HEADS, FOX_HEAD_DIM), axis=0, keepdims=True)
    grads = dict(meta_tokens=g_meta, norm1_gain=g_norm1, w_in_t=g_w_in_t, fox_b_f=g_bf[:, :FOX_HEADS],
                 q_norm_gain=g_q_gain, k_norm_gain=g_k_gain, lb=g_lb, hg_out_gain=g_hg_gain,
                 w_a_t=g_w_a_t, w_b_t=g_w_b_t, w_out=g_w_out, norm2_gain=g_norm2, w_up_t=g_w_up_t,
                 conv_w=g_conv_w, conv_b=g_conv_b, w_down=g_w_down, early=early)
    return loss, grad_x, grads


SMALL = ("norm1_gain", "fox_b_f", "q_norm_gain", "k_norm_gain", "hg_lb_logits", "hg_out_gain", "norm2_gain", "conv_b")
ORDER = ("meta_tokens", "norm1_gain", "w_in", "fox_b_f", "q_norm_gain", "k_norm_gain", "hg_lb_logits", "hg_out_gain",
         "w_branch_a", "w_branch_b", "w_out", "norm2_gain", "w_up", "conv_w", "conv_b", "w_down")


def kernel(x, meta_tokens, norm1_gain, w_in, fox_b_f, q_norm_gain, k_norm_gain, hg_lb_logits, hg_out_gain, w_branch_a, w_branch_b, w_out, norm2_gain, w_up, conv_w, conv_b, w_down, loss_target, m_meta_tokens, m_norm1_gain, m_w_in, m_fox_b_f, m_q_norm_gain, m_k_norm_gain, m_hg_lb_logits, m_hg_out_gain, m_w_branch_a, m_w_branch_b, m_w_out, m_norm2_gain, m_w_up, m_conv_w, m_conv_b, m_w_down, v_meta_tokens, v_norm1_gain, v_w_in, v_fox_b_f, v_q_norm_gain, v_k_norm_gain, v_hg_lb_logits, v_hg_out_gain, v_w_branch_a, v_w_branch_b, v_w_out, v_norm2_gain, v_w_up, v_conv_w, v_conv_b, v_w_down):
    w = dict(meta_tokens=meta_tokens, norm1_gain=norm1_gain, w_in=w_in, fox_b_f=fox_b_f, q_norm_gain=q_norm_gain,
             k_norm_gain=k_norm_gain, hg_lb_logits=hg_lb_logits, hg_out_gain=hg_out_gain, w_branch_a=w_branch_a,
             w_branch_b=w_branch_b, w_out=w_out, norm2_gain=norm2_gain, w_up=w_up, conv_w=conv_w, conv_b=conv_b,
             w_down=w_down)
    mom = dict(meta_tokens=m_meta_tokens, norm1_gain=m_norm1_gain, w_in=m_w_in, fox_b_f=m_fox_b_f,
               q_norm_gain=m_q_norm_gain, k_norm_gain=m_k_norm_gain, hg_lb_logits=m_hg_lb_logits,
               hg_out_gain=m_hg_out_gain, w_branch_a=m_w_branch_a, w_branch_b=m_w_branch_b, w_out=m_w_out,
               norm2_gain=m_norm2_gain, w_up=m_w_up, conv_w=m_conv_w, conv_b=m_conv_b, w_down=m_w_down)
    var = dict(meta_tokens=v_meta_tokens, norm1_gain=v_norm1_gain, w_in=v_w_in, fox_b_f=v_fox_b_f,
               q_norm_gain=v_q_norm_gain, k_norm_gain=v_k_norm_gain, hg_lb_logits=v_hg_lb_logits,
               hg_out_gain=v_hg_out_gain, w_branch_a=v_w_branch_a, w_branch_b=v_w_branch_b, w_out=v_w_out,
               norm2_gain=v_norm2_gain, w_up=v_w_up, conv_w=v_conv_w, conv_b=v_conv_b, w_down=v_w_down)
    d = D_MODEL
    n_in, n_up = w_in.shape[2], w_up.shape[2]
    n_ab, n_meta = w_branch_a.shape[2], meta_tokens.shape[1]

    g_in, g_tiny = _all_gather([w_in[0].T.astype(BF16), _tiny_pack(conv_w[0], meta_tokens)], "gather_weights")
    w_in_t = _align_axis0(g_in.reshape(N_DEV * n_in, d))
    cw_slots, meta_slots = _tiny_unpack(g_tiny, n_up, n_meta)
    conv_w_f = _ffn_interleave(jnp.transpose(cw_slots, (1, 0, 2)).reshape(CONV_WIDTH, -1), 1)
    meta_f = jnp.transpose(meta_slots, (1, 0, 2)).reshape(N_META, -1)
    conv_b_i = _ffn_interleave(conv_b, 1)

    lb, lb_vjp = jax.vjp(_lower_bound, hg_lb_logits)
    loss, grad_x, g = _local_step(
        x, loss_target, meta_f, norm1_gain, w_in_t, fox_b_f, q_norm_gain, k_norm_gain, lb, hg_out_gain,
        None, None, None, norm2_gain, None, conv_w_f, conv_b_i, None,
        ffn_shards=(w_up[0].T.astype(BF16), w_down[0].astype(BF16),
                    jnp.stack([w_branch_a[0].T, w_branch_b[0].T]).astype(BF16), w_out[0].astype(BF16)))

    g["hg_lb_logits"] = lb_vjp(g.pop("lb"))[0]
    g["conv_b"] = _ffn_deinterleave(g["conv_b"], 1)
    gcw = _ffn_deinterleave(g["conv_w"], 1).reshape(CONV_WIDTH, N_DEV, n_up)
    gmeta = g["meta_tokens"].reshape(N_META, N_DEV, n_meta)
    tiny = jnp.concatenate([
        jnp.pad(jnp.transpose(gcw, (1, 0, 2)), ((0, 0), (0, 8 - CONV_WIDTH), (0, TINY_COLS - n_up))),
        jnp.pad(jnp.transpose(gmeta, (1, 0, 2)), ((0, 0), (0, 0), (0, TINY_COLS - n_meta)))], axis=1)
    small_shapes = [w[n].shape for n in SMALL]
    rows_sm = _rows_for(small_shapes, extra=1)
    small = _pack([g[n].reshape(-1) for n in SMALL] + [loss.reshape(1)], rows_sm)
    r_tiny, r_small = _exchange([tiny], [small], "exchange_grads")
    r_up, r_down, r_out, r_ab, r_in = g["early"]

    res = {}
    g_in_s = _slot_sum(r_in, "sum_w_in").T
    res["w_in"] = (g_in_s,) + tuple(_adamw(g_in_s, w_in[0], m_w_in[0], v_w_in[0], "adamw_w_in"))
    g_up_s = _slot_sum(r_up, "sum_w_up").T
    res["w_up"] = (g_up_s,) + tuple(_adamw(g_up_s, w_up[0], m_w_up[0], v_w_up[0], "adamw_w_up"))
    g_ab_s = jnp.swapaxes(_slot_sum(r_ab.reshape(N_DEV, 2 * n_ab, -1), "sum_w_ab").reshape(2, n_ab, -1), 1, 2)
    ab = lambda t: jnp.concatenate([t["w_branch_a"][0], t["w_branch_b"][0]], axis=0)
    o_ab = (g_ab_s.reshape(-1, n_ab),) + tuple(_adamw(g_ab_s.reshape(-1, n_ab), ab(w), ab(mom), ab(var), "adamw_w_ab"))
    half = o_ab[0].shape[0] // 2
    res["w_branch_a"] = tuple(o[:half] for o in o_ab)
    res["w_branch_b"] = tuple(o[half:] for o in o_ab)
    res["w_out"] = tuple(_sum_adamw(r_out, w_out[0], m_w_out[0], v_w_out[0], "adamw_w_out"))
    res["w_down"] = tuple(_sum_adamw(r_down, w_down[0], m_w_down[0], v_w_down[0], "adamw_w_down"))
    tp = lambda t: _tiny_pack(t["conv_w"][0], t["meta_tokens"])
    o_tiny = [_tiny_unpack(o, n_up, n_meta) for o in _sum_adamw(r_tiny, tp(w), tp(mom), tp(var), "adamw_tiny")]
    res["conv_w"] = tuple(o[0] for o in o_tiny)
    res["meta_tokens"] = tuple(o[1] for o in o_tiny)
    zero1 = jnp.zeros((1,), F32)
    sp = lambda t: _pack([t[n].reshape(-1) for n in SMALL] + [zero1], rows_sm)
    o_small = [_unpack(o, small_shapes + [(1,)]) for o in _sum_adamw(r_small, sp(w), sp(mom), sp(var), "adamw_small")]
    for i, n in enumerate(SMALL):
        res[n] = tuple(o[i] for o in o_small)
    loss_all = o_small[0][len(SMALL)].reshape(())

    result = [[res[n][k].reshape(w[n].shape) for n in ORDER] for k in range(4)]
    return (loss_all, grad_x, *result[0], *result[1], *result[2], *result[3])
```

```python
import functools

import jax
import jax.numpy as jnp
import numpy as np
from jax import lax
from jax.experimental import pallas as pl
from jax.experimental.pallas import tpu as pltpu

F32 = jnp.float32
BF16 = jnp.bfloat16

D_MODEL = 1024
N_META = 16
FOX_HEADS = 8
FOX_HEAD_DIM = 64
FOX_WIDTH = FOX_HEADS * FOX_HEAD_DIM
HG_HEADS = 4
HG_DIM = 128
HG_WIDTH = HG_HEADS * HG_DIM
D_FF = 2816
CONV_WIDTH = 3
EPS = 1e-6
IN_COLS = 3 * FOX_WIDTH + FOX_HEADS + 4 * HG_WIDTH + 2 * D_MODEL
N_DEV = 8

ADAM_LR = 0.001
ADAM_B1 = 0.9
ADAM_B2 = 0.999
ADAM_EPS = 1e-08
ADAM_WD = 0.01
ADAM_STEP = 10

LANES = 128
SEQ_BLOCK = 128
SUB = 16
NEG = -1e30
VMEM_LIMIT = 48 * 1024 * 1024

FOX_CB = 2 * D_MODEL // FOX_WIDTH
CB_HQ = (2 * D_MODEL + 3 * FOX_WIDTH) // LANES
CB_HF = CB_HQ + HG_HEADS
CB_HI = CB_HF + HG_HEADS
CB_HG = CB_HI + HG_HEADS
CB_FF = CB_HG + HG_HEADS


def _div_tile(n, target, mult):
    best = None
    for t in range(mult, min(n, target) + 1, mult):
        if n % t == 0:
            best = t
    if best is None:
        best = n
    return best


def _cp(*sem):
    return pltpu.CompilerParams(dimension_semantics=sem, vmem_limit_bytes=VMEM_LIMIT)


def _sigmoid(x):
    return 0.5 * jnp.tanh(0.5 * x) + 0.5


def _dot(a, b, dims, precision=None):
    return lax.dot_general(a, b, (dims, ((), ())), preferred_element_type=F32, precision=precision)


NN = ((1,), (0,))
NT = ((1,), (1,))
TN = ((0,), (0,))
HI = lax.Precision.HIGHEST


MATMUL_VMEM_BUDGET = 30 * 1024 * 1024
MATMUL_MAX_TILE = 2048


def _tile_options(n):
    return [t for t in range(LANES, min(n, MATMUL_MAX_TILE) + 1, LANES) if n % t == 0] or [n]


def _matmul_tiles(m, n, k, a_bytes, b_bytes, o_bytes, has_res):
    tk = _div_tile(k, MATMUL_MAX_TILE, LANES)
    best = None
    for tm in _tile_options(m):
        for tn in _tile_options(n):
            vmem = 2 * (tm * tk * a_bytes + tk * tn * b_bytes) + 2 * tm * tn * o_bytes
            vmem += tm * tn * 4 if (tk < k and o_bytes != 4) else 0
            vmem += 2 * tm * tn * 4 if has_res else 0
            if vmem > MATMUL_VMEM_BUDGET:
                continue
            key = (tm * tn, tn % 256 == 0, tn)
            if best is None or key > best[0]:
                best = (key, tm, tn)
    assert best is not None, (m, n, k)
    return best[1], best[2], tk


def _matmul(a, b, mode, out_dtype, name, residual=None, ride=()):
    if mode == "nn":
        (m, k), (k2, n) = a.shape, b.shape
    elif mode == "nt":
        (m, k), (n, k2) = a.shape, b.shape
    else:
        (k, m), (k2, n) = a.shape, b.shape
    assert k == k2, (a.shape, b.shape, mode)
    has_res = residual is not None
    tm, tn, tk = _matmul_tiles(m, n, k, a.dtype.itemsize, b.dtype.itemsize, jnp.dtype(out_dtype).itemsize, has_res)
    nk = k // tk
    in_place = jnp.dtype(out_dtype) == jnp.dtype(F32)
    if mode == "nn":
        a_spec = pl.BlockSpec((tm, tk), lambda i, j, kk: (i, kk))
        b_spec = pl.BlockSpec((tk, tn), lambda i, j, kk: (kk, j))
        dims = NN
    elif mode == "nt":
        a_spec = pl.BlockSpec((tm, tk), lambda i, j, kk: (i, kk))
        b_spec = pl.BlockSpec((tn, tk), lambda i, j, kk: (j, kk))
        dims = NT
    else:
        a_spec = pl.BlockSpec((tk, tm), lambda i, j, kk: (kk, i))
        b_spec = pl.BlockSpec((tk, tn), lambda i, j, kk: (kk, j))
        dims = TN
    o_spec = pl.BlockSpec((tm, tn), lambda i, j, kk: (i, j))
    grid = (m // tm, n // tn, nk)
    x_arrays, x_shapes, x_sems = _exchange_io(ride, ())
    nx = len(x_arrays)
    n_in = 3 if has_res else 2

    def body(*refs):
        if nx:
            first, last = _grid_ends(grid)
            x_in, x_out = refs[n_in:n_in + nx], refs[n_in + nx + 1:n_in + 2 * nx + 1]
            start, wait = _exchange_copies(x_in, x_out, nx, *refs[n_in + 2 * nx + 1:n_in + 2 * nx + 4])
            pl.when(first)(start)
        compute(*refs)
        if nx:
            pl.when(last)(wait)

    def compute(*refs):
        a_ref, b_ref = refs[0], refs[1]
        r_ref = refs[2] if has_res else None
        o_ref = refs[n_in + nx]
        if nk == 1:
            part = _dot(a_ref[...].astype(BF16), b_ref[...].astype(BF16), dims)
            o_ref[...] = (part + r_ref[...] if has_res else part).astype(o_ref.dtype)
            return
        acc_ref = o_ref if in_place else refs[-1]
        kk = pl.program_id(2)

        @pl.when(kk == 0)
        def _():
            acc_ref[...] = r_ref[...] if (has_res and in_place) else jnp.zeros_like(acc_ref)

        acc_ref[...] += _dot(a_ref[...].astype(BF16), b_ref[...].astype(BF16), dims)

        if not in_place:
            @pl.when(kk == nk - 1)
            def _():
                acc = acc_ref[...]
                if has_res:
                    acc = acc + r_ref[...]
                o_ref[...] = acc.astype(o_ref.dtype)

    in_specs = [a_spec, b_spec] + ([o_spec] if has_res else [])
    args = (a, b) + ((residual,) if has_res else ())
    out_shape = jax.ShapeDtypeStruct((m, n), out_dtype)
    acc = [pltpu.VMEM((tm, tn), F32)] if (nk > 1 and not in_place) else []
    if not nx:
        return pl.pallas_call(
            body, name=name, grid=grid, in_specs=in_specs, out_specs=o_spec, out_shape=out_shape, scratch_shapes=acc,
            compiler_params=_cp("parallel", "parallel", "arbitrary"),
        )(*args)
    return pl.pallas_call(
        body, name=name, grid=grid, in_specs=in_specs + [_HBM] * nx, out_specs=[o_spec] + [_HBM] * nx,
        out_shape=[out_shape] + x_shapes, scratch_shapes=x_sems + acc,
        compiler_params=_cp("arbitrary", "arbitrary", "arbitrary"),
    )(*args, *x_arrays)


def _rms_fwd(x, gain, name):
    m, d = x.shape
    tm = _div_tile(m, 512, 16)

    def body(x_ref, g_ref, o_ref):
        xv = x_ref[...]
        r = lax.rsqrt(jnp.mean(xv * xv, axis=-1, keepdims=True) + EPS)
        o_ref[...] = ((xv * r) * g_ref[...]).astype(o_ref.dtype)

    return pl.pallas_call(
        body, name=name, grid=(m // tm,),
        in_specs=[pl.BlockSpec((tm, d), lambda i: (i, 0)), pl.BlockSpec((1, d), lambda i: (0, 0))],
        out_specs=pl.BlockSpec((tm, d), lambda i: (i, 0)),
        out_shape=jax.ShapeDtypeStruct((m, d), BF16),
        compiler_params=_cp("parallel"),
    )(x, gain)


def _rms_bwd(x, gain, dy, dres, name):
    m, d = x.shape
    tm = _div_tile(m, 256, 8)

    def body(x_ref, g_ref, dy_ref, dr_ref, dx_ref, dg_ref):
        xv = x_ref[...]
        r = lax.rsqrt(jnp.mean(xv * xv, axis=-1, keepdims=True) + EPS)
        nv = xv * r
        dyv = dy_ref[...]
        gdy = dyv * g_ref[...]
        dx_ref[...] = dr_ref[...] + r * (gdy - nv * jnp.mean(gdy * nv, axis=-1, keepdims=True))
        part = jnp.sum(dyv * nv, axis=0, keepdims=True)

        @pl.when(pl.program_id(0) == 0)
        def _():
            dg_ref[...] = part

        @pl.when(pl.program_id(0) > 0)
        def _():
            dg_ref[...] += part

    row = pl.BlockSpec((tm, d), lambda i: (i, 0))
    vec = pl.BlockSpec((1, d), lambda i: (0, 0))
    return pl.pallas_call(
        body, name=name, grid=(m // tm,),
        in_specs=[row, vec, row, row], out_specs=[row, vec],
        out_shape=[jax.ShapeDtypeStruct((m, d), F32), jax.ShapeDtypeStruct((1, d), F32)],
        compiler_params=_cp("arbitrary"),
    )(x, gain, dy, dres)


def _head_stats(xv, lo):
    sq = xv * xv
    s_lo = jnp.sum(jnp.where(lo, sq, 0.0), axis=1, keepdims=True)
    s_hi = jnp.sum(jnp.where(lo, 0.0, sq), axis=1, keepdims=True)
    return jnp.where(lo, s_lo, s_hi) * (1.0 / FOX_HEAD_DIM)


BIAS_LANE = FOX_HEAD_DIM
N_SPLIT = 3


def _split3(c):
    c1 = c.astype(BF16).astype(F32)
    r1 = c - c1
    c2 = r1.astype(BF16).astype(F32)
    c3 = (r1 - c2).astype(BF16).astype(F32)
    return c1, c2, c3


def _fox_prep(proj, qg, kg, bf, nb, lp, name):
    m = proj.shape[0]
    ts = SEQ_BLOCK
    tg = _hg_rows(lp)
    nblk = lp // tg
    scale = FOX_HEAD_DIM ** -0.5

    def body(q_ref, k_ref, v_ref, f_ref, qg_ref, kg_ref, bf_ref, qo_ref, ko_ref, vo_ref, carry_ref):
        @pl.when(pl.program_id(1) == 0)
        def _():
            carry_ref[...] = jnp.zeros_like(carry_ref)

        for hb in range(tg // ts):
            rows = pl.ds(hb * ts, ts)
            block(q_ref.at[rows], k_ref.at[rows], v_ref.at[rows], f_ref.at[rows], qg_ref, kg_ref, bf_ref,
                  qo_ref.at[rows], ko_ref.at[rows], vo_ref.at[rows], carry_ref)

    def block(q_ref, k_ref, v_ref, f_ref, qg_ref, kg_ref, bf_ref, qo_ref, ko_ref, vo_ref, carry_ref):
        lane = lax.broadcasted_iota(jnp.int32, (1, LANES), 1)
        lo = lane < FOX_HEAD_DIM
        z = f_ref[...] + bf_ref[...]
        logf = jnp.minimum(z, 0.0) - jnp.log(1.0 + jnp.exp(-jnp.abs(z)))
        logf = jnp.where(lane < FOX_HEADS, logf, 0.0)
        r = lax.broadcasted_iota(jnp.int32, (ts, ts), 0)
        c = lax.broadcasted_iota(jnp.int32, (ts, ts), 1)
        tri = jnp.where(c <= r, 1.0, 0.0).astype(F32)
        cum = _dot(tri, logf, NN, HI) + carry_ref[...]
        carry_ref[...] = cum[ts - 1:ts, :]

        ones = jnp.where((lane >= BIAS_LANE + N_SPLIT) & (lane < BIAS_LANE + 2 * N_SPLIT), 1.0, 0.0)
        ones_k = jnp.where((lane >= BIAS_LANE) & (lane < BIAS_LANE + N_SPLIT), 1.0, 0.0)
        for j in range(FOX_WIDTH // LANES):
            cs = slice(j * LANES, (j + 1) * LANES)
            xq = q_ref[:, cs]
            yq = ((xq * lax.rsqrt(_head_stats(xq, lo) + EPS)) * qg_ref[:, cs]) * scale
            xk = k_ref[:, cs]
            yk = (xk * lax.rsqrt(_head_stats(xk, lo) + EPS)) * kg_ref[:, cs]
            for hh in range(2):
                h = 2 * j + hh
                pieces = _split3(_lane_pick(cum, lane, h))
                qb, kb = ones, ones_k
                for i, piece in enumerate(pieces):
                    qb = jnp.where(lane == BIAS_LANE + i, piece, qb)
                    kb = jnp.where(lane == BIAS_LANE + N_SPLIT + i, -piece, kb)
                yq_h = yq if hh == 0 else pltpu.roll(yq, FOX_HEAD_DIM, 1)
                yk_h = yk if hh == 0 else pltpu.roll(yk, FOX_HEAD_DIM, 1)
                hs = slice(h * LANES, (h + 1) * LANES)
                qo_ref[:, hs] = jnp.where(lo, yq_h, qb).astype(BF16)
                ko_ref[:, hs] = jnp.where(lo, yk_h, kb).astype(BF16)
        vo_ref[...] = v_ref[...].astype(BF16)

    w = FOX_WIDTH
    row = lambda b, i: (b * nblk + i, 0)
    return pl.pallas_call(
        body, name=name, grid=(nb, nblk),
        in_specs=[pl.BlockSpec((tg, w), lambda b, i: (b * nblk + i, FOX_CB)),
                  pl.BlockSpec((tg, w), lambda b, i: (b * nblk + i, FOX_CB + 1)),
                  pl.BlockSpec((tg, w), lambda b, i: (b * nblk + i, FOX_CB + 2)),
                  pl.BlockSpec((tg, LANES), lambda b, i: (b * nblk + i, CB_FF)),
                  pl.BlockSpec((1, w), lambda b, i: (0, 0)),
                  pl.BlockSpec((1, w), lambda b, i: (0, 0)),
                  pl.BlockSpec((1, LANES), lambda b, i: (0, 0))],
        out_specs=[pl.BlockSpec((tg, 2 * w), row), pl.BlockSpec((tg, 2 * w), row), pl.BlockSpec((tg, w), row)],
        out_shape=[jax.ShapeDtypeStruct((m, 2 * w), BF16)] * 2 + [jax.ShapeDtypeStruct((m, w), BF16)],
        scratch_shapes=[pltpu.VMEM((1, LANES), F32)],
        compiler_params=_cp("arbitrary", "arbitrary"),
    )(proj, proj, proj, proj, qg, kg, bf)


def _att_tile(lp):
    return 384 if (lp % 384 == 0 and lp > 384) else 128


def _lane_pick(blk, lane, idx):
    return jnp.sum(jnp.where(lane == idx, blk, 0.0), axis=1, keepdims=True)


def _head_masks():
    lane = lax.broadcasted_iota(jnp.int32, (1, LANES), 1)
    return lane, [(lane >= hh * FOX_HEAD_DIM) & (lane < (hh + 1) * FOX_HEAD_DIM) for hh in range(2)]


def _fox_fwd(qa, ka, vb, nb, lp, name, ride=()):
    m = qa.shape[0]
    tq = _att_tile(lp)
    nq = lp // tq
    npair = FOX_WIDTH // LANES
    grid = (nb, npair, nq)
    r_arrays, r_shapes, r_sems = _exchange_io((), ride)
    nr = len(r_arrays)

    def body(q_ref, k_ref, v_ref, *rest):
        r_in, (o_ref, lse_ref), r_out, sems = rest[:nr], rest[nr:nr + 2], rest[nr + 2:2 * nr + 2], rest[2 * nr + 2:]
        if nr:
            first, last = _grid_ends(grid)
            start, wait = _exchange_copies(r_in, r_out, 0, *sems)
            pl.when(first)(start)
        qi = pl.program_id(2)
        lane, hmasks = _head_masks()
        zero16 = jnp.zeros((), BF16)
        causal = lax.broadcasted_iota(jnp.int32, (tq, 1), 0) >= lax.broadcasted_iota(jnp.int32, (1, tq), 1)
        qs = [q_ref[:, hh * LANES:(hh + 1) * LANES] for hh in range(2)]

        def tile(j, carry, diagonal):
            k0 = pl.multiple_of(j * tq, tq)
            vb = v_ref[pl.ds(k0, tq), :]
            out = []
            for hh in range(2):
                mx, l, acc = carry[3 * hh:3 * hh + 3]
                vz = jnp.where(hmasks[hh], vb, zero16)
                s = _dot(qs[hh], k_ref[pl.ds(k0, tq), hh * LANES:(hh + 1) * LANES], NT)
                if diagonal:
                    s = jnp.where(causal, s, NEG)
                m_new = jnp.maximum(mx, jnp.max(s, axis=1, keepdims=True))
                alpha = jnp.exp(mx - m_new)
                pe = jnp.exp(s - m_new)
                l = alpha * l + jnp.sum(pe, axis=1, keepdims=True)
                acc = alpha * acc + _dot(pe.astype(BF16), vz, NN)
                out += [m_new, l, acc]
            return tuple(out)

        init = (jnp.full((tq, 1), NEG, F32), jnp.zeros((tq, 1), F32), jnp.zeros((tq, LANES), F32)) * 2
        carry = lax.fori_loop(0, qi, lambda j, c: tile(j, c, False), init)
        m0, l0, acc0, m1, l1, acc1 = tile(qi, carry, True)
        o_ref[...] = acc0 / l0 + acc1 / l1
        lse_ref[...] = jnp.where(lane == 0, m0 + jnp.log(l0), jnp.where(lane == 1, m1 + jnp.log(l1), 0.0))
        if nr:
            pl.when(last)(wait)

    return pl.pallas_call(
        body, name=name, grid=grid,
        in_specs=[pl.BlockSpec((tq, 2 * LANES), lambda b, p, i: (b * nq + i, p)),
                  pl.BlockSpec((lp, 2 * LANES), lambda b, p, i: (b, p)),
                  pl.BlockSpec((lp, LANES), lambda b, p, i: (b, p))] + [_HBM] * nr,
        out_specs=[pl.BlockSpec((tq, LANES), lambda b, p, i: (b * nq + i, p)),
                   pl.BlockSpec((None, None, tq, LANES), lambda b, p, i: (b, p, i, 0))] + [_HBM] * nr,
        out_shape=[jax.ShapeDtypeStruct((m, FOX_WIDTH), F32),
                   jax.ShapeDtypeStruct((nb, npair, lp, LANES), F32)] + r_shapes,
        scratch_shapes=r_sems if nr else [],
        compiler_params=_cp(*(["arbitrary"] * 3 if nr else ["parallel", "parallel", "arbitrary"])),
    )(qa, ka, vb, *r_arrays)


def _fox_bwd(qa, ka, vb, do, o, lse, nb, lp, name, ride=()):
    m = qa.shape[0]
    tq = _att_tile(lp)
    nq = lp // tq
    npair = FOX_WIDTH // LANES
    grid = (nb, npair, nq)
    r_arrays, r_shapes, r_sems = _exchange_io(ride, ())
    nr = len(r_arrays)

    def body(k_ref, v_ref, q_ref, do_ref, o_ref, lse_ref, *rest):
        r_in, r_out, sems = rest[:nr], rest[nr + 5:2 * nr + 5], rest[2 * nr + 5:]
        dq_ref, dk_ref, dv_ref, dc0_ref, dc1_ref = rest[nr:nr + 5]
        if nr:
            first, last = _grid_ends(grid)
            start, wait = _exchange_copies(r_in, r_out, nr, *sems)
            pl.when(first)(start)
        j = pl.program_id(2)
        lane, hmasks = _head_masks()
        zero16 = jnp.zeros((), BF16)
        causal = lax.broadcasted_iota(jnp.int32, (tq, 1), 0) >= lax.broadcasted_iota(jnp.int32, (1, tq), 1)

        @pl.when(j == 0)
        def _():
            dq_ref[...] = jnp.zeros_like(dq_ref)

        vv = v_ref[...]
        vzs = [jnp.where(hm, vv, zero16) for hm in hmasks]

        def tile(qi, carry, diagonal):
            dk0, dk1, dv, dc0, dc1 = carry
            q0 = pl.multiple_of(qi * tq, tq)
            dob16 = do_ref[pl.ds(q0, tq), :].astype(BF16)
            ob = o_ref[pl.ds(q0, tq), :]
            lseb = lse_ref[pl.ds(q0, tq), :]
            dks, dcs = [dk0, dk1], [dc0, dc1]
            for hh in range(2):
                hs = slice(hh * LANES, (hh + 1) * LANES)
                q = q_ref[pl.ds(q0, tq), hs]
                doz16 = jnp.where(hmasks[hh], dob16, zero16)
                delta = jnp.sum(doz16.astype(F32) * ob, axis=1, keepdims=True)
                s = _dot(q, k_ref[:, hs], NT) - _lane_pick(lseb, lane, hh)
                if diagonal:
                    s = jnp.where(causal, s, NEG)
                pm = jnp.exp(s)
                ds = pm * (_dot(doz16, vzs[hh], NT) - delta)
                ds16 = ds.astype(BF16)
                dv = dv + _dot(pm.astype(BF16), doz16, TN)
                dks[hh] = dks[hh] + _dot(ds16, q, TN)
                dq_ref[pl.ds(q0, tq), hs] += _dot(ds16, k_ref[:, hs], NN)
                dcs[hh] = dcs[hh] - jnp.sum(ds, axis=0, keepdims=True)
            return dks[0], dks[1], dv, dcs[0], dcs[1]

        zt = jnp.zeros((tq, LANES), F32)
        zr = jnp.zeros((1, tq), F32)
        carry = tile(j, (zt, zt, zt, zr, zr), True)
        dk0, dk1, dv, dc0, dc1 = lax.fori_loop(j + 1, nq, lambda qi, c: tile(qi, c, False), carry)
        dk_ref[:, :LANES] = dk0
        dk_ref[:, LANES:] = dk1
        dv_ref[...] = dv
        dc0_ref[...] = dc0
        dc1_ref[...] = dc1
        if nr:
            pl.when(last)(wait)

    full2 = pl.BlockSpec((lp, 2 * LANES), lambda b, p, j: (b, p))
    full = pl.BlockSpec((lp, LANES), lambda b, p, j: (b, p))
    blk2 = pl.BlockSpec((tq, 2 * LANES), lambda b, p, j: (b * nq + j, p))
    blk = pl.BlockSpec((tq, LANES), lambda b, p, j: (b * nq + j, p))
    dcs = pl.BlockSpec((None, None, 1, tq), lambda b, p, j: (b, p, 0, j))
    return pl.pallas_call(
        body, name=name, grid=grid,
        in_specs=[blk2, blk, full2, full, full,
                  pl.BlockSpec((None, None, lp, LANES), lambda b, p, j: (b, p, 0, 0))] + [_HBM] * nr,
        out_specs=[full2, blk2, blk, dcs, dcs] + [_HBM] * nr,
        out_shape=[jax.ShapeDtypeStruct((m, 2 * FOX_WIDTH), F32)] * 2 + [jax.ShapeDtypeStruct((m, FOX_WIDTH), F32)]
        + [jax.ShapeDtypeStruct((nb, npair, 1, lp), F32)] * 2 + r_shapes,
        scratch_shapes=r_sems if nr else [],
        compiler_params=_cp(*(["arbitrary"] * 3 if nr else ["parallel", "parallel", "arbitrary"])),
    )(ka, vb, qa, do, o, lse, *r_arrays)


def _fox_prep_bwd(proj, dqa, dka, dv, dcum, qg, kg, bf, nb, lp, name):
    m = proj.shape[0]
    ts = SEQ_BLOCK
    tg = _hg_rows(lp)
    nblk = lp // tg
    scale = FOX_HEAD_DIM ** -0.5
    w = FOX_WIDTH
    wo = 3 * w

    def body(q_ref, k_ref, f_ref, dq_ref, dk_ref, dv_ref, dc_ref, qg_ref, kg_ref, bf_ref,
             out_ref, dff_ref, dqg_ref, dkg_ref, dbf_ref, carry_ref):
        @pl.when((pl.program_id(0) == 0) & (pl.program_id(1) == 0))
        def _():
            dqg_ref[...] = jnp.zeros_like(dqg_ref)
            dkg_ref[...] = jnp.zeros_like(dkg_ref)
            dbf_ref[...] = jnp.zeros_like(dbf_ref)

        @pl.when(pl.program_id(1) == 0)
        def _():
            carry_ref[...] = jnp.zeros_like(carry_ref)

        for hb in reversed(range(tg // ts)):
            rows = pl.ds(hb * ts, ts)
            block(q_ref.at[rows], k_ref.at[rows], f_ref.at[rows], dq_ref.at[rows], dk_ref.at[rows], dv_ref.at[rows],
                  dc_ref.at[rows], qg_ref, kg_ref, bf_ref, out_ref.at[rows], dff_ref.at[rows], dqg_ref, dkg_ref, dbf_ref,
                  carry_ref)

    def block(q_ref, k_ref, f_ref, dq_ref, dk_ref, dv_ref, dc_ref, qg_ref, kg_ref, bf_ref,
              out_ref, dff_ref, dqg_ref, dkg_ref, dbf_ref, carry_ref):
        lane = lax.broadcasted_iota(jnp.int32, (1, LANES), 1)
        lo = lane < FOX_HEAD_DIM

        def norm_bwd(x, g, dy):
            r = lax.rsqrt(_head_stats(x, lo) + EPS)
            nv = x * r
            gdy = dy * g
            prod = gdy * nv
            s_lo = jnp.sum(jnp.where(lo, prod, 0.0), axis=1, keepdims=True)
            s_hi = jnp.sum(jnp.where(lo, 0.0, prod), axis=1, keepdims=True)
            mean = jnp.where(lo, s_lo, s_hi) * (1.0 / FOX_HEAD_DIM)
            return r * (gdy - nv * mean), jnp.sum(dy * nv, axis=0, keepdims=True)

        def pair(d_ref, jj):
            even = d_ref[:, 2 * jj * LANES:(2 * jj + 1) * LANES]
            odd = d_ref[:, (2 * jj + 1) * LANES:(2 * jj + 2) * LANES]
            return jnp.where(lo, even, pltpu.roll(odd, FOX_HEAD_DIM, 1))

        for jj in range(w // LANES):
            cs = slice(jj * LANES, (jj + 1) * LANES)
            dx, dg = norm_bwd(q_ref[:, cs], qg_ref[:, cs], pair(dq_ref, jj) * scale)
            out_ref[:, cs] = dx.astype(BF16)
            dqg_ref[:, cs] += dg
            dx, dg = norm_bwd(k_ref[:, cs], kg_ref[:, cs], pair(dk_ref, jj))
            out_ref[:, w + jj * LANES:w + (jj + 1) * LANES] = dx.astype(BF16)
            dkg_ref[:, cs] += dg
        out_ref[:, 2 * w:3 * w] = dv_ref[...].astype(BF16)

        dc = dc_ref[...]
        r = lax.broadcasted_iota(jnp.int32, (ts, ts), 0)
        c = lax.broadcasted_iota(jnp.int32, (ts, ts), 1)
        triu = jnp.where(c >= r, 1.0, 0.0).astype(F32)
        dlogf = _dot(triu, dc, NN, HI) + carry_ref[...]
        carry_ref[...] += jnp.sum(dc, axis=0, keepdims=True)
        z = f_ref[...] + bf_ref[...]
        dz = jnp.where(lane < FOX_HEADS, dlogf * _sigmoid(-z), 0.0)
        dff_ref[...] = dz.astype(BF16)
        dbf_ref[...] += jnp.sum(dz, axis=0, keepdims=True)

    rev = lambda b, i: (b * nblk + (nblk - 1 - i), 0)
    vec = lambda n: pl.BlockSpec((1, n), lambda b, i: (0, 0))
    return pl.pallas_call(
        body, name=name, grid=(nb, nblk),
        in_specs=[pl.BlockSpec((tg, w), lambda b, i: (b * nblk + (nblk - 1 - i), FOX_CB)),
                  pl.BlockSpec((tg, w), lambda b, i: (b * nblk + (nblk - 1 - i), FOX_CB + 1)),
                  pl.BlockSpec((tg, LANES), lambda b, i: (b * nblk + (nblk - 1 - i), CB_FF)),
                  pl.BlockSpec((tg, 2 * w), rev), pl.BlockSpec((tg, 2 * w), rev), pl.BlockSpec((tg, w), rev),
                  pl.BlockSpec((tg, LANES), rev), vec(w), vec(w), vec(LANES)],
        out_specs=[pl.BlockSpec((tg, wo), rev), pl.BlockSpec((tg, LANES), rev), vec(w), vec(w), vec(LANES)],
        out_shape=[jax.ShapeDtypeStruct((m, wo), BF16), jax.ShapeDtypeStruct((m, LANES), BF16),
                   jax.ShapeDtypeStruct((1, w), F32),
                   jax.ShapeDtypeStruct((1, w), F32), jax.ShapeDtypeStruct((1, LANES), F32)],
        scratch_shapes=[pltpu.VMEM((1, LANES), F32)],
        compiler_params=_cp("arbitrary", "arbitrary"),
    )(proj, proj, proj, dqa, dka, dv, dcum, qg, kg, bf)


def _hg_rows(lp):
    return 3 * SEQ_BLOCK if lp % (3 * SEQ_BLOCK) == 0 else SEQ_BLOCK


def _chunk_masks():
    r = lax.broadcasted_iota(jnp.int32, (SEQ_BLOCK, SEQ_BLOCK), 0)
    c = lax.broadcasted_iota(jnp.int32, (SEQ_BLOCK, SEQ_BLOCK), 1)
    same = (r // SUB) == (c // SUB)
    return r, c, same


def _hg_gates(hf, lb):
    sg = _sigmoid(hf)
    f = lb + (1.0 - lb) * sg
    return sg, f, jnp.log(f), (1.0 - lb) * _sigmoid(-hf)


def _hg_intra_e(g_ref, base, t, srow):
    diff = g_ref[pl.ds(base + t, 1), :] - g_ref[pl.ds(base, SUB), :]
    return jnp.exp(jnp.where(srow <= t, diff, NEG))


def _hgrn_fwd(proj, lb, gain, nb, lp, name):
    m = proj.shape[0]
    tb = _hg_rows(lp)
    nblk = lp // tb
    nsb = tb // SEQ_BLOCK
    ns = SEQ_BLOCK // SUB

    def body(q_ref, f_ref, i_ref, g_ref, lb_ref, gain_ref, oraw_ref, y_ref, ssave_ref,
             st_ref, g_scr, kin_scr, o_scr):
        @pl.when(pl.program_id(2) == 0)
        def _():
            st_ref[...] = jnp.zeros_like(st_ref)

        for hb in range(nsb):
            rows = pl.ds(hb * SEQ_BLOCK, SEQ_BLOCK)
            block(q_ref.at[rows], f_ref.at[rows], i_ref.at[rows], g_ref.at[rows], lb_ref, gain_ref, oraw_ref.at[rows],
                  y_ref.at[rows], ssave_ref.at[hb], st_ref, g_scr.at[rows], kin_scr.at[rows], o_scr.at[rows])

    def block(q_ref, f_ref, i_ref, g_ref, lb_ref, gain_ref, oraw_ref, y_ref, ssave_ref,
              st_ref, g_scr, kin_scr, o_scr):
        ssave_ref[...] = st_ref[...]
        lbv = lb_ref[...]
        _, _, lf, kin = _hg_gates(f_ref[...], lbv)
        r, c, same = _chunk_masks()
        ltri = jnp.where(same & (c <= r), 1.0, 0.0).astype(F32)
        lall = jnp.where(same, 1.0, 0.0).astype(F32)
        g = _dot(ltri, lf, NN, HI)
        gt = _dot(lall, lf, NN, HI)
        g_scr[...] = g
        kin_scr[...] = kin
        qv = q_ref[...]
        qg = (qv * jnp.exp(g)).astype(BF16)
        kg = (kin * jnp.exp(gt - g)).astype(BF16)
        et = jnp.exp(gt)
        srow = lax.broadcasted_iota(jnp.int32, (SUB, 1), 0)
        subs = [slice(cc * SUB, (cc + 1) * SUB) for cc in range(ns)]
        ups = [_dot(i_ref[sl, :].astype(BF16), kg[sl], TN) for sl in subs]
        st = st_ref[...]
        starts = []
        for cc in range(ns):
            starts.append(st)
            st = et[cc * SUB:cc * SUB + 1, :] * st + ups[cc]
        st_ref[...] = st
        for cc, sl in enumerate(subs):
            base = cc * SUB
            kc = kin_scr[sl, :]
            vc = i_ref[sl, :]
            for t in range(SUB):
                e = _hg_intra_e(g_scr, base, t, srow)
                a = jnp.sum((q_ref[pl.ds(base + t, 1), :] * kc) * e, axis=1, keepdims=True)
                o_scr[pl.ds(base + t, 1), :] = jnp.sum(a * vc, axis=0, keepdims=True)
            o_scr[sl, :] += _dot(qg[sl], starts[cc].astype(BF16), NT)
        o = o_scr[...]
        oraw_ref[...] = o
        rr = lax.rsqrt(jnp.mean(o * o, axis=-1, keepdims=True) + EPS)
        hg = g_ref[...]
        y_ref[...] = (((o * rr) * gain_ref[...]) * (hg * _sigmoid(hg))).astype(y_ref.dtype)

    col = lambda cb: pl.BlockSpec((tb, LANES), lambda b, h, i, cb=cb: (b * nblk + i, cb + h))
    out_blk = pl.BlockSpec((tb, LANES), lambda b, h, i: (b * nblk + i, h))
    return pl.pallas_call(
        body, name=name, grid=(nb, HG_HEADS, nblk),
        in_specs=[col(CB_HQ), col(CB_HF), col(CB_HI), col(CB_HG),
                  pl.BlockSpec((1, LANES), lambda b, h, i: (0, h)),
                  pl.BlockSpec((1, LANES), lambda b, h, i: (0, 0))],
        out_specs=[out_blk, out_blk,
                   pl.BlockSpec((None, None, nsb, HG_DIM, HG_DIM), lambda b, h, i: (b, h, i, 0, 0))],
        out_shape=[jax.ShapeDtypeStruct((m, HG_WIDTH), F32), jax.ShapeDtypeStruct((m, HG_WIDTH), BF16),
                   jax.ShapeDtypeStruct((nb, HG_HEADS, nblk * nsb, HG_DIM, HG_DIM), F32)],
        scratch_shapes=[pltpu.VMEM((HG_DIM, HG_DIM), F32), pltpu.VMEM((tb, LANES), F32),
                        pltpu.VMEM((tb, LANES), F32), pltpu.VMEM((tb, LANES), F32)],
        compiler_params=_cp("parallel", "parallel", "arbitrary"),
    )(proj, proj, proj, proj, lb, gain)


def _hgrn_bwd(proj, oraw, ssave, dy, lb, gain, nb, lp, name):
    m = proj.shape[0]
    tb = _hg_rows(lp)
    nblk = lp // tb
    nsb = tb // SEQ_BLOCK
    ns = SEQ_BLOCK // SUB

    def body(q_ref, f_ref, i_ref, g_ref, oraw_ref, ssave_ref, dy_ref, lb_ref, gain_ref,
             dq_ref, df_ref, di_ref, dg_ref, dgain_ref, dlb_ref,
             dst_ref, *scratch):
        hd = pl.program_id(0)
        bb = pl.program_id(1)
        ii = pl.program_id(2)

        @pl.when((hd == 0) & (bb == 0) & (ii == 0))
        def _():
            dgain_ref[...] = jnp.zeros_like(dgain_ref)

        @pl.when((bb == 0) & (ii == 0))
        def _():
            dlb_ref[...] = jnp.zeros_like(dlb_ref)

        @pl.when(ii == 0)
        def _():
            dst_ref[...] = jnp.zeros_like(dst_ref)

        for hb in reversed(range(nsb)):
            rows = pl.ds(hb * SEQ_BLOCK, SEQ_BLOCK)
            block(q_ref.at[rows], f_ref.at[rows], i_ref.at[rows], g_ref.at[rows], oraw_ref.at[rows], ssave_ref.at[hb],
                  dy_ref.at[rows], lb_ref, gain_ref, dq_ref.at[rows], df_ref.at[rows], di_ref.at[rows], dg_ref.at[rows],
                  dgain_ref, dlb_ref, dst_ref, *[sc.at[rows] for sc in scratch])

    def block(q_ref, f_ref, i_ref, g_ref, oraw_ref, ssave_ref, dy_ref, lb_ref, gain_ref,
              dq_ref, df_ref, di_ref, dg_ref, dgain_ref, dlb_ref,
              dst_ref, g_scr, kin_scr, do_scr, dq_scr, dk_scr, dv_scr, dgg_scr):
        gainv = gain_ref[...]
        lbv = lb_ref[...]

        o = oraw_ref[...]
        rr = lax.rsqrt(jnp.mean(o * o, axis=-1, keepdims=True) + EPS)
        nv = o * rr
        hg = g_ref[...]
        sgg = _sigmoid(hg)
        sil = hg * sgg
        dyv = dy_ref[...]
        dg_ref[...] = (dyv * nv * gainv * (sgg * (1.0 + hg * (1.0 - sgg)))).astype(dg_ref.dtype)
        dgain_ref[...] += jnp.sum(dyv * nv * sil, axis=0, keepdims=True)
        dn = dyv * gainv * sil
        do_scr[...] = rr * (dn - nv * jnp.mean(dn * nv, axis=-1, keepdims=True))

        hf = f_ref[...]
        sg, f, lf, kin = _hg_gates(hf, lbv)
        r, c, same = _chunk_masks()
        ltri = jnp.where(same & (c <= r), 1.0, 0.0).astype(F32)
        lall = jnp.where(same, 1.0, 0.0).astype(F32)
        g = _dot(ltri, lf, NN, HI)
        gt = _dot(lall, lf, NN, HI)
        g_scr[...] = g
        kin_scr[...] = kin
        qv = q_ref[...]
        eg = jnp.exp(g)
        ekg = jnp.exp(gt - g)
        qg = qv * eg
        kg = kin * ekg
        qg16 = qg.astype(BF16)
        kg16 = kg.astype(BF16)
        et = jnp.exp(gt)
        subs = [slice(cc * SUB, (cc + 1) * SUB) for cc in range(ns)]
        ups = [_dot(i_ref[sl, :].astype(BF16), kg16[sl], TN) for sl in subs]
        st = ssave_ref[...]
        starts = []
        for cc in range(ns):
            starts.append(st)
            st = et[cc * SUB:cc * SUB + 1, :] * st + ups[cc]
        do16 = do_scr[...].astype(BF16)
        downs = [_dot(do16[sl], qg16[sl], TN) for sl in subs]
        dst = dst_ref[...]
        afters = [None] * ns
        for cc in reversed(range(ns)):
            afters[cc] = dst
            dst = et[cc * SUB:cc * SUB + 1, :] * dst + downs[cc]
        dst_ref[...] = dst

        srow = lax.broadcasted_iota(jnp.int32, (SUB, 1), 0)
        for cc, sl in enumerate(subs):
            base = cc * SUB
            st = starts[cc]
            st16 = st.astype(BF16)
            dst = afters[cc]
            dst16 = dst.astype(BF16)
            doc16 = do16[sl]
            vc = i_ref[sl, :]
            vc16 = vc.astype(BF16)
            kc = kin_scr[sl, :]
            etc = et[base:base + 1, :]
            dqg = _dot(doc16, st16, NN)
            dv_c = _dot(kg16[sl], dst16, NT)
            dkg = _dot(vc16, dst16, NN)
            dgt = jnp.sum(dst * st, axis=0, keepdims=True) * etc
            dq_c = dqg * eg[sl]
            dk_c = dkg * ekg[sl]
            dg_c = dqg * qg[sl] - dkg * kg[sl]
            dgt = dgt + jnp.sum(dkg * kg[sl], axis=0, keepdims=True)
            for t in range(SUB):
                e = _hg_intra_e(g_scr, base, t, srow)
                qt = q_ref[pl.ds(base + t, 1), :]
                dot_t = do_scr[pl.ds(base + t, 1), :]
                a = jnp.sum((qt * kc) * e, axis=1, keepdims=True)
                da = jnp.sum(dot_t * vc, axis=1, keepdims=True)
                dv_c = dv_c + a * dot_t
                w = da * e
                dq_scr[pl.ds(base + t, 1), :] = jnp.sum(w * kc, axis=0, keepdims=True)
                wq = w * qt
                dk_c = dk_c + wq
                dg_c = dg_c - kc * wq
            dq_i = dq_scr[sl, :]
            dg_c = dg_c + qv[sl] * dq_i + jnp.where(srow == SUB - 1, dgt, 0.0)
            dq_scr[sl, :] = dq_c + dq_i
            dk_scr[sl, :] = dk_c
            dv_scr[sl, :] = dv_c
            dgg_scr[sl, :] = dg_c

        utri = jnp.where(same & (c >= r), 1.0, 0.0).astype(F32)
        dlf = _dot(utri, dgg_scr[...], NN, HI)
        dkin = dk_scr[...]
        dsg = sg * (1.0 - sg)
        df_ref[...] = ((dlf / f - dkin) * ((1.0 - lbv) * dsg)).astype(df_ref.dtype)
        dlb_ref[...] += jnp.sum((dlf / f - dkin) * (1.0 - sg), axis=0, keepdims=True)
        dq_ref[...] = dq_scr[...].astype(dq_ref.dtype)
        di_ref[...] = dv_scr[...].astype(di_ref.dtype)

    rowi = lambda b, i: b * nblk + (nblk - 1 - i)
    col = lambda cb: pl.BlockSpec((tb, LANES), lambda h, b, i, cb=cb: (rowi(b, i), cb + h))
    hblk = pl.BlockSpec((tb, LANES), lambda h, b, i: (rowi(b, i), h))
    return pl.pallas_call(
        body, name=name, grid=(HG_HEADS, nb, nblk),
        in_specs=[col(CB_HQ), col(CB_HF), col(CB_HI), col(CB_HG), hblk,
                  pl.BlockSpec((None, None, nsb, HG_DIM, HG_DIM), lambda h, b, i: (b, h, nblk - 1 - i, 0, 0)),
                  hblk,
                  pl.BlockSpec((1, LANES), lambda h, b, i: (0, h)),
                  pl.BlockSpec((1, LANES), lambda h, b, i: (0, 0))],
        out_specs=[hblk, hblk, hblk, hblk,
                   pl.BlockSpec((1, LANES), lambda h, b, i: (0, 0)),
                   pl.BlockSpec((1, LANES), lambda h, b, i: (0, h))],
        out_shape=[jax.ShapeDtypeStruct((m, HG_WIDTH), BF16)] * 4
        + [jax.ShapeDtypeStruct((1, LANES), F32), jax.ShapeDtypeStruct((1, HG_WIDTH), F32)],
        scratch_shapes=[pltpu.VMEM((HG_DIM, HG_DIM), F32)] + [pltpu.VMEM((tb, LANES), F32)] * 7,
        compiler_params=_cp("arbitrary", "arbitrary", "arbitrary"),
    )(proj, proj, proj, proj, oraw, ssave, dy, lb, gain)


def _gate_fwd(proj, ya, yb, name):
    m = proj.shape[0]
    tm = _div_tile(m, 256, 16)

    def body(ga_ref, gb_ref, ya_ref, yb_ref, o_ref):
        ya, yb = ya_ref[...].astype(F32), yb_ref[...].astype(F32)
        o_ref[...] = (_sigmoid(ga_ref[...]) * ya + _sigmoid(gb_ref[...]) * yb).astype(o_ref.dtype)

    row = pl.BlockSpec((tm, D_MODEL), lambda i: (i, 0))
    return pl.pallas_call(
        body, name=name, grid=(m // tm,),
        in_specs=[row, pl.BlockSpec((tm, D_MODEL), lambda i: (i, 1)), row, row],
        out_specs=row, out_shape=jax.ShapeDtypeStruct((m, D_MODEL), BF16),
        compiler_params=_cp("parallel"),
    )(proj, proj, ya, yb)


def _gate_bwd(proj, ya, yb, dm, name):
    m = proj.shape[0]
    tm = _div_tile(m, 256, 16)

    def body(ga_ref, gb_ref, ya_ref, yb_ref, dm_ref, dya_ref, dyb_ref, dg_ref):
        dmv = dm_ref[...].astype(F32)
        sa = _sigmoid(ga_ref[...])
        sb = _sigmoid(gb_ref[...])
        dya_ref[...] = (dmv * sa).astype(BF16)
        dyb_ref[...] = (dmv * sb).astype(BF16)
        dg_ref[:, :D_MODEL] = (dmv * ya_ref[...].astype(F32) * (sa * (1.0 - sa))).astype(BF16)
        dg_ref[:, D_MODEL:] = (dmv * yb_ref[...].astype(F32) * (sb * (1.0 - sb))).astype(BF16)

    row = pl.BlockSpec((tm, D_MODEL), lambda i: (i, 0))
    wide = pl.BlockSpec((tm, 2 * D_MODEL), lambda i: (i, 0))
    return pl.pallas_call(
        body, name=name, grid=(m // tm,),
        in_specs=[row, pl.BlockSpec((tm, D_MODEL), lambda i: (i, 1)), row, row, row],
        out_specs=[row, row, wide],
        out_shape=[jax.ShapeDtypeStruct((m, D_MODEL), BF16)] * 2 + [jax.ShapeDtypeStruct((m, 2 * D_MODEL), BF16)],
        compiler_params=_cp("parallel"),
    )(proj, proj, ya, yb, dm)


CONV_ROWS = 128


def _conv3(x, xprev, w_ref, b_ref, rowi):
    r = x.shape[0]
    x1 = jnp.where(rowi < 1, pltpu.roll(xprev, 1, 0), pltpu.roll(x, 1, 0))
    x2 = jnp.where(rowi < 2, pltpu.roll(xprev, 2, 0), pltpu.roll(x, 2, 0))
    u = w_ref[0:1, :] * x2 + w_ref[1:2, :] * x1 + w_ref[2:3, :] * x + b_ref[...]
    return u, x1, x2


def _conv_fwd(up, cw, cb, nb, lp, name):
    m = up.shape[0]
    nct = D_FF // LANES
    r = CONV_ROWS
    nch = lp // r

    def body(u_ref, w_ref, b_ref, o_ref):
        rowi = lax.broadcasted_iota(jnp.int32, (r, 1), 0)

        def step(i, xp):
            r0 = pl.multiple_of(i * r, r)
            xc = u_ref[pl.ds(r0, r), :].astype(F32)
            u, _, _ = _conv3(xc, xp, w_ref, b_ref, rowi)
            ug, uv = u[:, :LANES], u[:, LANES:]
            o_ref[pl.ds(r0, r), :] = ((ug * _sigmoid(ug)) * uv).astype(o_ref.dtype)
            return xc

        lax.fori_loop(0, nch, step, jnp.zeros((r, 2 * LANES), F32))

    return pl.pallas_call(
        body, name=name, grid=(nb, nct),
        in_specs=[pl.BlockSpec((lp, 2 * LANES), lambda b, c: (b, c)),
                  pl.BlockSpec((CONV_WIDTH, 2 * LANES), lambda b, c: (0, c)),
                  pl.BlockSpec((1, 2 * LANES), lambda b, c: (0, c))],
        out_specs=pl.BlockSpec((lp, LANES), lambda b, c: (b, c)),
        out_shape=jax.ShapeDtypeStruct((m, D_FF), BF16),
        compiler_params=_cp("parallel", "parallel"),
    )(up, cw, cb)


def _conv_bwd(up, dact, cw, cb, nb, lp, name):
    m = up.shape[0]
    nct = D_FF // LANES
    r = CONV_ROWS
    nch = lp // r

    def body(u_ref, da_ref, w_ref, b_ref, dup_ref, dw_ref, db_ref):
        rowi = lax.broadcasted_iota(jnp.int32, (r, 1), 0)
        wv = w_ref[...]

        def step(k, carry):
            dun, dw0, dw1, dw2, dbs = carry
            i = nch - 1 - k
            r0 = pl.multiple_of(i * r, r)
            rp = pl.multiple_of(jnp.maximum(i - 1, 0) * r, r)
            xc = u_ref[pl.ds(r0, r), :].astype(F32)
            xp = u_ref[pl.ds(rp, r), :].astype(F32) * (i > 0).astype(F32)
            u, x1, x2 = _conv3(xc, xp, w_ref, b_ref, rowi)
            ug, uv = u[:, :LANES], u[:, LANES:]
            da = da_ref[pl.ds(r0, r), :].astype(F32)
            sg = _sigmoid(ug)
            du = jnp.concatenate([da * uv * (sg * (1.0 + ug * (1.0 - sg))), da * (ug * sg)], axis=1)
            d1 = jnp.where(rowi >= r - 1, pltpu.roll(dun, r - 1, 0), pltpu.roll(du, r - 1, 0))
            d2 = jnp.where(rowi >= r - 2, pltpu.roll(dun, r - 2, 0), pltpu.roll(du, r - 2, 0))
            dup_ref[pl.ds(r0, r), :] = (wv[2:3, :] * du + wv[1:2, :] * d1 + wv[0:1, :] * d2).astype(dup_ref.dtype)
            dw0 = dw0 + jnp.sum(du * x2, axis=0, keepdims=True)
            dw1 = dw1 + jnp.sum(du * x1, axis=0, keepdims=True)
            dw2 = dw2 + jnp.sum(du * xc, axis=0, keepdims=True)
            dbs = dbs + jnp.sum(du, axis=0, keepdims=True)
            return du, dw0, dw1, dw2, dbs

        z1 = jnp.zeros((1, 2 * LANES), F32)
        _, dw0, dw1, dw2, dbs = lax.fori_loop(0, nch, step, (jnp.zeros((r, 2 * LANES), F32), z1, z1, z1, z1))

        @pl.when(pl.program_id(1) == 0)
        def _():
            dw_ref[...] = jnp.zeros_like(dw_ref)
            db_ref[...] = jnp.zeros_like(db_ref)

        dw_ref[0:1, :] += dw0
        dw_ref[1:2, :] += dw1
        dw_ref[2:3, :] += dw2
        db_ref[...] += dbs

    return pl.pallas_call(
        body, name=name, grid=(nct, nb),
        in_specs=[pl.BlockSpec((lp, 2 * LANES), lambda c, b: (b, c)),
                  pl.BlockSpec((lp, LANES), lambda c, b: (b, c)),
                  pl.BlockSpec((CONV_WIDTH, 2 * LANES), lambda c, b: (0, c)),
                  pl.BlockSpec((1, 2 * LANES), lambda c, b: (0, c))],
        out_specs=[pl.BlockSpec((lp, 2 * LANES), lambda c, b: (b, c)),
                   pl.BlockSpec((CONV_WIDTH, 2 * LANES), lambda c, b: (0, c)),
                   pl.BlockSpec((1, 2 * LANES), lambda c, b: (0, c))],
        out_shape=[jax.ShapeDtypeStruct((m, 2 * D_FF), BF16),
                   jax.ShapeDtypeStruct((CONV_WIDTH, 2 * D_FF), F32),
                   jax.ShapeDtypeStruct((1, 2 * D_FF), F32)],
        compiler_params=_cp("parallel", "arbitrary"),
    )(up, dact, cw, cb)


def _ffn_interleave(a, axis):
    shp = a.shape
    a = a.reshape(shp[:axis] + (2, D_FF // LANES, LANES) + shp[axis + 1:])
    return jnp.swapaxes(a, axis, axis + 1).reshape(shp)


def _ffn_deinterleave(a, axis):
    shp = a.shape
    a = a.reshape(shp[:axis] + (D_FF // LANES, 2, LANES) + shp[axis + 1:])
    return jnp.swapaxes(a, axis, axis + 1).reshape(shp)


def _shifted_rows(prev_ref, cur_ref):
    keep = SEQ_BLOCK - N_META
    return jnp.concatenate([prev_ref[keep:, :], cur_ref[:keep, :]], axis=0)


def _frame_specs(nblk, nfb, d):
    prev = pl.BlockSpec((SEQ_BLOCK, d), lambda b, i: (b * nfb + jnp.clip(i - 1, 0, nfb - 1), 0))
    cur = pl.BlockSpec((SEQ_BLOCK, d), lambda b, i: (b * nfb + jnp.clip(i, 0, nfb - 1), 0))
    return prev, cur


def _embed_rms(x2, meta, gain, nb, lp, l, name):
    d = x2.shape[1]
    tr = SEQ_BLOCK
    nblk = lp // tr
    nfb = (l - N_META) // tr
    m = nb * lp

    def body(prev_ref, cur_ref, meta_ref, g_ref, h_ref, o_ref):
        i = pl.program_id(1)
        t = i * tr + lax.broadcasted_iota(jnp.int32, (tr, 1), 0)
        rows = jnp.where(t < l, _shifted_rows(prev_ref, cur_ref), 0.0)
        head = jnp.concatenate([meta_ref[...], jnp.zeros((tr - N_META, d), F32)], axis=0)
        xv = jnp.where(t < N_META, head, rows)
        h_ref[...] = xv
        r = lax.rsqrt(jnp.mean(xv * xv, axis=-1, keepdims=True) + EPS)
        o_ref[...] = ((xv * r) * g_ref[...]).astype(o_ref.dtype)

    prev, cur = _frame_specs(nblk, nfb, d)
    row = pl.BlockSpec((tr, d), lambda b, i: (b * nblk + i, 0))
    return pl.pallas_call(
        body, name=name, grid=(nb, nblk),
        in_specs=[prev, cur, pl.BlockSpec((N_META, d), lambda b, i: (0, 0)), pl.BlockSpec((1, d), lambda b, i: (0, 0))],
        out_specs=[row, row],
        out_shape=[jax.ShapeDtypeStruct((m, d), F32), jax.ShapeDtypeStruct((m, d), BF16)],
        compiler_params=_cp("parallel", "parallel"),
    )(x2, x2, meta, gain)


def _loss_head(out, tgt2, nb, lp, l, name):
    m, d = out.shape
    tr = SEQ_BLOCK
    nblk = lp // tr
    nfb = (l - N_META) // tr

    def body(o_ref, prev_ref, cur_ref, dy_ref, ls_ref):
        t = pl.program_id(1) * tr + lax.broadcasted_iota(jnp.int32, (tr, 1), 0)
        valid = (t >= N_META) & (t < l)
        err = jnp.where(valid, o_ref[...] - _shifted_rows(prev_ref, cur_ref), 0.0)
        dy_ref[...] = err * (1.0 / d)
        part = jnp.sum(err * err, axis=0, keepdims=True)
        first = (pl.program_id(0) == 0) & (pl.program_id(1) == 0)

        @pl.when(first)
        def _():
            ls_ref[...] = part

        @pl.when(jnp.logical_not(first))
        def _():
            ls_ref[...] += part

    prev, cur = _frame_specs(nblk, nfb, d)
    row = pl.BlockSpec((tr, d), lambda b, i: (b * nblk + i, 0))
    return pl.pallas_call(
        body, name=name, grid=(nb, nblk),
        in_specs=[row, prev, cur], out_specs=[row, pl.BlockSpec((1, d), lambda b, i: (0, 0))],
        out_shape=[jax.ShapeDtypeStruct((m, d), F32), jax.ShapeDtypeStruct((1, d), F32)],
        compiler_params=_cp("arbitrary", "arbitrary"),
    )(out, tgt2, tgt2)


def _adam_math(g, w, mom, var):
    c1 = 1.0 - ADAM_B1 ** ADAM_STEP
    c2 = 1.0 - ADAM_B2 ** ADAM_STEP
    mn = ADAM_B1 * mom + (1.0 - ADAM_B1) * g
    vn = ADAM_B2 * var + (1.0 - ADAM_B2) * (g * g)
    delta = -ADAM_LR * ((mn / c1) / (jnp.sqrt(vn / c2) + ADAM_EPS) + ADAM_WD * w)
    return delta, mn, vn


def _slot_sum(recv, name):
    _, r, c = recv.shape
    tc = _div_tile(c, 256, LANES)

    def body(r_ref, g_ref):
        g = r_ref[0].astype(F32)
        for s in range(1, N_DEV):
            g = g + r_ref[s].astype(F32)
        g_ref[...] = g

    return pl.pallas_call(
        body, name=name, grid=(c // tc,),
        in_specs=[pl.BlockSpec((N_DEV, r, tc), lambda j: (0, 0, j))],
        out_specs=pl.BlockSpec((r, tc), lambda j: (0, j)),
        out_shape=jax.ShapeDtypeStruct((r, c), F32),
        compiler_params=_cp("parallel"),
    )(recv)


def _adamw(g, w, mom, var, name):
    r, c = w.shape
    tr = _div_tile(r, 256, 8)

    def body(g_ref, w_ref, m_ref, v_ref, d_ref, mo_ref, vo_ref):
        d_ref[...], mo_ref[...], vo_ref[...] = _adam_math(g_ref[...], w_ref[...], m_ref[...], v_ref[...])

    row = pl.BlockSpec((tr, c), lambda i: (i, 0))
    return pl.pallas_call(
        body, name=name, grid=(r // tr,), in_specs=[row] * 4, out_specs=[row] * 3,
        out_shape=[jax.ShapeDtypeStruct((r, c), F32)] * 3,
        compiler_params=_cp("parallel"),
    )(g, w, mom, var)


def _sum_adamw(recv, w, mom, var, name):
    r, c = w.shape
    tr = _div_tile(r, 256, 8)

    def body(r_ref, w_ref, m_ref, v_ref, g_ref, d_ref, mo_ref, vo_ref):
        g = r_ref[0].astype(F32)
        for s in range(1, N_DEV):
            g = g + r_ref[s].astype(F32)
        g_ref[...] = g
        d_ref[...], mo_ref[...], vo_ref[...] = _adam_math(g, w_ref[...], m_ref[...], v_ref[...])

    row = pl.BlockSpec((tr, c), lambda i: (i, 0))
    return pl.pallas_call(
        body, name=name, grid=(r // tr,),
        in_specs=[pl.BlockSpec((N_DEV, tr, c), lambda i: (0, i, 0)), row, row, row],
        out_specs=[row] * 4,
        out_shape=[jax.ShapeDtypeStruct((r, c), F32)] * 4,
        compiler_params=_cp("parallel"),
    )(recv, w, mom, var)


_MESH = pl.DeviceIdType.MESH
_HBM = pl.BlockSpec(memory_space=pltpu.HBM)
N_PEER = N_DEV - 1


def _position():
    return lax.axis_index("x"), lax.axis_index("y"), lax.axis_index("c")


def _all_gather(shards, name):
    n = len(shards)

    def body(*refs):
        x_refs, out_refs = refs[:n], refs[n:2 * n]
        send_sems, recv_sems, local_sems = refs[2 * n:]
        x, y, c = _position()
        me, sibling = (x, y, c), (x, y, 1 - c)
        chips = [(1 - x, y), (x, 1 - y), (1 - x, 1 - y)]

        def copy(a, k, block, to, src=None):
            slot = out_refs[a].at[4 * block[0] + 2 * block[1] + block[2]]
            return pltpu.make_async_remote_copy(
                src_ref=slot if src is None else src, dst_ref=slot,
                send_sem=send_sems.at[a * N_PEER + k], recv_sem=recv_sems.at[a * N_PEER + k],
                device_id=to, device_id_type=_MESH)

        mine, sent = [], []
        for a in range(n):
            cp = pltpu.make_async_copy(x_refs[a], out_refs[a].at[4 * x + 2 * y + c], local_sems.at[a])
            cp.start()
            mine.append(cp)
            first = [copy(a, 0, me, sibling, src=x_refs[a])]
            first += [copy(a, 1 + j, me, (*chip, c), src=x_refs[a]) for j, chip in enumerate(chips)]
            for cp in first:
                cp.start()
            sent += first
        for a in range(n):
            for j, chip in enumerate(chips):
                copy(a, 1 + j, (*chip, c), me).wait_recv()
                fwd = copy(a, 4 + j, (*chip, c), sibling)
                fwd.start()
                sent.append(fwd)
        for a in range(n):
            copy(a, 0, sibling, me).wait_recv()
            for j, chip in enumerate(chips):
                copy(a, 4 + j, (*chip, 1 - c), me).wait_recv()
        for cp in sent:
            cp.wait_send()
        for cp in mine:
            cp.wait()

    return pl.pallas_call(
        body, name=name,
        out_shape=[jax.ShapeDtypeStruct((N_DEV,) + a.shape, a.dtype) for a in shards],
        in_specs=[_HBM] * n, out_specs=[_HBM] * n,
        scratch_shapes=[pltpu.SemaphoreType.DMA((n * N_PEER,)), pltpu.SemaphoreType.DMA((n * N_PEER,)),
                        pltpu.SemaphoreType.DMA((n,))],
    )(*shards)


_FLIPS = [(fx, fy, fc) for fx in (0, 1) for fy in (0, 1) for fc in (0, 1)][1:]


def _exchange_copies(in_refs, out_refs, nblk, send_sems, recv_sems, local_sems):
    n = len(in_refs)
    x, y, c = _position()
    me = 4 * x + 2 * y + c

    def peer(f):
        return (1 - x if f[0] else x, 1 - y if f[1] else y, 1 - c if f[2] else c)

    def idx(p):
        return 4 * p[0] + 2 * p[1] + p[2]

    def local(a):
        return pltpu.make_async_copy(in_refs[a].at[me] if a < nblk else in_refs[a], out_refs[a].at[me], local_sems.at[a])

    def remote(a, k, sending):
        p = peer(_FLIPS[k])
        src = in_refs[a].at[idx(p)] if a < nblk else in_refs[a]
        dst = out_refs[a].at[me] if sending else out_refs[a].at[idx(p)]
        return pltpu.make_async_remote_copy(
            src_ref=src, dst_ref=dst, send_sem=send_sems.at[a * N_PEER + k], recv_sem=recv_sems.at[a * N_PEER + k],
            device_id=p, device_id_type=_MESH)

    def start():
        for a in range(n):
            local(a).start()
            for k in range(N_PEER):
                remote(a, k, True).start()

    def wait():
        for a in range(n):
            for k in range(N_PEER):
                remote(a, k, False).wait_recv()
        for a in range(n):
            for k in range(N_PEER):
                remote(a, k, True).wait_send()
            local(a).wait()

    return start, wait


def _exchange_io(blocks, shared):
    arrays = list(blocks) + list(shared)
    n = len(arrays)
    out_shape = [jax.ShapeDtypeStruct(a.shape, a.dtype) for a in blocks]
    out_shape += [jax.ShapeDtypeStruct((N_DEV,) + a.shape, a.dtype) for a in shared]
    sems = [pltpu.SemaphoreType.DMA((n * N_PEER,)), pltpu.SemaphoreType.DMA((n * N_PEER,)), pltpu.SemaphoreType.DMA((n,))]
    return arrays, out_shape, sems


def _exchange(blocks, shared, name):
    arrays, out_shape, sems = _exchange_io(blocks, shared)
    n = len(arrays)

    def body(*refs):
        start, wait = _exchange_copies(refs[:n], refs[n:2 * n], len(blocks), *refs[2 * n:])
        start()
        wait()

    return pl.pallas_call(
        body, name=name, out_shape=out_shape, in_specs=[_HBM] * n, out_specs=[_HBM] * n, scratch_shapes=sems,
    )(*arrays)


def _grid_ends(grid):
    ids = [pl.program_id(i) for i in range(len(grid))]
    first = functools.reduce(jnp.logical_and, [i == 0 for i in ids])
    last = functools.reduce(jnp.logical_and, [i == g - 1 for i, g in zip(ids, grid)])
    return first, last


def _pack(parts, rows):
    flat = jnp.concatenate(parts, axis=-1)
    return jnp.pad(flat, [(0, rows * LANES - flat.shape[-1])]).reshape(rows, LANES)


def _unpack(packed, shapes):
    flat = packed.reshape(-1)
    out, off = [], 0
    for shp in shapes:
        n = int(np.prod(shp))
        out.append(flat[off:off + n].reshape(shp))
        off += n
    return out


def _rows_for(shapes, extra=0):
    n = sum(int(np.prod(s)) for s in shapes) + extra
    return -(-n // (8 * LANES)) * 8


def _lower_bound(logits):
    return jnp.cumsum(jax.nn.softmax(logits.astype(F32), axis=0), axis=0)[0:1]


def _align_axis0(w):
    a, b = 3 * FOX_WIDTH, 3 * FOX_WIDTH + FOX_HEADS
    c = b + 4 * HG_WIDTH
    pad = [(0, LANES - FOX_HEADS)] + [(0, 0)] * (w.ndim - 1)
    return jnp.concatenate([w[c:], w[:a], w[b:c], jnp.pad(w[a:b], pad)], axis=0)


def _unalign_axis0(g):
    a, b = 2 * D_MODEL, 2 * D_MODEL + 3 * FOX_WIDTH
    c = b + 4 * HG_WIDTH
    return jnp.concatenate([g[a:b], g[c:c + FOX_HEADS], g[b:c], g[:a]], axis=0)


TINY_COLS = 768


def _tiny_pack(conv_w_shard, meta_shard):
    cw = jnp.pad(conv_w_shard, ((0, 8 - CONV_WIDTH), (0, TINY_COLS - conv_w_shard.shape[1])))
    mt = jnp.pad(meta_shard, ((0, 0), (0, TINY_COLS - meta_shard.shape[1])))
    return jnp.concatenate([cw, mt], axis=0)


def _tiny_unpack(t, ncw, nmeta):
    return t[..., :CONV_WIDTH, :ncw], t[..., 8:8 + N_META, :nmeta]


def _late_weights(g_up, g_down, g_ab, g_out):
    d = g_up.shape[-1]
    w_a_t = g_ab[:, 0].reshape(-1, g_ab.shape[-1])
    w_b_t = g_ab[:, 1].reshape(-1, g_ab.shape[-1])
    return _ffn_interleave(g_up.reshape(-1, d), 0), g_down.reshape(-1, d), w_a_t, w_b_t, g_out.reshape(-1, d)


def _early_blocks(g_w_up_t, g_w_down, g_w_out, g_w_a_t, g_w_b_t):
    d = g_w_out.shape[-1]
    ab = jnp.stack([g_w_a_t.reshape(N_DEV, -1, g_w_a_t.shape[-1]), g_w_b_t.reshape(N_DEV, -1, g_w_b_t.shape[-1])], axis=1)
    return [_ffn_deinterleave(g_w_up_t, 0).reshape(N_DEV, -1, d).astype(BF16), g_w_down.reshape(N_DEV, -1, d).astype(BF16),
            g_w_out.reshape(N_DEV, -1, d).astype(BF16), ab.astype(BF16)]


def _local_step(x, target, meta, norm1_gain, w_in_t, fox_b_f, q_gain, k_gain, lb, hg_out_gain, w_a_t, w_b_t, w_out,
                norm2_gain, w_up_t, conv_w, conv_b, w_down, ffn_shards=None):
    nb, seq, d = x.shape
    assert seq % SEQ_BLOCK == 0 and N_META < SEQ_BLOCK
    l = seq + N_META
    lp = -(-l // SEQ_BLOCK) * SEQ_BLOCK
    m = nb * lp
    qg = jnp.tile(q_gain, (1, FOX_HEADS))
    kg = jnp.tile(k_gain, (1, FOX_HEADS))
    bf = jnp.pad(fox_b_f, ((0, 0), (0, LANES - FOX_HEADS)))

    h0, xn = _embed_rms(x.reshape(nb * seq, d), meta, norm1_gain, nb, lp, l, "embed_rms1")
    proj = _matmul(xn, w_in_t, "nt", F32, "proj_in")
    qa, ka, vb = _fox_prep(proj, qg, kg, bf, nb, lp, "fox_prep")
    if ffn_shards is None:
        o_fox, lse = _fox_fwd(qa, ka, vb, nb, lp, "fox_fwd")
    else:
        o_fox, lse, *late = _fox_fwd(qa, ka, vb, nb, lp, "fox_fwd", ride=ffn_shards)
        w_up_t, w_down, w_a_t, w_b_t, w_out = _late_weights(*late)
    o_raw, o_hg, s_save = _hgrn_fwd(proj, lb, hg_out_gain, nb, lp, "hgrn_fwd")
    ya = _matmul(o_hg, w_a_t, "nt", BF16, "branch_a")
    yb = _matmul(o_fox, w_b_t, "nt", BF16, "branch_b")
    merged = _gate_fwd(proj, ya, yb, "gate_fwd")
    h1 = _matmul(merged, w_out, "nn", F32, "mix_out", residual=h0)
    hn = _rms_fwd(h1, norm2_gain, "rms2_fwd")
    up = _matmul(hn, w_up_t, "nt", BF16, "ffn_up")
    act = _conv_fwd(up, conv_w, conv_b, nb, lp, "conv_fwd")
    out = _matmul(act, w_down, "nn", F32, "ffn_down", residual=h1)
    dy, lsum = _loss_head(out, target.reshape(nb * seq, d), nb, lp, l, "loss_head")
    loss = (0.5 / d) * jnp.sum(lsum)

    dact = _matmul(dy, w_down, "nt", BF16, "d_act")
    g_w_down = _matmul(act, dy, "tn", F32, "g_w_down")
    dup, g_conv_w, g_conv_b = _conv_bwd(up, dact, conv_w, conv_b, nb, lp, "conv_bwd")
    dhn = _matmul(dup, w_up_t, "nn", F32, "d_hn")
    g_w_up_t = _matmul(dup, hn, "tn", F32, "g_w_up")
    dh1, g_norm2 = _rms_bwd(h1, norm2_gain, dhn, dy, "rms2_bwd")

    dmerged = _matmul(dh1, w_out, "nt", BF16, "d_merged")
    g_w_out = _matmul(merged, dh1, "tn", F32, "g_w_out")
    dya, dyb, dgab = _gate_bwd(proj, ya, yb, dmerged, "gate_bwd")
    do_hg = _matmul(dya, w_a_t, "nn", F32, "d_o_hg")
    g_w_a_t = _matmul(dya, o_hg, "tn", F32, "g_w_a")
    do_fox = _matmul(dyb, w_b_t, "nn", BF16, "d_o_fox")
    g_w_b_t = _matmul(dyb, o_fox, "tn", F32, "g_w_b")
    dhq, dhf, dhi, dhg, g_hg_gain, g_lb = _hgrn_bwd(proj, o_raw, s_save, do_hg, lb, hg_out_gain, nb, lp, "hgrn_bwd")
    if ffn_shards is None:
        dqs, dkn, dvv, dc0, dc1 = _fox_bwd(qa, ka, vb, do_fox, o_fox, lse, nb, lp, "fox_bwd")
        early = None
    else:
        dqs, dkn, dvv, dc0, dc1, *early = _fox_bwd(qa, ka, vb, do_fox, o_fox, lse, nb, lp, "fox_bwd",
                                                   ride=_early_blocks(g_w_up_t, g_w_down, g_w_out, g_w_a_t, g_w_b_t))
    dcum = jnp.stack([dc0, dc1], axis=2).reshape(nb, FOX_HEADS, lp)
    dcum = jnp.pad(jnp.transpose(dcum, (0, 2, 1)), ((0, 0), (0, 0), (0, LANES - FOX_HEADS))).reshape(m, LANES)
    dfqkv, dff, g_qg, g_kg, g_bf = _fox_prep_bwd(proj, dqs, dkn, dvv, dcum, qg, kg, bf, nb, lp, "fox_prep_bwd")
    dproj = jnp.concatenate([dgab, dfqkv, dhq, dhf, dhi, dhg, dff], axis=1)
    g_w_in_t = _matmul(dproj, xn, "tn", F32, "g_w_in")
    if ffn_shards is None:
        dxn = _matmul(dproj, w_in_t, "nn", F32, "d_xn")
    else:
        blocks_in = _unalign_axis0(g_w_in_t).reshape(N_DEV, -1, d).astype(BF16)
        dxn, r_in = _matmul(dproj, w_in_t, "nn", F32, "d_xn", ride=[blocks_in])
        early = early + [r_in]
    dh0, g_norm1 = _rms_bwd(h0, norm1_gain, dxn, dh1, "rms1_bwd")

    dh0 = dh0.reshape(nb, lp, d)
    grad_x = dh0[:, N_META:l]
    g_meta = jnp.sum(dh0[:, :N_META], axis=0)
    g_q_gain = jnp.sum(g_qg.reshape(FOX_HEADS, FOX_HEAD_DIM), axis=0, keepdims=True)
    g_k_gain = jnp.sum(g_kg.reshape(FOX_HEADS, FOX_HEAD_DIM), axis=0, keepdims=True)
    grads = dict(meta_tokens=g_meta, norm1_gain=g_norm1, w_in_t=g_w_in_t, fox_b_f=g_bf[:, :FOX_HEADS],
                 q_norm_gain=g_q_gain, k_norm_gain=g_k_gain, lb=g_lb, hg_out_gain=g_hg_gain,
                 w_a_t=g_w_a_t, w_b_t=g_w_b_t, w_out=g_w_out, norm2_gain=g_norm2, w_up_t=g_w_up_t,
                 conv_w=g_conv_w, conv_b=g_conv_b, w_down=g_w_down, early=early)
    return loss, grad_x, grads


SMALL = ("norm1_gain", "fox_b_f", "q_norm_gain", "k_norm_gain", "hg_lb_logits", "hg_out_gain", "norm2_gain", "conv_b")
ORDER = ("meta_tokens", "norm1_gain", "w_in", "fox_b_f", "q_norm_gain", "k_norm_gain", "hg_lb_logits", "hg_out_gain",
         "w_branch_a", "w_branch_b", "w_out", "norm2_gain", "w_up", "conv_w", "conv_b", "w_down")


def kernel(x, meta_tokens, norm1_gain, w_in, fox_b_f, q_norm_gain, k_norm_gain, hg_lb_logits, hg_out_gain, w_branch_a, w_branch_b, w_out, norm2_gain, w_up, conv_w, conv_b, w_down, loss_target, m_meta_tokens, m_norm1_gain, m_w_in, m_fox_b_f, m_q_norm_gain, m_k_norm_gain, m_hg_lb_logits, m_hg_out_gain, m_w_branch_a, m_w_branch_b, m_w_out, m_norm2_gain, m_w_up, m_conv_w, m_conv_b, m_w_down, v_meta_tokens, v_norm1_gain, v_w_in, v_fox_b_f, v_q_norm_gain, v_k_norm_gain, v_hg_lb_logits, v_hg_out_gain, v_w_branch_a, v_w_branch_b, v_w_out, v_norm2_gain, v_w_up, v_conv_w, v_conv_b, v_w_down):
    w = dict(meta_tokens=meta_tokens, norm1_gain=norm1_gain, w_in=w_in, fox_b_f=fox_b_f, q_norm_gain=q_norm_gain,
             k_norm_gain=k_norm_gain, hg_lb_logits=hg_lb_logits, hg_out_gain=hg_out_gain, w_branch_a=w_branch_a,
             w_branch_b=w_branch_b, w_out=w_out, norm2_gain=norm2_gain, w_up=w_up, conv_w=conv_w, conv_b=conv_b,
             w_down=w_down)
    mom = dict(meta_tokens=m_meta_tokens, norm1_gain=m_norm1_gain, w_in=m_w_in, fox_b_f=m_fox_b_f,
               q_norm_gain=m_q_norm_gain, k_norm_gain=m_k_norm_gain, hg_lb_logits=m_hg_lb_logits,
               hg_out_gain=m_hg_out_gain, w_branch_a=m_w_branch_a, w_branch_b=m_w_branch_b, w_out=m_w_out,
               norm2_gain=m_norm2_gain, w_up=m_w_up, conv_w=m_conv_w, conv_b=m_conv_b, w_down=m_w_down)
    var = dict(meta_tokens=v_meta_tokens, norm1_gain=v_norm1_gain, w_in=v_w_in, fox_b_f=v_fox_b_f,
               q_norm_gain=v_q_norm_gain, k_norm_gain=v_k_norm_gain, hg_lb_logits=v_hg_lb_logits,
               hg_out_gain=v_hg_out_gain, w_branch_a=v_w_branch_a, w_branch_b=v_w_branch_b, w_out=v_w_out,
               norm2_gain=v_norm2_gain, w_up=v_w_up, conv_w=v_conv_w, conv_b=v_conv_b, w_down=v_w_down)
    d = D_MODEL
    n_in, n_up = w_in.shape[2], w_up.shape[2]
    n_ab, n_meta = w_branch_a.shape[2], meta_tokens.shape[1]

    g_in, g_tiny = _all_gather([w_in[0].T.astype(BF16), _tiny_pack(conv_w[0], meta_tokens)], "gather_weights")
    w_in_t = _align_axis0(g_in.reshape(N_DEV * n_in, d))
    cw_slots, meta_slots = _tiny_unpack(g_tiny, n_up, n_meta)
    conv_w_f = _ffn_interleave(jnp.transpose(cw_slots, (1, 0, 2)).reshape(CONV_WIDTH, -1), 1)
    meta_f = jnp.transpose(meta_slots, (1, 0, 2)).reshape(N_META, -1)
    conv_b_i = _ffn_interleave(conv_b, 1)

    lb, lb_vjp = jax.vjp(_lower_bound, hg_lb_logits)
    loss, grad_x, g = _local_step(
        x, loss_target, meta_f, norm1_gain, w_in_t, fox_b_f, q_norm_gain, k_norm_gain, lb, hg_out_gain,
        None, None, None, norm2_gain, None, conv_w_f, conv_b_i, None,
        ffn_shards=(w_up[0].T.astype(BF16), w_down[0].astype(BF16),
                    jnp.stack([w_branch_a[0].T, w_branch_b[0].T]).astype(BF16), w_out[0].astype(BF16)))

    g["hg_lb_logits"] = lb_vjp(g.pop("lb"))[0]
    g["conv_b"] = _ffn_deinterleave(g["conv_b"], 1)
    gcw = _ffn_deinterleave(g["conv_w"], 1).reshape(CONV_WIDTH, N_DEV, n_up)
    gmeta = g["meta_tokens"].reshape(N_META, N_DEV, n_meta)
    tiny = jnp.concatenate([
        jnp.pad(jnp.transpose(gcw, (1, 0, 2)), ((0, 0), (0, 8 - CONV_WIDTH), (0, TINY_COLS - n_up))),
        jnp.pad(jnp.transpose(gmeta, (1, 0, 2)), ((0, 0), (0, 0), (0, TINY_COLS - n_meta)))], axis=1)
    small_shapes = [w[n].shape for n in SMALL]
    rows_sm = _rows_for(small_shapes, extra=1)
    small = _pack([g[n].reshape(-1) for n in SMALL] + [loss.reshape(1)], rows_sm)
    r_tiny, r_small = _exchange([tiny], [small], "exchange_grads")
    r_up, r_down, r_out, r_ab, r_in = g["early"]

    res = {}
    g_in_s = _slot_sum(r_in, "sum_w_in").T
    res["w_in"] = (g_in_s,) + tuple(_adamw(g_in_s, w_in[0], m_w_in[0], v_w_in[0], "adamw_w_in"))
    g_up_s = _slot_sum(r_up, "sum_w_up").T
    res["w_up"] = (g_up_s,) + tuple(_adamw(g_up_s, w_up[0], m_w_up[0], v_w_up[0], "adamw_w_up"))
    g_ab_s = jnp.swapaxes(_slot_sum(r_ab.reshape(N_DEV, 2 * n_ab, -1), "sum_w_ab").reshape(2, n_ab, -1), 1, 2)
    ab = lambda t: jnp.concatenate([t["w_branch_a"][0], t["w_branch_b"][0]], axis=0)
    o_ab = (g_ab_s.reshape(-1, n_ab),) + tuple(_adamw(g_ab_s.reshape(-1, n_ab), ab(w), ab(mom), ab(var), "adamw_w_ab"))
    half = o_ab[0].shape[0] // 2
    res["w_branch_a"] = tuple(o[:half] for o in o_ab)
    res["w_branch_b"] = tuple(o[half:] for o in o_ab)
    res["w_out"] = tuple(_sum_adamw(r_out, w_out[0], m_w_out[0], v_w_out[0], "adamw_w_out"))
    res["w_down"] = tuple(_sum_adamw(r_down, w_down[0], m_w_down[0], v_w_down[0], "adamw_w_down"))
    tp = lambda t: _tiny_pack(t["conv_w"][0], t["meta_tokens"])
    o_tiny = [_tiny_unpack(o, n_up, n_meta) for o in _sum_adamw(r_tiny, tp(w), tp(mom), tp(var), "adamw_tiny")]
    res["conv_w"] = tuple(o[0] for o in o_tiny)
    res["meta_tokens"] = tuple(o[1] for o in o_tiny)
    zero1 = jnp.zeros((1,), F32)
    sp = lambda t: _pack([t[n].reshape(-1) for n in SMALL] + [zero1], rows_sm)
    o_small = [_unpack(o, small_shapes + [(1,)]) for o in _sum_adamw(r_small, sp(w), sp(mom), sp(var), "adamw_small")]
    for i, n in enumerate(SMALL):
        res[n] = tuple(o[i] for o in o_small)
    loss_all = o_small[0][len(SMALL)].reshape(())

    result = [[res[n][k].reshape(w[n].shape) for n in ORDER] for k in range(4)]
    return (loss_all, grad_x, *result[0], *result[1], *result[2], *result[3])
```

```python
import functools

import jax
import jax.numpy as jnp
import numpy as np
from jax import lax
from jax.experimental import pallas as pl
from jax.experimental.pallas import tpu as pltpu

F32 = jnp.float32
BF16 = jnp.bfloat16

D_MODEL = 1024
N_META = 16
FOX_HEADS = 8
FOX_HEAD_DIM = 64
FOX_WIDTH = FOX_HEADS * FOX_HEAD_DIM
HG_HEADS = 4
HG_DIM = 128
HG_WIDTH = HG_HEADS * HG_DIM
D_FF = 2816
CONV_WIDTH = 3
EPS = 1e-6
IN_COLS = 3 * FOX_WIDTH + FOX_HEADS + 4 * HG_WIDTH + 2 * D_MODEL
N_DEV = 8

ADAM_LR = 0.001
ADAM_B1 = 0.9
ADAM_B2 = 0.999
ADAM_EPS = 1e-08
ADAM_WD = 0.01
ADAM_STEP = 10

LANES = 128
SEQ_BLOCK = 128
SUB = 16
NEG = -1e30
VMEM_LIMIT = 48 * 1024 * 1024

FOX_CB = 2 * D_MODEL // FOX_WIDTH
CB_HQ = (2 * D_MODEL + 3 * FOX_WIDTH) // LANES
CB_HF = CB_HQ + HG_HEADS
CB_HI = CB_HF + HG_HEADS
CB_HG = CB_HI + HG_HEADS
CB_FF = CB_HG + HG_HEADS


def _div_tile(n, target, mult):
    best = None
    for t in range(mult, min(n, target) + 1, mult):
        if n % t == 0:
            best = t
    if best is None:
        best = n
    return best


def _cp(*sem):
    return pltpu.CompilerParams(dimension_semantics=sem, vmem_limit_bytes=VMEM_LIMIT)


def _sigmoid(x):
    return 0.5 * jnp.tanh(0.5 * x) + 0.5


def _dot(a, b, dims, precision=None):
    return lax.dot_general(a, b, (dims, ((), ())), preferred_element_type=F32, precision=precision)


NN = ((1,), (0,))
NT = ((1,), (1,))
TN = ((0,), (0,))
HI = lax.Precision.HIGHEST


MATMUL_VMEM_BUDGET = 30 * 1024 * 1024
MATMUL_MAX_TILE = 2048


def _tile_options(n):
    return [t for t in range(LANES, min(n, MATMUL_MAX_TILE) + 1, LANES) if n % t == 0] or [n]


def _matmul_tiles(m, n, k, a_bytes, b_bytes, o_bytes, has_res):
    tk = _div_tile(k, MATMUL_MAX_TILE, LANES)
    best = None
    for tm in _tile_options(m):
        for tn in _tile_options(n):
            vmem = 2 * (tm * tk * a_bytes + tk * tn * b_bytes) + 2 * tm * tn * o_bytes
            vmem += tm * tn * 4 if (tk < k and o_bytes != 4) else 0
            vmem += 2 * tm * tn * 4 if has_res else 0
            if vmem > MATMUL_VMEM_BUDGET:
                continue
            key = (tm * tn, tn % 256 == 0, tn)
            if best is None or key > best[0]:
                best = (key, tm, tn)
    assert best is not None, (m, n, k)
    return best[1], best[2], tk


def _matmul(a, b, mode, out_dtype, name, residual=None, ride=()):
    if mode == "nn":
        (m, k), (k2, n) = a.shape, b.shape
    elif mode == "nt":
        (m, k), (n, k2) = a.shape, b.shape
    else:
        (k, m), (k2, n) = a.shape, b.shape
    assert k == k2, (a.shape, b.shape, mode)
    has_res = residual is not None
    tm, tn, tk = _matmul_tiles(m, n, k, a.dtype.itemsize, b.dtype.itemsize, jnp.dtype(out_dtype).itemsize, has_res)
    nk = k // tk
    in_place = jnp.dtype(out_dtype) == jnp.dtype(F32)
    if mode == "nn":
        a_spec = pl.BlockSpec((tm, tk), lambda i, j, kk: (i, kk))
        b_spec = pl.BlockSpec((tk, tn), lambda i, j, kk: (kk, j))
        dims = NN
    elif mode == "nt":
        a_spec = pl.BlockSpec((tm, tk), lambda i, j, kk: (i, kk))
        b_spec = pl.BlockSpec((tn, tk), lambda i, j, kk: (j, kk))
        dims = NT
    else:
        a_spec = pl.BlockSpec((tk, tm), lambda i, j, kk: (kk, i))
        b_spec = pl.BlockSpec((tk, tn), lambda i, j, kk: (kk, j))
        dims = TN
    o_spec = pl.BlockSpec((tm, tn), lambda i, j, kk: (i, j))
    grid = (m // tm, n // tn, nk)
    x_arrays, x_shapes, x_sems = _exchange_io(ride, ())
    nx = len(x_arrays)
    n_in = 3 if has_res else 2

    def body(*refs):
        if nx:
            first, last = _grid_ends(grid)
            x_in, x_out = refs[n_in:n_in + nx], refs[n_in + nx + 1:n_in + 2 * nx + 1]
            start, wait = _exchange_copies(x_in, x_out, nx, *refs[n_in + 2 * nx + 1:n_in + 2 * nx + 4])
            pl.when(first)(start)
        compute(*refs)
        if nx:
            pl.when(last)(wait)

    def compute(*refs):
        a_ref, b_ref = refs[0], refs[1]
        r_ref = refs[2] if has_res else None
        o_ref = refs[n_in + nx]
        if nk == 1:
            part = _dot(a_ref[...].astype(BF16), b_ref[...].astype(BF16), dims)
            o_ref[...] = (part + r_ref[...] if has_res else part).astype(o_ref.dtype)
            return
        acc_ref = o_ref if in_place else refs[-1]
        kk = pl.program_id(2)

        @pl.when(kk == 0)
        def _():
            acc_ref[...] = r_ref[...] if (has_res and in_place) else jnp.zeros_like(acc_ref)

        acc_ref[...] += _dot(a_ref[...].astype(BF16), b_ref[...].astype(BF16), dims)

        if not in_place:
            @pl.when(kk == nk - 1)
            def _():
                acc = acc_ref[...]
                if has_res:
                    acc = acc + r_ref[...]
                o_ref[...] = acc.astype(o_ref.dtype)

    in_specs = [a_spec, b_spec] + ([o_spec] if has_res else [])
    args = (a, b) + ((residual,) if has_res else ())
    out_shape = jax.ShapeDtypeStruct((m, n), out_dtype)
    acc = [pltpu.VMEM((tm, tn), F32)] if (nk > 1 and not in_place) else []
    if not nx:
        return pl.pallas_call(
            body, name=name, grid=grid, in_specs=in_specs, out_specs=o_spec, out_shape=out_shape, scratch_shapes=acc,
            compiler_params=_cp("parallel", "parallel", "arbitrary"),
        )(*args)
    return pl.pallas_call(
        body, name=name, grid=grid, in_specs=in_specs + [_HBM] * nx, out_specs=[o_spec] + [_HBM] * nx,
        out_shape=[out_shape] + x_shapes, scratch_shapes=x_sems + acc,
        compiler_params=_cp("arbitrary", "arbitrary", "arbitrary"),
    )(*args, *x_arrays)


def _rms_fwd(x, gain, name):
    m, d = x.shape
    tm = _div_tile(m, 512, 16)

    def body(x_ref, g_ref, o_ref):
        xv = x_ref[...]
        r = lax.rsqrt(jnp.mean(xv * xv, axis=-1, keepdims=True) + EPS)
        o_ref[...] = ((xv * r) * g_ref[...]).astype(o_ref.dtype)

    return pl.pallas_call(
        body, name=name, grid=(m // tm,),
        in_specs=[pl.BlockSpec((tm, d), lambda i: (i, 0)), pl.BlockSpec((1, d), lambda i: (0, 0))],
        out_specs=pl.BlockSpec((tm, d), lambda i: (i, 0)),
        out_shape=jax.ShapeDtypeStruct((m, d), BF16),
        compiler_params=_cp("parallel"),
    )(x, gain)


def _rms_bwd(x, gain, dy, dres, name):
    m, d = x.shape
    tm = _div_tile(m, 256, 8)

    def body(x_ref, g_ref, dy_ref, dr_ref, dx_ref, dg_ref):
        xv = x_ref[...]
        r = lax.rsqrt(jnp.mean(xv * xv, axis=-1, keepdims=True) + EPS)
        nv = xv * r
        dyv = dy_ref[...]
        gdy = dyv * g_ref[...]
        dx_ref[...] = dr_ref[...] + r * (gdy - nv * jnp.mean(gdy * nv, axis=-1, keepdims=True))
        part = jnp.sum(dyv * nv, axis=0, keepdims=True)

        @pl.when(pl.program_id(0) == 0)
        def _():
            dg_ref[...] = part

        @pl.when(pl.program_id(0) > 0)
        def _():
            dg_ref[...] += part

    row = pl.BlockSpec((tm, d), lambda i: (i, 0))
    vec = pl.BlockSpec((1, d), lambda i: (0, 0))
    return pl.pallas_call(
        body, name=name, grid=(m // tm,),
        in_specs=[row, vec, row, row], out_specs=[row, vec],
        out_shape=[jax.ShapeDtypeStruct((m, d), F32), jax.ShapeDtypeStruct((1, d), F32)],
        compiler_params=_cp("arbitrary"),
    )(x, gain, dy, dres)


def _head_stats(xv, lo):
    sq = xv * xv
    s_lo = jnp.sum(jnp.where(lo, sq, 0.0), axis=1, keepdims=True)
    s_hi = jnp.sum(jnp.where(lo, 0.0, sq), axis=1, keepdims=True)
    return jnp.where(lo, s_lo, s_hi) * (1.0 / FOX_HEAD_DIM)


BIAS_LANE = FOX_HEAD_DIM
N_SPLIT = 3


def _split3(c):
    c1 = c.astype(BF16).astype(F32)
    r1 = c - c1
    c2 = r1.astype(BF16).astype(F32)
    c3 = (r1 - c2).astype(BF16).astype(F32)
    return c1, c2, c3


def _fox_prep(proj, qg, kg, bf, nb, lp, name):
    m = proj.shape[0]
    ts = SEQ_BLOCK
    tg = _hg_rows(lp)
    nblk = lp // tg
    scale = FOX_HEAD_DIM ** -0.5

    def body(q_ref, k_ref, v_ref, f_ref, qg_ref, kg_ref, bf_ref, qo_ref, ko_ref, vo_ref, carry_ref):
        @pl.when(pl.program_id(1) == 0)
        def _():
            carry_ref[...] = jnp.zeros_like(carry_ref)

        for hb in range(tg // ts):
            rows = pl.ds(hb * ts, ts)
            block(q_ref.at[rows], k_ref.at[rows], v_ref.at[rows], f_ref.at[rows], qg_ref, kg_ref, bf_ref,
                  qo_ref.at[rows], ko_ref.at[rows], vo_ref.at[rows], carry_ref)

    def block(q_ref, k_ref, v_ref, f_ref, qg_ref, kg_ref, bf_ref, qo_ref, ko_ref, vo_ref, carry_ref):
        lane = lax.broadcasted_iota(jnp.int32, (1, LANES), 1)
        lo = lane < FOX_HEAD_DIM
        z = f_ref[...] + bf_ref[...]
        logf = jnp.minimum(z, 0.0) - jnp.log(1.0 + jnp.exp(-jnp.abs(z)))
        logf = jnp.where(lane < FOX_HEADS, logf, 0.0)
        r = lax.broadcasted_iota(jnp.int32, (ts, ts), 0)
        c = lax.broadcasted_iota(jnp.int32, (ts, ts), 1)
        tri = jnp.where(c <= r, 1.0, 0.0).astype(F32)
        cum = _dot(tri, logf, NN, HI) + carry_ref[...]
        carry_ref[...] = cum[ts - 1:ts, :]

        ones = jnp.where((lane >= BIAS_LANE + N_SPLIT) & (lane < BIAS_LANE + 2 * N_SPLIT), 1.0, 0.0)
        ones_k = jnp.where((lane >= BIAS_LANE) & (lane < BIAS_LANE + N_SPLIT), 1.0, 0.0)
        for j in range(FOX_WIDTH // LANES):
            cs = slice(j * LANES, (j + 1) * LANES)
            xq = q_ref[:, cs]
            yq = ((xq * lax.rsqrt(_head_stats(xq, lo) + EPS)) * qg_ref[:, cs]) * scale
            xk = k_ref[:, cs]
            yk = (xk * lax.rsqrt(_head_stats(xk, lo) + EPS)) * kg_ref[:, cs]
            for hh in range(2):
                h = 2 * j + hh
                pieces = _split3(_lane_pick(cum, lane, h))
                qb, kb = ones, ones_k
                for i, piece in enumerate(pieces):
                    qb = jnp.where(lane == BIAS_LANE + i, piece, qb)
                    kb = jnp.where(lane == BIAS_LANE + N_SPLIT + i, -piece, kb)
                yq_h = yq if hh == 0 else pltpu.roll(yq, FOX_HEAD_DIM, 1)
                yk_h = yk if hh == 0 else pltpu.roll(yk, FOX_HEAD_DIM, 1)
                hs = slice(h * LANES, (h + 1) * LANES)
                qo_ref[:, hs] = jnp.where(lo, yq_h, qb).astype(BF16)
                ko_ref[:, hs] = jnp.where(lo, yk_h, kb).astype(BF16)
        vo_ref[...] = v_ref[...].astype(BF16)

    w = FOX_WIDTH
    row = lambda b, i: (b * nblk + i, 0)
    return pl.pallas_call(
        body, name=name, grid=(nb, nblk),
        in_specs=[pl.BlockSpec((tg, w), lambda b, i: (b * nblk + i, FOX_CB)),
                  pl.BlockSpec((tg, w), lambda b, i: (b * nblk + i, FOX_CB + 1)),
                  pl.BlockSpec((tg, w), lambda b, i: (b * nblk + i, FOX_CB + 2)),
                  pl.BlockSpec((tg, LANES), lambda b, i: (b * nblk + i, CB_FF)),
                  pl.BlockSpec((1, w), lambda b, i: (0, 0)),
                  pl.BlockSpec((1, w), lambda b, i: (0, 0)),
                  pl.BlockSpec((1, LANES), lambda b, i: (0, 0))],
        out_specs=[pl.BlockSpec((tg, 2 * w), row), pl.BlockSpec((tg, 2 * w), row), pl.BlockSpec((tg, w), row)],
        out_shape=[jax.ShapeDtypeStruct((m, 2 * w), BF16)] * 2 + [jax.ShapeDtypeStruct((m, w), BF16)],
        scratch_shapes=[pltpu.VMEM((1, LANES), F32)],
        compiler_params=_cp("arbitrary", "arbitrary"),
    )(proj, proj, proj, proj, qg, kg, bf)


def _att_tile(lp):
    return 384 if (lp % 384 == 0 and lp > 384) else 128


def _lane_pick(blk, lane, idx):
    return jnp.sum(jnp.where(lane == idx, blk, 0.0), axis=1, keepdims=True)


def _head_masks():
    lane = lax.broadcasted_iota(jnp.int32, (1, LANES), 1)
    return lane, [(lane >= hh * FOX_HEAD_DIM) & (lane < (hh + 1) * FOX_HEAD_DIM) for hh in range(2)]


def _fox_fwd(qa, ka, vb, nb, lp, name, ride=()):
    m = qa.shape[0]
    tq = _att_tile(lp)
    nq = lp // tq
    npair = FOX_WIDTH // LANES
    grid = (nb, npair, nq)
    r_arrays, r_shapes, r_sems = _exchange_io((), ride)
    nr = len(r_arrays)

    def body(q_ref, k_ref, v_ref, *rest):
        r_in, (o_ref, lse_ref), r_out, sems = rest[:nr], rest[nr:nr + 2], rest[nr + 2:2 * nr + 2], rest[2 * nr + 2:]
        if nr:
            first, last = _grid_ends(grid)
            start, wait = _exchange_copies(r_in, r_out, 0, *sems)
            pl.when(first)(start)
        qi = pl.program_id(2)
        lane, hmasks = _head_masks()
        zero16 = jnp.zeros((), BF16)
        causal = lax.broadcasted_iota(jnp.int32, (tq, 1), 0) >= lax.broadcasted_iota(jnp.int32, (1, tq), 1)
        qs = [q_ref[:, hh * LANES:(hh + 1) * LANES] for hh in range(2)]

        def tile(j, carry, diagonal):
            k0 = pl.multiple_of(j * tq, tq)
            vb = v_ref[pl.ds(k0, tq), :]
            out = []
            for hh in range(2):
                mx, l, acc = carry[3 * hh:3 * hh + 3]
                vz = jnp.where(hmasks[hh], vb, zero16)
                s = _dot(qs[hh], k_ref[pl.ds(k0, tq), hh * LANES:(hh + 1) * LANES], NT)
                if diagonal:
                    s = jnp.where(causal, s, NEG)
                m_new = jnp.maximum(mx, jnp.max(s, axis=1, keepdims=True))
                alpha = jnp.exp(mx - m_new)
                pe = jnp.exp(s - m_new)
                l = alpha * l + jnp.sum(pe, axis=1, keepdims=True)
                acc = alpha * acc + _dot(pe.astype(BF16), vz, NN)
                out += [m_new, l, acc]
            return tuple(out)

        init = (jnp.full((tq, 1), NEG, F32), jnp.zeros((tq, 1), F32), jnp.zeros((tq, LANES), F32)) * 2
        carry = lax.fori_loop(0, qi, lambda j, c: tile(j, c, False), init)
        m0, l0, acc0, m1, l1, acc1 = tile(qi, carry, True)
        o_ref[...] = acc0 / l0 + acc1 / l1
        lse_ref[...] = jnp.where(lane == 0, m0 + jnp.log(l0), jnp.where(lane == 1, m1 + jnp.log(l1), 0.0))
        if nr:
            pl.when(last)(wait)

    return pl.pallas_call(
        body, name=name, grid=grid,
        in_specs=[pl.BlockSpec((tq, 2 * LANES), lambda b, p, i: (b * nq + i, p)),
                  pl.BlockSpec((lp, 2 * LANES), lambda b, p, i: (b, p)),
                  pl.BlockSpec((lp, LANES), lambda b, p, i: (b, p))] + [_HBM] * nr,
        out_specs=[pl.BlockSpec((tq, LANES), lambda b, p, i: (b * nq + i, p)),
                   pl.BlockSpec((None, None, tq, LANES), lambda b, p, i: (b, p, i, 0))] + [_HBM] * nr,
        out_shape=[jax.ShapeDtypeStruct((m, FOX_WIDTH), F32),
                   jax.ShapeDtypeStruct((nb, npair, lp, LANES), F32)] + r_shapes,
        scratch_shapes=r_sems if nr else [],
        compiler_params=_cp(*(["arbitrary"] * 3 if nr else ["parallel", "parallel", "arbitrary"])),
    )(qa, ka, vb, *r_arrays)


def _fox_bwd(qa, ka, vb, do, o, lse, nb, lp, name, ride=()):
    m = qa.shape[0]
    tq = _att_tile(lp)
    nq = lp // tq
    npair = FOX_WIDTH // LANES
    grid = (nb, npair, nq)
    r_arrays, r_shapes, r_sems = _exchange_io(ride, ())
    nr = len(r_arrays)

    def body(k_ref, v_ref, q_ref, do_ref, o_ref, lse_ref, *rest):
        r_in, r_out, sems = rest[:nr], rest[nr + 5:2 * nr + 5], rest[2 * nr + 5:]
        dq_ref, dk_ref, dv_ref, dc0_ref, dc1_ref = rest[nr:nr + 5]
        if nr:
            first, last = _grid_ends(grid)
            start, wait = _exchange_copies(r_in, r_out, nr, *sems)
            pl.when(first)(start)
        j = pl.program_id(2)
        lane, hmasks = _head_masks()
        zero16 = jnp.zeros((), BF16)
        causal = lax.broadcasted_iota(jnp.int32, (tq, 1), 0) >= lax.broadcasted_iota(jnp.int32, (1, tq), 1)

        @pl.when(j == 0)
        def _():
            dq_ref[...] = jnp.zeros_like(dq_ref)

        vv = v_ref[...]
        vzs = [jnp.where(hm, vv, zero16) for hm in hmasks]

        def tile(qi, carry, diagonal):
            dk0, dk1, dv, dc0, dc1 = carry
            q0 = pl.multiple_of(qi * tq, tq)
            dob16 = do_ref[pl.ds(q0, tq), :].astype(BF16)
            ob = o_ref[pl.ds(q0, tq), :]
            lseb = lse_ref[pl.ds(q0, tq), :]
            dks, dcs = [dk0, dk1], [dc0, dc1]
            for hh in range(2):
                hs = slice(hh * LANES, (hh + 1) * LANES)
                q = q_ref[pl.ds(q0, tq), hs]
                doz16 = jnp.where(hmasks[hh], dob16, zero16)
                delta = jnp.sum(doz16.astype(F32) * ob, axis=1, keepdims=True)
                s = _dot(q, k_ref[:, hs], NT) - _lane_pick(lseb, lane, hh)
                if diagonal:
                    s = jnp.where(causal, s, NEG)
                pm = jnp.exp(s)
                ds = pm * (_dot(doz16, vzs[hh], NT) - delta)
                ds16 = ds.astype(BF16)
                dv = dv + _dot(pm.astype(BF16), doz16, TN)
                dks[hh] = dks[hh] + _dot(ds16, q, TN)
                dq_ref[pl.ds(q0, tq), hs] += _dot(ds16, k_ref[:, hs], NN)
                dcs[hh] = dcs[hh] - jnp.sum(ds, axis=0, keepdims=True)
            return dks[0], dks[1], dv, dcs[0], dcs[1]

        zt = jnp.zeros((tq, LANES), F32)
        zr = jnp.zeros((1, tq), F32)
        carry = tile(j, (zt, zt, zt, zr, zr), True)
        dk0, dk1, dv, dc0, dc1 = lax.fori_loop(j + 1, nq, lambda qi, c: tile(qi, c, False), carry)
        dk_ref[:, :LANES] = dk0
        dk_ref[:, LANES:] = dk1
        dv_ref[...] = dv
        dc0_ref[...] = dc0
        dc1_ref[...] = dc1
        if nr:
            pl.when(last)(wait)

    full2 = pl.BlockSpec((lp, 2 * LANES), lambda b, p, j: (b, p))
    full = pl.BlockSpec((lp, LANES), lambda b, p, j: (b, p))
    blk2 = pl.BlockSpec((tq, 2 * LANES), lambda b, p, j: (b * nq + j, p))
    blk = pl.BlockSpec((tq, LANES), lambda b, p, j: (b * nq + j, p))
    dcs = pl.BlockSpec((None, None, 1, tq), lambda b, p, j: (b, p, 0, j))
    return pl.pallas_call(
        body, name=name, grid=grid,
        in_specs=[blk2, blk, full2, full, full,
                  pl.BlockSpec((None, None, lp, LANES), lambda b, p, j: (b, p, 0, 0))] + [_HBM] * nr,
        out_specs=[full2, blk2, blk, dcs, dcs] + [_HBM] * nr,
        out_shape=[jax.ShapeDtypeStruct((m, 2 * FOX_WIDTH), F32)] * 2 + [jax.ShapeDtypeStruct((m, FOX_WIDTH), F32)]
        + [jax.ShapeDtypeStruct((nb, npair, 1, lp), F32)] * 2 + r_shapes,
        scratch_shapes=r_sems if nr else [],
        compiler_params=_cp(*(["arbitrary"] * 3 if nr else ["parallel", "parallel", "arbitrary"])),
    )(ka, vb, qa, do, o, lse, *r_arrays)


def _fox_prep_bwd(proj, dqa, dka, dv, dcum, qg, kg, bf, nb, lp, name):
    m = proj.shape[0]
    ts = SEQ_BLOCK
    tg = _hg_rows(lp)
    nblk = lp // tg
    scale = FOX_HEAD_DIM ** -0.5
    w = FOX_WIDTH
    wo = 3 * w

    def body(q_ref, k_ref, f_ref, dq_ref, dk_ref, dv_ref, dc_ref, qg_ref, kg_ref, bf_ref,
             out_ref, dff_ref, dqg_ref, dkg_ref, dbf_ref, carry_ref):
        @pl.when((pl.program_id(0) == 0) & (pl.program_id(1) == 0))
        def _():
            dqg_ref[...] = jnp.zeros_like(dqg_ref)
            dkg_ref[...] = jnp.zeros_like(dkg_ref)
            dbf_ref[...] = jnp.zeros_like(dbf_ref)

        @pl.when(pl.program_id(1) == 0)
        def _():
            carry_ref[...] = jnp.zeros_like(carry_ref)

        for hb in reversed(range(tg // ts)):
            rows = pl.ds(hb * ts, ts)
            block(q_ref.at[rows], k_ref.at[rows], f_ref.at[rows], dq_ref.at[rows], dk_ref.at[rows], dv_ref.at[rows],
                  dc_ref.at[rows], qg_ref, kg_ref, bf_ref, out_ref.at[rows], dff_ref.at[rows], dqg_ref, dkg_ref, dbf_ref,
                  carry_ref)

    def block(q_ref, k_ref, f_ref, dq_ref, dk_ref, dv_ref, dc_ref, qg_ref, kg_ref, bf_ref,
              out_ref, dff_ref, dqg_ref, dkg_ref, dbf_ref, carry_ref):
        lane = lax.broadcasted_iota(jnp.int32, (1, LANES), 1)
        lo = lane < FOX_HEAD_DIM

        def norm_bwd(x, g, dy):
            r = lax.rsqrt(_head_stats(x, lo) + EPS)
            nv = x * r
            gdy = dy * g
            prod = gdy * nv
            s_lo = jnp.sum(jnp.where(lo, prod, 0.0), axis=1, keepdims=True)
            s_hi = jnp.sum(jnp.where(lo, 0.0, prod), axis=1, keepdims=True)
            mean = jnp.where(lo, s_lo, s_hi) * (1.0 / FOX_HEAD_DIM)
            return r * (gdy - nv * mean), jnp.sum(dy * nv, axis=0, keepdims=True)

        def pair(d_ref, jj):
            even = d_ref[:, 2 * jj * LANES:(2 * jj + 1) * LANES]
            odd = d_ref[:, (2 * jj + 1) * LANES:(2 * jj + 2) * LANES]
            return jnp.where(lo, even, pltpu.roll(odd, FOX_HEAD_DIM, 1))

        for jj in range(w // LANES):
            cs = slice(jj * LANES, (jj + 1) * LANES)
            dx, dg = norm_bwd(q_ref[:, cs], qg_ref[:, cs], pair(dq_ref, jj) * scale)
            out_ref[:, cs] = dx.astype(BF16)
            dqg_ref[:, cs] += dg
            dx, dg = norm_bwd(k_ref[:, cs], kg_ref[:, cs], pair(dk_ref, jj))
            out_ref[:, w + jj * LANES:w + (jj + 1) * LANES] = dx.astype(BF16)
            dkg_ref[:, cs] += dg
        out_ref[:, 2 * w:3 * w] = dv_ref[...].astype(BF16)

        dc = dc_ref[...]
        r = lax.broadcasted_iota(jnp.int32, (ts, ts), 0)
        c = lax.broadcasted_iota(jnp.int32, (ts, ts), 1)
        triu = jnp.where(c >= r, 1.0, 0.0).astype(F32)
        dlogf = _dot(triu, dc, NN, HI) + carry_ref[...]
        carry_ref[...] += jnp.sum(dc, axis=0, keepdims=True)
        z = f_ref[...] + bf_ref[...]
        dz = jnp.where(lane < FOX_HEADS, dlogf * _sigmoid(-z), 0.0)
        dff_ref[...] = dz.astype(BF16)
        dbf_ref[...] += jnp.sum(dz, axis=0, keepdims=True)

    rev = lambda b, i: (b * nblk + (nblk - 1 - i), 0)
    vec = lambda n: pl.BlockSpec((1, n), lambda b, i: (0, 0))
    return pl.pallas_call(
        body, name=name, grid=(nb, nblk),
        in_specs=[pl.BlockSpec((tg, w), lambda b, i: (b * nblk + (nblk - 1 - i), FOX_CB)),
                  pl.BlockSpec((tg, w), lambda b, i: (b * nblk + (nblk - 1 - i), FOX_CB + 1)),
                  pl.BlockSpec((tg, LANES), lambda b, i: (b * nblk + (nblk - 1 - i), CB_FF)),
                  pl.BlockSpec((tg, 2 * w), rev), pl.BlockSpec((tg, 2 * w), rev), pl.BlockSpec((tg, w), rev),
                  pl.BlockSpec((tg, LANES), rev), vec(w), vec(w), vec(LANES)],
        out_specs=[pl.BlockSpec((tg, wo), rev), pl.BlockSpec((tg, LANES), rev), vec(w), vec(w), vec(LANES)],
        out_shape=[jax.ShapeDtypeStruct((m, wo), BF16), jax.ShapeDtypeStruct((m, LANES), BF16),
                   jax.ShapeDtypeStruct((1, w), F32),
                   jax.ShapeDtypeStruct((1, w), F32), jax.ShapeDtypeStruct((1, LANES), F32)],
        scratch_shapes=[pltpu.VMEM((1, LANES), F32)],
        compiler_params=_cp("arbitrary", "arbitrary"),
    )(proj, proj, proj, dqa, dka, dv, dcum, qg, kg, bf)


HG_BLOCKS_PER_STEP = 11


def _hg_rows(lp, most=3):
    nblk = lp // SEQ_BLOCK
    return SEQ_BLOCK * max(n for n in range(1, most + 1) if nblk % n == 0)


def _chunk_masks():
    r = lax.broadcasted_iota(jnp.int32, (SEQ_BLOCK, SEQ_BLOCK), 0)
    c = lax.broadcasted_iota(jnp.int32, (SEQ_BLOCK, SEQ_BLOCK), 1)
    same = (r // SUB) == (c // SUB)
    return r, c, same


def _hg_gates(hf, lb):
    sg = _sigmoid(hf)
    f = lb + (1.0 - lb) * sg
    return sg, f, jnp.log(f), (1.0 - lb) * _sigmoid(-hf)


def _hg_intra_e(g_ref, base, t, srow):
    diff = g_ref[pl.ds(base + t, 1), :] - g_ref[pl.ds(base, SUB), :]
    return jnp.exp(jnp.where(srow <= t, diff, NEG))


def _hgrn_fwd(proj, lb, gain, nb, lp, name):
    m = proj.shape[0]
    tb = _hg_rows(lp, HG_BLOCKS_PER_STEP)
    nblk = lp // tb
    nsb = tb // SEQ_BLOCK
    ns = SEQ_BLOCK // SUB

    def body(q_ref, f_ref, i_ref, g_ref, lb_ref, gain_ref, oraw_ref, y_ref, ssave_ref,
             st_ref, g_scr, kin_scr, o_scr):
        @pl.when(pl.program_id(2) == 0)
        def _():
            st_ref[...] = jnp.zeros_like(st_ref)

        for hb in range(nsb):
            rows = pl.ds(hb * SEQ_BLOCK, SEQ_BLOCK)
            block(q_ref.at[rows], f_ref.at[rows], i_ref.at[rows], g_ref.at[rows], lb_ref, gain_ref, oraw_ref.at[rows],
                  y_ref.at[rows], ssave_ref.at[hb], st_ref, g_scr.at[rows], kin_scr.at[rows], o_scr.at[rows])

    def block(q_ref, f_ref, i_ref, g_ref, lb_ref, gain_ref, oraw_ref, y_ref, ssave_ref,
              st_ref, g_scr, kin_scr, o_scr):
        ssave_ref[...] = st_ref[...]
        lbv = lb_ref[...]
        _, _, lf, kin = _hg_gates(f_ref[...], lbv)
        r, c, same = _chunk_masks()
        ltri = jnp.where(same & (c <= r), 1.0, 0.0).astype(F32)
        lall = jnp.where(same, 1.0, 0.0).astype(F32)
        g = _dot(ltri, lf, NN, HI)
        gt = _dot(lall, lf, NN, HI)
        g_scr[...] = g
        kin_scr[...] = kin
        qv = q_ref[...]
        qg = (qv * jnp.exp(g)).astype(BF16)
        kg = (kin * jnp.exp(gt - g)).astype(BF16)
        et = jnp.exp(gt)
        srow = lax.broadcasted_iota(jnp.int32, (SUB, 1), 0)
        subs = [slice(cc * SUB, (cc + 1) * SUB) for cc in range(ns)]
        ups = [_dot(i_ref[sl, :].astype(BF16), kg[sl], TN) for sl in subs]
        st = st_ref[...]
        starts = []
        for cc in range(ns):
            starts.append(st)
            st = et[cc * SUB:cc * SUB + 1, :] * st + ups[cc]
        st_ref[...] = st
        for cc, sl in enumerate(subs):
            base = cc * SUB
            kc = kin_scr[sl, :]
            vc = i_ref[sl, :]
            for t in range(SUB):
                e = _hg_intra_e(g_scr, base, t, srow)
                a = jnp.sum((q_ref[pl.ds(base + t, 1), :] * kc) * e, axis=1, keepdims=True)
                o_scr[pl.ds(base + t, 1), :] = jnp.sum(a * vc, axis=0, keepdims=True)
            o_scr[sl, :] += _dot(qg[sl], starts[cc].astype(BF16), NT)
        o = o_scr[...]
        oraw_ref[...] = o
        rr = lax.rsqrt(jnp.mean(o * o, axis=-1, keepdims=True) + EPS)
        hg = g_ref[...]
        y_ref[...] = (((o * rr) * gain_ref[...]) * (hg * _sigmoid(hg))).astype(y_ref.dtype)

    col = lambda cb: pl.BlockSpec((tb, LANES), lambda b, h, i, cb=cb: (b * nblk + i, cb + h))
    out_blk = pl.BlockSpec((tb, LANES), lambda b, h, i: (b * nblk + i, h))
    return pl.pallas_call(
        body, name=name, grid=(nb, HG_HEADS, nblk),
        in_specs=[col(CB_HQ), col(CB_HF), col(CB_HI), col(CB_HG),
                  pl.BlockSpec((1, LANES), lambda b, h, i: (0, h)),
                  pl.BlockSpec((1, LANES), lambda b, h, i: (0, 0))],
        out_specs=[out_blk, out_blk,
                   pl.BlockSpec((None, None, nsb, HG_DIM, HG_DIM), lambda b, h, i: (b, h, i, 0, 0))],
        out_shape=[jax.ShapeDtypeStruct((m, HG_WIDTH), F32), jax.ShapeDtypeStruct((m, HG_WIDTH), BF16),
                   jax.ShapeDtypeStruct((nb, HG_HEADS, nblk * nsb, HG_DIM, HG_DIM), F32)],
        scratch_shapes=[pltpu.VMEM((HG_DIM, HG_DIM), F32), pltpu.VMEM((tb, LANES), F32),
                        pltpu.VMEM((tb, LANES), F32), pltpu.VMEM((tb, LANES), F32)],
        compiler_params=_cp("parallel", "parallel", "arbitrary"),
    )(proj, proj, proj, proj, lb, gain)


def _hgrn_bwd(proj, oraw, ssave, dy, lb, gain, nb, lp, name):
    m = proj.shape[0]
    tb = _hg_rows(lp, HG_BLOCKS_PER_STEP)
    nblk = lp // tb
    nsb = tb // SEQ_BLOCK
    ns = SEQ_BLOCK // SUB

    def body(q_ref, f_ref, i_ref, g_ref, oraw_ref, ssave_ref, dy_ref, lb_ref, gain_ref,
             dq_ref, df_ref, di_ref, dg_ref, dgain_ref, dlb_ref,
             dst_ref, *scratch):
        hd = pl.program_id(0)
        bb = pl.program_id(1)
        ii = pl.program_id(2)

        @pl.when((hd == 0) & (bb == 0) & (ii == 0))
        def _():
            dgain_ref[...] = jnp.zeros_like(dgain_ref)

        @pl.when((bb == 0) & (ii == 0))
        def _():
            dlb_ref[...] = jnp.zeros_like(dlb_ref)

        @pl.when(ii == 0)
        def _():
            dst_ref[...] = jnp.zeros_like(dst_ref)

        for hb in reversed(range(nsb)):
            rows = pl.ds(hb * SEQ_BLOCK, SEQ_BLOCK)
            block(q_ref.at[rows], f_ref.at[rows], i_ref.at[rows], g_ref.at[rows], oraw_ref.at[rows], ssave_ref.at[hb],
                  dy_ref.at[rows], lb_ref, gain_ref, dq_ref.at[rows], df_ref.at[rows], di_ref.at[rows], dg_ref.at[rows],
                  dgain_ref, dlb_ref, dst_ref, *[sc.at[rows] for sc in scratch])

    def block(q_ref, f_ref, i_ref, g_ref, oraw_ref, ssave_ref, dy_ref, lb_ref, gain_ref,
              dq_ref, df_ref, di_ref, dg_ref, dgain_ref, dlb_ref,
              dst_ref, g_scr, kin_scr, do_scr, dq_scr, dk_scr, dv_scr, dgg_scr):
        gainv = gain_ref[...]
        lbv = lb_ref[...]

        o = oraw_ref[...]
        rr = lax.rsqrt(jnp.mean(o * o, axis=-1, keepdims=True) + EPS)
        nv = o * rr
        hg = g_ref[...]
        sgg = _sigmoid(hg)
        sil = hg * sgg
        dyv = dy_ref[...]
        dg_ref[...] = (dyv * nv * gainv * (sgg * (1.0 + hg * (1.0 - sgg)))).astype(dg_ref.dtype)
        dgain_ref[...] += jnp.sum(dyv * nv * sil, axis=0, keepdims=True)
        dn = dyv * gainv * sil
        do_scr[...] = rr * (dn - nv * jnp.mean(dn * nv, axis=-1, keepdims=True))

        hf = f_ref[...]
        sg, f, lf, kin = _hg_gates(hf, lbv)
        r, c, same = _chunk_masks()
        ltri = jnp.where(same & (c <= r), 1.0, 0.0).astype(F32)
        lall = jnp.where(same, 1.0, 0.0).astype(F32)
        g = _dot(ltri, lf, NN, HI)
        gt = _dot(lall, lf, NN, HI)
        g_scr[...] = g
        kin_scr[...] = kin
        qv = q_ref[...]
        eg = jnp.exp(g)
        ekg = jnp.exp(gt - g)
        qg = qv * eg
        kg = kin * ekg
        qg16 = qg.astype(BF16)
        kg16 = kg.astype(BF16)
        et = jnp.exp(gt)
        subs = [slice(cc * SUB, (cc + 1) * SUB) for cc in range(ns)]
        ups = [_dot(i_ref[sl, :].astype(BF16), kg16[sl], TN) for sl in subs]
        st = ssave_ref[...]
        starts = []
        for cc in range(ns):
            starts.append(st)
            st = et[cc * SUB:cc * SUB + 1, :] * st + ups[cc]
        do16 = do_scr[...].astype(BF16)
        downs = [_dot(do16[sl], qg16[sl], TN) for sl in subs]
        dst = dst_ref[...]
        afters = [None] * ns
        for cc in reversed(range(ns)):
            afters[cc] = dst
            dst = et[cc * SUB:cc * SUB + 1, :] * dst + downs[cc]
        dst_ref[...] = dst

        srow = lax.broadcasted_iota(jnp.int32, (SUB, 1), 0)
        for cc, sl in enumerate(subs):
            base = cc * SUB
            st = starts[cc]
            st16 = st.astype(BF16)
            dst = afters[cc]
            dst16 = dst.astype(BF16)
            doc16 = do16[sl]
            vc = i_ref[sl, :]
            vc16 = vc.astype(BF16)
            kc = kin_scr[sl, :]
            etc = et[base:base + 1, :]
            dqg = _dot(doc16, st16, NN)
            dv_c = _dot(kg16[sl], dst16, NT)
            dkg = _dot(vc16, dst16, NN)
            dgt = jnp.sum(dst * st, axis=0, keepdims=True) * etc
            dq_c = dqg * eg[sl]
            dk_c = dkg * ekg[sl]
            dg_c = dqg * qg[sl] - dkg * kg[sl]
            dgt = dgt + jnp.sum(dkg * kg[sl], axis=0, keepdims=True)
            for t in range(SUB):
                e = _hg_intra_e(g_scr, base, t, srow)
                qt = q_ref[pl.ds(base + t, 1), :]
                dot_t = do_scr[pl.ds(base + t, 1), :]
                a = jnp.sum((qt * kc) * e, axis=1, keepdims=True)
                da = jnp.sum(dot_t * vc, axis=1, keepdims=True)
                dv_c = dv_c + a * dot_t
                w = da * e
                dq_scr[pl.ds(base + t, 1), :] = jnp.sum(w * kc, axis=0, keepdims=True)
                wq = w * qt
                dk_c = dk_c + wq
                dg_c = dg_c - kc * wq
            dq_i = dq_scr[sl, :]
            dg_c = dg_c + qv[sl] * dq_i + jnp.where(srow == SUB - 1, dgt, 0.0)
            dq_scr[sl, :] = dq_c + dq_i
            dk_scr[sl, :] = dk_c
            dv_scr[sl, :] = dv_c
            dgg_scr[sl, :] = dg_c

        utri = jnp.where(same & (c >= r), 1.0, 0.0).astype(F32)
        dlf = _dot(utri, dgg_scr[...], NN, HI)
        dkin = dk_scr[...]
        dsg = sg * (1.0 - sg)
        df_ref[...] = ((dlf / f - dkin) * ((1.0 - lbv) * dsg)).astype(df_ref.dtype)
        dlb_ref[...] += jnp.sum((dlf / f - dkin) * (1.0 - sg), axis=0, keepdims=True)
        dq_ref[...] = dq_scr[...].astype(dq_ref.dtype)
        di_ref[...] = dv_scr[...].astype(di_ref.dtype)

    rowi = lambda b, i: b * nblk + (nblk - 1 - i)
    col = lambda cb: pl.BlockSpec((tb, LANES), lambda h, b, i, cb=cb: (rowi(b, i), cb + h))
    hblk = pl.BlockSpec((tb, LANES), lambda h, b, i: (rowi(b, i), h))
    return pl.pallas_call(
        body, name=name, grid=(HG_HEADS, nb, nblk),
        in_specs=[col(CB_HQ), col(CB_HF), col(CB_HI), col(CB_HG), hblk,
                  pl.BlockSpec((None, None, nsb, HG_DIM, HG_DIM), lambda h, b, i: (b, h, nblk - 1 - i, 0, 0)),
                  hblk,
                  pl.BlockSpec((1, LANES), lambda h, b, i: (0, h)),
                  pl.BlockSpec((1, LANES), lambda h, b, i: (0, 0))],
        out_specs=[hblk, hblk, hblk, hblk,
                   pl.BlockSpec((1, LANES), lambda h, b, i: (0, 0)),
                   pl.BlockSpec((1, LANES), lambda h, b, i: (0, h))],
        out_shape=[jax.ShapeDtypeStruct((m, HG_WIDTH), BF16)] * 4
        + [jax.ShapeDtypeStruct((1, LANES), F32), jax.ShapeDtypeStruct((1, HG_WIDTH), F32)],
        scratch_shapes=[pltpu.VMEM((HG_DIM, HG_DIM), F32)] + [pltpu.VMEM((tb, LANES), F32)] * 7,
        compiler_params=_cp("arbitrary", "arbitrary", "arbitrary"),
    )(proj, proj, proj, proj, oraw, ssave, dy, lb, gain)


def _gate_fwd(proj, ya, yb, name):
    m = proj.shape[0]
    tm = _div_tile(m, 256, 16)

    def body(ga_ref, gb_ref, ya_ref, yb_ref, o_ref):
        ya, yb = ya_ref[...].astype(F32), yb_ref[...].astype(F32)
        o_ref[...] = (_sigmoid(ga_ref[...]) * ya + _sigmoid(gb_ref[...]) * yb).astype(o_ref.dtype)

    row = pl.BlockSpec((tm, D_MODEL), lambda i: (i, 0))
    return pl.pallas_call(
        body, name=name, grid=(m // tm,),
        in_specs=[row, pl.BlockSpec((tm, D_MODEL), lambda i: (i, 1)), row, row],
        out_specs=row, out_shape=jax.ShapeDtypeStruct((m, D_MODEL), BF16),
        compiler_params=_cp("parallel"),
    )(proj, proj, ya, yb)


def _gate_bwd(proj, ya, yb, dm, name):
    m = proj.shape[0]
    tm = _div_tile(m, 256, 16)

    def body(ga_ref, gb_ref, ya_ref, yb_ref, dm_ref, dya_ref, dyb_ref, dg_ref):
        dmv = dm_ref[...].astype(F32)
        sa = _sigmoid(ga_ref[...])
        sb = _sigmoid(gb_ref[...])
        dya_ref[...] = (dmv * sa).astype(BF16)
        dyb_ref[...] = (dmv * sb).astype(BF16)
        dg_ref[:, :D_MODEL] = (dmv * ya_ref[...].astype(F32) * (sa * (1.0 - sa))).astype(BF16)
        dg_ref[:, D_MODEL:] = (dmv * yb_ref[...].astype(F32) * (sb * (1.0 - sb))).astype(BF16)

    row = pl.BlockSpec((tm, D_MODEL), lambda i: (i, 0))
    wide = pl.BlockSpec((tm, 2 * D_MODEL), lambda i: (i, 0))
    return pl.pallas_call(
        body, name=name, grid=(m // tm,),
        in_specs=[row, pl.BlockSpec((tm, D_MODEL), lambda i: (i, 1)), row, row, row],
        out_specs=[row, row, wide],
        out_shape=[jax.ShapeDtypeStruct((m, D_MODEL), BF16)] * 2 + [jax.ShapeDtypeStruct((m, 2 * D_MODEL), BF16)],
        compiler_params=_cp("parallel"),
    )(proj, proj, ya, yb, dm)


CONV_ROWS = 128


def _conv3(x, xprev, w_ref, b_ref, rowi):
    r = x.shape[0]
    x1 = jnp.where(rowi < 1, pltpu.roll(xprev, 1, 0), pltpu.roll(x, 1, 0))
    x2 = jnp.where(rowi < 2, pltpu.roll(xprev, 2, 0), pltpu.roll(x, 2, 0))
    u = w_ref[0:1, :] * x2 + w_ref[1:2, :] * x1 + w_ref[2:3, :] * x + b_ref[...]
    return u, x1, x2


def _conv_fwd(up, cw, cb, nb, lp, name):
    m = up.shape[0]
    nct = D_FF // LANES
    r = CONV_ROWS
    nch = lp // r

    def body(u_ref, w_ref, b_ref, o_ref):
        rowi = lax.broadcasted_iota(jnp.int32, (r, 1), 0)

        def step(i, xp):
            r0 = pl.multiple_of(i * r, r)
            xc = u_ref[pl.ds(r0, r), :].astype(F32)
            u, _, _ = _conv3(xc, xp, w_ref, b_ref, rowi)
            ug, uv = u[:, :LANES], u[:, LANES:]
            o_ref[pl.ds(r0, r), :] = ((ug * _sigmoid(ug)) * uv).astype(o_ref.dtype)
            return xc

        lax.fori_loop(0, nch, step, jnp.zeros((r, 2 * LANES), F32))

    return pl.pallas_call(
        body, name=name, grid=(nb, nct),
        in_specs=[pl.BlockSpec((lp, 2 * LANES), lambda b, c: (b, c)),
                  pl.BlockSpec((CONV_WIDTH, 2 * LANES), lambda b, c: (0, c)),
                  pl.BlockSpec((1, 2 * LANES), lambda b, c: (0, c))],
        out_specs=pl.BlockSpec((lp, LANES), lambda b, c: (b, c)),
        out_shape=jax.ShapeDtypeStruct((m, D_FF), BF16),
        compiler_params=_cp("parallel", "parallel"),
    )(up, cw, cb)


def _conv_bwd(up, dact, cw, cb, nb, lp, name):
    m = up.shape[0]
    nct = D_FF // LANES
    r = CONV_ROWS
    nch = lp // r

    def body(u_ref, da_ref, w_ref, b_ref, dup_ref, dw_ref, db_ref):
        rowi = lax.broadcasted_iota(jnp.int32, (r, 1), 0)
        wv = w_ref[...]

        def step(k, carry):
            dun, dw0, dw1, dw2, dbs = carry
            i = nch - 1 - k
            r0 = pl.multiple_of(i * r, r)
            rp = pl.multiple_of(jnp.maximum(i - 1, 0) * r, r)
            xc = u_ref[pl.ds(r0, r), :].astype(F32)
            xp = u_ref[pl.ds(rp, r), :].astype(F32) * (i > 0).astype(F32)
            u, x1, x2 = _conv3(xc, xp, w_ref, b_ref, rowi)
            ug, uv = u[:, :LANES], u[:, LANES:]
            da = da_ref[pl.ds(r0, r), :].astype(F32)
            sg = _sigmoid(ug)
            du = jnp.concatenate([da * uv * (sg * (1.0 + ug * (1.0 - sg))), da * (ug * sg)], axis=1)
            d1 = jnp.where(rowi >= r - 1, pltpu.roll(dun, r - 1, 0), pltpu.roll(du, r - 1, 0))
            d2 = jnp.where(rowi >= r - 2, pltpu.roll(dun, r - 2, 0), pltpu.roll(du, r - 2, 0))
            dup_ref[pl.ds(r0, r), :] = (wv[2:3, :] * du + wv[1:2, :] * d1 + wv[0:1, :] * d2).astype(dup_ref.dtype)
            dw0 = dw0 + jnp.sum(du * x2, axis=0, keepdims=True)
            dw1 = dw1 + jnp.sum(du * x1, axis=0, keepdims=True)
            dw2 = dw2 + jnp.sum(du * xc, axis=0, keepdims=True)
            dbs = dbs + jnp.sum(du, axis=0, keepdims=True)
            return du, dw0, dw1, dw2, dbs

        z1 = jnp.zeros((1, 2 * LANES), F32)
        _, dw0, dw1, dw2, dbs = lax.fori_loop(0, nch, step, (jnp.zeros((r, 2 * LANES), F32), z1, z1, z1, z1))

        @pl.when(pl.program_id(1) == 0)
        def _():
            dw_ref[...] = jnp.zeros_like(dw_ref)
            db_ref[...] = jnp.zeros_like(db_ref)

        dw_ref[0:1, :] += dw0
        dw_ref[1:2, :] += dw1
        dw_ref[2:3, :] += dw2
        db_ref[...] += dbs

    return pl.pallas_call(
        body, name=name, grid=(nct, nb),
        in_specs=[pl.BlockSpec((lp, 2 * LANES), lambda c, b: (b, c)),
                  pl.BlockSpec((lp, LANES), lambda c, b: (b, c)),
                  pl.BlockSpec((CONV_WIDTH, 2 * LANES), lambda c, b: (0, c)),
                  pl.BlockSpec((1, 2 * LANES), lambda c, b: (0, c))],
        out_specs=[pl.BlockSpec((lp, 2 * LANES), lambda c, b: (b, c)),
                   pl.BlockSpec((CONV_WIDTH, 2 * LANES), lambda c, b: (0, c)),
                   pl.BlockSpec((1, 2 * LANES), lambda c, b: (0, c))],
        out_shape=[jax.ShapeDtypeStruct((m, 2 * D_FF), BF16),
                   jax.ShapeDtypeStruct((CONV_WIDTH, 2 * D_FF), F32),
                   jax.ShapeDtypeStruct((1, 2 * D_FF), F32)],
        compiler_params=_cp("parallel", "arbitrary"),
    )(up, dact, cw, cb)


def _ffn_interleave(a, axis):
    shp = a.shape
    a = a.reshape(shp[:axis] + (2, D_FF // LANES, LANES) + shp[axis + 1:])
    return jnp.swapaxes(a, axis, axis + 1).reshape(shp)


def _ffn_deinterleave(a, axis):
    shp = a.shape
    a = a.reshape(shp[:axis] + (D_FF // LANES, 2, LANES) + shp[axis + 1:])
    return jnp.swapaxes(a, axis, axis + 1).reshape(shp)


def _shifted_rows(prev_ref, cur_ref):
    keep = SEQ_BLOCK - N_META
    return jnp.concatenate([prev_ref[keep:, :], cur_ref[:keep, :]], axis=0)


def _frame_specs(nblk, nfb, d):
    prev = pl.BlockSpec((SEQ_BLOCK, d), lambda b, i: (b * nfb + jnp.clip(i - 1, 0, nfb - 1), 0))
    cur = pl.BlockSpec((SEQ_BLOCK, d), lambda b, i: (b * nfb + jnp.clip(i, 0, nfb - 1), 0))
    return prev, cur


def _embed_rms(x2, meta, gain, nb, lp, l, name):
    d = x2.shape[1]
    tr = SEQ_BLOCK
    nblk = lp // tr
    nfb = (l - N_META) // tr
    m = nb * lp

    def body(prev_ref, cur_ref, meta_ref, g_ref, h_ref, o_ref):
        i = pl.program_id(1)
        t = i * tr + lax.broadcasted_iota(jnp.int32, (tr, 1), 0)
        rows = jnp.where(t < l, _shifted_rows(prev_ref, cur_ref), 0.0)
        head = jnp.concatenate([meta_ref[...], jnp.zeros((tr - N_META, d), F32)], axis=0)
        xv = jnp.where(t < N_META, head, rows)
        h_ref[...] = xv
        r = lax.rsqrt(jnp.mean(xv * xv, axis=-1, keepdims=True) + EPS)
        o_ref[...] = ((xv * r) * g_ref[...]).astype(o_ref.dtype)

    prev, cur = _frame_specs(nblk, nfb, d)
    row = pl.BlockSpec((tr, d), lambda b, i: (b * nblk + i, 0))
    return pl.pallas_call(
        body, name=name, grid=(nb, nblk),
        in_specs=[prev, cur, pl.BlockSpec((N_META, d), lambda b, i: (0, 0)), pl.BlockSpec((1, d), lambda b, i: (0, 0))],
        out_specs=[row, row],
        out_shape=[jax.ShapeDtypeStruct((m, d), F32), jax.ShapeDtypeStruct((m, d), BF16)],
        compiler_params=_cp("parallel", "parallel"),
    )(x2, x2, meta, gain)


def _loss_head(out, tgt2, nb, lp, l, name):
    m, d = out.shape
    tr = SEQ_BLOCK
    nblk = lp // tr
    nfb = (l - N_META) // tr

    def body(o_ref, prev_ref, cur_ref, dy_ref, ls_ref):
        t = pl.program_id(1) * tr + lax.broadcasted_iota(jnp.int32, (tr, 1), 0)
        valid = (t >= N_META) & (t < l)
        err = jnp.where(valid, o_ref[...] - _shifted_rows(prev_ref, cur_ref), 0.0)
        dy_ref[...] = err * (1.0 / d)
        part = jnp.sum(err * err, axis=0, keepdims=True)
        first = (pl.program_id(0) == 0) & (pl.program_id(1) == 0)

        @pl.when(first)
        def _():
            ls_ref[...] = part

        @pl.when(jnp.logical_not(first))
        def _():
            ls_ref[...] += part

    prev, cur = _frame_specs(nblk, nfb, d)
    row = pl.BlockSpec((tr, d), lambda b, i: (b * nblk + i, 0))
    return pl.pallas_call(
        body, name=name, grid=(nb, nblk),
        in_specs=[row, prev, cur], out_specs=[row, pl.BlockSpec((1, d), lambda b, i: (0, 0))],
        out_shape=[jax.ShapeDtypeStruct((m, d), F32), jax.ShapeDtypeStruct((1, d), F32)],
        compiler_params=_cp("arbitrary", "arbitrary"),
    )(out, tgt2, tgt2)


def _adam_math(g, w, mom, var):
    c1 = 1.0 - ADAM_B1 ** ADAM_STEP
    c2 = 1.0 - ADAM_B2 ** ADAM_STEP
    mn = ADAM_B1 * mom + (1.0 - ADAM_B1) * g
    vn = ADAM_B2 * var + (1.0 - ADAM_B2) * (g * g)
    delta = -ADAM_LR * ((mn / c1) / (jnp.sqrt(vn / c2) + ADAM_EPS) + ADAM_WD * w)
    return delta, mn, vn


def _slot_sum(recv, name):
    _, r, c = recv.shape
    tc = _div_tile(c, 256, LANES)

    def body(r_ref, g_ref):
        g = r_ref[0].astype(F32)
        for s in range(1, N_DEV):
            g = g + r_ref[s].astype(F32)
        g_ref[...] = g

    return pl.pallas_call(
        body, name=name, grid=(c // tc,),
        in_specs=[pl.BlockSpec((N_DEV, r, tc), lambda j: (0, 0, j))],
        out_specs=pl.BlockSpec((r, tc), lambda j: (0, j)),
        out_shape=jax.ShapeDtypeStruct((r, c), F32),
        compiler_params=_cp("parallel"),
    )(recv)


def _adamw(g, w, mom, var, name):
    r, c = w.shape
    tr = _div_tile(r, 256, 8)

    def body(g_ref, w_ref, m_ref, v_ref, d_ref, mo_ref, vo_ref):
        d_ref[...], mo_ref[...], vo_ref[...] = _adam_math(g_ref[...], w_ref[...], m_ref[...], v_ref[...])

    row = pl.BlockSpec((tr, c), lambda i: (i, 0))
    return pl.pallas_call(
        body, name=name, grid=(r // tr,), in_specs=[row] * 4, out_specs=[row] * 3,
        out_shape=[jax.ShapeDtypeStruct((r, c), F32)] * 3,
        compiler_params=_cp("parallel"),
    )(g, w, mom, var)


def _sum_adamw(recv, w, mom, var, name):
    r, c = w.shape
    tr = _div_tile(r, 256, 8)

    def body(r_ref, w_ref, m_ref, v_ref, g_ref, d_ref, mo_ref, vo_ref):
        g = r_ref[0].astype(F32)
        for s in range(1, N_DEV):
            g = g + r_ref[s].astype(F32)
        g_ref[...] = g
        d_ref[...], mo_ref[...], vo_ref[...] = _adam_math(g, w_ref[...], m_ref[...], v_ref[...])

    row = pl.BlockSpec((tr, c), lambda i: (i, 0))
    return pl.pallas_call(
        body, name=name, grid=(r // tr,),
        in_specs=[pl.BlockSpec((N_DEV, tr, c), lambda i: (0, i, 0)), row, row, row],
        out_specs=[row] * 4,
        out_shape=[jax.ShapeDtypeStruct((r, c), F32)] * 4,
        compiler_params=_cp("parallel"),
    )(recv, w, mom, var)


_MESH = pl.DeviceIdType.MESH
_HBM = pl.BlockSpec(memory_space=pltpu.HBM)
N_PEER = N_DEV - 1


def _position():
    return lax.axis_index("x"), lax.axis_index("y"), lax.axis_index("c")


def _all_gather(shards, name):
    n = len(shards)

    def body(*refs):
        x_refs, out_refs = refs[:n], refs[n:2 * n]
        send_sems, recv_sems, local_sems = refs[2 * n:]
        x, y, c = _position()
        me, sibling = (x, y, c), (x, y, 1 - c)
        chips = [(1 - x, y), (x, 1 - y), (1 - x, 1 - y)]

        def copy(a, k, block, to, src=None):
            slot = out_refs[a].at[4 * block[0] + 2 * block[1] + block[2]]
            return pltpu.make_async_remote_copy(
                src_ref=slot if src is None else src, dst_ref=slot,
                send_sem=send_sems.at[a * N_PEER + k], recv_sem=recv_sems.at[a * N_PEER + k],
                device_id=to, device_id_type=_MESH)

        mine, sent = [], []
        for a in range(n):
            cp = pltpu.make_async_copy(x_refs[a], out_refs[a].at[4 * x + 2 * y + c], local_sems.at[a])
            cp.start()
            mine.append(cp)
            first = [copy(a, 0, me, sibling, src=x_refs[a])]
            first += [copy(a, 1 + j, me, (*chip, c), src=x_refs[a]) for j, chip in enumerate(chips)]
            for cp in first:
                cp.start()
            sent += first
        for a in range(n):
            for j, chip in enumerate(chips):
                copy(a, 1 + j, (*chip, c), me).wait_recv()
                fwd = copy(a, 4 + j, (*chip, c), sibling)
                fwd.start()
                sent.append(fwd)
        for a in range(n):
            copy(a, 0, sibling, me).wait_recv()
            for j, chip in enumerate(chips):
                copy(a, 4 + j, (*chip, 1 - c), me).wait_recv()
        for cp in sent:
            cp.wait_send()
        for cp in mine:
            cp.wait()

    return pl.pallas_call(
        body, name=name,
        out_shape=[jax.ShapeDtypeStruct((N_DEV,) + a.shape, a.dtype) for a in shards],
        in_specs=[_HBM] * n, out_specs=[_HBM] * n,
        scratch_shapes=[pltpu.SemaphoreType.DMA((n * N_PEER,)), pltpu.SemaphoreType.DMA((n * N_PEER,)),
                        pltpu.SemaphoreType.DMA((n,))],
    )(*shards)


_FLIPS = [(fx, fy, fc) for fx in (0, 1) for fy in (0, 1) for fc in (0, 1)][1:]


def _exchange_copies(in_refs, out_refs, nblk, send_sems, recv_sems, local_sems):
    n = len(in_refs)
    x, y, c = _position()
    me = 4 * x + 2 * y + c

    def peer(f):
        return (1 - x if f[0] else x, 1 - y if f[1] else y, 1 - c if f[2] else c)

    def idx(p):
        return 4 * p[0] + 2 * p[1] + p[2]

    def local(a):
        return pltpu.make_async_copy(in_refs[a].at[me] if a < nblk else in_refs[a], out_refs[a].at[me], local_sems.at[a])

    def remote(a, k, sending):
        p = peer(_FLIPS[k])
        src = in_refs[a].at[idx(p)] if a < nblk else in_refs[a]
        dst = out_refs[a].at[me] if sending else out_refs[a].at[idx(p)]
        return pltpu.make_async_remote_copy(
            src_ref=src, dst_ref=dst, send_sem=send_sems.at[a * N_PEER + k], recv_sem=recv_sems.at[a * N_PEER + k],
            device_id=p, device_id_type=_MESH)

    def start():
        for a in range(n):
            local(a).start()
            for k in range(N_PEER):
                remote(a, k, True).start()

    def wait():
        for a in range(n):
            for k in range(N_PEER):
                remote(a, k, False).wait_recv()
        for a in range(n):
            for k in range(N_PEER):
                remote(a, k, True).wait_send()
            local(a).wait()

    return start, wait


def _exchange_io(blocks, shared):
    arrays = list(blocks) + list(shared)
    n = len(arrays)
    out_shape = [jax.ShapeDtypeStruct(a.shape, a.dtype) for a in blocks]
    out_shape += [jax.ShapeDtypeStruct((N_DEV,) + a.shape, a.dtype) for a in shared]
    sems = [pltpu.SemaphoreType.DMA((n * N_PEER,)), pltpu.SemaphoreType.DMA((n * N_PEER,)), pltpu.SemaphoreType.DMA((n,))]
    return arrays, out_shape, sems


def _exchange(blocks, shared, name):
    arrays, out_shape, sems = _exchange_io(blocks, shared)
    n = len(arrays)

    def body(*refs):
        start, wait = _exchange_copies(refs[:n], refs[n:2 * n], len(blocks), *refs[2 * n:])
        start()
        wait()

    return pl.pallas_call(
        body, name=name, out_shape=out_shape, in_specs=[_HBM] * n, out_specs=[_HBM] * n, scratch_shapes=sems,
    )(*arrays)


def _grid_ends(grid):
    ids = [pl.program_id(i) for i in range(len(grid))]
    first = functools.reduce(jnp.logical_and, [i == 0 for i in ids])
    last = functools.reduce(jnp.logical_and, [i == g - 1 for i, g in zip(ids, grid)])
    return first, last


def _pack(parts, rows):
    flat = jnp.concatenate(parts, axis=-1)
    return jnp.pad(flat, [(0, rows * LANES - flat.shape[-1])]).reshape(rows, LANES)


def _unpack(packed, shapes):
    flat = packed.reshape(-1)
    out, off = [], 0
    for shp in shapes:
        n = int(np.prod(shp))
        out.append(flat[off:off + n].reshape(shp))
        off += n
    return out


def _rows_for(shapes, extra=0):
    n = sum(int(np.prod(s)) for s in shapes) + extra
    return -(-n // (8 * LANES)) * 8


def _lower_bound(logits):
    return jnp.cumsum(jax.nn.softmax(logits.astype(F32), axis=0), axis=0)[0:1]


def _align_axis0(w):
    a, b = 3 * FOX_WIDTH, 3 * FOX_WIDTH + FOX_HEADS
    c = b + 4 * HG_WIDTH
    pad = [(0, LANES - FOX_HEADS)] + [(0, 0)] * (w.ndim - 1)
    return jnp.concatenate([w[c:], w[:a], w[b:c], jnp.pad(w[a:b], pad)], axis=0)


def _unalign_axis0(g):
    a, b = 2 * D_MODEL, 2 * D_MODEL + 3 * FOX_WIDTH
    c = b + 4 * HG_WIDTH
    return jnp.concatenate([g[a:b], g[c:c + FOX_HEADS], g[b:c], g[:a]], axis=0)


TINY_COLS = 768


def _tiny_pack(conv_w_shard, meta_shard):
    cw = jnp.pad(conv_w_shard, ((0, 8 - CONV_WIDTH), (0, TINY_COLS - conv_w_shard.shape[1])))
    mt = jnp.pad(meta_shard, ((0, 0), (0, TINY_COLS - meta_shard.shape[1])))
    return jnp.concatenate([cw, mt], axis=0)


def _tiny_unpack(t, ncw, nmeta):
    return t[..., :CONV_WIDTH, :ncw], t[..., 8:8 + N_META, :nmeta]


def _late_weights(g_up, g_down, g_ab, g_out):
    d = g_up.shape[-1]
    w_a_t = g_ab[:, 0].reshape(-1, g_ab.shape[-1])
    w_b_t = g_ab[:, 1].reshape(-1, g_ab.shape[-1])
    return _ffn_interleave(g_up.reshape(-1, d), 0), g_down.reshape(-1, d), w_a_t, w_b_t, g_out.reshape(-1, d)


def _early_blocks(g_w_up_t, g_w_down, g_w_out, g_w_a_t, g_w_b_t):
    d = g_w_out.shape[-1]
    ab = jnp.stack([g_w_a_t.reshape(N_DEV, -1, g_w_a_t.shape[-1]), g_w_b_t.reshape(N_DEV, -1, g_w_b_t.shape[-1])], axis=1)
    return [_ffn_deinterleave(g_w_up_t, 0).reshape(N_DEV, -1, d).astype(BF16), g_w_down.reshape(N_DEV, -1, d).astype(BF16),
            g_w_out.reshape(N_DEV, -1, d).astype(BF16), ab.astype(BF16)]


def _local_step(x, target, meta, norm1_gain, w_in_t, fox_b_f, q_gain, k_gain, lb, hg_out_gain, w_a_t, w_b_t, w_out,
                norm2_gain, w_up_t, conv_w, conv_b, w_down, ffn_shards=None):
    nb, seq, d = x.shape
    assert seq % SEQ_BLOCK == 0 and N_META < SEQ_BLOCK
    l = seq + N_META
    lp = -(-l // SEQ_BLOCK) * SEQ_BLOCK
    m = nb * lp
    qg = jnp.tile(q_gain, (1, FOX_HEADS))
    kg = jnp.tile(k_gain, (1, FOX_HEADS))
    bf = jnp.pad(fox_b_f, ((0, 0), (0, LANES - FOX_HEADS)))

    h0, xn = _embed_rms(x.reshape(nb * seq, d), meta, norm1_gain, nb, lp, l, "embed_rms1")
    proj = _matmul(xn, w_in_t, "nt", F32, "proj_in")
    qa, ka, vb = _fox_prep(proj, qg, kg, bf, nb, lp, "fox_prep")
    if ffn_shards is None:
        o_fox, lse = _fox_fwd(qa, ka, vb, nb, lp, "fox_fwd")
    else:
        o_fox, lse, *late = _fox_fwd(qa, ka, vb, nb, lp, "fox_fwd", ride=ffn_shards)
        w_up_t, w_down, w_a_t, w_b_t, w_out = _late_weights(*late)
    o_raw, o_hg, s_save = _hgrn_fwd(proj, lb, hg_out_gain, nb, lp, "hgrn_fwd")
    ya = _matmul(o_hg, w_a_t, "nt", BF16, "branch_a")
    yb = _matmul(o_fox, w_b_t, "nt", BF16, "branch_b")
    merged = _gate_fwd(proj, ya, yb, "gate_fwd")
    h1 = _matmul(merged, w_out, "nn", F32, "mix_out", residual=h0)
    hn = _rms_fwd(h1, norm2_gain, "rms2_fwd")
    up = _matmul(hn, w_up_t, "nt", BF16, "ffn_up")
    act = _conv_fwd(up, conv_w, conv_b, nb, lp, "conv_fwd")
    out = _matmul(act, w_down, "nn", F32, "ffn_down", residual=h1)
    dy, lsum = _loss_head(out, target.reshape(nb * seq, d), nb, lp, l, "loss_head")
    loss = (0.5 / d) * jnp.sum(lsum)

    dact = _matmul(dy, w_down, "nt", BF16, "d_act")
    g_w_down = _matmul(act, dy, "tn", F32, "g_w_down")
    dup, g_conv_w, g_conv_b = _conv_bwd(up, dact, conv_w, conv_b, nb, lp, "conv_bwd")
    dhn = _matmul(dup, w_up_t, "nn", F32, "d_hn")
    g_w_up_t = _matmul(dup, hn, "tn", F32, "g_w_up")
    dh1, g_norm2 = _rms_bwd(h1, norm2_gain, dhn, dy, "rms2_bwd")

    dmerged = _matmul(dh1, w_out, "nt", BF16, "d_merged")
    g_w_out = _matmul(merged, dh1, "tn", F32, "g_w_out")
    dya, dyb, dgab = _gate_bwd(proj, ya, yb, dmerged, "gate_bwd")
    do_hg = _matmul(dya, w_a_t, "nn", F32, "d_o_hg")
    g_w_a_t = _matmul(dya, o_hg, "tn", F32, "g_w_a")
    do_fox = _matmul(dyb, w_b_t, "nn", BF16, "d_o_fox")
    g_w_b_t = _matmul(dyb, o_fox, "tn", F32, "g_w_b")
    dhq, dhf, dhi, dhg, g_hg_gain, g_lb = _hgrn_bwd(proj, o_raw, s_save, do_hg, lb, hg_out_gain, nb, lp, "hgrn_bwd")
    if ffn_shards is None:
        dqs, dkn, dvv, dc0, dc1 = _fox_bwd(qa, ka, vb, do_fox, o_fox, lse, nb, lp, "fox_bwd")
        early = None
    else:
        dqs, dkn, dvv, dc0, dc1, *early = _fox_bwd(qa, ka, vb, do_fox, o_fox, lse, nb, lp, "fox_bwd",
                                                   ride=_early_blocks(g_w_up_t, g_w_down, g_w_out, g_w_a_t, g_w_b_t))
    dcum = jnp.stack([dc0, dc1], axis=2).reshape(nb, FOX_HEADS, lp)
    dcum = jnp.pad(jnp.transpose(dcum, (0, 2, 1)), ((0, 0), (0, 0), (0, LANES - FOX_HEADS))).reshape(m, LANES)
    dfqkv, dff, g_qg, g_kg, g_bf = _fox_prep_bwd(proj, dqs, dkn, dvv, dcum, qg, kg, bf, nb, lp, "fox_prep_bwd")
    dproj = jnp.concatenate([dgab, dfqkv, dhq, dhf, dhi, dhg, dff], axis=1)
    g_w_in_t = _matmul(dproj, xn, "tn", F32, "g_w_in")
    if ffn_shards is None:
        dxn = _matmul(dproj, w_in_t, "nn", F32, "d_xn")
    else:
        blocks_in = _unalign_axis0(g_w_in_t).reshape(N_DEV, -1, d).astype(BF16)
        dxn, r_in = _matmul(dproj, w_in_t, "nn", F32, "d_xn", ride=[blocks_in])
        early = early + [r_in]
    dh0, g_norm1 = _rms_bwd(h0, norm1_gain, dxn, dh1, "rms1_bwd")

    dh0 = dh0.reshape(nb, lp, d)
    grad_x = dh0[:, N_META:l]
    g_meta = jnp.sum(dh0[:, :N_META], axis=0)
    g_q_gain = jnp.sum(g_qg.reshape(FOX_HEADS, FOX_HEAD_DIM), axis=0, keepdims=True)
    g_k_gain = jnp.sum(g_kg.reshape(FOX_HEADS, FOX_HEAD_DIM), axis=0, keepdims=True)
    grads = dict(meta_tokens=g_meta, norm1_gain=g_norm1, w_in_t=g_w_in_t, fox_b_f=g_bf[:, :FOX_HEADS],
                 q_norm_gain=g_q_gain, k_norm_gain=g_k_gain, lb=g_lb, hg_out_gain=g_hg_gain,
                 w_a_t=g_w_a_t, w_b_t=g_w_b_t, w_out=g_w_out, norm2_gain=g_norm2, w_up_t=g_w_up_t,
                 conv_w=g_conv_w, conv_b=g_conv_b, w_down=g_w_down, early=early)
    return loss, grad_x, grads


SMALL = ("norm1_gain", "fox_b_f", "q_norm_gain", "k_norm_gain", "hg_lb_logits", "hg_out_gain", "norm2_gain", "conv_b")
ORDER = ("meta_tokens", "norm1_gain", "w_in", "fox_b_f", "q_norm_gain", "k_norm_gain", "hg_lb_logits", "hg_out_gain",
         "w_branch_a", "w_branch_b", "w_out", "norm2_gain", "w_up", "conv_w", "conv_b", "w_down")


def kernel(x, meta_tokens, norm1_gain, w_in, fox_b_f, q_norm_gain, k_norm_gain, hg_lb_logits, hg_out_gain, w_branch_a, w_branch_b, w_out, norm2_gain, w_up, conv_w, conv_b, w_down, loss_target, m_meta_tokens, m_norm1_gain, m_w_in, m_fox_b_f, m_q_norm_gain, m_k_norm_gain, m_hg_lb_logits, m_hg_out_gain, m_w_branch_a, m_w_branch_b, m_w_out, m_norm2_gain, m_w_up, m_conv_w, m_conv_b, m_w_down, v_meta_tokens, v_norm1_gain, v_w_in, v_fox_b_f, v_q_norm_gain, v_k_norm_gain, v_hg_lb_logits, v_hg_out_gain, v_w_branch_a, v_w_branch_b, v_w_out, v_norm2_gain, v_w_up, v_conv_w, v_conv_b, v_w_down):
    w = dict(meta_tokens=meta_tokens, norm1_gain=norm1_gain, w_in=w_in, fox_b_f=fox_b_f, q_norm_gain=q_norm_gain,
             k_norm_gain=k_norm_gain, hg_lb_logits=hg_lb_logits, hg_out_gain=hg_out_gain, w_branch_a=w_branch_a,
             w_branch_b=w_branch_b, w_out=w_out, norm2_gain=norm2_gain, w_up=w_up, conv_w=conv_w, conv_b=conv_b,
             w_down=w_down)
    mom = dict(meta_tokens=m_meta_tokens, norm1_gain=m_norm1_gain, w_in=m_w_in, fox_b_f=m_fox_b_f,
               q_norm_gain=m_q_norm_gain, k_norm_gain=m_k_norm_gain, hg_lb_logits=m_hg_lb_logits,
               hg_out_gain=m_hg_out_gain, w_branch_a=m_w_branch_a, w_branch_b=m_w_branch_b, w_out=m_w_out,
               norm2_gain=m_norm2_gain, w_up=m_w_up, conv_w=m_conv_w, conv_b=m_conv_b, w_down=m_w_down)
    var = dict(meta_tokens=v_meta_tokens, norm1_gain=v_norm1_gain, w_in=v_w_in, fox_b_f=v_fox_b_f,
               q_norm_gain=v_q_norm_gain, k_norm_gain=v_k_norm_gain, hg_lb_logits=v_hg_lb_logits,
               hg_out_gain=v_hg_out_gain, w_branch_a=v_w_branch_a, w_branch_b=v_w_branch_b, w_out=v_w_out,
               norm2_gain=v_norm2_gain, w_up=v_w_up, conv_w=v_conv_w, conv_b=v_conv_b, w_down=v_w_down)
    d = D_MODEL
    n_in, n_up = w_in.shape[2], w_up.shape[2]
    n_ab, n_meta = w_branch_a.shape[2], meta_tokens.shape[1]

    g_in, g_tiny = _all_gather([w_in[0].T.astype(BF16), _tiny_pack(conv_w[0], meta_tokens)], "gather_weights")
    w_in_t = _align_axis0(g_in.reshape(N_DEV * n_in, d))
    cw_slots, meta_slots = _tiny_unpack(g_tiny, n_up, n_meta)
    conv_w_f = _ffn_interleave(jnp.transpose(cw_slots, (1, 0, 2)).reshape(CONV_WIDTH, -1), 1)
    meta_f = jnp.transpose(meta_slots, (1, 0, 2)).reshape(N_META, -1)
    conv_b_i = _ffn_interleave(conv_b, 1)

    lb, lb_vjp = jax.vjp(_lower_bound, hg_lb_logits)
    loss, grad_x, g = _local_step(
        x, loss_target, meta_f, norm1_gain, w_in_t, fox_b_f, q_norm_gain, k_norm_gain, lb, hg_out_gain,
        None, None, None, norm2_gain, None, conv_w_f, conv_b_i, None,
        ffn_shards=(w_up[0].T.astype(BF16), w_down[0].astype(BF16),
                    jnp.stack([w_branch_a[0].T, w_branch_b[0].T]).astype(BF16), w_out[0].astype(BF16)))

    g["hg_lb_logits"] = lb_vjp(g.pop("lb"))[0]
    g["conv_b"] = _ffn_deinterleave(g["conv_b"], 1)
    gcw = _ffn_deinterleave(g["conv_w"], 1).reshape(CONV_WIDTH, N_DEV, n_up)
    gmeta = g["meta_tokens"].reshape(N_META, N_DEV, n_meta)
    tiny = jnp.concatenate([
        jnp.pad(jnp.transpose(gcw, (1, 0, 2)), ((0, 0), (0, 8 - CONV_WIDTH), (0, TINY_COLS - n_up))),
        jnp.pad(jnp.transpose(gmeta, (1, 0, 2)), ((0, 0), (0, 0), (0, TINY_COLS - n_meta)))], axis=1)
    small_shapes = [w[n].shape for n in SMALL]
    rows_sm = _rows_for(small_shapes, extra=1)
    small = _pack([g[n].reshape(-1) for n in SMALL] + [loss.reshape(1)], rows_sm)
    r_tiny, r_small = _exchange([tiny], [small], "exchange_grads")
    r_up, r_down, r_out, r_ab, r_in = g["early"]

    res = {}
    g_in_s = _slot_sum(r_in, "sum_w_in").T
    res["w_in"] = (g_in_s,) + tuple(_adamw(g_in_s, w_in[0], m_w_in[0], v_w_in[0], "adamw_w_in"))
    g_up_s = _slot_sum(r_up, "sum_w_up").T
    res["w_up"] = (g_up_s,) + tuple(_adamw(g_up_s, w_up[0], m_w_up[0], v_w_up[0], "adamw_w_up"))
    g_ab_s = jnp.swapaxes(_slot_sum(r_ab.reshape(N_DEV, 2 * n_ab, -1), "sum_w_ab").reshape(2, n_ab, -1), 1, 2)
    ab = lambda t: jnp.concatenate([t["w_branch_a"][0], t["w_branch_b"][0]], axis=0)
    o_ab = (g_ab_s.reshape(-1, n_ab),) + tuple(_adamw(g_ab_s.reshape(-1, n_ab), ab(w), ab(mom), ab(var), "adamw_w_ab"))
    half = o_ab[0].shape[0] // 2
    res["w_branch_a"] = tuple(o[:half] for o in o_ab)
    res["w_branch_b"] = tuple(o[half:] for o in o_ab)
    res["w_out"] = tuple(_sum_adamw(r_out, w_out[0], m_w_out[0], v_w_out[0], "adamw_w_out"))
    res["w_down"] = tuple(_sum_adamw(r_down, w_down[0], m_w_down[0], v_w_down[0], "adamw_w_down"))
    tp = lambda t: _tiny_pack(t["conv_w"][0], t["meta_tokens"])
    o_tiny = [_tiny_unpack(o, n_up, n_meta) for o in _sum_adamw(r_tiny, tp(w), tp(mom), tp(var), "adamw_tiny")]
    res["conv_w"] = tuple(o[0] for o in o_tiny)
    res["meta_tokens"] = tuple(o[1] for o in o_tiny)
    zero1 = jnp.zeros((1,), F32)
    sp = lambda t: _pack([t[n].reshape(-1) for n in SMALL] + [zero1], rows_sm)
    o_small = [_unpack(o, small_shapes + [(1,)]) for o in _sum_adamw(r_small, sp(w), sp(mom), sp(var), "adamw_small")]
    for i, n in enumerate(SMALL):
        res[n] = tuple(o[i] for o in o_small)
    loss_all = o_small[0][len(SMALL)].reshape(())

    result = [[res[n][k].reshape(w[n].shape) for n in ORDER] for k in range(4)]
    return (loss_all, grad_x, *result[0], *result[1], *result[2], *result[3])
```

```python
import functools

import jax
import jax.numpy as jnp
import numpy as np
from jax import lax
from jax.experimental import pallas as pl
from jax.experimental.pallas import tpu as pltpu

F32 = jnp.float32
BF16 = jnp.bfloat16

D_MODEL = 1024
N_META = 16
FOX_HEADS = 8
FOX_HEAD_DIM = 64
FOX_WIDTH = FOX_HEADS * FOX_HEAD_DIM
HG_HEADS = 4
HG_DIM = 128
HG_WIDTH = HG_HEADS * HG_DIM
D_FF = 2816
CONV_WIDTH = 3
EPS = 1e-6
IN_COLS = 3 * FOX_WIDTH + FOX_HEADS + 4 * HG_WIDTH + 2 * D_MODEL
N_DEV = 8

ADAM_LR = 0.001
ADAM_B1 = 0.9
ADAM_B2 = 0.999
ADAM_EPS = 1e-08
ADAM_WD = 0.01
ADAM_STEP = 10

LANES = 128
SEQ_BLOCK = 128
SUB = 16
NEG = -1e30
VMEM_LIMIT = 48 * 1024 * 1024

FOX_CB = 2 * D_MODEL // FOX_WIDTH
CB_HQ = (2 * D_MODEL + 3 * FOX_WIDTH) // LANES
CB_HF = CB_HQ + HG_HEADS
CB_HI = CB_HF + HG_HEADS
CB_HG = CB_HI + HG_HEADS
CB_FF = CB_HG + HG_HEADS


def _div_tile(n, target, mult):
    best = None
    for t in range(mult, min(n, target) + 1, mult):
        if n % t == 0:
            best = t
    if best is None:
        best = n
    return best


def _cp(*sem):
    return pltpu.CompilerParams(dimension_semantics=sem, vmem_limit_bytes=VMEM_LIMIT)


def _sigmoid(x):
    return 0.5 * jnp.tanh(0.5 * x) + 0.5


def _dot(a, b, dims, precision=None):
    return lax.dot_general(a, b, (dims, ((), ())), preferred_element_type=F32, precision=precision)


NN = ((1,), (0,))
NT = ((1,), (1,))
TN = ((0,), (0,))
HI = lax.Precision.HIGHEST


MATMUL_VMEM_BUDGET = 30 * 1024 * 1024
MATMUL_MAX_TILE = 2048


def _tile_options(n):
    return [t for t in range(LANES, min(n, MATMUL_MAX_TILE) + 1, LANES) if n % t == 0] or [n]


def _matmul_tiles(m, n, k, a_bytes, b_bytes, o_bytes, has_res):
    tk = _div_tile(k, MATMUL_MAX_TILE, LANES)
    best = None
    for tm in _tile_options(m):
        for tn in _tile_options(n):
            vmem = 2 * (tm * tk * a_bytes + tk * tn * b_bytes) + 2 * tm * tn * o_bytes
            vmem += tm * tn * 4 if (tk < k and o_bytes != 4) else 0
            vmem += 2 * tm * tn * 4 if has_res else 0
            if vmem > MATMUL_VMEM_BUDGET:
                continue
            key = (tm * tn, tn % 256 == 0, tn)
            if best is None or key > best[0]:
                best = (key, tm, tn)
    assert best is not None, (m, n, k)
    return best[1], best[2], tk


def _matmul(a, b, mode, out_dtype, name, residual=None, ride=()):
    if mode == "nn":
        (m, k), (k2, n) = a.shape, b.shape
    elif mode == "nt":
        (m, k), (n, k2) = a.shape, b.shape
    else:
        (k, m), (k2, n) = a.shape, b.shape
    assert k == k2, (a.shape, b.shape, mode)
    has_res = residual is not None
    tm, tn, tk = _matmul_tiles(m, n, k, a.dtype.itemsize, b.dtype.itemsize, jnp.dtype(out_dtype).itemsize, has_res)
    nk = k // tk
    in_place = jnp.dtype(out_dtype) == jnp.dtype(F32)
    if mode == "nn":
        a_spec = pl.BlockSpec((tm, tk), lambda i, j, kk: (i, kk))
        b_spec = pl.BlockSpec((tk, tn), lambda i, j, kk: (kk, j))
        dims = NN
    elif mode == "nt":
        a_spec = pl.BlockSpec((tm, tk), lambda i, j, kk: (i, kk))
        b_spec = pl.BlockSpec((tn, tk), lambda i, j, kk: (j, kk))
        dims = NT
    else:
        a_spec = pl.BlockSpec((tk, tm), lambda i, j, kk: (kk, i))
        b_spec = pl.BlockSpec((tk, tn), lambda i, j, kk: (kk, j))
        dims = TN
    o_spec = pl.BlockSpec((tm, tn), lambda i, j, kk: (i, j))
    grid = (m // tm, n // tn, nk)
    x_arrays, x_shapes, x_sems = _exchange_io(ride, ())
    nx = len(x_arrays)
    n_in = 3 if has_res else 2

    def body(*refs):
        if nx:
            first, last = _grid_ends(grid)
            x_in, x_out = refs[n_in:n_in + nx], refs[n_in + nx + 1:n_in + 2 * nx + 1]
            start, wait = _exchange_copies(x_in, x_out, nx, *refs[n_in + 2 * nx + 1:n_in + 2 * nx + 4])
            pl.when(first)(start)
        compute(*refs)
        if nx:
            pl.when(last)(wait)

    def compute(*refs):
        a_ref, b_ref = refs[0], refs[1]
        r_ref = refs[2] if has_res else None
        o_ref = refs[n_in + nx]
        if nk == 1:
            part = _dot(a_ref[...].astype(BF16), b_ref[...].astype(BF16), dims)
            o_ref[...] = (part + r_ref[...] if has_res else part).astype(o_ref.dtype)
            return
        acc_ref = o_ref if in_place else refs[-1]
        kk = pl.program_id(2)

        @pl.when(kk == 0)
        def _():
            acc_ref[...] = r_ref[...] if (has_res and in_place) else jnp.zeros_like(acc_ref)

        acc_ref[...] += _dot(a_ref[...].astype(BF16), b_ref[...].astype(BF16), dims)

        if not in_place:
            @pl.when(kk == nk - 1)
            def _():
                acc = acc_ref[...]
                if has_res:
                    acc = acc + r_ref[...]
                o_ref[...] = acc.astype(o_ref.dtype)

    in_specs = [a_spec, b_spec] + ([o_spec] if has_res else [])
    args = (a, b) + ((residual,) if has_res else ())
    out_shape = jax.ShapeDtypeStruct((m, n), out_dtype)
    acc = [pltpu.VMEM((tm, tn), F32)] if (nk > 1 and not in_place) else []
    if not nx:
        return pl.pallas_call(
            body, name=name, grid=grid, in_specs=in_specs, out_specs=o_spec, out_shape=out_shape, scratch_shapes=acc,
            compiler_params=_cp("parallel", "parallel", "arbitrary"),
        )(*args)
    return pl.pallas_call(
        body, name=name, grid=grid, in_specs=in_specs + [_HBM] * nx, out_specs=[o_spec] + [_HBM] * nx,
        out_shape=[out_shape] + x_shapes, scratch_shapes=x_sems + acc,
        compiler_params=_cp("arbitrary", "arbitrary", "arbitrary"),
    )(*args, *x_arrays)


def _rms_fwd(x, gain, name):
    m, d = x.shape
    tm = _div_tile(m, 512, 16)

    def body(x_ref, g_ref, o_ref):
        xv = x_ref[...]
        r = lax.rsqrt(jnp.mean(xv * xv, axis=-1, keepdims=True) + EPS)
        o_ref[...] = ((xv * r) * g_ref[...]).astype(o_ref.dtype)

    return pl.pallas_call(
        body, name=name, grid=(m // tm,),
        in_specs=[pl.BlockSpec((tm, d), lambda i: (i, 0)), pl.BlockSpec((1, d), lambda i: (0, 0))],
        out_specs=pl.BlockSpec((tm, d), lambda i: (i, 0)),
        out_shape=jax.ShapeDtypeStruct((m, d), BF16),
        compiler_params=_cp("parallel"),
    )(x, gain)


def _rms_bwd(x, gain, dy, dres, name):
    m, d = x.shape
    tm = _div_tile(m, 256, 8)

    def body(x_ref, g_ref, dy_ref, dr_ref, dx_ref, dg_ref):
        xv = x_ref[...]
        r = lax.rsqrt(jnp.mean(xv * xv, axis=-1, keepdims=True) + EPS)
        nv = xv * r
        dyv = dy_ref[...]
        gdy = dyv * g_ref[...]
        dx_ref[...] = dr_ref[...] + r * (gdy - nv * jnp.mean(gdy * nv, axis=-1, keepdims=True))
        part = jnp.sum(dyv * nv, axis=0, keepdims=True)

        @pl.when(pl.program_id(0) == 0)
        def _():
            dg_ref[...] = part

        @pl.when(pl.program_id(0) > 0)
        def _():
            dg_ref[...] += part

    row = pl.BlockSpec((tm, d), lambda i: (i, 0))
    vec = pl.BlockSpec((1, d), lambda i: (0, 0))
    return pl.pallas_call(
        body, name=name, grid=(m // tm,),
        in_specs=[row, vec, row, row], out_specs=[row, vec],
        out_shape=[jax.ShapeDtypeStruct((m, d), F32), jax.ShapeDtypeStruct((1, d), F32)],
        compiler_params=_cp("arbitrary"),
    )(x, gain, dy, dres)


def _head_stats(xv, lo):
    sq = xv * xv
    s_lo = jnp.sum(jnp.where(lo, sq, 0.0), axis=1, keepdims=True)
    s_hi = jnp.sum(jnp.where(lo, 0.0, sq), axis=1, keepdims=True)
    return jnp.where(lo, s_lo, s_hi) * (1.0 / FOX_HEAD_DIM)


BIAS_LANE = FOX_HEAD_DIM
N_SPLIT = 3


def _split3(c):
    c1 = c.astype(BF16).astype(F32)
    r1 = c - c1
    c2 = r1.astype(BF16).astype(F32)
    c3 = (r1 - c2).astype(BF16).astype(F32)
    return c1, c2, c3


def _fox_prep(proj, qg, kg, bf, nb, lp, name):
    m = proj.shape[0]
    ts = SEQ_BLOCK
    tg = _hg_rows(lp)
    nblk = lp // tg
    scale = FOX_HEAD_DIM ** -0.5

    def body(q_ref, k_ref, v_ref, f_ref, qg_ref, kg_ref, bf_ref, qo_ref, ko_ref, vo_ref, carry_ref):
        @pl.when(pl.program_id(1) == 0)
        def _():
            carry_ref[...] = jnp.zeros_like(carry_ref)

        for hb in range(tg // ts):
            rows = pl.ds(hb * ts, ts)
            block(q_ref.at[rows], k_ref.at[rows], v_ref.at[rows], f_ref.at[rows], qg_ref, kg_ref, bf_ref,
                  qo_ref.at[rows], ko_ref.at[rows], vo_ref.at[rows], carry_ref)

    def block(q_ref, k_ref, v_ref, f_ref, qg_ref, kg_ref, bf_ref, qo_ref, ko_ref, vo_ref, carry_ref):
        lane = lax.broadcasted_iota(jnp.int32, (1, LANES), 1)
        lo = lane < FOX_HEAD_DIM
        z = f_ref[...] + bf_ref[...]
        logf = jnp.minimum(z, 0.0) - jnp.log(1.0 + jnp.exp(-jnp.abs(z)))
        logf = jnp.where(lane < FOX_HEADS, logf, 0.0)
        r = lax.broadcasted_iota(jnp.int32, (ts, ts), 0)
        c = lax.broadcasted_iota(jnp.int32, (ts, ts), 1)
        tri = jnp.where(c <= r, 1.0, 0.0).astype(F32)
        cum = _dot(tri, logf, NN, HI) + carry_ref[...]
        carry_ref[...] = cum[ts - 1:ts, :]

        ones = jnp.where((lane >= BIAS_LANE + N_SPLIT) & (lane < BIAS_LANE + 2 * N_SPLIT), 1.0, 0.0)
        ones_k = jnp.where((lane >= BIAS_LANE) & (lane < BIAS_LANE + N_SPLIT), 1.0, 0.0)
        for j in range(FOX_WIDTH // LANES):
            cs = slice(j * LANES, (j + 1) * LANES)
            xq = q_ref[:, cs]
            yq = ((xq * lax.rsqrt(_head_stats(xq, lo) + EPS)) * qg_ref[:, cs]) * scale
            xk = k_ref[:, cs]
            yk = (xk * lax.rsqrt(_head_stats(xk, lo) + EPS)) * kg_ref[:, cs]
            for hh in range(2):
                h = 2 * j + hh
                pieces = _split3(_lane_pick(cum, lane, h))
                qb, kb = ones, ones_k
                for i, piece in enumerate(pieces):
                    qb = jnp.where(lane == BIAS_LANE + i, piece, qb)
                    kb = jnp.where(lane == BIAS_LANE + N_SPLIT + i, -piece, kb)
                yq_h = yq if hh == 0 else pltpu.roll(yq, FOX_HEAD_DIM, 1)
                yk_h = yk if hh == 0 else pltpu.roll(yk, FOX_HEAD_DIM, 1)
                hs = slice(h * LANES, (h + 1) * LANES)
                qo_ref[:, hs] = jnp.where(lo, yq_h, qb).astype(BF16)
                ko_ref[:, hs] = jnp.where(lo, yk_h, kb).astype(BF16)
        vo_ref[...] = v_ref[...].astype(BF16)

    w = FOX_WIDTH
    row = lambda b, i: (b * nblk + i, 0)
    return pl.pallas_call(
        body, name=name, grid=(nb, nblk),
        in_specs=[pl.BlockSpec((tg, w), lambda b, i: (b * nblk + i, FOX_CB)),
                  pl.BlockSpec((tg, w), lambda b, i: (b * nblk + i, FOX_CB + 1)),
                  pl.BlockSpec((tg, w), lambda b, i: (b * nblk + i, FOX_CB + 2)),
                  pl.BlockSpec((tg, LANES), lambda b, i: (b * nblk + i, CB_FF)),
                  pl.BlockSpec((1, w), lambda b, i: (0, 0)),
                  pl.BlockSpec((1, w), lambda b, i: (0, 0)),
                  pl.BlockSpec((1, LANES), lambda b, i: (0, 0))],
        out_specs=[pl.BlockSpec((tg, 2 * w), row), pl.BlockSpec((tg, 2 * w), row), pl.BlockSpec((tg, w), row)],
        out_shape=[jax.ShapeDtypeStruct((m, 2 * w), BF16)] * 2 + [jax.ShapeDtypeStruct((m, w), BF16)],
        scratch_shapes=[pltpu.VMEM((1, LANES), F32)],
        compiler_params=_cp("arbitrary", "arbitrary"),
    )(proj, proj, proj, proj, qg, kg, bf)


def _att_tile(lp):
    return 384 if (lp % 384 == 0 and lp > 384) else 128


def _lane_pick(blk, lane, idx):
    return jnp.sum(jnp.where(lane == idx, blk, 0.0), axis=1, keepdims=True)


def _head_masks():
    lane = lax.broadcasted_iota(jnp.int32, (1, LANES), 1)
    return lane, [(lane >= hh * FOX_HEAD_DIM) & (lane < (hh + 1) * FOX_HEAD_DIM) for hh in range(2)]


def _fox_fwd(qa, ka, vb, nb, lp, name, ride=()):
    m = qa.shape[0]
    tq = _att_tile(lp)
    nq = lp // tq
    npair = FOX_WIDTH // LANES
    grid = (nb, npair, nq)
    r_arrays, r_shapes, r_sems = _exchange_io((), ride)
    nr = len(r_arrays)

    def body(q_ref, k_ref, v_ref, *rest):
        r_in, (o_ref, lse_ref), r_out, sems = rest[:nr], rest[nr:nr + 2], rest[nr + 2:2 * nr + 2], rest[2 * nr + 2:]
        if nr:
            first, last = _grid_ends(grid)
            start, wait = _exchange_copies(r_in, r_out, 0, *sems)
            pl.when(first)(start)
        qi = pl.program_id(2)
        lane, hmasks = _head_masks()
        zero16 = jnp.zeros((), BF16)
        causal = lax.broadcasted_iota(jnp.int32, (tq, 1), 0) >= lax.broadcasted_iota(jnp.int32, (1, tq), 1)
        qs = [q_ref[:, hh * LANES:(hh + 1) * LANES] for hh in range(2)]

        def tile(j, carry, diagonal):
            k0 = pl.multiple_of(j * tq, tq)
            vb = v_ref[pl.ds(k0, tq), :]
            out = []
            for hh in range(2):
                mx, l, acc = carry[3 * hh:3 * hh + 3]
                vz = jnp.where(hmasks[hh], vb, zero16)
                s = _dot(qs[hh], k_ref[pl.ds(k0, tq), hh * LANES:(hh + 1) * LANES], NT)
                if diagonal:
                    s = jnp.where(causal, s, NEG)
                m_new = jnp.maximum(mx, jnp.max(s, axis=1, keepdims=True))
                alpha = jnp.exp(mx - m_new)
                pe = jnp.exp(s - m_new)
                l = alpha * l + jnp.sum(pe, axis=1, keepdims=True)
                acc = alpha * acc + _dot(pe.astype(BF16), vz, NN)
                out += [m_new, l, acc]
            return tuple(out)

        init = (jnp.full((tq, 1), NEG, F32), jnp.zeros((tq, 1), F32), jnp.zeros((tq, LANES), F32)) * 2
        carry = lax.fori_loop(0, qi, lambda j, c: tile(j, c, False), init)
        m0, l0, acc0, m1, l1, acc1 = tile(qi, carry, True)
        o_ref[...] = acc0 / l0 + acc1 / l1
        lse_ref[...] = jnp.where(lane == 0, m0 + jnp.log(l0), jnp.where(lane == 1, m1 + jnp.log(l1), 0.0))
        if nr:
            pl.when(last)(wait)

    return pl.pallas_call(
        body, name=name, grid=grid,
        in_specs=[pl.BlockSpec((tq, 2 * LANES), lambda b, p, i: (b * nq + i, p)),
                  pl.BlockSpec((lp, 2 * LANES), lambda b, p, i: (b, p)),
                  pl.BlockSpec((lp, LANES), lambda b, p, i: (b, p))] + [_HBM] * nr,
        out_specs=[pl.BlockSpec((tq, LANES), lambda b, p, i: (b * nq + i, p)),
                   pl.BlockSpec((None, None, tq, LANES), lambda b, p, i: (b, p, i, 0))] + [_HBM] * nr,
        out_shape=[jax.ShapeDtypeStruct((m, FOX_WIDTH), F32),
                   jax.ShapeDtypeStruct((nb, npair, lp, LANES), F32)] + r_shapes,
        scratch_shapes=r_sems if nr else [],
        compiler_params=_cp(*(["arbitrary"] * 3 if nr else ["parallel", "parallel", "arbitrary"])),
    )(qa, ka, vb, *r_arrays)


def _fox_bwd(qa, ka, vb, do, o, lse, nb, lp, name, ride=()):
    m = qa.shape[0]
    tq = _att_tile(lp)
    nq = lp // tq
    npair = FOX_WIDTH // LANES
    grid = (nb, npair, nq)
    r_arrays, r_shapes, r_sems = _exchange_io(ride, ())
    nr = len(r_arrays)

    def body(k_ref, v_ref, q_ref, do_ref, o_ref, lse_ref, *rest):
        r_in, r_out, sems = rest[:nr], rest[nr + 5:2 * nr + 5], rest[2 * nr + 5:]
        dq_ref, dk_ref, dv_ref, dc0_ref, dc1_ref = rest[nr:nr + 5]
        if nr:
            first, last = _grid_ends(grid)
            start, wait = _exchange_copies(r_in, r_out, nr, *sems)
            pl.when(first)(start)
        j = pl.program_id(2)
        lane, hmasks = _head_masks()
        zero16 = jnp.zeros((), BF16)
        causal = lax.broadcasted_iota(jnp.int32, (tq, 1), 0) >= lax.broadcasted_iota(jnp.int32, (1, tq), 1)

        @pl.when(j == 0)
        def _():
            dq_ref[...] = jnp.zeros_like(dq_ref)

        vv = v_ref[...]
        vzs = [jnp.where(hm, vv, zero16) for hm in hmasks]

        def tile(qi, carry, diagonal):
            dk0, dk1, dv, dc0, dc1 = carry
            q0 = pl.multiple_of(qi * tq, tq)
            dob16 = do_ref[pl.ds(q0, tq), :].astype(BF16)
            ob = o_ref[pl.ds(q0, tq), :]
            lseb = lse_ref[pl.ds(q0, tq), :]
            dks, dcs = [dk0, dk1], [dc0, dc1]
            for hh in range(2):
                hs = slice(hh * LANES, (hh + 1) * LANES)
                q = q_ref[pl.ds(q0, tq), hs]
                doz16 = jnp.where(hmasks[hh], dob16, zero16)
                delta = jnp.sum(doz16.astype(F32) * ob, axis=1, keepdims=True)
                s = _dot(q, k_ref[:, hs], NT) - _lane_pick(lseb, lane, hh)
                if diagonal:
                    s = jnp.where(causal, s, NEG)
                pm = jnp.exp(s)
                ds = pm * (_dot(doz16, vzs[hh], NT) - delta)
                ds16 = ds.astype(BF16)
                dv = dv + _dot(pm.astype(BF16), doz16, TN)
                dks[hh] = dks[hh] + _dot(ds16, q, TN)
                dq_ref[pl.ds(q0, tq), hs] += _dot(ds16, k_ref[:, hs], NN)
                dcs[hh] = dcs[hh] - jnp.sum(ds, axis=0, keepdims=True)
            return dks[0], dks[1], dv, dcs[0], dcs[1]

        zt = jnp.zeros((tq, LANES), F32)
        zr = jnp.zeros((1, tq), F32)
        carry = tile(j, (zt, zt, zt, zr, zr), True)
        dk0, dk1, dv, dc0, dc1 = lax.fori_loop(j + 1, nq, lambda qi, c: tile(qi, c, False), carry)
        dk_ref[:, :LANES] = dk0
        dk_ref[:, LANES:] = dk1
        dv_ref[...] = dv
        dc0_ref[...] = dc0
        dc1_ref[...] = dc1
        if nr:
            pl.when(last)(wait)

    full2 = pl.BlockSpec((lp, 2 * LANES), lambda b, p, j: (b, p))
    full = pl.BlockSpec((lp, LANES), lambda b, p, j: (b, p))
    blk2 = pl.BlockSpec((tq, 2 * LANES), lambda b, p, j: (b * nq + j, p))
    blk = pl.BlockSpec((tq, LANES), lambda b, p, j: (b * nq + j, p))
    dcs = pl.BlockSpec((None, None, 1, tq), lambda b, p, j: (b, p, 0, j))
    return pl.pallas_call(
        body, name=name, grid=grid,
        in_specs=[blk2, blk, full2, full, full,
                  pl.BlockSpec((None, None, lp, LANES), lambda b, p, j: (b, p, 0, 0))] + [_HBM] * nr,
        out_specs=[full2, blk2, blk, dcs, dcs] + [_HBM] * nr,
        out_shape=[jax.ShapeDtypeStruct((m, 2 * FOX_WIDTH), F32)] * 2 + [jax.ShapeDtypeStruct((m, FOX_WIDTH), F32)]
        + [jax.ShapeDtypeStruct((nb, npair, 1, lp), F32)] * 2 + r_shapes,
        scratch_shapes=r_sems if nr else [],
        compiler_params=_cp(*(["arbitrary"] * 3 if nr else ["parallel", "parallel", "arbitrary"])),
    )(ka, vb, qa, do, o, lse, *r_arrays)


def _fox_prep_bwd(proj, dqa, dka, dv, dcum, qg, kg, bf, nb, lp, name):
    m = proj.shape[0]
    ts = SEQ_BLOCK
    tg = _hg_rows(lp)
    nblk = lp // tg
    scale = FOX_HEAD_DIM ** -0.5
    w = FOX_WIDTH
    wo = 3 * w

    def body(q_ref, k_ref, f_ref, dq_ref, dk_ref, dv_ref, dc_ref, qg_ref, kg_ref, bf_ref,
             out_ref, dff_ref, dqg_ref, dkg_ref, dbf_ref, carry_ref):
        @pl.when((pl.program_id(0) == 0) & (pl.program_id(1) == 0))
        def _():
            dqg_ref[...] = jnp.zeros_like(dqg_ref)
            dkg_ref[...] = jnp.zeros_like(dkg_ref)
            dbf_ref[...] = jnp.zeros_like(dbf_ref)

        @pl.when(pl.program_id(1) == 0)
        def _():
            carry_ref[...] = jnp.zeros_like(carry_ref)

        for hb in reversed(range(tg // ts)):
            rows = pl.ds(hb * ts, ts)
            block(q_ref.at[rows], k_ref.at[rows], f_ref.at[rows], dq_ref.at[rows], dk_ref.at[rows], dv_ref.at[rows],
                  dc_ref.at[rows], qg_ref, kg_ref, bf_ref, out_ref.at[rows], dff_ref.at[rows], dqg_ref, dkg_ref, dbf_ref,
                  carry_ref)

    def block(q_ref, k_ref, f_ref, dq_ref, dk_ref, dv_ref, dc_ref, qg_ref, kg_ref, bf_ref,
              out_ref, dff_ref, dqg_ref, dkg_ref, dbf_ref, carry_ref):
        lane = lax.broadcasted_iota(jnp.int32, (1, LANES), 1)
        lo = lane < FOX_HEAD_DIM

        def norm_bwd(x, g, dy):
            r = lax.rsqrt(_head_stats(x, lo) + EPS)
            nv = x * r
            gdy = dy * g
            prod = gdy * nv
            s_lo = jnp.sum(jnp.where(lo, prod, 0.0), axis=1, keepdims=True)
            s_hi = jnp.sum(jnp.where(lo, 0.0, prod), axis=1, keepdims=True)
            mean = jnp.where(lo, s_lo, s_hi) * (1.0 / FOX_HEAD_DIM)
            return r * (gdy - nv * mean), jnp.sum(dy * nv, axis=0, keepdims=True)

        def pair(d_ref, jj):
            even = d_ref[:, 2 * jj * LANES:(2 * jj + 1) * LANES]
            odd = d_ref[:, (2 * jj + 1) * LANES:(2 * jj + 2) * LANES]
            return jnp.where(lo, even, pltpu.roll(odd, FOX_HEAD_DIM, 1))

        for jj in range(w // LANES):
            cs = slice(jj * LANES, (jj + 1) * LANES)
            dx, dg = norm_bwd(q_ref[:, cs], qg_ref[:, cs], pair(dq_ref, jj) * scale)
            out_ref[:, cs] = dx.astype(BF16)
            dqg_ref[:, cs] += dg
            dx, dg = norm_bwd(k_ref[:, cs], kg_ref[:, cs], pair(dk_ref, jj))
            out_ref[:, w + jj * LANES:w + (jj + 1) * LANES] = dx.astype(BF16)
            dkg_ref[:, cs] += dg
        out_ref[:, 2 * w:3 * w] = dv_ref[...].astype(BF16)

        dc = dc_ref[...]
        r = lax.broadcasted_iota(jnp.int32, (ts, ts), 0)
        c = lax.broadcasted_iota(jnp.int32, (ts, ts), 1)
        triu = jnp.where(c >= r, 1.0, 0.0).astype(F32)
        dlogf = _dot(triu, dc, NN, HI) + carry_ref[...]
        carry_ref[...] += jnp.sum(dc, axis=0, keepdims=True)
        z = f_ref[...] + bf_ref[...]
        dz = jnp.where(lane < FOX_HEADS, dlogf * _sigmoid(-z), 0.0)
        dff_ref[...] = dz.astype(BF16)
        dbf_ref[...] += jnp.sum(dz, axis=0, keepdims=True)

    rev = lambda b, i: (b * nblk + (nblk - 1 - i), 0)
    vec = lambda n: pl.BlockSpec((1, n), lambda b, i: (0, 0))
    return pl.pallas_call(
        body, name=name, grid=(nb, nblk),
        in_specs=[pl.BlockSpec((tg, w), lambda b, i: (b * nblk + (nblk - 1 - i), FOX_CB)),
                  pl.BlockSpec((tg, w), lambda b, i: (b * nblk + (nblk - 1 - i), FOX_CB + 1)),
                  pl.BlockSpec((tg, LANES), lambda b, i: (b * nblk + (nblk - 1 - i), CB_FF)),
                  pl.BlockSpec((tg, 2 * w), rev), pl.BlockSpec((tg, 2 * w), rev), pl.BlockSpec((tg, w), rev),
                  pl.BlockSpec((tg, LANES), rev), vec(w), vec(w), vec(LANES)],
        out_specs=[pl.BlockSpec((tg, wo), rev), pl.BlockSpec((tg, LANES), rev), vec(w), vec(w), vec(LANES)],
        out_shape=[jax.ShapeDtypeStruct((m, wo), BF16), jax.ShapeDtypeStruct((m, LANES), BF16),
                   jax.ShapeDtypeStruct((1, w), F32),
                   jax.ShapeDtypeStruct((1, w), F32), jax.ShapeDtypeStruct((1, LANES), F32)],
        scratch_shapes=[pltpu.VMEM((1, LANES), F32)],
        compiler_params=_cp("arbitrary", "arbitrary"),
    )(proj, proj, proj, dqa, dka, dv, dcum, qg, kg, bf)


HG_BLOCKS_PER_STEP = 11


def _hg_rows(lp, most=3):
    nblk = lp // SEQ_BLOCK
    return SEQ_BLOCK * max(n for n in range(1, most + 1) if nblk % n == 0)


def _chunk_masks():
    r = lax.broadcasted_iota(jnp.int32, (SEQ_BLOCK, SEQ_BLOCK), 0)
    c = lax.broadcasted_iota(jnp.int32, (SEQ_BLOCK, SEQ_BLOCK), 1)
    same = (r // SUB) == (c // SUB)
    return r, c, same


def _hg_gates(hf, lb):
    sg = _sigmoid(hf)
    f = lb + (1.0 - lb) * sg
    return sg, f, jnp.log(f), (1.0 - lb) * _sigmoid(-hf)


def _hg_intra_e(g_ref, base, t, srow):
    diff = g_ref[pl.ds(base + t, 1), :] - g_ref[pl.ds(base, SUB), :]
    return jnp.exp(jnp.where(srow <= t, diff, NEG))


def _hgrn_fwd(proj, lb, gain, nb, lp, name):
    m = proj.shape[0]
    tb = _hg_rows(lp, HG_BLOCKS_PER_STEP)
    nblk = lp // tb
    nsb = tb // SEQ_BLOCK
    ns = SEQ_BLOCK // SUB

    def body(q_ref, f_ref, i_ref, g_ref, lb_ref, gain_ref, oraw_ref, y_ref, ssave_ref,
             st_ref, g_scr, kin_scr, o_scr):
        @pl.when(pl.program_id(2) == 0)
        def _():
            st_ref[...] = jnp.zeros_like(st_ref)

        for hb in range(nsb):
            rows = pl.ds(hb * SEQ_BLOCK, SEQ_BLOCK)
            block(q_ref.at[rows], f_ref.at[rows], i_ref.at[rows], g_ref.at[rows], lb_ref, gain_ref, oraw_ref.at[rows],
                  y_ref.at[rows], ssave_ref.at[hb], st_ref, g_scr.at[rows], kin_scr.at[rows], o_scr.at[rows])

    def block(q_ref, f_ref, i_ref, g_ref, lb_ref, gain_ref, oraw_ref, y_ref, ssave_ref,
              st_ref, g_scr, kin_scr, o_scr):
        ssave_ref[...] = st_ref[...]
        lbv = lb_ref[...]
        _, _, lf, kin = _hg_gates(f_ref[...], lbv)
        r, c, same = _chunk_masks()
        ltri = jnp.where(same & (c <= r), 1.0, 0.0).astype(F32)
        lall = jnp.where(same, 1.0, 0.0).astype(F32)
        g = _dot(ltri, lf, NN, HI)
        gt = _dot(lall, lf, NN, HI)
        g_scr[...] = g
        kin_scr[...] = kin
        qv = q_ref[...]
        qg = (qv * jnp.exp(g)).astype(BF16)
        kg = (kin * jnp.exp(gt - g)).astype(BF16)
        et = jnp.exp(gt)
        srow = lax.broadcasted_iota(jnp.int32, (SUB, 1), 0)
        subs = [slice(cc * SUB, (cc + 1) * SUB) for cc in range(ns)]
        ups = [_dot(i_ref[sl, :].astype(BF16), kg[sl], TN) for sl in subs]
        st = st_ref[...]
        starts = []
        for cc in range(ns):
            starts.append(st)
            st = et[cc * SUB:cc * SUB + 1, :] * st + ups[cc]
        st_ref[...] = st
        for cc, sl in enumerate(subs):
            base = cc * SUB
            kc = kin_scr[sl, :]
            vc = i_ref[sl, :]
            for t in range(SUB):
                e = _hg_intra_e(g_scr, base, t, srow)
                a = jnp.sum((q_ref[pl.ds(base + t, 1), :] * kc) * e, axis=1, keepdims=True)
                o_scr[pl.ds(base + t, 1), :] = jnp.sum(a * vc, axis=0, keepdims=True)
            o_scr[sl, :] += _dot(qg[sl], starts[cc].astype(BF16), NT)
        o = o_scr[...]
        oraw_ref[...] = o
        rr = lax.rsqrt(jnp.mean(o * o, axis=-1, keepdims=True) + EPS)
        hg = g_ref[...]
        y_ref[...] = (((o * rr) * gain_ref[...]) * (hg * _sigmoid(hg))).astype(y_ref.dtype)

    col = lambda cb: pl.BlockSpec((tb, LANES), lambda b, h, i, cb=cb: (b * nblk + i, cb + h))
    out_blk = pl.BlockSpec((tb, LANES), lambda b, h, i: (b * nblk + i, h))
    return pl.pallas_call(
        body, name=name, grid=(nb, HG_HEADS, nblk),
        in_specs=[col(CB_HQ), col(CB_HF), col(CB_HI), col(CB_HG),
                  pl.BlockSpec((1, LANES), lambda b, h, i: (0, h)),
                  pl.BlockSpec((1, LANES), lambda b, h, i: (0, 0))],
        out_specs=[out_blk, out_blk,
                   pl.BlockSpec((None, None, nsb, HG_DIM, HG_DIM), lambda b, h, i: (b, h, i, 0, 0))],
        out_shape=[jax.ShapeDtypeStruct((m, HG_WIDTH), F32), jax.ShapeDtypeStruct((m, HG_WIDTH), BF16),
                   jax.ShapeDtypeStruct((nb, HG_HEADS, nblk * nsb, HG_DIM, HG_DIM), F32)],
        scratch_shapes=[pltpu.VMEM((HG_DIM, HG_DIM), F32), pltpu.VMEM((tb, LANES), F32),
                        pltpu.VMEM((tb, LANES), F32), pltpu.VMEM((tb, LANES), F32)],
        compiler_params=_cp("parallel", "parallel", "arbitrary"),
    )(proj, proj, proj, proj, lb, gain)


def _hgrn_bwd(proj, oraw, ssave, dy, lb, gain, nb, lp, name):
    m = proj.shape[0]
    tb = _hg_rows(lp, HG_BLOCKS_PER_STEP)
    nblk = lp // tb
    nsb = tb // SEQ_BLOCK
    ns = SEQ_BLOCK // SUB

    def body(q_ref, f_ref, i_ref, g_ref, oraw_ref, ssave_ref, dy_ref, lb_ref, gain_ref,
             dq_ref, df_ref, di_ref, dg_ref, dgain_ref, dlb_ref,
             dst_ref, *scratch):
        hd = pl.program_id(0)
        bb = pl.program_id(1)
        ii = pl.program_id(2)

        @pl.when((hd == 0) & (bb == 0) & (ii == 0))
        def _():
            dgain_ref[...] = jnp.zeros_like(dgain_ref)

        @pl.when((bb == 0) & (ii == 0))
        def _():
            dlb_ref[...] = jnp.zeros_like(dlb_ref)

        @pl.when(ii == 0)
        def _():
            dst_ref[...] = jnp.zeros_like(dst_ref)

        for hb in reversed(range(nsb)):
            rows = pl.ds(hb * SEQ_BLOCK, SEQ_BLOCK)
            block(q_ref.at[rows], f_ref.at[rows], i_ref.at[rows], g_ref.at[rows], oraw_ref.at[rows], ssave_ref.at[hb],
                  dy_ref.at[rows], lb_ref, gain_ref, dq_ref.at[rows], df_ref.at[rows], di_ref.at[rows], dg_ref.at[rows],
                  dgain_ref, dlb_ref, dst_ref, *[sc.at[rows] for sc in scratch])

    def block(q_ref, f_ref, i_ref, g_ref, oraw_ref, ssave_ref, dy_ref, lb_ref, gain_ref,
              dq_ref, df_ref, di_ref, dg_ref, dgain_ref, dlb_ref,
              dst_ref, g_scr, kin_scr, do_scr, dq_scr, dk_scr, dv_scr, dgg_scr):
        gainv = gain_ref[...]
        lbv = lb_ref[...]

        o = oraw_ref[...]
        rr = lax.rsqrt(jnp.mean(o * o, axis=-1, keepdims=True) + EPS)
        nv = o * rr
        hg = g_ref[...]
        sgg = _sigmoid(hg)
        sil = hg * sgg
        dyv = dy_ref[...]
        dg_ref[...] = (dyv * nv * gainv * (sgg * (1.0 + hg * (1.0 - sgg)))).astype(dg_ref.dtype)
        dgain_ref[...] += jnp.sum(dyv * nv * sil, axis=0, keepdims=True)
        dn = dyv * gainv * sil
        do_scr[...] = rr * (dn - nv * jnp.mean(dn * nv, axis=-1, keepdims=True))

        hf = f_ref[...]
        sg, f, lf, kin = _hg_gates(hf, lbv)
        r, c, same = _chunk_masks()
        ltri = jnp.where(same & (c <= r), 1.0, 0.0).astype(F32)
        lall = jnp.where(same, 1.0, 0.0).astype(F32)
        g = _dot(ltri, lf, NN, HI)
        gt = _dot(lall, lf, NN, HI)
        g_scr[...] = g
        kin_scr[...] = kin
        qv = q_ref[...]
        eg = jnp.exp(g)
        ekg = jnp.exp(gt - g)
        qg = qv * eg
        kg = kin * ekg
        qg16 = qg.astype(BF16)
        kg16 = kg.astype(BF16)
        et = jnp.exp(gt)
        subs = [slice(cc * SUB, (cc + 1) * SUB) for cc in range(ns)]
        ups = [_dot(i_ref[sl, :].astype(BF16), kg16[sl], TN) for sl in subs]
        st = ssave_ref[...]
        starts = []
        for cc in range(ns):
            starts.append(st)
            st = et[cc * SUB:cc * SUB + 1, :] * st + ups[cc]
        do16 = do_scr[...].astype(BF16)
        downs = [_dot(do16[sl], qg16[sl], TN) for sl in subs]
        dst = dst_ref[...]
        afters = [None] * ns
        for cc in reversed(range(ns)):
            afters[cc] = dst
            dst = et[cc * SUB:cc * SUB + 1, :] * dst + downs[cc]
        dst_ref[...] = dst

        srow = lax.broadcasted_iota(jnp.int32, (SUB, 1), 0)
        for cc, sl in enumerate(subs):
            base = cc * SUB
            st = starts[cc]
            st16 = st.astype(BF16)
            dst = afters[cc]
            dst16 = dst.astype(BF16)
            doc16 = do16[sl]
            vc = i_ref[sl, :]
            vc16 = vc.astype(BF16)
            kc = kin_scr[sl, :]
            etc = et[base:base + 1, :]
            dqg = _dot(doc16, st16, NN)
            dv_c = _dot(kg16[sl], dst16, NT)
            dkg = _dot(vc16, dst16, NN)
            dgt = jnp.sum(dst * st, axis=0, keepdims=True) * etc
            dq_c = dqg * eg[sl]
            dk_c = dkg * ekg[sl]
            dg_c = dqg * qg[sl] - dkg * kg[sl]
            dgt = dgt + jnp.sum(dkg * kg[sl], axis=0, keepdims=True)
            for t in range(SUB):
                e = _hg_intra_e(g_scr, base, t, srow)
                qt = q_ref[pl.ds(base + t, 1), :]
                dot_t = do_scr[pl.ds(base + t, 1), :]
                a = jnp.sum((qt * kc) * e, axis=1, keepdims=True)
                da = jnp.sum(dot_t * vc, axis=1, keepdims=True)
                dv_c = dv_c + a * dot_t
                w = da * e
                dq_scr[pl.ds(base + t, 1), :] = jnp.sum(w * kc, axis=0, keepdims=True)
                wq = w * qt
                dk_c = dk_c + wq
                dg_c = dg_c - kc * wq
            dq_i = dq_scr[sl, :]
            dg_c = dg_c + qv[sl] * dq_i + jnp.where(srow == SUB - 1, dgt, 0.0)
            dq_scr[sl, :] = dq_c + dq_i
            dk_scr[sl, :] = dk_c
            dv_scr[sl, :] = dv_c
            dgg_scr[sl, :] = dg_c

        utri = jnp.where(same & (c >= r), 1.0, 0.0).astype(F32)
        dlf = _dot(utri, dgg_scr[...], NN, HI)
        dkin = dk_scr[...]
        dsg = sg * (1.0 - sg)
        df_ref[...] = ((dlf / f - dkin) * ((1.0 - lbv) * dsg)).astype(df_ref.dtype)
        dlb_ref[...] += jnp.sum((dlf / f - dkin) * (1.0 - sg), axis=0, keepdims=True)
        dq_ref[...] = dq_scr[...].astype(dq_ref.dtype)
        di_ref[...] = dv_scr[...].astype(di_ref.dtype)

    rowi = lambda b, i: b * nblk + (nblk - 1 - i)
    col = lambda cb: pl.BlockSpec((tb, LANES), lambda h, b, i, cb=cb: (rowi(b, i), cb + h))
    hblk = pl.BlockSpec((tb, LANES), lambda h, b, i: (rowi(b, i), h))
    return pl.pallas_call(
        body, name=name, grid=(HG_HEADS, nb, nblk),
        in_specs=[col(CB_HQ), col(CB_HF), col(CB_HI), col(CB_HG), hblk,
                  pl.BlockSpec((None, None, nsb, HG_DIM, HG_DIM), lambda h, b, i: (b, h, nblk - 1 - i, 0, 0)),
                  hblk,
                  pl.BlockSpec((1, LANES), lambda h, b, i: (0, h)),
                  pl.BlockSpec((1, LANES), lambda h, b, i: (0, 0))],
        out_specs=[hblk, hblk, hblk, hblk,
                   pl.BlockSpec((1, LANES), lambda h, b, i: (0, 0)),
                   pl.BlockSpec((1, LANES), lambda h, b, i: (0, h))],
        out_shape=[jax.ShapeDtypeStruct((m, HG_WIDTH), BF16)] * 4
        + [jax.ShapeDtypeStruct((1, LANES), F32), jax.ShapeDtypeStruct((1, HG_WIDTH), F32)],
        scratch_shapes=[pltpu.VMEM((HG_DIM, HG_DIM), F32)] + [pltpu.VMEM((tb, LANES), F32)] * 7,
        compiler_params=_cp("arbitrary", "arbitrary", "arbitrary"),
    )(proj, proj, proj, proj, oraw, ssave, dy, lb, gain)


def _gate_fwd(proj, ya, yb, name):
    m = proj.shape[0]
    tm = _div_tile(m, 256, 16)

    def body(ga_ref, gb_ref, ya_ref, yb_ref, o_ref):
        ya, yb = ya_ref[...].astype(F32), yb_ref[...].astype(F32)
        o_ref[...] = (_sigmoid(ga_ref[...]) * ya + _sigmoid(gb_ref[...]) * yb).astype(o_ref.dtype)

    row = pl.BlockSpec((tm, D_MODEL), lambda i: (i, 0))
    return pl.pallas_call(
        body, name=name, grid=(m // tm,),
        in_specs=[row, pl.BlockSpec((tm, D_MODEL), lambda i: (i, 1)), row, row],
        out_specs=row, out_shape=jax.ShapeDtypeStruct((m, D_MODEL), BF16),
        compiler_params=_cp("parallel"),
    )(proj, proj, ya, yb)


def _gate_bwd(proj, ya, yb, dm, name):
    m = proj.shape[0]
    tm = _div_tile(m, 256, 16)

    def body(ga_ref, gb_ref, ya_ref, yb_ref, dm_ref, dya_ref, dyb_ref, dg_ref):
        dmv = dm_ref[...].astype(F32)
        sa = _sigmoid(ga_ref[...])
        sb = _sigmoid(gb_ref[...])
        dya_ref[...] = (dmv * sa).astype(BF16)
        dyb_ref[...] = (dmv * sb).astype(BF16)
        dg_ref[:, :D_MODEL] = (dmv * ya_ref[...].astype(F32) * (sa * (1.0 - sa))).astype(BF16)
        dg_ref[:, D_MODEL:] = (dmv * yb_ref[...].astype(F32) * (sb * (1.0 - sb))).astype(BF16)

    row = pl.BlockSpec((tm, D_MODEL), lambda i: (i, 0))
    wide = pl.BlockSpec((tm, 2 * D_MODEL), lambda i: (i, 0))
    return pl.pallas_call(
        body, name=name, grid=(m // tm,),
        in_specs=[row, pl.BlockSpec((tm, D_MODEL), lambda i: (i, 1)), row, row, row],
        out_specs=[row, row, wide],
        out_shape=[jax.ShapeDtypeStruct((m, D_MODEL), BF16)] * 2 + [jax.ShapeDtypeStruct((m, 2 * D_MODEL), BF16)],
        compiler_params=_cp("parallel"),
    )(proj, proj, ya, yb, dm)


CONV_ROWS = 128


def _conv3(x, xprev, w_ref, b_ref, rowi):
    r = x.shape[0]
    x1 = jnp.where(rowi < 1, pltpu.roll(xprev, 1, 0), pltpu.roll(x, 1, 0))
    x2 = jnp.where(rowi < 2, pltpu.roll(xprev, 2, 0), pltpu.roll(x, 2, 0))
    u = w_ref[0:1, :] * x2 + w_ref[1:2, :] * x1 + w_ref[2:3, :] * x + b_ref[...]
    return u, x1, x2


def _conv_fwd(up, cw, cb, nb, lp, name):
    m = up.shape[0]
    nct = D_FF // LANES
    r = CONV_ROWS
    nch = lp // r

    def body(u_ref, w_ref, b_ref, o_ref):
        rowi = lax.broadcasted_iota(jnp.int32, (r, 1), 0)

        def step(i, xp):
            r0 = pl.multiple_of(i * r, r)
            xc = u_ref[pl.ds(r0, r), :].astype(F32)
            u, _, _ = _conv3(xc, xp, w_ref, b_ref, rowi)
            ug, uv = u[:, :LANES], u[:, LANES:]
            o_ref[pl.ds(r0, r), :] = ((ug * _sigmoid(ug)) * uv).astype(o_ref.dtype)
            return xc

        lax.fori_loop(0, nch, step, jnp.zeros((r, 2 * LANES), F32))

    return pl.pallas_call(
        body, name=name, grid=(nb, nct),
        in_specs=[pl.BlockSpec((lp, 2 * LANES), lambda b, c: (b, c)),
                  pl.BlockSpec((CONV_WIDTH, 2 * LANES), lambda b, c: (0, c)),
                  pl.BlockSpec((1, 2 * LANES), lambda b, c: (0, c))],
        out_specs=pl.BlockSpec((lp, LANES), lambda b, c: (b, c)),
        out_shape=jax.ShapeDtypeStruct((m, D_FF), BF16),
        compiler_params=_cp("parallel", "parallel"),
    )(up, cw, cb)


def _conv_bwd(up, dact, cw, cb, nb, lp, name):
    m = up.shape[0]
    nct = D_FF // LANES
    r = CONV_ROWS
    nch = lp // r

    def body(u_ref, da_ref, w_ref, b_ref, dup_ref, dw_ref, db_ref):
        rowi = lax.broadcasted_iota(jnp.int32, (r, 1), 0)
        wv = w_ref[...]

        def step(k, carry):
            dun, dw0, dw1, dw2, dbs = carry
            i = nch - 1 - k
            r0 = pl.multiple_of(i * r, r)
            rp = pl.multiple_of(jnp.maximum(i - 1, 0) * r, r)
            xc = u_ref[pl.ds(r0, r), :].astype(F32)
            xp = u_ref[pl.ds(rp, r), :].astype(F32) * (i > 0).astype(F32)
            u, x1, x2 = _conv3(xc, xp, w_ref, b_ref, rowi)
            ug, uv = u[:, :LANES], u[:, LANES:]
            da = da_ref[pl.ds(r0, r), :].astype(F32)
            sg = _sigmoid(ug)
            du = jnp.concatenate([da * uv * (sg * (1.0 + ug * (1.0 - sg))), da * (ug * sg)], axis=1)
            d1 = jnp.where(rowi >= r - 1, pltpu.roll(dun, r - 1, 0), pltpu.roll(du, r - 1, 0))
            d2 = jnp.where(rowi >= r - 2, pltpu.roll(dun, r - 2, 0), pltpu.roll(du, r - 2, 0))
            dup_ref[pl.ds(r0, r), :] = (wv[2:3, :] * du + wv[1:2, :] * d1 + wv[0:1, :] * d2).astype(dup_ref.dtype)
            dw0 = dw0 + jnp.sum(du * x2, axis=0, keepdims=True)
            dw1 = dw1 + jnp.sum(du * x1, axis=0, keepdims=True)
            dw2 = dw2 + jnp.sum(du * xc, axis=0, keepdims=True)
            dbs = dbs + jnp.sum(du, axis=0, keepdims=True)
            return du, dw0, dw1, dw2, dbs

        z1 = jnp.zeros((1, 2 * LANES), F32)
        _, dw0, dw1, dw2, dbs = lax.fori_loop(0, nch, step, (jnp.zeros((r, 2 * LANES), F32), z1, z1, z1, z1))

        @pl.when(pl.program_id(1) == 0)
        def _():
            dw_ref[...] = jnp.zeros_like(dw_ref)
            db_ref[...] = jnp.zeros_like(db_ref)

        dw_ref[0:1, :] += dw0
        dw_ref[1:2, :] += dw1
        dw_ref[2:3, :] += dw2
        db_ref[...] += dbs

    return pl.pallas_call(
        body, name=name, grid=(nct, nb),
        in_specs=[pl.BlockSpec((lp, 2 * LANES), lambda c, b: (b, c)),
                  pl.BlockSpec((lp, LANES), lambda c, b: (b, c)),
                  pl.BlockSpec((CONV_WIDTH, 2 * LANES), lambda c, b: (0, c)),
                  pl.BlockSpec((1, 2 * LANES), lambda c, b: (0, c))],
        out_specs=[pl.BlockSpec((lp, 2 * LANES), lambda c, b: (b, c)),
                   pl.BlockSpec((CONV_WIDTH, 2 * LANES), lambda c, b: (0, c)),
                   pl.BlockSpec((1, 2 * LANES), lambda c, b: (0, c))],
        out_shape=[jax.ShapeDtypeStruct((m, 2 * D_FF), BF16),
                   jax.ShapeDtypeStruct((CONV_WIDTH, 2 * D_FF), F32),
                   jax.ShapeDtypeStruct((1, 2 * D_FF), F32)],
        compiler_params=_cp("parallel", "arbitrary"),
    )(up, dact, cw, cb)


def _ffn_interleave(a, axis):
    shp = a.shape
    a = a.reshape(shp[:axis] + (2, D_FF // LANES, LANES) + shp[axis + 1:])
    return jnp.swapaxes(a, axis, axis + 1).reshape(shp)


def _ffn_deinterleave(a, axis):
    shp = a.shape
    a = a.reshape(shp[:axis] + (D_FF // LANES, 2, LANES) + shp[axis + 1:])
    return jnp.swapaxes(a, axis, axis + 1).reshape(shp)


def _shifted_rows(prev_ref, cur_ref):
    keep = SEQ_BLOCK - N_META
    return jnp.concatenate([prev_ref[keep:, :], cur_ref[:keep, :]], axis=0)


def _frame_specs(nb, nfb, d):
    prev = pl.BlockSpec((nb, SEQ_BLOCK, d), lambda i: (0, jnp.clip(i - 1, 0, nfb - 1), 0))
    cur = pl.BlockSpec((nb, SEQ_BLOCK, d), lambda i: (0, jnp.clip(i, 0, nfb - 1), 0))
    return prev, cur


def _embed_rms(x, meta, gain, lp, name):
    nb, seq, d = x.shape
    l = seq + N_META
    tr = SEQ_BLOCK
    nblk = lp // tr

    def body(prev_ref, cur_ref, meta_ref, g_ref, h_ref, o_ref):
        t = pl.program_id(0) * tr + lax.broadcasted_iota(jnp.int32, (tr, 1), 0)
        head = jnp.concatenate([meta_ref[...], jnp.zeros((tr - N_META, d), F32)], axis=0)
        for b in range(nb):
            rows = jnp.where(t < l, _shifted_rows(prev_ref.at[b], cur_ref.at[b]), 0.0)
            xv = jnp.where(t < N_META, head, rows)
            h_ref[b] = xv
            r = lax.rsqrt(jnp.mean(xv * xv, axis=-1, keepdims=True) + EPS)
            o_ref[b] = ((xv * r) * g_ref[...]).astype(o_ref.dtype)

    prev, cur = _frame_specs(nb, seq // tr, d)
    row = pl.BlockSpec((nb, tr, d), lambda i: (0, i, 0))
    h0, xn = pl.pallas_call(
        body, name=name, grid=(nblk,),
        in_specs=[prev, cur, pl.BlockSpec((N_META, d), lambda i: (0, 0)), pl.BlockSpec((1, d), lambda i: (0, 0))],
        out_specs=[row, row],
        out_shape=[jax.ShapeDtypeStruct((nb, lp, d), F32), jax.ShapeDtypeStruct((nb, lp, d), BF16)],
        compiler_params=_cp("parallel"),
    )(x, x, meta, gain)
    return h0.reshape(nb * lp, d), xn.reshape(nb * lp, d)


def _loss_head(out, target, lp, name):
    nb, seq, d = target.shape
    l = seq + N_META
    tr = SEQ_BLOCK
    nblk = lp // tr

    def body(o_ref, prev_ref, cur_ref, dy_ref, ls_ref):
        t = pl.program_id(0) * tr + lax.broadcasted_iota(jnp.int32, (tr, 1), 0)
        valid = (t >= N_META) & (t < l)
        part = jnp.zeros((1, d), F32)
        for b in range(nb):
            err = jnp.where(valid, o_ref[b] - _shifted_rows(prev_ref.at[b], cur_ref.at[b]), 0.0)
            dy_ref[b] = err * (1.0 / d)
            part = part + jnp.sum(err * err, axis=0, keepdims=True)

        @pl.when(pl.program_id(0) == 0)
        def _():
            ls_ref[...] = part

        @pl.when(pl.program_id(0) > 0)
        def _():
            ls_ref[...] += part

    prev, cur = _frame_specs(nb, seq // tr, d)
    row = pl.BlockSpec((nb, tr, d), lambda i: (0, i, 0))
    dy, lsum = pl.pallas_call(
        body, name=name, grid=(nblk,),
        in_specs=[row, prev, cur], out_specs=[row, pl.BlockSpec((1, d), lambda i: (0, 0))],
        out_shape=[jax.ShapeDtypeStruct((nb, lp, d), F32), jax.ShapeDtypeStruct((1, d), F32)],
        compiler_params=_cp("arbitrary"),
    )(out.reshape(nb, lp, d), target, target)
    return dy.reshape(nb * lp, d), lsum


def _adam_math(g, w, mom, var):
    c1 = 1.0 - ADAM_B1 ** ADAM_STEP
    c2 = 1.0 - ADAM_B2 ** ADAM_STEP
    mn = ADAM_B1 * mom + (1.0 - ADAM_B1) * g
    vn = ADAM_B2 * var + (1.0 - ADAM_B2) * (g * g)
    delta = -ADAM_LR * ((mn / c1) / (jnp.sqrt(vn / c2) + ADAM_EPS) + ADAM_WD * w)
    return delta, mn, vn


def _slot_sum(recv, name):
    _, r, c = recv.shape
    tc = _div_tile(c, 256, LANES)

    def body(r_ref, g_ref):
        g = r_ref[0].astype(F32)
        for s in range(1, N_DEV):
            g = g + r_ref[s].astype(F32)
        g_ref[...] = g

    return pl.pallas_call(
        body, name=name, grid=(c // tc,),
        in_specs=[pl.BlockSpec((N_DEV, r, tc), lambda j: (0, 0, j))],
        out_specs=pl.BlockSpec((r, tc), lambda j: (0, j)),
        out_shape=jax.ShapeDtypeStruct((r, c), F32),
        compiler_params=_cp("parallel"),
    )(recv)


def _adamw(g, w, mom, var, name):
    r, c = w.shape
    tr = _div_tile(r, 256, 8)

    def body(g_ref, w_ref, m_ref, v_ref, d_ref, mo_ref, vo_ref):
        d_ref[...], mo_ref[...], vo_ref[...] = _adam_math(g_ref[...], w_ref[...], m_ref[...], v_ref[...])

    row = pl.BlockSpec((tr, c), lambda i: (i, 0))
    return pl.pallas_call(
        body, name=name, grid=(r // tr,), in_specs=[row] * 4, out_specs=[row] * 3,
        out_shape=[jax.ShapeDtypeStruct((r, c), F32)] * 3,
        compiler_params=_cp("parallel"),
    )(g, w, mom, var)


def _sum_adamw(recv, w, mom, var, name):
    r, c = w.shape
    tr = _div_tile(r, 256, 8)

    def body(r_ref, w_ref, m_ref, v_ref, g_ref, d_ref, mo_ref, vo_ref):
        g = r_ref[0].astype(F32)
        for s in range(1, N_DEV):
            g = g + r_ref[s].astype(F32)
        g_ref[...] = g
        d_ref[...], mo_ref[...], vo_ref[...] = _adam_math(g, w_ref[...], m_ref[...], v_ref[...])

    row = pl.BlockSpec((tr, c), lambda i: (i, 0))
    return pl.pallas_call(
        body, name=name, grid=(r // tr,),
        in_specs=[pl.BlockSpec((N_DEV, tr, c), lambda i: (0, i, 0)), row, row, row],
        out_specs=[row] * 4,
        out_shape=[jax.ShapeDtypeStruct((r, c), F32)] * 4,
        compiler_params=_cp("parallel"),
    )(recv, w, mom, var)


_MESH = pl.DeviceIdType.MESH
_HBM = pl.BlockSpec(memory_space=pltpu.HBM)
N_PEER = N_DEV - 1


def _position():
    return lax.axis_index("x"), lax.axis_index("y"), lax.axis_index("c")


def _all_gather(shards, name):
    n = len(shards)

    def body(*refs):
        x_refs, out_refs = refs[:n], refs[n:2 * n]
        send_sems, recv_sems, local_sems = refs[2 * n:]
        x, y, c = _position()
        me, sibling = (x, y, c), (x, y, 1 - c)
        chips = [(1 - x, y), (x, 1 - y), (1 - x, 1 - y)]

        def copy(a, k, block, to, src=None):
            slot = out_refs[a].at[4 * block[0] + 2 * block[1] + block[2]]
            return pltpu.make_async_remote_copy(
                src_ref=slot if src is None else src, dst_ref=slot,
                send_sem=send_sems.at[a * N_PEER + k], recv_sem=recv_sems.at[a * N_PEER + k],
                device_id=to, device_id_type=_MESH)

        mine, sent = [], []
        for a in range(n):
            cp = pltpu.make_async_copy(x_refs[a], out_refs[a].at[4 * x + 2 * y + c], local_sems.at[a])
            cp.start()
            mine.append(cp)
            first = [copy(a, 0, me, sibling, src=x_refs[a])]
            first += [copy(a, 1 + j, me, (*chip, c), src=x_refs[a]) for j, chip in enumerate(chips)]
            for cp in first:
                cp.start()
            sent += first
        for a in range(n):
            for j, chip in enumerate(chips):
                copy(a, 1 + j, (*chip, c), me).wait_recv()
                fwd = copy(a, 4 + j, (*chip, c), sibling)
                fwd.start()
                sent.append(fwd)
        for a in range(n):
            copy(a, 0, sibling, me).wait_recv()
            for j, chip in enumerate(chips):
                copy(a, 4 + j, (*chip, 1 - c), me).wait_recv()
        for cp in sent:
            cp.wait_send()
        for cp in mine:
            cp.wait()

    return pl.pallas_call(
        body, name=name,
        out_shape=[jax.ShapeDtypeStruct((N_DEV,) + a.shape, a.dtype) for a in shards],
        in_specs=[_HBM] * n, out_specs=[_HBM] * n,
        scratch_shapes=[pltpu.SemaphoreType.DMA((n * N_PEER,)), pltpu.SemaphoreType.DMA((n * N_PEER,)),
                        pltpu.SemaphoreType.DMA((n,))],
    )(*shards)


_FLIPS = [(fx, fy, fc) for fx in (0, 1) for fy in (0, 1) for fc in (0, 1)][1:]


def _exchange_copies(in_refs, out_refs, nblk, send_sems, recv_sems, local_sems):
    n = len(in_refs)
    x, y, c = _position()
    me = 4 * x + 2 * y + c

    def peer(f):
        return (1 - x if f[0] else x, 1 - y if f[1] else y, 1 - c if f[2] else c)

    def idx(p):
        return 4 * p[0] + 2 * p[1] + p[2]

    def local(a):
        return pltpu.make_async_copy(in_refs[a].at[me] if a < nblk else in_refs[a], out_refs[a].at[me], local_sems.at[a])

    def remote(a, k, sending):
        p = peer(_FLIPS[k])
        src = in_refs[a].at[idx(p)] if a < nblk else in_refs[a]
        dst = out_refs[a].at[me] if sending else out_refs[a].at[idx(p)]
        return pltpu.make_async_remote_copy(
            src_ref=src, dst_ref=dst, send_sem=send_sems.at[a * N_PEER + k], recv_sem=recv_sems.at[a * N_PEER + k],
            device_id=p, device_id_type=_MESH)

    def start():
        for a in range(n):
            local(a).start()
            for k in range(N_PEER):
                remote(a, k, True).start()

    def wait():
        for a in range(n):
            for k in range(N_PEER):
                remote(a, k, False).wait_recv()
        for a in range(n):
            for k in range(N_PEER):
                remote(a, k, True).wait_send()
            local(a).wait()

    return start, wait


def _exchange_io(blocks, shared):
    arrays = list(blocks) + list(shared)
    n = len(arrays)
    out_shape = [jax.ShapeDtypeStruct(a.shape, a.dtype) for a in blocks]
    out_shape += [jax.ShapeDtypeStruct((N_DEV,) + a.shape, a.dtype) for a in shared]
    sems = [pltpu.SemaphoreType.DMA((n * N_PEER,)), pltpu.SemaphoreType.DMA((n * N_PEER,)), pltpu.SemaphoreType.DMA((n,))]
    return arrays, out_shape, sems


def _exchange(blocks, shared, name):
    arrays, out_shape, sems = _exchange_io(blocks, shared)
    n = len(arrays)

    def body(*refs):
        start, wait = _exchange_copies(refs[:n], refs[n:2 * n], len(blocks), *refs[2 * n:])
        start()
        wait()

    return pl.pallas_call(
        body, name=name, out_shape=out_shape, in_specs=[_HBM] * n, out_specs=[_HBM] * n, scratch_shapes=sems,
    )(*arrays)


def _grid_ends(grid):
    ids = [pl.program_id(i) for i in range(len(grid))]
    first = functools.reduce(jnp.logical_and, [i == 0 for i in ids])
    last = functools.reduce(jnp.logical_and, [i == g - 1 for i, g in zip(ids, grid)])
    return first, last


def _pack(parts, rows):
    flat = jnp.concatenate(parts, axis=-1)
    return jnp.pad(flat, [(0, rows * LANES - flat.shape[-1])]).reshape(rows, LANES)


def _unpack(packed, shapes):
    flat = packed.reshape(-1)
    out, off = [], 0
    for shp in shapes:
        n = int(np.prod(shp))
        out.append(flat[off:off + n].reshape(shp))
        off += n
    return out


def _rows_for(shapes, extra=0):
    n = sum(int(np.prod(s)) for s in shapes) + extra
    return -(-n // (8 * LANES)) * 8


def _lower_bound(logits):
    return jnp.cumsum(jax.nn.softmax(logits.astype(F32), axis=0), axis=0)[0:1]


def _align_axis0(w):
    a, b = 3 * FOX_WIDTH, 3 * FOX_WIDTH + FOX_HEADS
    c = b + 4 * HG_WIDTH
    pad = [(0, LANES - FOX_HEADS)] + [(0, 0)] * (w.ndim - 1)
    return jnp.concatenate([w[c:], w[:a], w[b:c], jnp.pad(w[a:b], pad)], axis=0)


def _unalign_axis0(g):
    a, b = 2 * D_MODEL, 2 * D_MODEL + 3 * FOX_WIDTH
    c = b + 4 * HG_WIDTH
    return jnp.concatenate([g[a:b], g[c:c + FOX_HEADS], g[b:c], g[:a]], axis=0)


TINY_COLS = 768


def _tiny_pack(conv_w_shard, meta_shard):
    cw = jnp.pad(conv_w_shard, ((0, 8 - CONV_WIDTH), (0, TINY_COLS - conv_w_shard.shape[1])))
    mt = jnp.pad(meta_shard, ((0, 0), (0, TINY_COLS - meta_shard.shape[1])))
    return jnp.concatenate([cw, mt], axis=0)


def _tiny_unpack(t, ncw, nmeta):
    return t[..., :CONV_WIDTH, :ncw], t[..., 8:8 + N_META, :nmeta]


def _late_weights(g_up, g_down, g_ab, g_out):
    d = g_up.shape[-1]
    w_a_t = g_ab[:, 0].reshape(-1, g_ab.shape[-1])
    w_b_t = g_ab[:, 1].reshape(-1, g_ab.shape[-1])
    return _ffn_interleave(g_up.reshape(-1, d), 0), g_down.reshape(-1, d), w_a_t, w_b_t, g_out.reshape(-1, d)


def _early_blocks(g_w_up_t, g_w_down, g_w_out, g_w_a_t, g_w_b_t):
    d = g_w_out.shape[-1]
    ab = jnp.stack([g_w_a_t.reshape(N_DEV, -1, g_w_a_t.shape[-1]), g_w_b_t.reshape(N_DEV, -1, g_w_b_t.shape[-1])], axis=1)
    return [_ffn_deinterleave(g_w_up_t, 0).reshape(N_DEV, -1, d).astype(BF16), g_w_down.reshape(N_DEV, -1, d).astype(BF16),
            g_w_out.reshape(N_DEV, -1, d).astype(BF16), ab.astype(BF16)]


def _local_step(x, target, meta, norm1_gain, w_in_t, fox_b_f, q_gain, k_gain, lb, hg_out_gain, w_a_t, w_b_t, w_out,
                norm2_gain, w_up_t, conv_w, conv_b, w_down, ffn_shards=None):
    nb, seq, d = x.shape
    assert seq % SEQ_BLOCK == 0 and N_META < SEQ_BLOCK
    l = seq + N_META
    lp = -(-l // SEQ_BLOCK) * SEQ_BLOCK
    m = nb * lp
    qg = jnp.tile(q_gain, (1, FOX_HEADS))
    kg = jnp.tile(k_gain, (1, FOX_HEADS))
    bf = jnp.pad(fox_b_f, ((0, 0), (0, LANES - FOX_HEADS)))

    h0, xn = _embed_rms(x, meta, norm1_gain, lp, "embed_rms1")
    proj = _matmul(xn, w_in_t, "nt", F32, "proj_in")
    qa, ka, vb = _fox_prep(proj, qg, kg, bf, nb, lp, "fox_prep")
    if ffn_shards is None:
        o_fox, lse = _fox_fwd(qa, ka, vb, nb, lp, "fox_fwd")
    else:
        o_fox, lse, *late = _fox_fwd(qa, ka, vb, nb, lp, "fox_fwd", ride=ffn_shards)
        w_up_t, w_down, w_a_t, w_b_t, w_out = _late_weights(*late)
    o_raw, o_hg, s_save = _hgrn_fwd(proj, lb, hg_out_gain, nb, lp, "hgrn_fwd")
    ya = _matmul(o_hg, w_a_t, "nt", BF16, "branch_a")
    yb = _matmul(o_fox, w_b_t, "nt", BF16, "branch_b")
    merged = _gate_fwd(proj, ya, yb, "gate_fwd")
    h1 = _matmul(merged, w_out, "nn", F32, "mix_out", residual=h0)
    hn = _rms_fwd(h1, norm2_gain, "rms2_fwd")
    up = _matmul(hn, w_up_t, "nt", BF16, "ffn_up")
    act = _conv_fwd(up, conv_w, conv_b, nb, lp, "conv_fwd")
    out = _matmul(act, w_down, "nn", F32, "ffn_down", residual=h1)
    dy, lsum = _loss_head(out, target, lp, "loss_head")
    loss = (0.5 / d) * jnp.sum(lsum)

    dact = _matmul(dy, w_down, "nt", BF16, "d_act")
    g_w_down = _matmul(act, dy, "tn", F32, "g_w_down")
    dup, g_conv_w, g_conv_b = _conv_bwd(up, dact, conv_w, conv_b, nb, lp, "conv_bwd")
    dhn = _matmul(dup, w_up_t, "nn", F32, "d_hn")
    g_w_up_t = _matmul(dup, hn, "tn", F32, "g_w_up")
    dh1, g_norm2 = _rms_bwd(h1, norm2_gain, dhn, dy, "rms2_bwd")

    dmerged = _matmul(dh1, w_out, "nt", BF16, "d_merged")
    g_w_out = _matmul(merged, dh1, "tn", F32, "g_w_out")
    dya, dyb, dgab = _gate_bwd(proj, ya, yb, dmerged, "gate_bwd")
    do_hg = _matmul(dya, w_a_t, "nn", F32, "d_o_hg")
    g_w_a_t = _matmul(dya, o_hg, "tn", F32, "g_w_a")
    do_fox = _matmul(dyb, w_b_t, "nn", BF16, "d_o_fox")
    g_w_b_t = _matmul(dyb, o_fox, "tn", F32, "g_w_b")
    dhq, dhf, dhi, dhg, g_hg_gain, g_lb = _hgrn_bwd(proj, o_raw, s_save, do_hg, lb, hg_out_gain, nb, lp, "hgrn_bwd")
    if ffn_shards is None:
        dqs, dkn, dvv, dc0, dc1 = _fox_bwd(qa, ka, vb, do_fox, o_fox, lse, nb, lp, "fox_bwd")
        early = None
    else:
        dqs, dkn, dvv, dc0, dc1, *early = _fox_bwd(qa, ka, vb, do_fox, o_fox, lse, nb, lp, "fox_bwd",
                                                   ride=_early_blocks(g_w_up_t, g_w_down, g_w_out, g_w_a_t, g_w_b_t))
    dcum = jnp.stack([dc0, dc1], axis=2).reshape(nb, FOX_HEADS, lp)
    dcum = jnp.pad(jnp.transpose(dcum, (0, 2, 1)), ((0, 0), (0, 0), (0, LANES - FOX_HEADS))).reshape(m, LANES)
    dfqkv, dff, g_qg, g_kg, g_bf = _fox_prep_bwd(proj, dqs, dkn, dvv, dcum, qg, kg, bf, nb, lp, "fox_prep_bwd")
    dproj = jnp.concatenate([dgab, dfqkv, dhq, dhf, dhi, dhg, dff], axis=1)
    g_w_in_t = _matmul(dproj, xn, "tn", F32, "g_w_in")
    if ffn_shards is None:
        dxn = _matmul(dproj, w_in_t, "nn", F32, "d_xn")
    else:
        blocks_in = _unalign_axis0(g_w_in_t).reshape(N_DEV, -1, d).astype(BF16)
        dxn, r_in = _matmul(dproj, w_in_t, "nn", F32, "d_xn", ride=[blocks_in])
        early = early + [r_in]
    dh0, g_norm1 = _rms_bwd(h0, norm1_gain, dxn, dh1, "rms1_bwd")

    dh0 = dh0.reshape(nb, lp, d)
    grad_x = dh0[:, N_META:l]
    g_meta = jnp.sum(dh0[:, :N_META], axis=0)
    g_q_gain = jnp.sum(g_qg.reshape(FOX_HEADS, FOX_HEAD_DIM), axis=0, keepdims=True)
    g_k_gain = jnp.sum(g_kg.reshape(FOX_HEADS, FOX_HEAD_DIM), axis=0, keepdims=True)
    grads = dict(meta_tokens=g_meta, norm1_gain=g_norm1, w_in_t=g_w_in_t, fox_b_f=g_bf[:, :FOX_HEADS],
                 q_norm_gain=g_q_gain, k_norm_gain=g_k_gain, lb=g_lb, hg_out_gain=g_hg_gain,
                 w_a_t=g_w_a_t, w_b_t=g_w_b_t, w_out=g_w_out, norm2_gain=g_norm2, w_up_t=g_w_up_t,
                 conv_w=g_conv_w, conv_b=g_conv_b, w_down=g_w_down, early=early)
    return loss, grad_x, grads


SMALL = ("norm1_gain", "fox_b_f", "q_norm_gain", "k_norm_gain", "hg_lb_logits", "hg_out_gain", "norm2_gain", "conv_b")
ORDER = ("meta_tokens", "norm1_gain", "w_in", "fox_b_f", "q_norm_gain", "k_norm_gain", "hg_lb_logits", "hg_out_gain",
         "w_branch_a", "w_branch_b", "w_out", "norm2_gain", "w_up", "conv_w", "conv_b", "w_down")


def kernel(x, meta_tokens, norm1_gain, w_in, fox_b_f, q_norm_gain, k_norm_gain, hg_lb_logits, hg_out_gain, w_branch_a, w_branch_b, w_out, norm2_gain, w_up, conv_w, conv_b, w_down, loss_target, m_meta_tokens, m_norm1_gain, m_w_in, m_fox_b_f, m_q_norm_gain, m_k_norm_gain, m_hg_lb_logits, m_hg_out_gain, m_w_branch_a, m_w_branch_b, m_w_out, m_norm2_gain, m_w_up, m_conv_w, m_conv_b, m_w_down, v_meta_tokens, v_norm1_gain, v_w_in, v_fox_b_f, v_q_norm_gain, v_k_norm_gain, v_hg_lb_logits, v_hg_out_gain, v_w_branch_a, v_w_branch_b, v_w_out, v_norm2_gain, v_w_up, v_conv_w, v_conv_b, v_w_down):
    w = dict(meta_tokens=meta_tokens, norm1_gain=norm1_gain, w_in=w_in, fox_b_f=fox_b_f, q_norm_gain=q_norm_gain,
             k_norm_gain=k_norm_gain, hg_lb_logits=hg_lb_logits, hg_out_gain=hg_out_gain, w_branch_a=w_branch_a,
             w_branch_b=w_branch_b, w_out=w_out, norm2_gain=norm2_gain, w_up=w_up, conv_w=conv_w, conv_b=conv_b,
             w_down=w_down)
    mom = dict(meta_tokens=m_meta_tokens, norm1_gain=m_norm1_gain, w_in=m_w_in, fox_b_f=m_fox_b_f,
               q_norm_gain=m_q_norm_gain, k_norm_gain=m_k_norm_gain, hg_lb_logits=m_hg_lb_logits,
               hg_out_gain=m_hg_out_gain, w_branch_a=m_w_branch_a, w_branch_b=m_w_branch_b, w_out=m_w_out,
               norm2_gain=m_norm2_gain, w_up=m_w_up, conv_w=m_conv_w, conv_b=m_conv_b, w_down=m_w_down)
    var = dict(meta_tokens=v_meta_tokens, norm1_gain=v_norm1_gain, w_in=v_w_in, fox_b_f=v_fox_b_f,
               q_norm_gain=v_q_norm_gain, k_norm_gain=v_k_norm_gain, hg_lb_logits=v_hg_lb_logits,
               hg_out_gain=v_hg_out_gain, w_branch_a=v_w_branch_a, w_branch_b=v_w_branch_b, w_out=v_w_out,
               norm2_gain=v_norm2_gain, w_up=v_w_up, conv_w=v_conv_w, conv_b=v_conv_b, w_down=v_w_down)
    d = D_MODEL
    n_in, n_up = w_in.shape[2], w_up.shape[2]
    n_ab, n_meta = w_branch_a.shape[2], meta_tokens.shape[1]

    g_in, g_tiny = _all_gather([w_in[0].T.astype(BF16), _tiny_pack(conv_w[0], meta_tokens)], "gather_weights")
    w_in_t = _align_axis0(g_in.reshape(N_DEV * n_in, d))
    cw_slots, meta_slots = _tiny_unpack(g_tiny, n_up, n_meta)
    conv_w_f = _ffn_interleave(jnp.transpose(cw_slots, (1, 0, 2)).reshape(CONV_WIDTH, -1), 1)
    meta_f = jnp.transpose(meta_slots, (1, 0, 2)).reshape(N_META, -1)
    conv_b_i = _ffn_interleave(conv_b, 1)

    lb, lb_vjp = jax.vjp(_lower_bound, hg_lb_logits)
    loss, grad_x, g = _local_step(
        x, loss_target, meta_f, norm1_gain, w_in_t, fox_b_f, q_norm_gain, k_norm_gain, lb, hg_out_gain,
        None, None, None, norm2_gain, None, conv_w_f, conv_b_i, None,
        ffn_shards=(w_up[0].T.astype(BF16), w_down[0].astype(BF16),
                    jnp.stack([w_branch_a[0].T, w_branch_b[0].T]).astype(BF16), w_out[0].astype(BF16)))

    g["hg_lb_logits"] = lb_vjp(g.pop("lb"))[0]
    g["conv_b"] = _ffn_deinterleave(g["conv_b"], 1)
    gcw = _ffn_deinterleave(g["conv_w"], 1).reshape(CONV_WIDTH, N_DEV, n_up)
    gmeta = g["meta_tokens"].reshape(N_META, N_DEV, n_meta)
    tiny = jnp.concatenate([
        jnp.pad(jnp.transpose(gcw, (1, 0, 2)), ((0, 0), (0, 8 - CONV_WIDTH), (0, TINY_COLS - n_up))),
        jnp.pad(jnp.transpose(gmeta, (1, 0, 2)), ((0, 0), (0, 0), (0, TINY_COLS - n_meta)))], axis=1)
    small_shapes = [w[n].shape for n in SMALL]
    rows_sm = _rows_for(small_shapes, extra=1)
    small = _pack([g[n].reshape(-1) for n in SMALL] + [loss.reshape(1)], rows_sm)
    r_tiny, r_small = _exchange([tiny], [small], "exchange_grads")
    r_up, r_down, r_out, r_ab, r_in = g["early"]

    res = {}
    g_in_s = _slot_sum(r_in, "sum_w_in").T
    res["w_in"] = (g_in_s,) + tuple(_adamw(g_in_s, w_in[0], m_w_in[0], v_w_in[0], "adamw_w_in"))
    g_up_s = _slot_sum(r_up, "sum_w_up").T
    res["w_up"] = (g_up_s,) + tuple(_adamw(g_up_s, w_up[0], m_w_up[0], v_w_up[0], "adamw_w_up"))
    g_ab_s = jnp.swapaxes(_slot_sum(r_ab.reshape(N_DEV, 2 * n_ab, -1), "sum_w_ab").reshape(2, n_ab, -1), 1, 2)
    ab = lambda t: jnp.concatenate([t["w_branch_a"][0], t["w_branch_b"][0]], axis=0)
    o_ab = (g_ab_s.reshape(-1, n_ab),) + tuple(_adamw(g_ab_s.reshape(-1, n_ab), ab(w), ab(mom), ab(var), "adamw_w_ab"))
    half = o_ab[0].shape[0] // 2
    res["w_branch_a"] = tuple(o[:half] for o in o_ab)
    res["w_branch_b"] = tuple(o[half:] for o in o_ab)
    res["w_out"] = tuple(_sum_adamw(r_out, w_out[0], m_w_out[0], v_w_out[0], "adamw_w_out"))
    res["w_down"] = tuple(_sum_adamw(r_down, w_down[0], m_w_down[0], v_w_down[0], "adamw_w_down"))
    tp = lambda t: _tiny_pack(t["conv_w"][0], t["meta_tokens"])
    o_tiny = [_tiny_unpack(o, n_up, n_meta) for o in _sum_adamw(r_tiny, tp(w), tp(mom), tp(var), "adamw_tiny")]
    res["conv_w"] = tuple(o[0] for o in o_tiny)
    res["meta_tokens"] = tuple(o[1] for o in o_tiny)
    zero1 = jnp.zeros((1,), F32)
    sp = lambda t: _pack([t[n].reshape(-1) for n in SMALL] + [zero1], rows_sm)
    o_small = [_unpack(o, small_shapes + [(1,)]) for o in _sum_adamw(r_small, sp(w), sp(mom), sp(var), "adamw_small")]
    for i, n in enumerate(SMALL):
        res[n] = tuple(o[i] for o in o_small)
    loss_all = o_small[0][len(SMALL)].reshape(())

    result = [[res[n][k].reshape(w[n].shape) for n in ORDER] for k in range(4)]
    return (loss_all, grad_x, *result[0], *result[1], *result[2], *result[3])
```

```python
import functools

import jax
import jax.numpy as jnp
import numpy as np
from jax import lax
from jax.experimental import pallas as pl
from jax.experimental.pallas import tpu as pltpu

F32 = jnp.float32
BF16 = jnp.bfloat16

D_MODEL = 1024
N_META = 16
FOX_HEADS = 8
FOX_HEAD_DIM = 64
FOX_WIDTH = FOX_HEADS * FOX_HEAD_DIM
HG_HEADS = 4
HG_DIM = 128
HG_WIDTH = HG_HEADS * HG_DIM
D_FF = 2816
CONV_WIDTH = 3
EPS = 1e-6
IN_COLS = 3 * FOX_WIDTH + FOX_HEADS + 4 * HG_WIDTH + 2 * D_MODEL
N_DEV = 8

ADAM_LR = 0.001
ADAM_B1 = 0.9
ADAM_B2 = 0.999
ADAM_EPS = 1e-08
ADAM_WD = 0.01
ADAM_STEP = 10

LANES = 128
SEQ_BLOCK = 128
SUB = 16
NEG = -1e30
VMEM_LIMIT = 48 * 1024 * 1024

FOX_CB = 2 * D_MODEL // FOX_WIDTH
CB_HQ = (2 * D_MODEL + 3 * FOX_WIDTH) // LANES
CB_HF = CB_HQ + HG_HEADS
CB_HI = CB_HF + HG_HEADS
CB_HG = CB_HI + HG_HEADS
CB_FF = CB_HG + HG_HEADS


def _div_tile(n, target, mult):
    best = None
    for t in range(mult, min(n, target) + 1, mult):
        if n % t == 0:
            best = t
    if best is None:
        best = n
    return best


def _cp(*sem):
    return pltpu.CompilerParams(dimension_semantics=sem, vmem_limit_bytes=VMEM_LIMIT)


def _sigmoid(x):
    return 0.5 * jnp.tanh(0.5 * x) + 0.5


def _dot(a, b, dims, precision=None):
    return lax.dot_general(a, b, (dims, ((), ())), preferred_element_type=F32, precision=precision)


NN = ((1,), (0,))
NT = ((1,), (1,))
TN = ((0,), (0,))
HI = lax.Precision.HIGHEST


MATMUL_VMEM_BUDGET = 30 * 1024 * 1024
MATMUL_MAX_TILE = 2048


def _tile_options(n):
    return [t for t in range(LANES, min(n, MATMUL_MAX_TILE) + 1, LANES) if n % t == 0] or [n]


def _matmul_tiles(m, n, k, a_bytes, b_bytes, o_bytes, has_res):
    tk = _div_tile(k, MATMUL_MAX_TILE, LANES)
    best = None
    for tm in _tile_options(m):
        for tn in _tile_options(n):
            vmem = 2 * (tm * tk * a_bytes + tk * tn * b_bytes) + 2 * tm * tn * o_bytes
            vmem += tm * tn * 4 if (tk < k and o_bytes != 4) else 0
            vmem += 2 * tm * tn * 4 if has_res else 0
            if vmem > MATMUL_VMEM_BUDGET:
                continue
            key = (tm * tn, tn % 256 == 0, tn)
            if best is None or key > best[0]:
                best = (key, tm, tn)
    assert best is not None, (m, n, k)
    return best[1], best[2], tk


def _matmul(a, b, mode, out_dtype, name, residual=None, ride=()):
    if mode == "nn":
        (m, k), (k2, n) = a.shape, b.shape
    elif mode == "nt":
        (m, k), (n, k2) = a.shape, b.shape
    else:
        (k, m), (k2, n) = a.shape, b.shape
    assert k == k2, (a.shape, b.shape, mode)
    has_res = residual is not None
    tm, tn, tk = _matmul_tiles(m, n, k, a.dtype.itemsize, b.dtype.itemsize, jnp.dtype(out_dtype).itemsize, has_res)
    nk = k // tk
    in_place = jnp.dtype(out_dtype) == jnp.dtype(F32)
    if mode == "nn":
        a_spec = pl.BlockSpec((tm, tk), lambda i, j, kk: (i, kk))
        b_spec = pl.BlockSpec((tk, tn), lambda i, j, kk: (kk, j))
        dims = NN
    elif mode == "nt":
        a_spec = pl.BlockSpec((tm, tk), lambda i, j, kk: (i, kk))
        b_spec = pl.BlockSpec((tn, tk), lambda i, j, kk: (j, kk))
        dims = NT
    else:
        a_spec = pl.BlockSpec((tk, tm), lambda i, j, kk: (kk, i))
        b_spec = pl.BlockSpec((tk, tn), lambda i, j, kk: (kk, j))
        dims = TN
    o_spec = pl.BlockSpec((tm, tn), lambda i, j, kk: (i, j))
    grid = (m // tm, n // tn, nk)
    x_arrays, x_shapes, x_sems = _exchange_io(ride, ())
    nx = len(x_arrays)
    n_in = 3 if has_res else 2

    def body(*refs):
        if nx:
            first, last = _grid_ends(grid)
            x_in, x_out = refs[n_in:n_in + nx], refs[n_in + nx + 1:n_in + 2 * nx + 1]
            start, wait = _exchange_copies(x_in, x_out, nx, *refs[n_in + 2 * nx + 1:n_in + 2 * nx + 4])
            pl.when(first)(start)
        compute(*refs)
        if nx:
            pl.when(last)(wait)

    def compute(*refs):
        a_ref, b_ref = refs[0], refs[1]
        r_ref = refs[2] if has_res else None
        o_ref = refs[n_in + nx]
        if nk == 1:
            part = _dot(a_ref[...].astype(BF16), b_ref[...].astype(BF16), dims)
            o_ref[...] = (part + r_ref[...] if has_res else part).astype(o_ref.dtype)
            return
        acc_ref = o_ref if in_place else refs[-1]
        kk = pl.program_id(2)

        @pl.when(kk == 0)
        def _():
            acc_ref[...] = r_ref[...] if (has_res and in_place) else jnp.zeros_like(acc_ref)

        acc_ref[...] += _dot(a_ref[...].astype(BF16), b_ref[...].astype(BF16), dims)

        if not in_place:
            @pl.when(kk == nk - 1)
            def _():
                acc = acc_ref[...]
                if has_res:
                    acc = acc + r_ref[...]
                o_ref[...] = acc.astype(o_ref.dtype)

    in_specs = [a_spec, b_spec] + ([o_spec] if has_res else [])
    args = (a, b) + ((residual,) if has_res else ())
    out_shape = jax.ShapeDtypeStruct((m, n), out_dtype)
    acc = [pltpu.VMEM((tm, tn), F32)] if (nk > 1 and not in_place) else []
    if not nx:
        return pl.pallas_call(
            body, name=name, grid=grid, in_specs=in_specs, out_specs=o_spec, out_shape=out_shape, scratch_shapes=acc,
            compiler_params=_cp("parallel", "parallel", "arbitrary"),
        )(*args)
    return pl.pallas_call(
        body, name=name, grid=grid, in_specs=in_specs + [_HBM] * nx, out_specs=[o_spec] + [_HBM] * nx,
        out_shape=[out_shape] + x_shapes, scratch_shapes=x_sems + acc,
        compiler_params=_cp("arbitrary", "arbitrary", "arbitrary"),
    )(*args, *x_arrays)


def _rms_fwd(x, gain, name):
    m, d = x.shape
    tm = _div_tile(m, 512, 16)

    def body(x_ref, g_ref, o_ref):
        xv = x_ref[...]
        r = lax.rsqrt(jnp.mean(xv * xv, axis=-1, keepdims=True) + EPS)
        o_ref[...] = ((xv * r) * g_ref[...]).astype(o_ref.dtype)

    return pl.pallas_call(
        body, name=name, grid=(m // tm,),
        in_specs=[pl.BlockSpec((tm, d), lambda i: (i, 0)), pl.BlockSpec((1, d), lambda i: (0, 0))],
        out_specs=pl.BlockSpec((tm, d), lambda i: (i, 0)),
        out_shape=jax.ShapeDtypeStruct((m, d), BF16),
        compiler_params=_cp("parallel"),
    )(x, gain)


def _rms_bwd(x, gain, dy, dres, name):
    m, d = x.shape
    tm = _div_tile(m, 768, 8)

    def body(x_ref, g_ref, dy_ref, dr_ref, dx_ref, dg_ref):
        xv = x_ref[...]
        r = lax.rsqrt(jnp.mean(xv * xv, axis=-1, keepdims=True) + EPS)
        nv = xv * r
        dyv = dy_ref[...]
        gdy = dyv * g_ref[...]
        dx_ref[...] = dr_ref[...] + r * (gdy - nv * jnp.mean(gdy * nv, axis=-1, keepdims=True))
        part = jnp.sum(dyv * nv, axis=0, keepdims=True)

        @pl.when(pl.program_id(0) == 0)
        def _():
            dg_ref[...] = part

        @pl.when(pl.program_id(0) > 0)
        def _():
            dg_ref[...] += part

    row = pl.BlockSpec((tm, d), lambda i: (i, 0))
    vec = pl.BlockSpec((1, d), lambda i: (0, 0))
    return pl.pallas_call(
        body, name=name, grid=(m // tm,),
        in_specs=[row, vec, row, row], out_specs=[row, vec],
        out_shape=[jax.ShapeDtypeStruct((m, d), F32), jax.ShapeDtypeStruct((1, d), F32)],
        compiler_params=_cp("arbitrary"),
    )(x, gain, dy, dres)


def _head_stats(xv, lo):
    sq = xv * xv
    s_lo = jnp.sum(jnp.where(lo, sq, 0.0), axis=1, keepdims=True)
    s_hi = jnp.sum(jnp.where(lo, 0.0, sq), axis=1, keepdims=True)
    return jnp.where(lo, s_lo, s_hi) * (1.0 / FOX_HEAD_DIM)


BIAS_LANE = FOX_HEAD_DIM
N_SPLIT = 3


def _split3(c):
    c1 = c.astype(BF16).astype(F32)
    r1 = c - c1
    c2 = r1.astype(BF16).astype(F32)
    c3 = (r1 - c2).astype(BF16).astype(F32)
    return c1, c2, c3


def _fox_prep(proj, qg, kg, bf, nb, lp, name):
    m = proj.shape[0]
    ts = SEQ_BLOCK
    tg = _hg_rows(lp)
    nblk = lp // tg
    scale = FOX_HEAD_DIM ** -0.5

    def body(q_ref, k_ref, v_ref, f_ref, qg_ref, kg_ref, bf_ref, qo_ref, ko_ref, vo_ref, carry_ref):
        @pl.when(pl.program_id(1) == 0)
        def _():
            carry_ref[...] = jnp.zeros_like(carry_ref)

        for hb in range(tg // ts):
            rows = pl.ds(hb * ts, ts)
            block(q_ref.at[rows], k_ref.at[rows], v_ref.at[rows], f_ref.at[rows], qg_ref, kg_ref, bf_ref,
                  qo_ref.at[rows], ko_ref.at[rows], vo_ref.at[rows], carry_ref)

    def block(q_ref, k_ref, v_ref, f_ref, qg_ref, kg_ref, bf_ref, qo_ref, ko_ref, vo_ref, carry_ref):
        lane = lax.broadcasted_iota(jnp.int32, (1, LANES), 1)
        lo = lane < FOX_HEAD_DIM
        z = f_ref[...] + bf_ref[...]
        logf = jnp.minimum(z, 0.0) - jnp.log(1.0 + jnp.exp(-jnp.abs(z)))
        logf = jnp.where(lane < FOX_HEADS, logf, 0.0)
        r = lax.broadcasted_iota(jnp.int32, (ts, ts), 0)
        c = lax.broadcasted_iota(jnp.int32, (ts, ts), 1)
        tri = jnp.where(c <= r, 1.0, 0.0).astype(F32)
        cum = _dot(tri, logf, NN, HI) + carry_ref[...]
        carry_ref[...] = cum[ts - 1:ts, :]

        ones = jnp.where((lane >= BIAS_LANE + N_SPLIT) & (lane < BIAS_LANE + 2 * N_SPLIT), 1.0, 0.0)
        ones_k = jnp.where((lane >= BIAS_LANE) & (lane < BIAS_LANE + N_SPLIT), 1.0, 0.0)
        for j in range(FOX_WIDTH // LANES):
            cs = slice(j * LANES, (j + 1) * LANES)
            xq = q_ref[:, cs]
            yq = ((xq * lax.rsqrt(_head_stats(xq, lo) + EPS)) * qg_ref[:, cs]) * scale
            xk = k_ref[:, cs]
            yk = (xk * lax.rsqrt(_head_stats(xk, lo) + EPS)) * kg_ref[:, cs]
            for hh in range(2):
                h = 2 * j + hh
                pieces = _split3(_lane_pick(cum, lane, h))
                qb, kb = ones, ones_k
                for i, piece in enumerate(pieces):
                    qb = jnp.where(lane == BIAS_LANE + i, piece, qb)
                    kb = jnp.where(lane == BIAS_LANE + N_SPLIT + i, -piece, kb)
                yq_h = yq if hh == 0 else pltpu.roll(yq, FOX_HEAD_DIM, 1)
                yk_h = yk if hh == 0 else pltpu.roll(yk, FOX_HEAD_DIM, 1)
                hs = slice(h * LANES, (h + 1) * LANES)
                qo_ref[:, hs] = jnp.where(lo, yq_h, qb).astype(BF16)
                ko_ref[:, hs] = jnp.where(lo, yk_h, kb).astype(BF16)
        vo_ref[...] = v_ref[...].astype(BF16)

    w = FOX_WIDTH
    row = lambda b, i: (b * nblk + i, 0)
    return pl.pallas_call(
        body, name=name, grid=(nb, nblk),
        in_specs=[pl.BlockSpec((tg, w), lambda b, i: (b * nblk + i, FOX_CB)),
                  pl.BlockSpec((tg, w), lambda b, i: (b * nblk + i, FOX_CB + 1)),
                  pl.BlockSpec((tg, w), lambda b, i: (b * nblk + i, FOX_CB + 2)),
                  pl.BlockSpec((tg, LANES), lambda b, i: (b * nblk + i, CB_FF)),
                  pl.BlockSpec((1, w), lambda b, i: (0, 0)),
                  pl.BlockSpec((1, w), lambda b, i: (0, 0)),
                  pl.BlockSpec((1, LANES), lambda b, i: (0, 0))],
        out_specs=[pl.BlockSpec((tg, 2 * w), row), pl.BlockSpec((tg, 2 * w), row), pl.BlockSpec((tg, w), row)],
        out_shape=[jax.ShapeDtypeStruct((m, 2 * w), BF16)] * 2 + [jax.ShapeDtypeStruct((m, w), BF16)],
        scratch_shapes=[pltpu.VMEM((1, LANES), F32)],
        compiler_params=_cp("arbitrary", "arbitrary"),
    )(proj, proj, proj, proj, qg, kg, bf)


def _att_tile(lp):
    return 384 if (lp % 384 == 0 and lp > 384) else 128


def _lane_pick(blk, lane, idx):
    return jnp.sum(jnp.where(lane == idx, blk, 0.0), axis=1, keepdims=True)


def _head_masks():
    lane = lax.broadcasted_iota(jnp.int32, (1, LANES), 1)
    return lane, [(lane >= hh * FOX_HEAD_DIM) & (lane < (hh + 1) * FOX_HEAD_DIM) for hh in range(2)]


def _fox_fwd(qa, ka, vb, nb, lp, name, ride=()):
    m = qa.shape[0]
    tq = _att_tile(lp)
    nq = lp // tq
    npair = FOX_WIDTH // LANES
    grid = (nb, npair, nq)
    r_arrays, r_shapes, r_sems = _exchange_io((), ride)
    nr = len(r_arrays)

    def body(q_ref, k_ref, v_ref, *rest):
        r_in, (o_ref, lse_ref), r_out, sems = rest[:nr], rest[nr:nr + 2], rest[nr + 2:2 * nr + 2], rest[2 * nr + 2:]
        if nr:
            first, last = _grid_ends(grid)
            start, wait = _exchange_copies(r_in, r_out, 0, *sems)
            pl.when(first)(start)
        qi = pl.program_id(2)
        lane, hmasks = _head_masks()
        zero16 = jnp.zeros((), BF16)
        causal = lax.broadcasted_iota(jnp.int32, (tq, 1), 0) >= lax.broadcasted_iota(jnp.int32, (1, tq), 1)
        qs = [q_ref[:, hh * LANES:(hh + 1) * LANES] for hh in range(2)]

        def tile(j, carry, diagonal):
            k0 = pl.multiple_of(j * tq, tq)
            vb = v_ref[pl.ds(k0, tq), :]
            out = []
            for hh in range(2):
                mx, l, acc = carry[3 * hh:3 * hh + 3]
                vz = jnp.where(hmasks[hh], vb, zero16)
                s = _dot(qs[hh], k_ref[pl.ds(k0, tq), hh * LANES:(hh + 1) * LANES], NT)
                if diagonal:
                    s = jnp.where(causal, s, NEG)
                m_new = jnp.maximum(mx, jnp.max(s, axis=1, keepdims=True))
                alpha = jnp.exp(mx - m_new)
                pe = jnp.exp(s - m_new)
                l = alpha * l + jnp.sum(pe, axis=1, keepdims=True)
                acc = alpha * acc + _dot(pe.astype(BF16), vz, NN)
                out += [m_new, l, acc]
            return tuple(out)

        init = (jnp.full((tq, 1), NEG, F32), jnp.zeros((tq, 1), F32), jnp.zeros((tq, LANES), F32)) * 2
        carry = lax.fori_loop(0, qi, lambda j, c: tile(j, c, False), init)
        m0, l0, acc0, m1, l1, acc1 = tile(qi, carry, True)
        o_ref[...] = acc0 / l0 + acc1 / l1
        lse_ref[...] = jnp.where(lane == 0, m0 + jnp.log(l0), jnp.where(lane == 1, m1 + jnp.log(l1), 0.0))
        if nr:
            pl.when(last)(wait)

    return pl.pallas_call(
        body, name=name, grid=grid,
        in_specs=[pl.BlockSpec((tq, 2 * LANES), lambda b, p, i: (b * nq + i, p)),
                  pl.BlockSpec((lp, 2 * LANES), lambda b, p, i: (b, p)),
                  pl.BlockSpec((lp, LANES), lambda b, p, i: (b, p))] + [_HBM] * nr,
        out_specs=[pl.BlockSpec((tq, LANES), lambda b, p, i: (b * nq + i, p)),
                   pl.BlockSpec((None, None, tq, LANES), lambda b, p, i: (b, p, i, 0))] + [_HBM] * nr,
        out_shape=[jax.ShapeDtypeStruct((m, FOX_WIDTH), F32),
                   jax.ShapeDtypeStruct((nb, npair, lp, LANES), F32)] + r_shapes,
        scratch_shapes=r_sems if nr else [],
        compiler_params=_cp(*(["arbitrary"] * 3 if nr else ["parallel", "parallel", "arbitrary"])),
    )(qa, ka, vb, *r_arrays)


def _fox_bwd(qa, ka, vb, do, o, lse, nb, lp, name, ride=()):
    m = qa.shape[0]
    tq = _att_tile(lp)
    nq = lp // tq
    npair = FOX_WIDTH // LANES
    grid = (nb, npair, nq)
    r_arrays, r_shapes, r_sems = _exchange_io(ride, ())
    nr = len(r_arrays)

    def body(k_ref, v_ref, q_ref, do_ref, o_ref, lse_ref, *rest):
        r_in, r_out, sems = rest[:nr], rest[nr + 5:2 * nr + 5], rest[2 * nr + 5:]
        dq_ref, dk_ref, dv_ref, dc0_ref, dc1_ref = rest[nr:nr + 5]
        if nr:
            first, last = _grid_ends(grid)
            start, wait = _exchange_copies(r_in, r_out, nr, *sems)
            pl.when(first)(start)
        j = pl.program_id(2)
        lane, hmasks = _head_masks()
        zero16 = jnp.zeros((), BF16)
        causal = lax.broadcasted_iota(jnp.int32, (tq, 1), 0) >= lax.broadcasted_iota(jnp.int32, (1, tq), 1)

        @pl.when(j == 0)
        def _():
            dq_ref[...] = jnp.zeros_like(dq_ref)

        vv = v_ref[...]
        vzs = [jnp.where(hm, vv, zero16) for hm in hmasks]

        def tile(qi, carry, diagonal):
            dk0, dk1, dv, dc0, dc1 = carry
            q0 = pl.multiple_of(qi * tq, tq)
            dob16 = do_ref[pl.ds(q0, tq), :].astype(BF16)
            ob = o_ref[pl.ds(q0, tq), :]
            lseb = lse_ref[pl.ds(q0, tq), :]
            dks, dcs = [dk0, dk1], [dc0, dc1]
            for hh in range(2):
                hs = slice(hh * LANES, (hh + 1) * LANES)
                q = q_ref[pl.ds(q0, tq), hs]
                doz16 = jnp.where(hmasks[hh], dob16, zero16)
                delta = jnp.sum(doz16.astype(F32) * ob, axis=1, keepdims=True)
                s = _dot(q, k_ref[:, hs], NT) - _lane_pick(lseb, lane, hh)
                if diagonal:
                    s = jnp.where(causal, s, NEG)
                pm = jnp.exp(s)
                ds = pm * (_dot(doz16, vzs[hh], NT) - delta)
                ds16 = ds.astype(BF16)
                dv = dv + _dot(pm.astype(BF16), doz16, TN)
                dks[hh] = dks[hh] + _dot(ds16, q, TN)
                dq_ref[pl.ds(q0, tq), hs] += _dot(ds16, k_ref[:, hs], NN)
                dcs[hh] = dcs[hh] - jnp.sum(ds, axis=0, keepdims=True)
            return dks[0], dks[1], dv, dcs[0], dcs[1]

        zt = jnp.zeros((tq, LANES), F32)
        zr = jnp.zeros((1, tq), F32)
        carry = tile(j, (zt, zt, zt, zr, zr), True)
        dk0, dk1, dv, dc0, dc1 = lax.fori_loop(j + 1, nq, lambda qi, c: tile(qi, c, False), carry)
        dk_ref[:, :LANES] = dk0
        dk_ref[:, LANES:] = dk1
        dv_ref[...] = dv
        dc0_ref[...] = dc0
        dc1_ref[...] = dc1
        if nr:
            pl.when(last)(wait)

    full2 = pl.BlockSpec((lp, 2 * LANES), lambda b, p, j: (b, p))
    full = pl.BlockSpec((lp, LANES), lambda b, p, j: (b, p))
    blk2 = pl.BlockSpec((tq, 2 * LANES), lambda b, p, j: (b * nq + j, p))
    blk = pl.BlockSpec((tq, LANES), lambda b, p, j: (b * nq + j, p))
    dcs = pl.BlockSpec((None, None, 1, tq), lambda b, p, j: (b, p, 0, j))
    return pl.pallas_call(
        body, name=name, grid=grid,
        in_specs=[blk2, blk, full2, full, full,
                  pl.BlockSpec((None, None, lp, LANES), lambda b, p, j: (b, p, 0, 0))] + [_HBM] * nr,
        out_specs=[full2, blk2, blk, dcs, dcs] + [_HBM] * nr,
        out_shape=[jax.ShapeDtypeStruct((m, 2 * FOX_WIDTH), F32)] * 2 + [jax.ShapeDtypeStruct((m, FOX_WIDTH), F32)]
        + [jax.ShapeDtypeStruct((nb, npair, 1, lp), F32)] * 2 + r_shapes,
        scratch_shapes=r_sems if nr else [],
        compiler_params=_cp(*(["arbitrary"] * 3 if nr else ["parallel", "parallel", "arbitrary"])),
    )(ka, vb, qa, do, o, lse, *r_arrays)


def _fox_prep_bwd(proj, dqa, dka, dv, dcum, qg, kg, bf, nb, lp, name):
    m = proj.shape[0]
    ts = SEQ_BLOCK
    tg = _hg_rows(lp)
    nblk = lp // tg
    scale = FOX_HEAD_DIM ** -0.5
    w = FOX_WIDTH
    wo = 3 * w

    def body(q_ref, k_ref, f_ref, dq_ref, dk_ref, dv_ref, dc_ref, qg_ref, kg_ref, bf_ref,
             out_ref, dff_ref, dqg_ref, dkg_ref, dbf_ref, carry_ref):
        @pl.when((pl.program_id(0) == 0) & (pl.program_id(1) == 0))
        def _():
            dqg_ref[...] = jnp.zeros_like(dqg_ref)
            dkg_ref[...] = jnp.zeros_like(dkg_ref)
            dbf_ref[...] = jnp.zeros_like(dbf_ref)

        @pl.when(pl.program_id(1) == 0)
        def _():
            carry_ref[...] = jnp.zeros_like(carry_ref)

        for hb in reversed(range(tg // ts)):
            rows = pl.ds(hb * ts, ts)
            block(q_ref.at[rows], k_ref.at[rows], f_ref.at[rows], dq_ref.at[rows], dk_ref.at[rows], dv_ref.at[rows],
                  dc_ref.at[rows], qg_ref, kg_ref, bf_ref, out_ref.at[rows], dff_ref.at[rows], dqg_ref, dkg_ref, dbf_ref,
                  carry_ref)

    def block(q_ref, k_ref, f_ref, dq_ref, dk_ref, dv_ref, dc_ref, qg_ref, kg_ref, bf_ref,
              out_ref, dff_ref, dqg_ref, dkg_ref, dbf_ref, carry_ref):
        lane = lax.broadcasted_iota(jnp.int32, (1, LANES), 1)
        lo = lane < FOX_HEAD_DIM

        def norm_bwd(x, g, dy):
            r = lax.rsqrt(_head_stats(x, lo) + EPS)
            nv = x * r
            gdy = dy * g
            prod = gdy * nv
            s_lo = jnp.sum(jnp.where(lo, prod, 0.0), axis=1, keepdims=True)
            s_hi = jnp.sum(jnp.where(lo, 0.0, prod), axis=1, keepdims=True)
            mean = jnp.where(lo, s_lo, s_hi) * (1.0 / FOX_HEAD_DIM)
            return r * (gdy - nv * mean), jnp.sum(dy * nv, axis=0, keepdims=True)

        def pair(d_ref, jj):
            even = d_ref[:, 2 * jj * LANES:(2 * jj + 1) * LANES]
            odd = d_ref[:, (2 * jj + 1) * LANES:(2 * jj + 2) * LANES]
            return jnp.where(lo, even, pltpu.roll(odd, FOX_HEAD_DIM, 1))

        for jj in range(w // LANES):
            cs = slice(jj * LANES, (jj + 1) * LANES)
            dx, dg = norm_bwd(q_ref[:, cs], qg_ref[:, cs], pair(dq_ref, jj) * scale)
            out_ref[:, cs] = dx.astype(BF16)
            dqg_ref[:, cs] += dg
            dx, dg = norm_bwd(k_ref[:, cs], kg_ref[:, cs], pair(dk_ref, jj))
            out_ref[:, w + jj * LANES:w + (jj + 1) * LANES] = dx.astype(BF16)
            dkg_ref[:, cs] += dg
        out_ref[:, 2 * w:3 * w] = dv_ref[...].astype(BF16)

        dc = dc_ref[...]
        r = lax.broadcasted_iota(jnp.int32, (ts, ts), 0)
        c = lax.broadcasted_iota(jnp.int32, (ts, ts), 1)
        triu = jnp.where(c >= r, 1.0, 0.0).astype(F32)
        dlogf = _dot(triu, dc, NN, HI) + carry_ref[...]
        carry_ref[...] += jnp.sum(dc, axis=0, keepdims=True)
        z = f_ref[...] + bf_ref[...]
        dz = jnp.where(lane < FOX_HEADS, dlogf * _sigmoid(-z), 0.0)
        dff_ref[...] = dz.astype(BF16)
        dbf_ref[...] += jnp.sum(dz, axis=0, keepdims=True)

    rev = lambda b, i: (b * nblk + (nblk - 1 - i), 0)
    vec = lambda n: pl.BlockSpec((1, n), lambda b, i: (0, 0))
    return pl.pallas_call(
        body, name=name, grid=(nb, nblk),
        in_specs=[pl.BlockSpec((tg, w), lambda b, i: (b * nblk + (nblk - 1 - i), FOX_CB)),
                  pl.BlockSpec((tg, w), lambda b, i: (b * nblk + (nblk - 1 - i), FOX_CB + 1)),
                  pl.BlockSpec((tg, LANES), lambda b, i: (b * nblk + (nblk - 1 - i), CB_FF)),
                  pl.BlockSpec((tg, 2 * w), rev), pl.BlockSpec((tg, 2 * w), rev), pl.BlockSpec((tg, w), rev),
                  pl.BlockSpec((tg, LANES), rev), vec(w), vec(w), vec(LANES)],
        out_specs=[pl.BlockSpec((tg, wo), rev), pl.BlockSpec((tg, LANES), rev), vec(w), vec(w), vec(LANES)],
        out_shape=[jax.ShapeDtypeStruct((m, wo), BF16), jax.ShapeDtypeStruct((m, LANES), BF16),
                   jax.ShapeDtypeStruct((1, w), F32),
                   jax.ShapeDtypeStruct((1, w), F32), jax.ShapeDtypeStruct((1, LANES), F32)],
        scratch_shapes=[pltpu.VMEM((1, LANES), F32)],
        compiler_params=_cp("arbitrary", "arbitrary"),
    )(proj, proj, proj, dqa, dka, dv, dcum, qg, kg, bf)


HG_BLOCKS_PER_STEP = 11


def _hg_rows(lp, most=3):
    nblk = lp // SEQ_BLOCK
    return SEQ_BLOCK * max(n for n in range(1, most + 1) if nblk % n == 0)


def _chunk_masks():
    r = lax.broadcasted_iota(jnp.int32, (SEQ_BLOCK, SEQ_BLOCK), 0)
    c = lax.broadcasted_iota(jnp.int32, (SEQ_BLOCK, SEQ_BLOCK), 1)
    same = (r // SUB) == (c // SUB)
    return r, c, same


def _hg_gates(hf, lb):
    sg = _sigmoid(hf)
    f = lb + (1.0 - lb) * sg
    return sg, f, jnp.log(f), (1.0 - lb) * _sigmoid(-hf)


def _hg_intra_e(g_ref, base, t, srow):
    diff = g_ref[pl.ds(base + t, 1), :] - g_ref[pl.ds(base, SUB), :]
    return jnp.exp(jnp.where(srow <= t, diff, NEG))


def _hgrn_fwd(proj, lb, gain, nb, lp, name):
    m = proj.shape[0]
    tb = _hg_rows(lp, HG_BLOCKS_PER_STEP)
    nblk = lp // tb
    nsb = tb // SEQ_BLOCK
    ns = SEQ_BLOCK // SUB

    def body(q_ref, f_ref, i_ref, g_ref, lb_ref, gain_ref, oraw_ref, y_ref, ssave_ref,
             st_ref, g_scr, kin_scr, o_scr):
        @pl.when(pl.program_id(2) == 0)
        def _():
            st_ref[...] = jnp.zeros_like(st_ref)

        for hb in range(nsb):
            rows = pl.ds(hb * SEQ_BLOCK, SEQ_BLOCK)
            block(q_ref.at[rows], f_ref.at[rows], i_ref.at[rows], g_ref.at[rows], lb_ref, gain_ref, oraw_ref.at[rows],
                  y_ref.at[rows], ssave_ref.at[hb], st_ref, g_scr.at[rows], kin_scr.at[rows], o_scr.at[rows])

    def block(q_ref, f_ref, i_ref, g_ref, lb_ref, gain_ref, oraw_ref, y_ref, ssave_ref,
              st_ref, g_scr, kin_scr, o_scr):
        ssave_ref[...] = st_ref[...]
        lbv = lb_ref[...]
        _, _, lf, kin = _hg_gates(f_ref[...], lbv)
        r, c, same = _chunk_masks()
        ltri = jnp.where(same & (c <= r), 1.0, 0.0).astype(F32)
        lall = jnp.where(same, 1.0, 0.0).astype(F32)
        g = _dot(ltri, lf, NN, HI)
        gt = _dot(lall, lf, NN, HI)
        g_scr[...] = g
        kin_scr[...] = kin
        qv = q_ref[...]
        qg = (qv * jnp.exp(g)).astype(BF16)
        kg = (kin * jnp.exp(gt - g)).astype(BF16)
        et = jnp.exp(gt)
        srow = lax.broadcasted_iota(jnp.int32, (SUB, 1), 0)
        subs = [slice(cc * SUB, (cc + 1) * SUB) for cc in range(ns)]
        ups = [_dot(i_ref[sl, :].astype(BF16), kg[sl], TN) for sl in subs]
        st = st_ref[...]
        starts = []
        for cc in range(ns):
            starts.append(st)
            st = et[cc * SUB:cc * SUB + 1, :] * st + ups[cc]
        st_ref[...] = st
        for cc, sl in enumerate(subs):
            base = cc * SUB
            kc = kin_scr[sl, :]
            vc = i_ref[sl, :]
            for t in range(SUB):
                e = _hg_intra_e(g_scr, base, t, srow)
                a = jnp.sum((q_ref[pl.ds(base + t, 1), :] * kc) * e, axis=1, keepdims=True)
                o_scr[pl.ds(base + t, 1), :] = jnp.sum(a * vc, axis=0, keepdims=True)
            o_scr[sl, :] += _dot(qg[sl], starts[cc].astype(BF16), NT)
        o = o_scr[...]
        oraw_ref[...] = o
        rr = lax.rsqrt(jnp.mean(o * o, axis=-1, keepdims=True) + EPS)
        hg = g_ref[...]
        y_ref[...] = (((o * rr) * gain_ref[...]) * (hg * _sigmoid(hg))).astype(y_ref.dtype)

    col = lambda cb: pl.BlockSpec((tb, LANES), lambda b, h, i, cb=cb: (b * nblk + i, cb + h))
    out_blk = pl.BlockSpec((tb, LANES), lambda b, h, i: (b * nblk + i, h))
    return pl.pallas_call(
        body, name=name, grid=(nb, HG_HEADS, nblk),
        in_specs=[col(CB_HQ), col(CB_HF), col(CB_HI), col(CB_HG),
                  pl.BlockSpec((1, LANES), lambda b, h, i: (0, h)),
                  pl.BlockSpec((1, LANES), lambda b, h, i: (0, 0))],
        out_specs=[out_blk, out_blk,
                   pl.BlockSpec((None, None, nsb, HG_DIM, HG_DIM), lambda b, h, i: (b, h, i, 0, 0))],
        out_shape=[jax.ShapeDtypeStruct((m, HG_WIDTH), F32), jax.ShapeDtypeStruct((m, HG_WIDTH), BF16),
                   jax.ShapeDtypeStruct((nb, HG_HEADS, nblk * nsb, HG_DIM, HG_DIM), F32)],
        scratch_shapes=[pltpu.VMEM((HG_DIM, HG_DIM), F32), pltpu.VMEM((tb, LANES), F32),
                        pltpu.VMEM((tb, LANES), F32), pltpu.VMEM((tb, LANES), F32)],
        compiler_params=_cp("parallel", "parallel", "arbitrary"),
    )(proj, proj, proj, proj, lb, gain)


def _hgrn_bwd(proj, oraw, ssave, dy, lb, gain, nb, lp, name):
    m = proj.shape[0]
    tb = _hg_rows(lp, HG_BLOCKS_PER_STEP)
    nblk = lp // tb
    nsb = tb // SEQ_BLOCK
    ns = SEQ_BLOCK // SUB

    def body(q_ref, f_ref, i_ref, g_ref, oraw_ref, ssave_ref, dy_ref, lb_ref, gain_ref,
             dq_ref, df_ref, di_ref, dg_ref, dgain_ref, dlb_ref,
             dst_ref, *scratch):
        hd = pl.program_id(0)
        bb = pl.program_id(1)
        ii = pl.program_id(2)

        @pl.when((hd == 0) & (bb == 0) & (ii == 0))
        def _():
            dgain_ref[...] = jnp.zeros_like(dgain_ref)

        @pl.when((bb == 0) & (ii == 0))
        def _():
            dlb_ref[...] = jnp.zeros_like(dlb_ref)

        @pl.when(ii == 0)
        def _():
            dst_ref[...] = jnp.zeros_like(dst_ref)

        for hb in reversed(range(nsb)):
            rows = pl.ds(hb * SEQ_BLOCK, SEQ_BLOCK)
            block(q_ref.at[rows], f_ref.at[rows], i_ref.at[rows], g_ref.at[rows], oraw_ref.at[rows], ssave_ref.at[hb],
                  dy_ref.at[rows], lb_ref, gain_ref, dq_ref.at[rows], df_ref.at[rows], di_ref.at[rows], dg_ref.at[rows],
                  dgain_ref, dlb_ref, dst_ref, *[sc.at[rows] for sc in scratch])

    def block(q_ref, f_ref, i_ref, g_ref, oraw_ref, ssave_ref, dy_ref, lb_ref, gain_ref,
              dq_ref, df_ref, di_ref, dg_ref, dgain_ref, dlb_ref,
              dst_ref, g_scr, kin_scr, do_scr, dq_scr, dk_scr, dv_scr, dgg_scr):
        gainv = gain_ref[...]
        lbv = lb_ref[...]

        o = oraw_ref[...]
        rr = lax.rsqrt(jnp.mean(o * o, axis=-1, keepdims=True) + EPS)
        nv = o * rr
        hg = g_ref[...]
        sgg = _sigmoid(hg)
        sil = hg * sgg
        dyv = dy_ref[...]
        dg_ref[...] = (dyv * nv * gainv * (sgg * (1.0 + hg * (1.0 - sgg)))).astype(dg_ref.dtype)
        dgain_ref[...] += jnp.sum(dyv * nv * sil, axis=0, keepdims=True)
        dn = dyv * gainv * sil
        do_scr[...] = rr * (dn - nv * jnp.mean(dn * nv, axis=-1, keepdims=True))

        hf = f_ref[...]
        sg, f, lf, kin = _hg_gates(hf, lbv)
        r, c, same = _chunk_masks()
        ltri = jnp.where(same & (c <= r), 1.0, 0.0).astype(F32)
        lall = jnp.where(same, 1.0, 0.0).astype(F32)
        g = _dot(ltri, lf, NN, HI)
        gt = _dot(lall, lf, NN, HI)
        g_scr[...] = g
        kin_scr[...] = kin
        qv = q_ref[...]
        eg = jnp.exp(g)
        ekg = jnp.exp(gt - g)
        qg = qv * eg
        kg = kin * ekg
        qg16 = qg.astype(BF16)
        kg16 = kg.astype(BF16)
        et = jnp.exp(gt)
        subs = [slice(cc * SUB, (cc + 1) * SUB) for cc in range(ns)]
        ups = [_dot(i_ref[sl, :].astype(BF16), kg16[sl], TN) for sl in subs]
        st = ssave_ref[...]
        starts = []
        for cc in range(ns):
            starts.append(st)
            st = et[cc * SUB:cc * SUB + 1, :] * st + ups[cc]
        do16 = do_scr[...].astype(BF16)
        downs = [_dot(do16[sl], qg16[sl], TN) for sl in subs]
        dst = dst_ref[...]
        afters = [None] * ns
        for cc in reversed(range(ns)):
            afters[cc] = dst
            dst = et[cc * SUB:cc * SUB + 1, :] * dst + downs[cc]
        dst_ref[...] = dst

        srow = lax.broadcasted_iota(jnp.int32, (SUB, 1), 0)
        for cc, sl in enumerate(subs):
            base = cc * SUB
            st = starts[cc]
            st16 = st.astype(BF16)
            dst = afters[cc]
            dst16 = dst.astype(BF16)
            doc16 = do16[sl]
            vc = i_ref[sl, :]
            vc16 = vc.astype(BF16)
            kc = kin_scr[sl, :]
            etc = et[base:base + 1, :]
            dqg = _dot(doc16, st16, NN)
            dv_c = _dot(kg16[sl], dst16, NT)
            dkg = _dot(vc16, dst16, NN)
            dgt = jnp.sum(dst * st, axis=0, keepdims=True) * etc
            dq_c = dqg * eg[sl]
            dk_c = dkg * ekg[sl]
            dg_c = dqg * qg[sl] - dkg * kg[sl]
            dgt = dgt + jnp.sum(dkg * kg[sl], axis=0, keepdims=True)
            for t in range(SUB):
                e = _hg_intra_e(g_scr, base, t, srow)
                qt = q_ref[pl.ds(base + t, 1), :]
                dot_t = do_scr[pl.ds(base + t, 1), :]
                a = jnp.sum((qt * kc) * e, axis=1, keepdims=True)
                da = jnp.sum(dot_t * vc, axis=1, keepdims=True)
                dv_c = dv_c + a * dot_t
                w = da * e
                dq_scr[pl.ds(base + t, 1), :] = jnp.sum(w * kc, axis=0, keepdims=True)
                wq = w * qt
                dk_c = dk_c + wq
                dg_c = dg_c - kc * wq
            dq_i = dq_scr[sl, :]
            dg_c = dg_c + qv[sl] * dq_i + jnp.where(srow == SUB - 1, dgt, 0.0)
            dq_scr[sl, :] = dq_c + dq_i
            dk_scr[sl, :] = dk_c
            dv_scr[sl, :] = dv_c
            dgg_scr[sl, :] = dg_c

        utri = jnp.where(same & (c >= r), 1.0, 0.0).astype(F32)
        dlf = _dot(utri, dgg_scr[...], NN, HI)
        dkin = dk_scr[...]
        dsg = sg * (1.0 - sg)
        df_ref[...] = ((dlf / f - dkin) * ((1.0 - lbv) * dsg)).astype(df_ref.dtype)
        dlb_ref[...] += jnp.sum((dlf / f - dkin) * (1.0 - sg), axis=0, keepdims=True)
        dq_ref[...] = dq_scr[...].astype(dq_ref.dtype)
        di_ref[...] = dv_scr[...].astype(di_ref.dtype)

    rowi = lambda b, i: b * nblk + (nblk - 1 - i)
    col = lambda cb: pl.BlockSpec((tb, LANES), lambda h, b, i, cb=cb: (rowi(b, i), cb + h))
    hblk = pl.BlockSpec((tb, LANES), lambda h, b, i: (rowi(b, i), h))
    return pl.pallas_call(
        body, name=name, grid=(HG_HEADS, nb, nblk),
        in_specs=[col(CB_HQ), col(CB_HF), col(CB_HI), col(CB_HG), hblk,
                  pl.BlockSpec((None, None, nsb, HG_DIM, HG_DIM), lambda h, b, i: (b, h, nblk - 1 - i, 0, 0)),
                  hblk,
                  pl.BlockSpec((1, LANES), lambda h, b, i: (0, h)),
                  pl.BlockSpec((1, LANES), lambda h, b, i: (0, 0))],
        out_specs=[hblk, hblk, hblk, hblk,
                   pl.BlockSpec((1, LANES), lambda h, b, i: (0, 0)),
                   pl.BlockSpec((1, LANES), lambda h, b, i: (0, h))],
        out_shape=[jax.ShapeDtypeStruct((m, HG_WIDTH), BF16)] * 4
        + [jax.ShapeDtypeStruct((1, LANES), F32), jax.ShapeDtypeStruct((1, HG_WIDTH), F32)],
        scratch_shapes=[pltpu.VMEM((HG_DIM, HG_DIM), F32)] + [pltpu.VMEM((tb, LANES), F32)] * 7,
        compiler_params=_cp("arbitrary", "arbitrary", "arbitrary"),
    )(proj, proj, proj, proj, oraw, ssave, dy, lb, gain)


def _gate_fwd(proj, ya, yb, name):
    m = proj.shape[0]
    tm = _div_tile(m, 384, 16)

    def body(ga_ref, gb_ref, ya_ref, yb_ref, o_ref):
        ya, yb = ya_ref[...].astype(F32), yb_ref[...].astype(F32)
        o_ref[...] = (_sigmoid(ga_ref[...]) * ya + _sigmoid(gb_ref[...]) * yb).astype(o_ref.dtype)

    row = pl.BlockSpec((tm, D_MODEL), lambda i: (i, 0))
    return pl.pallas_call(
        body, name=name, grid=(m // tm,),
        in_specs=[row, pl.BlockSpec((tm, D_MODEL), lambda i: (i, 1)), row, row],
        out_specs=row, out_shape=jax.ShapeDtypeStruct((m, D_MODEL), BF16),
        compiler_params=_cp("parallel"),
    )(proj, proj, ya, yb)


def _gate_bwd(proj, ya, yb, dm, name):
    m = proj.shape[0]
    tm = _div_tile(m, 384, 16)

    def body(ga_ref, gb_ref, ya_ref, yb_ref, dm_ref, dya_ref, dyb_ref, dg_ref):
        dmv = dm_ref[...].astype(F32)
        sa = _sigmoid(ga_ref[...])
        sb = _sigmoid(gb_ref[...])
        dya_ref[...] = (dmv * sa).astype(BF16)
        dyb_ref[...] = (dmv * sb).astype(BF16)
        dg_ref[:, :D_MODEL] = (dmv * ya_ref[...].astype(F32) * (sa * (1.0 - sa))).astype(BF16)
        dg_ref[:, D_MODEL:] = (dmv * yb_ref[...].astype(F32) * (sb * (1.0 - sb))).astype(BF16)

    row = pl.BlockSpec((tm, D_MODEL), lambda i: (i, 0))
    wide = pl.BlockSpec((tm, 2 * D_MODEL), lambda i: (i, 0))
    return pl.pallas_call(
        body, name=name, grid=(m // tm,),
        in_specs=[row, pl.BlockSpec((tm, D_MODEL), lambda i: (i, 1)), row, row, row],
        out_specs=[row, row, wide],
        out_shape=[jax.ShapeDtypeStruct((m, D_MODEL), BF16)] * 2 + [jax.ShapeDtypeStruct((m, 2 * D_MODEL), BF16)],
        compiler_params=_cp("parallel"),
    )(proj, proj, ya, yb, dm)


CONV_ROWS = 128


def _conv3(x, xprev, w_ref, b_ref, rowi):
    r = x.shape[0]
    x1 = jnp.where(rowi < 1, pltpu.roll(xprev, 1, 0), pltpu.roll(x, 1, 0))
    x2 = jnp.where(rowi < 2, pltpu.roll(xprev, 2, 0), pltpu.roll(x, 2, 0))
    u = w_ref[0:1, :] * x2 + w_ref[1:2, :] * x1 + w_ref[2:3, :] * x + b_ref[...]
    return u, x1, x2


def _conv_fwd(up, cw, cb, nb, lp, name):
    m = up.shape[0]
    nct = D_FF // LANES
    r = CONV_ROWS
    nch = lp // r

    def body(u_ref, w_ref, b_ref, o_ref):
        rowi = lax.broadcasted_iota(jnp.int32, (r, 1), 0)

        def step(i, xp):
            r0 = pl.multiple_of(i * r, r)
            xc = u_ref[pl.ds(r0, r), :].astype(F32)
            u, _, _ = _conv3(xc, xp, w_ref, b_ref, rowi)
            ug, uv = u[:, :LANES], u[:, LANES:]
            o_ref[pl.ds(r0, r), :] = ((ug * _sigmoid(ug)) * uv).astype(o_ref.dtype)
            return xc

        lax.fori_loop(0, nch, step, jnp.zeros((r, 2 * LANES), F32))

    return pl.pallas_call(
        body, name=name, grid=(nb, nct),
        in_specs=[pl.BlockSpec((lp, 2 * LANES), lambda b, c: (b, c)),
                  pl.BlockSpec((CONV_WIDTH, 2 * LANES), lambda b, c: (0, c)),
                  pl.BlockSpec((1, 2 * LANES), lambda b, c: (0, c))],
        out_specs=pl.BlockSpec((lp, LANES), lambda b, c: (b, c)),
        out_shape=jax.ShapeDtypeStruct((m, D_FF), BF16),
        compiler_params=_cp("parallel", "parallel"),
    )(up, cw, cb)


def _conv_bwd(up, dact, cw, cb, nb, lp, name):
    m = up.shape[0]
    nct = D_FF // LANES
    r = CONV_ROWS
    nch = lp // r

    def body(u_ref, da_ref, w_ref, b_ref, dup_ref, dw_ref, db_ref):
        rowi = lax.broadcasted_iota(jnp.int32, (r, 1), 0)
        wv = w_ref[...]

        def step(k, carry):
            dun, dw0, dw1, dw2, dbs = carry
            i = nch - 1 - k
            r0 = pl.multiple_of(i * r, r)
            rp = pl.multiple_of(jnp.maximum(i - 1, 0) * r, r)
            xc = u_ref[pl.ds(r0, r), :].astype(F32)
            xp = u_ref[pl.ds(rp, r), :].astype(F32) * (i > 0).astype(F32)
            u, x1, x2 = _conv3(xc, xp, w_ref, b_ref, rowi)
            ug, uv = u[:, :LANES], u[:, LANES:]
            da = da_ref[pl.ds(r0, r), :].astype(F32)
            sg = _sigmoid(ug)
            du = jnp.concatenate([da * uv * (sg * (1.0 + ug * (1.0 - sg))), da * (ug * sg)], axis=1)
            d1 = jnp.where(rowi >= r - 1, pltpu.roll(dun, r - 1, 0), pltpu.roll(du, r - 1, 0))
            d2 = jnp.where(rowi >= r - 2, pltpu.roll(dun, r - 2, 0), pltpu.roll(du, r - 2, 0))
            dup_ref[pl.ds(r0, r), :] = (wv[2:3, :] * du + wv[1:2, :] * d1 + wv[0:1, :] * d2).astype(dup_ref.dtype)
            dw0 = dw0 + jnp.sum(du * x2, axis=0, keepdims=True)
            dw1 = dw1 + jnp.sum(du * x1, axis=0, keepdims=True)
            dw2 = dw2 + jnp.sum(du * xc, axis=0, keepdims=True)
            dbs = dbs + jnp.sum(du, axis=0, keepdims=True)
            return du, dw0, dw1, dw2, dbs

        z1 = jnp.zeros((1, 2 * LANES), F32)
        _, dw0, dw1, dw2, dbs = lax.fori_loop(0, nch, step, (jnp.zeros((r, 2 * LANES), F32), z1, z1, z1, z1))

        @pl.when(pl.program_id(1) == 0)
        def _():
            dw_ref[...] = jnp.zeros_like(dw_ref)
            db_ref[...] = jnp.zeros_like(db_ref)

        dw_ref[0:1, :] += dw0
        dw_ref[1:2, :] += dw1
        dw_ref[2:3, :] += dw2
        db_ref[...] += dbs

    return pl.pallas_call(
        body, name=name, grid=(nct, nb),
        in_specs=[pl.BlockSpec((lp, 2 * LANES), lambda c, b: (b, c)),
                  pl.BlockSpec((lp, LANES), lambda c, b: (b, c)),
                  pl.BlockSpec((CONV_WIDTH, 2 * LANES), lambda c, b: (0, c)),
                  pl.BlockSpec((1, 2 * LANES), lambda c, b: (0, c))],
        out_specs=[pl.BlockSpec((lp, 2 * LANES), lambda c, b: (b, c)),
                   pl.BlockSpec((CONV_WIDTH, 2 * LANES), lambda c, b: (0, c)),
                   pl.BlockSpec((1, 2 * LANES), lambda c, b: (0, c))],
        out_shape=[jax.ShapeDtypeStruct((m, 2 * D_FF), BF16),
                   jax.ShapeDtypeStruct((CONV_WIDTH, 2 * D_FF), F32),
                   jax.ShapeDtypeStruct((1, 2 * D_FF), F32)],
        compiler_params=_cp("parallel", "arbitrary"),
    )(up, dact, cw, cb)


def _ffn_interleave(a, axis):
    shp = a.shape
    a = a.reshape(shp[:axis] + (2, D_FF // LANES, LANES) + shp[axis + 1:])
    return jnp.swapaxes(a, axis, axis + 1).reshape(shp)


def _ffn_deinterleave(a, axis):
    shp = a.shape
    a = a.reshape(shp[:axis] + (D_FF // LANES, 2, LANES) + shp[axis + 1:])
    return jnp.swapaxes(a, axis, axis + 1).reshape(shp)


def _shifted_rows(prev_ref, cur_ref):
    keep = SEQ_BLOCK - N_META
    return jnp.concatenate([prev_ref[keep:, :], cur_ref[:keep, :]], axis=0)


def _frame_specs(nb, nfb, d):
    prev = pl.BlockSpec((nb, SEQ_BLOCK, d), lambda i: (0, jnp.clip(i - 1, 0, nfb - 1), 0))
    cur = pl.BlockSpec((nb, SEQ_BLOCK, d), lambda i: (0, jnp.clip(i, 0, nfb - 1), 0))
    return prev, cur


def _embed_rms(x, meta, gain, lp, name):
    nb, seq, d = x.shape
    l = seq + N_META
    tr = SEQ_BLOCK
    nblk = lp // tr

    def body(prev_ref, cur_ref, meta_ref, g_ref, h_ref, o_ref):
        t = pl.program_id(0) * tr + lax.broadcasted_iota(jnp.int32, (tr, 1), 0)
        head = jnp.concatenate([meta_ref[...], jnp.zeros((tr - N_META, d), F32)], axis=0)
        for b in range(nb):
            rows = jnp.where(t < l, _shifted_rows(prev_ref.at[b], cur_ref.at[b]), 0.0)
            xv = jnp.where(t < N_META, head, rows)
            h_ref[b] = xv
            r = lax.rsqrt(jnp.mean(xv * xv, axis=-1, keepdims=True) + EPS)
            o_ref[b] = ((xv * r) * g_ref[...]).astype(o_ref.dtype)

    prev, cur = _frame_specs(nb, seq // tr, d)
    row = pl.BlockSpec((nb, tr, d), lambda i: (0, i, 0))
    h0, xn = pl.pallas_call(
        body, name=name, grid=(nblk,),
        in_specs=[prev, cur, pl.BlockSpec((N_META, d), lambda i: (0, 0)), pl.BlockSpec((1, d), lambda i: (0, 0))],
        out_specs=[row, row],
        out_shape=[jax.ShapeDtypeStruct((nb, lp, d), F32), jax.ShapeDtypeStruct((nb, lp, d), BF16)],
        compiler_params=_cp("parallel"),
    )(x, x, meta, gain)
    return h0.reshape(nb * lp, d), xn.reshape(nb * lp, d)


def _loss_head(out, target, lp, name):
    nb, seq, d = target.shape
    l = seq + N_META
    tr = SEQ_BLOCK
    nblk = lp // tr

    def body(o_ref, prev_ref, cur_ref, dy_ref, ls_ref):
        t = pl.program_id(0) * tr + lax.broadcasted_iota(jnp.int32, (tr, 1), 0)
        valid = (t >= N_META) & (t < l)
        part = jnp.zeros((1, d), F32)
        for b in range(nb):
            err = jnp.where(valid, o_ref[b] - _shifted_rows(prev_ref.at[b], cur_ref.at[b]), 0.0)
            dy_ref[b] = err * (1.0 / d)
            part = part + jnp.sum(err * err, axis=0, keepdims=True)

        @pl.when(pl.program_id(0) == 0)
        def _():
            ls_ref[...] = part

        @pl.when(pl.program_id(0) > 0)
        def _():
            ls_ref[...] += part

    prev, cur = _frame_specs(nb, seq // tr, d)
    row = pl.BlockSpec((nb, tr, d), lambda i: (0, i, 0))
    dy, lsum = pl.pallas_call(
        body, name=name, grid=(nblk,),
        in_specs=[row, prev, cur], out_specs=[row, pl.BlockSpec((1, d), lambda i: (0, 0))],
        out_shape=[jax.ShapeDtypeStruct((nb, lp, d), F32), jax.ShapeDtypeStruct((1, d), F32)],
        compiler_params=_cp("arbitrary"),
    )(out.reshape(nb, lp, d), target, target)
    return dy.reshape(nb * lp, d), lsum


def _adam_math(g, w, mom, var):
    c1 = 1.0 - ADAM_B1 ** ADAM_STEP
    c2 = 1.0 - ADAM_B2 ** ADAM_STEP
    mn = ADAM_B1 * mom + (1.0 - ADAM_B1) * g
    vn = ADAM_B2 * var + (1.0 - ADAM_B2) * (g * g)
    delta = -ADAM_LR * ((mn / c1) / (jnp.sqrt(vn / c2) + ADAM_EPS) + ADAM_WD * w)
    return delta, mn, vn


def _slot_sum(recv, name):
    _, r, c = recv.shape
    tc = _div_tile(c, 256, LANES)

    def body(r_ref, g_ref):
        g = r_ref[0].astype(F32)
        for s in range(1, N_DEV):
            g = g + r_ref[s].astype(F32)
        g_ref[...] = g

    return pl.pallas_call(
        body, name=name, grid=(c // tc,),
        in_specs=[pl.BlockSpec((N_DEV, r, tc), lambda j: (0, 0, j))],
        out_specs=pl.BlockSpec((r, tc), lambda j: (0, j)),
        out_shape=jax.ShapeDtypeStruct((r, c), F32),
        compiler_params=_cp("parallel"),
    )(recv)


def _adamw(g, w, mom, var, name):
    r, c = w.shape
    tr = _div_tile(r, 256, 8)

    def body(g_ref, w_ref, m_ref, v_ref, d_ref, mo_ref, vo_ref):
        d_ref[...], mo_ref[...], vo_ref[...] = _adam_math(g_ref[...], w_ref[...], m_ref[...], v_ref[...])

    row = pl.BlockSpec((tr, c), lambda i: (i, 0))
    return pl.pallas_call(
        body, name=name, grid=(r // tr,), in_specs=[row] * 4, out_specs=[row] * 3,
        out_shape=[jax.ShapeDtypeStruct((r, c), F32)] * 3,
        compiler_params=_cp("parallel"),
    )(g, w, mom, var)


def _sum_adamw(recv, w, mom, var, name):
    r, c = w.shape
    tr = _div_tile(r, 256, 8)

    def body(r_ref, w_ref, m_ref, v_ref, g_ref, d_ref, mo_ref, vo_ref):
        g = r_ref[0].astype(F32)
        for s in range(1, N_DEV):
            g = g + r_ref[s].astype(F32)
        g_ref[...] = g
        d_ref[...], mo_ref[...], vo_ref[...] = _adam_math(g, w_ref[...], m_ref[...], v_ref[...])

    row = pl.BlockSpec((tr, c), lambda i: (i, 0))
    return pl.pallas_call(
        body, name=name, grid=(r // tr,),
        in_specs=[pl.BlockSpec((N_DEV, tr, c), lambda i: (0, i, 0)), row, row, row],
        out_specs=[row] * 4,
        out_shape=[jax.ShapeDtypeStruct((r, c), F32)] * 4,
        compiler_params=_cp("parallel"),
    )(recv, w, mom, var)


_MESH = pl.DeviceIdType.MESH
_HBM = pl.BlockSpec(memory_space=pltpu.HBM)
N_PEER = N_DEV - 1


def _position():
    return lax.axis_index("x"), lax.axis_index("y"), lax.axis_index("c")


def _all_gather(shards, name):
    n = len(shards)

    def body(*refs):
        x_refs, out_refs = refs[:n], refs[n:2 * n]
        send_sems, recv_sems, local_sems = refs[2 * n:]
        x, y, c = _position()
        me, sibling = (x, y, c), (x, y, 1 - c)
        chips = [(1 - x, y), (x, 1 - y), (1 - x, 1 - y)]

        def copy(a, k, block, to, src=None):
            slot = out_refs[a].at[4 * block[0] + 2 * block[1] + block[2]]
            return pltpu.make_async_remote_copy(
                src_ref=slot if src is None else src, dst_ref=slot,
                send_sem=send_sems.at[a * N_PEER + k], recv_sem=recv_sems.at[a * N_PEER + k],
                device_id=to, device_id_type=_MESH)

        mine, sent = [], []
        for a in range(n):
            cp = pltpu.make_async_copy(x_refs[a], out_refs[a].at[4 * x + 2 * y + c], local_sems.at[a])
            cp.start()
            mine.append(cp)
            first = [copy(a, 0, me, sibling, src=x_refs[a])]
            first += [copy(a, 1 + j, me, (*chip, c), src=x_refs[a]) for j, chip in enumerate(chips)]
            for cp in first:
                cp.start()
            sent += first
        for a in range(n):
            for j, chip in enumerate(chips):
                copy(a, 1 + j, (*chip, c), me).wait_recv()
                fwd = copy(a, 4 + j, (*chip, c), sibling)
                fwd.start()
                sent.append(fwd)
        for a in range(n):
            copy(a, 0, sibling, me).wait_recv()
            for j, chip in enumerate(chips):
                copy(a, 4 + j, (*chip, 1 - c), me).wait_recv()
        for cp in sent:
            cp.wait_send()
        for cp in mine:
            cp.wait()

    return pl.pallas_call(
        body, name=name,
        out_shape=[jax.ShapeDtypeStruct((N_DEV,) + a.shape, a.dtype) for a in shards],
        in_specs=[_HBM] * n, out_specs=[_HBM] * n,
        scratch_shapes=[pltpu.SemaphoreType.DMA((n * N_PEER,)), pltpu.SemaphoreType.DMA((n * N_PEER,)),
                        pltpu.SemaphoreType.DMA((n,))],
    )(*shards)


_FLIPS = [(fx, fy, fc) for fx in (0, 1) for fy in (0, 1) for fc in (0, 1)][1:]


def _exchange_copies(in_refs, out_refs, nblk, send_sems, recv_sems, local_sems):
    n = len(in_refs)
    x, y, c = _position()
    me = 4 * x + 2 * y + c

    def peer(f):
        return (1 - x if f[0] else x, 1 - y if f[1] else y, 1 - c if f[2] else c)

    def idx(p):
        return 4 * p[0] + 2 * p[1] + p[2]

    def local(a):
        return pltpu.make_async_copy(in_refs[a].at[me] if a < nblk else in_refs[a], out_refs[a].at[me], local_sems.at[a])

    def remote(a, k, sending):
        p = peer(_FLIPS[k])
        src = in_refs[a].at[idx(p)] if a < nblk else in_refs[a]
        dst = out_refs[a].at[me] if sending else out_refs[a].at[idx(p)]
        return pltpu.make_async_remote_copy(
            src_ref=src, dst_ref=dst, send_sem=send_sems.at[a * N_PEER + k], recv_sem=recv_sems.at[a * N_PEER + k],
            device_id=p, device_id_type=_MESH)

    def start():
        for a in range(n):
            local(a).start()
            for k in range(N_PEER):
                remote(a, k, True).start()

    def wait():
        for a in range(n):
            for k in range(N_PEER):
                remote(a, k, False).wait_recv()
        for a in range(n):
            for k in range(N_PEER):
                remote(a, k, True).wait_send()
            local(a).wait()

    return start, wait


def _exchange_io(blocks, shared):
    arrays = list(blocks) + list(shared)
    n = len(arrays)
    out_shape = [jax.ShapeDtypeStruct(a.shape, a.dtype) for a in blocks]
    out_shape += [jax.ShapeDtypeStruct((N_DEV,) + a.shape, a.dtype) for a in shared]
    sems = [pltpu.SemaphoreType.DMA((n * N_PEER,)), pltpu.SemaphoreType.DMA((n * N_PEER,)), pltpu.SemaphoreType.DMA((n,))]
    return arrays, out_shape, sems


def _exchange(blocks, shared, name):
    arrays, out_shape, sems = _exchange_io(blocks, shared)
    n = len(arrays)

    def body(*refs):
        start, wait = _exchange_copies(refs[:n], refs[n:2 * n], len(blocks), *refs[2 * n:])
        start()
        wait()

    return pl.pallas_call(
        body, name=name, out_shape=out_shape, in_specs=[_HBM] * n, out_specs=[_HBM] * n, scratch_shapes=sems,
    )(*arrays)


def _grid_ends(grid):
    ids = [pl.program_id(i) for i in range(len(grid))]
    first = functools.reduce(jnp.logical_and, [i == 0 for i in ids])
    last = functools.reduce(jnp.logical_and, [i == g - 1 for i, g in zip(ids, grid)])
    return first, last


def _pack(parts, rows):
    flat = jnp.concatenate(parts, axis=-1)
    return jnp.pad(flat, [(0, rows * LANES - flat.shape[-1])]).reshape(rows, LANES)


def _unpack(packed, shapes):
    flat = packed.reshape(-1)
    out, off = [], 0
    for shp in shapes:
        n = int(np.prod(shp))
        out.append(flat[off:off + n].reshape(shp))
        off += n
    return out


def _rows_for(shapes, extra=0):
    n = sum(int(np.prod(s)) for s in shapes) + extra
    return -(-n // (8 * LANES)) * 8


def _lower_bound(logits):
    return jnp.cumsum(jax.nn.softmax(logits.astype(F32), axis=0), axis=0)[0:1]


def _align_axis0(w):
    a, b = 3 * FOX_WIDTH, 3 * FOX_WIDTH + FOX_HEADS
    c = b + 4 * HG_WIDTH
    pad = [(0, LANES - FOX_HEADS)] + [(0, 0)] * (w.ndim - 1)
    return jnp.concatenate([w[c:], w[:a], w[b:c], jnp.pad(w[a:b], pad)], axis=0)


def _unalign_axis0(g):
    a, b = 2 * D_MODEL, 2 * D_MODEL + 3 * FOX_WIDTH
    c = b + 4 * HG_WIDTH
    return jnp.concatenate([g[a:b], g[c:c + FOX_HEADS], g[b:c], g[:a]], axis=0)


TINY_COLS = 768


def _tiny_pack(conv_w_shard, meta_shard):
    cw = jnp.pad(conv_w_shard, ((0, 8 - CONV_WIDTH), (0, TINY_COLS - conv_w_shard.shape[1])))
    mt = jnp.pad(meta_shard, ((0, 0), (0, TINY_COLS - meta_shard.shape[1])))
    return jnp.concatenate([cw, mt], axis=0)


def _tiny_unpack(t, ncw, nmeta):
    return t[..., :CONV_WIDTH, :ncw], t[..., 8:8 + N_META, :nmeta]


def _late_weights(g_up, g_down, g_ab, g_out):
    d = g_up.shape[-1]
    w_a_t = g_ab[:, 0].reshape(-1, g_ab.shape[-1])
    w_b_t = g_ab[:, 1].reshape(-1, g_ab.shape[-1])
    return _ffn_interleave(g_up.reshape(-1, d), 0), g_down.reshape(-1, d), w_a_t, w_b_t, g_out.reshape(-1, d)


def _early_blocks(g_w_up_t, g_w_down, g_w_out, g_w_a_t, g_w_b_t):
    d = g_w_out.shape[-1]
    ab = jnp.stack([g_w_a_t.reshape(N_DEV, -1, g_w_a_t.shape[-1]), g_w_b_t.reshape(N_DEV, -1, g_w_b_t.shape[-1])], axis=1)
    return [_ffn_deinterleave(g_w_up_t, 0).reshape(N_DEV, -1, d).astype(BF16), g_w_down.reshape(N_DEV, -1, d).astype(BF16),
            g_w_out.reshape(N_DEV, -1, d).astype(BF16), ab.astype(BF16)]


def _local_step(x, target, meta, norm1_gain, w_in_t, fox_b_f, q_gain, k_gain, lb, hg_out_gain, w_a_t, w_b_t, w_out,
                norm2_gain, w_up_t, conv_w, conv_b, w_down, ffn_shards=None):
    nb, seq, d = x.shape
    assert seq % SEQ_BLOCK == 0 and N_META < SEQ_BLOCK
    l = seq + N_META
    lp = -(-l // SEQ_BLOCK) * SEQ_BLOCK
    m = nb * lp
    qg = jnp.tile(q_gain, (1, FOX_HEADS))
    kg = jnp.tile(k_gain, (1, FOX_HEADS))
    bf = jnp.pad(fox_b_f, ((0, 0), (0, LANES - FOX_HEADS)))

    h0, xn = _embed_rms(x, meta, norm1_gain, lp, "embed_rms1")
    proj = _matmul(xn, w_in_t, "nt", F32, "proj_in")
    qa, ka, vb = _fox_prep(proj, qg, kg, bf, nb, lp, "fox_prep")
    if ffn_shards is None:
        o_fox, lse = _fox_fwd(qa, ka, vb, nb, lp, "fox_fwd")
    else:
        o_fox, lse, *late = _fox_fwd(qa, ka, vb, nb, lp, "fox_fwd", ride=ffn_shards)
        w_up_t, w_down, w_a_t, w_b_t, w_out = _late_weights(*late)
    o_raw, o_hg, s_save = _hgrn_fwd(proj, lb, hg_out_gain, nb, lp, "hgrn_fwd")
    ya = _matmul(o_hg, w_a_t, "nt", BF16, "branch_a")
    yb = _matmul(o_fox, w_b_t, "nt", BF16, "branch_b")
    merged = _gate_fwd(proj, ya, yb, "gate_fwd")
    h1 = _matmul(merged, w_out, "nn", F32, "mix_out", residual=h0)
    hn = _rms_fwd(h1, norm2_gain, "rms2_fwd")
    up = _matmul(hn, w_up_t, "nt", BF16, "ffn_up")
    act = _conv_fwd(up, conv_w, conv_b, nb, lp, "conv_fwd")
    out = _matmul(act, w_down, "nn", F32, "ffn_down", residual=h1)
    dy, lsum = _loss_head(out, target, lp, "loss_head")
    loss = (0.5 / d) * jnp.sum(lsum)

    dact = _matmul(dy, w_down, "nt", BF16, "d_act")
    g_w_down = _matmul(act, dy, "tn", F32, "g_w_down")
    dup, g_conv_w, g_conv_b = _conv_bwd(up, dact, conv_w, conv_b, nb, lp, "conv_bwd")
    dhn = _matmul(dup, w_up_t, "nn", F32, "d_hn")
    g_w_up_t = _matmul(dup, hn, "tn", F32, "g_w_up")
    dh1, g_norm2 = _rms_bwd(h1, norm2_gain, dhn, dy, "rms2_bwd")

    dmerged = _matmul(dh1, w_out, "nt", BF16, "d_merged")
    g_w_out = _matmul(merged, dh1, "tn", F32, "g_w_out")
    dya, dyb, dgab = _gate_bwd(proj, ya, yb, dmerged, "gate_bwd")
    do_hg = _matmul(dya, w_a_t, "nn", F32, "d_o_hg")
    g_w_a_t = _matmul(dya, o_hg, "tn", F32, "g_w_a")
    do_fox = _matmul(dyb, w_b_t, "nn", BF16, "d_o_fox")
    g_w_b_t = _matmul(dyb, o_fox, "tn", F32, "g_w_b")
    dhq, dhf, dhi, dhg, g_hg_gain, g_lb = _hgrn_bwd(proj, o_raw, s_save, do_hg, lb, hg_out_gain, nb, lp, "hgrn_bwd")
    if ffn_shards is None:
        dqs, dkn, dvv, dc0, dc1 = _fox_bwd(qa, ka, vb, do_fox, o_fox, lse, nb, lp, "fox_bwd")
        early = None
    else:
        dqs, dkn, dvv, dc0, dc1, *early = _fox_bwd(qa, ka, vb, do_fox, o_fox, lse, nb, lp, "fox_bwd",
                                                   ride=_early_blocks(g_w_up_t, g_w_down, g_w_out, g_w_a_t, g_w_b_t))
    dcum = jnp.stack([dc0, dc1], axis=2).reshape(nb, FOX_HEADS, lp)
    dcum = jnp.pad(jnp.transpose(dcum, (0, 2, 1)), ((0, 0), (0, 0), (0, LANES - FOX_HEADS))).reshape(m, LANES)
    dfqkv, dff, g_qg, g_kg, g_bf = _fox_prep_bwd(proj, dqs, dkn, dvv, dcum, qg, kg, bf, nb, lp, "fox_prep_bwd")
    dproj = jnp.concatenate([dgab, dfqkv, dhq, dhf, dhi, dhg, dff], axis=1)
    g_w_in_t = _matmul(dproj, xn, "tn", F32, "g_w_in")
    if ffn_shards is None:
        dxn = _matmul(dproj, w_in_t, "nn", F32, "d_xn")
    else:
        blocks_in = _unalign_axis0(g_w_in_t).reshape(N_DEV, -1, d).astype(BF16)
        dxn, r_in = _matmul(dproj, w_in_t, "nn", F32, "d_xn", ride=[blocks_in])
        early = early + [r_in]
    dh0, g_norm1 = _rms_bwd(h0, norm1_gain, dxn, dh1, "rms1_bwd")

    dh0 = dh0.reshape(nb, lp, d)
    grad_x = dh0[:, N_META:l]
    g_meta = jnp.sum(dh0[:, :N_META], axis=0)
    g_q_gain = jnp.sum(g_qg.reshape(FOX_HEADS, FOX_HEAD_DIM), axis=0, keepdims=True)
    g_k_gain = jnp.sum(g_kg.reshape(FOX_HEADS, FOX_HEAD_DIM), axis=0, keepdims=True)
    grads = dict(meta_tokens=g_meta, norm1_gain=g_norm1, w_in_t=g_w_in_t, fox_b_f=g_bf[:, :FOX_HEADS],
                 q_norm_gain=g_q_gain, k_norm_gain=g_k_gain, lb=g_lb, hg_out_gain=g_hg_gain,
                 w_a_t=g_w_a_t, w_b_t=g_w_b_t, w_out=g_w_out, norm2_gain=g_norm2, w_up_t=g_w_up_t,
                 conv_w=g_conv_w, conv_b=g_conv_b, w_down=g_w_down, early=early)
    return loss, grad_x, grads


SMALL = ("norm1_gain", "fox_b_f", "q_norm_gain", "k_norm_gain", "hg_lb_logits", "hg_out_gain", "norm2_gain", "conv_b")
ORDER = ("meta_tokens", "norm1_gain", "w_in", "fox_b_f", "q_norm_gain", "k_norm_gain", "hg_lb_logits", "hg_out_gain",
         "w_branch_a", "w_branch_b", "w_out", "norm2_gain", "w_up", "conv_w", "conv_b", "w_down")


def kernel(x, meta_tokens, norm1_gain, w_in, fox_b_f, q_norm_gain, k_norm_gain, hg_lb_logits, hg_out_gain, w_branch_a, w_branch_b, w_out, norm2_gain, w_up, conv_w, conv_b, w_down, loss_target, m_meta_tokens, m_norm1_gain, m_w_in, m_fox_b_f, m_q_norm_gain, m_k_norm_gain, m_hg_lb_logits, m_hg_out_gain, m_w_branch_a, m_w_branch_b, m_w_out, m_norm2_gain, m_w_up, m_conv_w, m_conv_b, m_w_down, v_meta_tokens, v_norm1_gain, v_w_in, v_fox_b_f, v_q_norm_gain, v_k_norm_gain, v_hg_lb_logits, v_hg_out_gain, v_w_branch_a, v_w_branch_b, v_w_out, v_norm2_gain, v_w_up, v_conv_w, v_conv_b, v_w_down):
    w = dict(meta_tokens=meta_tokens, norm1_gain=norm1_gain, w_in=w_in, fox_b_f=fox_b_f, q_norm_gain=q_norm_gain,
             k_norm_gain=k_norm_gain, hg_lb_logits=hg_lb_logits, hg_out_gain=hg_out_gain, w_branch_a=w_branch_a,
             w_branch_b=w_branch_b, w_out=w_out, norm2_gain=norm2_gain, w_up=w_up, conv_w=conv_w, conv_b=conv_b,
             w_down=w_down)
    mom = dict(meta_tokens=m_meta_tokens, norm1_gain=m_norm1_gain, w_in=m_w_in, fox_b_f=m_fox_b_f,
               q_norm_gain=m_q_norm_gain, k_norm_gain=m_k_norm_gain, hg_lb_logits=m_hg_lb_logits,
               hg_out_gain=m_hg_out_gain, w_branch_a=m_w_branch_a, w_branch_b=m_w_branch_b, w_out=m_w_out,
               norm2_gain=m_norm2_gain, w_up=m_w_up, conv_w=m_conv_w, conv_b=m_conv_b, w_down=m_w_down)
    var = dict(meta_tokens=v_meta_tokens, norm1_gain=v_norm1_gain, w_in=v_w_in, fox_b_f=v_fox_b_f,
               q_norm_gain=v_q_norm_gain, k_norm_gain=v_k_norm_gain, hg_lb_logits=v_hg_lb_logits,
               hg_out_gain=v_hg_out_gain, w_branch_a=v_w_branch_a, w_branch_b=v_w_branch_b, w_out=v_w_out,
               norm2_gain=v_norm2_gain, w_up=v_w_up, conv_w=v_conv_w, conv_b=v_conv_b, w_down=v_w_down)
    d = D_MODEL
    n_in, n_up = w_in.shape[2], w_up.shape[2]
    n_ab, n_meta = w_branch_a.shape[2], meta_tokens.shape[1]

    g_in, g_tiny = _all_gather([w_in[0].T.astype(BF16), _tiny_pack(conv_w[0], meta_tokens)], "gather_weights")
    w_in_t = _align_axis0(g_in.reshape(N_DEV * n_in, d))
    cw_slots, meta_slots = _tiny_unpack(g_tiny, n_up, n_meta)
    conv_w_f = _ffn_interleave(jnp.transpose(cw_slots, (1, 0, 2)).reshape(CONV_WIDTH, -1), 1)
    meta_f = jnp.transpose(meta_slots, (1, 0, 2)).reshape(N_META, -1)
    conv_b_i = _ffn_interleave(conv_b, 1)

    lb, lb_vjp = jax.vjp(_lower_bound, hg_lb_logits)
    loss, grad_x, g = _local_step(
        x, loss_target, meta_f, norm1_gain, w_in_t, fox_b_f, q_norm_gain, k_norm_gain, lb, hg_out_gain,
        None, None, None, norm2_gain, None, conv_w_f, conv_b_i, None,
        ffn_shards=(w_up[0].T.astype(BF16), w_down[0].astype(BF16),
                    jnp.stack([w_branch_a[0].T, w_branch_b[0].T]).astype(BF16), w_out[0].astype(BF16)))

    g["hg_lb_logits"] = lb_vjp(g.pop("lb"))[0]
    g["conv_b"] = _ffn_deinterleave(g["conv_b"], 1)
    gcw = _ffn_deinterleave(g["conv_w"], 1).reshape(CONV_WIDTH, N_DEV, n_up)
    gmeta = g["meta_tokens"].reshape(N_META, N_DEV, n_meta)
    tiny = jnp.concatenate([
        jnp.pad(jnp.transpose(gcw, (1, 0, 2)), ((0, 0), (0, 8 - CONV_WIDTH), (0, TINY_COLS - n_up))),
        jnp.pad(jnp.transpose(gmeta, (1, 0, 2)), ((0, 0), (0, 0), (0, TINY_COLS - n_meta)))], axis=1)
    small_shapes = [w[n].shape for n in SMALL]
    rows_sm = _rows_for(small_shapes, extra=1)
    small = _pack([g[n].reshape(-1) for n in SMALL] + [loss.reshape(1)], rows_sm)
    r_tiny, r_small = _exchange([tiny], [small], "exchange_grads")
    r_up, r_down, r_out, r_ab, r_in = g["early"]

    res = {}
    g_in_s = _slot_sum(r_in, "sum_w_in").T
    res["w_in"] = (g_in_s,) + tuple(_adamw(g_in_s, w_in[0], m_w_in[0], v_w_in[0], "adamw_w_in"))
    g_up_s = _slot_sum(r_up, "sum_w_up").T
    res["w_up"] = (g_up_s,) + tuple(_adamw(g_up_s, w_up[0], m_w_up[0], v_w_up[0], "adamw_w_up"))
    g_ab_s = jnp.swapaxes(_slot_sum(r_ab.reshape(N_DEV, 2 * n_ab, -1), "sum_w_ab").reshape(2, n_ab, -1), 1, 2)
    ab = lambda t: jnp.concatenate([t["w_branch_a"][0], t["w_branch_b"][0]], axis=0)
    o_ab = (g_ab_s.reshape(-1, n_ab),) + tuple(_adamw(g_ab_s.reshape(-1, n_ab), ab(w), ab(mom), ab(var), "adamw_w_ab"))
    half = o_ab[0].shape[0] // 2
    res["w_branch_a"] = tuple(o[:half] for o in o_ab)
    res["w_branch_b"] = tuple(o[half:] for o in o_ab)
    res["w_out"] = tuple(_sum_adamw(r_out, w_out[0], m_w_out[0], v_w_out[0], "adamw_w_out"))
    res["w_down"] = tuple(_sum_adamw(r_down, w_down[0], m_w_down[0], v_w_down[0], "adamw_w_down"))
    tp = lambda t: _tiny_pack(t["conv_w"][0], t["meta_tokens"])
    o_tiny = [_tiny_unpack(o, n_up, n_meta) for o in _sum_adamw(r_tiny, tp(w), tp(mom), tp(var), "adamw_tiny")]
    res["conv_w"] = tuple(o[0] for o in o_tiny)
    res["meta_tokens"] = tuple(o[1] for o in o_tiny)
    zero1 = jnp.zeros((1,), F32)
    sp = lambda t: _pack([t[n].reshape(-1) for n in SMALL] + [zero1], rows_sm)
    o_small = [_unpack(o, small_shapes + [(1,)]) for o in _sum_adamw(r_small, sp(w), sp(mom), sp(var), "adamw_small")]
    for i, n in enumerate(SMALL):
        res[n] = tuple(o[i] for o in o_small)
    loss_all = o_small[0][len(SMALL)].reshape(())

    result = [[res[n][k].reshape(w[n].shape) for n in ORDER] for k in range(4)]
    return (loss_all, grad_x, *result[0], *result[1], *result[2], *result[3])
```

```python
import functools

import jax
import jax.numpy as jnp
import numpy as np
from jax import lax
from jax.experimental import pallas as pl
from jax.experimental.pallas import tpu as pltpu

F32 = jnp.float32
BF16 = jnp.bfloat16

D_MODEL = 1024
N_META = 16
FOX_HEADS = 8
FOX_HEAD_DIM = 64
FOX_WIDTH = FOX_HEADS * FOX_HEAD_DIM
HG_HEADS = 4
HG_DIM = 128
HG_WIDTH = HG_HEADS * HG_DIM
D_FF = 2816
CONV_WIDTH = 3
EPS = 1e-6
IN_COLS = 3 * FOX_WIDTH + FOX_HEADS + 4 * HG_WIDTH + 2 * D_MODEL
N_DEV = 8

ADAM_LR = 0.001
ADAM_B1 = 0.9
ADAM_B2 = 0.999
ADAM_EPS = 1e-08
ADAM_WD = 0.01
ADAM_STEP = 10

LANES = 128
SEQ_BLOCK = 128
SUB = 16
NEG = -1e30
VMEM_LIMIT = 48 * 1024 * 1024

FOX_CB = 2 * D_MODEL // FOX_WIDTH
CB_HQ = (2 * D_MODEL + 3 * FOX_WIDTH) // LANES
CB_HF = CB_HQ + HG_HEADS
CB_HI = CB_HF + HG_HEADS
CB_HG = CB_HI + HG_HEADS
CB_FF = CB_HG + HG_HEADS


def _div_tile(n, target, mult):
    best = None
    for t in range(mult, min(n, target) + 1, mult):
        if n % t == 0:
            best = t
    if best is None:
        best = n
    return best


def _cp(*sem):
    return pltpu.CompilerParams(dimension_semantics=sem, vmem_limit_bytes=VMEM_LIMIT)


def _sigmoid(x):
    return 0.5 * jnp.tanh(0.5 * x) + 0.5


def _dot(a, b, dims, precision=None):
    return lax.dot_general(a, b, (dims, ((), ())), preferred_element_type=F32, precision=precision)


NN = ((1,), (0,))
NT = ((1,), (1,))
TN = ((0,), (0,))
HI = lax.Precision.HIGHEST


MATMUL_VMEM_BUDGET = 30 * 1024 * 1024
MATMUL_MAX_TILE = 2048


def _tile_options(n):
    return [t for t in range(LANES, min(n, MATMUL_MAX_TILE) + 1, LANES) if n % t == 0] or [n]


def _matmul_tiles(m, n, k, a_bytes, b_bytes, o_bytes, has_res):
    tk = _div_tile(k, MATMUL_MAX_TILE, LANES)
    best = None
    for tm in _tile_options(m):
        for tn in _tile_options(n):
            vmem = 2 * (tm * tk * a_bytes + tk * tn * b_bytes) + 2 * tm * tn * o_bytes
            vmem += tm * tn * 4 if (tk < k and o_bytes != 4) else 0
            vmem += 2 * tm * tn * 4 if has_res else 0
            if vmem > MATMUL_VMEM_BUDGET:
                continue
            key = (tm * tn, tn % 256 == 0, tn)
            if best is None or key > best[0]:
                best = (key, tm, tn)
    assert best is not None, (m, n, k)
    return best[1], best[2], tk


def _matmul(a, b, mode, out_dtype, name, residual=None, ride=()):
    if mode == "nn":
        (m, k), (k2, n) = a.shape, b.shape
    elif mode == "nt":
        (m, k), (n, k2) = a.shape, b.shape
    else:
        (k, m), (k2, n) = a.shape, b.shape
    assert k == k2, (a.shape, b.shape, mode)
    has_res = residual is not None
    tm, tn, tk = _matmul_tiles(m, n, k, a.dtype.itemsize, b.dtype.itemsize, jnp.dtype(out_dtype).itemsize, has_res)
    nk = k // tk
    in_place = jnp.dtype(out_dtype) == jnp.dtype(F32)
    if mode == "nn":
        a_spec = pl.BlockSpec((tm, tk), lambda i, j, kk: (i, kk))
        b_spec = pl.BlockSpec((tk, tn), lambda i, j, kk: (kk, j))
        dims = NN
    elif mode == "nt":
        a_spec = pl.BlockSpec((tm, tk), lambda i, j, kk: (i, kk))
        b_spec = pl.BlockSpec((tn, tk), lambda i, j, kk: (j, kk))
        dims = NT
    else:
        a_spec = pl.BlockSpec((tk, tm), lambda i, j, kk: (kk, i))
        b_spec = pl.BlockSpec((tk, tn), lambda i, j, kk: (kk, j))
        dims = TN
    o_spec = pl.BlockSpec((tm, tn), lambda i, j, kk: (i, j))
    grid = (m // tm, n // tn, nk)
    x_arrays, x_shapes, x_sems = _exchange_io(ride, ())
    nx = len(x_arrays)
    n_in = 3 if has_res else 2

    def body(*refs):
        if nx:
            first, last = _grid_ends(grid)
            x_in, x_out = refs[n_in:n_in + nx], refs[n_in + nx + 1:n_in + 2 * nx + 1]
            start, wait = _exchange_copies(x_in, x_out, nx, *refs[n_in + 2 * nx + 1:n_in + 2 * nx + 4])
            pl.when(first)(start)
        compute(*refs)
        if nx:
            pl.when(last)(wait)

    def compute(*refs):
        a_ref, b_ref = refs[0], refs[1]
        r_ref = refs[2] if has_res else None
        o_ref = refs[n_in + nx]
        if nk == 1:
            part = _dot(a_ref[...].astype(BF16), b_ref[...].astype(BF16), dims)
            o_ref[...] = (part + r_ref[...] if has_res else part).astype(o_ref.dtype)
            return
        acc_ref = o_ref if in_place else refs[-1]
        kk = pl.program_id(2)

        @pl.when(kk == 0)
        def _():
            acc_ref[...] = r_ref[...] if (has_res and in_place) else jnp.zeros_like(acc_ref)

        acc_ref[...] += _dot(a_ref[...].astype(BF16), b_ref[...].astype(BF16), dims)

        if not in_place:
            @pl.when(kk == nk - 1)
            def _():
                acc = acc_ref[...]
                if has_res:
                    acc = acc + r_ref[...]
                o_ref[...] = acc.astype(o_ref.dtype)

    in_specs = [a_spec, b_spec] + ([o_spec] if has_res else [])
    args = (a, b) + ((residual,) if has_res else ())
    out_shape = jax.ShapeDtypeStruct((m, n), out_dtype)
    acc = [pltpu.VMEM((tm, tn), F32)] if (nk > 1 and not in_place) else []
    if not nx:
        return pl.pallas_call(
            body, name=name, grid=grid, in_specs=in_specs, out_specs=o_spec, out_shape=out_shape, scratch_shapes=acc,
            compiler_params=_cp("parallel", "parallel", "arbitrary"),
        )(*args)
    return pl.pallas_call(
        body, name=name, grid=grid, in_specs=in_specs + [_HBM] * nx, out_specs=[o_spec] + [_HBM] * nx,
        out_shape=[out_shape] + x_shapes, scratch_shapes=x_sems + acc,
        compiler_params=_cp("arbitrary", "arbitrary", "arbitrary"),
    )(*args, *x_arrays)


def _matmul_res_rms(a, b, residual, gain, name):
    (m, k), (_, n) = a.shape, b.shape
    tm = _div_tile(m, 768, LANES)

    def body(a_ref, b_ref, r_ref, g_ref, h_ref, o_ref):
        h = _dot(a_ref[...].astype(BF16), b_ref[...].astype(BF16), NN) + r_ref[...]
        h_ref[...] = h
        r = lax.rsqrt(jnp.mean(h * h, axis=-1, keepdims=True) + EPS)
        o_ref[...] = ((h * r) * g_ref[...]).astype(o_ref.dtype)

    row = pl.BlockSpec((tm, n), lambda i: (i, 0))
    return pl.pallas_call(
        body, name=name, grid=(m // tm,),
        in_specs=[pl.BlockSpec((tm, k), lambda i: (i, 0)), pl.BlockSpec((k, n), lambda i: (0, 0)), row,
                  pl.BlockSpec((1, n), lambda i: (0, 0))],
        out_specs=[row, row],
        out_shape=[jax.ShapeDtypeStruct((m, n), F32), jax.ShapeDtypeStruct((m, n), BF16)],
        compiler_params=_cp("parallel"),
    )(a, b, residual, gain)


def _rms_bwd(x, gain, dy, dres, name):
    m, d = x.shape
    tm = _div_tile(m, 768, 8)

    def body(x_ref, g_ref, dy_ref, dr_ref, dx_ref, dg_ref):
        xv = x_ref[...]
        r = lax.rsqrt(jnp.mean(xv * xv, axis=-1, keepdims=True) + EPS)
        nv = xv * r
        dyv = dy_ref[...]
        gdy = dyv * g_ref[...]
        dx_ref[...] = dr_ref[...] + r * (gdy - nv * jnp.mean(gdy * nv, axis=-1, keepdims=True))
        part = jnp.sum(dyv * nv, axis=0, keepdims=True)

        @pl.when(pl.program_id(0) == 0)
        def _():
            dg_ref[...] = part

        @pl.when(pl.program_id(0) > 0)
        def _():
            dg_ref[...] += part

    row = pl.BlockSpec((tm, d), lambda i: (i, 0))
    vec = pl.BlockSpec((1, d), lambda i: (0, 0))
    return pl.pallas_call(
        body, name=name, grid=(m // tm,),
        in_specs=[row, vec, row, row], out_specs=[row, vec],
        out_shape=[jax.ShapeDtypeStruct((m, d), F32), jax.ShapeDtypeStruct((1, d), F32)],
        compiler_params=_cp("arbitrary"),
    )(x, gain, dy, dres)


def _head_stats(xv, lo):
    sq = xv * xv
    s_lo = jnp.sum(jnp.where(lo, sq, 0.0), axis=1, keepdims=True)
    s_hi = jnp.sum(jnp.where(lo, 0.0, sq), axis=1, keepdims=True)
    return jnp.where(lo, s_lo, s_hi) * (1.0 / FOX_HEAD_DIM)


BIAS_LANE = FOX_HEAD_DIM
N_SPLIT = 3


def _split3(c):
    c1 = c.astype(BF16).astype(F32)
    r1 = c - c1
    c2 = r1.astype(BF16).astype(F32)
    c3 = (r1 - c2).astype(BF16).astype(F32)
    return c1, c2, c3


def _fox_prep(proj, qg, kg, bf, nb, lp, name):
    m = proj.shape[0]
    ts = SEQ_BLOCK
    tg = _hg_rows(lp)
    nblk = lp // tg
    scale = FOX_HEAD_DIM ** -0.5

    def body(q_ref, k_ref, v_ref, f_ref, qg_ref, kg_ref, bf_ref, qo_ref, ko_ref, vo_ref, carry_ref):
        @pl.when(pl.program_id(1) == 0)
        def _():
            carry_ref[...] = jnp.zeros_like(carry_ref)

        for hb in range(tg // ts):
            rows = pl.ds(hb * ts, ts)
            block(q_ref.at[rows], k_ref.at[rows], v_ref.at[rows], f_ref.at[rows], qg_ref, kg_ref, bf_ref,
                  qo_ref.at[rows], ko_ref.at[rows], vo_ref.at[rows], carry_ref)

    def block(q_ref, k_ref, v_ref, f_ref, qg_ref, kg_ref, bf_ref, qo_ref, ko_ref, vo_ref, carry_ref):
        lane = lax.broadcasted_iota(jnp.int32, (1, LANES), 1)
        lo = lane < FOX_HEAD_DIM
        z = f_ref[...] + bf_ref[...]
        logf = jnp.minimum(z, 0.0) - jnp.log(1.0 + jnp.exp(-jnp.abs(z)))
        logf = jnp.where(lane < FOX_HEADS, logf, 0.0)
        r = lax.broadcasted_iota(jnp.int32, (ts, ts), 0)
        c = lax.broadcasted_iota(jnp.int32, (ts, ts), 1)
        tri = jnp.where(c <= r, 1.0, 0.0).astype(F32)
        cum = _dot(tri, logf, NN, HI) + carry_ref[...]
        carry_ref[...] = cum[ts - 1:ts, :]

        ones = jnp.where((lane >= BIAS_LANE + N_SPLIT) & (lane < BIAS_LANE + 2 * N_SPLIT), 1.0, 0.0)
        ones_k = jnp.where((lane >= BIAS_LANE) & (lane < BIAS_LANE + N_SPLIT), 1.0, 0.0)
        for j in range(FOX_WIDTH // LANES):
            cs = slice(j * LANES, (j + 1) * LANES)
            xq = q_ref[:, cs]
            yq = ((xq * lax.rsqrt(_head_stats(xq, lo) + EPS)) * qg_ref[:, cs]) * scale
            xk = k_ref[:, cs]
            yk = (xk * lax.rsqrt(_head_stats(xk, lo) + EPS)) * kg_ref[:, cs]
            for hh in range(2):
                h = 2 * j + hh
                pieces = _split3(_lane_pick(cum, lane, h))
                qb, kb = ones, ones_k
                for i, piece in enumerate(pieces):
                    qb = jnp.where(lane == BIAS_LANE + i, piece, qb)
                    kb = jnp.where(lane == BIAS_LANE + N_SPLIT + i, -piece, kb)
                yq_h = yq if hh == 0 else pltpu.roll(yq, FOX_HEAD_DIM, 1)
                yk_h = yk if hh == 0 else pltpu.roll(yk, FOX_HEAD_DIM, 1)
                hs = slice(h * LANES, (h + 1) * LANES)
                qo_ref[:, hs] = jnp.where(lo, yq_h, qb).astype(BF16)
                ko_ref[:, hs] = jnp.where(lo, yk_h, kb).astype(BF16)
        vo_ref[...] = v_ref[...].astype(BF16)

    w = FOX_WIDTH
    row = lambda b, i: (b * nblk + i, 0)
    return pl.pallas_call(
        body, name=name, grid=(nb, nblk),
        in_specs=[pl.BlockSpec((tg, w), lambda b, i: (b * nblk + i, FOX_CB)),
                  pl.BlockSpec((tg, w), lambda b, i: (b * nblk + i, FOX_CB + 1)),
                  pl.BlockSpec((tg, w), lambda b, i: (b * nblk + i, FOX_CB + 2)),
                  pl.BlockSpec((tg, LANES), lambda b, i: (b * nblk + i, CB_FF)),
                  pl.BlockSpec((1, w), lambda b, i: (0, 0)),
                  pl.BlockSpec((1, w), lambda b, i: (0, 0)),
                  pl.BlockSpec((1, LANES), lambda b, i: (0, 0))],
        out_specs=[pl.BlockSpec((tg, 2 * w), row), pl.BlockSpec((tg, 2 * w), row), pl.BlockSpec((tg, w), row)],
        out_shape=[jax.ShapeDtypeStruct((m, 2 * w), BF16)] * 2 + [jax.ShapeDtypeStruct((m, w), BF16)],
        scratch_shapes=[pltpu.VMEM((1, LANES), F32)],
        compiler_params=_cp("arbitrary", "arbitrary"),
    )(proj, proj, proj, proj, qg, kg, bf)


def _att_tile(lp):
    return 384 if (lp % 384 == 0 and lp > 384) else 128


def _lane_pick(blk, lane, idx):
    return jnp.sum(jnp.where(lane == idx, blk, 0.0), axis=1, keepdims=True)


def _head_masks():
    lane = lax.broadcasted_iota(jnp.int32, (1, LANES), 1)
    return lane, [(lane >= hh * FOX_HEAD_DIM) & (lane < (hh + 1) * FOX_HEAD_DIM) for hh in range(2)]


def _fox_fwd(qa, ka, vb, nb, lp, name, ride=()):
    m = qa.shape[0]
    tq = _att_tile(lp)
    nq = lp // tq
    npair = FOX_WIDTH // LANES
    grid = (nb, npair, nq)
    r_arrays, r_shapes, r_sems = _exchange_io((), ride)
    nr = len(r_arrays)

    def body(q_ref, k_ref, v_ref, *rest):
        r_in, (o_ref, lse_ref), r_out, sems = rest[:nr], rest[nr:nr + 2], rest[nr + 2:2 * nr + 2], rest[2 * nr + 2:]
        if nr:
            first, last = _grid_ends(grid)
            start, wait = _exchange_copies(r_in, r_out, 0, *sems)
            pl.when(first)(start)
        qi = pl.program_id(2)
        lane, hmasks = _head_masks()
        zero16 = jnp.zeros((), BF16)
        causal = lax.broadcasted_iota(jnp.int32, (tq, 1), 0) >= lax.broadcasted_iota(jnp.int32, (1, tq), 1)
        qs = [q_ref[:, hh * LANES:(hh + 1) * LANES] for hh in range(2)]

        def tile(j, carry, diagonal):
            k0 = pl.multiple_of(j * tq, tq)
            vb = v_ref[pl.ds(k0, tq), :]
            out = []
            for hh in range(2):
                mx, l, acc = carry[3 * hh:3 * hh + 3]
                vz = jnp.where(hmasks[hh], vb, zero16)
                s = _dot(qs[hh], k_ref[pl.ds(k0, tq), hh * LANES:(hh + 1) * LANES], NT)
                if diagonal:
                    s = jnp.where(causal, s, NEG)
                m_new = jnp.maximum(mx, jnp.max(s, axis=1, keepdims=True))
                alpha = jnp.exp(mx - m_new)
                pe = jnp.exp(s - m_new)
                l = alpha * l + jnp.sum(pe, axis=1, keepdims=True)
                acc = alpha * acc + _dot(pe.astype(BF16), vz, NN)
                out += [m_new, l, acc]
            return tuple(out)

        init = (jnp.full((tq, 1), NEG, F32), jnp.zeros((tq, 1), F32), jnp.zeros((tq, LANES), F32)) * 2
        carry = lax.fori_loop(0, qi, lambda j, c: tile(j, c, False), init)
        m0, l0, acc0, m1, l1, acc1 = tile(qi, carry, True)
        o_ref[...] = acc0 / l0 + acc1 / l1
        lse_ref[...] = jnp.where(lane == 0, m0 + jnp.log(l0), jnp.where(lane == 1, m1 + jnp.log(l1), 0.0))
        if nr:
            pl.when(last)(wait)

    return pl.pallas_call(
        body, name=name, grid=grid,
        in_specs=[pl.BlockSpec((tq, 2 * LANES), lambda b, p, i: (b * nq + i, p)),
                  pl.BlockSpec((lp, 2 * LANES), lambda b, p, i: (b, p)),
                  pl.BlockSpec((lp, LANES), lambda b, p, i: (b, p))] + [_HBM] * nr,
        out_specs=[pl.BlockSpec((tq, LANES), lambda b, p, i: (b * nq + i, p)),
                   pl.BlockSpec((None, None, tq, LANES), lambda b, p, i: (b, p, i, 0))] + [_HBM] * nr,
        out_shape=[jax.ShapeDtypeStruct((m, FOX_WIDTH), F32),
                   jax.ShapeDtypeStruct((nb, npair, lp, LANES), F32)] + r_shapes,
        scratch_shapes=r_sems if nr else [],
        compiler_params=_cp(*(["arbitrary"] * 3 if nr else ["parallel", "parallel", "arbitrary"])),
    )(qa, ka, vb, *r_arrays)


def _fox_bwd(qa, ka, vb, do, o, lse, nb, lp, name, ride=()):
    m = qa.shape[0]
    tq = _att_tile(lp)
    nq = lp // tq
    npair = FOX_WIDTH // LANES
    grid = (nb, npair, nq)
    r_arrays, r_shapes, r_sems = _exchange_io(ride, ())
    nr = len(r_arrays)

    def body(k_ref, v_ref, q_ref, do_ref, o_ref, lse_ref, *rest):
        r_in, r_out, sems = rest[:nr], rest[nr + 5:2 * nr + 5], rest[2 * nr + 5:]
        dq_ref, dk_ref, dv_ref, dc0_ref, dc1_ref = rest[nr:nr + 5]
        if nr:
            first, last = _grid_ends(grid)
            start, wait = _exchange_copies(r_in, r_out, nr, *sems)
            pl.when(first)(start)
        j = pl.program_id(2)
        lane, hmasks = _head_masks()
        zero16 = jnp.zeros((), BF16)
        causal = lax.broadcasted_iota(jnp.int32, (tq, 1), 0) >= lax.broadcasted_iota(jnp.int32, (1, tq), 1)

        @pl.when(j == 0)
        def _():
            dq_ref[...] = jnp.zeros_like(dq_ref)

        vv = v_ref[...]
        vzs = [jnp.where(hm, vv, zero16) for hm in hmasks]

        def tile(qi, carry, diagonal):
            dk0, dk1, dv, dc0, dc1 = carry
            q0 = pl.multiple_of(qi * tq, tq)
            dob16 = do_ref[pl.ds(q0, tq), :].astype(BF16)
            ob = o_ref[pl.ds(q0, tq), :]
            lseb = lse_ref[pl.ds(q0, tq), :]
            dks, dcs = [dk0, dk1], [dc0, dc1]
            for hh in range(2):
                hs = slice(hh * LANES, (hh + 1) * LANES)
                q = q_ref[pl.ds(q0, tq), hs]
                doz16 = jnp.where(hmasks[hh], dob16, zero16)
                delta = jnp.sum(doz16.astype(F32) * ob, axis=1, keepdims=True)
                s = _dot(q, k_ref[:, hs], NT) - _lane_pick(lseb, lane, hh)
                if diagonal:
                    s = jnp.where(causal, s, NEG)
                pm = jnp.exp(s)
                ds = pm * (_dot(doz16, vzs[hh], NT) - delta)
                ds16 = ds.astype(BF16)
                dv = dv + _dot(pm.astype(BF16), doz16, TN)
                dks[hh] = dks[hh] + _dot(ds16, q, TN)
                dq_ref[pl.ds(q0, tq), hs] += _dot(ds16, k_ref[:, hs], NN)
                dcs[hh] = dcs[hh] - jnp.sum(ds, axis=0, keepdims=True)
            return dks[0], dks[1], dv, dcs[0], dcs[1]

        zt = jnp.zeros((tq, LANES), F32)
        zr = jnp.zeros((1, tq), F32)
        carry = tile(j, (zt, zt, zt, zr, zr), True)
        dk0, dk1, dv, dc0, dc1 = lax.fori_loop(j + 1, nq, lambda qi, c: tile(qi, c, False), carry)
        dk_ref[:, :LANES] = dk0
        dk_ref[:, LANES:] = dk1
        dv_ref[...] = dv
        dc0_ref[...] = dc0
        dc1_ref[...] = dc1
        if nr:
            pl.when(last)(wait)

    full2 = pl.BlockSpec((lp, 2 * LANES), lambda b, p, j: (b, p))
    full = pl.BlockSpec((lp, LANES), lambda b, p, j: (b, p))
    blk2 = pl.BlockSpec((tq, 2 * LANES), lambda b, p, j: (b * nq + j, p))
    blk = pl.BlockSpec((tq, LANES), lambda b, p, j: (b * nq + j, p))
    dcs = pl.BlockSpec((None, None, 1, tq), lambda b, p, j: (b, p, 0, j))
    return pl.pallas_call(
        body, name=name, grid=grid,
        in_specs=[blk2, blk, full2, full, full,
                  pl.BlockSpec((None, None, lp, LANES), lambda b, p, j: (b, p, 0, 0))] + [_HBM] * nr,
        out_specs=[full2, blk2, blk, dcs, dcs] + [_HBM] * nr,
        out_shape=[jax.ShapeDtypeStruct((m, 2 * FOX_WIDTH), F32)] * 2 + [jax.ShapeDtypeStruct((m, FOX_WIDTH), F32)]
        + [jax.ShapeDtypeStruct((nb, npair, 1, lp), F32)] * 2 + r_shapes,
        scratch_shapes=r_sems if nr else [],
        compiler_params=_cp(*(["arbitrary"] * 3 if nr else ["parallel", "parallel", "arbitrary"])),
    )(ka, vb, qa, do, o, lse, *r_arrays)


def _fox_prep_bwd(proj, dqa, dka, dv, dcum, qg, kg, bf, nb, lp, name):
    m = proj.shape[0]
    ts = SEQ_BLOCK
    tg = _hg_rows(lp)
    nblk = lp // tg
    scale = FOX_HEAD_DIM ** -0.5
    w = FOX_WIDTH
    wo = 3 * w

    def body(q_ref, k_ref, f_ref, dq_ref, dk_ref, dv_ref, dc_ref, qg_ref, kg_ref, bf_ref,
             out_ref, dff_ref, dqg_ref, dkg_ref, dbf_ref, carry_ref):
        @pl.when((pl.program_id(0) == 0) & (pl.program_id(1) == 0))
        def _():
            dqg_ref[...] = jnp.zeros_like(dqg_ref)
            dkg_ref[...] = jnp.zeros_like(dkg_ref)
            dbf_ref[...] = jnp.zeros_like(dbf_ref)

        @pl.when(pl.program_id(1) == 0)
        def _():
            carry_ref[...] = jnp.zeros_like(carry_ref)

        for hb in reversed(range(tg // ts)):
            rows = pl.ds(hb * ts, ts)
            block(q_ref.at[rows], k_ref.at[rows], f_ref.at[rows], dq_ref.at[rows], dk_ref.at[rows], dv_ref.at[rows],
                  dc_ref.at[rows], qg_ref, kg_ref, bf_ref, out_ref.at[rows], dff_ref.at[rows], dqg_ref, dkg_ref, dbf_ref,
                  carry_ref)

    def block(q_ref, k_ref, f_ref, dq_ref, dk_ref, dv_ref, dc_ref, qg_ref, kg_ref, bf_ref,
              out_ref, dff_ref, dqg_ref, dkg_ref, dbf_ref, carry_ref):
        lane = lax.broadcasted_iota(jnp.int32, (1, LANES), 1)
        lo = lane < FOX_HEAD_DIM

        def norm_bwd(x, g, dy):
            r = lax.rsqrt(_head_stats(x, lo) + EPS)
            nv = x * r
            gdy = dy * g
            prod = gdy * nv
            s_lo = jnp.sum(jnp.where(lo, prod, 0.0), axis=1, keepdims=True)
            s_hi = jnp.sum(jnp.where(lo, 0.0, prod), axis=1, keepdims=True)
            mean = jnp.where(lo, s_lo, s_hi) * (1.0 / FOX_HEAD_DIM)
            return r * (gdy - nv * mean), jnp.sum(dy * nv, axis=0, keepdims=True)

        def pair(d_ref, jj):
            even = d_ref[:, 2 * jj * LANES:(2 * jj + 1) * LANES]
            odd = d_ref[:, (2 * jj + 1) * LANES:(2 * jj + 2) * LANES]
            return jnp.where(lo, even, pltpu.roll(odd, FOX_HEAD_DIM, 1))

        for jj in range(w // LANES):
            cs = slice(jj * LANES, (jj + 1) * LANES)
            dx, dg = norm_bwd(q_ref[:, cs], qg_ref[:, cs], pair(dq_ref, jj) * scale)
            out_ref[:, cs] = dx.astype(BF16)
            dqg_ref[:, cs] += dg
            dx, dg = norm_bwd(k_ref[:, cs], kg_ref[:, cs], pair(dk_ref, jj))
            out_ref[:, w + jj * LANES:w + (jj + 1) * LANES] = dx.astype(BF16)
            dkg_ref[:, cs] += dg
        out_ref[:, 2 * w:3 * w] = dv_ref[...].astype(BF16)

        dc = dc_ref[...]
        r = lax.broadcasted_iota(jnp.int32, (ts, ts), 0)
        c = lax.broadcasted_iota(jnp.int32, (ts, ts), 1)
        triu = jnp.where(c >= r, 1.0, 0.0).astype(F32)
        dlogf = _dot(triu, dc, NN, HI) + carry_ref[...]
        carry_ref[...] += jnp.sum(dc, axis=0, keepdims=True)
        z = f_ref[...] + bf_ref[...]
        dz = jnp.where(lane < FOX_HEADS, dlogf * _sigmoid(-z), 0.0)
        dff_ref[...] = dz.astype(BF16)
        dbf_ref[...] += jnp.sum(dz, axis=0, keepdims=True)

    rev = lambda b, i: (b * nblk + (nblk - 1 - i), 0)
    vec = lambda n: pl.BlockSpec((1, n), lambda b, i: (0, 0))
    return pl.pallas_call(
        body, name=name, grid=(nb, nblk),
        in_specs=[pl.BlockSpec((tg, w), lambda b, i: (b * nblk + (nblk - 1 - i), FOX_CB)),
                  pl.BlockSpec((tg, w), lambda b, i: (b * nblk + (nblk - 1 - i), FOX_CB + 1)),
                  pl.BlockSpec((tg, LANES), lambda b, i: (b * nblk + (nblk - 1 - i), CB_FF)),
                  pl.BlockSpec((tg, 2 * w), rev), pl.BlockSpec((tg, 2 * w), rev), pl.BlockSpec((tg, w), rev),
                  pl.BlockSpec((tg, LANES), rev), vec(w), vec(w), vec(LANES)],
        out_specs=[pl.BlockSpec((tg, wo), rev), pl.BlockSpec((tg, LANES), rev), vec(w), vec(w), vec(LANES)],
        out_shape=[jax.ShapeDtypeStruct((m, wo), BF16), jax.ShapeDtypeStruct((m, LANES), BF16),
                   jax.ShapeDtypeStruct((1, w), F32),
                   jax.ShapeDtypeStruct((1, w), F32), jax.ShapeDtypeStruct((1, LANES), F32)],
        scratch_shapes=[pltpu.VMEM((1, LANES), F32)],
        compiler_params=_cp("arbitrary", "arbitrary"),
    )(proj, proj, proj, dqa, dka, dv, dcum, qg, kg, bf)


HG_BLOCKS_PER_STEP = 11


def _hg_rows(lp, most=3):
    nblk = lp // SEQ_BLOCK
    return SEQ_BLOCK * max(n for n in range(1, most + 1) if nblk % n == 0)


def _chunk_masks():
    r = lax.broadcasted_iota(jnp.int32, (SEQ_BLOCK, SEQ_BLOCK), 0)
    c = lax.broadcasted_iota(jnp.int32, (SEQ_BLOCK, SEQ_BLOCK), 1)
    same = (r // SUB) == (c // SUB)
    return r, c, same


def _hg_gates(hf, lb):
    sg = _sigmoid(hf)
    f = lb + (1.0 - lb) * sg
    return sg, f, jnp.log(f), (1.0 - lb) * _sigmoid(-hf)


def _hg_intra_e(g_ref, base, t, srow):
    diff = g_ref[pl.ds(base + t, 1), :] - g_ref[pl.ds(base, SUB), :]
    return jnp.exp(jnp.where(srow <= t, diff, NEG))


def _hgrn_fwd(proj, lb, gain, nb, lp, name):
    m = proj.shape[0]
    tb = _hg_rows(lp, HG_BLOCKS_PER_STEP)
    nblk = lp // tb
    nsb = tb // SEQ_BLOCK
    ns = SEQ_BLOCK // SUB

    def body(q_ref, f_ref, i_ref, g_ref, lb_ref, gain_ref, oraw_ref, y_ref, ssave_ref,
             st_ref, g_scr, kin_scr, o_scr):
        @pl.when(pl.program_id(2) == 0)
        def _():
            st_ref[...] = jnp.zeros_like(st_ref)

        for hb in range(nsb):
            rows = pl.ds(hb * SEQ_BLOCK, SEQ_BLOCK)
            block(q_ref.at[rows], f_ref.at[rows], i_ref.at[rows], g_ref.at[rows], lb_ref, gain_ref, oraw_ref.at[rows],
                  y_ref.at[rows], ssave_ref.at[hb], st_ref, g_scr.at[rows], kin_scr.at[rows], o_scr.at[rows])

    def block(q_ref, f_ref, i_ref, g_ref, lb_ref, gain_ref, oraw_ref, y_ref, ssave_ref,
              st_ref, g_scr, kin_scr, o_scr):
        ssave_ref[...] = st_ref[...]
        lbv = lb_ref[...]
        _, _, lf, kin = _hg_gates(f_ref[...], lbv)
        r, c, same = _chunk_masks()
        ltri = jnp.where(same & (c <= r), 1.0, 0.0).astype(F32)
        lall = jnp.where(same, 1.0, 0.0).astype(F32)
        g = _dot(ltri, lf, NN, HI)
        gt = _dot(lall, lf, NN, HI)
        g_scr[...] = g
        kin_scr[...] = kin
        qv = q_ref[...]
        qg = (qv * jnp.exp(g)).astype(BF16)
        kg = (kin * jnp.exp(gt - g)).astype(BF16)
        et = jnp.exp(gt)
        srow = lax.broadcasted_iota(jnp.int32, (SUB, 1), 0)
        subs = [slice(cc * SUB, (cc + 1) * SUB) for cc in range(ns)]
        ups = [_dot(i_ref[sl, :].astype(BF16), kg[sl], TN) for sl in subs]
        st = st_ref[...]
        starts = []
        for cc in range(ns):
            starts.append(st)
            st = et[cc * SUB:cc * SUB + 1, :] * st + ups[cc]
        st_ref[...] = st
        for cc, sl in enumerate(subs):
            base = cc * SUB
            kc = kin_scr[sl, :]
            vc = i_ref[sl, :]
            for t in range(SUB):
                e = _hg_intra_e(g_scr, base, t, srow)
                a = jnp.sum((q_ref[pl.ds(base + t, 1), :] * kc) * e, axis=1, keepdims=True)
                o_scr[pl.ds(base + t, 1), :] = jnp.sum(a * vc, axis=0, keepdims=True)
            o_scr[sl, :] += _dot(qg[sl], starts[cc].astype(BF16), NT)
        o = o_scr[...]
        oraw_ref[...] = o
        rr = lax.rsqrt(jnp.mean(o * o, axis=-1, keepdims=True) + EPS)
        hg = g_ref[...]
        y_ref[...] = (((o * rr) * gain_ref[...]) * (hg * _sigmoid(hg))).astype(y_ref.dtype)

    col = lambda cb: pl.BlockSpec((tb, LANES), lambda b, h, i, cb=cb: (b * nblk + i, cb + h))
    out_blk = pl.BlockSpec((tb, LANES), lambda b, h, i: (b * nblk + i, h))
    return pl.pallas_call(
        body, name=name, grid=(nb, HG_HEADS, nblk),
        in_specs=[col(CB_HQ), col(CB_HF), col(CB_HI), col(CB_HG),
                  pl.BlockSpec((1, LANES), lambda b, h, i: (0, h)),
                  pl.BlockSpec((1, LANES), lambda b, h, i: (0, 0))],
        out_specs=[out_blk, out_blk,
                   pl.BlockSpec((None, None, nsb, HG_DIM, HG_DIM), lambda b, h, i: (b, h, i, 0, 0))],
        out_shape=[jax.ShapeDtypeStruct((m, HG_WIDTH), F32), jax.ShapeDtypeStruct((m, HG_WIDTH), BF16),
                   jax.ShapeDtypeStruct((nb, HG_HEADS, nblk * nsb, HG_DIM, HG_DIM), F32)],
        scratch_shapes=[pltpu.VMEM((HG_DIM, HG_DIM), F32), pltpu.VMEM((tb, LANES), F32),
                        pltpu.VMEM((tb, LANES), F32), pltpu.VMEM((tb, LANES), F32)],
        compiler_params=_cp("parallel", "parallel", "arbitrary"),
    )(proj, proj, proj, proj, lb, gain)


def _hgrn_bwd(proj, oraw, ssave, dy, lb, gain, nb, lp, name):
    m = proj.shape[0]
    tb = _hg_rows(lp, HG_BLOCKS_PER_STEP)
    nblk = lp // tb
    nsb = tb // SEQ_BLOCK
    ns = SEQ_BLOCK // SUB

    def body(q_ref, f_ref, i_ref, g_ref, oraw_ref, ssave_ref, dy_ref, lb_ref, gain_ref,
             dq_ref, df_ref, di_ref, dg_ref, dgain_ref, dlb_ref,
             dst_ref, *scratch):
        hd = pl.program_id(0)
        bb = pl.program_id(1)
        ii = pl.program_id(2)

        @pl.when((hd == 0) & (bb == 0) & (ii == 0))
        def _():
            dgain_ref[...] = jnp.zeros_like(dgain_ref)

        @pl.when((bb == 0) & (ii == 0))
        def _():
            dlb_ref[...] = jnp.zeros_like(dlb_ref)

        @pl.when(ii == 0)
        def _():
            dst_ref[...] = jnp.zeros_like(dst_ref)

        for hb in reversed(range(nsb)):
            rows = pl.ds(hb * SEQ_BLOCK, SEQ_BLOCK)
            block(q_ref.at[rows], f_ref.at[rows], i_ref.at[rows], g_ref.at[rows], oraw_ref.at[rows], ssave_ref.at[hb],
                  dy_ref.at[rows], lb_ref, gain_ref, dq_ref.at[rows], df_ref.at[rows], di_ref.at[rows], dg_ref.at[rows],
                  dgain_ref, dlb_ref, dst_ref, *[sc.at[rows] for sc in scratch])

    def block(q_ref, f_ref, i_ref, g_ref, oraw_ref, ssave_ref, dy_ref, lb_ref, gain_ref,
              dq_ref, df_ref, di_ref, dg_ref, dgain_ref, dlb_ref,
              dst_ref, g_scr, kin_scr, do_scr, dq_scr, dk_scr, dv_scr, dgg_scr):
        gainv = gain_ref[...]
        lbv = lb_ref[...]

        o = oraw_ref[...]
        rr = lax.rsqrt(jnp.mean(o * o, axis=-1, keepdims=True) + EPS)
        nv = o * rr
        hg = g_ref[...]
        sgg = _sigmoid(hg)
        sil = hg * sgg
        dyv = dy_ref[...]
        dg_ref[...] = (dyv * nv * gainv * (sgg * (1.0 + hg * (1.0 - sgg)))).astype(dg_ref.dtype)
        dgain_ref[...] += jnp.sum(dyv * nv * sil, axis=0, keepdims=True)
        dn = dyv * gainv * sil
        do_scr[...] = rr * (dn - nv * jnp.mean(dn * nv, axis=-1, keepdims=True))

        hf = f_ref[...]
        sg, f, lf, kin = _hg_gates(hf, lbv)
        r, c, same = _chunk_masks()
        ltri = jnp.where(same & (c <= r), 1.0, 0.0).astype(F32)
        lall = jnp.where(same, 1.0, 0.0).astype(F32)
        g = _dot(ltri, lf, NN, HI)
        gt = _dot(lall, lf, NN, HI)
        g_scr[...] = g
        kin_scr[...] = kin
        qv = q_ref[...]
        eg = jnp.exp(g)
        ekg = jnp.exp(gt - g)
        qg = qv * eg
        kg = kin * ekg
        qg16 = qg.astype(BF16)
        kg16 = kg.astype(BF16)
        et = jnp.exp(gt)
        subs = [slice(cc * SUB, (cc + 1) * SUB) for cc in range(ns)]
        ups = [_dot(i_ref[sl, :].astype(BF16), kg16[sl], TN) for sl in subs]
        st = ssave_ref[...]
        starts = []
        for cc in range(ns):
            starts.append(st)
            st = et[cc * SUB:cc * SUB + 1, :] * st + ups[cc]
        do16 = do_scr[...].astype(BF16)
        downs = [_dot(do16[sl], qg16[sl], TN) for sl in subs]
        dst = dst_ref[...]
        afters = [None] * ns
        for cc in reversed(range(ns)):
            afters[cc] = dst
            dst = et[cc * SUB:cc * SUB + 1, :] * dst + downs[cc]
        dst_ref[...] = dst

        srow = lax.broadcasted_iota(jnp.int32, (SUB, 1), 0)
        for cc, sl in enumerate(subs):
            base = cc * SUB
            st = starts[cc]
            st16 = st.astype(BF16)
            dst = afters[cc]
            dst16 = dst.astype(BF16)
            doc16 = do16[sl]
            vc = i_ref[sl, :]
            vc16 = vc.astype(BF16)
            kc = kin_scr[sl, :]
            etc = et[base:base + 1, :]
            dqg = _dot(doc16, st16, NN)
            dv_c = _dot(kg16[sl], dst16, NT)
            dkg = _dot(vc16, dst16, NN)
            dgt = jnp.sum(dst * st, axis=0, keepdims=True) * etc
            dq_c = dqg * eg[sl]
            dk_c = dkg * ekg[sl]
            dg_c = dqg * qg[sl] - dkg * kg[sl]
            dgt = dgt + jnp.sum(dkg * kg[sl], axis=0, keepdims=True)
            for t in range(SUB):
                e = _hg_intra_e(g_scr, base, t, srow)
                qt = q_ref[pl.ds(base + t, 1), :]
                dot_t = do_scr[pl.ds(base + t, 1), :]
                a = jnp.sum((qt * kc) * e, axis=1, keepdims=True)
                da = jnp.sum(dot_t * vc, axis=1, keepdims=True)
                dv_c = dv_c + a * dot_t
                w = da * e
                dq_scr[pl.ds(base + t, 1), :] = jnp.sum(w * kc, axis=0, keepdims=True)
                wq = w * qt
                dk_c = dk_c + wq
                dg_c = dg_c - kc * wq
            dq_i = dq_scr[sl, :]
            dg_c = dg_c + qv[sl] * dq_i + jnp.where(srow == SUB - 1, dgt, 0.0)
            dq_scr[sl, :] = dq_c + dq_i
            dk_scr[sl, :] = dk_c
            dv_scr[sl, :] = dv_c
            dgg_scr[sl, :] = dg_c

        utri = jnp.where(same & (c >= r), 1.0, 0.0).astype(F32)
        dlf = _dot(utri, dgg_scr[...], NN, HI)
        dkin = dk_scr[...]
        dsg = sg * (1.0 - sg)
        df_ref[...] = ((dlf / f - dkin) * ((1.0 - lbv) * dsg)).astype(df_ref.dtype)
        dlb_ref[...] += jnp.sum((dlf / f - dkin) * (1.0 - sg), axis=0, keepdims=True)
        dq_ref[...] = dq_scr[...].astype(dq_ref.dtype)
        di_ref[...] = dv_scr[...].astype(di_ref.dtype)

    rowi = lambda b, i: b * nblk + (nblk - 1 - i)
    col = lambda cb: pl.BlockSpec((tb, LANES), lambda h, b, i, cb=cb: (rowi(b, i), cb + h))
    hblk = pl.BlockSpec((tb, LANES), lambda h, b, i: (rowi(b, i), h))
    return pl.pallas_call(
        body, name=name, grid=(HG_HEADS, nb, nblk),
        in_specs=[col(CB_HQ), col(CB_HF), col(CB_HI), col(CB_HG), hblk,
                  pl.BlockSpec((None, None, nsb, HG_DIM, HG_DIM), lambda h, b, i: (b, h, nblk - 1 - i, 0, 0)),
                  hblk,
                  pl.BlockSpec((1, LANES), lambda h, b, i: (0, h)),
                  pl.BlockSpec((1, LANES), lambda h, b, i: (0, 0))],
        out_specs=[hblk, hblk, hblk, hblk,
                   pl.BlockSpec((1, LANES), lambda h, b, i: (0, 0)),
                   pl.BlockSpec((1, LANES), lambda h, b, i: (0, h))],
        out_shape=[jax.ShapeDtypeStruct((m, HG_WIDTH), BF16)] * 4
        + [jax.ShapeDtypeStruct((1, LANES), F32), jax.ShapeDtypeStruct((1, HG_WIDTH), F32)],
        scratch_shapes=[pltpu.VMEM((HG_DIM, HG_DIM), F32)] + [pltpu.VMEM((tb, LANES), F32)] * 7,
        compiler_params=_cp("arbitrary", "arbitrary", "arbitrary"),
    )(proj, proj, proj, proj, oraw, ssave, dy, lb, gain)


def _gate_fwd(proj, ya, yb, name):
    m = proj.shape[0]
    tm = _div_tile(m, 384, 16)

    def body(ga_ref, gb_ref, ya_ref, yb_ref, o_ref):
        ya, yb = ya_ref[...].astype(F32), yb_ref[...].astype(F32)
        o_ref[...] = (_sigmoid(ga_ref[...]) * ya + _sigmoid(gb_ref[...]) * yb).astype(o_ref.dtype)

    row = pl.BlockSpec((tm, D_MODEL), lambda i: (i, 0))
    return pl.pallas_call(
        body, name=name, grid=(m // tm,),
        in_specs=[row, pl.BlockSpec((tm, D_MODEL), lambda i: (i, 1)), row, row],
        out_specs=row, out_shape=jax.ShapeDtypeStruct((m, D_MODEL), BF16),
        compiler_params=_cp("parallel"),
    )(proj, proj, ya, yb)


def _gate_bwd(proj, ya, yb, dm, name):
    m = proj.shape[0]
    tm = _div_tile(m, 384, 16)

    def body(ga_ref, gb_ref, ya_ref, yb_ref, dm_ref, dya_ref, dyb_ref, dg_ref):
        dmv = dm_ref[...].astype(F32)
        sa = _sigmoid(ga_ref[...])
        sb = _sigmoid(gb_ref[...])
        dya_ref[...] = (dmv * sa).astype(BF16)
        dyb_ref[...] = (dmv * sb).astype(BF16)
        dg_ref[:, :D_MODEL] = (dmv * ya_ref[...].astype(F32) * (sa * (1.0 - sa))).astype(BF16)
        dg_ref[:, D_MODEL:] = (dmv * yb_ref[...].astype(F32) * (sb * (1.0 - sb))).astype(BF16)

    row = pl.BlockSpec((tm, D_MODEL), lambda i: (i, 0))
    wide = pl.BlockSpec((tm, 2 * D_MODEL), lambda i: (i, 0))
    return pl.pallas_call(
        body, name=name, grid=(m // tm,),
        in_specs=[row, pl.BlockSpec((tm, D_MODEL), lambda i: (i, 1)), row, row, row],
        out_specs=[row, row, wide],
        out_shape=[jax.ShapeDtypeStruct((m, D_MODEL), BF16)] * 2 + [jax.ShapeDtypeStruct((m, 2 * D_MODEL), BF16)],
        compiler_params=_cp("parallel"),
    )(proj, proj, ya, yb, dm)


CONV_ROWS = 128


def _conv3(x, xprev, w_ref, b_ref, rowi):
    r = x.shape[0]
    x1 = jnp.where(rowi < 1, pltpu.roll(xprev, 1, 0), pltpu.roll(x, 1, 0))
    x2 = jnp.where(rowi < 2, pltpu.roll(xprev, 2, 0), pltpu.roll(x, 2, 0))
    u = w_ref[0:1, :] * x2 + w_ref[1:2, :] * x1 + w_ref[2:3, :] * x + b_ref[...]
    return u, x1, x2


def _conv_fwd(up, cw, cb, nb, lp, name):
    m = up.shape[0]
    nct = D_FF // LANES
    r = CONV_ROWS
    nch = lp // r

    def body(u_ref, w_ref, b_ref, o_ref):
        rowi = lax.broadcasted_iota(jnp.int32, (r, 1), 0)

        def step(i, xp):
            r0 = pl.multiple_of(i * r, r)
            xc = u_ref[pl.ds(r0, r), :].astype(F32)
            u, _, _ = _conv3(xc, xp, w_ref, b_ref, rowi)
            ug, uv = u[:, :LANES], u[:, LANES:]
            o_ref[pl.ds(r0, r), :] = ((ug * _sigmoid(ug)) * uv).astype(o_ref.dtype)
            return xc

        lax.fori_loop(0, nch, step, jnp.zeros((r, 2 * LANES), F32))

    return pl.pallas_call(
        body, name=name, grid=(nb, nct),
        in_specs=[pl.BlockSpec((lp, 2 * LANES), lambda b, c: (b, c)),
                  pl.BlockSpec((CONV_WIDTH, 2 * LANES), lambda b, c: (0, c)),
                  pl.BlockSpec((1, 2 * LANES), lambda b, c: (0, c))],
        out_specs=pl.BlockSpec((lp, LANES), lambda b, c: (b, c)),
        out_shape=jax.ShapeDtypeStruct((m, D_FF), BF16),
        compiler_params=_cp("parallel", "parallel"),
    )(up, cw, cb)


def _conv_bwd(up, dact, cw, cb, nb, lp, name):
    m = up.shape[0]
    nct = D_FF // LANES
    r = CONV_ROWS
    nch = lp // r

    def body(u_ref, da_ref, w_ref, b_ref, dup_ref, dw_ref, db_ref):
        rowi = lax.broadcasted_iota(jnp.int32, (r, 1), 0)
        wv = w_ref[...]

        def step(k, carry):
            dun, dw0, dw1, dw2, dbs = carry
            i = nch - 1 - k
            r0 = pl.multiple_of(i * r, r)
            rp = pl.multiple_of(jnp.maximum(i - 1, 0) * r, r)
            xc = u_ref[pl.ds(r0, r), :].astype(F32)
            xp = u_ref[pl.ds(rp, r), :].astype(F32) * (i > 0).astype(F32)
            u, x1, x2 = _conv3(xc, xp, w_ref, b_ref, rowi)
            ug, uv = u[:, :LANES], u[:, LANES:]
            da = da_ref[pl.ds(r0, r), :].astype(F32)
            sg = _sigmoid(ug)
            du = jnp.concatenate([da * uv * (sg * (1.0 + ug * (1.0 - sg))), da * (ug * sg)], axis=1)
            d1 = jnp.where(rowi >= r - 1, pltpu.roll(dun, r - 1, 0), pltpu.roll(du, r - 1, 0))
            d2 = jnp.where(rowi >= r - 2, pltpu.roll(dun, r - 2, 0), pltpu.roll(du, r - 2, 0))
            dup_ref[pl.ds(r0, r), :] = (wv[2:3, :] * du + wv[1:2, :] * d1 + wv[0:1, :] * d2).astype(dup_ref.dtype)
            dw0 = dw0 + jnp.sum(du * x2, axis=0, keepdims=True)
            dw1 = dw1 + jnp.sum(du * x1, axis=0, keepdims=True)
            dw2 = dw2 + jnp.sum(du * xc, axis=0, keepdims=True)
            dbs = dbs + jnp.sum(du, axis=0, keepdims=True)
            return du, dw0, dw1, dw2, dbs

        z1 = jnp.zeros((1, 2 * LANES), F32)
        _, dw0, dw1, dw2, dbs = lax.fori_loop(0, nch, step, (jnp.zeros((r, 2 * LANES), F32), z1, z1, z1, z1))

        @pl.when(pl.program_id(1) == 0)
        def _():
            dw_ref[...] = jnp.zeros_like(dw_ref)
            db_ref[...] = jnp.zeros_like(db_ref)

        dw_ref[0:1, :] += dw0
        dw_ref[1:2, :] += dw1
        dw_ref[2:3, :] += dw2
        db_ref[...] += dbs

    return pl.pallas_call(
        body, name=name, grid=(nct, nb),
        in_specs=[pl.BlockSpec((lp, 2 * LANES), lambda c, b: (b, c)),
                  pl.BlockSpec((lp, LANES), lambda c, b: (b, c)),
                  pl.BlockSpec((CONV_WIDTH, 2 * LANES), lambda c, b: (0, c)),
                  pl.BlockSpec((1, 2 * LANES), lambda c, b: (0, c))],
        out_specs=[pl.BlockSpec((lp, 2 * LANES), lambda c, b: (b, c)),
                   pl.BlockSpec((CONV_WIDTH, 2 * LANES), lambda c, b: (0, c)),
                   pl.BlockSpec((1, 2 * LANES), lambda c, b: (0, c))],
        out_shape=[jax.ShapeDtypeStruct((m, 2 * D_FF), BF16),
                   jax.ShapeDtypeStruct((CONV_WIDTH, 2 * D_FF), F32),
                   jax.ShapeDtypeStruct((1, 2 * D_FF), F32)],
        compiler_params=_cp("parallel", "arbitrary"),
    )(up, dact, cw, cb)


def _ffn_interleave(a, axis):
    shp = a.shape
    a = a.reshape(shp[:axis] + (2, D_FF // LANES, LANES) + shp[axis + 1:])
    return jnp.swapaxes(a, axis, axis + 1).reshape(shp)


def _ffn_deinterleave(a, axis):
    shp = a.shape
    a = a.reshape(shp[:axis] + (D_FF // LANES, 2, LANES) + shp[axis + 1:])
    return jnp.swapaxes(a, axis, axis + 1).reshape(shp)


def _shifted_rows(prev_ref, cur_ref):
    keep = SEQ_BLOCK - N_META
    return jnp.concatenate([prev_ref[keep:, :], cur_ref[:keep, :]], axis=0)


def _frame_specs(nb, nfb, d):
    prev = pl.BlockSpec((nb, SEQ_BLOCK, d), lambda i: (0, jnp.clip(i - 1, 0, nfb - 1), 0))
    cur = pl.BlockSpec((nb, SEQ_BLOCK, d), lambda i: (0, jnp.clip(i, 0, nfb - 1), 0))
    return prev, cur


def _embed_rms(x, meta, gain, lp, name):
    nb, seq, d = x.shape
    l = seq + N_META
    tr = SEQ_BLOCK
    nblk = lp // tr

    def body(prev_ref, cur_ref, meta_ref, g_ref, h_ref, o_ref):
        t = pl.program_id(0) * tr + lax.broadcasted_iota(jnp.int32, (tr, 1), 0)
        head = jnp.concatenate([meta_ref[...], jnp.zeros((tr - N_META, d), F32)], axis=0)
        for b in range(nb):
            rows = jnp.where(t < l, _shifted_rows(prev_ref.at[b], cur_ref.at[b]), 0.0)
            xv = jnp.where(t < N_META, head, rows)
            h_ref[b] = xv
            r = lax.rsqrt(jnp.mean(xv * xv, axis=-1, keepdims=True) + EPS)
            o_ref[b] = ((xv * r) * g_ref[...]).astype(o_ref.dtype)

    prev, cur = _frame_specs(nb, seq // tr, d)
    row = pl.BlockSpec((nb, tr, d), lambda i: (0, i, 0))
    h0, xn = pl.pallas_call(
        body, name=name, grid=(nblk,),
        in_specs=[prev, cur, pl.BlockSpec((N_META, d), lambda i: (0, 0)), pl.BlockSpec((1, d), lambda i: (0, 0))],
        out_specs=[row, row],
        out_shape=[jax.ShapeDtypeStruct((nb, lp, d), F32), jax.ShapeDtypeStruct((nb, lp, d), BF16)],
        compiler_params=_cp("parallel"),
    )(x, x, meta, gain)
    return h0.reshape(nb * lp, d), xn.reshape(nb * lp, d)


def _loss_head(out, target, lp, name):
    nb, seq, d = target.shape
    l = seq + N_META
    tr = SEQ_BLOCK
    nblk = lp // tr

    def body(o_ref, prev_ref, cur_ref, dy_ref, ls_ref):
        t = pl.program_id(0) * tr + lax.broadcasted_iota(jnp.int32, (tr, 1), 0)
        valid = (t >= N_META) & (t < l)
        part = jnp.zeros((1, d), F32)
        for b in range(nb):
            err = jnp.where(valid, o_ref[b] - _shifted_rows(prev_ref.at[b], cur_ref.at[b]), 0.0)
            dy_ref[b] = err * (1.0 / d)
            part = part + jnp.sum(err * err, axis=0, keepdims=True)

        @pl.when(pl.program_id(0) == 0)
        def _():
            ls_ref[...] = part

        @pl.when(pl.program_id(0) > 0)
        def _():
            ls_ref[...] += part

    prev, cur = _frame_specs(nb, seq // tr, d)
    row = pl.BlockSpec((nb, tr, d), lambda i: (0, i, 0))
    dy, lsum = pl.pallas_call(
        body, name=name, grid=(nblk,),
        in_specs=[row, prev, cur], out_specs=[row, pl.BlockSpec((1, d), lambda i: (0, 0))],
        out_shape=[jax.ShapeDtypeStruct((nb, lp, d), F32), jax.ShapeDtypeStruct((1, d), F32)],
        compiler_params=_cp("arbitrary"),
    )(out.reshape(nb, lp, d), target, target)
    return dy.reshape(nb * lp, d), lsum


def _adam_math(g, w, mom, var):
    c1 = 1.0 - ADAM_B1 ** ADAM_STEP
    c2 = 1.0 - ADAM_B2 ** ADAM_STEP
    mn = ADAM_B1 * mom + (1.0 - ADAM_B1) * g
    vn = ADAM_B2 * var + (1.0 - ADAM_B2) * (g * g)
    delta = -ADAM_LR * ((mn / c1) / (jnp.sqrt(vn / c2) + ADAM_EPS) + ADAM_WD * w)
    return delta, mn, vn


def _slot_sum(recv, name):
    _, r, c = recv.shape
    tc = _div_tile(c, 256, LANES)

    def body(r_ref, g_ref):
        g = r_ref[0].astype(F32)
        for s in range(1, N_DEV):
            g = g + r_ref[s].astype(F32)
        g_ref[...] = g

    return pl.pallas_call(
        body, name=name, grid=(c // tc,),
        in_specs=[pl.BlockSpec((N_DEV, r, tc), lambda j: (0, 0, j))],
        out_specs=pl.BlockSpec((r, tc), lambda j: (0, j)),
        out_shape=jax.ShapeDtypeStruct((r, c), F32),
        compiler_params=_cp("parallel"),
    )(recv)


def _adamw(g, w, mom, var, name):
    r, c = w.shape
    tr = _div_tile(r, 256, 8)

    def body(g_ref, w_ref, m_ref, v_ref, d_ref, mo_ref, vo_ref):
        d_ref[...], mo_ref[...], vo_ref[...] = _adam_math(g_ref[...], w_ref[...], m_ref[...], v_ref[...])

    row = pl.BlockSpec((tr, c), lambda i: (i, 0))
    return pl.pallas_call(
        body, name=name, grid=(r // tr,), in_specs=[row] * 4, out_specs=[row] * 3,
        out_shape=[jax.ShapeDtypeStruct((r, c), F32)] * 3,
        compiler_params=_cp("parallel"),
    )(g, w, mom, var)


def _sum_adamw(recv, w, mom, var, name):
    r, c = w.shape
    tr = _div_tile(r, 256, 8)

    def body(r_ref, w_ref, m_ref, v_ref, g_ref, d_ref, mo_ref, vo_ref):
        g = r_ref[0].astype(F32)
        for s in range(1, N_DEV):
            g = g + r_ref[s].astype(F32)
        g_ref[...] = g
        d_ref[...], mo_ref[...], vo_ref[...] = _adam_math(g, w_ref[...], m_ref[...], v_ref[...])

    row = pl.BlockSpec((tr, c), lambda i: (i, 0))
    return pl.pallas_call(
        body, name=name, grid=(r // tr,),
        in_specs=[pl.BlockSpec((N_DEV, tr, c), lambda i: (0, i, 0)), row, row, row],
        out_specs=[row] * 4,
        out_shape=[jax.ShapeDtypeStruct((r, c), F32)] * 4,
        compiler_params=_cp("parallel"),
    )(recv, w, mom, var)


_MESH = pl.DeviceIdType.MESH
_HBM = pl.BlockSpec(memory_space=pltpu.HBM)
N_PEER = N_DEV - 1


def _position():
    return lax.axis_index("x"), lax.axis_index("y"), lax.axis_index("c")


def _all_gather(shards, name):
    n = len(shards)

    def body(*refs):
        x_refs, out_refs = refs[:n], refs[n:2 * n]
        send_sems, recv_sems, local_sems = refs[2 * n:]
        x, y, c = _position()
        me, sibling = (x, y, c), (x, y, 1 - c)
        chips = [(1 - x, y), (x, 1 - y), (1 - x, 1 - y)]

        def copy(a, k, block, to, src=None):
            slot = out_refs[a].at[4 * block[0] + 2 * block[1] + block[2]]
            return pltpu.make_async_remote_copy(
                src_ref=slot if src is None else src, dst_ref=slot,
                send_sem=send_sems.at[a * N_PEER + k], recv_sem=recv_sems.at[a * N_PEER + k],
                device_id=to, device_id_type=_MESH)

        mine, sent = [], []
        for a in range(n):
            cp = pltpu.make_async_copy(x_refs[a], out_refs[a].at[4 * x + 2 * y + c], local_sems.at[a])
            cp.start()
            mine.append(cp)
            first = [copy(a, 0, me, sibling, src=x_refs[a])]
            first += [copy(a, 1 + j, me, (*chip, c), src=x_refs[a]) for j, chip in enumerate(chips)]
            for cp in first:
                cp.start()
            sent += first
        for a in range(n):
            for j, chip in enumerate(chips):
                copy(a, 1 + j, (*chip, c), me).wait_recv()
                fwd = copy(a, 4 + j, (*chip, c), sibling)
                fwd.start()
                sent.append(fwd)
        for a in range(n):
            copy(a, 0, sibling, me).wait_recv()
            for j, chip in enumerate(chips):
                copy(a, 4 + j, (*chip, 1 - c), me).wait_recv()
        for cp in sent:
            cp.wait_send()
        for cp in mine:
            cp.wait()

    return pl.pallas_call(
        body, name=name,
        out_shape=[jax.ShapeDtypeStruct((N_DEV,) + a.shape, a.dtype) for a in shards],
        in_specs=[_HBM] * n, out_specs=[_HBM] * n,
        scratch_shapes=[pltpu.SemaphoreType.DMA((n * N_PEER,)), pltpu.SemaphoreType.DMA((n * N_PEER,)),
                        pltpu.SemaphoreType.DMA((n,))],
    )(*shards)


_FLIPS = [(fx, fy, fc) for fx in (0, 1) for fy in (0, 1) for fc in (0, 1)][1:]


def _exchange_copies(in_refs, out_refs, nblk, send_sems, recv_sems, local_sems):
    n = len(in_refs)
    x, y, c = _position()
    me = 4 * x + 2 * y + c

    def peer(f):
        return (1 - x if f[0] else x, 1 - y if f[1] else y, 1 - c if f[2] else c)

    def idx(p):
        return 4 * p[0] + 2 * p[1] + p[2]

    def local(a):
        return pltpu.make_async_copy(in_refs[a].at[me] if a < nblk else in_refs[a], out_refs[a].at[me], local_sems.at[a])

    def remote(a, k, sending):
        p = peer(_FLIPS[k])
        src = in_refs[a].at[idx(p)] if a < nblk else in_refs[a]
        dst = out_refs[a].at[me] if sending else out_refs[a].at[idx(p)]
        return pltpu.make_async_remote_copy(
            src_ref=src, dst_ref=dst, send_sem=send_sems.at[a * N_PEER + k], recv_sem=recv_sems.at[a * N_PEER + k],
            device_id=p, device_id_type=_MESH)

    def start():
        for a in range(n):
            local(a).start()
            for k in range(N_PEER):
                remote(a, k, True).start()

    def wait():
        for a in range(n):
            for k in range(N_PEER):
                remote(a, k, False).wait_recv()
        for a in range(n):
            for k in range(N_PEER):
                remote(a, k, True).wait_send()
            local(a).wait()

    return start, wait


def _exchange_io(blocks, shared):
    arrays = list(blocks) + list(shared)
    n = len(arrays)
    out_shape = [jax.ShapeDtypeStruct(a.shape, a.dtype) for a in blocks]
    out_shape += [jax.ShapeDtypeStruct((N_DEV,) + a.shape, a.dtype) for a in shared]
    sems = [pltpu.SemaphoreType.DMA((n * N_PEER,)), pltpu.SemaphoreType.DMA((n * N_PEER,)), pltpu.SemaphoreType.DMA((n,))]
    return arrays, out_shape, sems


def _exchange(blocks, shared, name):
    arrays, out_shape, sems = _exchange_io(blocks, shared)
    n = len(arrays)

    def body(*refs):
        start, wait = _exchange_copies(refs[:n], refs[n:2 * n], len(blocks), *refs[2 * n:])
        start()
        wait()

    return pl.pallas_call(
        body, name=name, out_shape=out_shape, in_specs=[_HBM] * n, out_specs=[_HBM] * n, scratch_shapes=sems,
    )(*arrays)


def _grid_ends(grid):
    ids = [pl.program_id(i) for i in range(len(grid))]
    first = functools.reduce(jnp.logical_and, [i == 0 for i in ids])
    last = functools.reduce(jnp.logical_and, [i == g - 1 for i, g in zip(ids, grid)])
    return first, last


def _pack(parts, rows):
    flat = jnp.concatenate(parts, axis=-1)
    return jnp.pad(flat, [(0, rows * LANES - flat.shape[-1])]).reshape(rows, LANES)


def _unpack(packed, shapes):
    flat = packed.reshape(-1)
    out, off = [], 0
    for shp in shapes:
        n = int(np.prod(shp))
        out.append(flat[off:off + n].reshape(shp))
        off += n
    return out


def _rows_for(shapes, extra=0):
    n = sum(int(np.prod(s)) for s in shapes) + extra
    return -(-n // (8 * LANES)) * 8


def _lower_bound(logits):
    return jnp.cumsum(jax.nn.softmax(logits.astype(F32), axis=0), axis=0)[0:1]


def _align_axis0(w):
    a, b = 3 * FOX_WIDTH, 3 * FOX_WIDTH + FOX_HEADS
    c = b + 4 * HG_WIDTH
    pad = [(0, LANES - FOX_HEADS)] + [(0, 0)] * (w.ndim - 1)
    return jnp.concatenate([w[c:], w[:a], w[b:c], jnp.pad(w[a:b], pad)], axis=0)


def _unalign_axis0(g):
    a, b = 2 * D_MODEL, 2 * D_MODEL + 3 * FOX_WIDTH
    c = b + 4 * HG_WIDTH
    return jnp.concatenate([g[a:b], g[c:c + FOX_HEADS], g[b:c], g[:a]], axis=0)


TINY_COLS = 768


def _tiny_pack(conv_w_shard, meta_shard):
    cw = jnp.pad(conv_w_shard, ((0, 8 - CONV_WIDTH), (0, TINY_COLS - conv_w_shard.shape[1])))
    mt = jnp.pad(meta_shard, ((0, 0), (0, TINY_COLS - meta_shard.shape[1])))
    return jnp.concatenate([cw, mt], axis=0)


def _tiny_unpack(t, ncw, nmeta):
    return t[..., :CONV_WIDTH, :ncw], t[..., 8:8 + N_META, :nmeta]


def _late_weights(g_up, g_down, g_ab, g_out):
    d = g_up.shape[-1]
    w_a_t = g_ab[:, 0].reshape(-1, g_ab.shape[-1])
    w_b_t = g_ab[:, 1].reshape(-1, g_ab.shape[-1])
    return _ffn_interleave(g_up.reshape(-1, d), 0), g_down.reshape(-1, d), w_a_t, w_b_t, g_out.reshape(-1, d)


def _early_blocks(g_w_up_t, g_w_down, g_w_out, g_w_a_t, g_w_b_t):
    d = g_w_out.shape[-1]
    ab = jnp.stack([g_w_a_t.reshape(N_DEV, -1, g_w_a_t.shape[-1]), g_w_b_t.reshape(N_DEV, -1, g_w_b_t.shape[-1])], axis=1)
    return [_ffn_deinterleave(g_w_up_t, 0).reshape(N_DEV, -1, d).astype(BF16), g_w_down.reshape(N_DEV, -1, d).astype(BF16),
            g_w_out.reshape(N_DEV, -1, d).astype(BF16), ab.astype(BF16)]


def _local_step(x, target, meta, norm1_gain, w_in_t, fox_b_f, q_gain, k_gain, lb, hg_out_gain, w_a_t, w_b_t, w_out,
                norm2_gain, w_up_t, conv_w, conv_b, w_down, ffn_shards=None):
    nb, seq, d = x.shape
    assert seq % SEQ_BLOCK == 0 and N_META < SEQ_BLOCK
    l = seq + N_META
    lp = -(-l // SEQ_BLOCK) * SEQ_BLOCK
    m = nb * lp
    qg = jnp.tile(q_gain, (1, FOX_HEADS))
    kg = jnp.tile(k_gain, (1, FOX_HEADS))
    bf = jnp.pad(fox_b_f, ((0, 0), (0, LANES - FOX_HEADS)))

    h0, xn = _embed_rms(x, meta, norm1_gain, lp, "embed_rms1")
    proj = _matmul(xn, w_in_t, "nt", F32, "proj_in")
    qa, ka, vb = _fox_prep(proj, qg, kg, bf, nb, lp, "fox_prep")
    if ffn_shards is None:
        o_fox, lse = _fox_fwd(qa, ka, vb, nb, lp, "fox_fwd")
    else:
        o_fox, lse, *late = _fox_fwd(qa, ka, vb, nb, lp, "fox_fwd", ride=ffn_shards)
        w_up_t, w_down, w_a_t, w_b_t, w_out = _late_weights(*late)
    o_raw, o_hg, s_save = _hgrn_fwd(proj, lb, hg_out_gain, nb, lp, "hgrn_fwd")
    ya = _matmul(o_hg, w_a_t, "nt", BF16, "branch_a")
    yb = _matmul(o_fox, w_b_t, "nt", BF16, "branch_b")
    merged = _gate_fwd(proj, ya, yb, "gate_fwd")
    h1, hn = _matmul_res_rms(merged, w_out, h0, norm2_gain, "mix_out_rms2")
    up = _matmul(hn, w_up_t, "nt", BF16, "ffn_up")
    act = _conv_fwd(up, conv_w, conv_b, nb, lp, "conv_fwd")
    out = _matmul(act, w_down, "nn", F32, "ffn_down", residual=h1)
    dy, lsum = _loss_head(out, target, lp, "loss_head")
    loss = (0.5 / d) * jnp.sum(lsum)

    dact = _matmul(dy, w_down, "nt", BF16, "d_act")
    g_w_down = _matmul(act, dy, "tn", F32, "g_w_down")
    dup, g_conv_w, g_conv_b = _conv_bwd(up, dact, conv_w, conv_b, nb, lp, "conv_bwd")
    dhn = _matmul(dup, w_up_t, "nn", F32, "d_hn")
    g_w_up_t = _matmul(dup, hn, "tn", F32, "g_w_up")
    dh1, g_norm2 = _rms_bwd(h1, norm2_gain, dhn, dy, "rms2_bwd")

    dmerged = _matmul(dh1, w_out, "nt", BF16, "d_merged")
    g_w_out = _matmul(merged, dh1, "tn", F32, "g_w_out")
    dya, dyb, dgab = _gate_bwd(proj, ya, yb, dmerged, "gate_bwd")
    do_hg = _matmul(dya, w_a_t, "nn", F32, "d_o_hg")
    g_w_a_t = _matmul(dya, o_hg, "tn", F32, "g_w_a")
    do_fox = _matmul(dyb, w_b_t, "nn", BF16, "d_o_fox")
    g_w_b_t = _matmul(dyb, o_fox, "tn", F32, "g_w_b")
    dhq, dhf, dhi, dhg, g_hg_gain, g_lb = _hgrn_bwd(proj, o_raw, s_save, do_hg, lb, hg_out_gain, nb, lp, "hgrn_bwd")
    if ffn_shards is None:
        dqs, dkn, dvv, dc0, dc1 = _fox_bwd(qa, ka, vb, do_fox, o_fox, lse, nb, lp, "fox_bwd")
        early = None
    else:
        dqs, dkn, dvv, dc0, dc1, *early = _fox_bwd(qa, ka, vb, do_fox, o_fox, lse, nb, lp, "fox_bwd",
                                                   ride=_early_blocks(g_w_up_t, g_w_down, g_w_out, g_w_a_t, g_w_b_t))
    dcum = jnp.stack([dc0, dc1], axis=2).reshape(nb, FOX_HEADS, lp)
    dcum = jnp.pad(jnp.transpose(dcum, (0, 2, 1)), ((0, 0), (0, 0), (0, LANES - FOX_HEADS))).reshape(m, LANES)
    dfqkv, dff, g_qg, g_kg, g_bf = _fox_prep_bwd(proj, dqs, dkn, dvv, dcum, qg, kg, bf, nb, lp, "fox_prep_bwd")
    dproj = jnp.concatenate([dgab, dfqkv, dhq, dhf, dhi, dhg, dff], axis=1)
    g_w_in_t = _matmul(dproj, xn, "tn", F32, "g_w_in")
    if ffn_shards is None:
        dxn = _matmul(dproj, w_in_t, "nn", F32, "d_xn")
    else:
        blocks_in = _unalign_axis0(g_w_in_t).reshape(N_DEV, -1, d).astype(BF16)
        dxn, r_in = _matmul(dproj, w_in_t, "nn", F32, "d_xn", ride=[blocks_in])
        early = early + [r_in]
    dh0, g_norm1 = _rms_bwd(h0, norm1_gain, dxn, dh1, "rms1_bwd")

    dh0 = dh0.reshape(nb, lp, d)
    grad_x = dh0[:, N_META:l]
    g_meta = jnp.sum(dh0[:, :N_META], axis=0)
    g_q_gain = jnp.sum(g_qg.reshape(FOX_HEADS, FOX_HEAD_DIM), axis=0, keepdims=True)
    g_k_gain = jnp.sum(g_kg.reshape(FOX_HEADS, FOX_HEAD_DIM), axis=0, keepdims=True)
    grads = dict(meta_tokens=g_meta, norm1_gain=g_norm1, w_in_t=g_w_in_t, fox_b_f=g_bf[:, :FOX_HEADS],
                 q_norm_gain=g_q_gain, k_norm_gain=g_k_gain, lb=g_lb, hg_out_gain=g_hg_gain,
                 w_a_t=g_w_a_t, w_b_t=g_w_b_t, w_out=g_w_out, norm2_gain=g_norm2, w_up_t=g_w_up_t,
                 conv_w=g_conv_w, conv_b=g_conv_b, w_down=g_w_down, early=early)
    return loss, grad_x, grads


SMALL = ("norm1_gain", "fox_b_f", "q_norm_gain", "k_norm_gain", "hg_lb_logits", "hg_out_gain", "norm2_gain", "conv_b")
ORDER = ("meta_tokens", "norm1_gain", "w_in", "fox_b_f", "q_norm_gain", "k_norm_gain", "hg_lb_logits", "hg_out_gain",
         "w_branch_a", "w_branch_b", "w_out", "norm2_gain", "w_up", "conv_w", "conv_b", "w_down")


def kernel(x, meta_tokens, norm1_gain, w_in, fox_b_f, q_norm_gain, k_norm_gain, hg_lb_logits, hg_out_gain, w_branch_a, w_branch_b, w_out, norm2_gain, w_up, conv_w, conv_b, w_down, loss_target, m_meta_tokens, m_norm1_gain, m_w_in, m_fox_b_f, m_q_norm_gain, m_k_norm_gain, m_hg_lb_logits, m_hg_out_gain, m_w_branch_a, m_w_branch_b, m_w_out, m_norm2_gain, m_w_up, m_conv_w, m_conv_b, m_w_down, v_meta_tokens, v_norm1_gain, v_w_in, v_fox_b_f, v_q_norm_gain, v_k_norm_gain, v_hg_lb_logits, v_hg_out_gain, v_w_branch_a, v_w_branch_b, v_w_out, v_norm2_gain, v_w_up, v_conv_w, v_conv_b, v_w_down):
    w = dict(meta_tokens=meta_tokens, norm1_gain=norm1_gain, w_in=w_in, fox_b_f=fox_b_f, q_norm_gain=q_norm_gain,
             k_norm_gain=k_norm_gain, hg_lb_logits=hg_lb_logits, hg_out_gain=hg_out_gain, w_branch_a=w_branch_a,
             w_branch_b=w_branch_b, w_out=w_out, norm2_gain=norm2_gain, w_up=w_up, conv_w=conv_w, conv_b=conv_b,
             w_down=w_down)
    mom = dict(meta_tokens=m_meta_tokens, norm1_gain=m_norm1_gain, w_in=m_w_in, fox_b_f=m_fox_b_f,
               q_norm_gain=m_q_norm_gain, k_norm_gain=m_k_norm_gain, hg_lb_logits=m_hg_lb_logits,
               hg_out_gain=m_hg_out_gain, w_branch_a=m_w_branch_a, w_branch_b=m_w_branch_b, w_out=m_w_out,
               norm2_gain=m_norm2_gain, w_up=m_w_up, conv_w=m_conv_w, conv_b=m_conv_b, w_down=m_w_down)
    var = dict(meta_tokens=v_meta_tokens, norm1_gain=v_norm1_gain, w_in=v_w_in, fox_b_f=v_fox_b_f,
               q_norm_gain=v_q_norm_gain, k_norm_gain=v_k_norm_gain, hg_lb_logits=v_hg_lb_logits,
               hg_out_gain=v_hg_out_gain, w_branch_a=v_w_branch_a, w_branch_b=v_w_branch_b, w_out=v_w_out,
               norm2_gain=v_norm2_gain, w_up=v_w_up, conv_w=v_conv_w, conv_b=v_conv_b, w_down=v_w_down)
    d = D_MODEL
    n_in, n_up = w_in.shape[2], w_up.shape[2]
    n_ab, n_meta = w_branch_a.shape[2], meta_tokens.shape[1]

    g_in, g_tiny = _all_gather([w_in[0].T.astype(BF16), _tiny_pack(conv_w[0], meta_tokens)], "gather_weights")
    w_in_t = _align_axis0(g_in.reshape(N_DEV * n_in, d))
    cw_slots, meta_slots = _tiny_unpack(g_tiny, n_up, n_meta)
    conv_w_f = _ffn_interleave(jnp.transpose(cw_slots, (1, 0, 2)).reshape(CONV_WIDTH, -1), 1)
    meta_f = jnp.transpose(meta_slots, (1, 0, 2)).reshape(N_META, -1)
    conv_b_i = _ffn_interleave(conv_b, 1)

    lb, lb_vjp = jax.vjp(_lower_bound, hg_lb_logits)
    loss, grad_x, g = _local_step(
        x, loss_target, meta_f, norm1_gain, w_in_t, fox_b_f, q_norm_gain, k_norm_gain, lb, hg_out_gain,
        None, None, None, norm2_gain, None, conv_w_f, conv_b_i, None,
        ffn_shards=(w_up[0].T.astype(BF16), w_down[0].astype(BF16),
                    jnp.stack([w_branch_a[0].T, w_branch_b[0].T]).astype(BF16), w_out[0].astype(BF16)))

    g["hg_lb_logits"] = lb_vjp(g.pop("lb"))[0]
    g["conv_b"] = _ffn_deinterleave(g["conv_b"], 1)
    gcw = _ffn_deinterleave(g["conv_w"], 1).reshape(CONV_WIDTH, N_DEV, n_up)
    gmeta = g["meta_tokens"].reshape(N_META, N_DEV, n_meta)
    tiny = jnp.concatenate([
        jnp.pad(jnp.transpose(gcw, (1, 0, 2)), ((0, 0), (0, 8 - CONV_WIDTH), (0, TINY_COLS - n_up))),
        jnp.pad(jnp.transpose(gmeta, (1, 0, 2)), ((0, 0), (0, 0), (0, TINY_COLS - n_meta)))], axis=1)
    small_shapes = [w[n].shape for n in SMALL]
    rows_sm = _rows_for(small_shapes, extra=1)
    small = _pack([g[n].reshape(-1) for n in SMALL] + [loss.reshape(1)], rows_sm)
    r_tiny, r_small = _exchange([tiny], [small], "exchange_grads")
    r_up, r_down, r_out, r_ab, r_in = g["early"]

    res = {}
    g_in_s = _slot_sum(r_in, "sum_w_in").T
    res["w_in"] = (g_in_s,) + tuple(_adamw(g_in_s, w_in[0], m_w_in[0], v_w_in[0], "adamw_w_in"))
    g_up_s = _slot_sum(r_up, "sum_w_up").T
    res["w_up"] = (g_up_s,) + tuple(_adamw(g_up_s, w_up[0], m_w_up[0], v_w_up[0], "adamw_w_up"))
    g_ab_s = jnp.swapaxes(_slot_sum(r_ab.reshape(N_DEV, 2 * n_ab, -1), "sum_w_ab").reshape(2, n_ab, -1), 1, 2)
    ab = lambda t: jnp.concatenate([t["w_branch_a"][0], t["w_branch_b"][0]], axis=0)
    o_ab = (g_ab_s.reshape(-1, n_ab),) + tuple(_adamw(g_ab_s.reshape(-1, n_ab), ab(w), ab(mom), ab(var), "adamw_w_ab"))
    half = o_ab[0].shape[0] // 2
    res["w_branch_a"] = tuple(o[:half] for o in o_ab)
    res["w_branch_b"] = tuple(o[half:] for o in o_ab)
    res["w_out"] = tuple(_sum_adamw(r_out, w_out[0], m_w_out[0], v_w_out[0], "adamw_w_out"))
    res["w_down"] = tuple(_sum_adamw(r_down, w_down[0], m_w_down[0], v_w_down[0], "adamw_w_down"))
    tp = lambda t: _tiny_pack(t["conv_w"][0], t["meta_tokens"])
    o_tiny = [_tiny_unpack(o, n_up, n_meta) for o in _sum_adamw(r_tiny, tp(w), tp(mom), tp(var), "adamw_tiny")]
    res["conv_w"] = tuple(o[0] for o in o_tiny)
    res["meta_tokens"] = tuple(o[1] for o in o_tiny)
    zero1 = jnp.zeros((1,), F32)
    sp = lambda t: _pack([t[n].reshape(-1) for n in SMALL] + [zero1], rows_sm)
    o_small = [_unpack(o, small_shapes + [(1,)]) for o in _sum_adamw(r_small, sp(w), sp(mom), sp(var), "adamw_small")]
    for i, n in enumerate(SMALL):
        res[n] = tuple(o[i] for o in o_small)
    loss_all = o_small[0][len(SMALL)].reshape(())

    result = [[res[n][k].reshape(w[n].shape) for n in ORDER] for k in range(4)]
    return (loss_all, grad_x, *result[0], *result[1], *result[2], *result[3])
```
